```python
import jax, jax.numpy as jnp
from jax import lax
import numpy as np

D_MODEL = 1024
BATCH = 4
SEQ = 8192
DEPTH = 1

HEAD_DIM = 64
N_HEADS_FOX = D_MODEL // (2 * HEAD_DIM)
N_HEADS_SB = D_MODEL // (2 * HEAD_DIM)
D_FOX = N_HEADS_FOX * HEAD_DIM
D_SB = N_HEADS_SB * HEAD_DIM
D_MIX = D_FOX + D_SB
N_IN = 3 * D_FOX + 3 * D_SB + N_HEADS_FOX
BLOCK_Q = 128
D_FF = int(round(8 * D_MODEL / 3 / 64)) * 64
CONV_WIDTH = 3
N_MOD = 6
EPS = 1e-6

kernel_name = 'hybrid_fox_stickbreaking_convffn_adaln'


def rms_norm(x, g):
    xf = x.astype(jnp.float32)
    y = xf * lax.rsqrt(jnp.mean(xf * xf, axis=-1, keepdims=True) + EPS)
    return (y * g.astype(jnp.float32)).astype(x.dtype)


def split_heads(t, n_heads):
    B, S, _ = t.shape
    return t.reshape(B, S, n_heads, HEAD_DIM).transpose(0, 2, 1, 3)


def head_rms_norm(o, g):
    B, H, S, Dh = o.shape
    o = o.transpose(0, 2, 1, 3)
    return rms_norm(o, g.reshape(H, Dh)).reshape(B, S, H * Dh)


def to_blocks(t):
    B, H, S = t.shape[:3]
    nb = S // BLOCK_Q
    t = t.reshape((B, H, nb, BLOCK_Q) + t.shape[3:])
    return jnp.moveaxis(t, 2, 0)


def from_blocks(t):
    nb, B, H, bq, Dh = t.shape
    return jnp.moveaxis(t, 0, 2).reshape(B, H, nb * bq, Dh)


def forgetting_attention(q, k, v, log_f):
    S = q.shape[2]
    scale = HEAD_DIM ** -0.5
    F = jnp.cumsum(log_f.astype(jnp.float32), axis=-1)
    kpos = jnp.arange(S)
    nb = S // BLOCK_Q

    def one_block(args):
        i, q_blk, F_blk = args
        qpos = i * BLOCK_Q + jnp.arange(BLOCK_Q)
        s = jnp.einsum('bhqd,bhkd->bhqk', q_blk, k, preferred_element_type=jnp.float32) * scale
        s = s + F_blk[..., None] - F[:, :, None, :]
        causal = kpos[None, :] <= qpos[:, None]
        s = jnp.where(causal, s, -jnp.inf)
        p = jax.nn.softmax(s, axis=-1)
        return jnp.einsum('bhqk,bhkd->bhqd', p.astype(v.dtype), v)

    out = lax.map(one_block, (jnp.arange(nb), to_blocks(q), to_blocks(F)))
    return from_blocks(out)


def stick_breaking_attention(q, k, v):
    S = q.shape[2]
    scale = HEAD_DIM ** -0.5
    kpos = jnp.arange(S)
    nb = S // BLOCK_Q

    def one_block(args):
        i, q_blk = args
        qpos = i * BLOCK_Q + jnp.arange(BLOCK_Q)
        z = jnp.einsum('bhqd,bhkd->bhqk', q_blk, k, preferred_element_type=jnp.float32) * scale
        strict = kpos[None, :] < qpos[:, None]
        log_one_minus_beta = jnp.where(strict, jax.nn.log_sigmoid(-z), 0.0)
        rest = lax.cumsum(log_one_minus_beta, axis=3, reverse=True) - log_one_minus_beta
        log_a = jax.nn.log_sigmoid(z) + rest
        a = jnp.where(strict, jnp.exp(log_a), 0.0)
        return jnp.einsum('bhqk,bhkd->bhqd', a.astype(v.dtype), v)

    out = lax.map(one_block, (jnp.arange(nb), to_blocks(q)))
    return from_blocks(out)


def causal_depthwise_conv(u, w, b):
    C = u.shape[-1]
    y = lax.conv_general_dilated(
        u, w.astype(u.dtype).reshape(CONV_WIDTH, 1, C),
        window_strides=(1,), padding=[(CONV_WIDTH - 1, 0)],
        dimension_numbers=('NWC', 'WIO', 'NWC'), feature_group_count=C)
    return y + b.astype(u.dtype)


def setup_inputs(seed: int = 0) -> dict:
    key = jax.random.key(seed)
    ks = jax.random.split(key, 16)
    L, D = DEPTH, D_MODEL
    f32 = jnp.float32

    def nrm(k, shape, s):
        return jax.random.normal(k, shape, f32) * s

    return {
        'x': nrm(ks[0], (BATCH, SEQ, D), 1.0),
        'c': nrm(ks[1], (BATCH, D), 1.0),
        'w_ada': nrm(ks[2], (L, D, N_MOD * D), D ** -0.5),
        'b_ada': nrm(ks[3], (L, N_MOD * D), 0.02),
        'g_attn': 1.0 + nrm(ks[4], (L, D), 0.02),
        'w_in': nrm(ks[5], (L, D, N_IN), D ** -0.5),
        'b_fgate': 2.0 + nrm(ks[6], (L, N_HEADS_FOX), 0.5),
        'g_out_fox': 1.0 + nrm(ks[7], (L, D_FOX), 0.02),
        'g_out_sb': 1.0 + nrm(ks[8], (L, D_SB), 0.02),
        'w_out': nrm(ks[9], (L, D_MIX, D), D_MIX ** -0.5),
        'g_mlp': 1.0 + nrm(ks[10], (L, D), 0.02),
        'w_up': nrm(ks[11], (L, D, 2 * D_FF), D ** -0.5),
        'conv_w': nrm(ks[12], (L, CONV_WIDTH, 2 * D_FF), CONV_WIDTH ** -0.5),
        'conv_b': nrm(ks[13], (L, 2 * D_FF), 0.02),
        'w_down': nrm(ks[14], (L, D_FF, D), D_FF ** -0.5),
        'g_final': 1.0 + nrm(ks[15], (D,), 0.02),
    }


def reference(x, c, w_ada, b_ada, g_attn, w_in, b_fgate, g_out_fox, g_out_sb, w_out,
              g_mlp, w_up, conv_w, conv_b, w_down, g_final):
    sizes = [D_FOX, D_FOX, D_FOX, D_SB, D_SB, D_SB]
    offsets = np.cumsum(sizes).tolist()
    for l in range(DEPTH):
        mod = jax.nn.silu(c) @ w_ada[l] + b_ada[l]
        shift_a, scale_a, gate_a, shift_m, scale_m, gate_m = [
            m[:, None, :] for m in jnp.split(mod, N_MOD, axis=-1)]

        h = rms_norm(x, g_attn[l]) * (1.0 + scale_a) + shift_a
        proj = h @ w_in[l]
        q_f, k_f, v_f, q_s, k_s, v_s, f_logit = jnp.split(proj, offsets, axis=-1)
        log_f = jax.nn.log_sigmoid((f_logit + b_fgate[l]).astype(jnp.float32))
        o_fox = forgetting_attention(split_heads(q_f, N_HEADS_FOX), split_heads(k_f, N_HEADS_FOX),
                                     split_heads(v_f, N_HEADS_FOX), log_f.transpose(0, 2, 1))
        o_sb = stick_breaking_attention(split_heads(q_s, N_HEADS_SB), split_heads(k_s, N_HEADS_SB),
                                        split_heads(v_s, N_HEADS_SB))
        mix = jnp.concatenate([head_rms_norm(o_fox, g_out_fox[l]),
                               head_rms_norm(o_sb, g_out_sb[l])], axis=-1)
        x = x + gate_a * (mix @ w_out[l])

        h = rms_norm(x, g_mlp[l]) * (1.0 + scale_m) + shift_m
        u = causal_depthwise_conv(h @ w_up[l], conv_w[l], conv_b[l])
        u_gate, u_val = jnp.split(u, 2, axis=-1)
        x = x + gate_m * ((jax.nn.silu(u_gate) * u_val) @ w_down[l])
    return rms_norm(x, g_final)
```

```python
import functools

import numpy as np
import jax
import jax.numpy as jnp
from jax import lax
from jax.experimental import pallas as pl
from jax.experimental.pallas import tpu as pltpu

HEAD_DIM = 64
N_MOD = 6
CONV_WIDTH = 3
EPS = 1e-6

LANES = 128
BF16_SUBLANES = 16
VMEM_LIMIT_BYTES = 48 * 1024 * 1024

ATT_BLOCK = 256
PROJ_ROWS = 256
OUT_ROWS = 512
FF_CHUNK = 256

F32 = jnp.float32
BF16 = jnp.bfloat16
NT_DIMS = (((1,), (1,)), ((), ()))


def _dot(a, b):
    return jnp.dot(a, b, preferred_element_type=F32)


def _dot_nt(a, b):
    return lax.dot_general(a, b, NT_DIMS, preferred_element_type=F32)


def _params(*sem):
    return pltpu.CompilerParams(dimension_semantics=sem, vmem_limit_bytes=VMEM_LIMIT_BYTES)


def _rms_rows(x):
    return x * lax.rsqrt(jnp.mean(x * x, axis=-1, keepdims=True) + EPS)


def _softplus(z):
    return jnp.maximum(z, 0.0) + jnp.log(1.0 + jnp.exp(-jnp.abs(z)))


def _split3(x):
    hi = x.astype(BF16)
    r1 = x - hi.astype(F32)
    mid = r1.astype(BF16)
    lo = (r1 - mid.astype(F32)).astype(BF16)
    return hi, mid, lo


def _ada_kernel(c_ref, w_ref, b_ref, o_ref):
    c = c_ref[...]
    o_ref[...] = _dot(c * jax.nn.sigmoid(c), w_ref[...]) + b_ref[...]


def _ada(c, w, b):
    bsz, d = c.shape
    n = w.shape[1]
    return pl.pallas_call(
        _ada_kernel,
        grid=(n // d,),
        in_specs=[pl.BlockSpec((bsz, d), lambda j: (0, 0)),
                  pl.BlockSpec((d, d), lambda j: (0, j)),
                  pl.BlockSpec((1, d), lambda j: (0, j))],
        out_specs=pl.BlockSpec((bsz, d), lambda j: (0, j)),
        out_shape=jax.ShapeDtypeStruct((bsz, n), F32),
        compiler_params=_params("arbitrary"),
        name="ada",
    )(c, w, b.reshape(1, n))


def _inproj_kernel(x_ref, mod_ref, g_ref, wn_ref, wvt_ref, wg_ref, bg_ref, qk_ref, vt_ref, lf_ref):
    shift = mod_ref[0, 0:1, :]
    scale = mod_ref[0, 1:2, :]
    h = (_rms_rows(x_ref[0]) * g_ref[...] * (1.0 + scale) + shift).astype(BF16)
    qk_ref[0] = _dot(h, wn_ref[...]).astype(BF16)
    vt_ref[0, 0] = _dot_nt(wvt_ref[...], h).astype(BF16)
    logit = _dot(h, wg_ref[...]) + bg_ref[...]
    lf_ref[0] = -_softplus(-logit)


def _inproj(x, mod, g, w_nat, w_vt, w_gate, b_gate):
    bsz, s, d = x.shape
    tm = PROJ_ROWS
    n_nat, n_v = w_nat.shape[1], w_vt.shape[0]
    const = lambda b, i: (0, 0)
    return pl.pallas_call(
        _inproj_kernel,
        grid=(bsz, s // tm),
        in_specs=[pl.BlockSpec((1, tm, d), lambda b, i: (b, i, 0)),
                  pl.BlockSpec((1, N_MOD, d), lambda b, i: (b, 0, 0)),
                  pl.BlockSpec((1, d), const),
                  pl.BlockSpec((d, n_nat), const),
                  pl.BlockSpec((n_v, d), const),
                  pl.BlockSpec((d, LANES), const),
                  pl.BlockSpec((1, LANES), const)],
        out_specs=[pl.BlockSpec((1, tm, n_nat), lambda b, i: (b, i, 0)),
                   pl.BlockSpec((1, 1, n_v, tm), lambda b, i: (b, i, 0, 0)),
                   pl.BlockSpec((1, tm, LANES), lambda b, i: (b, i, 0))],
        out_shape=[jax.ShapeDtypeStruct((bsz, s, n_nat), BF16),
                   jax.ShapeDtypeStruct((bsz, s // tm, n_v, tm), BF16),
                   jax.ShapeDtypeStruct((bsz, s, LANES), F32)],
        compiler_params=_params("arbitrary", "arbitrary"),
        name="inproj",
    )(x, mod, g, w_nat, w_vt, w_gate, b_gate)


def _decay_kernel(lf_ref, k_ref, sel_ref, kaug_ref, carry_ref, *, n_heads):
    @pl.when(pl.program_id(1) == 0)
    def _():
        carry_ref[...] = jnp.zeros_like(carry_ref)

    tm = lf_ref.shape[1]
    lane = lax.broadcasted_iota(jnp.int32, (tm, LANES), 1)
    lf = jnp.where(lane < n_heads, lf_ref[0], 0.0)
    row = lax.broadcasted_iota(jnp.int32, (tm, tm), 0)
    col = lax.broadcasted_iota(jnp.int32, (tm, tm), 1)
    tri = (col <= row).astype(BF16)
    hi, mid, lo = _split3(lf)
    f_run = carry_ref[...] + (_dot(tri, hi) + _dot(tri, mid) + _dot(tri, lo))
    carry_ref[...] = f_run[tm - 1:tm, :]
    ghi, gmid, glo = _split3(-f_run)
    packed = (ghi.astype(F32) + pltpu.roll(gmid.astype(F32), n_heads, 1)
              + pltpu.roll(glo.astype(F32), 2 * n_heads, 1)).astype(BF16)
    placed = _dot(packed, sel_ref[...])
    k_all = k_ref[0]
    for h in range(n_heads):
        k_pair = k_all[:, (h // 2) * LANES:(h // 2 + 1) * LANES]
        own = (lane < HEAD_DIM) if h % 2 == 0 else (lane >= HEAD_DIM)
        kaug_ref[0, h] = jnp.where(own, k_pair, placed[:, h * LANES:(h + 1) * LANES].astype(BF16))


def _decay_select_matrix(n_heads):
    sel = np.zeros((LANES, n_heads * LANES), np.float32)
    for h in range(n_heads):
        base = h * LANES + (HEAD_DIM if h % 2 == 0 else 0)
        for term in range(3):
            sel[term * n_heads + h, base + term] = 1.0
    return jnp.asarray(sel, BF16)


def _decay(log_f, qk, n_heads, k_block):
    bsz, s, _ = log_f.shape
    tm = PROJ_ROWS
    d_grp = n_heads * HEAD_DIM
    return pl.pallas_call(
        functools.partial(_decay_kernel, n_heads=n_heads),
        grid=(bsz, s // tm),
        in_specs=[pl.BlockSpec((1, tm, LANES), lambda b, i: (b, i, 0)),
                  pl.BlockSpec((1, tm, d_grp), lambda b, i: (b, i, k_block)),
                  pl.BlockSpec((LANES, n_heads * LANES), lambda b, i: (0, 0))],
        out_specs=pl.BlockSpec((1, n_heads, tm, LANES), lambda b, i: (b, 0, i, 0)),
        out_shape=jax.ShapeDtypeStruct((bsz, n_heads, s, LANES), BF16),
        scratch_shapes=[pltpu.VMEM((1, LANES), F32)],
        compiler_params=_params("arbitrary", "arbitrary"),
        name="decay",
    )(log_f, qk, _decay_select_matrix(n_heads))


def _head_queries(q_ref, extra_even, extra_odd):
    q = q_ref[0].astype(F32) * (HEAD_DIM ** -0.5)
    lane = lax.broadcasted_iota(jnp.int32, q.shape, 1)
    q_even = jnp.where(lane < HEAD_DIM, q, extra_even(lane)).astype(BF16)
    q_odd = jnp.where(lane >= HEAD_DIM, q, extra_odd(lane)).astype(BF16)
    return q_even, q_odd


def _finish_heads(outs, g_ref, o_ref):
    normed = [o * lax.rsqrt(jnp.mean(o * o, axis=0, keepdims=True) + EPS) for o in outs]
    o_ref[0] = (jnp.concatenate(normed, axis=0).T * g_ref[...]).astype(o_ref.dtype)


def _fox_kernel(q_ref, k_ref, vt_ref, g_ref, o_ref):
    qi = pl.program_id(2)
    bq = q_ref.shape[1]
    ones3 = lambda lo: (lambda lane: jnp.where((lane >= lo) & (lane < lo + 3), 1.0, 0.0))
    queries = _head_queries(q_ref, ones3(HEAD_DIM), ones3(0))
    key_idx = lax.broadcasted_iota(jnp.int32, (bq, bq), 0)
    qry_idx = lax.broadcasted_iota(jnp.int32, (bq, bq), 1)

    def tile(j, carry, diagonal):
        start = pl.multiple_of(j * bq, bq)
        new = []
        for hh in range(2):
            m, l, acc = carry[hh]
            s = _dot_nt(k_ref[0, hh, pl.ds(start, bq), :], queries[hh])
            if diagonal:
                s = jnp.where(key_idx <= qry_idx, s, -jnp.inf)
            m_new = jnp.maximum(m, jnp.max(s, axis=0, keepdims=True))
            p = jnp.exp(s - m_new)
            alpha = jnp.exp(m - m_new)
            l = alpha * l + jnp.sum(p, axis=0, keepdims=True)
            vt = vt_ref[0, j, hh * HEAD_DIM:(hh + 1) * HEAD_DIM, :]
            acc = alpha * acc + _dot(vt, p.astype(BF16))
            new.append((m_new, l, acc))
        return tuple(new)

    init = tuple((jnp.full((1, bq), -jnp.inf, F32), jnp.zeros((1, bq), F32),
                  jnp.zeros((HEAD_DIM, bq), F32)) for _ in range(2))
    carry = lax.fori_loop(0, qi, lambda j, c: tile(j, c, False), init)
    carry = tile(qi, carry, True)
    _finish_heads([acc / l for (_, l, acc) in carry], g_ref, o_ref)


def _sb_kernel(q_ref, k_ref, vt_ref, g_ref, o_ref):
    qi = pl.program_id(2)
    bq = q_ref.shape[1]
    zero = lambda lane: 0.0
    queries = _head_queries(q_ref, zero, zero)
    key_idx = lax.broadcasted_iota(jnp.int32, (bq, bq), 0)
    qry_idx = lax.broadcasted_iota(jnp.int32, (bq, bq), 1)
    strict = key_idx < qry_idx
    suffix = (qry_idx >= key_idx).astype(BF16)

    def tile(j, carry, diagonal):
        start = pl.multiple_of(j * bq, bq)
        k = k_ref[0, pl.ds(start, bq), :]
        new = []
        for hh in range(2):
            later, acc = carry[hh]
            z = _dot_nt(k, queries[hh])
            sp = _softplus(z)
            if diagonal:
                sp = jnp.where(strict, sp, 0.0)
            within = _dot(suffix, sp.astype(BF16))
            a = jnp.exp(z - within - later)
            if diagonal:
                a = jnp.where(strict, a, 0.0)
            vt = vt_ref[0, j, hh * HEAD_DIM:(hh + 1) * HEAD_DIM, :]
            new.append((later + within[0:1, :], acc + _dot(vt, a.astype(BF16))))
        return tuple(new)

    init = tuple((jnp.zeros((1, bq), F32), jnp.zeros((HEAD_DIM, bq), F32)) for _ in range(2))
    carry = tile(qi, init, True)
    carry = lax.fori_loop(0, qi, lambda t, c: tile(qi - 1 - t, c, False), carry)
    _finish_heads([acc for (_, acc) in carry], g_ref, o_ref)


def _attention(body, name, qk, k_arr, k_spec, vt, g, q_block0, vt_block0, n_heads):
    bsz, s, _ = qk.shape
    bq = ATT_BLOCK
    d_grp = n_heads * HEAD_DIM
    return pl.pallas_call(
        body,
        grid=(bsz, n_heads // 2, s // bq),
        in_specs=[pl.BlockSpec((1, bq, LANES), lambda b, p, i: (b, i, q_block0 + p)),
                  k_spec,
                  pl.BlockSpec((1, s // bq, LANES, bq), lambda b, p, i: (b, 0, vt_block0 + p, 0)),
                  pl.BlockSpec((1, LANES), lambda b, p, i: (0, p))],
        out_specs=pl.BlockSpec((1, bq, LANES), lambda b, p, i: (b, i, p)),
        out_shape=jax.ShapeDtypeStruct((bsz, s, d_grp), BF16),
        compiler_params=_params("arbitrary", "arbitrary", "arbitrary"),
        name=name,
    )(qk, k_arr, vt, g.reshape(1, d_grp))


def _outproj_kernel(x_ref, mf_ref, ms_ref, w_ref, mod_ref, g_ref, x1_ref, h2_ref):
    mix = jnp.concatenate([mf_ref[0], ms_ref[0]], axis=-1)
    x1 = x_ref[0] + mod_ref[0, 2:3, :] * _dot(mix, w_ref[...])
    x1_ref[0] = x1
    shift = mod_ref[0, 3:4, :]
    scale = mod_ref[0, 4:5, :]
    h2_ref[0] = (_rms_rows(x1) * g_ref[...] * (1.0 + scale) + shift).astype(BF16)


def _outproj(x, mix_f, mix_s, w_out, mod, g):
    bsz, s, d = x.shape
    tm = OUT_ROWS
    row = lambda b, i: (b, i, 0)
    return pl.pallas_call(
        _outproj_kernel,
        grid=(bsz, s // tm),
        in_specs=[pl.BlockSpec((1, tm, d), row),
                  pl.BlockSpec((1, tm, mix_f.shape[2]), row),
                  pl.BlockSpec((1, tm, mix_s.shape[2]), row),
                  pl.BlockSpec(w_out.shape, lambda b, i: (0, 0)),
                  pl.BlockSpec((1, N_MOD, d), lambda b, i: (b, 0, 0)),
                  pl.BlockSpec((1, d), lambda b, i: (0, 0))],
        out_specs=[pl.BlockSpec((1, tm, d), row), pl.BlockSpec((1, tm, d), row)],
        out_shape=[jax.ShapeDtypeStruct((bsz, s, d), F32), jax.ShapeDtypeStruct((bsz, s, d), BF16)],
        compiler_params=_params("arbitrary", "arbitrary"),
        name="outproj",
    )(x, mix_f, mix_s, w_out, mod, g)


def _mlp_kernel(h_ref, halo_ref, x1_ref, mod_ref, wg_ref, wv_ref, cwg_ref, cbg_ref, cwv_ref, cbv_ref,
                wd_ref, gf_ref, o_ref, acc_ref, *, final_norm):
    i = pl.program_id(1)
    c = pl.program_id(2)
    tm = h_ref.shape[1]
    halo = halo_ref[0]
    halo = jnp.where(i > 0, halo, jnp.zeros_like(halo))
    hx = jnp.concatenate([halo, h_ref[0]], axis=0)

    def conv_branch(w_ref, cw_ref, cb_ref):
        u = _dot(hx, w_ref[...])
        out = cb_ref[...]
        for tap in range(CONV_WIDTH):
            lag = CONV_WIDTH - 1 - tap
            first = BF16_SUBLANES - lag
            out = out + cw_ref[tap:tap + 1, :] * u[first:first + tm, :]
        return out

    u_gate = conv_branch(wg_ref, cwg_ref, cbg_ref)
    u_val = conv_branch(wv_ref, cwv_ref, cbv_ref)
    part = _dot((u_gate * jax.nn.sigmoid(u_gate) * u_val).astype(BF16), wd_ref[...])

    @pl.when(c == 0)
    def _():
        acc_ref[...] = part

    @pl.when(c > 0)
    def _():
        acc_ref[...] += part

    @pl.when(c == pl.num_programs(2) - 1)
    def _():
        x2 = x1_ref[0] + mod_ref[0, 5:6, :] * acc_ref[...]
        o_ref[0] = _rms_rows(x2) * gf_ref[...] if final_norm else x2


def _mlp(h2, x1, mod, w_gate, w_val, cw_gate, cb_gate, cw_val, cb_val, w_down, g_final, final_norm):
    bsz, s, d = x1.shape
    tm, tf = OUT_ROWS, FF_CHUNK
    halo_blocks = tm // BF16_SUBLANES
    row = lambda b, i, c: (b, i, 0)
    col = lambda b, i, c: (0, c)
    return pl.pallas_call(
        functools.partial(_mlp_kernel, final_norm=final_norm),
        grid=(bsz, s // tm, w_gate.shape[1] // tf),
        in_specs=[pl.BlockSpec((1, tm, d), row),
                  pl.BlockSpec((1, BF16_SUBLANES, d),
                               lambda b, i, c: (b, jnp.maximum(i * halo_blocks - 1, 0), 0)),
                  pl.BlockSpec((1, tm, d), row),
                  pl.BlockSpec((1, N_MOD, d), lambda b, i, c: (b, 0, 0)),
                  pl.BlockSpec((d, tf), col), pl.BlockSpec((d, tf), col),
                  pl.BlockSpec((CONV_WIDTH, tf), col), pl.BlockSpec((1, tf), col),
                  pl.BlockSpec((CONV_WIDTH, tf), col), pl.BlockSpec((1, tf), col),
                  pl.BlockSpec((tf, d), lambda b, i, c: (c, 0)),
                  pl.BlockSpec((1, d), lambda b, i, c: (0, 0))],
        out_specs=pl.BlockSpec((1, tm, d), row),
        out_shape=jax.ShapeDtypeStruct((bsz, s, d), F32),
        scratch_shapes=[pltpu.VMEM((tm, d), F32)],
        compiler_params=_params("arbitrary", "arbitrary", "arbitrary"),
        name="mlp",
    )(h2, h2, x1, mod, w_gate, w_val, cw_gate, cb_gate, cw_val, cb_val, w_down, g_final)


def _pad_cols(a, n):
    return jnp.pad(a, ((0, 0), (0, n - a.shape[1])))


def kernel(x, c, w_ada, b_ada, g_attn, w_in, b_fgate, g_out_fox, g_out_sb, w_out,
           g_mlp, w_up, conv_w, conv_b, w_down, g_final):
    depth, d, _ = w_ada.shape
    n_fox = b_fgate.shape[1]
    d_fox = n_fox * HEAD_DIM
    d_sb = g_out_sb.shape[1]
    n_sb = d_sb // HEAD_DIM
    d_ff = w_down.shape[1]
    d_ff_pad = -(-d_ff // FF_CHUNK) * FF_CHUNK
    assert n_fox % 2 == 0 and n_sb % 2 == 0 and 3 * n_fox <= LANES
    assert x.shape[1] % OUT_ROWS == 0 and x.shape[1] % ATT_BLOCK == 0 and PROJ_ROWS == ATT_BLOCK
    o_kf, o_vf, o_qs, o_ks, o_vs, o_gate = (d_fox, 2 * d_fox, 3 * d_fox, 3 * d_fox + d_sb,
                                             3 * d_fox + 2 * d_sb, 3 * d_fox + 3 * d_sb)

    for l in range(depth):
        mod = _ada(c, w_ada[l], b_ada[l]).reshape(-1, N_MOD, d)
        w = w_in[l]
        w_nat = jnp.concatenate([w[:, :o_vf], w[:, o_qs:o_vs]], axis=1).astype(BF16)
        w_vt = jnp.concatenate([w[:, o_vf:o_qs], w[:, o_vs:o_gate]], axis=1).T.astype(BF16)
        w_gate = _pad_cols(w[:, o_gate:], LANES).astype(BF16)
        b_gate = _pad_cols(b_fgate[l].reshape(1, n_fox), LANES)
        qk, vt, log_f = _inproj(x, mod, g_attn[l].reshape(1, d), w_nat, w_vt, w_gate, b_gate)

        pairs_f, pairs_s = n_fox // 2, n_sb // 2
        k_aug = _decay(log_f, qk, n_fox, k_block=1)
        fox_k_spec = pl.BlockSpec((1, 2, x.shape[1], LANES), lambda b, p, i: (b, p, 0, 0))
        mix_f = _attention(_fox_kernel, "fox", qk, k_aug, fox_k_spec, vt, g_out_fox[l],
                           q_block0=0, vt_block0=0, n_heads=n_fox)
        sb_k_spec = pl.BlockSpec((1, x.shape[1], LANES), lambda b, p, i: (b, 0, 2 * pairs_f + pairs_s + p))
        mix_s = _attention(_sb_kernel, "sb", qk, qk, sb_k_spec, vt, g_out_sb[l],
                           q_block0=2 * pairs_f, vt_block0=pairs_f, n_heads=n_sb)

        x1, h2 = _outproj(x, mix_f, mix_s, w_out[l].astype(BF16), mod, g_mlp[l].reshape(1, d))

        wu, cw, cb = w_up[l], conv_w[l], conv_b[l].reshape(1, -1)
        x = _mlp(h2, x1, mod,
                 _pad_cols(wu[:, :d_ff], d_ff_pad).astype(BF16), _pad_cols(wu[:, d_ff:], d_ff_pad).astype(BF16),
                 _pad_cols(cw[:, :d_ff], d_ff_pad), _pad_cols(cb[:, :d_ff], d_ff_pad),
                 _pad_cols(cw[:, d_ff:], d_ff_pad), _pad_cols(cb[:, d_ff:], d_ff_pad),
                 jnp.pad(w_down[l], ((0, d_ff_pad - d_ff), (0, 0))).astype(BF16),
                 g_final.reshape(1, d), final_norm=(l == depth - 1))
    return x
```

```python
import functools

import numpy as np
import jax
import jax.numpy as jnp
from jax import lax
from jax.experimental import pallas as pl
from jax.experimental.pallas import tpu as pltpu

HEAD_DIM = 64
N_MOD = 6
CONV_WIDTH = 3
EPS = 1e-6

LANES = 128
BF16_SUBLANES = 16
VMEM_LIMIT_BYTES = 48 * 1024 * 1024

ATT_Q = 512
ATT_K = 256
PROJ_ROWS = ATT_K
LOG2E = 1.4426950408889634
MASKED = -1e30
M_INIT = -1e29
OUT_ROWS = 512
FF_CHUNK = 256

F32 = jnp.float32
BF16 = jnp.bfloat16
NT_DIMS = (((1,), (1,)), ((), ()))


def _dot(a, b):
    return jnp.dot(a, b, preferred_element_type=F32)


def _dot_nt(a, b):
    return lax.dot_general(a, b, NT_DIMS, preferred_element_type=F32)


def _params(*sem):
    return pltpu.CompilerParams(dimension_semantics=sem, vmem_limit_bytes=VMEM_LIMIT_BYTES)


def _rms_rows(x):
    return x * lax.rsqrt(jnp.mean(x * x, axis=-1, keepdims=True) + EPS)


def _softplus(z):
    return jnp.maximum(z, 0.0) + jnp.log(1.0 + jnp.exp(-jnp.abs(z)))


def _split3(x):
    hi = x.astype(BF16)
    r1 = x - hi.astype(F32)
    mid = r1.astype(BF16)
    lo = (r1 - mid.astype(F32)).astype(BF16)
    return hi, mid, lo


def _ada_kernel(c_ref, w_ref, b_ref, o_ref):
    c = c_ref[...]
    o_ref[...] = _dot(c * jax.nn.sigmoid(c), w_ref[...]) + b_ref[...]


def _ada(c, w, b):
    bsz, d = c.shape
    n = w.shape[1]
    return pl.pallas_call(
        _ada_kernel,
        grid=(n // d,),
        in_specs=[pl.BlockSpec((bsz, d), lambda j: (0, 0)),
                  pl.BlockSpec((d, d), lambda j: (0, j)),
                  pl.BlockSpec((1, d), lambda j: (0, j))],
        out_specs=pl.BlockSpec((bsz, d), lambda j: (0, j)),
        out_shape=jax.ShapeDtypeStruct((bsz, n), F32),
        compiler_params=_params("arbitrary"),
        name="ada",
    )(c, w, b.reshape(1, n))


def _inproj_kernel(x_ref, mod_ref, g_ref, wn_ref, wvt_ref, wg_ref, bg_ref, qk_ref, vt_ref, lf_ref):
    shift = mod_ref[0, 0:1, :]
    scale = mod_ref[0, 1:2, :]
    h = (_rms_rows(x_ref[0]) * g_ref[...] * (1.0 + scale) + shift).astype(BF16)
    qk_ref[0] = _dot(h, wn_ref[...]).astype(BF16)
    vt_ref[0, 0] = _dot_nt(wvt_ref[...], h).astype(BF16)
    logit = _dot(h, wg_ref[...]) + bg_ref[...]
    lf_ref[0] = -_softplus(-logit)


def _inproj(x, mod, g, w_nat, w_vt, w_gate, b_gate):
    bsz, s, d = x.shape
    tm = PROJ_ROWS
    n_nat, n_v = w_nat.shape[1], w_vt.shape[0]
    const = lambda b, i: (0, 0)
    return pl.pallas_call(
        _inproj_kernel,
        grid=(bsz, s // tm),
        in_specs=[pl.BlockSpec((1, tm, d), lambda b, i: (b, i, 0)),
                  pl.BlockSpec((1, N_MOD, d), lambda b, i: (b, 0, 0)),
                  pl.BlockSpec((1, d), const),
                  pl.BlockSpec((d, n_nat), const),
                  pl.BlockSpec((n_v, d), const),
                  pl.BlockSpec((d, LANES), const),
                  pl.BlockSpec((1, LANES), const)],
        out_specs=[pl.BlockSpec((1, tm, n_nat), lambda b, i: (b, i, 0)),
                   pl.BlockSpec((1, 1, n_v, tm), lambda b, i: (b, i, 0, 0)),
                   pl.BlockSpec((1, tm, LANES), lambda b, i: (b, i, 0))],
        out_shape=[jax.ShapeDtypeStruct((bsz, s, n_nat), BF16),
                   jax.ShapeDtypeStruct((bsz, s // tm, n_v, tm), BF16),
                   jax.ShapeDtypeStruct((bsz, s, LANES), F32)],
        compiler_params=_params("arbitrary", "arbitrary"),
        name="inproj",
    )(x, mod, g, w_nat, w_vt, w_gate, b_gate)


def _decay_kernel(lf_ref, k_ref, sel_ref, kaug_ref, carry_ref, *, n_heads):
    @pl.when(pl.program_id(1) == 0)
    def _():
        carry_ref[...] = jnp.zeros_like(carry_ref)

    tm = lf_ref.shape[1]
    lane = lax.broadcasted_iota(jnp.int32, (tm, LANES), 1)
    lf = jnp.where(lane < n_heads, lf_ref[0], 0.0)
    row = lax.broadcasted_iota(jnp.int32, (tm, tm), 0)
    col = lax.broadcasted_iota(jnp.int32, (tm, tm), 1)
    tri = (col <= row).astype(BF16)
    hi, mid, lo = _split3(lf)
    f_run = carry_ref[...] + (_dot(tri, hi) + _dot(tri, mid) + _dot(tri, lo))
    carry_ref[...] = f_run[tm - 1:tm, :]
    ghi, gmid, glo = _split3(-LOG2E * f_run)
    packed = (ghi.astype(F32) + pltpu.roll(gmid.astype(F32), n_heads, 1)
              + pltpu.roll(glo.astype(F32), 2 * n_heads, 1)).astype(BF16)
    placed = _dot(packed, sel_ref[...])
    k_all = k_ref[0]
    for h in range(n_heads):
        k_pair = k_all[:, (h // 2) * LANES:(h // 2 + 1) * LANES]
        own = (lane < HEAD_DIM) if h % 2 == 0 else (lane >= HEAD_DIM)
        kaug_ref[0, h] = jnp.where(own, k_pair, placed[:, h * LANES:(h + 1) * LANES].astype(BF16))


def _decay_select_matrix(n_heads):
    sel = np.zeros((LANES, n_heads * LANES), np.float32)
    for h in range(n_heads):
        base = h * LANES + (HEAD_DIM if h % 2 == 0 else 0)
        for term in range(3):
            sel[term * n_heads + h, base + term] = 1.0
    return jnp.asarray(sel, BF16)


def _decay(log_f, qk, n_heads, k_block):
    bsz, s, _ = log_f.shape
    tm = PROJ_ROWS
    d_grp = n_heads * HEAD_DIM
    return pl.pallas_call(
        functools.partial(_decay_kernel, n_heads=n_heads),
        grid=(bsz, s // tm),
        in_specs=[pl.BlockSpec((1, tm, LANES), lambda b, i: (b, i, 0)),
                  pl.BlockSpec((1, tm, d_grp), lambda b, i: (b, i, k_block)),
                  pl.BlockSpec((LANES, n_heads * LANES), lambda b, i: (0, 0))],
        out_specs=pl.BlockSpec((1, n_heads, tm, LANES), lambda b, i: (b, 0, i, 0)),
        out_shape=jax.ShapeDtypeStruct((bsz, n_heads, s, LANES), BF16),
        scratch_shapes=[pltpu.VMEM((1, LANES), F32)],
        compiler_params=_params("arbitrary", "arbitrary"),
        name="decay",
    )(log_f, qk, _decay_select_matrix(n_heads))


def _head_queries(q_ref, extra_even, extra_odd):
    q = q_ref[0].astype(F32) * (HEAD_DIM ** -0.5 * LOG2E)
    lane = lax.broadcasted_iota(jnp.int32, q.shape, 1)
    q_even = jnp.where(lane < HEAD_DIM, q, extra_even(lane)).astype(BF16)
    q_odd = jnp.where(lane >= HEAD_DIM, q, extra_odd(lane)).astype(BF16)
    return q_even, q_odd


def _finish_heads(outs, g_ref, o_ref):
    normed = [o * lax.rsqrt(jnp.mean(o * o, axis=0, keepdims=True) + EPS) for o in outs]
    o_ref[0] = (jnp.concatenate(normed, axis=0).T * g_ref[...]).astype(o_ref.dtype)


def _fox_kernel(q_ref, k_ref, vt_ref, g_ref, o_ref, s_buf, p_buf, acc_buf):
    qi = pl.program_id(2)
    bq, bk = q_ref.shape[1], vt_ref.shape[3]
    n_tiles = (qi + 1) * (bq // bk)
    ones3 = lambda lo: (lambda lane: jnp.where((lane >= lo) & (lane < lo + 3), 1.0, 0.0))
    queries = _head_queries(q_ref, ones3(HEAD_DIM), ones3(0))
    key_minus_query = (lax.broadcasted_iota(jnp.int32, (bk, bq), 0)
                       - lax.broadcasted_iota(jnp.int32, (bk, bq), 1))
    s_buf[...] = jnp.full(s_buf.shape, MASKED, F32)
    p_buf[...] = jnp.zeros(p_buf.shape, BF16)
    acc_buf[...] = jnp.zeros(acc_buf.shape, F32)

    def step(t, carry):
        j_val = jnp.clip(t - 1, 0, n_tiles - 1)
        pv = [_dot(vt_ref[0, j_val, hh * HEAD_DIM:(hh + 1) * HEAD_DIM, :], p_buf[hh]) for hh in range(2)]
        new = []
        for hh in range(2):
            m, l = carry[hh]
            s = s_buf[hh]
            m_new = jnp.maximum(m, jnp.max(s, axis=0, keepdims=True))
            alpha = jnp.exp2(m - m_new)
            p = jnp.exp2(s - m_new)
            p_buf[hh] = p.astype(BF16)
            acc_buf[hh] = alpha * (acc_buf[hh] + pv[hh])
            new.append((m_new, alpha * l + jnp.sum(p, axis=0, keepdims=True)))
        j_new = jnp.minimum(t + 1, n_tiles - 1)
        limit = jnp.where(t + 1 < n_tiles, qi * bq - j_new * bk, -(bk + bq))
        visible = key_minus_query <= limit
        start = pl.multiple_of(j_new * bk, bk)
        for hh in range(2):
            s_buf[hh] = jnp.where(visible, _dot_nt(k_ref[0, hh, pl.ds(start, bk), :], queries[hh]), MASKED)
        return tuple(new)

    init = tuple((jnp.full((1, bq), M_INIT, F32), jnp.zeros((1, bq), F32)) for _ in range(2))
    carry = lax.fori_loop(-1, n_tiles + 1, step, init)
    _finish_heads([acc_buf[hh] / carry[hh][1] for hh in range(2)], g_ref, o_ref)


def _sb_kernel(q_ref, k_ref, vt_ref, g_ref, o_ref, z_ring, sp_buf, e_buf, wrow_buf, acc_buf):
    qi = pl.program_id(2)
    bq, bk = q_ref.shape[1], vt_ref.shape[3]
    n_tiles = (qi + 1) * (bq // bk)
    zero = lambda lane: 0.0
    queries = _head_queries(q_ref, zero, zero)
    key_minus_query = (lax.broadcasted_iota(jnp.int32, (bk, bq), 0)
                       - lax.broadcasted_iota(jnp.int32, (bk, bq), 1))
    suffix = (lax.broadcasted_iota(jnp.int32, (bk, bk), 1)
              >= lax.broadcasted_iota(jnp.int32, (bk, bk), 0)).astype(BF16)
    z_ring[...] = jnp.full(z_ring.shape, MASKED, F32)
    e_buf[...] = jnp.full(e_buf.shape, MASKED, F32)
    sp_buf[...] = jnp.zeros(sp_buf.shape, BF16)
    wrow_buf[...] = jnp.zeros(wrow_buf.shape, F32)
    acc_buf[...] = jnp.zeros(acc_buf.shape, F32)

    def step(t, later):
        j_val = n_tiles - 1 - jnp.clip(t, 0, n_tiles - 1)
        new_later = []
        for hh in range(2):
            a = jnp.exp2(e_buf[hh] - later[hh])
            acc_buf[hh] += _dot(vt_ref[0, j_val, hh * HEAD_DIM:(hh + 1) * HEAD_DIM, :], a.astype(BF16))
            new_later.append(later[hh] + wrow_buf[hh])
        slot = (t + 1) & 1
        for hh in range(2):
            within = _dot(suffix, sp_buf[hh])
            e_buf[hh] = z_ring[slot, hh] - within
            wrow_buf[hh] = within[0:1, :]
        for hh in range(2):
            z = z_ring[1 - slot, hh]
            sp_buf[hh] = (jnp.maximum(z, 0.0) + jnp.log2(1.0 + jnp.exp2(-jnp.abs(z)))).astype(BF16)
        u_new = jnp.minimum(t + 3, n_tiles - 1)
        j_new = n_tiles - 1 - u_new
        limit = jnp.where(t + 3 < n_tiles, qi * bq - j_new * bk, -(bk + bq))
        visible = key_minus_query < limit
        k = k_ref[0, pl.ds(pl.multiple_of(j_new * bk, bk), bk), :]
        for hh in range(2):
            z_ring[slot, hh] = jnp.where(visible, _dot_nt(k, queries[hh]), MASKED)
        return tuple(new_later)

    lax.fori_loop(-3, n_tiles, step, tuple(jnp.zeros((1, bq), F32) for _ in range(2)))
    _finish_heads([acc_buf[hh] for hh in range(2)], g_ref, o_ref)


def _attention(body, name, scratch, qk, k_arr, k_spec, vt, g, q_block0, vt_block0, n_heads):
    bsz, s, _ = qk.shape
    bq, bk = ATT_Q, ATT_K
    d_grp = n_heads * HEAD_DIM
    return pl.pallas_call(
        body,
        grid=(bsz, n_heads // 2, s // bq),
        in_specs=[pl.BlockSpec((1, bq, LANES), lambda b, p, i: (b, i, q_block0 + p)),
                  k_spec,
                  pl.BlockSpec((1, s // bk, LANES, bk), lambda b, p, i: (b, 0, vt_block0 + p, 0)),
                  pl.BlockSpec((1, LANES), lambda b, p, i: (0, p))],
        out_specs=pl.BlockSpec((1, bq, LANES), lambda b, p, i: (b, i, p)),
        out_shape=jax.ShapeDtypeStruct((bsz, s, d_grp), BF16),
        scratch_shapes=scratch,
        compiler_params=_params("arbitrary", "arbitrary", "arbitrary"),
        name=name,
    )(qk, k_arr, vt, g.reshape(1, d_grp))


def _fox_scratch():
    return [pltpu.VMEM((2, ATT_K, ATT_Q), F32), pltpu.VMEM((2, ATT_K, ATT_Q), BF16),
            pltpu.VMEM((2, HEAD_DIM, ATT_Q), F32)]


def _sb_scratch():
    return [pltpu.VMEM((2, 2, ATT_K, ATT_Q), F32), pltpu.VMEM((2, ATT_K, ATT_Q), BF16),
            pltpu.VMEM((2, ATT_K, ATT_Q), F32), pltpu.VMEM((2, 1, ATT_Q), F32),
            pltpu.VMEM((2, HEAD_DIM, ATT_Q), F32)]


def _outproj_kernel(x_ref, mf_ref, ms_ref, w_ref, mod_ref, g_ref, x1_ref, h2_ref):
    mix = jnp.concatenate([mf_ref[0], ms_ref[0]], axis=-1)
    x1 = x_ref[0] + mod_ref[0, 2:3, :] * _dot(mix, w_ref[...])
    x1_ref[0] = x1
    shift = mod_ref[0, 3:4, :]
    scale = mod_ref[0, 4:5, :]
    h2_ref[0] = (_rms_rows(x1) * g_ref[...] * (1.0 + scale) + shift).astype(BF16)


def _outproj(x, mix_f, mix_s, w_out, mod, g):
    bsz, s, d = x.shape
    tm = OUT_ROWS
    row = lambda b, i: (b, i, 0)
    return pl.pallas_call(
        _outproj_kernel,
        grid=(bsz, s // tm),
        in_specs=[pl.BlockSpec((1, tm, d), row),
                  pl.BlockSpec((1, tm, mix_f.shape[2]), row),
                  pl.BlockSpec((1, tm, mix_s.shape[2]), row),
                  pl.BlockSpec(w_out.shape, lambda b, i: (0, 0)),
                  pl.BlockSpec((1, N_MOD, d), lambda b, i: (b, 0, 0)),
                  pl.BlockSpec((1, d), lambda b, i: (0, 0))],
        out_specs=[pl.BlockSpec((1, tm, d), row), pl.BlockSpec((1, tm, d), row)],
        out_shape=[jax.ShapeDtypeStruct((bsz, s, d), F32), jax.ShapeDtypeStruct((bsz, s, d), BF16)],
        compiler_params=_params("arbitrary", "arbitrary"),
        name="outproj",
    )(x, mix_f, mix_s, w_out, mod, g)


def _mlp_kernel(h_ref, halo_ref, x1_ref, mod_ref, wg_ref, wv_ref, cwg_ref, cbg_ref, cwv_ref, cbv_ref,
                wd_ref, gf_ref, o_ref, acc_ref, *, final_norm):
    i = pl.program_id(1)
    c = pl.program_id(2)
    tm = h_ref.shape[1]
    halo = halo_ref[0]
    halo = jnp.where(i > 0, halo, jnp.zeros_like(halo))
    hx = jnp.concatenate([halo, h_ref[0]], axis=0)

    def conv_branch(w_ref, cw_ref, cb_ref):
        u = _dot(hx, w_ref[...])
        out = cb_ref[...]
        for tap in range(CONV_WIDTH):
            lag = CONV_WIDTH - 1 - tap
            first = BF16_SUBLANES - lag
            out = out + cw_ref[tap:tap + 1, :] * u[first:first + tm, :]
        return out

    u_gate = conv_branch(wg_ref, cwg_ref, cbg_ref)
    u_val = conv_branch(wv_ref, cwv_ref, cbv_ref)
    part = _dot((u_gate * jax.nn.sigmoid(u_gate) * u_val).astype(BF16), wd_ref[...])

    @pl.when(c == 0)
    def _():
        acc_ref[...] = part

    @pl.when(c > 0)
    def _():
        acc_ref[...] += part

    @pl.when(c == pl.num_programs(2) - 1)
    def _():
        x2 = x1_ref[0] + mod_ref[0, 5:6, :] * acc_ref[...]
        o_ref[0] = _rms_rows(x2) * gf_ref[...] if final_norm else x2


def _mlp(h2, x1, mod, w_gate, w_val, cw_gate, cb_gate, cw_val, cb_val, w_down, g_final, final_norm):
    bsz, s, d = x1.shape
    tm, tf = OUT_ROWS, FF_CHUNK
    halo_blocks = tm // BF16_SUBLANES
    row = lambda b, i, c: (b, i, 0)
    col = lambda b, i, c: (0, c)
    return pl.pallas_call(
        functools.partial(_mlp_kernel, final_norm=final_norm),
        grid=(bsz, s // tm, w_gate.shape[1] // tf),
        in_specs=[pl.BlockSpec((1, tm, d), row),
                  pl.BlockSpec((1, BF16_SUBLANES, d),
                               lambda b, i, c: (b, jnp.maximum(i * halo_blocks - 1, 0), 0)),
                  pl.BlockSpec((1, tm, d), row),
                  pl.BlockSpec((1, N_MOD, d), lambda b, i, c: (b, 0, 0)),
                  pl.BlockSpec((d, tf), col), pl.BlockSpec((d, tf), col),
                  pl.BlockSpec((CONV_WIDTH, tf), col), pl.BlockSpec((1, tf), col),
                  pl.BlockSpec((CONV_WIDTH, tf), col), pl.BlockSpec((1, tf), col),
                  pl.BlockSpec((tf, d), lambda b, i, c: (c, 0)),
                  pl.BlockSpec((1, d), lambda b, i, c: (0, 0))],
        out_specs=pl.BlockSpec((1, tm, d), row),
        out_shape=jax.ShapeDtypeStruct((bsz, s, d), F32),
        scratch_shapes=[pltpu.VMEM((tm, d), F32)],
        compiler_params=_params("arbitrary", "arbitrary", "arbitrary"),
        name="mlp",
    )(h2, h2, x1, mod, w_gate, w_val, cw_gate, cb_gate, cw_val, cb_val, w_down, g_final)


def _pad_cols(a, n):
    return jnp.pad(a, ((0, 0), (0, n - a.shape[1])))


def kernel(x, c, w_ada, b_ada, g_attn, w_in, b_fgate, g_out_fox, g_out_sb, w_out,
           g_mlp, w_up, conv_w, conv_b, w_down, g_final):
    depth, d, _ = w_ada.shape
    n_fox = b_fgate.shape[1]
    d_fox = n_fox * HEAD_DIM
    d_sb = g_out_sb.shape[1]
    n_sb = d_sb // HEAD_DIM
    d_ff = w_down.shape[1]
    d_ff_pad = -(-d_ff // FF_CHUNK) * FF_CHUNK
    assert n_fox % 2 == 0 and n_sb % 2 == 0 and 3 * n_fox <= LANES
    assert x.shape[1] % OUT_ROWS == 0 and x.shape[1] % ATT_Q == 0 and ATT_Q % ATT_K == 0
    o_kf, o_vf, o_qs, o_ks, o_vs, o_gate = (d_fox, 2 * d_fox, 3 * d_fox, 3 * d_fox + d_sb,
                                             3 * d_fox + 2 * d_sb, 3 * d_fox + 3 * d_sb)

    for l in range(depth):
        mod = _ada(c, w_ada[l], b_ada[l]).reshape(-1, N_MOD, d)
        w = w_in[l]
        w_nat = jnp.concatenate([w[:, :o_vf], w[:, o_qs:o_vs]], axis=1).astype(BF16)
        w_vt = jnp.concatenate([w[:, o_vf:o_qs], w[:, o_vs:o_gate]], axis=1).T.astype(BF16)
        w_gate = _pad_cols(w[:, o_gate:], LANES).astype(BF16)
        b_gate = _pad_cols(b_fgate[l].reshape(1, n_fox), LANES)
        qk, vt, log_f = _inproj(x, mod, g_attn[l].reshape(1, d), w_nat, w_vt, w_gate, b_gate)

        pairs_f, pairs_s = n_fox // 2, n_sb // 2
        k_aug = _decay(log_f, qk, n_fox, k_block=1)
        fox_k_spec = pl.BlockSpec((1, 2, x.shape[1], LANES), lambda b, p, i: (b, p, 0, 0))
        mix_f = _attention(_fox_kernel, "fox", _fox_scratch(), qk, k_aug, fox_k_spec, vt, g_out_fox[l],
                           q_block0=0, vt_block0=0, n_heads=n_fox)
        sb_k_spec = pl.BlockSpec((1, x.shape[1], LANES), lambda b, p, i: (b, 0, 2 * pairs_f + pairs_s + p))
        mix_s = _attention(_sb_kernel, "sb", _sb_scratch(), qk, qk, sb_k_spec, vt, g_out_sb[l],
                           q_block0=2 * pairs_f, vt_block0=pairs_f, n_heads=n_sb)

        x1, h2 = _outproj(x, mix_f, mix_s, w_out[l].astype(BF16), mod, g_mlp[l].reshape(1, d))

        wu, cw, cb = w_up[l], conv_w[l], conv_b[l].reshape(1, -1)
        x = _mlp(h2, x1, mod,
                 _pad_cols(wu[:, :d_ff], d_ff_pad).astype(BF16), _pad_cols(wu[:, d_ff:], d_ff_pad).astype(BF16),
                 _pad_cols(cw[:, :d_ff], d_ff_pad), _pad_cols(cb[:, :d_ff], d_ff_pad),
                 _pad_cols(cw[:, d_ff:], d_ff_pad), _pad_cols(cb[:, d_ff:], d_ff_pad),
                 jnp.pad(w_down[l], ((0, d_ff_pad - d_ff), (0, 0))).astype(BF16),
                 g_final.reshape(1, d), final_norm=(l == depth - 1))
    return x
```

```python
import functools

import numpy as np
import jax
import jax.numpy as jnp
from jax import lax
from jax.experimental import pallas as pl
from jax.experimental.pallas import tpu as pltpu

HEAD_DIM = 64
N_MOD = 6
CONV_WIDTH = 3
EPS = 1e-6

LANES = 128
BF16_SUBLANES = 16
VMEM_LIMIT_BYTES = 48 * 1024 * 1024

ATT_Q = 512
ATT_K = 256
ATT_COLS = 256
PROJ_ROWS = ATT_K
LOG2E = 1.4426950408889634
MASKED = -1e30
M_INIT = -1e29
OUT_ROWS = 512
FF_CHUNK = 256

F32 = jnp.float32
BF16 = jnp.bfloat16
NT_DIMS = (((1,), (1,)), ((), ()))


def _dot(a, b):
    return jnp.dot(a, b, preferred_element_type=F32)


def _dot_nt(a, b):
    return lax.dot_general(a, b, NT_DIMS, preferred_element_type=F32)


def _params(*sem):
    return pltpu.CompilerParams(dimension_semantics=sem, vmem_limit_bytes=VMEM_LIMIT_BYTES)


def _rms_rows(x):
    return x * lax.rsqrt(jnp.mean(x * x, axis=-1, keepdims=True) + EPS)


def _softplus(z):
    return jnp.maximum(z, 0.0) + jnp.log(1.0 + jnp.exp(-jnp.abs(z)))


def _split3(x):
    hi = x.astype(BF16)
    r1 = x - hi.astype(F32)
    mid = r1.astype(BF16)
    lo = (r1 - mid.astype(F32)).astype(BF16)
    return hi, mid, lo


def _ada_kernel(c_ref, w_ref, b_ref, o_ref):
    c = c_ref[...]
    o_ref[...] = _dot(c * jax.nn.sigmoid(c), w_ref[...]) + b_ref[...]


def _ada(c, w, b):
    bsz, d = c.shape
    n = w.shape[1]
    return pl.pallas_call(
        _ada_kernel,
        grid=(n // d,),
        in_specs=[pl.BlockSpec((bsz, d), lambda j: (0, 0)),
                  pl.BlockSpec((d, d), lambda j: (0, j)),
                  pl.BlockSpec((1, d), lambda j: (0, j))],
        out_specs=pl.BlockSpec((bsz, d), lambda j: (0, j)),
        out_shape=jax.ShapeDtypeStruct((bsz, n), F32),
        compiler_params=_params("arbitrary"),
        name="ada",
    )(c, w, b.reshape(1, n))


def _inproj_kernel(x_ref, mod_ref, g_ref, wn_ref, wvt_ref, wg_ref, bg_ref, qk_ref, vt_ref, lf_ref):
    shift = mod_ref[0, 0:1, :]
    scale = mod_ref[0, 1:2, :]
    h = (_rms_rows(x_ref[0]) * g_ref[...] * (1.0 + scale) + shift).astype(BF16)
    qk_ref[0] = _dot(h, wn_ref[...]).astype(BF16)
    vt_ref[0, 0] = _dot_nt(wvt_ref[...], h).astype(BF16)
    logit = _dot(h, wg_ref[...]) + bg_ref[...]
    lf_ref[0] = -_softplus(-logit)


def _inproj(x, mod, g, w_nat, w_vt, w_gate, b_gate):
    bsz, s, d = x.shape
    tm = PROJ_ROWS
    n_nat, n_v = w_nat.shape[1], w_vt.shape[0]
    const = lambda b, i: (0, 0)
    return pl.pallas_call(
        _inproj_kernel,
        grid=(bsz, s // tm),
        in_specs=[pl.BlockSpec((1, tm, d), lambda b, i: (b, i, 0)),
                  pl.BlockSpec((1, N_MOD, d), lambda b, i: (b, 0, 0)),
                  pl.BlockSpec((1, d), const),
                  pl.BlockSpec((d, n_nat), const),
                  pl.BlockSpec((n_v, d), const),
                  pl.BlockSpec((d, LANES), const),
                  pl.BlockSpec((1, LANES), const)],
        out_specs=[pl.BlockSpec((1, tm, n_nat), lambda b, i: (b, i, 0)),
                   pl.BlockSpec((1, 1, n_v, tm), lambda b, i: (b, i, 0, 0)),
                   pl.BlockSpec((1, tm, LANES), lambda b, i: (b, i, 0))],
        out_shape=[jax.ShapeDtypeStruct((bsz, s, n_nat), BF16),
                   jax.ShapeDtypeStruct((bsz, s // tm, n_v, tm), BF16),
                   jax.ShapeDtypeStruct((bsz, s, LANES), F32)],
        compiler_params=_params("arbitrary", "arbitrary"),
        name="inproj",
    )(x, mod, g, w_nat, w_vt, w_gate, b_gate)


def _decay_kernel(lf_ref, k_ref, sel_ref, kaug_ref, carry_ref, *, n_heads):
    @pl.when(pl.program_id(1) == 0)
    def _():
        carry_ref[...] = jnp.zeros_like(carry_ref)

    tm = lf_ref.shape[1]
    lane = lax.broadcasted_iota(jnp.int32, (tm, LANES), 1)
    lf = jnp.where(lane < n_heads, lf_ref[0], 0.0)
    row = lax.broadcasted_iota(jnp.int32, (tm, tm), 0)
    col = lax.broadcasted_iota(jnp.int32, (tm, tm), 1)
    tri = (col <= row).astype(BF16)
    hi, mid, lo = _split3(lf)
    f_run = carry_ref[...] + (_dot(tri, hi) + _dot(tri, mid) + _dot(tri, lo))
    carry_ref[...] = f_run[tm - 1:tm, :]
    ghi, gmid, glo = _split3(-LOG2E * f_run)
    packed = (ghi.astype(F32) + pltpu.roll(gmid.astype(F32), n_heads, 1)
              + pltpu.roll(glo.astype(F32), 2 * n_heads, 1)).astype(BF16)
    placed = _dot(packed, sel_ref[...])
    k_all = k_ref[0]
    for h in range(n_heads):
        k_pair = k_all[:, (h // 2) * LANES:(h // 2 + 1) * LANES]
        own = (lane < HEAD_DIM) if h % 2 == 0 else (lane >= HEAD_DIM)
        kaug_ref[0, h] = jnp.where(own, k_pair, placed[:, h * LANES:(h + 1) * LANES].astype(BF16))


def _decay_select_matrix(n_heads):
    sel = np.zeros((LANES, n_heads * LANES), np.float32)
    for h in range(n_heads):
        base = h * LANES + (HEAD_DIM if h % 2 == 0 else 0)
        for term in range(3):
            sel[term * n_heads + h, base + term] = 1.0
    return jnp.asarray(sel, BF16)


def _decay(log_f, qk, n_heads, k_block):
    bsz, s, _ = log_f.shape
    tm = PROJ_ROWS
    d_grp = n_heads * HEAD_DIM
    return pl.pallas_call(
        functools.partial(_decay_kernel, n_heads=n_heads),
        grid=(bsz, s // tm),
        in_specs=[pl.BlockSpec((1, tm, LANES), lambda b, i: (b, i, 0)),
                  pl.BlockSpec((1, tm, d_grp), lambda b, i: (b, i, k_block)),
                  pl.BlockSpec((LANES, n_heads * LANES), lambda b, i: (0, 0))],
        out_specs=pl.BlockSpec((1, n_heads, tm, LANES), lambda b, i: (b, 0, i, 0)),
        out_shape=jax.ShapeDtypeStruct((bsz, n_heads, s, LANES), BF16),
        scratch_shapes=[pltpu.VMEM((1, LANES), F32)],
        compiler_params=_params("arbitrary", "arbitrary"),
        name="decay",
    )(log_f, qk, _decay_select_matrix(n_heads))


def _lane_queries(q_ref, extra_even, extra_odd, cw):
    q = q_ref[0].astype(F32) * (HEAD_DIM ** -0.5 * LOG2E)
    lane = lax.broadcasted_iota(jnp.int32, q.shape, 1)
    heads = (jnp.where(lane < HEAD_DIM, q, extra_even(lane)).T.astype(BF16),
             jnp.where(lane >= HEAD_DIM, q, extra_odd(lane)).T.astype(BF16))
    return [heads[hh][:, c * cw:(c + 1) * cw] for hh in range(2) for c in range(q.shape[0] // cw)]


def _visibility(first_key, first_query, bk, cw, strict):
    last_visible_gap = -1 if strict else 0
    if first_key + bk - 1 - first_query <= last_visible_gap:
        return "all"
    if first_key - (first_query + cw - 1) > last_visible_gap:
        return "none"
    gap = (lax.broadcasted_iota(jnp.int32, (bk, cw), 0) - lax.broadcasted_iota(jnp.int32, (bk, cw), 1)
           + (first_key - first_query))
    return gap <= last_visible_gap


def _finish_heads(lanes, g_ref, o_ref):
    n_chunks = len(lanes) // 2
    outs = [jnp.concatenate(lanes[hh * n_chunks:(hh + 1) * n_chunks], axis=1) for hh in range(2)]
    normed = [o * lax.rsqrt(jnp.mean(o * o, axis=0, keepdims=True) + EPS) for o in outs]
    o_ref[0] = (jnp.concatenate(normed, axis=0).T * g_ref[...]).astype(o_ref.dtype)


def _fox_kernel(q_ref, k_ref, vt_ref, g_ref, o_ref, s_buf, cmax_buf, p_buf, acc_buf):
    qi = pl.program_id(2)
    bk = vt_ref.shape[3]
    n_lanes, cw = acc_buf.shape[0], acc_buf.shape[2]
    n_chunks = n_lanes // 2
    n_full = 2 * qi
    ones3 = lambda lo: (lambda lane: jnp.where((lane >= lo) & (lane < lo + 3), 1.0, 0.0))
    queries = _lane_queries(q_ref, ones3(HEAD_DIM), ones3(0), cw)
    s_buf[1] = jnp.full(s_buf.shape[1:], MASKED, F32)
    cmax_buf[1] = jnp.full(cmax_buf.shape[1:], MASKED, F32)
    p_buf[0] = jnp.zeros(p_buf.shape[1:], BF16)
    acc_buf[...] = jnp.zeros(acc_buf.shape, F32)

    def step(t, slot, carry, diagonal=None):
        new = []
        for li in range(n_lanes):
            hh, c = divmod(li, n_chunks)
            if diagonal is None:
                visible = "all"
            elif diagonal >= 2:
                visible = "none"
            else:
                visible = _visibility(diagonal * bk, c * cw, bk, cw, strict=False)
            if isinstance(visible, str) and visible == "none":
                s_buf[1 - slot, li] = jnp.full((bk, cw), MASKED, F32)
                cmax_buf[1 - slot, li] = jnp.full((1, cw), MASKED, F32)
            else:
                start = pl.multiple_of((t + 1) * bk, bk)
                s_new = _dot(k_ref[0, hh, pl.ds(start, bk), :], queries[li])
                if not isinstance(visible, str):
                    s_new = jnp.where(visible, s_new, MASKED)
                s_buf[1 - slot, li] = s_new
                cmax_buf[1 - slot, li] = jnp.max(s_new, axis=0, keepdims=True)
            vt = vt_ref[0, jnp.maximum(t - 1, 0), hh * HEAD_DIM:(hh + 1) * HEAD_DIM, :]
            pv = _dot(vt, p_buf[1 - slot, li])
            m, l = carry[li]
            m_new = jnp.maximum(m, cmax_buf[slot, li])
            alpha = jnp.exp2(m - m_new)
            p = jnp.exp2(s_buf[slot, li] - m_new)
            p_buf[slot, li] = p.astype(BF16)
            acc_buf[li] = alpha * (acc_buf[li] + pv)
            new.append((m_new, alpha * l + jnp.sum(p, axis=0, keepdims=True)))
        return tuple(new)

    def step_pair(i, carry):
        t = 2 * i - 1
        return step(t + 1, 0, step(t, 1, carry))

    carry = tuple((jnp.full((1, cw), M_INIT, F32), jnp.zeros((1, cw), F32)) for _ in range(n_lanes))
    carry = lax.fori_loop(0, qi, step_pair, carry)
    carry = step(n_full - 1, 1, carry, diagonal=0)
    carry = step(n_full, 0, carry, diagonal=1)
    carry = step(n_full + 1, 1, carry, diagonal=2)
    carry = step(n_full + 2, 0, carry, diagonal=3)
    _finish_heads([acc_buf[li] / carry[li][1] for li in range(n_lanes)], g_ref, o_ref)


def _sb_kernel(q_ref, k_ref, vt_ref, g_ref, o_ref, z_buf, sp_buf, e_buf, wrow_buf, acc_buf):
    qi = pl.program_id(2)
    bk = vt_ref.shape[3]
    n_lanes, cw = acc_buf.shape[0], acc_buf.shape[2]
    n_chunks = n_lanes // 2
    n_tiles = 2 * (qi + 1)
    zero = lambda lane: 0.0
    queries = _lane_queries(q_ref, zero, zero, cw)
    suffix = (lax.broadcasted_iota(jnp.int32, (bk, bk), 1)
              >= lax.broadcasted_iota(jnp.int32, (bk, bk), 0)).astype(BF16)
    z_buf[...] = jnp.full(z_buf.shape, MASKED, F32)
    e_buf[1] = jnp.full(e_buf.shape[1:], MASKED, F32)
    sp_buf[0] = jnp.zeros(sp_buf.shape[1:], BF16)
    wrow_buf[1] = jnp.zeros(wrow_buf.shape[1:], F32)
    acc_buf[...] = jnp.zeros(acc_buf.shape, F32)

    def step(t, slot, later, diagonal=None):
        new_later = []
        for li in range(n_lanes):
            hh, c = divmod(li, n_chunks)
            within = _dot(suffix, sp_buf[1 - slot, li])
            e_buf[1 - slot, li] = z_buf[1 - slot, li] - within
            wrow_buf[1 - slot, li] = within[0:1, :]
            if diagonal is None:
                visible = "all"
            elif diagonal < 0:
                visible = "none"
            else:
                visible = _visibility(diagonal * bk, c * cw, bk, cw, strict=True)
            if isinstance(visible, str) and visible == "none":
                z_buf[1 - slot, li] = jnp.full((bk, cw), MASKED, F32)
            else:
                start = pl.multiple_of((n_tiles - 1 - (t + 3)) * bk, bk)
                z_new = _dot(k_ref[0, pl.ds(start, bk), :], queries[li])
                if not isinstance(visible, str):
                    z_new = jnp.where(visible, z_new, MASKED)
                z_buf[1 - slot, li] = z_new
            a = jnp.exp2(e_buf[slot, li] - later[li])
            vt = vt_ref[0, n_tiles - 1 - jnp.maximum(t, 0), hh * HEAD_DIM:(hh + 1) * HEAD_DIM, :]
            acc_buf[li] += _dot(vt, a.astype(BF16))
            new_later.append(later[li] + wrow_buf[slot, li])
            z = z_buf[slot, li]
            sp_buf[slot, li] = (jnp.maximum(z, 0.0) + jnp.log2(1.0 + jnp.exp2(-jnp.abs(z)))).astype(BF16)
        return tuple(new_later)

    def step_pair(i, later):
        t = 2 * i - 1
        return step(t + 1, 0, step(t, 1, later))

    later = tuple(jnp.zeros((1, cw), F32) for _ in range(n_lanes))
    later = step(-3, 1, later, diagonal=1)
    later = step(-2, 0, later, diagonal=0)
    later = lax.fori_loop(0, qi, step_pair, later)
    later = step(n_tiles - 3, 1, later, diagonal=-1)
    later = step(n_tiles - 2, 0, later, diagonal=-1)
    later = step(n_tiles - 1, 1, later, diagonal=-1)
    _finish_heads([acc_buf[li] for li in range(n_lanes)], g_ref, o_ref)


def _attention(body, name, scratch, qk, k_arr, k_spec, vt, g, q_block0, vt_block0, n_heads):
    bsz, s, _ = qk.shape
    bq, bk = ATT_Q, ATT_K
    d_grp = n_heads * HEAD_DIM
    return pl.pallas_call(
        body,
        grid=(bsz, n_heads // 2, s // bq),
        in_specs=[pl.BlockSpec((1, bq, LANES), lambda b, p, i: (b, i, q_block0 + p)),
                  k_spec,
                  pl.BlockSpec((1, s // bk, LANES, bk), lambda b, p, i: (b, 0, vt_block0 + p, 0)),
                  pl.BlockSpec((1, LANES), lambda b, p, i: (0, p))],
        out_specs=pl.BlockSpec((1, bq, LANES), lambda b, p, i: (b, i, p)),
        out_shape=jax.ShapeDtypeStruct((bsz, s, d_grp), BF16),
        scratch_shapes=scratch,
        compiler_params=_params("arbitrary", "arbitrary", "arbitrary"),
        name=name,
    )(qk, k_arr, vt, g.reshape(1, d_grp))


ATT_LANE_GROUPS = 2 * (ATT_Q // ATT_COLS)


def _fox_scratch():
    n = ATT_LANE_GROUPS
    return [pltpu.VMEM((2, n, ATT_K, ATT_COLS), F32), pltpu.VMEM((2, n, 1, ATT_COLS), F32),
            pltpu.VMEM((2, n, ATT_K, ATT_COLS), BF16), pltpu.VMEM((n, HEAD_DIM, ATT_COLS), F32)]


def _sb_scratch():
    n = ATT_LANE_GROUPS
    return [pltpu.VMEM((2, n, ATT_K, ATT_COLS), F32), pltpu.VMEM((2, n, ATT_K, ATT_COLS), BF16),
            pltpu.VMEM((2, n, ATT_K, ATT_COLS), F32), pltpu.VMEM((2, n, 1, ATT_COLS), F32),
            pltpu.VMEM((n, HEAD_DIM, ATT_COLS), F32)]


def _outproj_kernel(x_ref, mf_ref, ms_ref, w_ref, mod_ref, g_ref, x1_ref, h2_ref):
    mix = jnp.concatenate([mf_ref[0], ms_ref[0]], axis=-1)
    x1 = x_ref[0] + mod_ref[0, 2:3, :] * _dot(mix, w_ref[...])
    x1_ref[0] = x1
    shift = mod_ref[0, 3:4, :]
    scale = mod_ref[0, 4:5, :]
    h2_ref[0] = (_rms_rows(x1) * g_ref[...] * (1.0 + scale) + shift).astype(BF16)


def _outproj(x, mix_f, mix_s, w_out, mod, g):
    bsz, s, d = x.shape
    tm = OUT_ROWS
    row = lambda b, i: (b, i, 0)
    return pl.pallas_call(
        _outproj_kernel,
        grid=(bsz, s // tm),
        in_specs=[pl.BlockSpec((1, tm, d), row),
                  pl.BlockSpec((1, tm, mix_f.shape[2]), row),
                  pl.BlockSpec((1, tm, mix_s.shape[2]), row),
                  pl.BlockSpec(w_out.shape, lambda b, i: (0, 0)),
                  pl.BlockSpec((1, N_MOD, d), lambda b, i: (b, 0, 0)),
                  pl.BlockSpec((1, d), lambda b, i: (0, 0))],
        out_specs=[pl.BlockSpec((1, tm, d), row), pl.BlockSpec((1, tm, d), row)],
        out_shape=[jax.ShapeDtypeStruct((bsz, s, d), F32), jax.ShapeDtypeStruct((bsz, s, d), BF16)],
        compiler_params=_params("arbitrary", "arbitrary"),
        name="outproj",
    )(x, mix_f, mix_s, w_out, mod, g)


def _mlp_kernel(h_ref, halo_ref, x1_ref, mod_ref, wg_ref, wv_ref, cwg_ref, cbg_ref, cwv_ref, cbv_ref,
                wd_ref, gf_ref, o_ref, acc_ref, *, final_norm):
    i = pl.program_id(1)
    c = pl.program_id(2)
    tm = h_ref.shape[1]
    halo = halo_ref[0]
    halo = jnp.where(i > 0, halo, jnp.zeros_like(halo))
    hx = jnp.concatenate([halo, h_ref[0]], axis=0)

    def conv_branch(w_ref, cw_ref, cb_ref):
        u = _dot(hx, w_ref[...])
        out = cb_ref[...]
        for tap in range(CONV_WIDTH):
            lag = CONV_WIDTH - 1 - tap
            first = BF16_SUBLANES - lag
            out = out + cw_ref[tap:tap + 1, :] * u[first:first + tm, :]
        return out

    u_gate = conv_branch(wg_ref, cwg_ref, cbg_ref)
    u_val = conv_branch(wv_ref, cwv_ref, cbv_ref)
    part = _dot((u_gate * jax.nn.sigmoid(u_gate) * u_val).astype(BF16), wd_ref[...])

    @pl.when(c == 0)
    def _():
        acc_ref[...] = part

    @pl.when(c > 0)
    def _():
        acc_ref[...] += part

    @pl.when(c == pl.num_programs(2) - 1)
    def _():
        x2 = x1_ref[0] + mod_ref[0, 5:6, :] * acc_ref[...]
        o_ref[0] = _rms_rows(x2) * gf_ref[...] if final_norm else x2


def _mlp(h2, x1, mod, w_gate, w_val, cw_gate, cb_gate, cw_val, cb_val, w_down, g_final, final_norm):
    bsz, s, d = x1.shape
    tm, tf = OUT_ROWS, FF_CHUNK
    halo_blocks = tm // BF16_SUBLANES
    row = lambda b, i, c: (b, i, 0)
    col = lambda b, i, c: (0, c)
    return pl.pallas_call(
        functools.partial(_mlp_kernel, final_norm=final_norm),
        grid=(bsz, s // tm, w_gate.shape[1] // tf),
        in_specs=[pl.BlockSpec((1, tm, d), row),
                  pl.BlockSpec((1, BF16_SUBLANES, d),
                               lambda b, i, c: (b, jnp.maximum(i * halo_blocks - 1, 0), 0)),
                  pl.BlockSpec((1, tm, d), row),
                  pl.BlockSpec((1, N_MOD, d), lambda b, i, c: (b, 0, 0)),
                  pl.BlockSpec((d, tf), col), pl.BlockSpec((d, tf), col),
                  pl.BlockSpec((CONV_WIDTH, tf), col), pl.BlockSpec((1, tf), col),
                  pl.BlockSpec((CONV_WIDTH, tf), col), pl.BlockSpec((1, tf), col),
                  pl.BlockSpec((tf, d), lambda b, i, c: (c, 0)),
                  pl.BlockSpec((1, d), lambda b, i, c: (0, 0))],
        out_specs=pl.BlockSpec((1, tm, d), row),
        out_shape=jax.ShapeDtypeStruct((bsz, s, d), F32),
        scratch_shapes=[pltpu.VMEM((tm, d), F32)],
        compiler_params=_params("arbitrary", "arbitrary", "arbitrary"),
        name="mlp",
    )(h2, h2, x1, mod, w_gate, w_val, cw_gate, cb_gate, cw_val, cb_val, w_down, g_final)


def _pad_cols(a, n):
    return jnp.pad(a, ((0, 0), (0, n - a.shape[1])))


def kernel(x, c, w_ada, b_ada, g_attn, w_in, b_fgate, g_out_fox, g_out_sb, w_out,
           g_mlp, w_up, conv_w, conv_b, w_down, g_final):
    depth, d, _ = w_ada.shape
    n_fox = b_fgate.shape[1]
    d_fox = n_fox * HEAD_DIM
    d_sb = g_out_sb.shape[1]
    n_sb = d_sb // HEAD_DIM
    d_ff = w_down.shape[1]
    d_ff_pad = -(-d_ff // FF_CHUNK) * FF_CHUNK
    assert n_fox % 2 == 0 and n_sb % 2 == 0 and 3 * n_fox <= LANES
    assert x.shape[1] % OUT_ROWS == 0 and x.shape[1] % ATT_Q == 0 and ATT_Q == 2 * ATT_K
    o_kf, o_vf, o_qs, o_ks, o_vs, o_gate = (d_fox, 2 * d_fox, 3 * d_fox, 3 * d_fox + d_sb,
                                             3 * d_fox + 2 * d_sb, 3 * d_fox + 3 * d_sb)

    for l in range(depth):
        mod = _ada(c, w_ada[l], b_ada[l]).reshape(-1, N_MOD, d)
        w = w_in[l]
        w_nat = jnp.concatenate([w[:, :o_vf], w[:, o_qs:o_vs]], axis=1).astype(BF16)
        w_vt = jnp.concatenate([w[:, o_vf:o_qs], w[:, o_vs:o_gate]], axis=1).T.astype(BF16)
        w_gate = _pad_cols(w[:, o_gate:], LANES).astype(BF16)
        b_gate = _pad_cols(b_fgate[l].reshape(1, n_fox), LANES)
        qk, vt, log_f = _inproj(x, mod, g_attn[l].reshape(1, d), w_nat, w_vt, w_gate, b_gate)

        pairs_f, pairs_s = n_fox // 2, n_sb // 2
        k_aug = _decay(log_f, qk, n_fox, k_block=1)
        fox_k_spec = pl.BlockSpec((1, 2, x.shape[1], LANES), lambda b, p, i: (b, p, 0, 0))
        mix_f = _attention(_fox_kernel, "fox", _fox_scratch(), qk, k_aug, fox_k_spec, vt, g_out_fox[l],
                           q_block0=0, vt_block0=0, n_heads=n_fox)
        sb_k_spec = pl.BlockSpec((1, x.shape[1], LANES), lambda b, p, i: (b, 0, 2 * pairs_f + pairs_s + p))
        mix_s = _attention(_sb_kernel, "sb", _sb_scratch(), qk, qk, sb_k_spec, vt, g_out_sb[l],
                           q_block0=2 * pairs_f, vt_block0=pairs_f, n_heads=n_sb)

        x1, h2 = _outproj(x, mix_f, mix_s, w_out[l].astype(BF16), mod, g_mlp[l].reshape(1, d))

        wu, cw, cb = w_up[l], conv_w[l], conv_b[l].reshape(1, -1)
        x = _mlp(h2, x1, mod,
                 _pad_cols(wu[:, :d_ff], d_ff_pad).astype(BF16), _pad_cols(wu[:, d_ff:], d_ff_pad).astype(BF16),
                 _pad_cols(cw[:, :d_ff], d_ff_pad), _pad_cols(cb[:, :d_ff], d_ff_pad),
                 _pad_cols(cw[:, d_ff:], d_ff_pad), _pad_cols(cb[:, d_ff:], d_ff_pad),
                 jnp.pad(w_down[l], ((0, d_ff_pad - d_ff), (0, 0))).astype(BF16),
                 g_final.reshape(1, d), final_norm=(l == depth - 1))
    return x
```

```python
import functools

import numpy as np
import jax
import jax.numpy as jnp
from jax import lax
from jax.experimental import pallas as pl
from jax.experimental.pallas import tpu as pltpu

HEAD_DIM = 64
N_MOD = 6
CONV_WIDTH = 3
EPS = 1e-6

LANES = 128
BF16_SUBLANES = 16
VMEM_LIMIT_BYTES = 48 * 1024 * 1024

ATT_Q = 512
ATT_K = 256
ATT_COLS = 256
PROJ_ROWS = ATT_K
LOG2E = 1.4426950408889634
MASKED = -1e30
M_INIT = -1e29
OUT_ROWS = 512
FF_CHUNK = 256

F32 = jnp.float32
BF16 = jnp.bfloat16
NT_DIMS = (((1,), (1,)), ((), ()))


def _dot(a, b):
    return jnp.dot(a, b, preferred_element_type=F32)


def _dot_nt(a, b):
    return lax.dot_general(a, b, NT_DIMS, preferred_element_type=F32)


def _params(*sem):
    return pltpu.CompilerParams(dimension_semantics=sem, vmem_limit_bytes=VMEM_LIMIT_BYTES)


def _rms_rows(x):
    return x * lax.rsqrt(jnp.mean(x * x, axis=-1, keepdims=True) + EPS)


def _softplus(z):
    return jnp.maximum(z, 0.0) + jnp.log(1.0 + jnp.exp(-jnp.abs(z)))


def _split3(x):
    hi = x.astype(BF16)
    r1 = x - hi.astype(F32)
    mid = r1.astype(BF16)
    lo = (r1 - mid.astype(F32)).astype(BF16)
    return hi, mid, lo


def _ada_kernel(c_ref, w_ref, b_ref, o_ref):
    c = c_ref[...]
    o_ref[...] = _dot(c * jax.nn.sigmoid(c), w_ref[...]) + b_ref[...]


def _ada(c, w, b):
    bsz, d = c.shape
    n = w.shape[1]
    return pl.pallas_call(
        _ada_kernel,
        grid=(n // d,),
        in_specs=[pl.BlockSpec((bsz, d), lambda j: (0, 0)),
                  pl.BlockSpec((d, d), lambda j: (0, j)),
                  pl.BlockSpec((1, d), lambda j: (0, j))],
        out_specs=pl.BlockSpec((bsz, d), lambda j: (0, j)),
        out_shape=jax.ShapeDtypeStruct((bsz, n), F32),
        compiler_params=_params("arbitrary"),
        name="ada",
    )(c, w, b.reshape(1, n))


def _inproj_kernel(x_ref, mod_ref, g_ref, wn_ref, wvt_ref, wg_ref, bg_ref, qk_ref, vt_ref, lf_ref):
    shift = mod_ref[0, 0:1, :]
    scale = mod_ref[0, 1:2, :]
    h = (_rms_rows(x_ref[0]) * g_ref[...] * (1.0 + scale) + shift).astype(BF16)
    qk_ref[0] = _dot(h, wn_ref[...]).astype(BF16)
    vt_ref[0, 0] = _dot_nt(wvt_ref[...], h).astype(BF16)
    logit = _dot(h, wg_ref[...]) + bg_ref[...]
    lf_ref[0] = -_softplus(-logit)


def _inproj(x, mod, g, w_nat, w_vt, w_gate, b_gate):
    bsz, s, d = x.shape
    tm = PROJ_ROWS
    n_nat, n_v = w_nat.shape[1], w_vt.shape[0]
    const = lambda b, i: (0, 0)
    return pl.pallas_call(
        _inproj_kernel,
        grid=(bsz, s // tm),
        in_specs=[pl.BlockSpec((1, tm, d), lambda b, i: (b, i, 0)),
                  pl.BlockSpec((1, N_MOD, d), lambda b, i: (b, 0, 0)),
                  pl.BlockSpec((1, d), const),
                  pl.BlockSpec((d, n_nat), const),
                  pl.BlockSpec((n_v, d), const),
                  pl.BlockSpec((d, LANES), const),
                  pl.BlockSpec((1, LANES), const)],
        out_specs=[pl.BlockSpec((1, tm, n_nat), lambda b, i: (b, i, 0)),
                   pl.BlockSpec((1, 1, n_v, tm), lambda b, i: (b, i, 0, 0)),
                   pl.BlockSpec((1, tm, LANES), lambda b, i: (b, i, 0))],
        out_shape=[jax.ShapeDtypeStruct((bsz, s, n_nat), BF16),
                   jax.ShapeDtypeStruct((bsz, s // tm, n_v, tm), BF16),
                   jax.ShapeDtypeStruct((bsz, s, LANES), F32)],
        compiler_params=_params("arbitrary", "arbitrary"),
        name="inproj",
    )(x, mod, g, w_nat, w_vt, w_gate, b_gate)


def _decay_kernel(lf_ref, k_ref, sel_ref, kaug_ref, carry_ref, *, n_heads):
    @pl.when(pl.program_id(1) == 0)
    def _():
        carry_ref[...] = jnp.zeros_like(carry_ref)

    tm = lf_ref.shape[1]
    lane = lax.broadcasted_iota(jnp.int32, (tm, LANES), 1)
    lf = jnp.where(lane < n_heads, lf_ref[0], 0.0)
    row = lax.broadcasted_iota(jnp.int32, (tm, tm), 0)
    col = lax.broadcasted_iota(jnp.int32, (tm, tm), 1)
    tri = (col <= row).astype(BF16)
    hi, mid, lo = _split3(lf)
    f_run = carry_ref[...] + (_dot(tri, hi) + _dot(tri, mid) + _dot(tri, lo))
    carry_ref[...] = f_run[tm - 1:tm, :]
    ghi, gmid, glo = _split3(-LOG2E * f_run)
    packed = (ghi.astype(F32) + pltpu.roll(gmid.astype(F32), n_heads, 1)
              + pltpu.roll(glo.astype(F32), 2 * n_heads, 1)).astype(BF16)
    placed = _dot(packed, sel_ref[...])
    k_all = k_ref[0]
    for h in range(n_heads):
        k_pair = k_all[:, (h // 2) * LANES:(h // 2 + 1) * LANES]
        own = (lane < HEAD_DIM) if h % 2 == 0 else (lane >= HEAD_DIM)
        kaug_ref[0, h] = jnp.where(own, k_pair, placed[:, h * LANES:(h + 1) * LANES].astype(BF16))


def _decay_select_matrix(n_heads):
    sel = np.zeros((LANES, n_heads * LANES), np.float32)
    for h in range(n_heads):
        base = h * LANES + (HEAD_DIM if h % 2 == 0 else 0)
        for term in range(3):
            sel[term * n_heads + h, base + term] = 1.0
    return jnp.asarray(sel, BF16)


def _decay(log_f, qk, n_heads, k_block):
    bsz, s, _ = log_f.shape
    tm = PROJ_ROWS
    d_grp = n_heads * HEAD_DIM
    return pl.pallas_call(
        functools.partial(_decay_kernel, n_heads=n_heads),
        grid=(bsz, s // tm),
        in_specs=[pl.BlockSpec((1, tm, LANES), lambda b, i: (b, i, 0)),
                  pl.BlockSpec((1, tm, d_grp), lambda b, i: (b, i, k_block)),
                  pl.BlockSpec((LANES, n_heads * LANES), lambda b, i: (0, 0))],
        out_specs=pl.BlockSpec((1, n_heads, tm, LANES), lambda b, i: (b, 0, i, 0)),
        out_shape=jax.ShapeDtypeStruct((bsz, n_heads, s, LANES), BF16),
        scratch_shapes=[pltpu.VMEM((1, LANES), F32)],
        compiler_params=_params("arbitrary", "arbitrary"),
        name="decay",
    )(log_f, qk, _decay_select_matrix(n_heads))


def _lane_queries(q_ref, extra_even, extra_odd, cw):
    q = q_ref[0].astype(F32) * (HEAD_DIM ** -0.5 * LOG2E)
    lane = lax.broadcasted_iota(jnp.int32, q.shape, 1)
    heads = (jnp.where(lane < HEAD_DIM, q, extra_even(lane)).T.astype(BF16),
             jnp.where(lane >= HEAD_DIM, q, extra_odd(lane)).T.astype(BF16))
    return [heads[hh][:, c * cw:(c + 1) * cw] for hh in range(2) for c in range(q.shape[0] // cw)]


def _visibility(first_key, first_query, bk, cw, strict):
    last_visible_gap = -1 if strict else 0
    if first_key + bk - 1 - first_query <= last_visible_gap:
        return "all"
    if first_key - (first_query + cw - 1) > last_visible_gap:
        return "none"
    gap = (lax.broadcasted_iota(jnp.int32, (bk, cw), 0) - lax.broadcasted_iota(jnp.int32, (bk, cw), 1)
           + (first_key - first_query))
    return gap <= last_visible_gap


def _diag_visibility(u, c, bk, cw, strict):
    return _visibility((1 - u) * bk, c * cw, bk, cw, strict)


def _hidden(visibility):
    return isinstance(visibility, str) and visibility == "none"


def _finish_heads(lanes, g_ref, o_ref):
    n_chunks = len(lanes) // 2
    outs = [jnp.concatenate(lanes[hh * n_chunks:(hh + 1) * n_chunks], axis=1) for hh in range(2)]
    normed = [o * lax.rsqrt(jnp.mean(o * o, axis=0, keepdims=True) + EPS) for o in outs]
    o_ref[0] = (jnp.concatenate(normed, axis=0).T * g_ref[...]).astype(o_ref.dtype)


def _fox_kernel(q_ref, k_ref, vt_ref, g_ref, o_ref, s_buf, cmax_buf, p_buf, acc_buf):
    qi = pl.program_id(2)
    bk = vt_ref.shape[3]
    n_lanes, cw = acc_buf.shape[0], acc_buf.shape[2]
    n_chunks = n_lanes // 2
    n_tiles = 2 * (qi + 1)
    ones3 = lambda lo: (lambda lane: jnp.where((lane >= lo) & (lane < lo + 3), 1.0, 0.0))
    queries = _lane_queries(q_ref, ones3(HEAD_DIM), ones3(0), cw)
    acc_buf[...] = jnp.zeros(acc_buf.shape, F32)
    for li in range(n_lanes):
        if _hidden(_diag_visibility(0, li % n_chunks, bk, cw, strict=False)):
            p_buf[0, li] = jnp.zeros((bk, cw), BF16)

    def step(t, slot, carry, score="below", softmax="below", value=True):
        new = []
        for li in range(n_lanes):
            hh, c = divmod(li, n_chunks)
            see = lambda u: "all" if u == "below" else _diag_visibility(u, c, bk, cw, strict=False)
            if score is not None and not _hidden(see(score)):
                start = pl.multiple_of((n_tiles - 2 - t) * bk, bk)
                s_new = _dot(k_ref[0, hh, pl.ds(start, bk), :], queries[li])
                if not isinstance(see(score), str):
                    s_new = jnp.where(see(score), s_new, MASKED)
                s_buf[1 - slot, li] = s_new
                cmax_buf[1 - slot, li] = jnp.max(s_new, axis=0, keepdims=True)
            pv = None
            if value:
                vt = vt_ref[0, n_tiles - t, hh * HEAD_DIM:(hh + 1) * HEAD_DIM, :]
                pv = _dot(vt, p_buf[1 - slot, li])
            m, l = carry[li]
            if softmax is not None and not _hidden(see(softmax)):
                m_new = jnp.maximum(m, cmax_buf[slot, li])
                alpha = jnp.exp2(m - m_new)
                p = jnp.exp2(s_buf[slot, li] - m_new)
                p_buf[slot, li] = p.astype(BF16)
                m, l = m_new, alpha * l + jnp.sum(p, axis=0, keepdims=True)
                acc_buf[li] = alpha * (acc_buf[li] if pv is None else acc_buf[li] + pv)
            elif pv is not None:
                acc_buf[li] += pv
            new.append((m, l))
        return tuple(new)

    def step_pair(i, carry):
        t = 2 * i + 1
        return step(t + 1, 0, step(t, 1, carry))

    carry = tuple((jnp.full((1, cw), M_INIT, F32), jnp.zeros((1, cw), F32)) for _ in range(n_lanes))
    carry = step(-1, 1, carry, score=0, softmax=None, value=False)
    carry = step(0, 0, carry, score=1, softmax=0, value=False)
    carry = lax.fori_loop(0, qi, step_pair, carry)
    carry = step(n_tiles - 1, 1, carry, score=None)
    carry = step(n_tiles, 0, carry, score=None, softmax=None)
    _finish_heads([acc_buf[li] / carry[li][1] for li in range(n_lanes)], g_ref, o_ref)


def _sb_kernel(q_ref, k_ref, vt_ref, g_ref, o_ref, z_buf, sp_buf, e_buf, wrow_buf, acc_buf):
    qi = pl.program_id(2)
    bk = vt_ref.shape[3]
    n_lanes, cw = acc_buf.shape[0], acc_buf.shape[2]
    n_chunks = n_lanes // 2
    n_tiles = 2 * (qi + 1)
    zero = lambda lane: 0.0
    queries = _lane_queries(q_ref, zero, zero, cw)
    suffix = (lax.broadcasted_iota(jnp.int32, (bk, bk), 1)
              >= lax.broadcasted_iota(jnp.int32, (bk, bk), 0)).astype(BF16)
    e_buf[1] = jnp.full(e_buf.shape[1:], MASKED, F32)
    wrow_buf[1] = jnp.zeros(wrow_buf.shape[1:], F32)
    acc_buf[...] = jnp.zeros(acc_buf.shape, F32)
    for li in range(n_lanes):
        if _hidden(_diag_visibility(0, li % n_chunks, bk, cw, strict=True)):
            z_buf[0, li] = jnp.full((bk, cw), MASKED, F32)
            sp_buf[0, li] = jnp.zeros((bk, cw), BF16)

    def step(t, slot, later, score="below", softplus="below", cumsum=True, weight=True):
        new_later = []
        for li in range(n_lanes):
            hh, c = divmod(li, n_chunks)
            see = lambda u: "all" if u == "below" else _diag_visibility(u, c, bk, cw, strict=True)
            if cumsum:
                within = _dot(suffix, sp_buf[1 - slot, li])
                e_buf[1 - slot, li] = z_buf[1 - slot, li] - within
                wrow_buf[1 - slot, li] = within[0:1, :]
            if score is not None and not _hidden(see(score)):
                start = pl.multiple_of((n_tiles - 1 - (t + 3)) * bk, bk)
                z_new = _dot(k_ref[0, pl.ds(start, bk), :], queries[li])
                if not isinstance(see(score), str):
                    z_new = jnp.where(see(score), z_new, MASKED)
                z_buf[1 - slot, li] = z_new
            if weight:
                a = jnp.exp2(e_buf[slot, li] - later[li])
                vt = vt_ref[0, n_tiles - 1 - jnp.maximum(t, 0), hh * HEAD_DIM:(hh + 1) * HEAD_DIM, :]
                acc_buf[li] += _dot(vt, a.astype(BF16))
                new_later.append(later[li] + wrow_buf[slot, li])
            else:
                new_later.append(later[li])
            if softplus is not None and not _hidden(see(softplus)):
                z = z_buf[slot, li]
                sp_buf[slot, li] = (jnp.maximum(z, 0.0) + jnp.log2(1.0 + jnp.exp2(-jnp.abs(z)))).astype(BF16)
        return tuple(new_later)

    def step_pair(i, later):
        t = 2 * i - 1
        return step(t + 1, 0, step(t, 1, later))

    later = tuple(jnp.zeros((1, cw), F32) for _ in range(n_lanes))
    later = step(-3, 1, later, score=0, softplus=None, cumsum=False, weight=False)
    later = step(-2, 0, later, score=1, softplus=0, cumsum=False, weight=False)
    later = lax.fori_loop(0, qi, step_pair, later)
    later = step(n_tiles - 3, 1, later, score=None)
    later = step(n_tiles - 2, 0, later, score=None, softplus=None)
    later = step(n_tiles - 1, 1, later, score=None, softplus=None, cumsum=False)
    _finish_heads([acc_buf[li] for li in range(n_lanes)], g_ref, o_ref)


def _attention(body, name, scratch, qk, k_arr, k_spec, vt, g, q_block0, vt_block0, n_heads):
    bsz, s, _ = qk.shape
    bq, bk = ATT_Q, ATT_K
    d_grp = n_heads * HEAD_DIM
    return pl.pallas_call(
        body,
        grid=(bsz, n_heads // 2, s // bq),
        in_specs=[pl.BlockSpec((1, bq, LANES), lambda b, p, i: (b, i, q_block0 + p)),
                  k_spec,
                  pl.BlockSpec((1, s // bk, LANES, bk), lambda b, p, i: (b, 0, vt_block0 + p, 0)),
                  pl.BlockSpec((1, LANES), lambda b, p, i: (0, p))],
        out_specs=pl.BlockSpec((1, bq, LANES), lambda b, p, i: (b, i, p)),
        out_shape=jax.ShapeDtypeStruct((bsz, s, d_grp), BF16),
        scratch_shapes=scratch,
        compiler_params=_params("arbitrary", "arbitrary", "arbitrary"),
        name=name,
    )(qk, k_arr, vt, g.reshape(1, d_grp))


ATT_LANE_GROUPS = 2 * (ATT_Q // ATT_COLS)


def _fox_scratch():
    n = ATT_LANE_GROUPS
    return [pltpu.VMEM((2, n, ATT_K, ATT_COLS), F32), pltpu.VMEM((2, n, 1, ATT_COLS), F32),
            pltpu.VMEM((2, n, ATT_K, ATT_COLS), BF16), pltpu.VMEM((n, HEAD_DIM, ATT_COLS), F32)]


def _sb_scratch():
    n = ATT_LANE_GROUPS
    return [pltpu.VMEM((2, n, ATT_K, ATT_COLS), F32), pltpu.VMEM((2, n, ATT_K, ATT_COLS), BF16),
            pltpu.VMEM((2, n, ATT_K, ATT_COLS), F32), pltpu.VMEM((2, n, 1, ATT_COLS), F32),
            pltpu.VMEM((n, HEAD_DIM, ATT_COLS), F32)]


def _outproj_kernel(x_ref, mf_ref, ms_ref, w_ref, mod_ref, g_ref, x1_ref, h2_ref):
    mix = jnp.concatenate([mf_ref[0], ms_ref[0]], axis=-1)
    x1 = x_ref[0] + mod_ref[0, 2:3, :] * _dot(mix, w_ref[...])
    x1_ref[0] = x1
    shift = mod_ref[0, 3:4, :]
    scale = mod_ref[0, 4:5, :]
    h2_ref[0] = (_rms_rows(x1) * g_ref[...] * (1.0 + scale) + shift).astype(BF16)


def _outproj(x, mix_f, mix_s, w_out, mod, g):
    bsz, s, d = x.shape
    tm = OUT_ROWS
    row = lambda b, i: (b, i, 0)
    return pl.pallas_call(
        _outproj_kernel,
        grid=(bsz, s // tm),
        in_specs=[pl.BlockSpec((1, tm, d), row),
                  pl.BlockSpec((1, tm, mix_f.shape[2]), row),
                  pl.BlockSpec((1, tm, mix_s.shape[2]), row),
                  pl.BlockSpec(w_out.shape, lambda b, i: (0, 0)),
                  pl.BlockSpec((1, N_MOD, d), lambda b, i: (b, 0, 0)),
                  pl.BlockSpec((1, d), lambda b, i: (0, 0))],
        out_specs=[pl.BlockSpec((1, tm, d), row), pl.BlockSpec((1, tm, d), row)],
        out_shape=[jax.ShapeDtypeStruct((bsz, s, d), F32), jax.ShapeDtypeStruct((bsz, s, d), BF16)],
        compiler_params=_params("arbitrary", "arbitrary"),
        name="outproj",
    )(x, mix_f, mix_s, w_out, mod, g)


def _mlp_kernel(h_ref, halo_ref, x1_ref, mod_ref, wg_ref, wv_ref, cwg_ref, cbg_ref, cwv_ref, cbv_ref,
                wd_ref, gf_ref, o_ref, acc_ref, *, final_norm):
    i = pl.program_id(1)
    c = pl.program_id(2)
    tm = h_ref.shape[1]
    halo = halo_ref[0]
    halo = jnp.where(i > 0, halo, jnp.zeros_like(halo))
    hx = jnp.concatenate([halo, h_ref[0]], axis=0)

    def conv_branch(w_ref, cw_ref, cb_ref):
        u = _dot(hx, w_ref[...])
        out = cb_ref[...]
        for tap in range(CONV_WIDTH):
            lag = CONV_WIDTH - 1 - tap
            first = BF16_SUBLANES - lag
            out = out + cw_ref[tap:tap + 1, :] * u[first:first + tm, :]
        return out

    u_gate = conv_branch(wg_ref, cwg_ref, cbg_ref)
    u_val = conv_branch(wv_ref, cwv_ref, cbv_ref)
    part = _dot((u_gate * jax.nn.sigmoid(u_gate) * u_val).astype(BF16), wd_ref[...])

    @pl.when(c == 0)
    def _():
        acc_ref[...] = part

    @pl.when(c > 0)
    def _():
        acc_ref[...] += part

    @pl.when(c == pl.num_programs(2) - 1)
    def _():
        x2 = x1_ref[0] + mod_ref[0, 5:6, :] * acc_ref[...]
        o_ref[0] = _rms_rows(x2) * gf_ref[...] if final_norm else x2


def _mlp(h2, x1, mod, w_gate, w_val, cw_gate, cb_gate, cw_val, cb_val, w_down, g_final, final_norm):
    bsz, s, d = x1.shape
    tm, tf = OUT_ROWS, FF_CHUNK
    halo_blocks = tm // BF16_SUBLANES
    row = lambda b, i, c: (b, i, 0)
    col = lambda b, i, c: (0, c)
    return pl.pallas_call(
        functools.partial(_mlp_kernel, final_norm=final_norm),
        grid=(bsz, s // tm, w_gate.shape[1] // tf),
        in_specs=[pl.BlockSpec((1, tm, d), row),
                  pl.BlockSpec((1, BF16_SUBLANES, d),
                               lambda b, i, c: (b, jnp.maximum(i * halo_blocks - 1, 0), 0)),
                  pl.BlockSpec((1, tm, d), row),
                  pl.BlockSpec((1, N_MOD, d), lambda b, i, c: (b, 0, 0)),
                  pl.BlockSpec((d, tf), col), pl.BlockSpec((d, tf), col),
                  pl.BlockSpec((CONV_WIDTH, tf), col), pl.BlockSpec((1, tf), col),
                  pl.BlockSpec((CONV_WIDTH, tf), col), pl.BlockSpec((1, tf), col),
                  pl.BlockSpec((tf, d), lambda b, i, c: (c, 0)),
                  pl.BlockSpec((1, d), lambda b, i, c: (0, 0))],
        out_specs=pl.BlockSpec((1, tm, d), row),
        out_shape=jax.ShapeDtypeStruct((bsz, s, d), F32),
        scratch_shapes=[pltpu.VMEM((tm, d), F32)],
        compiler_params=_params("arbitrary", "arbitrary", "arbitrary"),
        name="mlp",
    )(h2, h2, x1, mod, w_gate, w_val, cw_gate, cb_gate, cw_val, cb_val, w_down, g_final)


def _pad_cols(a, n):
    return jnp.pad(a, ((0, 0), (0, n - a.shape[1])))


def kernel(x, c, w_ada, b_ada, g_attn, w_in, b_fgate, g_out_fox, g_out_sb, w_out,
           g_mlp, w_up, conv_w, conv_b, w_down, g_final):
    depth, d, _ = w_ada.shape
    n_fox = b_fgate.shape[1]
    d_fox = n_fox * HEAD_DIM
    d_sb = g_out_sb.shape[1]
    n_sb = d_sb // HEAD_DIM
    d_ff = w_down.shape[1]
    d_ff_pad = -(-d_ff // FF_CHUNK) * FF_CHUNK
    assert n_fox % 2 == 0 and n_sb % 2 == 0 and 3 * n_fox <= LANES
    assert x.shape[1] % OUT_ROWS == 0 and x.shape[1] % ATT_Q == 0 and ATT_Q == 2 * ATT_K
    o_kf, o_vf, o_qs, o_ks, o_vs, o_gate = (d_fox, 2 * d_fox, 3 * d_fox, 3 * d_fox + d_sb,
                                             3 * d_fox + 2 * d_sb, 3 * d_fox + 3 * d_sb)

    for l in range(depth):
        mod = _ada(c, w_ada[l], b_ada[l]).reshape(-1, N_MOD, d)
        w = w_in[l]
        w_nat = jnp.concatenate([w[:, :o_vf], w[:, o_qs:o_vs]], axis=1).astype(BF16)
        w_vt = jnp.concatenate([w[:, o_vf:o_qs], w[:, o_vs:o_gate]], axis=1).T.astype(BF16)
        w_gate = _pad_cols(w[:, o_gate:], LANES).astype(BF16)
        b_gate = _pad_cols(b_fgate[l].reshape(1, n_fox), LANES)
        qk, vt, log_f = _inproj(x, mod, g_attn[l].reshape(1, d), w_nat, w_vt, w_gate, b_gate)

        pairs_f, pairs_s = n_fox // 2, n_sb // 2
        k_aug = _decay(log_f, qk, n_fox, k_block=1)
        fox_k_spec = pl.BlockSpec((1, 2, x.shape[1], LANES), lambda b, p, i: (b, p, 0, 0))
        mix_f = _attention(_fox_kernel, "fox", _fox_scratch(), qk, k_aug, fox_k_spec, vt, g_out_fox[l],
                           q_block0=0, vt_block0=0, n_heads=n_fox)
        sb_k_spec = pl.BlockSpec((1, x.shape[1], LANES), lambda b, p, i: (b, 0, 2 * pairs_f + pairs_s + p))
        mix_s = _attention(_sb_kernel, "sb", _sb_scratch(), qk, qk, sb_k_spec, vt, g_out_sb[l],
                           q_block0=2 * pairs_f, vt_block0=pairs_f, n_heads=n_sb)

        x1, h2 = _outproj(x, mix_f, mix_s, w_out[l].astype(BF16), mod, g_mlp[l].reshape(1, d))

        wu, cw, cb = w_up[l], conv_w[l], conv_b[l].reshape(1, -1)
        x = _mlp(h2, x1, mod,
                 _pad_cols(wu[:, :d_ff], d_ff_pad).astype(BF16), _pad_cols(wu[:, d_ff:], d_ff_pad).astype(BF16),
                 _pad_cols(cw[:, :d_ff], d_ff_pad), _pad_cols(cb[:, :d_ff], d_ff_pad),
                 _pad_cols(cw[:, d_ff:], d_ff_pad), _pad_cols(cb[:, d_ff:], d_ff_pad),
                 jnp.pad(w_down[l], ((0, d_ff_pad - d_ff), (0, 0))).astype(BF16),
                 g_final.reshape(1, d), final_norm=(l == depth - 1))
    return x
```

```python
import functools

import numpy as np
import jax
import jax.numpy as jnp
from jax import lax
from jax.experimental import pallas as pl
from jax.experimental.pallas import tpu as pltpu

HEAD_DIM = 64
N_MOD = 6
CONV_WIDTH = 3
EPS = 1e-6

LANES = 128
BF16_SUBLANES = 16
VMEM_LIMIT_BYTES = 48 * 1024 * 1024

ATT_Q = 512
ATT_K = 256
ATT_COLS = 256
PROJ_ROWS = ATT_K
LOG2E = 1.4426950408889634
MASKED = -1e30
M_INIT = -1e29
PRUNE_LOG2 = 160.0
NORM_SLACK = 1.02
STATS_ROWS = 8
OUT_ROWS = 512
FF_CHUNK = 256

F32 = jnp.float32
BF16 = jnp.bfloat16
NT_DIMS = (((1,), (1,)), ((), ()))


def _dot(a, b):
    return jnp.dot(a, b, preferred_element_type=F32)


def _dot_nt(a, b):
    return lax.dot_general(a, b, NT_DIMS, preferred_element_type=F32)


def _params(*sem):
    return pltpu.CompilerParams(dimension_semantics=sem, vmem_limit_bytes=VMEM_LIMIT_BYTES)


def _rms_rows(x):
    return x * lax.rsqrt(jnp.mean(x * x, axis=-1, keepdims=True) + EPS)


def _softplus(z):
    return jnp.maximum(z, 0.0) + jnp.log(1.0 + jnp.exp(-jnp.abs(z)))


def _split3(x):
    hi = x.astype(BF16)
    r1 = x - hi.astype(F32)
    mid = r1.astype(BF16)
    lo = (r1 - mid.astype(F32)).astype(BF16)
    return hi, mid, lo


def _ada_kernel(c_ref, w_ref, b_ref, o_ref):
    c = c_ref[...]
    o_ref[...] = _dot(c * jax.nn.sigmoid(c), w_ref[...]) + b_ref[...]


def _ada(c, w, b):
    bsz, d = c.shape
    n = w.shape[1]
    return pl.pallas_call(
        _ada_kernel,
        grid=(n // d,),
        in_specs=[pl.BlockSpec((bsz, d), lambda j: (0, 0)),
                  pl.BlockSpec((d, d), lambda j: (0, j)),
                  pl.BlockSpec((1, d), lambda j: (0, j))],
        out_specs=pl.BlockSpec((bsz, d), lambda j: (0, j)),
        out_shape=jax.ShapeDtypeStruct((bsz, n), F32),
        compiler_params=_params("arbitrary"),
        name="ada",
    )(c, w, b.reshape(1, n))


def _inproj_kernel(x_ref, mod_ref, g_ref, wn_ref, wvt_ref, wg_ref, bg_ref, qk_ref, vt_ref, lf_ref):
    shift = mod_ref[0, 0:1, :]
    scale = mod_ref[0, 1:2, :]
    h = (_rms_rows(x_ref[0]) * g_ref[...] * (1.0 + scale) + shift).astype(BF16)
    qk_ref[0] = _dot(h, wn_ref[...]).astype(BF16)
    vt_ref[0, 0] = _dot_nt(wvt_ref[...], h).astype(BF16)
    logit = _dot(h, wg_ref[...]) + bg_ref[...]
    lf_ref[0] = -_softplus(-logit)


def _inproj(x, mod, g, w_nat, w_vt, w_gate, b_gate):
    bsz, s, d = x.shape
    tm = PROJ_ROWS
    n_nat, n_v = w_nat.shape[1], w_vt.shape[0]
    const = lambda b, i: (0, 0)
    return pl.pallas_call(
        _inproj_kernel,
        grid=(bsz, s // tm),
        in_specs=[pl.BlockSpec((1, tm, d), lambda b, i: (b, i, 0)),
                  pl.BlockSpec((1, N_MOD, d), lambda b, i: (b, 0, 0)),
                  pl.BlockSpec((1, d), const),
                  pl.BlockSpec((d, n_nat), const),
                  pl.BlockSpec((n_v, d), const),
                  pl.BlockSpec((d, LANES), const),
                  pl.BlockSpec((1, LANES), const)],
        out_specs=[pl.BlockSpec((1, tm, n_nat), lambda b, i: (b, i, 0)),
                   pl.BlockSpec((1, 1, n_v, tm), lambda b, i: (b, i, 0, 0)),
                   pl.BlockSpec((1, tm, LANES), lambda b, i: (b, i, 0))],
        out_shape=[jax.ShapeDtypeStruct((bsz, s, n_nat), BF16),
                   jax.ShapeDtypeStruct((bsz, s // tm, n_v, tm), BF16),
                   jax.ShapeDtypeStruct((bsz, s, LANES), F32)],
        compiler_params=_params("arbitrary", "arbitrary"),
        name="inproj",
    )(x, mod, g, w_nat, w_vt, w_gate, b_gate)


def _decay_kernel(lf_ref, k_ref, ks_ref, sel_ref, ind_ref, kaug_ref, stats_ref, carry_ref, kpre_ref, *, n_heads):
    @pl.when(pl.program_id(1) == 0)
    def _():
        carry_ref[...] = jnp.zeros_like(carry_ref)
        kpre_ref[...] = jnp.zeros_like(kpre_ref)

    tm = lf_ref.shape[1]
    lane = lax.broadcasted_iota(jnp.int32, (tm, LANES), 1)
    lf = jnp.where(lane < n_heads, lf_ref[0], 0.0)
    row = lax.broadcasted_iota(jnp.int32, (tm, tm), 0)
    col = lax.broadcasted_iota(jnp.int32, (tm, tm), 1)
    tri = (col <= row).astype(BF16)
    hi, mid, lo = _split3(lf)
    f_run = carry_ref[...] + (_dot(tri, hi) + _dot(tri, mid) + _dot(tri, lo))
    carry_ref[...] = f_run[tm - 1:tm, :]
    ghi, gmid, glo = _split3(-LOG2E * f_run)
    packed = (ghi.astype(F32) + pltpu.roll(gmid.astype(F32), n_heads, 1)
              + pltpu.roll(glo.astype(F32), 2 * n_heads, 1)).astype(BF16)
    placed = _dot(packed, sel_ref[...])
    k_all = k_ref[0]
    for h in range(n_heads):
        k_pair = k_all[:, (h // 2) * LANES:(h // 2 + 1) * LANES]
        own = (lane < HEAD_DIM) if h % 2 == 0 else (lane >= HEAD_DIM)
        kaug_ref[0, h] = jnp.where(own, k_pair, placed[:, h * LANES:(h + 1) * LANES].astype(BF16))

    def head_norm_bound(k):
        k32 = k.astype(F32)
        sq = _dot((k32 * k32).astype(BF16), ind_ref[...])
        return jnp.sqrt(jnp.max(sq, axis=0, keepdims=True) * NORM_SLACK)

    kpre_f = jnp.maximum(kpre_ref[0:1, :], head_norm_bound(k_all))
    kpre_s = jnp.maximum(kpre_ref[1:2, :], head_norm_bound(ks_ref[0]))
    kpre_ref[0:1, :] = kpre_f
    kpre_ref[1:2, :] = kpre_s
    g_end = -LOG2E * f_run[tm - 1:tm, :]
    lane1 = lax.broadcasted_iota(jnp.int32, (1, LANES), 1)

    def spread(v, h):
        return jnp.broadcast_to(jnp.sum(jnp.where(lane1 == h, v, 0.0), axis=1, keepdims=True), (1, ATT_COLS))

    for p in range(n_heads // 2):
        rows = [spread(v, 2 * p + hh) for v in (kpre_f, g_end, kpre_s) for hh in range(2)]
        rows += [jnp.zeros((1, ATT_COLS), F32)] * (stats_ref.shape[3] - len(rows))
        stats_ref[0, p, 0] = jnp.concatenate(rows, axis=0)


def _head_indicator(n_heads):
    ind = np.zeros((n_heads * HEAD_DIM, LANES), np.float32)
    ind[np.arange(n_heads * HEAD_DIM), np.arange(n_heads * HEAD_DIM) // HEAD_DIM] = 1.0
    return jnp.asarray(ind, BF16)


def _decay_select_matrix(n_heads):
    sel = np.zeros((LANES, n_heads * LANES), np.float32)
    for h in range(n_heads):
        base = h * LANES + (HEAD_DIM if h % 2 == 0 else 0)
        for term in range(3):
            sel[term * n_heads + h, base + term] = 1.0
    return jnp.asarray(sel, BF16)


def _decay(log_f, qk, n_heads, k_fox_block, k_sb_block):
    bsz, s, _ = log_f.shape
    tm = PROJ_ROWS
    d_grp = n_heads * HEAD_DIM
    return pl.pallas_call(
        functools.partial(_decay_kernel, n_heads=n_heads),
        grid=(bsz, s // tm),
        in_specs=[pl.BlockSpec((1, tm, LANES), lambda b, i: (b, i, 0)),
                  pl.BlockSpec((1, tm, d_grp), lambda b, i: (b, i, k_fox_block)),
                  pl.BlockSpec((1, tm, d_grp), lambda b, i: (b, i, k_sb_block)),
                  pl.BlockSpec((LANES, n_heads * LANES), lambda b, i: (0, 0)),
                  pl.BlockSpec((d_grp, LANES), lambda b, i: (0, 0))],
        out_specs=[pl.BlockSpec((1, n_heads, tm, LANES), lambda b, i: (b, 0, i, 0)),
                   pl.BlockSpec((1, n_heads // 2, 1, STATS_ROWS, ATT_COLS), lambda b, i: (b, 0, i, 0, 0))],
        out_shape=[jax.ShapeDtypeStruct((bsz, n_heads, s, LANES), BF16),
                   jax.ShapeDtypeStruct((bsz, n_heads // 2, s // tm, STATS_ROWS, ATT_COLS), F32)],
        scratch_shapes=[pltpu.VMEM((1, LANES), F32), pltpu.VMEM((2, LANES), F32)],
        compiler_params=_params("arbitrary", "arbitrary"),
        name="decay",
    )(log_f, qk, qk, _decay_select_matrix(n_heads), _head_indicator(n_heads))


def _lane_queries(q_ref, extra_even, extra_odd, cw):
    q = q_ref[0].astype(F32) * (HEAD_DIM ** -0.5 * LOG2E)
    lane = lax.broadcasted_iota(jnp.int32, q.shape, 1)
    heads = (jnp.where(lane < HEAD_DIM, q, extra_even(lane)).T.astype(BF16),
             jnp.where(lane >= HEAD_DIM, q, extra_odd(lane)).T.astype(BF16))
    return [heads[hh][:, c * cw:(c + 1) * cw] for hh in range(2) for c in range(q.shape[0] // cw)]


def _visibility(first_key, first_query, bk, cw, strict):
    last_visible_gap = -1 if strict else 0
    if first_key + bk - 1 - first_query <= last_visible_gap:
        return "all"
    if first_key - (first_query + cw - 1) > last_visible_gap:
        return "none"
    gap = (lax.broadcasted_iota(jnp.int32, (bk, cw), 0) - lax.broadcasted_iota(jnp.int32, (bk, cw), 1)
           + (first_key - first_query))
    return gap <= last_visible_gap


def _diag_visibility(u, c, bk, cw, strict):
    return _visibility((1 - u) * bk, c * cw, bk, cw, strict)


def _hidden(visibility):
    return isinstance(visibility, str) and visibility == "none"


def _query_norm_bounds(queries, n_chunks):
    bounds = []
    for li, q in enumerate(queries):
        hh = li // n_chunks
        own = q[hh * HEAD_DIM:(hh + 1) * HEAD_DIM, :].astype(F32)
        bounds.append(jnp.sqrt(jnp.sum(own * own, axis=0, keepdims=True) * NORM_SLACK))
    return bounds


def _finish_heads(lanes, g_ref, o_ref):
    n_chunks = len(lanes) // 2
    outs = [jnp.concatenate(lanes[hh * n_chunks:(hh + 1) * n_chunks], axis=1) for hh in range(2)]
    normed = [o * lax.rsqrt(jnp.mean(o * o, axis=0, keepdims=True) + EPS) for o in outs]
    o_ref[0] = (jnp.concatenate(normed, axis=0).T * g_ref[...]).astype(o_ref.dtype)


def _fox_kernel(q_ref, k_ref, vt_ref, g_ref, stats_ref, o_ref, s_buf, cmax_buf, p_buf, acc_buf):
    qi = pl.program_id(2)
    bk = vt_ref.shape[3]
    n_lanes, cw = acc_buf.shape[0], acc_buf.shape[2]
    n_chunks = n_lanes // 2
    n_tiles = 2 * (qi + 1)
    ones3 = lambda lo: (lambda lane: jnp.where((lane >= lo) & (lane < lo + 3), 1.0, 0.0))
    queries = _lane_queries(q_ref, ones3(HEAD_DIM), ones3(0), cw)
    acc_buf[...] = jnp.zeros(acc_buf.shape, F32)
    for li in range(n_lanes):
        if _hidden(_diag_visibility(0, li % n_chunks, bk, cw, strict=False)):
            p_buf[0, li] = jnp.zeros((bk, cw), BF16)

    def step(t, slot, carry, score="below", softmax="below", value=True):
        new = []
        for li in range(n_lanes):
            hh, c = divmod(li, n_chunks)
            see = lambda u: "all" if u == "below" else _diag_visibility(u, c, bk, cw, strict=False)
            if score is not None and not _hidden(see(score)):
                start = pl.multiple_of((n_tiles - 2 - t) * bk, bk)
                s_new = _dot(k_ref[0, hh, pl.ds(start, bk), :], queries[li])
                if not isinstance(see(score), str):
                    s_new = jnp.where(see(score), s_new, MASKED)
                s_buf[1 - slot, li] = s_new
                cmax_buf[1 - slot, li] = jnp.max(s_new, axis=0, keepdims=True)
            pv = None
            if value:
                vt = vt_ref[0, n_tiles - t, hh * HEAD_DIM:(hh + 1) * HEAD_DIM, :]
                pv = _dot(vt, p_buf[1 - slot, li])
            m, l = carry[li]
            if softmax is not None and not _hidden(see(softmax)):
                m_new = jnp.maximum(m, cmax_buf[slot, li])
                alpha = jnp.exp2(m - m_new)
                p = jnp.exp2(s_buf[slot, li] - m_new)
                p_buf[slot, li] = p.astype(BF16)
                m, l = m_new, alpha * l + jnp.sum(p, axis=0, keepdims=True)
                acc_buf[li] = alpha * (acc_buf[li] if pv is None else acc_buf[li] + pv)
            elif pv is not None:
                acc_buf[li] += pv
            new.append((m, l))
        return tuple(new)

    def step_pair(i, carry):
        t = 2 * i + 1
        return step(t + 1, 0, step(t, 1, carry))

    q_norm = _query_norm_bounds(queries, n_chunks)

    def later_tiles_matter(i, carry):
        j_rest = jnp.maximum(n_tiles - 5 - 2 * i, 0)
        worst = None
        for li in range(n_lanes):
            hh = li // n_chunks
            bound = (q_norm[li] * stats_ref[0, 0, j_rest, hh:hh + 1, :]
                     + stats_ref[0, 0, j_rest, 2 + hh:3 + hh, :] - carry[li][0])
            worst = bound if worst is None else jnp.maximum(worst, bound)
        return jnp.max(worst) >= -PRUNE_LOG2

    def pair_and_check(state):
        i, _, carry = state
        return i + 1, later_tiles_matter(i, carry), step_pair(i, carry)

    carry = tuple((jnp.full((1, cw), M_INIT, F32), jnp.zeros((1, cw), F32)) for _ in range(n_lanes))
    carry = step(-1, 1, carry, score=0, softmax=None, value=False)
    carry = step(0, 0, carry, score=1, softmax=0, value=False)
    n_pairs, _, carry = lax.while_loop(lambda st: (st[0] < qi) & st[1], pair_and_check,
                                       (jnp.int32(0), jnp.bool_(True), carry))
    carry = step(2 * n_pairs + 1, 1, carry, score=None)
    carry = step(2 * n_pairs + 2, 0, carry, score=None, softmax=None)
    _finish_heads([acc_buf[li] / carry[li][1] for li in range(n_lanes)], g_ref, o_ref)


def _sb_kernel(q_ref, k_ref, vt_ref, g_ref, stats_ref, o_ref, z_buf, sp_buf, e_buf, wrow_buf, acc_buf):
    qi = pl.program_id(2)
    bk = vt_ref.shape[3]
    n_lanes, cw = acc_buf.shape[0], acc_buf.shape[2]
    n_chunks = n_lanes // 2
    n_tiles = 2 * (qi + 1)
    zero = lambda lane: 0.0
    queries = _lane_queries(q_ref, zero, zero, cw)
    suffix = (lax.broadcasted_iota(jnp.int32, (bk, bk), 1)
              >= lax.broadcasted_iota(jnp.int32, (bk, bk), 0)).astype(BF16)
    e_buf[1] = jnp.full(e_buf.shape[1:], MASKED, F32)
    wrow_buf[1] = jnp.zeros(wrow_buf.shape[1:], F32)
    acc_buf[...] = jnp.zeros(acc_buf.shape, F32)
    for li in range(n_lanes):
        if _hidden(_diag_visibility(0, li % n_chunks, bk, cw, strict=True)):
            z_buf[0, li] = jnp.full((bk, cw), MASKED, F32)
            sp_buf[0, li] = jnp.zeros((bk, cw), BF16)

    def step(t, slot, later, score="below", softplus="below", cumsum=True, weight=True):
        new_later = []
        for li in range(n_lanes):
            hh, c = divmod(li, n_chunks)
            see = lambda u: "all" if u == "below" else _diag_visibility(u, c, bk, cw, strict=True)
            if cumsum:
                within = _dot(suffix, sp_buf[1 - slot, li])
                e_buf[1 - slot, li] = z_buf[1 - slot, li] - within
                wrow_buf[1 - slot, li] = within[0:1, :]
            if score is not None and not _hidden(see(score)):
                start = pl.multiple_of((n_tiles - 1 - (t + 3)) * bk, bk)
                z_new = _dot(k_ref[0, pl.ds(start, bk), :], queries[li])
                if not isinstance(see(score), str):
                    z_new = jnp.where(see(score), z_new, MASKED)
                z_buf[1 - slot, li] = z_new
            if weight:
                a = jnp.exp2(e_buf[slot, li] - later[li])
                vt = vt_ref[0, n_tiles - 1 - jnp.maximum(t, 0), hh * HEAD_DIM:(hh + 1) * HEAD_DIM, :]
                acc_buf[li] += _dot(vt, a.astype(BF16))
                new_later.append(later[li] + wrow_buf[slot, li])
            else:
                new_later.append(later[li])
            if softplus is not None and not _hidden(see(softplus)):
                z = z_buf[slot, li]
                sp_buf[slot, li] = (jnp.maximum(z, 0.0) + jnp.log2(1.0 + jnp.exp2(-jnp.abs(z)))).astype(BF16)
        return tuple(new_later)

    def step_pair(i, later):
        t = 2 * i - 1
        return step(t + 1, 0, step(t, 1, later))

    q_norm = _query_norm_bounds(queries, n_chunks)

    def later_tiles_matter(i, later):
        j_rest = jnp.maximum(n_tiles - 5 - 2 * i, 0)
        worst = None
        for li in range(n_lanes):
            hh = li // n_chunks
            bound = q_norm[li] * stats_ref[0, 0, j_rest, 4 + hh:5 + hh, :] - later[li]
            worst = bound if worst is None else jnp.maximum(worst, bound)
        return jnp.max(worst) >= -PRUNE_LOG2

    def pair_and_check(state):
        i, _, later = state
        return i + 1, later_tiles_matter(i, later), step_pair(i, later)

    later = tuple(jnp.zeros((1, cw), F32) for _ in range(n_lanes))
    later = step(-3, 1, later, score=0, softplus=None, cumsum=False, weight=False)
    later = step(-2, 0, later, score=1, softplus=0, cumsum=False, weight=False)
    n_pairs, _, later = lax.while_loop(lambda st: (st[0] < qi) & st[1], pair_and_check,
                                       (jnp.int32(0), jnp.bool_(True), later))
    later = step(2 * n_pairs - 1, 1, later, score=None)
    later = step(2 * n_pairs, 0, later, score=None, softplus=None)
    later = step(2 * n_pairs + 1, 1, later, score=None, softplus=None, cumsum=False)
    _finish_heads([acc_buf[li] for li in range(n_lanes)], g_ref, o_ref)


def _attention(body, name, scratch, qk, k_arr, k_spec, vt, g, stats, q_block0, vt_block0, n_heads):
    bsz, s, _ = qk.shape
    bq, bk = ATT_Q, ATT_K
    d_grp = n_heads * HEAD_DIM
    return pl.pallas_call(
        body,
        grid=(bsz, n_heads // 2, s // bq),
        in_specs=[pl.BlockSpec((1, bq, LANES), lambda b, p, i: (b, i, q_block0 + p)),
                  k_spec,
                  pl.BlockSpec((1, s // bk, LANES, bk), lambda b, p, i: (b, 0, vt_block0 + p, 0)),
                  pl.BlockSpec((1, LANES), lambda b, p, i: (0, p)),
                  pl.BlockSpec((1, 1) + stats.shape[2:], lambda b, p, i: (b, p, 0, 0, 0))],
        out_specs=pl.BlockSpec((1, bq, LANES), lambda b, p, i: (b, i, p)),
        out_shape=jax.ShapeDtypeStruct((bsz, s, d_grp), BF16),
        scratch_shapes=scratch,
        compiler_params=_params("arbitrary", "arbitrary", "arbitrary"),
        name=name,
    )(qk, k_arr, vt, g.reshape(1, d_grp), stats)


ATT_LANE_GROUPS = 2 * (ATT_Q // ATT_COLS)


def _fox_scratch():
    n = ATT_LANE_GROUPS
    return [pltpu.VMEM((2, n, ATT_K, ATT_COLS), F32), pltpu.VMEM((2, n, 1, ATT_COLS), F32),
            pltpu.VMEM((2, n, ATT_K, ATT_COLS), BF16), pltpu.VMEM((n, HEAD_DIM, ATT_COLS), F32)]


def _sb_scratch():
    n = ATT_LANE_GROUPS
    return [pltpu.VMEM((2, n, ATT_K, ATT_COLS), F32), pltpu.VMEM((2, n, ATT_K, ATT_COLS), BF16),
            pltpu.VMEM((2, n, ATT_K, ATT_COLS), F32), pltpu.VMEM((2, n, 1, ATT_COLS), F32),
            pltpu.VMEM((n, HEAD_DIM, ATT_COLS), F32)]


def _outproj_kernel(x_ref, mf_ref, ms_ref, w_ref, mod_ref, g_ref, x1_ref, h2_ref):
    mix = jnp.concatenate([mf_ref[0], ms_ref[0]], axis=-1)
    x1 = x_ref[0] + mod_ref[0, 2:3, :] * _dot(mix, w_ref[...])
    x1_ref[0] = x1
    shift = mod_ref[0, 3:4, :]
    scale = mod_ref[0, 4:5, :]
    h2_ref[0] = (_rms_rows(x1) * g_ref[...] * (1.0 + scale) + shift).astype(BF16)


def _outproj(x, mix_f, mix_s, w_out, mod, g):
    bsz, s, d = x.shape
    tm = OUT_ROWS
    row = lambda b, i: (b, i, 0)
    return pl.pallas_call(
        _outproj_kernel,
        grid=(bsz, s // tm),
        in_specs=[pl.BlockSpec((1, tm, d), row),
                  pl.BlockSpec((1, tm, mix_f.shape[2]), row),
                  pl.BlockSpec((1, tm, mix_s.shape[2]), row),
                  pl.BlockSpec(w_out.shape, lambda b, i: (0, 0)),
                  pl.BlockSpec((1, N_MOD, d), lambda b, i: (b, 0, 0)),
                  pl.BlockSpec((1, d), lambda b, i: (0, 0))],
        out_specs=[pl.BlockSpec((1, tm, d), row), pl.BlockSpec((1, tm, d), row)],
        out_shape=[jax.ShapeDtypeStruct((bsz, s, d), F32), jax.ShapeDtypeStruct((bsz, s, d), BF16)],
        compiler_params=_params("arbitrary", "arbitrary"),
        name="outproj",
    )(x, mix_f, mix_s, w_out, mod, g)


def _mlp_kernel(h_ref, halo_ref, x1_ref, mod_ref, wg_ref, wv_ref, cwg_ref, cbg_ref, cwv_ref, cbv_ref,
                wd_ref, gf_ref, o_ref, acc_ref, *, final_norm):
    i = pl.program_id(1)
    c = pl.program_id(2)
    tm = h_ref.shape[1]
    halo = halo_ref[0]
    halo = jnp.where(i > 0, halo, jnp.zeros_like(halo))
    hx = jnp.concatenate([halo, h_ref[0]], axis=0)

    def conv_branch(w_ref, cw_ref, cb_ref):
        u = _dot(hx, w_ref[...])
        out = cb_ref[...]
        for tap in range(CONV_WIDTH):
            lag = CONV_WIDTH - 1 - tap
            first = BF16_SUBLANES - lag
            out = out + cw_ref[tap:tap + 1, :] * u[first:first + tm, :]
        return out

    u_gate = conv_branch(wg_ref, cwg_ref, cbg_ref)
    u_val = conv_branch(wv_ref, cwv_ref, cbv_ref)
    part = _dot((u_gate * jax.nn.sigmoid(u_gate) * u_val).astype(BF16), wd_ref[...])

    @pl.when(c == 0)
    def _():
        acc_ref[...] = part

    @pl.when(c > 0)
    def _():
        acc_ref[...] += part

    @pl.when(c == pl.num_programs(2) - 1)
    def _():
        x2 = x1_ref[0] + mod_ref[0, 5:6, :] * acc_ref[...]
        o_ref[0] = _rms_rows(x2) * gf_ref[...] if final_norm else x2


def _mlp(h2, x1, mod, w_gate, w_val, cw_gate, cb_gate, cw_val, cb_val, w_down, g_final, final_norm):
    bsz, s, d = x1.shape
    tm, tf = OUT_ROWS, FF_CHUNK
    halo_blocks = tm // BF16_SUBLANES
    row = lambda b, i, c: (b, i, 0)
    col = lambda b, i, c: (0, c)
    return pl.pallas_call(
        functools.partial(_mlp_kernel, final_norm=final_norm),
        grid=(bsz, s // tm, w_gate.shape[1] // tf),
        in_specs=[pl.BlockSpec((1, tm, d), row),
                  pl.BlockSpec((1, BF16_SUBLANES, d),
                               lambda b, i, c: (b, jnp.maximum(i * halo_blocks - 1, 0), 0)),
                  pl.BlockSpec((1, tm, d), row),
                  pl.BlockSpec((1, N_MOD, d), lambda b, i, c: (b, 0, 0)),
                  pl.BlockSpec((d, tf), col), pl.BlockSpec((d, tf), col),
                  pl.BlockSpec((CONV_WIDTH, tf), col), pl.BlockSpec((1, tf), col),
                  pl.BlockSpec((CONV_WIDTH, tf), col), pl.BlockSpec((1, tf), col),
                  pl.BlockSpec((tf, d), lambda b, i, c: (c, 0)),
                  pl.BlockSpec((1, d), lambda b, i, c: (0, 0))],
        out_specs=pl.BlockSpec((1, tm, d), row),
        out_shape=jax.ShapeDtypeStruct((bsz, s, d), F32),
        scratch_shapes=[pltpu.VMEM((tm, d), F32)],
        compiler_params=_params("arbitrary", "arbitrary", "arbitrary"),
        name="mlp",
    )(h2, h2, x1, mod, w_gate, w_val, cw_gate, cb_gate, cw_val, cb_val, w_down, g_final)


def _pad_cols(a, n):
    return jnp.pad(a, ((0, 0), (0, n - a.shape[1])))


def kernel(x, c, w_ada, b_ada, g_attn, w_in, b_fgate, g_out_fox, g_out_sb, w_out,
           g_mlp, w_up, conv_w, conv_b, w_down, g_final):
    depth, d, _ = w_ada.shape
    n_fox = b_fgate.shape[1]
    d_fox = n_fox * HEAD_DIM
    d_sb = g_out_sb.shape[1]
    n_sb = d_sb // HEAD_DIM
    d_ff = w_down.shape[1]
    d_ff_pad = -(-d_ff // FF_CHUNK) * FF_CHUNK
    assert n_fox % 2 == 0 and n_sb == n_fox and 3 * n_fox <= LANES
    assert x.shape[1] % OUT_ROWS == 0 and x.shape[1] % ATT_Q == 0 and ATT_Q == 2 * ATT_K
    o_kf, o_vf, o_qs, o_ks, o_vs, o_gate = (d_fox, 2 * d_fox, 3 * d_fox, 3 * d_fox + d_sb,
                                             3 * d_fox + 2 * d_sb, 3 * d_fox + 3 * d_sb)

    for l in range(depth):
        mod = _ada(c, w_ada[l], b_ada[l]).reshape(-1, N_MOD, d)
        w = w_in[l]
        w_nat = jnp.concatenate([w[:, :o_vf], w[:, o_qs:o_vs]], axis=1).astype(BF16)
        w_vt = jnp.concatenate([w[:, o_vf:o_qs], w[:, o_vs:o_gate]], axis=1).T.astype(BF16)
        w_gate = _pad_cols(w[:, o_gate:], LANES).astype(BF16)
        b_gate = _pad_cols(b_fgate[l].reshape(1, n_fox), LANES)
        qk, vt, log_f = _inproj(x, mod, g_attn[l].reshape(1, d), w_nat, w_vt, w_gate, b_gate)

        pairs_f, pairs_s = n_fox // 2, n_sb // 2
        k_aug, stats = _decay(log_f, qk, n_fox, k_fox_block=1, k_sb_block=3)
        fox_k_spec = pl.BlockSpec((1, 2, x.shape[1], LANES), lambda b, p, i: (b, p, 0, 0))
        mix_f = _attention(_fox_kernel, "fox", _fox_scratch(), qk, k_aug, fox_k_spec, vt, g_out_fox[l], stats,
                           q_block0=0, vt_block0=0, n_heads=n_fox)
        sb_k_spec = pl.BlockSpec((1, x.shape[1], LANES), lambda b, p, i: (b, 0, 2 * pairs_f + pairs_s + p))
        mix_s = _attention(_sb_kernel, "sb", _sb_scratch(), qk, qk, sb_k_spec, vt, g_out_sb[l], stats,
                           q_block0=2 * pairs_f, vt_block0=pairs_f, n_heads=n_sb)

        x1, h2 = _outproj(x, mix_f, mix_s, w_out[l].astype(BF16), mod, g_mlp[l].reshape(1, d))

        wu, cw, cb = w_up[l], conv_w[l], conv_b[l].reshape(1, -1)
        x = _mlp(h2, x1, mod,
                 _pad_cols(wu[:, :d_ff], d_ff_pad).astype(BF16), _pad_cols(wu[:, d_ff:], d_ff_pad).astype(BF16),
                 _pad_cols(cw[:, :d_ff], d_ff_pad), _pad_cols(cb[:, :d_ff], d_ff_pad),
                 _pad_cols(cw[:, d_ff:], d_ff_pad), _pad_cols(cb[:, d_ff:], d_ff_pad),
                 jnp.pad(w_down[l], ((0, d_ff_pad - d_ff), (0, 0))).astype(BF16),
                 g_final.reshape(1, d), final_norm=(l == depth - 1))
    return x
```

```python
import functools

import numpy as np
import jax
import jax.numpy as jnp
from jax import lax
from jax.experimental import pallas as pl
from jax.experimental.pallas import tpu as pltpu

HEAD_DIM = 64
N_MOD = 6
CONV_WIDTH = 3
EPS = 1e-6

LANES = 128
BF16_SUBLANES = 16
VMEM_LIMIT_BYTES = 48 * 1024 * 1024

ATT_Q = 512
ATT_K = 256
ATT_COLS = 256
PROJ_ROWS = ATT_K
LOG2E = 1.4426950408889634
MASKED = -1e30
M_INIT = -1e29
PRUNE_LOG2 = 160.0
NORM_SLACK = 1.02
STATS_ROWS = 8
OUT_ROWS = 512
FF_CHUNK = 256

F32 = jnp.float32
BF16 = jnp.bfloat16
NT_DIMS = (((1,), (1,)), ((), ()))


def _dot(a, b):
    return jnp.dot(a, b, preferred_element_type=F32)


def _dot_nt(a, b):
    return lax.dot_general(a, b, NT_DIMS, preferred_element_type=F32)


def _params(*sem):
    return pltpu.CompilerParams(dimension_semantics=sem, vmem_limit_bytes=VMEM_LIMIT_BYTES)


def _rms_rows(x):
    return x * lax.rsqrt(jnp.mean(x * x, axis=-1, keepdims=True) + EPS)


def _softplus(z):
    return jnp.maximum(z, 0.0) + jnp.log(1.0 + jnp.exp(-jnp.abs(z)))


def _split3(x):
    hi = x.astype(BF16)
    r1 = x - hi.astype(F32)
    mid = r1.astype(BF16)
    lo = (r1 - mid.astype(F32)).astype(BF16)
    return hi, mid, lo


def _ada_kernel(c_ref, w_ref, b_ref, o_ref):
    c = c_ref[...]
    o_ref[...] = _dot(c * jax.nn.sigmoid(c), w_ref[...]) + b_ref[...]


def _ada(c, w, b):
    bsz, d = c.shape
    n = w.shape[1]
    return pl.pallas_call(
        _ada_kernel,
        grid=(n // d,),
        in_specs=[pl.BlockSpec((bsz, d), lambda j: (0, 0)),
                  pl.BlockSpec((d, d), lambda j: (0, j)),
                  pl.BlockSpec((1, d), lambda j: (0, j))],
        out_specs=pl.BlockSpec((bsz, d), lambda j: (0, j)),
        out_shape=jax.ShapeDtypeStruct((bsz, n), F32),
        compiler_params=_params("arbitrary"),
        name="ada",
    )(c, w, b.reshape(1, n))


def _inproj_kernel(x_ref, mod_ref, g_ref, wn_ref, wvt_ref, wg_ref, bg_ref, qk_ref, vt_ref, lf_ref):
    shift = mod_ref[0, 0:1, :]
    scale = mod_ref[0, 1:2, :]
    h = (_rms_rows(x_ref[0]) * g_ref[...] * (1.0 + scale) + shift).astype(BF16)
    qk_ref[0] = _dot(h, wn_ref[...]).astype(BF16)
    vt_ref[0, 0] = _dot_nt(wvt_ref[...], h).astype(BF16)
    logit = _dot(h, wg_ref[...]) + bg_ref[...]
    lf_ref[0] = -_softplus(-logit)


def _inproj(x, mod, g, w_nat, w_vt, w_gate, b_gate):
    bsz, s, d = x.shape
    tm = PROJ_ROWS
    n_nat, n_v = w_nat.shape[1], w_vt.shape[0]
    const = lambda b, i: (0, 0)
    return pl.pallas_call(
        _inproj_kernel,
        grid=(bsz, s // tm),
        in_specs=[pl.BlockSpec((1, tm, d), lambda b, i: (b, i, 0)),
                  pl.BlockSpec((1, N_MOD, d), lambda b, i: (b, 0, 0)),
                  pl.BlockSpec((1, d), const),
                  pl.BlockSpec((d, n_nat), const),
                  pl.BlockSpec((n_v, d), const),
                  pl.BlockSpec((d, LANES), const),
                  pl.BlockSpec((1, LANES), const)],
        out_specs=[pl.BlockSpec((1, tm, n_nat), lambda b, i: (b, i, 0)),
                   pl.BlockSpec((1, 1, n_v, tm), lambda b, i: (b, i, 0, 0)),
                   pl.BlockSpec((1, tm, LANES), lambda b, i: (b, i, 0))],
        out_shape=[jax.ShapeDtypeStruct((bsz, s, n_nat), BF16),
                   jax.ShapeDtypeStruct((bsz, s // tm, n_v, tm), BF16),
                   jax.ShapeDtypeStruct((bsz, s, LANES), F32)],
        compiler_params=_params("arbitrary", "arbitrary"),
        name="inproj",
    )(x, mod, g, w_nat, w_vt, w_gate, b_gate)


def _decay_kernel(lf_ref, k_ref, ks_ref, sel_ref, ind_ref, kaug_ref, stats_ref, carry_ref, kpre_ref, *, n_heads):
    @pl.when(pl.program_id(1) == 0)
    def _():
        carry_ref[...] = jnp.zeros_like(carry_ref)
        kpre_ref[...] = jnp.zeros_like(kpre_ref)

    tm = lf_ref.shape[1]
    lane = lax.broadcasted_iota(jnp.int32, (tm, LANES), 1)
    lf = jnp.where(lane < n_heads, lf_ref[0], 0.0)
    row = lax.broadcasted_iota(jnp.int32, (tm, tm), 0)
    col = lax.broadcasted_iota(jnp.int32, (tm, tm), 1)
    tri = (col <= row).astype(BF16)
    hi, mid, lo = _split3(lf)
    f_run = carry_ref[...] + (_dot(tri, hi) + _dot(tri, mid) + _dot(tri, lo))
    carry_ref[...] = f_run[tm - 1:tm, :]
    ghi, gmid, glo = _split3(-LOG2E * f_run)
    packed = (ghi.astype(F32) + pltpu.roll(gmid.astype(F32), n_heads, 1)
              + pltpu.roll(glo.astype(F32), 2 * n_heads, 1)).astype(BF16)
    placed = _dot(packed, sel_ref[...])
    k_all = k_ref[0]
    for h in range(n_heads):
        k_pair = k_all[:, (h // 2) * LANES:(h // 2 + 1) * LANES]
        own = (lane < HEAD_DIM) if h % 2 == 0 else (lane >= HEAD_DIM)
        kaug_ref[0, h] = jnp.where(own, k_pair, placed[:, h * LANES:(h + 1) * LANES].astype(BF16))

    def head_norm_bound(k):
        k32 = k.astype(F32)
        sq = _dot((k32 * k32).astype(BF16), ind_ref[...])
        return jnp.sqrt(jnp.max(sq, axis=0, keepdims=True) * NORM_SLACK)

    kpre_f = jnp.maximum(kpre_ref[0:1, :], head_norm_bound(k_all))
    kpre_s = jnp.maximum(kpre_ref[1:2, :], head_norm_bound(ks_ref[0]))
    kpre_ref[0:1, :] = kpre_f
    kpre_ref[1:2, :] = kpre_s
    g_end = -LOG2E * f_run[tm - 1:tm, :]
    lane1 = lax.broadcasted_iota(jnp.int32, (1, LANES), 1)

    def spread(v, h):
        return jnp.broadcast_to(jnp.sum(jnp.where(lane1 == h, v, 0.0), axis=1, keepdims=True), (1, ATT_COLS))

    for p in range(n_heads // 2):
        rows = [spread(v, 2 * p + hh) for v in (kpre_f, g_end, kpre_s) for hh in range(2)]
        rows += [jnp.zeros((1, ATT_COLS), F32)] * (stats_ref.shape[3] - len(rows))
        stats_ref[0, p, 0] = jnp.concatenate(rows, axis=0)


def _head_indicator(n_heads):
    ind = np.zeros((n_heads * HEAD_DIM, LANES), np.float32)
    ind[np.arange(n_heads * HEAD_DIM), np.arange(n_heads * HEAD_DIM) // HEAD_DIM] = 1.0
    return jnp.asarray(ind, BF16)


def _decay_select_matrix(n_heads):
    sel = np.zeros((LANES, n_heads * LANES), np.float32)
    for h in range(n_heads):
        base = h * LANES + (HEAD_DIM if h % 2 == 0 else 0)
        for term in range(3):
            sel[term * n_heads + h, base + term] = 1.0
    return jnp.asarray(sel, BF16)


def _decay(log_f, qk, n_heads, k_fox_block, k_sb_block):
    bsz, s, _ = log_f.shape
    tm = PROJ_ROWS
    d_grp = n_heads * HEAD_DIM
    return pl.pallas_call(
        functools.partial(_decay_kernel, n_heads=n_heads),
        grid=(bsz, s // tm),
        in_specs=[pl.BlockSpec((1, tm, LANES), lambda b, i: (b, i, 0)),
                  pl.BlockSpec((1, tm, d_grp), lambda b, i: (b, i, k_fox_block)),
                  pl.BlockSpec((1, tm, d_grp), lambda b, i: (b, i, k_sb_block)),
                  pl.BlockSpec((LANES, n_heads * LANES), lambda b, i: (0, 0)),
                  pl.BlockSpec((d_grp, LANES), lambda b, i: (0, 0))],
        out_specs=[pl.BlockSpec((1, n_heads, tm, LANES), lambda b, i: (b, 0, i, 0)),
                   pl.BlockSpec((1, n_heads // 2, 1, STATS_ROWS, ATT_COLS), lambda b, i: (b, 0, i, 0, 0))],
        out_shape=[jax.ShapeDtypeStruct((bsz, n_heads, s, LANES), BF16),
                   jax.ShapeDtypeStruct((bsz, n_heads // 2, s // tm, STATS_ROWS, ATT_COLS), F32)],
        scratch_shapes=[pltpu.VMEM((1, LANES), F32), pltpu.VMEM((2, LANES), F32)],
        compiler_params=_params("arbitrary", "arbitrary"),
        name="decay",
    )(log_f, qk, qk, _decay_select_matrix(n_heads), _head_indicator(n_heads))


def _lane_queries(q_ref, extra_even, extra_odd, cw):
    q = q_ref[0].astype(F32) * (HEAD_DIM ** -0.5 * LOG2E)
    lane = lax.broadcasted_iota(jnp.int32, q.shape, 1)
    heads = (jnp.where(lane < HEAD_DIM, q, extra_even(lane)).T.astype(BF16),
             jnp.where(lane >= HEAD_DIM, q, extra_odd(lane)).T.astype(BF16))
    return [heads[hh][:, c * cw:(c + 1) * cw] for hh in range(2) for c in range(q.shape[0] // cw)]


def _visibility(first_key, first_query, bk, cw, strict):
    last_visible_gap = -1 if strict else 0
    if first_key + bk - 1 - first_query <= last_visible_gap:
        return "all"
    if first_key - (first_query + cw - 1) > last_visible_gap:
        return "none"
    gap = (lax.broadcasted_iota(jnp.int32, (bk, cw), 0) - lax.broadcasted_iota(jnp.int32, (bk, cw), 1)
           + (first_key - first_query))
    return gap <= last_visible_gap


def _diag_visibility(u, c, bk, cw, strict):
    return _visibility((1 - u) * bk, c * cw, bk, cw, strict)


def _hidden(visibility):
    return isinstance(visibility, str) and visibility == "none"


def _query_norm_bounds(queries, n_chunks):
    bounds = []
    for li, q in enumerate(queries):
        hh = li // n_chunks
        own = q[hh * HEAD_DIM:(hh + 1) * HEAD_DIM, :].astype(F32)
        bounds.append(jnp.sqrt(jnp.sum(own * own, axis=0, keepdims=True) * NORM_SLACK))
    return bounds


def _finish_heads(lanes, g_ref, o_ref):
    n_chunks = len(lanes) // 2
    outs = [jnp.concatenate(lanes[hh * n_chunks:(hh + 1) * n_chunks], axis=1) for hh in range(2)]
    normed = [o * lax.rsqrt(jnp.mean(o * o, axis=0, keepdims=True) + EPS) for o in outs]
    o_ref[0] = (jnp.concatenate(normed, axis=0).T * g_ref[...]).astype(o_ref.dtype)


def _fox_kernel(q_ref, k_ref, vt_ref, g_ref, stats_ref, o_ref, s_buf, cmax_buf, p_buf, acc_buf):
    qi = pl.program_id(2)
    bk = vt_ref.shape[3]
    n_lanes, cw = acc_buf.shape[0], acc_buf.shape[2]
    n_chunks = n_lanes // 2
    n_tiles = 2 * (qi + 1)
    ones3 = lambda lo: (lambda lane: jnp.where((lane >= lo) & (lane < lo + 3), 1.0, 0.0))
    queries = _lane_queries(q_ref, ones3(HEAD_DIM), ones3(0), cw)
    acc_buf[...] = jnp.zeros(acc_buf.shape, F32)
    for li in range(n_lanes):
        if _hidden(_diag_visibility(0, li % n_chunks, bk, cw, strict=False)):
            p_buf[0, li] = jnp.zeros((bk, cw), BF16)

    def step(t, slot, carry, score="below", softmax="below", value=True):
        new = []
        for li in range(n_lanes):
            hh, c = divmod(li, n_chunks)
            see = lambda u: "all" if u == "below" else _diag_visibility(u, c, bk, cw, strict=False)
            if score is not None and not _hidden(see(score)):
                start = pl.multiple_of((n_tiles - 2 - t) * bk, bk)
                s_new = _dot(k_ref[0, hh, pl.ds(start, bk), :], queries[li])
                if not isinstance(see(score), str):
                    s_new = jnp.where(see(score), s_new, MASKED)
                s_buf[1 - slot, li] = s_new
                cmax_buf[1 - slot, li] = jnp.max(s_new, axis=0, keepdims=True)
            pv = None
            if value:
                vt = vt_ref[0, n_tiles - t, hh * HEAD_DIM:(hh + 1) * HEAD_DIM, :]
                pv = _dot(vt, p_buf[1 - slot, li])
            m, l = carry[li]
            if softmax is not None and not _hidden(see(softmax)):
                m_new = jnp.maximum(m, cmax_buf[slot, li])
                alpha = jnp.exp2(m - m_new)
                p = jnp.exp2(s_buf[slot, li] - m_new)
                p_buf[slot, li] = p.astype(BF16)
                m, l = m_new, alpha * l + jnp.sum(p, axis=0, keepdims=True)
                acc_buf[li] = alpha * (acc_buf[li] if pv is None else acc_buf[li] + pv)
            elif pv is not None:
                acc_buf[li] += pv
            new.append((m, l))
        return tuple(new)

    def step_pair(i, carry):
        t = 2 * i + 1
        return step(t + 1, 0, step(t, 1, carry))

    q_norm = _query_norm_bounds(queries, n_chunks)

    def later_tiles_matter(i, carry):
        j_rest = jnp.maximum(n_tiles - 5 - 2 * i, 0)
        worst = None
        for li in range(n_lanes):
            hh = li // n_chunks
            bound = (q_norm[li] * stats_ref[0, 0, j_rest, hh:hh + 1, :]
                     + stats_ref[0, 0, j_rest, 2 + hh:3 + hh, :] - carry[li][0])
            worst = bound if worst is None else jnp.maximum(worst, bound)
        return jnp.max(worst) >= -PRUNE_LOG2

    def pair_and_check(state):
        i, _, carry = state
        carry = step_pair(i, carry)
        return i + 1, later_tiles_matter(i, carry), carry

    carry = tuple((jnp.full((1, cw), M_INIT, F32), jnp.zeros((1, cw), F32)) for _ in range(n_lanes))
    carry = step(-1, 1, carry, score=0, softmax=None, value=False)
    carry = step(0, 0, carry, score=1, softmax=0, value=False)
    n_pairs, _, carry = lax.while_loop(lambda st: (st[0] < qi) & st[1], pair_and_check,
                                       (jnp.int32(0), jnp.bool_(True), carry))
    carry = step(2 * n_pairs + 1, 1, carry, score=None)
    carry = step(2 * n_pairs + 2, 0, carry, score=None, softmax=None)
    _finish_heads([acc_buf[li] / carry[li][1] for li in range(n_lanes)], g_ref, o_ref)


def _sb_kernel(q_ref, k_ref, vt_ref, g_ref, stats_ref, o_ref, z_buf, sp_buf, e_buf, wrow_buf, acc_buf):
    qi = pl.program_id(2)
    bk = vt_ref.shape[3]
    n_lanes, cw = acc_buf.shape[0], acc_buf.shape[2]
    n_chunks = n_lanes // 2
    n_tiles = 2 * (qi + 1)
    zero = lambda lane: 0.0
    queries = _lane_queries(q_ref, zero, zero, cw)
    suffix = (lax.broadcasted_iota(jnp.int32, (bk, bk), 1)
              >= lax.broadcasted_iota(jnp.int32, (bk, bk), 0)).astype(BF16)
    e_buf[1] = jnp.full(e_buf.shape[1:], MASKED, F32)
    wrow_buf[1] = jnp.zeros(wrow_buf.shape[1:], F32)
    acc_buf[...] = jnp.zeros(acc_buf.shape, F32)
    for li in range(n_lanes):
        if _hidden(_diag_visibility(0, li % n_chunks, bk, cw, strict=True)):
            z_buf[0, li] = jnp.full((bk, cw), MASKED, F32)
            sp_buf[0, li] = jnp.zeros((bk, cw), BF16)

    def step(t, slot, later, score="below", softplus="below", cumsum=True, weight=True):
        new_later = []
        for li in range(n_lanes):
            hh, c = divmod(li, n_chunks)
            see = lambda u: "all" if u == "below" else _diag_visibility(u, c, bk, cw, strict=True)
            if cumsum:
                within = _dot(suffix, sp_buf[1 - slot, li])
                e_buf[1 - slot, li] = z_buf[1 - slot, li] - within
                wrow_buf[1 - slot, li] = within[0:1, :]
            if score is not None and not _hidden(see(score)):
                start = pl.multiple_of((n_tiles - 1 - (t + 3)) * bk, bk)
                z_new = _dot(k_ref[0, pl.ds(start, bk), :], queries[li])
                if not isinstance(see(score), str):
                    z_new = jnp.where(see(score), z_new, MASKED)
                z_buf[1 - slot, li] = z_new
            if weight:
                a = jnp.exp2(e_buf[slot, li] - later[li])
                vt = vt_ref[0, n_tiles - 1 - jnp.maximum(t, 0), hh * HEAD_DIM:(hh + 1) * HEAD_DIM, :]
                acc_buf[li] += _dot(vt, a.astype(BF16))
                new_later.append(later[li] + wrow_buf[slot, li])
            else:
                new_later.append(later[li])
            if softplus is not None and not _hidden(see(softplus)):
                z = z_buf[slot, li]
                sp_buf[slot, li] = (jnp.maximum(z, 0.0) + jnp.log2(1.0 + jnp.exp2(-jnp.abs(z)))).astype(BF16)
        return tuple(new_later)

    def step_pair(i, later):
        t = 2 * i - 1
        return step(t + 1, 0, step(t, 1, later))

    q_norm = _query_norm_bounds(queries, n_chunks)

    def later_tiles_matter(i, later):
        j_rest = jnp.maximum(n_tiles - 5 - 2 * i, 0)
        worst = None
        for li in range(n_lanes):
            hh = li // n_chunks
            bound = q_norm[li] * stats_ref[0, 0, j_rest, 4 + hh:5 + hh, :] - (later[li] + wrow_buf[1, li])
            worst = bound if worst is None else jnp.maximum(worst, bound)
        return jnp.max(worst) >= -PRUNE_LOG2

    def pair_and_check(state):
        i, _, later = state
        later = step_pair(i, later)
        return i + 1, later_tiles_matter(i, later), later

    later = tuple(jnp.zeros((1, cw), F32) for _ in range(n_lanes))
    later = step(-3, 1, later, score=0, softplus=None, cumsum=False, weight=False)
    later = step(-2, 0, later, score=1, softplus=0, cumsum=False, weight=False)
    n_pairs, _, later = lax.while_loop(lambda st: (st[0] < qi) & st[1], pair_and_check,
                                       (jnp.int32(0), jnp.bool_(True), later))
    later = step(2 * n_pairs - 1, 1, later, score=None)
    later = step(2 * n_pairs, 0, later, score=None, softplus=None)
    later = step(2 * n_pairs + 1, 1, later, score=None, softplus=None, cumsum=False)
    _finish_heads([acc_buf[li] for li in range(n_lanes)], g_ref, o_ref)


def _attention(body, name, scratch, qk, k_arr, k_spec, vt, g, stats, q_block0, vt_block0, n_heads):
    bsz, s, _ = qk.shape
    bq, bk = ATT_Q, ATT_K
    d_grp = n_heads * HEAD_DIM
    return pl.pallas_call(
        body,
        grid=(bsz, n_heads // 2, s // bq),
        in_specs=[pl.BlockSpec((1, bq, LANES), lambda b, p, i: (b, i, q_block0 + p)),
                  k_spec,
                  pl.BlockSpec((1, s // bk, LANES, bk), lambda b, p, i: (b, 0, vt_block0 + p, 0)),
                  pl.BlockSpec((1, LANES), lambda b, p, i: (0, p)),
                  pl.BlockSpec((1, 1) + stats.shape[2:], lambda b, p, i: (b, p, 0, 0, 0))],
        out_specs=pl.BlockSpec((1, bq, LANES), lambda b, p, i: (b, i, p)),
        out_shape=jax.ShapeDtypeStruct((bsz, s, d_grp), BF16),
        scratch_shapes=scratch,
        compiler_params=_params("arbitrary", "arbitrary", "arbitrary"),
        name=name,
    )(qk, k_arr, vt, g.reshape(1, d_grp), stats)


ATT_LANE_GROUPS = 2 * (ATT_Q // ATT_COLS)


def _fox_scratch():
    n = ATT_LANE_GROUPS
    return [pltpu.VMEM((2, n, ATT_K, ATT_COLS), F32), pltpu.VMEM((2, n, 1, ATT_COLS), F32),
            pltpu.VMEM((2, n, ATT_K, ATT_COLS), BF16), pltpu.VMEM((n, HEAD_DIM, ATT_COLS), F32)]


def _sb_scratch():
    n = ATT_LANE_GROUPS
    return [pltpu.VMEM((2, n, ATT_K, ATT_COLS), F32), pltpu.VMEM((2, n, ATT_K, ATT_COLS), BF16),
            pltpu.VMEM((2, n, ATT_K, ATT_COLS), F32), pltpu.VMEM((2, n, 1, ATT_COLS), F32),
            pltpu.VMEM((n, HEAD_DIM, ATT_COLS), F32)]


def _outproj_kernel(x_ref, mf_ref, ms_ref, w_ref, mod_ref, g_ref, x1_ref, h2_ref):
    mix = jnp.concatenate([mf_ref[0], ms_ref[0]], axis=-1)
    x1 = x_ref[0] + mod_ref[0, 2:3, :] * _dot(mix, w_ref[...])
    x1_ref[0] = x1
    shift = mod_ref[0, 3:4, :]
    scale = mod_ref[0, 4:5, :]
    h2_ref[0] = (_rms_rows(x1) * g_ref[...] * (1.0 + scale) + shift).astype(BF16)


def _outproj(x, mix_f, mix_s, w_out, mod, g):
    bsz, s, d = x.shape
    tm = OUT_ROWS
    row = lambda b, i: (b, i, 0)
    return pl.pallas_call(
        _outproj_kernel,
        grid=(bsz, s // tm),
        in_specs=[pl.BlockSpec((1, tm, d), row),
                  pl.BlockSpec((1, tm, mix_f.shape[2]), row),
                  pl.BlockSpec((1, tm, mix_s.shape[2]), row),
                  pl.BlockSpec(w_out.shape, lambda b, i: (0, 0)),
                  pl.BlockSpec((1, N_MOD, d), lambda b, i: (b, 0, 0)),
                  pl.BlockSpec((1, d), lambda b, i: (0, 0))],
        out_specs=[pl.BlockSpec((1, tm, d), row), pl.BlockSpec((1, tm, d), row)],
        out_shape=[jax.ShapeDtypeStruct((bsz, s, d), F32), jax.ShapeDtypeStruct((bsz, s, d), BF16)],
        compiler_params=_params("arbitrary", "arbitrary"),
        name="outproj",
    )(x, mix_f, mix_s, w_out, mod, g)


def _mlp_kernel(h_ref, halo_ref, x1_ref, mod_ref, wg_ref, wv_ref, cwg_ref, cbg_ref, cwv_ref, cbv_ref,
                wd_ref, gf_ref, o_ref, acc_ref, *, final_norm):
    i = pl.program_id(1)
    c = pl.program_id(2)
    tm = h_ref.shape[1]
    halo = halo_ref[0]
    halo = jnp.where(i > 0, halo, jnp.zeros_like(halo))
    hx = jnp.concatenate([halo, h_ref[0]], axis=0)

    def conv_branch(w_ref, cw_ref, cb_ref):
        u = _dot(hx, w_ref[...])
        out = cb_ref[...]
        for tap in range(CONV_WIDTH):
            lag = CONV_WIDTH - 1 - tap
            first = BF16_SUBLANES - lag
            out = out + cw_ref[tap:tap + 1, :] * u[first:first + tm, :]
        return out

    u_gate = conv_branch(wg_ref, cwg_ref, cbg_ref)
    u_val = conv_branch(wv_ref, cwv_ref, cbv_ref)
    part = _dot((u_gate * jax.nn.sigmoid(u_gate) * u_val).astype(BF16), wd_ref[...])

    @pl.when(c == 0)
    def _():
        acc_ref[...] = part

    @pl.when(c > 0)
    def _():
        acc_ref[...] += part

    @pl.when(c == pl.num_programs(2) - 1)
    def _():
        x2 = x1_ref[0] + mod_ref[0, 5:6, :] * acc_ref[...]
        o_ref[0] = _rms_rows(x2) * gf_ref[...] if final_norm else x2


def _mlp(h2, x1, mod, w_gate, w_val, cw_gate, cb_gate, cw_val, cb_val, w_down, g_final, final_norm):
    bsz, s, d = x1.shape
    tm, tf = OUT_ROWS, FF_CHUNK
    halo_blocks = tm // BF16_SUBLANES
    row = lambda b, i, c: (b, i, 0)
    col = lambda b, i, c: (0, c)
    return pl.pallas_call(
        functools.partial(_mlp_kernel, final_norm=final_norm),
        grid=(bsz, s // tm, w_gate.shape[1] // tf),
        in_specs=[pl.BlockSpec((1, tm, d), row),
                  pl.BlockSpec((1, BF16_SUBLANES, d),
                               lambda b, i, c: (b, jnp.maximum(i * halo_blocks - 1, 0), 0)),
                  pl.BlockSpec((1, tm, d), row),
                  pl.BlockSpec((1, N_MOD, d), lambda b, i, c: (b, 0, 0)),
                  pl.BlockSpec((d, tf), col), pl.BlockSpec((d, tf), col),
                  pl.BlockSpec((CONV_WIDTH, tf), col), pl.BlockSpec((1, tf), col),
                  pl.BlockSpec((CONV_WIDTH, tf), col), pl.BlockSpec((1, tf), col),
                  pl.BlockSpec((tf, d), lambda b, i, c: (c, 0)),
                  pl.BlockSpec((1, d), lambda b, i, c: (0, 0))],
        out_specs=pl.BlockSpec((1, tm, d), row),
        out_shape=jax.ShapeDtypeStruct((bsz, s, d), F32),
        scratch_shapes=[pltpu.VMEM((tm, d), F32)],
        compiler_params=_params("arbitrary", "arbitrary", "arbitrary"),
        name="mlp",
    )(h2, h2, x1, mod, w_gate, w_val, cw_gate, cb_gate, cw_val, cb_val, w_down, g_final)


def _pad_cols(a, n):
    return jnp.pad(a, ((0, 0), (0, n - a.shape[1])))


def kernel(x, c, w_ada, b_ada, g_attn, w_in, b_fgate, g_out_fox, g_out_sb, w_out,
           g_mlp, w_up, conv_w, conv_b, w_down, g_final):
    depth, d, _ = w_ada.shape
    n_fox = b_fgate.shape[1]
    d_fox = n_fox * HEAD_DIM
    d_sb = g_out_sb.shape[1]
    n_sb = d_sb // HEAD_DIM
    d_ff = w_down.shape[1]
    d_ff_pad = -(-d_ff // FF_CHUNK) * FF_CHUNK
    assert n_fox % 2 == 0 and n_sb == n_fox and 3 * n_fox <= LANES
    assert x.shape[1] % OUT_ROWS == 0 and x.shape[1] % ATT_Q == 0 and ATT_Q == 2 * ATT_K
    o_kf, o_vf, o_qs, o_ks, o_vs, o_gate = (d_fox, 2 * d_fox, 3 * d_fox, 3 * d_fox + d_sb,
                                             3 * d_fox + 2 * d_sb, 3 * d_fox + 3 * d_sb)

    for l in range(depth):
        mod = _ada(c, w_ada[l], b_ada[l]).reshape(-1, N_MOD, d)
        w = w_in[l]
        w_nat = jnp.concatenate([w[:, :o_vf], w[:, o_qs:o_vs]], axis=1).astype(BF16)
        w_vt = jnp.concatenate([w[:, o_vf:o_qs], w[:, o_vs:o_gate]], axis=1).T.astype(BF16)
        w_gate = _pad_cols(w[:, o_gate:], LANES).astype(BF16)
        b_gate = _pad_cols(b_fgate[l].reshape(1, n_fox), LANES)
        qk, vt, log_f = _inproj(x, mod, g_attn[l].reshape(1, d), w_nat, w_vt, w_gate, b_gate)

        pairs_f, pairs_s = n_fox // 2, n_sb // 2
        k_aug, stats = _decay(log_f, qk, n_fox, k_fox_block=1, k_sb_block=3)
        fox_k_spec = pl.BlockSpec((1, 2, x.shape[1], LANES), lambda b, p, i: (b, p, 0, 0))
        mix_f = _attention(_fox_kernel, "fox", _fox_scratch(), qk, k_aug, fox_k_spec, vt, g_out_fox[l], stats,
                           q_block0=0, vt_block0=0, n_heads=n_fox)
        sb_k_spec = pl.BlockSpec((1, x.shape[1], LANES), lambda b, p, i: (b, 0, 2 * pairs_f + pairs_s + p))
        mix_s = _attention(_sb_kernel, "sb", _sb_scratch(), qk, qk, sb_k_spec, vt, g_out_sb[l], stats,
                           q_block0=2 * pairs_f, vt_block0=pairs_f, n_heads=n_sb)

        x1, h2 = _outproj(x, mix_f, mix_s, w_out[l].astype(BF16), mod, g_mlp[l].reshape(1, d))

        wu, cw, cb = w_up[l], conv_w[l], conv_b[l].reshape(1, -1)
        x = _mlp(h2, x1, mod,
                 _pad_cols(wu[:, :d_ff], d_ff_pad).astype(BF16), _pad_cols(wu[:, d_ff:], d_ff_pad).astype(BF16),
                 _pad_cols(cw[:, :d_ff], d_ff_pad), _pad_cols(cb[:, :d_ff], d_ff_pad),
                 _pad_cols(cw[:, d_ff:], d_ff_pad), _pad_cols(cb[:, d_ff:], d_ff_pad),
                 jnp.pad(w_down[l], ((0, d_ff_pad - d_ff), (0, 0))).astype(BF16),
                 g_final.reshape(1, d), final_norm=(l == depth - 1))
    return x
```

```python
import functools

import numpy as np
import jax
import jax.numpy as jnp
from jax import lax
from jax.experimental import pallas as pl
from jax.experimental.pallas import tpu as pltpu

HEAD_DIM = 64
N_MOD = 6
CONV_WIDTH = 3
EPS = 1e-6

LANES = 128
BF16_SUBLANES = 16
VMEM_LIMIT_BYTES = 48 * 1024 * 1024

ATT_Q = 512
ATT_K = 256
ATT_COLS = 256
PROJ_ROWS = ATT_K
LOG2E = 1.4426950408889634
MASKED = -1e30
M_INIT = -1e29
PRUNE_LOG2 = 160.0
NORM_SLACK = 1.02
STATS_ROWS = 8
OUT_ROWS = 512
FF_CHUNK = 256

F32 = jnp.float32
BF16 = jnp.bfloat16
NT_DIMS = (((1,), (1,)), ((), ()))


def _dot(a, b):
    return jnp.dot(a, b, preferred_element_type=F32)


def _dot_nt(a, b):
    return lax.dot_general(a, b, NT_DIMS, preferred_element_type=F32)


def _params(*sem):
    return pltpu.CompilerParams(dimension_semantics=sem, vmem_limit_bytes=VMEM_LIMIT_BYTES)


def _rms_rows(x):
    return x * lax.rsqrt(jnp.mean(x * x, axis=-1, keepdims=True) + EPS)


def _softplus(z):
    return jnp.maximum(z, 0.0) + jnp.log(1.0 + jnp.exp(-jnp.abs(z)))


def _split3(x):
    hi = x.astype(BF16)
    r1 = x - hi.astype(F32)
    mid = r1.astype(BF16)
    lo = (r1 - mid.astype(F32)).astype(BF16)
    return hi, mid, lo


def _ada_kernel(c_ref, w_ref, b_ref, o_ref):
    c = c_ref[...]
    o_ref[...] = _dot(c * jax.nn.sigmoid(c), w_ref[...]) + b_ref[...]


def _ada(c, w, b):
    bsz, d = c.shape
    n = w.shape[1]
    return pl.pallas_call(
        _ada_kernel,
        grid=(n // d,),
        in_specs=[pl.BlockSpec((bsz, d), lambda j: (0, 0)),
                  pl.BlockSpec((d, d), lambda j: (0, j)),
                  pl.BlockSpec((1, d), lambda j: (0, j))],
        out_specs=pl.BlockSpec((bsz, d), lambda j: (0, j)),
        out_shape=jax.ShapeDtypeStruct((bsz, n), F32),
        compiler_params=_params("arbitrary"),
        name="ada",
    )(c, w, b.reshape(1, n))


def _inproj_kernel(x_ref, mod_ref, g_ref, wn_ref, wvt_ref, wg_ref, bg_ref, qk_ref, vt_ref, lf_ref):
    shift = mod_ref[0, 0:1, :]
    scale = mod_ref[0, 1:2, :]
    h = (_rms_rows(x_ref[0]) * g_ref[...] * (1.0 + scale) + shift).astype(BF16)
    qk_ref[0] = _dot(h, wn_ref[...]).astype(BF16)
    vt_ref[0, 0] = _dot_nt(wvt_ref[...], h).astype(BF16)
    logit = _dot(h, wg_ref[...]) + bg_ref[...]
    lf_ref[0] = -_softplus(-logit)


def _inproj(x, mod, g, w_nat, w_vt, w_gate, b_gate):
    bsz, s, d = x.shape
    tm = PROJ_ROWS
    n_nat, n_v = w_nat.shape[1], w_vt.shape[0]
    const = lambda b, i: (0, 0)
    return pl.pallas_call(
        _inproj_kernel,
        grid=(bsz, s // tm),
        in_specs=[pl.BlockSpec((1, tm, d), lambda b, i: (b, i, 0)),
                  pl.BlockSpec((1, N_MOD, d), lambda b, i: (b, 0, 0)),
                  pl.BlockSpec((1, d), const),
                  pl.BlockSpec((d, n_nat), const),
                  pl.BlockSpec((n_v, d), const),
                  pl.BlockSpec((d, LANES), const),
                  pl.BlockSpec((1, LANES), const)],
        out_specs=[pl.BlockSpec((1, tm, n_nat), lambda b, i: (b, i, 0)),
                   pl.BlockSpec((1, 1, n_v, tm), lambda b, i: (b, i, 0, 0)),
                   pl.BlockSpec((1, tm, LANES), lambda b, i: (b, i, 0))],
        out_shape=[jax.ShapeDtypeStruct((bsz, s, n_nat), BF16),
                   jax.ShapeDtypeStruct((bsz, s // tm, n_v, tm), BF16),
                   jax.ShapeDtypeStruct((bsz, s, LANES), F32)],
        compiler_params=_params("arbitrary", "arbitrary"),
        name="inproj",
    )(x, mod, g, w_nat, w_vt, w_gate, b_gate)


def _decay_kernel(lf_ref, k_ref, ks_ref, sel_ref, ind_ref, kaug_ref, stats_ref, carry_ref, kpre_ref, *, n_heads):
    @pl.when(pl.program_id(1) == 0)
    def _():
        carry_ref[...] = jnp.zeros_like(carry_ref)
        kpre_ref[...] = jnp.zeros_like(kpre_ref)

    tm = lf_ref.shape[1]
    lane = lax.broadcasted_iota(jnp.int32, (tm, LANES), 1)
    lf = jnp.where(lane < n_heads, lf_ref[0], 0.0)
    row = lax.broadcasted_iota(jnp.int32, (tm, tm), 0)
    col = lax.broadcasted_iota(jnp.int32, (tm, tm), 1)
    tri = (col <= row).astype(BF16)
    hi, mid, lo = _split3(lf)
    f_run = carry_ref[...] + (_dot(tri, hi) + _dot(tri, mid) + _dot(tri, lo))
    carry_ref[...] = f_run[tm - 1:tm, :]
    ghi, gmid, glo = _split3(-LOG2E * f_run)
    packed = (ghi.astype(F32) + pltpu.roll(gmid.astype(F32), n_heads, 1)
              + pltpu.roll(glo.astype(F32), 2 * n_heads, 1)).astype(BF16)
    placed = _dot(packed, sel_ref[...])
    k_all = k_ref[0]
    for h in range(n_heads):
        k_pair = k_all[:, (h // 2) * LANES:(h // 2 + 1) * LANES]
        own = (lane < HEAD_DIM) if h % 2 == 0 else (lane >= HEAD_DIM)
        kaug_ref[0, h] = jnp.where(own, k_pair, placed[:, h * LANES:(h + 1) * LANES].astype(BF16))

    def head_norm_bound(k):
        k32 = k.astype(F32)
        sq = _dot((k32 * k32).astype(BF16), ind_ref[...])
        return jnp.sqrt(jnp.max(sq, axis=0, keepdims=True) * NORM_SLACK)

    kpre_f = jnp.maximum(kpre_ref[0:1, :], head_norm_bound(k_all))
    kpre_s = jnp.maximum(kpre_ref[1:2, :], head_norm_bound(ks_ref[0]))
    kpre_ref[0:1, :] = kpre_f
    kpre_ref[1:2, :] = kpre_s
    g_end = -LOG2E * f_run[tm - 1:tm, :]
    lane1 = lax.broadcasted_iota(jnp.int32, (1, LANES), 1)

    def spread(v, h):
        return jnp.broadcast_to(jnp.sum(jnp.where(lane1 == h, v, 0.0), axis=1, keepdims=True), (1, ATT_COLS))

    for p in range(n_heads // 2):
        rows = [spread(v, 2 * p + hh) for v in (kpre_f, g_end, kpre_s) for hh in range(2)]
        rows += [jnp.zeros((1, ATT_COLS), F32)] * (stats_ref.shape[3] - len(rows))
        stats_ref[0, p, 0] = jnp.concatenate(rows, axis=0)


def _head_indicator(n_heads):
    ind = np.zeros((n_heads * HEAD_DIM, LANES), np.float32)
    ind[np.arange(n_heads * HEAD_DIM), np.arange(n_heads * HEAD_DIM) // HEAD_DIM] = 1.0
    return jnp.asarray(ind, BF16)


def _decay_select_matrix(n_heads):
    sel = np.zeros((LANES, n_heads * LANES), np.float32)
    for h in range(n_heads):
        base = h * LANES + (HEAD_DIM if h % 2 == 0 else 0)
        for term in range(3):
            sel[term * n_heads + h, base + term] = 1.0
    return jnp.asarray(sel, BF16)


def _decay(log_f, qk, n_heads, k_fox_block, k_sb_block):
    bsz, s, _ = log_f.shape
    tm = PROJ_ROWS
    d_grp = n_heads * HEAD_DIM
    return pl.pallas_call(
        functools.partial(_decay_kernel, n_heads=n_heads),
        grid=(bsz, s // tm),
        in_specs=[pl.BlockSpec((1, tm, LANES), lambda b, i: (b, i, 0)),
                  pl.BlockSpec((1, tm, d_grp), lambda b, i: (b, i, k_fox_block)),
                  pl.BlockSpec((1, tm, d_grp), lambda b, i: (b, i, k_sb_block)),
                  pl.BlockSpec((LANES, n_heads * LANES), lambda b, i: (0, 0)),
                  pl.BlockSpec((d_grp, LANES), lambda b, i: (0, 0))],
        out_specs=[pl.BlockSpec((1, n_heads, tm, LANES), lambda b, i: (b, 0, i, 0)),
                   pl.BlockSpec((1, n_heads // 2, 1, STATS_ROWS, ATT_COLS), lambda b, i: (b, 0, i, 0, 0))],
        out_shape=[jax.ShapeDtypeStruct((bsz, n_heads, s, LANES), BF16),
                   jax.ShapeDtypeStruct((bsz, n_heads // 2, s // tm, STATS_ROWS, ATT_COLS), F32)],
        scratch_shapes=[pltpu.VMEM((1, LANES), F32), pltpu.VMEM((2, LANES), F32)],
        compiler_params=_params("arbitrary", "arbitrary"),
        name="decay",
    )(log_f, qk, qk, _decay_select_matrix(n_heads), _head_indicator(n_heads))


def _lane_queries(q_ref, extra_even, extra_odd, cw):
    q = q_ref[0].astype(F32) * (HEAD_DIM ** -0.5 * LOG2E)
    lane = lax.broadcasted_iota(jnp.int32, q.shape, 1)
    heads = (jnp.where(lane < HEAD_DIM, q, extra_even(lane)).T.astype(BF16),
             jnp.where(lane >= HEAD_DIM, q, extra_odd(lane)).T.astype(BF16))
    return [heads[hh][:, c * cw:(c + 1) * cw] for hh in range(2) for c in range(q.shape[0] // cw)]


def _visibility(first_key, first_query, bk, cw, strict):
    last_visible_gap = -1 if strict else 0
    if first_key + bk - 1 - first_query <= last_visible_gap:
        return "all"
    if first_key - (first_query + cw - 1) > last_visible_gap:
        return "none"
    gap = (lax.broadcasted_iota(jnp.int32, (bk, cw), 0) - lax.broadcasted_iota(jnp.int32, (bk, cw), 1)
           + (first_key - first_query))
    return gap <= last_visible_gap


def _diag_visibility(u, c, bk, cw, strict):
    return _visibility((1 - u) * bk, c * cw, bk, cw, strict)


def _hidden(visibility):
    return isinstance(visibility, str) and visibility == "none"


def _query_norm_bounds(queries, n_chunks):
    bounds = []
    for li, q in enumerate(queries):
        hh = li // n_chunks
        own = q[hh * HEAD_DIM:(hh + 1) * HEAD_DIM, :].astype(F32)
        bounds.append(jnp.sqrt(jnp.sum(own * own, axis=0, keepdims=True) * NORM_SLACK))
    return bounds


def _finish_heads(lanes, g_ref, o_ref):
    n_chunks = len(lanes) // 2
    outs = [jnp.concatenate(lanes[hh * n_chunks:(hh + 1) * n_chunks], axis=1) for hh in range(2)]
    normed = [o * lax.rsqrt(jnp.mean(o * o, axis=0, keepdims=True) + EPS) for o in outs]
    o_ref[0] = (jnp.concatenate(normed, axis=0).T * g_ref[...]).astype(o_ref.dtype)


def _fox_kernel(q_ref, k_ref, vt_ref, g_ref, stats_ref, o_ref, s_buf, cmax_buf, p_buf, acc_buf):
    qi = pl.program_id(2)
    bk = vt_ref.shape[3]
    n_lanes, cw = acc_buf.shape[0], acc_buf.shape[2]
    n_chunks = n_lanes // 2
    n_tiles = 2 * (qi + 1)
    ones3 = lambda lo: (lambda lane: jnp.where((lane >= lo) & (lane < lo + 3), 1.0, 0.0))
    queries = _lane_queries(q_ref, ones3(HEAD_DIM), ones3(0), cw)
    acc_buf[...] = jnp.zeros(acc_buf.shape, F32)
    for li in range(n_lanes):
        if _hidden(_diag_visibility(0, li % n_chunks, bk, cw, strict=False)):
            p_buf[0, li] = jnp.zeros((bk, cw), BF16)

    def step(t, slot, carry, score="below", softmax="below", value=True):
        new = []
        for li in range(n_lanes):
            hh, c = divmod(li, n_chunks)
            see = lambda u: "all" if u == "below" else _diag_visibility(u, c, bk, cw, strict=False)
            if score is not None and not _hidden(see(score)):
                start = pl.multiple_of((n_tiles - 2 - t) * bk, bk)
                s_new = _dot(k_ref[0, hh, pl.ds(start, bk), :], queries[li])
                if not isinstance(see(score), str):
                    s_new = jnp.where(see(score), s_new, MASKED)
                s_buf[1 - slot, li] = s_new
                cmax_buf[1 - slot, li] = jnp.max(s_new, axis=0, keepdims=True)
            pv = None
            if value:
                vt = vt_ref[0, n_tiles - t, hh * HEAD_DIM:(hh + 1) * HEAD_DIM, :]
                pv = _dot(vt, p_buf[1 - slot, li])
            m, l = carry[li]
            if softmax is not None and not _hidden(see(softmax)):
                m_new = jnp.maximum(m, cmax_buf[slot, li])
                alpha = jnp.exp2(m - m_new)
                p = jnp.exp2(s_buf[slot, li] - m_new)
                p_buf[slot, li] = p.astype(BF16)
                m, l = m_new, alpha * l + jnp.sum(p, axis=0, keepdims=True)
                acc_buf[li] = alpha * (acc_buf[li] if pv is None else acc_buf[li] + pv)
            elif pv is not None:
                acc_buf[li] += pv
            new.append((m, l))
        return tuple(new)

    def step_pair(i, carry):
        t = 2 * i + 1
        return step(t + 1, 0, step(t, 1, carry))

    q_norm = _query_norm_bounds(queries, n_chunks)

    def later_tiles_matter(i, carry):
        j_rest = jnp.maximum(n_tiles - 5 - 2 * i, 0)
        worst = None
        for li in range(n_lanes):
            hh = li // n_chunks
            bound = (q_norm[li] * stats_ref[0, 0, j_rest, hh:hh + 1, :]
                     + stats_ref[0, 0, j_rest, 2 + hh:3 + hh, :] - carry[li][0])
            worst = bound if worst is None else jnp.maximum(worst, bound)
        return jnp.max(worst) >= -PRUNE_LOG2

    def pair_and_check(state):
        i, _, carry = state
        carry = step_pair(i, carry)
        return i + 1, later_tiles_matter(i, carry), carry

    carry = tuple((jnp.full((1, cw), M_INIT, F32), jnp.zeros((1, cw), F32)) for _ in range(n_lanes))
    carry = step(-1, 1, carry, score=0, softmax=None, value=False)
    carry = step(0, 0, carry, score=1, softmax=0, value=False)
    n_pairs, _, carry = lax.while_loop(lambda st: (st[0] < qi) & st[1], pair_and_check,
                                       (jnp.int32(0), jnp.bool_(True), carry))
    carry = step(2 * n_pairs + 1, 1, carry, score=None)
    carry = step(2 * n_pairs + 2, 0, carry, score=None, softmax=None)
    _finish_heads([acc_buf[li] / carry[li][1] for li in range(n_lanes)], g_ref, o_ref)


def _sb_kernel(q_ref, k_ref, vt_ref, g_ref, stats_ref, o_ref, z_buf, sp_buf, e_buf, wrow_buf, acc_buf):
    qi = pl.program_id(2)
    bk = vt_ref.shape[3]
    n_lanes, cw = acc_buf.shape[0], acc_buf.shape[2]
    n_chunks = n_lanes // 2
    n_tiles = 2 * (qi + 1)
    zero = lambda lane: 0.0
    queries = _lane_queries(q_ref, zero, zero, cw)
    suffix = (lax.broadcasted_iota(jnp.int32, (bk, bk), 1)
              >= lax.broadcasted_iota(jnp.int32, (bk, bk), 0)).astype(BF16)
    e_buf[1] = jnp.full(e_buf.shape[1:], MASKED, F32)
    wrow_buf[1] = jnp.zeros(wrow_buf.shape[1:], F32)
    acc_buf[...] = jnp.zeros(acc_buf.shape, F32)
    for li in range(n_lanes):
        if _hidden(_diag_visibility(0, li % n_chunks, bk, cw, strict=True)):
            z_buf[0, li] = jnp.full((bk, cw), MASKED, F32)
            sp_buf[0, li] = jnp.zeros((bk, cw), BF16)

    def step(t, slot, later, score="below", softplus="below", cumsum=True, weight=True):
        new_later = []
        for li in range(n_lanes):
            hh, c = divmod(li, n_chunks)
            see = lambda u: "all" if u == "below" else _diag_visibility(u, c, bk, cw, strict=True)
            if cumsum:
                within = _dot(suffix, sp_buf[1 - slot, li])
                e_buf[1 - slot, li] = z_buf[1 - slot, li] - within
                wrow_buf[1 - slot, li] = within[0:1, :]
            if score is not None and not _hidden(see(score)):
                start = pl.multiple_of((n_tiles - 1 - (t + 3)) * bk, bk)
                z_new = _dot(k_ref[0, pl.ds(start, bk), :], queries[li])
                if not isinstance(see(score), str):
                    z_new = jnp.where(see(score), z_new, MASKED)
                z_buf[1 - slot, li] = z_new
            if weight:
                a = jnp.exp2(e_buf[slot, li] - later[li])
                vt = vt_ref[0, n_tiles - 1 - jnp.maximum(t, 0), hh * HEAD_DIM:(hh + 1) * HEAD_DIM, :]
                acc_buf[li] += _dot(vt, a.astype(BF16))
                new_later.append(later[li] + wrow_buf[slot, li])
            else:
                new_later.append(later[li])
            if softplus is not None and not _hidden(see(softplus)):
                z = z_buf[slot, li]
                sp_buf[slot, li] = (jnp.maximum(z, 0.0) + jnp.log2(1.0 + jnp.exp2(-jnp.abs(z)))).astype(BF16)
        return tuple(new_later)

    def step_pair(i, later):
        t = 2 * i - 1
        return step(t + 1, 0, step(t, 1, later))

    q_norm = _query_norm_bounds(queries, n_chunks)

    def later_tiles_matter(i, later):
        j_rest = jnp.maximum(n_tiles - 5 - 2 * i, 0)
        worst = None
        for li in range(n_lanes):
            hh = li // n_chunks
            bound = q_norm[li] * stats_ref[0, 0, j_rest, 4 + hh:5 + hh, :] - (later[li] + wrow_buf[1, li])
            worst = bound if worst is None else jnp.maximum(worst, bound)
        return jnp.max(worst) >= -PRUNE_LOG2

    def pair_and_check(state):
        i, _, later = state
        later = step_pair(i, later)
        return i + 1, later_tiles_matter(i, later), later

    later = tuple(jnp.zeros((1, cw), F32) for _ in range(n_lanes))
    later = step(-3, 1, later, score=0, softplus=None, cumsum=False, weight=False)
    later = step(-2, 0, later, score=1, softplus=0, cumsum=False, weight=False)
    n_pairs, _, later = lax.while_loop(lambda st: (st[0] < qi) & st[1], pair_and_check,
                                       (jnp.int32(0), jnp.bool_(True), later))
    later = step(2 * n_pairs - 1, 1, later, score=None)
    later = step(2 * n_pairs, 0, later, score=None, softplus=None)
    later = step(2 * n_pairs + 1, 1, later, score=None, softplus=None, cumsum=False)
    _finish_heads([acc_buf[li] for li in range(n_lanes)], g_ref, o_ref)


def _attention(body, name, scratch, qk, k_arr, k_spec, vt, g, stats, q_block0, vt_block0, n_heads):
    bsz, s, _ = qk.shape
    bq, bk = ATT_Q, ATT_K
    d_grp = n_heads * HEAD_DIM
    return pl.pallas_call(
        body,
        grid=(bsz, n_heads // 2, s // bq),
        in_specs=[pl.BlockSpec((1, bq, LANES), lambda b, p, i: (b, i, q_block0 + p)),
                  k_spec,
                  pl.BlockSpec((1, s // bk, LANES, bk), lambda b, p, i: (b, 0, vt_block0 + p, 0)),
                  pl.BlockSpec((1, LANES), lambda b, p, i: (0, p)),
                  pl.BlockSpec((1, 1) + stats.shape[2:], lambda b, p, i: (b, p, 0, 0, 0))],
        out_specs=pl.BlockSpec((1, bq, LANES), lambda b, p, i: (b, i, p)),
        out_shape=jax.ShapeDtypeStruct((bsz, s, d_grp), BF16),
        scratch_shapes=scratch,
        compiler_params=_params("arbitrary", "arbitrary", "arbitrary"),
        name=name,
    )(qk, k_arr, vt, g.reshape(1, d_grp), stats)


ATT_LANE_GROUPS = 2 * (ATT_Q // ATT_COLS)


def _fox_scratch():
    n = ATT_LANE_GROUPS
    return [pltpu.VMEM((2, n, ATT_K, ATT_COLS), F32), pltpu.VMEM((2, n, 1, ATT_COLS), F32),
            pltpu.VMEM((2, n, ATT_K, ATT_COLS), BF16), pltpu.VMEM((n, HEAD_DIM, ATT_COLS), F32)]


def _sb_scratch():
    n = ATT_LANE_GROUPS
    return [pltpu.VMEM((2, n, ATT_K, ATT_COLS), F32), pltpu.VMEM((2, n, ATT_K, ATT_COLS), BF16),
            pltpu.VMEM((2, n, ATT_K, ATT_COLS), F32), pltpu.VMEM((2, n, 1, ATT_COLS), F32),
            pltpu.VMEM((n, HEAD_DIM, ATT_COLS), F32)]


def _outproj_kernel(x_ref, mf_ref, ms_ref, w_ref, mod_ref, g_ref, x1_ref, h2_ref):
    mix = jnp.concatenate([mf_ref[0], ms_ref[0]], axis=-1)
    x1 = x_ref[0] + mod_ref[0, 2:3, :] * _dot(mix, w_ref[...])
    x1_ref[0] = x1
    shift = mod_ref[0, 3:4, :]
    scale = mod_ref[0, 4:5, :]
    h2_ref[0] = (_rms_rows(x1) * g_ref[...] * (1.0 + scale) + shift).astype(BF16)


def _outproj(x, mix_f, mix_s, w_out, mod, g):
    bsz, s, d = x.shape
    tm = OUT_ROWS
    row = lambda b, i: (b, i, 0)
    return pl.pallas_call(
        _outproj_kernel,
        grid=(bsz, s // tm),
        in_specs=[pl.BlockSpec((1, tm, d), row),
                  pl.BlockSpec((1, tm, mix_f.shape[2]), row),
                  pl.BlockSpec((1, tm, mix_s.shape[2]), row),
                  pl.BlockSpec(w_out.shape, lambda b, i: (0, 0)),
                  pl.BlockSpec((1, N_MOD, d), lambda b, i: (b, 0, 0)),
                  pl.BlockSpec((1, d), lambda b, i: (0, 0))],
        out_specs=[pl.BlockSpec((1, tm, d), row), pl.BlockSpec((1, tm, d), row)],
        out_shape=[jax.ShapeDtypeStruct((bsz, s, d), F32), jax.ShapeDtypeStruct((bsz, s, d), BF16)],
        compiler_params=_params("arbitrary", "arbitrary"),
        name="outproj",
    )(x, mix_f, mix_s, w_out, mod, g)


def _mlp_kernel(h_ref, halo_ref, x1_ref, mod_ref, wu_ref, cw_ref, cb_ref, wd_ref, gf_ref, o_ref,
                u_buf, acc_ref, *, final_norm):
    i = pl.program_id(1)
    tm = h_ref.shape[1]
    n_chunks = wd_ref.shape[0]
    halo = halo_ref[0]
    halo = jnp.where(i > 0, halo, jnp.zeros_like(halo))
    hx = jnp.concatenate([halo, h_ref[0]], axis=0)
    acc_ref[...] = jnp.zeros_like(acc_ref)

    def project_up(c, slot):
        for br in range(2):
            u_buf[slot, br] = _dot(hx, wu_ref[br, c])

    def mix_down(c, slot):
        branches = []
        for br in range(2):
            out = cb_ref[br, c]
            for tap in range(CONV_WIDTH):
                first = BF16_SUBLANES - (CONV_WIDTH - 1 - tap)
                out = out + cw_ref[br, c, tap:tap + 1, :] * u_buf[slot, br, pl.ds(first, tm), :]
            branches.append(out)
        u_gate, u_val = branches
        acc_ref[...] += _dot((u_gate * jax.nn.sigmoid(u_gate) * u_val).astype(BF16), wd_ref[c])

    def chunk_pair(j, _):
        c = 2 * j
        project_up(c + 1, 1)
        mix_down(c, 0)
        project_up(c + 2, 0)
        mix_down(c + 1, 1)
        return 0

    project_up(0, 0)
    lax.fori_loop(0, (n_chunks - 1) // 2, chunk_pair, 0)
    mix_down(n_chunks - 1, 0)
    x2 = x1_ref[0] + mod_ref[0, 5:6, :] * acc_ref[...]
    o_ref[0] = _rms_rows(x2) * gf_ref[...] if final_norm else x2


def _mlp(h2, x1, mod, w_up, conv_w, conv_b, w_down, g_final, final_norm):
    bsz, s, d = x1.shape
    tm = OUT_ROWS
    n_chunks, tf = w_down.shape[0], w_down.shape[1]
    assert n_chunks % 2 == 1
    halo_blocks = tm // BF16_SUBLANES
    row = lambda b, i: (b, i, 0)
    resident = lambda a: pl.BlockSpec(a.shape, lambda b, i: (0,) * a.ndim, pipeline_mode=pl.Buffered(1))
    return pl.pallas_call(
        functools.partial(_mlp_kernel, final_norm=final_norm),
        grid=(bsz, s // tm),
        in_specs=[pl.BlockSpec((1, tm, d), row),
                  pl.BlockSpec((1, BF16_SUBLANES, d), lambda b, i: (b, jnp.maximum(i * halo_blocks - 1, 0), 0)),
                  pl.BlockSpec((1, tm, d), row),
                  pl.BlockSpec((1, N_MOD, d), lambda b, i: (b, 0, 0)),
                  resident(w_up), resident(conv_w), resident(conv_b), resident(w_down),
                  pl.BlockSpec((1, d), lambda b, i: (0, 0))],
        out_specs=pl.BlockSpec((1, tm, d), row),
        out_shape=jax.ShapeDtypeStruct((bsz, s, d), F32),
        scratch_shapes=[pltpu.VMEM((2, 2, tm + BF16_SUBLANES, tf), F32), pltpu.VMEM((tm, d), F32)],
        compiler_params=_params("arbitrary", "arbitrary"),
        name="mlp",
    )(h2, h2, x1, mod, w_up, conv_w, conv_b, w_down, g_final)


def _chunk_columns(a, d_ff, n_chunks, tf):
    halves = jnp.stack([a[:, :d_ff], a[:, d_ff:]])
    halves = jnp.pad(halves, ((0, 0), (0, 0), (0, n_chunks * tf - d_ff)))
    return halves.reshape(2, a.shape[0], n_chunks, tf).transpose(0, 2, 1, 3)


def _pad_cols(a, n):
    return jnp.pad(a, ((0, 0), (0, n - a.shape[1])))


def kernel(x, c, w_ada, b_ada, g_attn, w_in, b_fgate, g_out_fox, g_out_sb, w_out,
           g_mlp, w_up, conv_w, conv_b, w_down, g_final):
    depth, d, _ = w_ada.shape
    n_fox = b_fgate.shape[1]
    d_fox = n_fox * HEAD_DIM
    d_sb = g_out_sb.shape[1]
    n_sb = d_sb // HEAD_DIM
    d_ff = w_down.shape[1]
    d_ff_pad = -(-d_ff // FF_CHUNK) * FF_CHUNK
    assert n_fox % 2 == 0 and n_sb == n_fox and 3 * n_fox <= LANES
    assert x.shape[1] % OUT_ROWS == 0 and x.shape[1] % ATT_Q == 0 and ATT_Q == 2 * ATT_K
    o_kf, o_vf, o_qs, o_ks, o_vs, o_gate = (d_fox, 2 * d_fox, 3 * d_fox, 3 * d_fox + d_sb,
                                             3 * d_fox + 2 * d_sb, 3 * d_fox + 3 * d_sb)

    for l in range(depth):
        mod = _ada(c, w_ada[l], b_ada[l]).reshape(-1, N_MOD, d)
        w = w_in[l]
        w_nat = jnp.concatenate([w[:, :o_vf], w[:, o_qs:o_vs]], axis=1).astype(BF16)
        w_vt = jnp.concatenate([w[:, o_vf:o_qs], w[:, o_vs:o_gate]], axis=1).T.astype(BF16)
        w_gate = _pad_cols(w[:, o_gate:], LANES).astype(BF16)
        b_gate = _pad_cols(b_fgate[l].reshape(1, n_fox), LANES)
        qk, vt, log_f = _inproj(x, mod, g_attn[l].reshape(1, d), w_nat, w_vt, w_gate, b_gate)

        pairs_f, pairs_s = n_fox // 2, n_sb // 2
        k_aug, stats = _decay(log_f, qk, n_fox, k_fox_block=1, k_sb_block=3)
        fox_k_spec = pl.BlockSpec((1, 2, x.shape[1], LANES), lambda b, p, i: (b, p, 0, 0))
        mix_f = _attention(_fox_kernel, "fox", _fox_scratch(), qk, k_aug, fox_k_spec, vt, g_out_fox[l], stats,
                           q_block0=0, vt_block0=0, n_heads=n_fox)
        sb_k_spec = pl.BlockSpec((1, x.shape[1], LANES), lambda b, p, i: (b, 0, 2 * pairs_f + pairs_s + p))
        mix_s = _attention(_sb_kernel, "sb", _sb_scratch(), qk, qk, sb_k_spec, vt, g_out_sb[l], stats,
                           q_block0=2 * pairs_f, vt_block0=pairs_f, n_heads=n_sb)

        x1, h2 = _outproj(x, mix_f, mix_s, w_out[l].astype(BF16), mod, g_mlp[l].reshape(1, d))

        n_ff = d_ff_pad // FF_CHUNK
        x = _mlp(h2, x1, mod,
                 _chunk_columns(w_up[l], d_ff, n_ff, FF_CHUNK).astype(BF16),
                 _chunk_columns(conv_w[l], d_ff, n_ff, FF_CHUNK),
                 _chunk_columns(conv_b[l].reshape(1, -1), d_ff, n_ff, FF_CHUNK),
                 jnp.pad(w_down[l], ((0, d_ff_pad - d_ff), (0, 0))).astype(BF16).reshape(n_ff, FF_CHUNK, d),
                 g_final.reshape(1, d), final_norm=(l == depth - 1))
    return x
```

```python
import functools

import numpy as np
import jax
import jax.numpy as jnp
from jax import lax
from jax.experimental import pallas as pl
from jax.experimental.pallas import tpu as pltpu

HEAD_DIM = 64
N_MOD = 6
CONV_WIDTH = 3
EPS = 1e-6

LANES = 128
BF16_SUBLANES = 16
VMEM_LIMIT_BYTES = 48 * 1024 * 1024

ATT_Q = 512
ATT_K = 256
ATT_COLS = 256
ATT_PAIRS = 2
PROJ_ROWS = ATT_K
LOG2E = 1.4426950408889634
MASKED = -1e30
M_INIT = -1e29
PRUNE_LOG2 = 160.0
NORM_SLACK = 1.02
STATS_ROWS = 8
OUT_ROWS = 512
FF_CHUNK = 256

F32 = jnp.float32
BF16 = jnp.bfloat16
NT_DIMS = (((1,), (1,)), ((), ()))


def _dot(a, b):
    return jnp.dot(a, b, preferred_element_type=F32)


def _dot_nt(a, b):
    return lax.dot_general(a, b, NT_DIMS, preferred_element_type=F32)


def _params(*sem):
    return pltpu.CompilerParams(dimension_semantics=sem, vmem_limit_bytes=VMEM_LIMIT_BYTES)


def _rms_rows(x):
    return x * lax.rsqrt(jnp.mean(x * x, axis=-1, keepdims=True) + EPS)


def _softplus(z):
    return jnp.maximum(z, 0.0) + jnp.log(1.0 + jnp.exp(-jnp.abs(z)))


def _split3(x):
    hi = x.astype(BF16)
    r1 = x - hi.astype(F32)
    mid = r1.astype(BF16)
    lo = (r1 - mid.astype(F32)).astype(BF16)
    return hi, mid, lo


def _ada_kernel(c_ref, w_ref, b_ref, o_ref):
    c = c_ref[...]
    o_ref[...] = _dot(c * jax.nn.sigmoid(c), w_ref[...]) + b_ref[...]


def _ada(c, w, b):
    bsz, d = c.shape
    n = w.shape[1]
    return pl.pallas_call(
        _ada_kernel,
        grid=(n // d,),
        in_specs=[pl.BlockSpec((bsz, d), lambda j: (0, 0)),
                  pl.BlockSpec((d, d), lambda j: (0, j)),
                  pl.BlockSpec((1, d), lambda j: (0, j))],
        out_specs=pl.BlockSpec((bsz, d), lambda j: (0, j)),
        out_shape=jax.ShapeDtypeStruct((bsz, n), F32),
        compiler_params=_params("arbitrary"),
        name="ada",
    )(c, w, b.reshape(1, n))


def _inproj_kernel(x_ref, mod_ref, g_ref, wn_ref, wvt_ref, wg_ref, bg_ref, qk_ref, vt_ref, lf_ref):
    shift = mod_ref[0, 0:1, :]
    scale = mod_ref[0, 1:2, :]
    h = (_rms_rows(x_ref[0]) * g_ref[...] * (1.0 + scale) + shift).astype(BF16)
    qk_ref[0] = _dot(h, wn_ref[...]).astype(BF16)
    vt_ref[0, 0] = _dot_nt(wvt_ref[...], h).astype(BF16)
    logit = _dot(h, wg_ref[...]) + bg_ref[...]
    lf_ref[0] = -_softplus(-logit)


def _inproj(x, mod, g, w_nat, w_vt, w_gate, b_gate):
    bsz, s, d = x.shape
    tm = PROJ_ROWS
    n_nat, n_v = w_nat.shape[1], w_vt.shape[0]
    const = lambda b, i: (0, 0)
    return pl.pallas_call(
        _inproj_kernel,
        grid=(bsz, s // tm),
        in_specs=[pl.BlockSpec((1, tm, d), lambda b, i: (b, i, 0)),
                  pl.BlockSpec((1, N_MOD, d), lambda b, i: (b, 0, 0)),
                  pl.BlockSpec((1, d), const),
                  pl.BlockSpec((d, n_nat), const),
                  pl.BlockSpec((n_v, d), const),
                  pl.BlockSpec((d, LANES), const),
                  pl.BlockSpec((1, LANES), const)],
        out_specs=[pl.BlockSpec((1, tm, n_nat), lambda b, i: (b, i, 0)),
                   pl.BlockSpec((1, 1, n_v, tm), lambda b, i: (b, i, 0, 0)),
                   pl.BlockSpec((1, tm, LANES), lambda b, i: (b, i, 0))],
        out_shape=[jax.ShapeDtypeStruct((bsz, s, n_nat), BF16),
                   jax.ShapeDtypeStruct((bsz, s // tm, n_v, tm), BF16),
                   jax.ShapeDtypeStruct((bsz, s, LANES), F32)],
        compiler_params=_params("arbitrary", "arbitrary"),
        name="inproj",
    )(x, mod, g, w_nat, w_vt, w_gate, b_gate)


def _decay_kernel(lf_ref, k_ref, ks_ref, sel_ref, ind_ref, kaug_ref, stats_ref, carry_ref, kpre_ref, *, n_heads):
    @pl.when(pl.program_id(1) == 0)
    def _():
        carry_ref[...] = jnp.zeros_like(carry_ref)
        kpre_ref[...] = jnp.zeros_like(kpre_ref)

    tm = lf_ref.shape[1]
    lane = lax.broadcasted_iota(jnp.int32, (tm, LANES), 1)
    lf = jnp.where(lane < n_heads, lf_ref[0], 0.0)
    row = lax.broadcasted_iota(jnp.int32, (tm, tm), 0)
    col = lax.broadcasted_iota(jnp.int32, (tm, tm), 1)
    tri = (col <= row).astype(BF16)
    hi, mid, lo = _split3(lf)
    f_run = carry_ref[...] + (_dot(tri, hi) + _dot(tri, mid) + _dot(tri, lo))
    carry_ref[...] = f_run[tm - 1:tm, :]
    ghi, gmid, glo = _split3(-LOG2E * f_run)
    packed = (ghi.astype(F32) + pltpu.roll(gmid.astype(F32), n_heads, 1)
              + pltpu.roll(glo.astype(F32), 2 * n_heads, 1)).astype(BF16)
    placed = _dot(packed, sel_ref[...])
    k_all = k_ref[0]
    for h in range(n_heads):
        k_pair = k_all[:, (h // 2) * LANES:(h // 2 + 1) * LANES]
        own = (lane < HEAD_DIM) if h % 2 == 0 else (lane >= HEAD_DIM)
        kaug_ref[0, h] = jnp.where(own, k_pair, placed[:, h * LANES:(h + 1) * LANES].astype(BF16))

    def head_norm_bound(k):
        k32 = k.astype(F32)
        sq = _dot((k32 * k32).astype(BF16), ind_ref[...])
        return jnp.sqrt(jnp.max(sq, axis=0, keepdims=True) * NORM_SLACK)

    kpre_f = jnp.maximum(kpre_ref[0:1, :], head_norm_bound(k_all))
    kpre_s = jnp.maximum(kpre_ref[1:2, :], head_norm_bound(ks_ref[0]))
    kpre_ref[0:1, :] = kpre_f
    kpre_ref[1:2, :] = kpre_s
    g_end = -LOG2E * f_run[tm - 1:tm, :]
    lane1 = lax.broadcasted_iota(jnp.int32, (1, LANES), 1)

    def spread(v, h):
        return jnp.broadcast_to(jnp.sum(jnp.where(lane1 == h, v, 0.0), axis=1, keepdims=True), (1, ATT_COLS))

    for p in range(n_heads // 2):
        rows = [spread(v, 2 * p + hh) for v in (kpre_f, g_end, kpre_s) for hh in range(2)]
        rows += [jnp.zeros((1, ATT_COLS), F32)] * (stats_ref.shape[3] - len(rows))
        stats_ref[0, p, 0] = jnp.concatenate(rows, axis=0)


def _head_indicator(n_heads):
    ind = np.zeros((n_heads * HEAD_DIM, LANES), np.float32)
    ind[np.arange(n_heads * HEAD_DIM), np.arange(n_heads * HEAD_DIM) // HEAD_DIM] = 1.0
    return jnp.asarray(ind, BF16)


def _decay_select_matrix(n_heads):
    sel = np.zeros((LANES, n_heads * LANES), np.float32)
    for h in range(n_heads):
        base = h * LANES + (HEAD_DIM if h % 2 == 0 else 0)
        for term in range(3):
            sel[term * n_heads + h, base + term] = 1.0
    return jnp.asarray(sel, BF16)


def _decay(log_f, qk, n_heads, k_fox_block, k_sb_block):
    bsz, s, _ = log_f.shape
    tm = PROJ_ROWS
    d_grp = n_heads * HEAD_DIM
    return pl.pallas_call(
        functools.partial(_decay_kernel, n_heads=n_heads),
        grid=(bsz, s // tm),
        in_specs=[pl.BlockSpec((1, tm, LANES), lambda b, i: (b, i, 0)),
                  pl.BlockSpec((1, tm, d_grp), lambda b, i: (b, i, k_fox_block)),
                  pl.BlockSpec((1, tm, d_grp), lambda b, i: (b, i, k_sb_block)),
                  pl.BlockSpec((LANES, n_heads * LANES), lambda b, i: (0, 0)),
                  pl.BlockSpec((d_grp, LANES), lambda b, i: (0, 0))],
        out_specs=[pl.BlockSpec((1, n_heads, tm, LANES), lambda b, i: (b, 0, i, 0)),
                   pl.BlockSpec((1, n_heads // 2, 1, STATS_ROWS, ATT_COLS), lambda b, i: (b, 0, i, 0, 0))],
        out_shape=[jax.ShapeDtypeStruct((bsz, n_heads, s, LANES), BF16),
                   jax.ShapeDtypeStruct((bsz, n_heads // 2, s // tm, STATS_ROWS, ATT_COLS), F32)],
        scratch_shapes=[pltpu.VMEM((1, LANES), F32), pltpu.VMEM((2, LANES), F32)],
        compiler_params=_params("arbitrary", "arbitrary"),
        name="decay",
    )(log_f, qk, qk, _decay_select_matrix(n_heads), _head_indicator(n_heads))


def _lane_queries(q_ref, extra_even, extra_odd, cw):
    out = []
    for pp in range(q_ref.shape[2] // LANES):
        q = q_ref[0, :, pp * LANES:(pp + 1) * LANES].astype(F32) * (HEAD_DIM ** -0.5 * LOG2E)
        lane = lax.broadcasted_iota(jnp.int32, q.shape, 1)
        heads = (jnp.where(lane < HEAD_DIM, q, extra_even(lane)).T.astype(BF16),
                 jnp.where(lane >= HEAD_DIM, q, extra_odd(lane)).T.astype(BF16))
        out += [heads[hh][:, c * cw:(c + 1) * cw] for hh in range(2) for c in range(q.shape[0] // cw)]
    return out


def _visibility(first_key, first_query, bk, cw, strict):
    last_visible_gap = -1 if strict else 0
    if first_key + bk - 1 - first_query <= last_visible_gap:
        return "all"
    if first_key - (first_query + cw - 1) > last_visible_gap:
        return "none"
    gap = (lax.broadcasted_iota(jnp.int32, (bk, cw), 0) - lax.broadcasted_iota(jnp.int32, (bk, cw), 1)
           + (first_key - first_query))
    return gap <= last_visible_gap


def _diag_visibility(u, c, bk, cw, strict):
    return _visibility((1 - u) * bk, c * cw, bk, cw, strict)


def _hidden(visibility):
    return isinstance(visibility, str) and visibility == "none"


def _query_norm_bounds(queries, n_chunks):
    bounds = []
    for li, q in enumerate(queries):
        hh = (li // n_chunks) % 2
        own = q[hh * HEAD_DIM:(hh + 1) * HEAD_DIM, :].astype(F32)
        bounds.append(jnp.sqrt(jnp.sum(own * own, axis=0, keepdims=True) * NORM_SLACK))
    return bounds


def _finish_heads(lanes, g_ref, o_ref):
    n_pairs = o_ref.shape[2] // LANES
    n_chunks = len(lanes) // (2 * n_pairs)
    for pp in range(n_pairs):
        mine = lanes[2 * pp * n_chunks:2 * (pp + 1) * n_chunks]
        outs = [jnp.concatenate(mine[hh * n_chunks:(hh + 1) * n_chunks], axis=1) for hh in range(2)]
        normed = [o * lax.rsqrt(jnp.mean(o * o, axis=0, keepdims=True) + EPS) for o in outs]
        cols = slice(pp * LANES, (pp + 1) * LANES)
        o_ref[0, :, cols] = (jnp.concatenate(normed, axis=0).T * g_ref[:, cols]).astype(o_ref.dtype)


def _fox_kernel(q_ref, k_ref, vt_ref, g_ref, stats_ref, o_ref, s_buf, cmax_buf, p_buf, acc_buf):
    qi = pl.program_id(2)
    bk = vt_ref.shape[3]
    n_lanes, cw = acc_buf.shape[0], acc_buf.shape[2]
    n_chunks = q_ref.shape[1] // cw
    lane_group = lambda li: (li // (2 * n_chunks), (li // n_chunks) % 2, li % n_chunks)
    n_tiles = 2 * (qi + 1)
    ones3 = lambda lo: (lambda lane: jnp.where((lane >= lo) & (lane < lo + 3), 1.0, 0.0))
    queries = _lane_queries(q_ref, ones3(HEAD_DIM), ones3(0), cw)
    acc_buf[...] = jnp.zeros(acc_buf.shape, F32)
    for li in range(n_lanes):
        if _hidden(_diag_visibility(0, lane_group(li)[2], bk, cw, strict=False)):
            p_buf[0, li] = jnp.zeros((bk, cw), BF16)

    def step(t, slot, carry, score="below", softmax="below", value=True):
        new = []
        for li in range(n_lanes):
            pp, hh, c = lane_group(li)
            see = lambda u: "all" if u == "below" else _diag_visibility(u, c, bk, cw, strict=False)
            if score is not None and not _hidden(see(score)):
                start = pl.multiple_of((n_tiles - 2 - t) * bk, bk)
                s_new = _dot(k_ref[0, 2 * pp + hh, pl.ds(start, bk), :], queries[li])
                if not isinstance(see(score), str):
                    s_new = jnp.where(see(score), s_new, MASKED)
                s_buf[1 - slot, li] = s_new
                cmax_buf[1 - slot, li] = jnp.max(s_new, axis=0, keepdims=True)
            pv = None
            if value:
                vt = vt_ref[0, n_tiles - t, pl.ds(pp * LANES + hh * HEAD_DIM, HEAD_DIM), :]
                pv = _dot(vt, p_buf[1 - slot, li])
            m, l = carry[li]
            if softmax is not None and not _hidden(see(softmax)):
                m_new = jnp.maximum(m, cmax_buf[slot, li])
                alpha = jnp.exp2(m - m_new)
                p = jnp.exp2(s_buf[slot, li] - m_new)
                p_buf[slot, li] = p.astype(BF16)
                m, l = m_new, alpha * l + jnp.sum(p, axis=0, keepdims=True)
                acc_buf[li] = alpha * (acc_buf[li] if pv is None else acc_buf[li] + pv)
            elif pv is not None:
                acc_buf[li] += pv
            new.append((m, l))
        return tuple(new)

    def step_pair(i, carry):
        t = 2 * i + 1
        return step(t + 1, 0, step(t, 1, carry))

    q_norm = _query_norm_bounds(queries, n_chunks)

    def later_tiles_matter(i, carry):
        j_rest = jnp.maximum(n_tiles - 5 - 2 * i, 0)
        worst = None
        for li in range(n_lanes):
            pp, hh, _ = lane_group(li)
            bound = (q_norm[li] * stats_ref[0, pp, j_rest, hh:hh + 1, :]
                     + stats_ref[0, pp, j_rest, 2 + hh:3 + hh, :] - carry[li][0])
            worst = bound if worst is None else jnp.maximum(worst, bound)
        return jnp.max(worst) >= -PRUNE_LOG2

    def pair_and_check(state):
        i, _, carry = state
        carry = step_pair(i, carry)
        return i + 1, later_tiles_matter(i, carry), carry

    carry = tuple((jnp.full((1, cw), M_INIT, F32), jnp.zeros((1, cw), F32)) for _ in range(n_lanes))
    carry = step(-1, 1, carry, score=0, softmax=None, value=False)
    carry = step(0, 0, carry, score=1, softmax=0, value=False)
    n_pairs, _, carry = lax.while_loop(lambda st: (st[0] < qi) & st[1], pair_and_check,
                                       (jnp.int32(0), jnp.bool_(True), carry))
    carry = step(2 * n_pairs + 1, 1, carry, score=None)
    carry = step(2 * n_pairs + 2, 0, carry, score=None, softmax=None)
    _finish_heads([acc_buf[li] / carry[li][1] for li in range(n_lanes)], g_ref, o_ref)


def _sb_kernel(q_ref, k_ref, vt_ref, g_ref, stats_ref, o_ref, z_buf, sp_buf, e_buf, wrow_buf, acc_buf):
    qi = pl.program_id(2)
    bk = vt_ref.shape[3]
    n_lanes, cw = acc_buf.shape[0], acc_buf.shape[2]
    n_chunks = q_ref.shape[1] // cw
    lane_group = lambda li: (li // (2 * n_chunks), (li // n_chunks) % 2, li % n_chunks)
    n_tiles = 2 * (qi + 1)
    zero = lambda lane: 0.0
    queries = _lane_queries(q_ref, zero, zero, cw)
    suffix = (lax.broadcasted_iota(jnp.int32, (bk, bk), 1)
              >= lax.broadcasted_iota(jnp.int32, (bk, bk), 0)).astype(BF16)
    e_buf[1] = jnp.full(e_buf.shape[1:], MASKED, F32)
    wrow_buf[1] = jnp.zeros(wrow_buf.shape[1:], F32)
    acc_buf[...] = jnp.zeros(acc_buf.shape, F32)
    for li in range(n_lanes):
        if _hidden(_diag_visibility(0, lane_group(li)[2], bk, cw, strict=True)):
            z_buf[0, li] = jnp.full((bk, cw), MASKED, F32)
            sp_buf[0, li] = jnp.zeros((bk, cw), BF16)

    def step(t, slot, later, score="below", softplus="below", cumsum=True, weight=True):
        new_later = []
        for li in range(n_lanes):
            pp, hh, c = lane_group(li)
            see = lambda u: "all" if u == "below" else _diag_visibility(u, c, bk, cw, strict=True)
            if cumsum:
                within = _dot(suffix, sp_buf[1 - slot, li])
                e_buf[1 - slot, li] = z_buf[1 - slot, li] - within
                wrow_buf[1 - slot, li] = within[0:1, :]
            if score is not None and not _hidden(see(score)):
                start = pl.multiple_of((n_tiles - 1 - (t + 3)) * bk, bk)
                z_new = _dot(k_ref[0, pl.ds(start, bk), pp * LANES:(pp + 1) * LANES], queries[li])
                if not isinstance(see(score), str):
                    z_new = jnp.where(see(score), z_new, MASKED)
                z_buf[1 - slot, li] = z_new
            if weight:
                a = jnp.exp2(e_buf[slot, li] - later[li])
                vt = vt_ref[0, n_tiles - 1 - jnp.maximum(t, 0), pl.ds(pp * LANES + hh * HEAD_DIM, HEAD_DIM), :]
                acc_buf[li] += _dot(vt, a.astype(BF16))
                new_later.append(later[li] + wrow_buf[slot, li])
            else:
                new_later.append(later[li])
            if softplus is not None and not _hidden(see(softplus)):
                z = z_buf[slot, li]
                sp_buf[slot, li] = (jnp.maximum(z, 0.0) + jnp.log2(1.0 + jnp.exp2(-jnp.abs(z)))).astype(BF16)
        return tuple(new_later)

    def step_pair(i, later):
        t = 2 * i - 1
        return step(t + 1, 0, step(t, 1, later))

    q_norm = _query_norm_bounds(queries, n_chunks)

    def later_tiles_matter(i, later):
        j_rest = jnp.maximum(n_tiles - 5 - 2 * i, 0)
        worst = None
        for li in range(n_lanes):
            pp, hh, _ = lane_group(li)
            bound = q_norm[li] * stats_ref[0, pp, j_rest, 4 + hh:5 + hh, :] - (later[li] + wrow_buf[1, li])
            worst = bound if worst is None else jnp.maximum(worst, bound)
        return jnp.max(worst) >= -PRUNE_LOG2

    def pair_and_check(state):
        i, _, later = state
        later = step_pair(i, later)
        return i + 1, later_tiles_matter(i, later), later

    later = tuple(jnp.zeros((1, cw), F32) for _ in range(n_lanes))
    later = step(-3, 1, later, score=0, softplus=None, cumsum=False, weight=False)
    later = step(-2, 0, later, score=1, softplus=0, cumsum=False, weight=False)
    n_pairs, _, later = lax.while_loop(lambda st: (st[0] < qi) & st[1], pair_and_check,
                                       (jnp.int32(0), jnp.bool_(True), later))
    later = step(2 * n_pairs - 1, 1, later, score=None)
    later = step(2 * n_pairs, 0, later, score=None, softplus=None)
    later = step(2 * n_pairs + 1, 1, later, score=None, softplus=None, cumsum=False)
    _finish_heads([acc_buf[li] for li in range(n_lanes)], g_ref, o_ref)


def _attention(body, name, scratch, qk, k_arr, k_spec, vt, g, stats, q_block0, vt_block0, n_heads):
    bsz, s, _ = qk.shape
    bq, bk, width = ATT_Q, ATT_K, ATT_PAIRS * LANES
    d_grp = n_heads * HEAD_DIM
    assert (n_heads // 2) % ATT_PAIRS == 0
    return pl.pallas_call(
        body,
        grid=(bsz, n_heads // 2 // ATT_PAIRS, s // bq),
        in_specs=[pl.BlockSpec((1, bq, width), lambda b, p, i: (b, i, q_block0 + p)),
                  k_spec,
                  pl.BlockSpec((1, s // bk, width, bk), lambda b, p, i: (b, 0, vt_block0 + p, 0)),
                  pl.BlockSpec((1, width), lambda b, p, i: (0, p)),
                  pl.BlockSpec((1, ATT_PAIRS) + stats.shape[2:], lambda b, p, i: (b, p, 0, 0, 0))],
        out_specs=pl.BlockSpec((1, bq, width), lambda b, p, i: (b, i, p)),
        out_shape=jax.ShapeDtypeStruct((bsz, s, d_grp), BF16),
        scratch_shapes=scratch,
        compiler_params=_params("arbitrary", "arbitrary", "arbitrary"),
        name=name,
    )(qk, k_arr, vt, g.reshape(1, d_grp), stats)


ATT_LANE_GROUPS = ATT_PAIRS * 2 * (ATT_Q // ATT_COLS)


def _fox_scratch():
    n = ATT_LANE_GROUPS
    return [pltpu.VMEM((2, n, ATT_K, ATT_COLS), F32), pltpu.VMEM((2, n, 1, ATT_COLS), F32),
            pltpu.VMEM((2, n, ATT_K, ATT_COLS), BF16), pltpu.VMEM((n, HEAD_DIM, ATT_COLS), F32)]


def _sb_scratch():
    n = ATT_LANE_GROUPS
    return [pltpu.VMEM((2, n, ATT_K, ATT_COLS), F32), pltpu.VMEM((2, n, ATT_K, ATT_COLS), BF16),
            pltpu.VMEM((2, n, ATT_K, ATT_COLS), F32), pltpu.VMEM((2, n, 1, ATT_COLS), F32),
            pltpu.VMEM((n, HEAD_DIM, ATT_COLS), F32)]


def _outproj_kernel(x_ref, mf_ref, ms_ref, w_ref, mod_ref, g_ref, x1_ref, h2_ref):
    mix = jnp.concatenate([mf_ref[0], ms_ref[0]], axis=-1)
    x1 = x_ref[0] + mod_ref[0, 2:3, :] * _dot(mix, w_ref[...])
    x1_ref[0] = x1
    shift = mod_ref[0, 3:4, :]
    scale = mod_ref[0, 4:5, :]
    h2_ref[0] = (_rms_rows(x1) * g_ref[...] * (1.0 + scale) + shift).astype(BF16)


def _outproj(x, mix_f, mix_s, w_out, mod, g):
    bsz, s, d = x.shape
    tm = OUT_ROWS
    row = lambda b, i: (b, i, 0)
    return pl.pallas_call(
        _outproj_kernel,
        grid=(bsz, s // tm),
        in_specs=[pl.BlockSpec((1, tm, d), row),
                  pl.BlockSpec((1, tm, mix_f.shape[2]), row),
                  pl.BlockSpec((1, tm, mix_s.shape[2]), row),
                  pl.BlockSpec(w_out.shape, lambda b, i: (0, 0)),
                  pl.BlockSpec((1, N_MOD, d), lambda b, i: (b, 0, 0)),
                  pl.BlockSpec((1, d), lambda b, i: (0, 0))],
        out_specs=[pl.BlockSpec((1, tm, d), row), pl.BlockSpec((1, tm, d), row)],
        out_shape=[jax.ShapeDtypeStruct((bsz, s, d), F32), jax.ShapeDtypeStruct((bsz, s, d), BF16)],
        compiler_params=_params("arbitrary", "arbitrary"),
        name="outproj",
    )(x, mix_f, mix_s, w_out, mod, g)


def _mlp_kernel(h_ref, halo_ref, x1_ref, mod_ref, wu_ref, cw_ref, cb_ref, wd_ref, gf_ref, o_ref,
                u_buf, acc_ref, *, final_norm):
    i = pl.program_id(1)
    tm = h_ref.shape[1]
    n_chunks = wd_ref.shape[0]
    halo = halo_ref[0]
    halo = jnp.where(i > 0, halo, jnp.zeros_like(halo))
    hx = jnp.concatenate([halo, h_ref[0]], axis=0)
    acc_ref[...] = jnp.zeros_like(acc_ref)

    def project_up(c, slot):
        for br in range(2):
            u_buf[slot, br] = _dot(hx, wu_ref[br, c])

    def mix_down(c, slot):
        branches = []
        for br in range(2):
            out = cb_ref[br, c]
            for tap in range(CONV_WIDTH):
                first = BF16_SUBLANES - (CONV_WIDTH - 1 - tap)
                out = out + cw_ref[br, c, tap:tap + 1, :] * u_buf[slot, br, pl.ds(first, tm), :]
            branches.append(out)
        u_gate, u_val = branches
        acc_ref[...] += _dot((u_gate * jax.nn.sigmoid(u_gate) * u_val).astype(BF16), wd_ref[c])

    def chunk_pair(j, _):
        c = 2 * j
        project_up(c + 1, 1)
        mix_down(c, 0)
        project_up(c + 2, 0)
        mix_down(c + 1, 1)
        return 0

    project_up(0, 0)
    lax.fori_loop(0, (n_chunks - 1) // 2, chunk_pair, 0)
    mix_down(n_chunks - 1, 0)
    x2 = x1_ref[0] + mod_ref[0, 5:6, :] * acc_ref[...]
    o_ref[0] = _rms_rows(x2) * gf_ref[...] if final_norm else x2


def _mlp(h2, x1, mod, w_up, conv_w, conv_b, w_down, g_final, final_norm):
    bsz, s, d = x1.shape
    tm = OUT_ROWS
    n_chunks, tf = w_down.shape[0], w_down.shape[1]
    assert n_chunks % 2 == 1
    halo_blocks = tm // BF16_SUBLANES
    row = lambda b, i: (b, i, 0)
    resident = lambda a: pl.BlockSpec(a.shape, lambda b, i: (0,) * a.ndim, pipeline_mode=pl.Buffered(1))
    return pl.pallas_call(
        functools.partial(_mlp_kernel, final_norm=final_norm),
        grid=(bsz, s // tm),
        in_specs=[pl.BlockSpec((1, tm, d), row),
                  pl.BlockSpec((1, BF16_SUBLANES, d), lambda b, i: (b, jnp.maximum(i * halo_blocks - 1, 0), 0)),
                  pl.BlockSpec((1, tm, d), row),
                  pl.BlockSpec((1, N_MOD, d), lambda b, i: (b, 0, 0)),
                  resident(w_up), resident(conv_w), resident(conv_b), resident(w_down),
                  pl.BlockSpec((1, d), lambda b, i: (0, 0))],
        out_specs=pl.BlockSpec((1, tm, d), row),
        out_shape=jax.ShapeDtypeStruct((bsz, s, d), F32),
        scratch_shapes=[pltpu.VMEM((2, 2, tm + BF16_SUBLANES, tf), F32), pltpu.VMEM((tm, d), F32)],
        compiler_params=_params("arbitrary", "arbitrary"),
        name="mlp",
    )(h2, h2, x1, mod, w_up, conv_w, conv_b, w_down, g_final)


def _chunk_columns(a, d_ff, n_chunks, tf):
    halves = jnp.stack([a[:, :d_ff], a[:, d_ff:]])
    halves = jnp.pad(halves, ((0, 0), (0, 0), (0, n_chunks * tf - d_ff)))
    return halves.reshape(2, a.shape[0], n_chunks, tf).transpose(0, 2, 1, 3)


def _pad_cols(a, n):
    return jnp.pad(a, ((0, 0), (0, n - a.shape[1])))


def kernel(x, c, w_ada, b_ada, g_attn, w_in, b_fgate, g_out_fox, g_out_sb, w_out,
           g_mlp, w_up, conv_w, conv_b, w_down, g_final):
    depth, d, _ = w_ada.shape
    n_fox = b_fgate.shape[1]
    d_fox = n_fox * HEAD_DIM
    d_sb = g_out_sb.shape[1]
    n_sb = d_sb // HEAD_DIM
    d_ff = w_down.shape[1]
    d_ff_pad = -(-d_ff // FF_CHUNK) * FF_CHUNK
    assert n_fox % 2 == 0 and n_sb == n_fox and 3 * n_fox <= LANES
    assert x.shape[1] % OUT_ROWS == 0 and x.shape[1] % ATT_Q == 0 and ATT_Q == 2 * ATT_K
    o_kf, o_vf, o_qs, o_ks, o_vs, o_gate = (d_fox, 2 * d_fox, 3 * d_fox, 3 * d_fox + d_sb,
                                             3 * d_fox + 2 * d_sb, 3 * d_fox + 3 * d_sb)

    for l in range(depth):
        mod = _ada(c, w_ada[l], b_ada[l]).reshape(-1, N_MOD, d)
        w = w_in[l]
        w_nat = jnp.concatenate([w[:, :o_vf], w[:, o_qs:o_vs]], axis=1).astype(BF16)
        w_vt = jnp.concatenate([w[:, o_vf:o_qs], w[:, o_vs:o_gate]], axis=1).T.astype(BF16)
        w_gate = _pad_cols(w[:, o_gate:], LANES).astype(BF16)
        b_gate = _pad_cols(b_fgate[l].reshape(1, n_fox), LANES)
        qk, vt, log_f = _inproj(x, mod, g_attn[l].reshape(1, d), w_nat, w_vt, w_gate, b_gate)

        steps_per_group = n_fox // 2 // ATT_PAIRS
        k_aug, stats = _decay(log_f, qk, n_fox, k_fox_block=1, k_sb_block=3)
        fox_k_spec = pl.BlockSpec((1, 2 * ATT_PAIRS, x.shape[1], LANES), lambda b, p, i: (b, p, 0, 0))
        mix_f = _attention(_fox_kernel, "fox", _fox_scratch(), qk, k_aug, fox_k_spec, vt, g_out_fox[l], stats,
                           q_block0=0, vt_block0=0, n_heads=n_fox)
        sb_k_spec = pl.BlockSpec((1, x.shape[1], ATT_PAIRS * LANES), lambda b, p, i: (b, 0, 3 * steps_per_group + p))
        mix_s = _attention(_sb_kernel, "sb", _sb_scratch(), qk, qk, sb_k_spec, vt, g_out_sb[l], stats,
                           q_block0=2 * steps_per_group, vt_block0=steps_per_group, n_heads=n_sb)

        x1, h2 = _outproj(x, mix_f, mix_s, w_out[l].astype(BF16), mod, g_mlp[l].reshape(1, d))

        n_ff = d_ff_pad // FF_CHUNK
        x = _mlp(h2, x1, mod,
                 _chunk_columns(w_up[l], d_ff, n_ff, FF_CHUNK).astype(BF16),
                 _chunk_columns(conv_w[l], d_ff, n_ff, FF_CHUNK),
                 _chunk_columns(conv_b[l].reshape(1, -1), d_ff, n_ff, FF_CHUNK),
                 jnp.pad(w_down[l], ((0, d_ff_pad - d_ff), (0, 0))).astype(BF16).reshape(n_ff, FF_CHUNK, d),
                 g_final.reshape(1, d), final_norm=(l == depth - 1))
    return x
```

```python
import functools

import numpy as np
import jax
import jax.numpy as jnp
from jax import lax
from jax.experimental import pallas as pl
from jax.experimental.pallas import tpu as pltpu

HEAD_DIM = 64
N_MOD = 6
CONV_WIDTH = 3
EPS = 1e-6

LANES = 128
BF16_SUBLANES = 16
VMEM_LIMIT_BYTES = 48 * 1024 * 1024

ATT_Q = 512
ATT_K = 256
ATT_COLS = 256
FOX_PAIRS = 1
SB_PAIRS = 2
PROJ_ROWS = ATT_K
LOG2E = 1.4426950408889634
MASKED = -1e30
M_INIT = -1e29
PRUNE_LOG2 = 160.0
NORM_SLACK = 1.02
STATS_ROWS = 8
OUT_ROWS = 512
FF_CHUNK = 256

F32 = jnp.float32
BF16 = jnp.bfloat16
NT_DIMS = (((1,), (1,)), ((), ()))


def _dot(a, b):
    return jnp.dot(a, b, preferred_element_type=F32)


def _dot_nt(a, b):
    return lax.dot_general(a, b, NT_DIMS, preferred_element_type=F32)


def _params(*sem):
    return pltpu.CompilerParams(dimension_semantics=sem, vmem_limit_bytes=VMEM_LIMIT_BYTES)


def _rms_rows(x):
    return x * lax.rsqrt(jnp.mean(x * x, axis=-1, keepdims=True) + EPS)


def _softplus(z):
    return jnp.maximum(z, 0.0) + jnp.log(1.0 + jnp.exp(-jnp.abs(z)))


def _split3(x):
    hi = x.astype(BF16)
    r1 = x - hi.astype(F32)
    mid = r1.astype(BF16)
    lo = (r1 - mid.astype(F32)).astype(BF16)
    return hi, mid, lo


def _ada_kernel(c_ref, w_ref, b_ref, o_ref):
    c = c_ref[...]
    o_ref[...] = _dot(c * jax.nn.sigmoid(c), w_ref[...]) + b_ref[...]


def _ada(c, w, b):
    bsz, d = c.shape
    n = w.shape[1]
    return pl.pallas_call(
        _ada_kernel,
        grid=(n // d,),
        in_specs=[pl.BlockSpec((bsz, d), lambda j: (0, 0)),
                  pl.BlockSpec((d, d), lambda j: (0, j)),
                  pl.BlockSpec((1, d), lambda j: (0, j))],
        out_specs=pl.BlockSpec((bsz, d), lambda j: (0, j)),
        out_shape=jax.ShapeDtypeStruct((bsz, n), F32),
        compiler_params=_params("arbitrary"),
        name="ada",
    )(c, w, b.reshape(1, n))


def _inproj_kernel(x_ref, mod_ref, g_ref, wn_ref, wvt_ref, wg_ref, bg_ref, qk_ref, vt_ref, lf_ref):
    shift = mod_ref[0, 0:1, :]
    scale = mod_ref[0, 1:2, :]
    h = (_rms_rows(x_ref[0]) * g_ref[...] * (1.0 + scale) + shift).astype(BF16)
    qk_ref[0] = _dot(h, wn_ref[...]).astype(BF16)
    vt_ref[0, 0] = _dot_nt(wvt_ref[...], h).astype(BF16)
    logit = _dot(h, wg_ref[...]) + bg_ref[...]
    lf_ref[0] = -_softplus(-logit)


def _inproj(x, mod, g, w_nat, w_vt, w_gate, b_gate):
    bsz, s, d = x.shape
    tm = PROJ_ROWS
    n_nat, n_v = w_nat.shape[1], w_vt.shape[0]
    const = lambda b, i: (0, 0)
    return pl.pallas_call(
        _inproj_kernel,
        grid=(bsz, s // tm),
        in_specs=[pl.BlockSpec((1, tm, d), lambda b, i: (b, i, 0)),
                  pl.BlockSpec((1, N_MOD, d), lambda b, i: (b, 0, 0)),
                  pl.BlockSpec((1, d), const),
                  pl.BlockSpec((d, n_nat), const),
                  pl.BlockSpec((n_v, d), const),
                  pl.BlockSpec((d, LANES), const),
                  pl.BlockSpec((1, LANES), const)],
        out_specs=[pl.BlockSpec((1, tm, n_nat), lambda b, i: (b, i, 0)),
                   pl.BlockSpec((1, 1, n_v, tm), lambda b, i: (b, i, 0, 0)),
                   pl.BlockSpec((1, tm, LANES), lambda b, i: (b, i, 0))],
        out_shape=[jax.ShapeDtypeStruct((bsz, s, n_nat), BF16),
                   jax.ShapeDtypeStruct((bsz, s // tm, n_v, tm), BF16),
                   jax.ShapeDtypeStruct((bsz, s, LANES), F32)],
        compiler_params=_params("arbitrary", "arbitrary"),
        name="inproj",
    )(x, mod, g, w_nat, w_vt, w_gate, b_gate)


def _decay_kernel(lf_ref, k_ref, ks_ref, sel_ref, ind_ref, kaug_ref, stats_ref, carry_ref, kpre_ref, *, n_heads):
    @pl.when(pl.program_id(1) == 0)
    def _():
        carry_ref[...] = jnp.zeros_like(carry_ref)
        kpre_ref[...] = jnp.zeros_like(kpre_ref)

    tm = lf_ref.shape[1]
    lane = lax.broadcasted_iota(jnp.int32, (tm, LANES), 1)
    lf = jnp.where(lane < n_heads, lf_ref[0], 0.0)
    row = lax.broadcasted_iota(jnp.int32, (tm, tm), 0)
    col = lax.broadcasted_iota(jnp.int32, (tm, tm), 1)
    tri = (col <= row).astype(BF16)
    hi, mid, lo = _split3(lf)
    f_run = carry_ref[...] + (_dot(tri, hi) + _dot(tri, mid) + _dot(tri, lo))
    carry_ref[...] = f_run[tm - 1:tm, :]
    ghi, gmid, glo = _split3(-LOG2E * f_run)
    packed = (ghi.astype(F32) + pltpu.roll(gmid.astype(F32), n_heads, 1)
              + pltpu.roll(glo.astype(F32), 2 * n_heads, 1)).astype(BF16)
    placed = _dot(packed, sel_ref[...])
    k_all = k_ref[0]
    for h in range(n_heads):
        k_pair = k_all[:, (h // 2) * LANES:(h // 2 + 1) * LANES]
        own = (lane < HEAD_DIM) if h % 2 == 0 else (lane >= HEAD_DIM)
        kaug_ref[0, h] = jnp.where(own, k_pair, placed[:, h * LANES:(h + 1) * LANES].astype(BF16))

    def head_norm_bound(k):
        k32 = k.astype(F32)
        sq = _dot((k32 * k32).astype(BF16), ind_ref[...])
        return jnp.sqrt(jnp.max(sq, axis=0, keepdims=True) * NORM_SLACK)

    kpre_f = jnp.maximum(kpre_ref[0:1, :], head_norm_bound(k_all))
    kpre_s = jnp.maximum(kpre_ref[1:2, :], head_norm_bound(ks_ref[0]))
    kpre_ref[0:1, :] = kpre_f
    kpre_ref[1:2, :] = kpre_s
    g_end = -LOG2E * f_run[tm - 1:tm, :]
    lane1 = lax.broadcasted_iota(jnp.int32, (1, LANES), 1)

    def spread(v, h):
        return jnp.broadcast_to(jnp.sum(jnp.where(lane1 == h, v, 0.0), axis=1, keepdims=True), (1, ATT_COLS))

    for p in range(n_heads // 2):
        rows = [spread(v, 2 * p + hh) for v in (kpre_f, g_end, kpre_s) for hh in range(2)]
        rows += [jnp.zeros((1, ATT_COLS), F32)] * (stats_ref.shape[3] - len(rows))
        stats_ref[0, p, 0] = jnp.concatenate(rows, axis=0)


def _head_indicator(n_heads):
    ind = np.zeros((n_heads * HEAD_DIM, LANES), np.float32)
    ind[np.arange(n_heads * HEAD_DIM), np.arange(n_heads * HEAD_DIM) // HEAD_DIM] = 1.0
    return jnp.asarray(ind, BF16)


def _decay_select_matrix(n_heads):
    sel = np.zeros((LANES, n_heads * LANES), np.float32)
    for h in range(n_heads):
        base = h * LANES + (HEAD_DIM if h % 2 == 0 else 0)
        for term in range(3):
            sel[term * n_heads + h, base + term] = 1.0
    return jnp.asarray(sel, BF16)


def _decay(log_f, qk, n_heads, k_fox_block, k_sb_block):
    bsz, s, _ = log_f.shape
    tm = PROJ_ROWS
    d_grp = n_heads * HEAD_DIM
    return pl.pallas_call(
        functools.partial(_decay_kernel, n_heads=n_heads),
        grid=(bsz, s // tm),
        in_specs=[pl.BlockSpec((1, tm, LANES), lambda b, i: (b, i, 0)),
                  pl.BlockSpec((1, tm, d_grp), lambda b, i: (b, i, k_fox_block)),
                  pl.BlockSpec((1, tm, d_grp), lambda b, i: (b, i, k_sb_block)),
                  pl.BlockSpec((LANES, n_heads * LANES), lambda b, i: (0, 0)),
                  pl.BlockSpec((d_grp, LANES), lambda b, i: (0, 0))],
        out_specs=[pl.BlockSpec((1, n_heads, tm, LANES), lambda b, i: (b, 0, i, 0)),
                   pl.BlockSpec((1, n_heads // 2, 1, STATS_ROWS, ATT_COLS), lambda b, i: (b, 0, i, 0, 0))],
        out_shape=[jax.ShapeDtypeStruct((bsz, n_heads, s, LANES), BF16),
                   jax.ShapeDtypeStruct((bsz, n_heads // 2, s // tm, STATS_ROWS, ATT_COLS), F32)],
        scratch_shapes=[pltpu.VMEM((1, LANES), F32), pltpu.VMEM((2, LANES), F32)],
        compiler_params=_params("arbitrary", "arbitrary"),
        name="decay",
    )(log_f, qk, qk, _decay_select_matrix(n_heads), _head_indicator(n_heads))


def _lane_queries(q_ref, extra_even, extra_odd, cw):
    out = []
    for pp in range(q_ref.shape[2] // LANES):
        q = q_ref[0, :, pp * LANES:(pp + 1) * LANES].astype(F32) * (HEAD_DIM ** -0.5 * LOG2E)
        lane = lax.broadcasted_iota(jnp.int32, q.shape, 1)
        heads = (jnp.where(lane < HEAD_DIM, q, extra_even(lane)).T.astype(BF16),
                 jnp.where(lane >= HEAD_DIM, q, extra_odd(lane)).T.astype(BF16))
        out += [heads[hh][:, c * cw:(c + 1) * cw] for hh in range(2) for c in range(q.shape[0] // cw)]
    return out


def _visibility(first_key, first_query, bk, cw, strict):
    last_visible_gap = -1 if strict else 0
    if first_key + bk - 1 - first_query <= last_visible_gap:
        return "all"
    if first_key - (first_query + cw - 1) > last_visible_gap:
        return "none"
    gap = (lax.broadcasted_iota(jnp.int32, (bk, cw), 0) - lax.broadcasted_iota(jnp.int32, (bk, cw), 1)
           + (first_key - first_query))
    return gap <= last_visible_gap


def _diag_visibility(u, c, bk, cw, strict):
    return _visibility((1 - u) * bk, c * cw, bk, cw, strict)


def _hidden(visibility):
    return isinstance(visibility, str) and visibility == "none"


def _query_norm_bounds(queries, n_chunks):
    bounds = []
    for li, q in enumerate(queries):
        hh = (li // n_chunks) % 2
        own = q[hh * HEAD_DIM:(hh + 1) * HEAD_DIM, :].astype(F32)
        bounds.append(jnp.sqrt(jnp.sum(own * own, axis=0, keepdims=True) * NORM_SLACK))
    return bounds


def _finish_heads(lanes, g_ref, o_ref):
    n_pairs = o_ref.shape[2] // LANES
    n_chunks = len(lanes) // (2 * n_pairs)
    for pp in range(n_pairs):
        mine = lanes[2 * pp * n_chunks:2 * (pp + 1) * n_chunks]
        outs = [jnp.concatenate(mine[hh * n_chunks:(hh + 1) * n_chunks], axis=1) for hh in range(2)]
        normed = [o * lax.rsqrt(jnp.mean(o * o, axis=0, keepdims=True) + EPS) for o in outs]
        cols = slice(pp * LANES, (pp + 1) * LANES)
        o_ref[0, :, cols] = (jnp.concatenate(normed, axis=0).T * g_ref[:, cols]).astype(o_ref.dtype)


def _fox_kernel(q_ref, k_ref, vt_ref, g_ref, stats_ref, o_ref, s_buf, cmax_buf, p_buf, acc_buf):
    qi = pl.program_id(2)
    bk = vt_ref.shape[3]
    n_lanes, cw = acc_buf.shape[0], acc_buf.shape[2]
    n_chunks = q_ref.shape[1] // cw
    lane_group = lambda li: (li // (2 * n_chunks), (li // n_chunks) % 2, li % n_chunks)
    n_tiles = 2 * (qi + 1)
    ones3 = lambda lo: (lambda lane: jnp.where((lane >= lo) & (lane < lo + 3), 1.0, 0.0))
    queries = _lane_queries(q_ref, ones3(HEAD_DIM), ones3(0), cw)
    acc_buf[...] = jnp.zeros(acc_buf.shape, F32)
    for li in range(n_lanes):
        if _hidden(_diag_visibility(0, lane_group(li)[2], bk, cw, strict=False)):
            p_buf[0, li] = jnp.zeros((bk, cw), BF16)

    def step(t, slot, carry, score="below", softmax="below", value=True):
        new = []
        for li in range(n_lanes):
            pp, hh, c = lane_group(li)
            see = lambda u: "all" if u == "below" else _diag_visibility(u, c, bk, cw, strict=False)
            if score is not None and not _hidden(see(score)):
                start = pl.multiple_of((n_tiles - 2 - t) * bk, bk)
                s_new = _dot(k_ref[0, 2 * pp + hh, pl.ds(start, bk), :], queries[li])
                if not isinstance(see(score), str):
                    s_new = jnp.where(see(score), s_new, MASKED)
                s_buf[1 - slot, li] = s_new
                cmax_buf[1 - slot, li] = jnp.max(s_new, axis=0, keepdims=True)
            pv = None
            if value:
                vt = vt_ref[0, n_tiles - t, pl.ds(pp * LANES + hh * HEAD_DIM, HEAD_DIM), :]
                pv = _dot(vt, p_buf[1 - slot, li])
            m, l = carry[li]
            if softmax is not None and not _hidden(see(softmax)):
                m_new = jnp.maximum(m, cmax_buf[slot, li])
                alpha = jnp.exp2(m - m_new)
                p = jnp.exp2(s_buf[slot, li] - m_new)
                p_buf[slot, li] = p.astype(BF16)
                m, l = m_new, alpha * l + jnp.sum(p, axis=0, keepdims=True)
                acc_buf[li] = alpha * (acc_buf[li] if pv is None else acc_buf[li] + pv)
            elif pv is not None:
                acc_buf[li] += pv
            new.append((m, l))
        return tuple(new)

    def step_pair(i, carry):
        t = 2 * i + 1
        return step(t + 1, 0, step(t, 1, carry))

    q_norm = _query_norm_bounds(queries, n_chunks)

    def later_tiles_matter(i, carry):
        j_rest = jnp.maximum(n_tiles - 5 - 2 * i, 0)
        worst = None
        for li in range(n_lanes):
            pp, hh, _ = lane_group(li)
            bound = (q_norm[li] * stats_ref[0, pp, j_rest, hh:hh + 1, :]
                     + stats_ref[0, pp, j_rest, 2 + hh:3 + hh, :] - carry[li][0])
            worst = bound if worst is None else jnp.maximum(worst, bound)
        return jnp.max(worst) >= -PRUNE_LOG2

    def pair_and_check(state):
        i, _, carry = state
        carry = step_pair(i, carry)
        return i + 1, later_tiles_matter(i, carry), carry

    carry = tuple((jnp.full((1, cw), M_INIT, F32), jnp.zeros((1, cw), F32)) for _ in range(n_lanes))
    carry = step(-1, 1, carry, score=0, softmax=None, value=False)
    carry = step(0, 0, carry, score=1, softmax=0, value=False)
    n_pairs, _, carry = lax.while_loop(lambda st: (st[0] < qi) & st[1], pair_and_check,
                                       (jnp.int32(0), jnp.bool_(True), carry))
    carry = step(2 * n_pairs + 1, 1, carry, score=None)
    carry = step(2 * n_pairs + 2, 0, carry, score=None, softmax=None)
    _finish_heads([acc_buf[li] / carry[li][1] for li in range(n_lanes)], g_ref, o_ref)


def _sb_kernel(q_ref, k_ref, vt_ref, g_ref, stats_ref, o_ref, z_buf, sp_buf, e_buf, wrow_buf, acc_buf):
    qi = pl.program_id(2)
    bk = vt_ref.shape[3]
    n_lanes, cw = acc_buf.shape[0], acc_buf.shape[2]
    n_chunks = q_ref.shape[1] // cw
    lane_group = lambda li: (li // (2 * n_chunks), (li // n_chunks) % 2, li % n_chunks)
    n_tiles = 2 * (qi + 1)
    zero = lambda lane: 0.0
    queries = _lane_queries(q_ref, zero, zero, cw)
    suffix = (lax.broadcasted_iota(jnp.int32, (bk, bk), 1)
              >= lax.broadcasted_iota(jnp.int32, (bk, bk), 0)).astype(BF16)
    e_buf[1] = jnp.full(e_buf.shape[1:], MASKED, F32)
    wrow_buf[1] = jnp.zeros(wrow_buf.shape[1:], F32)
    acc_buf[...] = jnp.zeros(acc_buf.shape, F32)
    for li in range(n_lanes):
        if _hidden(_diag_visibility(0, lane_group(li)[2], bk, cw, strict=True)):
            z_buf[0, li] = jnp.full((bk, cw), MASKED, F32)
            sp_buf[0, li] = jnp.zeros((bk, cw), BF16)

    def step(t, slot, later, score="below", softplus="below", cumsum=True, weight=True):
        new_later = []
        for li in range(n_lanes):
            pp, hh, c = lane_group(li)
            see = lambda u: "all" if u == "below" else _diag_visibility(u, c, bk, cw, strict=True)
            if cumsum:
                within = _dot(suffix, sp_buf[1 - slot, li])
                e_buf[1 - slot, li] = z_buf[1 - slot, li] - within
                wrow_buf[1 - slot, li] = within[0:1, :]
            if score is not None and not _hidden(see(score)):
                start = pl.multiple_of((n_tiles - 1 - (t + 3)) * bk, bk)
                z_new = _dot(k_ref[0, pl.ds(start, bk), pp * LANES:(pp + 1) * LANES], queries[li])
                if not isinstance(see(score), str):
                    z_new = jnp.where(see(score), z_new, MASKED)
                z_buf[1 - slot, li] = z_new
            if weight:
                a = jnp.exp2(e_buf[slot, li] - later[li])
                vt = vt_ref[0, n_tiles - 1 - jnp.maximum(t, 0), pl.ds(pp * LANES + hh * HEAD_DIM, HEAD_DIM), :]
                acc_buf[li] += _dot(vt, a.astype(BF16))
                new_later.append(later[li] + wrow_buf[slot, li])
            else:
                new_later.append(later[li])
            if softplus is not None and not _hidden(see(softplus)):
                z = z_buf[slot, li]
                sp_buf[slot, li] = (jnp.maximum(z, 0.0) + jnp.log2(1.0 + jnp.exp2(-jnp.abs(z)))).astype(BF16)
        return tuple(new_later)

    def step_pair(i, later):
        t = 2 * i - 1
        return step(t + 1, 0, step(t, 1, later))

    q_norm = _query_norm_bounds(queries, n_chunks)

    def later_tiles_matter(i, later):
        j_rest = jnp.maximum(n_tiles - 5 - 2 * i, 0)
        worst = None
        for li in range(n_lanes):
            pp, hh, _ = lane_group(li)
            bound = q_norm[li] * stats_ref[0, pp, j_rest, 4 + hh:5 + hh, :] - (later[li] + wrow_buf[1, li])
            worst = bound if worst is None else jnp.maximum(worst, bound)
        return jnp.max(worst) >= -PRUNE_LOG2

    def pair_and_check(state):
        i, _, later = state
        later = step_pair(i, later)
        return i + 1, later_tiles_matter(i, later), later

    later = tuple(jnp.zeros((1, cw), F32) for _ in range(n_lanes))
    later = step(-3, 1, later, score=0, softplus=None, cumsum=False, weight=False)
    later = step(-2, 0, later, score=1, softplus=0, cumsum=False, weight=False)
    n_pairs, _, later = lax.while_loop(lambda st: (st[0] < qi) & st[1], pair_and_check,
                                       (jnp.int32(0), jnp.bool_(True), later))
    later = step(2 * n_pairs - 1, 1, later, score=None)
    later = step(2 * n_pairs, 0, later, score=None, softplus=None)
    later = step(2 * n_pairs + 1, 1, later, score=None, softplus=None, cumsum=False)
    _finish_heads([acc_buf[li] for li in range(n_lanes)], g_ref, o_ref)


def _attention(body, name, pairs, scratch, qk, k_arr, k_spec, vt, g, stats, q_block0, vt_block0, n_heads):
    bsz, s, _ = qk.shape
    bq, bk, width = ATT_Q, ATT_K, pairs * LANES
    d_grp = n_heads * HEAD_DIM
    assert (n_heads // 2) % pairs == 0
    return pl.pallas_call(
        body,
        grid=(bsz, n_heads // 2 // pairs, s // bq),
        in_specs=[pl.BlockSpec((1, bq, width), lambda b, p, i: (b, i, q_block0 + p)),
                  k_spec,
                  pl.BlockSpec((1, s // bk, width, bk), lambda b, p, i: (b, 0, vt_block0 + p, 0)),
                  pl.BlockSpec((1, width), lambda b, p, i: (0, p)),
                  pl.BlockSpec((1, pairs) + stats.shape[2:], lambda b, p, i: (b, p, 0, 0, 0))],
        out_specs=pl.BlockSpec((1, bq, width), lambda b, p, i: (b, i, p)),
        out_shape=jax.ShapeDtypeStruct((bsz, s, d_grp), BF16),
        scratch_shapes=scratch,
        compiler_params=_params("arbitrary", "arbitrary", "arbitrary"),
        name=name,
    )(qk, k_arr, vt, g.reshape(1, d_grp), stats)


def _lane_groups(pairs):
    return pairs * 2 * (ATT_Q // ATT_COLS)


def _fox_scratch(pairs):
    n = _lane_groups(pairs)
    return [pltpu.VMEM((2, n, ATT_K, ATT_COLS), F32), pltpu.VMEM((2, n, 1, ATT_COLS), F32),
            pltpu.VMEM((2, n, ATT_K, ATT_COLS), BF16), pltpu.VMEM((n, HEAD_DIM, ATT_COLS), F32)]


def _sb_scratch(pairs):
    n = _lane_groups(pairs)
    return [pltpu.VMEM((2, n, ATT_K, ATT_COLS), F32), pltpu.VMEM((2, n, ATT_K, ATT_COLS), BF16),
            pltpu.VMEM((2, n, ATT_K, ATT_COLS), F32), pltpu.VMEM((2, n, 1, ATT_COLS), F32),
            pltpu.VMEM((n, HEAD_DIM, ATT_COLS), F32)]


def _outproj_kernel(x_ref, mf_ref, ms_ref, w_ref, mod_ref, g_ref, x1_ref, h2_ref):
    mix = jnp.concatenate([mf_ref[0], ms_ref[0]], axis=-1)
    x1 = x_ref[0] + mod_ref[0, 2:3, :] * _dot(mix, w_ref[...])
    x1_ref[0] = x1
    shift = mod_ref[0, 3:4, :]
    scale = mod_ref[0, 4:5, :]
    h2_ref[0] = (_rms_rows(x1) * g_ref[...] * (1.0 + scale) + shift).astype(BF16)


def _outproj(x, mix_f, mix_s, w_out, mod, g):
    bsz, s, d = x.shape
    tm = OUT_ROWS
    row = lambda b, i: (b, i, 0)
    return pl.pallas_call(
        _outproj_kernel,
        grid=(bsz, s // tm),
        in_specs=[pl.BlockSpec((1, tm, d), row),
                  pl.BlockSpec((1, tm, mix_f.shape[2]), row),
                  pl.BlockSpec((1, tm, mix_s.shape[2]), row),
                  pl.BlockSpec(w_out.shape, lambda b, i: (0, 0)),
                  pl.BlockSpec((1, N_MOD, d), lambda b, i: (b, 0, 0)),
                  pl.BlockSpec((1, d), lambda b, i: (0, 0))],
        out_specs=[pl.BlockSpec((1, tm, d), row), pl.BlockSpec((1, tm, d), row)],
        out_shape=[jax.ShapeDtypeStruct((bsz, s, d), F32), jax.ShapeDtypeStruct((bsz, s, d), BF16)],
        compiler_params=_params("arbitrary", "arbitrary"),
        name="outproj",
    )(x, mix_f, mix_s, w_out, mod, g)


def _mlp_kernel(h_ref, halo_ref, x1_ref, mod_ref, wu_ref, cw_ref, cb_ref, wd_ref, gf_ref, o_ref,
                u_buf, acc_ref, *, final_norm):
    i = pl.program_id(1)
    tm = h_ref.shape[1]
    n_chunks = wd_ref.shape[0]
    halo = halo_ref[0]
    halo = jnp.where(i > 0, halo, jnp.zeros_like(halo))
    hx = jnp.concatenate([halo, h_ref[0]], axis=0)
    acc_ref[...] = jnp.zeros_like(acc_ref)

    def project_up(c, slot):
        for br in range(2):
            u_buf[slot, br] = _dot(hx, wu_ref[br, c])

    def mix_down(c, slot):
        branches = []
        for br in range(2):
            out = cb_ref[br, c]
            for tap in range(CONV_WIDTH):
                first = BF16_SUBLANES - (CONV_WIDTH - 1 - tap)
                out = out + cw_ref[br, c, tap:tap + 1, :] * u_buf[slot, br, pl.ds(first, tm), :]
            branches.append(out)
        u_gate, u_val = branches
        acc_ref[...] += _dot((u_gate * jax.nn.sigmoid(u_gate) * u_val).astype(BF16), wd_ref[c])

    def chunk_pair(j, _):
        c = 2 * j
        project_up(c + 1, 1)
        mix_down(c, 0)
        project_up(c + 2, 0)
        mix_down(c + 1, 1)
        return 0

    project_up(0, 0)
    lax.fori_loop(0, (n_chunks - 1) // 2, chunk_pair, 0)
    mix_down(n_chunks - 1, 0)
    x2 = x1_ref[0] + mod_ref[0, 5:6, :] * acc_ref[...]
    o_ref[0] = _rms_rows(x2) * gf_ref[...] if final_norm else x2


def _mlp(h2, x1, mod, w_up, conv_w, conv_b, w_down, g_final, final_norm):
    bsz, s, d = x1.shape
    tm = OUT_ROWS
    n_chunks, tf = w_down.shape[0], w_down.shape[1]
    assert n_chunks % 2 == 1
    halo_blocks = tm // BF16_SUBLANES
    row = lambda b, i: (b, i, 0)
    resident = lambda a: pl.BlockSpec(a.shape, lambda b, i: (0,) * a.ndim, pipeline_mode=pl.Buffered(1))
    return pl.pallas_call(
        functools.partial(_mlp_kernel, final_norm=final_norm),
        grid=(bsz, s // tm),
        in_specs=[pl.BlockSpec((1, tm, d), row),
                  pl.BlockSpec((1, BF16_SUBLANES, d), lambda b, i: (b, jnp.maximum(i * halo_blocks - 1, 0), 0)),
                  pl.BlockSpec((1, tm, d), row),
                  pl.BlockSpec((1, N_MOD, d), lambda b, i: (b, 0, 0)),
                  resident(w_up), resident(conv_w), resident(conv_b), resident(w_down),
                  pl.BlockSpec((1, d), lambda b, i: (0, 0))],
        out_specs=pl.BlockSpec((1, tm, d), row),
        out_shape=jax.ShapeDtypeStruct((bsz, s, d), F32),
        scratch_shapes=[pltpu.VMEM((2, 2, tm + BF16_SUBLANES, tf), F32), pltpu.VMEM((tm, d), F32)],
        compiler_params=_params("arbitrary", "arbitrary"),
        name="mlp",
    )(h2, h2, x1, mod, w_up, conv_w, conv_b, w_down, g_final)


def _chunk_columns(a, d_ff, n_chunks, tf):
    halves = jnp.stack([a[:, :d_ff], a[:, d_ff:]])
    halves = jnp.pad(halves, ((0, 0), (0, 0), (0, n_chunks * tf - d_ff)))
    return halves.reshape(2, a.shape[0], n_chunks, tf).transpose(0, 2, 1, 3)


def _pad_cols(a, n):
    return jnp.pad(a, ((0, 0), (0, n - a.shape[1])))


def kernel(x, c, w_ada, b_ada, g_attn, w_in, b_fgate, g_out_fox, g_out_sb, w_out,
           g_mlp, w_up, conv_w, conv_b, w_down, g_final):
    depth, d, _ = w_ada.shape
    n_fox = b_fgate.shape[1]
    d_fox = n_fox * HEAD_DIM
    d_sb = g_out_sb.shape[1]
    n_sb = d_sb // HEAD_DIM
    d_ff = w_down.shape[1]
    d_ff_pad = -(-d_ff // FF_CHUNK) * FF_CHUNK
    assert n_fox % 2 == 0 and n_sb == n_fox and 3 * n_fox <= LANES
    assert x.shape[1] % OUT_ROWS == 0 and x.shape[1] % ATT_Q == 0 and ATT_Q == 2 * ATT_K
    o_kf, o_vf, o_qs, o_ks, o_vs, o_gate = (d_fox, 2 * d_fox, 3 * d_fox, 3 * d_fox + d_sb,
                                             3 * d_fox + 2 * d_sb, 3 * d_fox + 3 * d_sb)

    for l in range(depth):
        mod = _ada(c, w_ada[l], b_ada[l]).reshape(-1, N_MOD, d)
        w = w_in[l]
        w_nat = jnp.concatenate([w[:, :o_vf], w[:, o_qs:o_vs]], axis=1).astype(BF16)
        w_vt = jnp.concatenate([w[:, o_vf:o_qs], w[:, o_vs:o_gate]], axis=1).T.astype(BF16)
        w_gate = _pad_cols(w[:, o_gate:], LANES).astype(BF16)
        b_gate = _pad_cols(b_fgate[l].reshape(1, n_fox), LANES)
        qk, vt, log_f = _inproj(x, mod, g_attn[l].reshape(1, d), w_nat, w_vt, w_gate, b_gate)

        k_aug, stats = _decay(log_f, qk, n_fox, k_fox_block=1, k_sb_block=3)
        steps_f, steps_s = n_fox // 2 // FOX_PAIRS, n_sb // 2 // SB_PAIRS
        fox_k_spec = pl.BlockSpec((1, 2 * FOX_PAIRS, x.shape[1], LANES), lambda b, p, i: (b, p, 0, 0))
        mix_f = _attention(_fox_kernel, "fox", FOX_PAIRS, _fox_scratch(FOX_PAIRS), qk, k_aug, fox_k_spec, vt,
                           g_out_fox[l], stats, q_block0=0, vt_block0=0, n_heads=n_fox)
        sb_k_spec = pl.BlockSpec((1, x.shape[1], SB_PAIRS * LANES), lambda b, p, i: (b, 0, 3 * steps_s + p))
        mix_s = _attention(_sb_kernel, "sb", SB_PAIRS, _sb_scratch(SB_PAIRS), qk, qk, sb_k_spec, vt,
                           g_out_sb[l], stats, q_block0=2 * steps_s, vt_block0=steps_s, n_heads=n_sb)

        x1, h2 = _outproj(x, mix_f, mix_s, w_out[l].astype(BF16), mod, g_mlp[l].reshape(1, d))

        n_ff = d_ff_pad // FF_CHUNK
        x = _mlp(h2, x1, mod,
                 _chunk_columns(w_up[l], d_ff, n_ff, FF_CHUNK).astype(BF16),
                 _chunk_columns(conv_w[l], d_ff, n_ff, FF_CHUNK),
                 _chunk_columns(conv_b[l].reshape(1, -1), d_ff, n_ff, FF_CHUNK),
                 jnp.pad(w_down[l], ((0, d_ff_pad - d_ff), (0, 0))).astype(BF16).reshape(n_ff, FF_CHUNK, d),
                 g_final.reshape(1, d), final_norm=(l == depth - 1))
    return x
```

```python
import functools

import numpy as np
import jax
import jax.numpy as jnp
from jax import lax
from jax.experimental import pallas as pl
from jax.experimental.pallas import tpu as pltpu

HEAD_DIM = 64
N_MOD = 6
CONV_WIDTH = 3
EPS = 1e-6

LANES = 128
BF16_SUBLANES = 16
VMEM_LIMIT_BYTES = 48 * 1024 * 1024

ATT_Q = 512
ATT_K = 256
ATT_COLS = 256
FOX_PAIRS = 1
SB_PAIRS = 2
SB_WIDE_TILES = 4
PROJ_ROWS = ATT_K
LOG2E = 1.4426950408889634
MASKED = -1e30
M_INIT = -1e29
PRUNE_LOG2 = 160.0
NORM_SLACK = 1.02
STATS_ROWS = 8
OUT_ROWS = 512
FF_CHUNK = 256

F32 = jnp.float32
BF16 = jnp.bfloat16
NT_DIMS = (((1,), (1,)), ((), ()))


def _dot(a, b):
    return jnp.dot(a, b, preferred_element_type=F32)


def _dot_nt(a, b):
    return lax.dot_general(a, b, NT_DIMS, preferred_element_type=F32)


def _params(*sem):
    return pltpu.CompilerParams(dimension_semantics=sem, vmem_limit_bytes=VMEM_LIMIT_BYTES)


def _rms_rows(x):
    return x * lax.rsqrt(jnp.mean(x * x, axis=-1, keepdims=True) + EPS)


def _softplus(z):
    return jnp.maximum(z, 0.0) + jnp.log(1.0 + jnp.exp(-jnp.abs(z)))


def _split3(x):
    hi = x.astype(BF16)
    r1 = x - hi.astype(F32)
    mid = r1.astype(BF16)
    lo = (r1 - mid.astype(F32)).astype(BF16)
    return hi, mid, lo


def _ada_kernel(c_ref, w_ref, b_ref, o_ref):
    c = c_ref[...]
    o_ref[...] = _dot(c * jax.nn.sigmoid(c), w_ref[...]) + b_ref[...]


def _ada(c, w, b):
    bsz, d = c.shape
    n = w.shape[1]
    return pl.pallas_call(
        _ada_kernel,
        grid=(n // d,),
        in_specs=[pl.BlockSpec((bsz, d), lambda j: (0, 0)),
                  pl.BlockSpec((d, d), lambda j: (0, j)),
                  pl.BlockSpec((1, d), lambda j: (0, j))],
        out_specs=pl.BlockSpec((bsz, d), lambda j: (0, j)),
        out_shape=jax.ShapeDtypeStruct((bsz, n), F32),
        compiler_params=_params("arbitrary"),
        name="ada",
    )(c, w, b.reshape(1, n))


def _inproj_kernel(x_ref, mod_ref, g_ref, wn_ref, wvt_ref, wg_ref, bg_ref, qk_ref, vt_ref, lf_ref):
    shift = mod_ref[0, 0:1, :]
    scale = mod_ref[0, 1:2, :]
    h = (_rms_rows(x_ref[0]) * g_ref[...] * (1.0 + scale) + shift).astype(BF16)
    qk_ref[0] = _dot(h, wn_ref[...]).astype(BF16)
    vt_ref[0, 0] = _dot_nt(wvt_ref[...], h).astype(BF16)
    logit = _dot(h, wg_ref[...]) + bg_ref[...]
    lf_ref[0] = -_softplus(-logit)


def _inproj(x, mod, g, w_nat, w_vt, w_gate, b_gate):
    bsz, s, d = x.shape
    tm = PROJ_ROWS
    n_nat, n_v = w_nat.shape[1], w_vt.shape[0]
    const = lambda b, i: (0, 0)
    return pl.pallas_call(
        _inproj_kernel,
        grid=(bsz, s // tm),
        in_specs=[pl.BlockSpec((1, tm, d), lambda b, i: (b, i, 0)),
                  pl.BlockSpec((1, N_MOD, d), lambda b, i: (b, 0, 0)),
                  pl.BlockSpec((1, d), const),
                  pl.BlockSpec((d, n_nat), const),
                  pl.BlockSpec((n_v, d), const),
                  pl.BlockSpec((d, LANES), const),
                  pl.BlockSpec((1, LANES), const)],
        out_specs=[pl.BlockSpec((1, tm, n_nat), lambda b, i: (b, i, 0)),
                   pl.BlockSpec((1, 1, n_v, tm), lambda b, i: (b, i, 0, 0)),
                   pl.BlockSpec((1, tm, LANES), lambda b, i: (b, i, 0))],
        out_shape=[jax.ShapeDtypeStruct((bsz, s, n_nat), BF16),
                   jax.ShapeDtypeStruct((bsz, s // tm, n_v, tm), BF16),
                   jax.ShapeDtypeStruct((bsz, s, LANES), F32)],
        compiler_params=_params("arbitrary", "arbitrary"),
        name="inproj",
    )(x, mod, g, w_nat, w_vt, w_gate, b_gate)


def _decay_kernel(lf_ref, k_ref, ks_ref, sel_ref, ind_ref, kaug_ref, stats_ref, carry_ref, kpre_ref, *, n_heads):
    @pl.when(pl.program_id(1) == 0)
    def _():
        carry_ref[...] = jnp.zeros_like(carry_ref)
        kpre_ref[...] = jnp.zeros_like(kpre_ref)

    tm = lf_ref.shape[1]
    lane = lax.broadcasted_iota(jnp.int32, (tm, LANES), 1)
    lf = jnp.where(lane < n_heads, lf_ref[0], 0.0)
    row = lax.broadcasted_iota(jnp.int32, (tm, tm), 0)
    col = lax.broadcasted_iota(jnp.int32, (tm, tm), 1)
    tri = (col <= row).astype(BF16)
    hi, mid, lo = _split3(lf)
    f_run = carry_ref[...] + (_dot(tri, hi) + _dot(tri, mid) + _dot(tri, lo))
    carry_ref[...] = f_run[tm - 1:tm, :]
    ghi, gmid, glo = _split3(-LOG2E * f_run)
    packed = (ghi.astype(F32) + pltpu.roll(gmid.astype(F32), n_heads, 1)
              + pltpu.roll(glo.astype(F32), 2 * n_heads, 1)).astype(BF16)
    placed = _dot(packed, sel_ref[...])
    k_all = k_ref[0]
    for h in range(n_heads):
        k_pair = k_all[:, (h // 2) * LANES:(h // 2 + 1) * LANES]
        own = (lane < HEAD_DIM) if h % 2 == 0 else (lane >= HEAD_DIM)
        kaug_ref[0, h] = jnp.where(own, k_pair, placed[:, h * LANES:(h + 1) * LANES].astype(BF16))

    def head_norm_bound(k):
        k32 = k.astype(F32)
        sq = _dot((k32 * k32).astype(BF16), ind_ref[...])
        return jnp.sqrt(jnp.max(sq, axis=0, keepdims=True) * NORM_SLACK)

    kpre_f = jnp.maximum(kpre_ref[0:1, :], head_norm_bound(k_all))
    kpre_s = jnp.maximum(kpre_ref[1:2, :], head_norm_bound(ks_ref[0]))
    kpre_ref[0:1, :] = kpre_f
    kpre_ref[1:2, :] = kpre_s
    g_end = -LOG2E * f_run[tm - 1:tm, :]
    lane1 = lax.broadcasted_iota(jnp.int32, (1, LANES), 1)

    def spread(v, h):
        return jnp.broadcast_to(jnp.sum(jnp.where(lane1 == h, v, 0.0), axis=1, keepdims=True), (1, ATT_COLS))

    for p in range(n_heads // 2):
        rows = [spread(v, 2 * p + hh) for v in (kpre_f, g_end, kpre_s) for hh in range(2)]
        rows += [jnp.zeros((1, ATT_COLS), F32)] * (stats_ref.shape[3] - len(rows))
        stats_ref[0, p, 0] = jnp.concatenate(rows, axis=0)


def _head_indicator(n_heads):
    ind = np.zeros((n_heads * HEAD_DIM, LANES), np.float32)
    ind[np.arange(n_heads * HEAD_DIM), np.arange(n_heads * HEAD_DIM) // HEAD_DIM] = 1.0
    return jnp.asarray(ind, BF16)


def _decay_select_matrix(n_heads):
    sel = np.zeros((LANES, n_heads * LANES), np.float32)
    for h in range(n_heads):
        base = h * LANES + (HEAD_DIM if h % 2 == 0 else 0)
        for term in range(3):
            sel[term * n_heads + h, base + term] = 1.0
    return jnp.asarray(sel, BF16)


def _decay(log_f, qk, n_heads, k_fox_block, k_sb_block):
    bsz, s, _ = log_f.shape
    tm = PROJ_ROWS
    d_grp = n_heads * HEAD_DIM
    return pl.pallas_call(
        functools.partial(_decay_kernel, n_heads=n_heads),
        grid=(bsz, s // tm),
        in_specs=[pl.BlockSpec((1, tm, LANES), lambda b, i: (b, i, 0)),
                  pl.BlockSpec((1, tm, d_grp), lambda b, i: (b, i, k_fox_block)),
                  pl.BlockSpec((1, tm, d_grp), lambda b, i: (b, i, k_sb_block)),
                  pl.BlockSpec((LANES, n_heads * LANES), lambda b, i: (0, 0)),
                  pl.BlockSpec((d_grp, LANES), lambda b, i: (0, 0))],
        out_specs=[pl.BlockSpec((1, n_heads, tm, LANES), lambda b, i: (b, 0, i, 0)),
                   pl.BlockSpec((1, n_heads // 2, 1, STATS_ROWS, ATT_COLS), lambda b, i: (b, 0, i, 0, 0))],
        out_shape=[jax.ShapeDtypeStruct((bsz, n_heads, s, LANES), BF16),
                   jax.ShapeDtypeStruct((bsz, n_heads // 2, s // tm, STATS_ROWS, ATT_COLS), F32)],
        scratch_shapes=[pltpu.VMEM((1, LANES), F32), pltpu.VMEM((2, LANES), F32)],
        compiler_params=_params("arbitrary", "arbitrary"),
        name="decay",
    )(log_f, qk, qk, _decay_select_matrix(n_heads), _head_indicator(n_heads))


def _lane_queries(q_ref, extra_even, extra_odd, cw):
    out = []
    for pp in range(q_ref.shape[2] // LANES):
        q = q_ref[0, :, pp * LANES:(pp + 1) * LANES].astype(F32) * (HEAD_DIM ** -0.5 * LOG2E)
        lane = lax.broadcasted_iota(jnp.int32, q.shape, 1)
        heads = (jnp.where(lane < HEAD_DIM, q, extra_even(lane)).T.astype(BF16),
                 jnp.where(lane >= HEAD_DIM, q, extra_odd(lane)).T.astype(BF16))
        out += [heads[hh][:, c * cw:(c + 1) * cw] for hh in range(2) for c in range(q.shape[0] // cw)]
    return out


def _visibility(first_key, first_query, bk, cw, strict):
    last_visible_gap = -1 if strict else 0
    if first_key + bk - 1 - first_query <= last_visible_gap:
        return "all"
    if first_key - (first_query + cw - 1) > last_visible_gap:
        return "none"
    gap = (lax.broadcasted_iota(jnp.int32, (bk, cw), 0) - lax.broadcasted_iota(jnp.int32, (bk, cw), 1)
           + (first_key - first_query))
    return gap <= last_visible_gap


def _diag_visibility(u, c, bk, cw, strict):
    return _visibility((1 - u) * bk, c * cw, bk, cw, strict)


def _hidden(visibility):
    return isinstance(visibility, str) and visibility == "none"


def _query_norm_bounds(queries, n_chunks):
    bounds = []
    for li, q in enumerate(queries):
        hh = (li // n_chunks) % 2
        own = q[hh * HEAD_DIM:(hh + 1) * HEAD_DIM, :].astype(F32)
        bounds.append(jnp.sqrt(jnp.sum(own * own, axis=0, keepdims=True) * NORM_SLACK))
    return bounds


def _finish_heads(lanes, g_ref, o_ref):
    n_pairs = o_ref.shape[2] // LANES
    n_chunks = len(lanes) // (2 * n_pairs)
    for pp in range(n_pairs):
        mine = lanes[2 * pp * n_chunks:2 * (pp + 1) * n_chunks]
        outs = [jnp.concatenate(mine[hh * n_chunks:(hh + 1) * n_chunks], axis=1) for hh in range(2)]
        normed = [o * lax.rsqrt(jnp.mean(o * o, axis=0, keepdims=True) + EPS) for o in outs]
        cols = slice(pp * LANES, (pp + 1) * LANES)
        o_ref[0, :, cols] = (jnp.concatenate(normed, axis=0).T * g_ref[:, cols]).astype(o_ref.dtype)


def _fox_kernel(q_ref, k_ref, vt_ref, g_ref, stats_ref, o_ref, s_buf, cmax_buf, p_buf, acc_buf):
    qi = pl.program_id(2)
    bk = vt_ref.shape[3]
    n_lanes, cw = acc_buf.shape[0], acc_buf.shape[2]
    n_chunks = q_ref.shape[1] // cw
    lane_group = lambda li: (li // (2 * n_chunks), (li // n_chunks) % 2, li % n_chunks)
    n_tiles = 2 * (qi + 1)
    ones3 = lambda lo: (lambda lane: jnp.where((lane >= lo) & (lane < lo + 3), 1.0, 0.0))
    queries = _lane_queries(q_ref, ones3(HEAD_DIM), ones3(0), cw)
    acc_buf[...] = jnp.zeros(acc_buf.shape, F32)
    for li in range(n_lanes):
        if _hidden(_diag_visibility(0, lane_group(li)[2], bk, cw, strict=False)):
            p_buf[0, li] = jnp.zeros((bk, cw), BF16)

    def step(t, slot, carry, score="below", softmax="below", value=True):
        new = []
        for li in range(n_lanes):
            pp, hh, c = lane_group(li)
            see = lambda u: "all" if u == "below" else _diag_visibility(u, c, bk, cw, strict=False)
            if score is not None and not _hidden(see(score)):
                start = pl.multiple_of((n_tiles - 2 - t) * bk, bk)
                s_new = _dot(k_ref[0, 2 * pp + hh, pl.ds(start, bk), :], queries[li])
                if not isinstance(see(score), str):
                    s_new = jnp.where(see(score), s_new, MASKED)
                s_buf[1 - slot, li] = s_new
                cmax_buf[1 - slot, li] = jnp.max(s_new, axis=0, keepdims=True)
            pv = None
            if value:
                vt = vt_ref[0, n_tiles - t, pl.ds(pp * LANES + hh * HEAD_DIM, HEAD_DIM), :]
                pv = _dot(vt, p_buf[1 - slot, li])
            m, l = carry[li]
            if softmax is not None and not _hidden(see(softmax)):
                m_new = jnp.maximum(m, cmax_buf[slot, li])
                alpha = jnp.exp2(m - m_new)
                p = jnp.exp2(s_buf[slot, li] - m_new)
                p_buf[slot, li] = p.astype(BF16)
                m, l = m_new, alpha * l + jnp.sum(p, axis=0, keepdims=True)
                acc_buf[li] = alpha * (acc_buf[li] if pv is None else acc_buf[li] + pv)
            elif pv is not None:
                acc_buf[li] += pv
            new.append((m, l))
        return tuple(new)

    def step_pair(i, carry):
        t = 2 * i + 1
        return step(t + 1, 0, step(t, 1, carry))

    q_norm = _query_norm_bounds(queries, n_chunks)

    def later_tiles_matter(i, carry):
        j_rest = jnp.maximum(n_tiles - 5 - 2 * i, 0)
        worst = None
        for li in range(n_lanes):
            pp, hh, _ = lane_group(li)
            bound = (q_norm[li] * stats_ref[0, pp, j_rest, hh:hh + 1, :]
                     + stats_ref[0, pp, j_rest, 2 + hh:3 + hh, :] - carry[li][0])
            worst = bound if worst is None else jnp.maximum(worst, bound)
        return jnp.max(worst) >= -PRUNE_LOG2

    def pair_and_check(state):
        i, _, carry = state
        carry = step_pair(i, carry)
        return i + 1, later_tiles_matter(i, carry), carry

    carry = tuple((jnp.full((1, cw), M_INIT, F32), jnp.zeros((1, cw), F32)) for _ in range(n_lanes))
    carry = step(-1, 1, carry, score=0, softmax=None, value=False)
    carry = step(0, 0, carry, score=1, softmax=0, value=False)
    n_pairs, _, carry = lax.while_loop(lambda st: (st[0] < qi) & st[1], pair_and_check,
                                       (jnp.int32(0), jnp.bool_(True), carry))
    carry = step(2 * n_pairs + 1, 1, carry, score=None)
    carry = step(2 * n_pairs + 2, 0, carry, score=None, softmax=None)
    _finish_heads([acc_buf[li] / carry[li][1] for li in range(n_lanes)], g_ref, o_ref)


def _sb_kernel(q_ref, k_ref, vt_ref, g_ref, stats_ref, o_ref, z_buf, sp_buf, e_buf, wrow_buf, acc_buf):
    qi = pl.program_id(2)
    bk = vt_ref.shape[3]
    n_wide = z_buf.shape[0]
    n_lanes, cw = acc_buf.shape[0], acc_buf.shape[2]
    n_chunks = q_ref.shape[1] // cw
    lane_group = lambda li: (li // (2 * n_chunks), (li // n_chunks) % 2, li % n_chunks)
    n_tiles = 2 * (qi + 1)
    zero = lambda lane: 0.0
    queries = _lane_queries(q_ref, zero, zero, cw)
    q_norm = _query_norm_bounds(queries, n_chunks)
    suffix = (lax.broadcasted_iota(jnp.int32, (bk, bk), 1)
              >= lax.broadcasted_iota(jnp.int32, (bk, bk), 0)).astype(BF16)
    softplus2 = lambda z: jnp.maximum(z, 0.0) + jnp.log2(1.0 + jnp.exp2(-jnp.abs(z)))
    keys = lambda j, pp: k_ref[0, pl.ds(pl.multiple_of(j * bk, bk), bk), pp * LANES:(pp + 1) * LANES]
    values = lambda j, pp, hh: vt_ref[0, j, pl.ds(pp * LANES + hh * HEAD_DIM, HEAD_DIM), :]

    sees = lambda u, li: _diag_visibility(u, lane_group(li)[2], bk, cw, strict=True) if u < 2 else "all"
    live = [(u, li) for u in range(n_wide) for li in range(n_lanes) if not _hidden(sees(u, li))]
    for u, li in live:
        z = _dot(keys(jnp.maximum(n_tiles - 1 - u, 0), lane_group(li)[0]), queries[li])
        if not isinstance(sees(u, li), str):
            z = jnp.where(sees(u, li), z, MASKED)
        if u >= 2:
            z = jnp.where(u < n_tiles, z, MASKED)
        z_buf[u, li] = z
    for u, li in live:
        sp_buf[u, li] = softplus2(z_buf[u, li]).astype(BF16)
    col_sums = {}
    for u, li in live:
        within = _dot(suffix, sp_buf[u, li])
        col_sums[u, li] = within[0:1, :]
        z_buf[u, li] = z_buf[u, li] - within
    later = [jnp.zeros((1, cw), F32)] * n_lanes
    acc = [jnp.zeros((HEAD_DIM, cw), F32)] * n_lanes
    for u in range(n_wide):
        for li in range(n_lanes):
            if (u, li) in col_sums:
                pp, hh, _ = lane_group(li)
                a = jnp.exp2(z_buf[u, li] - later[li])
                acc[li] = acc[li] + _dot(values(jnp.maximum(n_tiles - 1 - u, 0), pp, hh), a.astype(BF16))
        later = [later[li] + col_sums[u, li] if (u, li) in col_sums else later[li] for li in range(n_lanes)]
    for li in range(n_lanes):
        acc_buf[li] = acc[li]

    n_rest = n_tiles - n_wide

    def rest_matters(first_unscored, mass):
        j_rest = jnp.maximum(n_rest - 1 - first_unscored, 0)
        worst = None
        for li in range(n_lanes):
            pp, hh, _ = lane_group(li)
            bound = q_norm[li] * stats_ref[0, pp, j_rest, 4 + hh:5 + hh, :] - mass[li]
            worst = bound if worst is None else jnp.maximum(worst, bound)
        return jnp.max(worst) >= -PRUNE_LOG2

    def step(t, slot, later, score=True, softplus=True, cumsum=True, weight=True):
        new_later = []
        for li in range(n_lanes):
            pp, hh, _ = lane_group(li)
            if cumsum:
                within = _dot(suffix, sp_buf[1 - slot, li])
                e_buf[1 - slot, li] = z_buf[1 - slot, li] - within
                wrow_buf[1 - slot, li] = within[0:1, :]
            if score:
                z_buf[1 - slot, li] = _dot(keys(n_rest - 1 - (t + 3), pp), queries[li])
            if weight:
                a = jnp.exp2(e_buf[slot, li] - later[li])
                acc_buf[li] += _dot(values(n_rest - 1 - jnp.maximum(t, 0), pp, hh), a.astype(BF16))
                new_later.append(later[li] + wrow_buf[slot, li])
            else:
                new_later.append(later[li])
            if softplus:
                sp_buf[slot, li] = softplus2(z_buf[slot, li]).astype(BF16)
        return tuple(new_later)

    def pair_and_check(state):
        i, _, later = state
        t = 2 * i - 1
        later = step(t + 1, 0, step(t, 1, later))
        mass = [later[li] + wrow_buf[1, li] for li in range(n_lanes)]
        return i + 1, rest_matters(2 * i + 4, mass), later

    @pl.when((n_rest > 0) & rest_matters(0, later))
    def _():
        e_buf[1] = jnp.full(e_buf.shape[1:], MASKED, F32)
        wrow_buf[1] = jnp.zeros(wrow_buf.shape[1:], F32)
        mass = step(-3, 1, tuple(later), softplus=False, cumsum=False, weight=False)
        mass = step(-2, 0, mass, cumsum=False, weight=False)
        n_pairs, _, mass = lax.while_loop(lambda st: (2 * st[0] + 2 < n_rest) & st[1], pair_and_check,
                                          (jnp.int32(0), jnp.bool_(True), mass))
        mass = step(2 * n_pairs - 1, 1, mass, score=False)
        mass = step(2 * n_pairs, 0, mass, score=False, softplus=False)
        step(2 * n_pairs + 1, 1, mass, score=False, softplus=False, cumsum=False)

    _finish_heads([acc_buf[li] for li in range(n_lanes)], g_ref, o_ref)


def _attention(body, name, pairs, scratch, qk, k_arr, k_spec, vt, g, stats, q_block0, vt_block0, n_heads):
    bsz, s, _ = qk.shape
    bq, bk, width = ATT_Q, ATT_K, pairs * LANES
    d_grp = n_heads * HEAD_DIM
    assert (n_heads // 2) % pairs == 0
    return pl.pallas_call(
        body,
        grid=(bsz, n_heads // 2 // pairs, s // bq),
        in_specs=[pl.BlockSpec((1, bq, width), lambda b, p, i: (b, i, q_block0 + p)),
                  k_spec,
                  pl.BlockSpec((1, s // bk, width, bk), lambda b, p, i: (b, 0, vt_block0 + p, 0)),
                  pl.BlockSpec((1, width), lambda b, p, i: (0, p)),
                  pl.BlockSpec((1, pairs) + stats.shape[2:], lambda b, p, i: (b, p, 0, 0, 0))],
        out_specs=pl.BlockSpec((1, bq, width), lambda b, p, i: (b, i, p)),
        out_shape=jax.ShapeDtypeStruct((bsz, s, d_grp), BF16),
        scratch_shapes=scratch,
        compiler_params=_params("arbitrary", "arbitrary", "arbitrary"),
        name=name,
    )(qk, k_arr, vt, g.reshape(1, d_grp), stats)


def _lane_groups(pairs):
    return pairs * 2 * (ATT_Q // ATT_COLS)


def _fox_scratch(pairs):
    n = _lane_groups(pairs)
    return [pltpu.VMEM((2, n, ATT_K, ATT_COLS), F32), pltpu.VMEM((2, n, 1, ATT_COLS), F32),
            pltpu.VMEM((2, n, ATT_K, ATT_COLS), BF16), pltpu.VMEM((n, HEAD_DIM, ATT_COLS), F32)]


def _sb_scratch(pairs):
    n = _lane_groups(pairs)
    return [pltpu.VMEM((SB_WIDE_TILES, n, ATT_K, ATT_COLS), F32), pltpu.VMEM((SB_WIDE_TILES, n, ATT_K, ATT_COLS), BF16),
            pltpu.VMEM((2, n, ATT_K, ATT_COLS), F32), pltpu.VMEM((2, n, 1, ATT_COLS), F32),
            pltpu.VMEM((n, HEAD_DIM, ATT_COLS), F32)]


def _outproj_kernel(x_ref, mf_ref, ms_ref, w_ref, mod_ref, g_ref, x1_ref, h2_ref):
    mix = jnp.concatenate([mf_ref[0], ms_ref[0]], axis=-1)
    x1 = x_ref[0] + mod_ref[0, 2:3, :] * _dot(mix, w_ref[...])
    x1_ref[0] = x1
    shift = mod_ref[0, 3:4, :]
    scale = mod_ref[0, 4:5, :]
    h2_ref[0] = (_rms_rows(x1) * g_ref[...] * (1.0 + scale) + shift).astype(BF16)


def _outproj(x, mix_f, mix_s, w_out, mod, g):
    bsz, s, d = x.shape
    tm = OUT_ROWS
    row = lambda b, i: (b, i, 0)
    return pl.pallas_call(
        _outproj_kernel,
        grid=(bsz, s // tm),
        in_specs=[pl.BlockSpec((1, tm, d), row),
                  pl.BlockSpec((1, tm, mix_f.shape[2]), row),
                  pl.BlockSpec((1, tm, mix_s.shape[2]), row),
                  pl.BlockSpec(w_out.shape, lambda b, i: (0, 0)),
                  pl.BlockSpec((1, N_MOD, d), lambda b, i: (b, 0, 0)),
                  pl.BlockSpec((1, d), lambda b, i: (0, 0))],
        out_specs=[pl.BlockSpec((1, tm, d), row), pl.BlockSpec((1, tm, d), row)],
        out_shape=[jax.ShapeDtypeStruct((bsz, s, d), F32), jax.ShapeDtypeStruct((bsz, s, d), BF16)],
        compiler_params=_params("arbitrary", "arbitrary"),
        name="outproj",
    )(x, mix_f, mix_s, w_out, mod, g)


def _mlp_kernel(h_ref, halo_ref, x1_ref, mod_ref, wu_ref, cw_ref, cb_ref, wd_ref, gf_ref, o_ref,
                u_buf, acc_ref, *, final_norm):
    i = pl.program_id(1)
    tm = h_ref.shape[1]
    n_chunks = wd_ref.shape[0]
    halo = halo_ref[0]
    halo = jnp.where(i > 0, halo, jnp.zeros_like(halo))
    hx = jnp.concatenate([halo, h_ref[0]], axis=0)
    acc_ref[...] = jnp.zeros_like(acc_ref)

    def project_up(c, slot):
        for br in range(2):
            u_buf[slot, br] = _dot(hx, wu_ref[br, c])

    def mix_down(c, slot):
        branches = []
        for br in range(2):
            out = cb_ref[br, c]
            for tap in range(CONV_WIDTH):
                first = BF16_SUBLANES - (CONV_WIDTH - 1 - tap)
                out = out + cw_ref[br, c, tap:tap + 1, :] * u_buf[slot, br, pl.ds(first, tm), :]
            branches.append(out)
        u_gate, u_val = branches
        acc_ref[...] += _dot((u_gate * jax.nn.sigmoid(u_gate) * u_val).astype(BF16), wd_ref[c])

    def chunk_pair(j, _):
        c = 2 * j
        project_up(c + 1, 1)
        mix_down(c, 0)
        project_up(c + 2, 0)
        mix_down(c + 1, 1)
        return 0

    project_up(0, 0)
    lax.fori_loop(0, (n_chunks - 1) // 2, chunk_pair, 0)
    mix_down(n_chunks - 1, 0)
    x2 = x1_ref[0] + mod_ref[0, 5:6, :] * acc_ref[...]
    o_ref[0] = _rms_rows(x2) * gf_ref[...] if final_norm else x2


def _mlp(h2, x1, mod, w_up, conv_w, conv_b, w_down, g_final, final_norm):
    bsz, s, d = x1.shape
    tm = OUT_ROWS
    n_chunks, tf = w_down.shape[0], w_down.shape[1]
    assert n_chunks % 2 == 1
    halo_blocks = tm // BF16_SUBLANES
    row = lambda b, i: (b, i, 0)
    resident = lambda a: pl.BlockSpec(a.shape, lambda b, i: (0,) * a.ndim, pipeline_mode=pl.Buffered(1))
    return pl.pallas_call(
        functools.partial(_mlp_kernel, final_norm=final_norm),
        grid=(bsz, s // tm),
        in_specs=[pl.BlockSpec((1, tm, d), row),
                  pl.BlockSpec((1, BF16_SUBLANES, d), lambda b, i: (b, jnp.maximum(i * halo_blocks - 1, 0), 0)),
                  pl.BlockSpec((1, tm, d), row),
                  pl.BlockSpec((1, N_MOD, d), lambda b, i: (b, 0, 0)),
                  resident(w_up), resident(conv_w), resident(conv_b), resident(w_down),
                  pl.BlockSpec((1, d), lambda b, i: (0, 0))],
        out_specs=pl.BlockSpec((1, tm, d), row),
        out_shape=jax.ShapeDtypeStruct((bsz, s, d), F32),
        scratch_shapes=[pltpu.VMEM((2, 2, tm + BF16_SUBLANES, tf), F32), pltpu.VMEM((tm, d), F32)],
        compiler_params=_params("arbitrary", "arbitrary"),
        name="mlp",
    )(h2, h2, x1, mod, w_up, conv_w, conv_b, w_down, g_final)


def _chunk_columns(a, d_ff, n_chunks, tf):
    halves = jnp.stack([a[:, :d_ff], a[:, d_ff:]])
    halves = jnp.pad(halves, ((0, 0), (0, 0), (0, n_chunks * tf - d_ff)))
    return halves.reshape(2, a.shape[0], n_chunks, tf).transpose(0, 2, 1, 3)


def _pad_cols(a, n):
    return jnp.pad(a, ((0, 0), (0, n - a.shape[1])))


def kernel(x, c, w_ada, b_ada, g_attn, w_in, b_fgate, g_out_fox, g_out_sb, w_out,
           g_mlp, w_up, conv_w, conv_b, w_down, g_final):
    depth, d, _ = w_ada.shape
    n_fox = b_fgate.shape[1]
    d_fox = n_fox * HEAD_DIM
    d_sb = g_out_sb.shape[1]
    n_sb = d_sb // HEAD_DIM
    d_ff = w_down.shape[1]
    d_ff_pad = -(-d_ff // FF_CHUNK) * FF_CHUNK
    assert n_fox % 2 == 0 and n_sb == n_fox and 3 * n_fox <= LANES
    assert x.shape[1] % OUT_ROWS == 0 and x.shape[1] % ATT_Q == 0 and ATT_Q == 2 * ATT_K
    o_kf, o_vf, o_qs, o_ks, o_vs, o_gate = (d_fox, 2 * d_fox, 3 * d_fox, 3 * d_fox + d_sb,
                                             3 * d_fox + 2 * d_sb, 3 * d_fox + 3 * d_sb)

    for l in range(depth):
        mod = _ada(c, w_ada[l], b_ada[l]).reshape(-1, N_MOD, d)
        w = w_in[l]
        w_nat = jnp.concatenate([w[:, :o_vf], w[:, o_qs:o_vs]], axis=1).astype(BF16)
        w_vt = jnp.concatenate([w[:, o_vf:o_qs], w[:, o_vs:o_gate]], axis=1).T.astype(BF16)
        w_gate = _pad_cols(w[:, o_gate:], LANES).astype(BF16)
        b_gate = _pad_cols(b_fgate[l].reshape(1, n_fox), LANES)
        qk, vt, log_f = _inproj(x, mod, g_attn[l].reshape(1, d), w_nat, w_vt, w_gate, b_gate)

        k_aug, stats = _decay(log_f, qk, n_fox, k_fox_block=1, k_sb_block=3)
        steps_f, steps_s = n_fox // 2 // FOX_PAIRS, n_sb // 2 // SB_PAIRS
        fox_k_spec = pl.BlockSpec((1, 2 * FOX_PAIRS, x.shape[1], LANES), lambda b, p, i: (b, p, 0, 0))
        mix_f = _attention(_fox_kernel, "fox", FOX_PAIRS, _fox_scratch(FOX_PAIRS), qk, k_aug, fox_k_spec, vt,
                           g_out_fox[l], stats, q_block0=0, vt_block0=0, n_heads=n_fox)
        sb_k_spec = pl.BlockSpec((1, x.shape[1], SB_PAIRS * LANES), lambda b, p, i: (b, 0, 3 * steps_s + p))
        mix_s = _attention(_sb_kernel, "sb", SB_PAIRS, _sb_scratch(SB_PAIRS), qk, qk, sb_k_spec, vt,
                           g_out_sb[l], stats, q_block0=2 * steps_s, vt_block0=steps_s, n_heads=n_sb)

        x1, h2 = _outproj(x, mix_f, mix_s, w_out[l].astype(BF16), mod, g_mlp[l].reshape(1, d))

        n_ff = d_ff_pad // FF_CHUNK
        x = _mlp(h2, x1, mod,
                 _chunk_columns(w_up[l], d_ff, n_ff, FF_CHUNK).astype(BF16),
                 _chunk_columns(conv_w[l], d_ff, n_ff, FF_CHUNK),
                 _chunk_columns(conv_b[l].reshape(1, -1), d_ff, n_ff, FF_CHUNK),
                 jnp.pad(w_down[l], ((0, d_ff_pad - d_ff), (0, 0))).astype(BF16).reshape(n_ff, FF_CHUNK, d),
                 g_final.reshape(1, d), final_norm=(l == depth - 1))
    return x
```

```python
import functools

import numpy as np
import jax
import jax.numpy as jnp
from jax import lax
from jax.experimental import pallas as pl
from jax.experimental.pallas import tpu as pltpu

HEAD_DIM = 64
N_MOD = 6
CONV_WIDTH = 3
EPS = 1e-6

LANES = 128
BF16_SUBLANES = 16
VMEM_LIMIT_BYTES = 48 * 1024 * 1024

ATT_Q = 512
ATT_K = 256
ATT_COLS = 256
FOX_PAIRS = 1
SB_PAIRS = 2
SB_WIDE_TILES = 4
PROJ_ROWS = ATT_K
LOG2E = 1.4426950408889634
MASKED = -1e30
M_INIT = -1e29
EXP2_MAX = 126.0
PRUNE_LOG2 = 160.0
NORM_SLACK = 1.02
STATS_ROWS = 8
OUT_ROWS = 512
FF_CHUNK = 256

F32 = jnp.float32
BF16 = jnp.bfloat16
NT_DIMS = (((1,), (1,)), ((), ()))


def _dot(a, b):
    return jnp.dot(a, b, preferred_element_type=F32)


def _dot_nt(a, b):
    return lax.dot_general(a, b, NT_DIMS, preferred_element_type=F32)


def _params(*sem):
    return pltpu.CompilerParams(dimension_semantics=sem, vmem_limit_bytes=VMEM_LIMIT_BYTES)


def _rms_rows(x):
    return x * lax.rsqrt(jnp.mean(x * x, axis=-1, keepdims=True) + EPS)


def _softplus(z):
    return jnp.maximum(z, 0.0) + jnp.log(1.0 + jnp.exp(-jnp.abs(z)))


def _split3(x):
    hi = x.astype(BF16)
    r1 = x - hi.astype(F32)
    mid = r1.astype(BF16)
    lo = (r1 - mid.astype(F32)).astype(BF16)
    return hi, mid, lo


def _ada_kernel(c_ref, w_ref, b_ref, o_ref):
    c = c_ref[...]
    o_ref[...] = _dot(c * jax.nn.sigmoid(c), w_ref[...]) + b_ref[...]


def _ada(c, w, b):
    bsz, d = c.shape
    n = w.shape[1]
    return pl.pallas_call(
        _ada_kernel,
        grid=(n // d,),
        in_specs=[pl.BlockSpec((bsz, d), lambda j: (0, 0)),
                  pl.BlockSpec((d, d), lambda j: (0, j)),
                  pl.BlockSpec((1, d), lambda j: (0, j))],
        out_specs=pl.BlockSpec((bsz, d), lambda j: (0, j)),
        out_shape=jax.ShapeDtypeStruct((bsz, n), F32),
        compiler_params=_params("arbitrary"),
        name="ada",
    )(c, w, b.reshape(1, n))


def _inproj_kernel(x_ref, mod_ref, g_ref, wn_ref, wvt_ref, wg_ref, bg_ref, qk_ref, vt_ref, lf_ref):
    shift = mod_ref[0, 0:1, :]
    scale = mod_ref[0, 1:2, :]
    h = (_rms_rows(x_ref[0]) * g_ref[...] * (1.0 + scale) + shift).astype(BF16)
    qk_ref[0] = _dot(h, wn_ref[...]).astype(BF16)
    vt_ref[0, 0] = _dot_nt(wvt_ref[...], h).astype(BF16)
    logit = _dot(h, wg_ref[...]) + bg_ref[...]
    lf_ref[0] = -_softplus(-logit)


def _inproj(x, mod, g, w_nat, w_vt, w_gate, b_gate):
    bsz, s, d = x.shape
    tm = PROJ_ROWS
    n_nat, n_v = w_nat.shape[1], w_vt.shape[0]
    const = lambda b, i: (0, 0)
    return pl.pallas_call(
        _inproj_kernel,
        grid=(bsz, s // tm),
        in_specs=[pl.BlockSpec((1, tm, d), lambda b, i: (b, i, 0)),
                  pl.BlockSpec((1, N_MOD, d), lambda b, i: (b, 0, 0)),
                  pl.BlockSpec((1, d), const),
                  pl.BlockSpec((d, n_nat), const),
                  pl.BlockSpec((n_v, d), const),
                  pl.BlockSpec((d, LANES), const),
                  pl.BlockSpec((1, LANES), const)],
        out_specs=[pl.BlockSpec((1, tm, n_nat), lambda b, i: (b, i, 0)),
                   pl.BlockSpec((1, 1, n_v, tm), lambda b, i: (b, i, 0, 0)),
                   pl.BlockSpec((1, tm, LANES), lambda b, i: (b, i, 0))],
        out_shape=[jax.ShapeDtypeStruct((bsz, s, n_nat), BF16),
                   jax.ShapeDtypeStruct((bsz, s // tm, n_v, tm), BF16),
                   jax.ShapeDtypeStruct((bsz, s, LANES), F32)],
        compiler_params=_params("arbitrary", "arbitrary"),
        name="inproj",
    )(x, mod, g, w_nat, w_vt, w_gate, b_gate)


def _decay_kernel(lf_ref, k_ref, ks_ref, sel_ref, ind_ref, kaug_ref, stats_ref, carry_ref, kpre_ref, *, n_heads):
    @pl.when(pl.program_id(1) == 0)
    def _():
        carry_ref[...] = jnp.zeros_like(carry_ref)
        kpre_ref[...] = jnp.zeros_like(kpre_ref)

    tm = lf_ref.shape[1]
    lane = lax.broadcasted_iota(jnp.int32, (tm, LANES), 1)
    lf = jnp.where(lane < n_heads, lf_ref[0], 0.0)
    row = lax.broadcasted_iota(jnp.int32, (tm, tm), 0)
    col = lax.broadcasted_iota(jnp.int32, (tm, tm), 1)
    tri = (col <= row).astype(BF16)
    hi, mid, lo = _split3(lf)
    f_run = carry_ref[...] + (_dot(tri, hi) + _dot(tri, mid) + _dot(tri, lo))
    carry_ref[...] = f_run[tm - 1:tm, :]
    ghi, gmid, glo = _split3(-LOG2E * f_run)
    packed = (ghi.astype(F32) + pltpu.roll(gmid.astype(F32), n_heads, 1)
              + pltpu.roll(glo.astype(F32), 2 * n_heads, 1)).astype(BF16)
    placed = _dot(packed, sel_ref[...])
    k_all = k_ref[0]
    for h in range(n_heads):
        k_pair = k_all[:, (h // 2) * LANES:(h // 2 + 1) * LANES]
        own = (lane < HEAD_DIM) if h % 2 == 0 else (lane >= HEAD_DIM)
        kaug_ref[0, h] = jnp.where(own, k_pair, placed[:, h * LANES:(h + 1) * LANES].astype(BF16))

    def head_norm_bound(k):
        k32 = k.astype(F32)
        sq = _dot((k32 * k32).astype(BF16), ind_ref[...])
        return jnp.sqrt(jnp.max(sq, axis=0, keepdims=True) * NORM_SLACK)

    kpre_f = jnp.maximum(kpre_ref[0:1, :], head_norm_bound(k_all))
    kpre_s = jnp.maximum(kpre_ref[1:2, :], head_norm_bound(ks_ref[0]))
    kpre_ref[0:1, :] = kpre_f
    kpre_ref[1:2, :] = kpre_s
    g_end = -LOG2E * f_run[tm - 1:tm, :]
    lane1 = lax.broadcasted_iota(jnp.int32, (1, LANES), 1)

    def spread(v, h):
        return jnp.broadcast_to(jnp.sum(jnp.where(lane1 == h, v, 0.0), axis=1, keepdims=True), (1, ATT_COLS))

    for p in range(n_heads // 2):
        rows = [spread(v, 2 * p + hh) for v in (kpre_f, g_end, kpre_s) for hh in range(2)]
        rows += [jnp.zeros((1, ATT_COLS), F32)] * (stats_ref.shape[3] - len(rows))
        stats_ref[0, p, 0] = jnp.concatenate(rows, axis=0)


def _head_indicator(n_heads):
    ind = np.zeros((n_heads * HEAD_DIM, LANES), np.float32)
    ind[np.arange(n_heads * HEAD_DIM), np.arange(n_heads * HEAD_DIM) // HEAD_DIM] = 1.0
    return jnp.asarray(ind, BF16)


def _decay_select_matrix(n_heads):
    sel = np.zeros((LANES, n_heads * LANES), np.float32)
    for h in range(n_heads):
        base = h * LANES + (HEAD_DIM if h % 2 == 0 else 0)
        for term in range(3):
            sel[term * n_heads + h, base + term] = 1.0
    return jnp.asarray(sel, BF16)


def _decay(log_f, qk, n_heads, k_fox_block, k_sb_block):
    bsz, s, _ = log_f.shape
    tm = PROJ_ROWS
    d_grp = n_heads * HEAD_DIM
    return pl.pallas_call(
        functools.partial(_decay_kernel, n_heads=n_heads),
        grid=(bsz, s // tm),
        in_specs=[pl.BlockSpec((1, tm, LANES), lambda b, i: (b, i, 0)),
                  pl.BlockSpec((1, tm, d_grp), lambda b, i: (b, i, k_fox_block)),
                  pl.BlockSpec((1, tm, d_grp), lambda b, i: (b, i, k_sb_block)),
                  pl.BlockSpec((LANES, n_heads * LANES), lambda b, i: (0, 0)),
                  pl.BlockSpec((d_grp, LANES), lambda b, i: (0, 0))],
        out_specs=[pl.BlockSpec((1, n_heads, tm, LANES), lambda b, i: (b, 0, i, 0)),
                   pl.BlockSpec((1, n_heads // 2, 1, STATS_ROWS, ATT_COLS), lambda b, i: (b, 0, i, 0, 0))],
        out_shape=[jax.ShapeDtypeStruct((bsz, n_heads, s, LANES), BF16),
                   jax.ShapeDtypeStruct((bsz, n_heads // 2, s // tm, STATS_ROWS, ATT_COLS), F32)],
        scratch_shapes=[pltpu.VMEM((1, LANES), F32), pltpu.VMEM((2, LANES), F32)],
        compiler_params=_params("arbitrary", "arbitrary"),
        name="decay",
    )(log_f, qk, qk, _decay_select_matrix(n_heads), _head_indicator(n_heads))


def _lane_queries(q_ref, extra_even, extra_odd, cw):
    out = []
    for pp in range(q_ref.shape[2] // LANES):
        q = q_ref[0, :, pp * LANES:(pp + 1) * LANES].astype(F32) * (HEAD_DIM ** -0.5 * LOG2E)
        lane = lax.broadcasted_iota(jnp.int32, q.shape, 1)
        heads = (jnp.where(lane < HEAD_DIM, q, extra_even(lane)).T.astype(BF16),
                 jnp.where(lane >= HEAD_DIM, q, extra_odd(lane)).T.astype(BF16))
        out += [heads[hh][:, c * cw:(c + 1) * cw] for hh in range(2) for c in range(q.shape[0] // cw)]
    return out


def _visibility(first_key, first_query, bk, cw, strict):
    last_visible_gap = -1 if strict else 0
    if first_key + bk - 1 - first_query <= last_visible_gap:
        return "all"
    if first_key - (first_query + cw - 1) > last_visible_gap:
        return "none"
    gap = (lax.broadcasted_iota(jnp.int32, (bk, cw), 0) - lax.broadcasted_iota(jnp.int32, (bk, cw), 1)
           + (first_key - first_query))
    return gap <= last_visible_gap


def _diag_visibility(u, c, bk, cw, strict):
    return _visibility((1 - u) * bk, c * cw, bk, cw, strict)


def _hidden(visibility):
    return isinstance(visibility, str) and visibility == "none"


def _query_norm_bounds(queries, n_chunks):
    bounds = []
    for li, q in enumerate(queries):
        hh = (li // n_chunks) % 2
        own = q[hh * HEAD_DIM:(hh + 1) * HEAD_DIM, :].astype(F32)
        bounds.append(jnp.sqrt(jnp.sum(own * own, axis=0, keepdims=True) * NORM_SLACK))
    return bounds


def _finish_heads(lanes, g_ref, o_ref):
    n_pairs = o_ref.shape[2] // LANES
    n_chunks = len(lanes) // (2 * n_pairs)
    for pp in range(n_pairs):
        mine = lanes[2 * pp * n_chunks:2 * (pp + 1) * n_chunks]
        outs = [jnp.concatenate(mine[hh * n_chunks:(hh + 1) * n_chunks], axis=1) for hh in range(2)]
        normed = [o * lax.rsqrt(jnp.mean(o * o, axis=0, keepdims=True) + EPS) for o in outs]
        cols = slice(pp * LANES, (pp + 1) * LANES)
        o_ref[0, :, cols] = (jnp.concatenate(normed, axis=0).T * g_ref[:, cols]).astype(o_ref.dtype)


def _fox_kernel(q_ref, k_ref, vt_ref, g_ref, stats_ref, o_ref, s_buf, cmax_buf, p_buf, acc_buf):
    qi = pl.program_id(2)
    bk = vt_ref.shape[3]
    n_lanes, cw = acc_buf.shape[0], acc_buf.shape[2]
    n_chunks = q_ref.shape[1] // cw
    lane_group = lambda li: (li // (2 * n_chunks), (li // n_chunks) % 2, li % n_chunks)
    n_tiles = 2 * (qi + 1)
    ones3 = lambda lo: (lambda lane: jnp.where((lane >= lo) & (lane < lo + 3), 1.0, 0.0))
    queries = _lane_queries(q_ref, ones3(HEAD_DIM), ones3(0), cw)
    acc_buf[...] = jnp.zeros(acc_buf.shape, F32)
    for li in range(n_lanes):
        if _hidden(_diag_visibility(0, lane_group(li)[2], bk, cw, strict=False)):
            p_buf[0, li] = jnp.zeros((bk, cw), BF16)

    def step(t, slot, carry, score="below", softmax="below", value=True):
        new = []
        for li in range(n_lanes):
            pp, hh, c = lane_group(li)
            see = lambda u: "all" if u == "below" else _diag_visibility(u, c, bk, cw, strict=False)
            if score is not None and not _hidden(see(score)):
                start = pl.multiple_of((n_tiles - 2 - t) * bk, bk)
                s_new = _dot(k_ref[0, 2 * pp + hh, pl.ds(start, bk), :], queries[li])
                if not isinstance(see(score), str):
                    s_new = jnp.where(see(score), s_new, MASKED)
                s_buf[1 - slot, li] = s_new
                cmax_buf[1 - slot, li] = jnp.max(s_new, axis=0, keepdims=True)
            pv = None
            if value:
                vt = vt_ref[0, n_tiles - t, pl.ds(pp * LANES + hh * HEAD_DIM, HEAD_DIM), :]
                pv = _dot(vt, p_buf[1 - slot, li])
            m, l = carry[li]
            if softmax is not None and not _hidden(see(softmax)):
                m_new = jnp.maximum(m, cmax_buf[slot, li])
                alpha = jnp.exp2(m - m_new)
                p = jnp.exp2(s_buf[slot, li] - m_new)
                p_buf[slot, li] = p.astype(BF16)
                m, l = m_new, alpha * l + jnp.sum(p, axis=0, keepdims=True)
                acc_buf[li] = alpha * (acc_buf[li] if pv is None else acc_buf[li] + pv)
            elif pv is not None:
                acc_buf[li] += pv
            new.append((m, l))
        return tuple(new)

    def step_pair(i, carry):
        t = 2 * i + 1
        return step(t + 1, 0, step(t, 1, carry))

    q_norm = _query_norm_bounds(queries, n_chunks)

    def later_tiles_matter(i, carry):
        j_rest = jnp.maximum(n_tiles - 5 - 2 * i, 0)
        worst = None
        for li in range(n_lanes):
            pp, hh, _ = lane_group(li)
            bound = (q_norm[li] * stats_ref[0, pp, j_rest, hh:hh + 1, :]
                     + stats_ref[0, pp, j_rest, 2 + hh:3 + hh, :] - carry[li][0])
            worst = bound if worst is None else jnp.maximum(worst, bound)
        return jnp.max(worst) >= -PRUNE_LOG2

    def pair_and_check(state):
        i, _, carry = state
        carry = step_pair(i, carry)
        return i + 1, later_tiles_matter(i, carry), carry

    carry = tuple((jnp.full((1, cw), M_INIT, F32), jnp.zeros((1, cw), F32)) for _ in range(n_lanes))
    carry = step(-1, 1, carry, score=0, softmax=None, value=False)
    carry = step(0, 0, carry, score=1, softmax=0, value=False)
    n_pairs, _, carry = lax.while_loop(lambda st: (st[0] < qi) & st[1], pair_and_check,
                                       (jnp.int32(0), jnp.bool_(True), carry))
    carry = step(2 * n_pairs + 1, 1, carry, score=None)
    carry = step(2 * n_pairs + 2, 0, carry, score=None, softmax=None)
    _finish_heads([acc_buf[li] / carry[li][1] for li in range(n_lanes)], g_ref, o_ref)


def _sb_kernel(q_ref, k_ref, vt_ref, g_ref, stats_ref, o_ref, z_buf, sp_buf, e_buf, wrow_buf, acc_buf):
    qi = pl.program_id(2)
    bk = vt_ref.shape[3]
    n_wide = z_buf.shape[0]
    n_lanes, cw = acc_buf.shape[0], acc_buf.shape[2]
    n_chunks = q_ref.shape[1] // cw
    lane_group = lambda li: (li // (2 * n_chunks), (li // n_chunks) % 2, li % n_chunks)
    n_tiles = 2 * (qi + 1)
    zero = lambda lane: 0.0
    queries = _lane_queries(q_ref, zero, zero, cw)
    q_norm = _query_norm_bounds(queries, n_chunks)
    suffix = (lax.broadcasted_iota(jnp.int32, (bk, bk), 1)
              >= lax.broadcasted_iota(jnp.int32, (bk, bk), 0)).astype(BF16)
    softplus2 = lambda z: jnp.maximum(z, jnp.log2(1.0 + jnp.exp2(jnp.minimum(z, EXP2_MAX))))
    keys = lambda j, pp: k_ref[0, pl.ds(pl.multiple_of(j * bk, bk), bk), pp * LANES:(pp + 1) * LANES]
    values = lambda j, pp, hh: vt_ref[0, j, pl.ds(pp * LANES + hh * HEAD_DIM, HEAD_DIM), :]

    sees = lambda u, li: _diag_visibility(u, lane_group(li)[2], bk, cw, strict=True) if u < 2 else "all"
    live = [(u, li) for u in range(n_wide) for li in range(n_lanes) if not _hidden(sees(u, li))]
    for u, li in live:
        z = _dot(keys(jnp.maximum(n_tiles - 1 - u, 0), lane_group(li)[0]), queries[li])
        if not isinstance(sees(u, li), str):
            z = jnp.where(sees(u, li), z, MASKED)
        if u >= 2:
            z = jnp.where(u < n_tiles, z, MASKED)
        z_buf[u, li] = z
    for u, li in live:
        sp_buf[u, li] = softplus2(z_buf[u, li]).astype(BF16)
    col_sums = {}
    for u, li in live:
        within = _dot(suffix, sp_buf[u, li])
        col_sums[u, li] = within[0:1, :]
        z_buf[u, li] = z_buf[u, li] - within
    later = []
    for li in range(n_lanes):
        pp, hh, _ = lane_group(li)
        mass, acc = jnp.zeros((1, cw), F32), jnp.zeros((HEAD_DIM, cw), F32)
        for u in range(n_wide):
            if (u, li) in col_sums:
                a = jnp.exp2(z_buf[u, li] - mass)
                acc = acc + _dot(values(jnp.maximum(n_tiles - 1 - u, 0), pp, hh), a.astype(BF16))
                mass = mass + col_sums[u, li]
        acc_buf[li] = acc
        later.append(mass)

    n_rest = n_tiles - n_wide

    def rest_matters(first_unscored, mass):
        j_rest = jnp.maximum(n_rest - 1 - first_unscored, 0)
        worst = None
        for li in range(n_lanes):
            pp, hh, _ = lane_group(li)
            bound = q_norm[li] * stats_ref[0, pp, j_rest, 4 + hh:5 + hh, :] - mass[li]
            worst = bound if worst is None else jnp.maximum(worst, bound)
        return jnp.max(worst) >= -PRUNE_LOG2

    def step(t, slot, later, score=True, softplus=True, cumsum=True, weight=True):
        new_later = []
        for li in range(n_lanes):
            pp, hh, _ = lane_group(li)
            if cumsum:
                within = _dot(suffix, sp_buf[1 - slot, li])
                e_buf[1 - slot, li] = z_buf[1 - slot, li] - within
                wrow_buf[1 - slot, li] = within[0:1, :]
            if score:
                z_buf[1 - slot, li] = _dot(keys(n_rest - 1 - (t + 3), pp), queries[li])
            if weight:
                a = jnp.exp2(e_buf[slot, li] - later[li])
                acc_buf[li] += _dot(values(n_rest - 1 - jnp.maximum(t, 0), pp, hh), a.astype(BF16))
                new_later.append(later[li] + wrow_buf[slot, li])
            else:
                new_later.append(later[li])
            if softplus:
                sp_buf[slot, li] = softplus2(z_buf[slot, li]).astype(BF16)
        return tuple(new_later)

    def pair_and_check(state):
        i, _, later = state
        t = 2 * i - 1
        later = step(t + 1, 0, step(t, 1, later))
        mass = [later[li] + wrow_buf[1, li] for li in range(n_lanes)]
        return i + 1, rest_matters(2 * i + 4, mass), later

    @pl.when((n_rest > 0) & rest_matters(0, later))
    def _():
        e_buf[1] = jnp.full(e_buf.shape[1:], MASKED, F32)
        wrow_buf[1] = jnp.zeros(wrow_buf.shape[1:], F32)
        mass = step(-3, 1, tuple(later), softplus=False, cumsum=False, weight=False)
        mass = step(-2, 0, mass, cumsum=False, weight=False)
        n_pairs, _, mass = lax.while_loop(lambda st: (2 * st[0] + 2 < n_rest) & st[1], pair_and_check,
                                          (jnp.int32(0), jnp.bool_(True), mass))
        mass = step(2 * n_pairs - 1, 1, mass, score=False)
        mass = step(2 * n_pairs, 0, mass, score=False, softplus=False)
        step(2 * n_pairs + 1, 1, mass, score=False, softplus=False, cumsum=False)

    _finish_heads([acc_buf[li] for li in range(n_lanes)], g_ref, o_ref)


def _attention(body, name, pairs, scratch, qk, k_arr, k_spec, vt, g, stats, q_block0, vt_block0, n_heads):
    bsz, s, _ = qk.shape
    bq, bk, width = ATT_Q, ATT_K, pairs * LANES
    d_grp = n_heads * HEAD_DIM
    assert (n_heads // 2) % pairs == 0
    return pl.pallas_call(
        body,
        grid=(bsz, n_heads // 2 // pairs, s // bq),
        in_specs=[pl.BlockSpec((1, bq, width), lambda b, p, i: (b, i, q_block0 + p)),
                  k_spec,
                  pl.BlockSpec((1, s // bk, width, bk), lambda b, p, i: (b, 0, vt_block0 + p, 0)),
                  pl.BlockSpec((1, width), lambda b, p, i: (0, p)),
                  pl.BlockSpec((1, pairs) + stats.shape[2:], lambda b, p, i: (b, p, 0, 0, 0))],
        out_specs=pl.BlockSpec((1, bq, width), lambda b, p, i: (b, i, p)),
        out_shape=jax.ShapeDtypeStruct((bsz, s, d_grp), BF16),
        scratch_shapes=scratch,
        compiler_params=_params("arbitrary", "arbitrary", "arbitrary"),
        name=name,
    )(qk, k_arr, vt, g.reshape(1, d_grp), stats)


def _lane_groups(pairs):
    return pairs * 2 * (ATT_Q // ATT_COLS)


def _fox_scratch(pairs):
    n = _lane_groups(pairs)
    return [pltpu.VMEM((2, n, ATT_K, ATT_COLS), F32), pltpu.VMEM((2, n, 1, ATT_COLS), F32),
            pltpu.VMEM((2, n, ATT_K, ATT_COLS), BF16), pltpu.VMEM((n, HEAD_DIM, ATT_COLS), F32)]


def _sb_scratch(pairs):
    n = _lane_groups(pairs)
    return [pltpu.VMEM((SB_WIDE_TILES, n, ATT_K, ATT_COLS), F32), pltpu.VMEM((SB_WIDE_TILES, n, ATT_K, ATT_COLS), BF16),
            pltpu.VMEM((2, n, ATT_K, ATT_COLS), F32), pltpu.VMEM((2, n, 1, ATT_COLS), F32),
            pltpu.VMEM((n, HEAD_DIM, ATT_COLS), F32)]


def _outproj_kernel(x_ref, mf_ref, ms_ref, w_ref, mod_ref, g_ref, x1_ref, h2_ref):
    mix = jnp.concatenate([mf_ref[0], ms_ref[0]], axis=-1)
    x1 = x_ref[0] + mod_ref[0, 2:3, :] * _dot(mix, w_ref[...])
    x1_ref[0] = x1
    shift = mod_ref[0, 3:4, :]
    scale = mod_ref[0, 4:5, :]
    h2_ref[0] = (_rms_rows(x1) * g_ref[...] * (1.0 + scale) + shift).astype(BF16)


def _outproj(x, mix_f, mix_s, w_out, mod, g):
    bsz, s, d = x.shape
    tm = OUT_ROWS
    row = lambda b, i: (b, i, 0)
    return pl.pallas_call(
        _outproj_kernel,
        grid=(bsz, s // tm),
        in_specs=[pl.BlockSpec((1, tm, d), row),
                  pl.BlockSpec((1, tm, mix_f.shape[2]), row),
                  pl.BlockSpec((1, tm, mix_s.shape[2]), row),
                  pl.BlockSpec(w_out.shape, lambda b, i: (0, 0)),
                  pl.BlockSpec((1, N_MOD, d), lambda b, i: (b, 0, 0)),
                  pl.BlockSpec((1, d), lambda b, i: (0, 0))],
        out_specs=[pl.BlockSpec((1, tm, d), row), pl.BlockSpec((1, tm, d), row)],
        out_shape=[jax.ShapeDtypeStruct((bsz, s, d), F32), jax.ShapeDtypeStruct((bsz, s, d), BF16)],
        compiler_params=_params("arbitrary", "arbitrary"),
        name="outproj",
    )(x, mix_f, mix_s, w_out, mod, g)


def _mlp_kernel(h_ref, halo_ref, x1_ref, mod_ref, wu_ref, cw_ref, cb_ref, wd_ref, gf_ref, o_ref,
                u_buf, acc_ref, *, final_norm):
    i = pl.program_id(1)
    tm = h_ref.shape[1]
    n_chunks = wd_ref.shape[0]
    halo = halo_ref[0]
    halo = jnp.where(i > 0, halo, jnp.zeros_like(halo))
    hx = jnp.concatenate([halo, h_ref[0]], axis=0)
    acc_ref[...] = jnp.zeros_like(acc_ref)

    def project_up(c, slot):
        for br in range(2):
            u_buf[slot, br] = _dot(hx, wu_ref[br, c])

    def mix_down(c, slot):
        branches = []
        for br in range(2):
            out = cb_ref[br, c]
            for tap in range(CONV_WIDTH):
                first = BF16_SUBLANES - (CONV_WIDTH - 1 - tap)
                out = out + cw_ref[br, c, tap:tap + 1, :] * u_buf[slot, br, pl.ds(first, tm), :]
            branches.append(out)
        u_gate, u_val = branches
        acc_ref[...] += _dot((u_gate * jax.nn.sigmoid(u_gate) * u_val).astype(BF16), wd_ref[c])

    def chunk_pair(j, _):
        c = 2 * j
        project_up(c + 1, 1)
        mix_down(c, 0)
        project_up(c + 2, 0)
        mix_down(c + 1, 1)
        return 0

    project_up(0, 0)
    lax.fori_loop(0, (n_chunks - 1) // 2, chunk_pair, 0)
    mix_down(n_chunks - 1, 0)
    x2 = x1_ref[0] + mod_ref[0, 5:6, :] * acc_ref[...]
    o_ref[0] = _rms_rows(x2) * gf_ref[...] if final_norm else x2


def _mlp(h2, x1, mod, w_up, conv_w, conv_b, w_down, g_final, final_norm):
    bsz, s, d = x1.shape
    tm = OUT_ROWS
    n_chunks, tf = w_down.shape[0], w_down.shape[1]
    assert n_chunks % 2 == 1
    halo_blocks = tm // BF16_SUBLANES
    row = lambda b, i: (b, i, 0)
    resident = lambda a: pl.BlockSpec(a.shape, lambda b, i: (0,) * a.ndim, pipeline_mode=pl.Buffered(1))
    return pl.pallas_call(
        functools.partial(_mlp_kernel, final_norm=final_norm),
        grid=(bsz, s // tm),
        in_specs=[pl.BlockSpec((1, tm, d), row),
                  pl.BlockSpec((1, BF16_SUBLANES, d), lambda b, i: (b, jnp.maximum(i * halo_blocks - 1, 0), 0)),
                  pl.BlockSpec((1, tm, d), row),
                  pl.BlockSpec((1, N_MOD, d), lambda b, i: (b, 0, 0)),
                  resident(w_up), resident(conv_w), resident(conv_b), resident(w_down),
                  pl.BlockSpec((1, d), lambda b, i: (0, 0))],
        out_specs=pl.BlockSpec((1, tm, d), row),
        out_shape=jax.ShapeDtypeStruct((bsz, s, d), F32),
        scratch_shapes=[pltpu.VMEM((2, 2, tm + BF16_SUBLANES, tf), F32), pltpu.VMEM((tm, d), F32)],
        compiler_params=_params("arbitrary", "arbitrary"),
        name="mlp",
    )(h2, h2, x1, mod, w_up, conv_w, conv_b, w_down, g_final)


def _chunk_columns(a, d_ff, n_chunks, tf):
    halves = jnp.stack([a[:, :d_ff], a[:, d_ff:]])
    halves = jnp.pad(halves, ((0, 0), (0, 0), (0, n_chunks * tf - d_ff)))
    return halves.reshape(2, a.shape[0], n_chunks, tf).transpose(0, 2, 1, 3)


def _pad_cols(a, n):
    return jnp.pad(a, ((0, 0), (0, n - a.shape[1])))


def kernel(x, c, w_ada, b_ada, g_attn, w_in, b_fgate, g_out_fox, g_out_sb, w_out,
           g_mlp, w_up, conv_w, conv_b, w_down, g_final):
    depth, d, _ = w_ada.shape
    n_fox = b_fgate.shape[1]
    d_fox = n_fox * HEAD_DIM
    d_sb = g_out_sb.shape[1]
    n_sb = d_sb // HEAD_DIM
    d_ff = w_down.shape[1]
    d_ff_pad = -(-d_ff // FF_CHUNK) * FF_CHUNK
    assert n_fox % 2 == 0 and n_sb == n_fox and 3 * n_fox <= LANES
    assert x.shape[1] % OUT_ROWS == 0 and x.shape[1] % ATT_Q == 0 and ATT_Q == 2 * ATT_K
    o_kf, o_vf, o_qs, o_ks, o_vs, o_gate = (d_fox, 2 * d_fox, 3 * d_fox, 3 * d_fox + d_sb,
                                             3 * d_fox + 2 * d_sb, 3 * d_fox + 3 * d_sb)

    for l in range(depth):
        mod = _ada(c, w_ada[l], b_ada[l]).reshape(-1, N_MOD, d)
        w = w_in[l]
        w_nat = jnp.concatenate([w[:, :o_vf], w[:, o_qs:o_vs]], axis=1).astype(BF16)
        w_vt = jnp.concatenate([w[:, o_vf:o_qs], w[:, o_vs:o_gate]], axis=1).T.astype(BF16)
        w_gate = _pad_cols(w[:, o_gate:], LANES).astype(BF16)
        b_gate = _pad_cols(b_fgate[l].reshape(1, n_fox), LANES)
        qk, vt, log_f = _inproj(x, mod, g_attn[l].reshape(1, d), w_nat, w_vt, w_gate, b_gate)

        k_aug, stats = _decay(log_f, qk, n_fox, k_fox_block=1, k_sb_block=3)
        steps_f, steps_s = n_fox // 2 // FOX_PAIRS, n_sb // 2 // SB_PAIRS
        fox_k_spec = pl.BlockSpec((1, 2 * FOX_PAIRS, x.shape[1], LANES), lambda b, p, i: (b, p, 0, 0))
        mix_f = _attention(_fox_kernel, "fox", FOX_PAIRS, _fox_scratch(FOX_PAIRS), qk, k_aug, fox_k_spec, vt,
                           g_out_fox[l], stats, q_block0=0, vt_block0=0, n_heads=n_fox)
        sb_k_spec = pl.BlockSpec((1, x.shape[1], SB_PAIRS * LANES), lambda b, p, i: (b, 0, 3 * steps_s + p))
        mix_s = _attention(_sb_kernel, "sb", SB_PAIRS, _sb_scratch(SB_PAIRS), qk, qk, sb_k_spec, vt,
                           g_out_sb[l], stats, q_block0=2 * steps_s, vt_block0=steps_s, n_heads=n_sb)

        x1, h2 = _outproj(x, mix_f, mix_s, w_out[l].astype(BF16), mod, g_mlp[l].reshape(1, d))

        n_ff = d_ff_pad // FF_CHUNK
        x = _mlp(h2, x1, mod,
                 _chunk_columns(w_up[l], d_ff, n_ff, FF_CHUNK).astype(BF16),
                 _chunk_columns(conv_w[l], d_ff, n_ff, FF_CHUNK),
                 _chunk_columns(conv_b[l].reshape(1, -1), d_ff, n_ff, FF_CHUNK),
                 jnp.pad(w_down[l], ((0, d_ff_pad - d_ff), (0, 0))).astype(BF16).reshape(n_ff, FF_CHUNK, d),
                 g_final.reshape(1, d), final_norm=(l == depth - 1))
    return x
```

```python
import functools

import numpy as np
import jax
import jax.numpy as jnp
from jax import lax
from jax.experimental import pallas as pl
from jax.experimental.pallas import tpu as pltpu

HEAD_DIM = 64
N_MOD = 6
CONV_WIDTH = 3
EPS = 1e-6

LANES = 128
BF16_SUBLANES = 16
VMEM_LIMIT_BYTES = 48 * 1024 * 1024

ATT_Q = 512
ATT_K = 256
ATT_COLS = 256
FOX_PAIRS = 1
SB_PAIRS = 1
SB_WIDE_TILES = 4
PROJ_ROWS = ATT_K
LOG2E = 1.4426950408889634
MASKED = -1e30
M_INIT = -1e29
EXP2_MAX = 126.0
PRUNE_LOG2 = 160.0
NORM_SLACK = 1.02
STATS_ROWS = 8
OUT_ROWS = 512
FF_CHUNK = 256

F32 = jnp.float32
BF16 = jnp.bfloat16
NT_DIMS = (((1,), (1,)), ((), ()))


def _dot(a, b):
    return jnp.dot(a, b, preferred_element_type=F32)


def _dot_nt(a, b):
    return lax.dot_general(a, b, NT_DIMS, preferred_element_type=F32)


def _params(*sem):
    return pltpu.CompilerParams(dimension_semantics=sem, vmem_limit_bytes=VMEM_LIMIT_BYTES)


def _rms_rows(x):
    return x * lax.rsqrt(jnp.mean(x * x, axis=-1, keepdims=True) + EPS)


def _softplus(z):
    return jnp.maximum(z, 0.0) + jnp.log(1.0 + jnp.exp(-jnp.abs(z)))


def _split3(x):
    hi = x.astype(BF16)
    r1 = x - hi.astype(F32)
    mid = r1.astype(BF16)
    lo = (r1 - mid.astype(F32)).astype(BF16)
    return hi, mid, lo


def _ada_kernel(c_ref, w_ref, b_ref, o_ref):
    c = c_ref[...]
    o_ref[...] = _dot(c * jax.nn.sigmoid(c), w_ref[...]) + b_ref[...]


def _ada(c, w, b):
    bsz, d = c.shape
    n = w.shape[1]
    return pl.pallas_call(
        _ada_kernel,
        grid=(n // d,),
        in_specs=[pl.BlockSpec((bsz, d), lambda j: (0, 0)),
                  pl.BlockSpec((d, d), lambda j: (0, j)),
                  pl.BlockSpec((1, d), lambda j: (0, j))],
        out_specs=pl.BlockSpec((bsz, d), lambda j: (0, j)),
        out_shape=jax.ShapeDtypeStruct((bsz, n), F32),
        compiler_params=_params("arbitrary"),
        name="ada",
    )(c, w, b.reshape(1, n))


def _inproj_kernel(x_ref, mod_ref, g_ref, wn_ref, wvt_ref, wg_ref, bg_ref, qk_ref, vt_ref, lf_ref):
    shift = mod_ref[0, 0:1, :]
    scale = mod_ref[0, 1:2, :]
    h = (_rms_rows(x_ref[0]) * g_ref[...] * (1.0 + scale) + shift).astype(BF16)
    qk_ref[0] = _dot(h, wn_ref[...]).astype(BF16)
    vt_ref[0, 0] = _dot_nt(wvt_ref[...], h).astype(BF16)
    logit = _dot(h, wg_ref[...]) + bg_ref[...]
    lf_ref[0] = -_softplus(-logit)


def _inproj(x, mod, g, w_nat, w_vt, w_gate, b_gate):
    bsz, s, d = x.shape
    tm = PROJ_ROWS
    n_nat, n_v = w_nat.shape[1], w_vt.shape[0]
    const = lambda b, i: (0, 0)
    return pl.pallas_call(
        _inproj_kernel,
        grid=(bsz, s // tm),
        in_specs=[pl.BlockSpec((1, tm, d), lambda b, i: (b, i, 0)),
                  pl.BlockSpec((1, N_MOD, d), lambda b, i: (b, 0, 0)),
                  pl.BlockSpec((1, d), const),
                  pl.BlockSpec((d, n_nat), const),
                  pl.BlockSpec((n_v, d), const),
                  pl.BlockSpec((d, LANES), const),
                  pl.BlockSpec((1, LANES), const)],
        out_specs=[pl.BlockSpec((1, tm, n_nat), lambda b, i: (b, i, 0)),
                   pl.BlockSpec((1, 1, n_v, tm), lambda b, i: (b, i, 0, 0)),
                   pl.BlockSpec((1, tm, LANES), lambda b, i: (b, i, 0))],
        out_shape=[jax.ShapeDtypeStruct((bsz, s, n_nat), BF16),
                   jax.ShapeDtypeStruct((bsz, s // tm, n_v, tm), BF16),
                   jax.ShapeDtypeStruct((bsz, s, LANES), F32)],
        compiler_params=_params("arbitrary", "arbitrary"),
        name="inproj",
    )(x, mod, g, w_nat, w_vt, w_gate, b_gate)


def _decay_kernel(lf_ref, k_ref, ks_ref, sel_ref, ind_ref, kaug_ref, stats_ref, carry_ref, kpre_ref, *, n_heads):
    @pl.when(pl.program_id(1) == 0)
    def _():
        carry_ref[...] = jnp.zeros_like(carry_ref)
        kpre_ref[...] = jnp.zeros_like(kpre_ref)

    tm = lf_ref.shape[1]
    lane = lax.broadcasted_iota(jnp.int32, (tm, LANES), 1)
    lf = jnp.where(lane < n_heads, lf_ref[0], 0.0)
    row = lax.broadcasted_iota(jnp.int32, (tm, tm), 0)
    col = lax.broadcasted_iota(jnp.int32, (tm, tm), 1)
    tri = (col <= row).astype(BF16)
    hi, mid, lo = _split3(lf)
    f_run = carry_ref[...] + (_dot(tri, hi) + _dot(tri, mid) + _dot(tri, lo))
    carry_ref[...] = f_run[tm - 1:tm, :]
    ghi, gmid, glo = _split3(-LOG2E * f_run)
    packed = (ghi.astype(F32) + pltpu.roll(gmid.astype(F32), n_heads, 1)
              + pltpu.roll(glo.astype(F32), 2 * n_heads, 1)).astype(BF16)
    placed = _dot(packed, sel_ref[...])
    k_all = k_ref[0]
    for h in range(n_heads):
        k_pair = k_all[:, (h // 2) * LANES:(h // 2 + 1) * LANES]
        own = (lane < HEAD_DIM) if h % 2 == 0 else (lane >= HEAD_DIM)
        kaug_ref[0, h] = jnp.where(own, k_pair, placed[:, h * LANES:(h + 1) * LANES].astype(BF16))

    def head_norm_bound(k):
        k32 = k.astype(F32)
        sq = _dot((k32 * k32).astype(BF16), ind_ref[...])
        return jnp.sqrt(jnp.max(sq, axis=0, keepdims=True) * NORM_SLACK)

    kpre_f = jnp.maximum(kpre_ref[0:1, :], head_norm_bound(k_all))
    kpre_s = jnp.maximum(kpre_ref[1:2, :], head_norm_bound(ks_ref[0]))
    kpre_ref[0:1, :] = kpre_f
    kpre_ref[1:2, :] = kpre_s
    g_end = -LOG2E * f_run[tm - 1:tm, :]
    lane1 = lax.broadcasted_iota(jnp.int32, (1, LANES), 1)

    def spread(v, h):
        return jnp.broadcast_to(jnp.sum(jnp.where(lane1 == h, v, 0.0), axis=1, keepdims=True), (1, ATT_COLS))

    for p in range(n_heads // 2):
        rows = [spread(v, 2 * p + hh) for v in (kpre_f, g_end, kpre_s) for hh in range(2)]
        rows += [jnp.zeros((1, ATT_COLS), F32)] * (stats_ref.shape[3] - len(rows))
        stats_ref[0, p, 0] = jnp.concatenate(rows, axis=0)


def _head_indicator(n_heads):
    ind = np.zeros((n_heads * HEAD_DIM, LANES), np.float32)
    ind[np.arange(n_heads * HEAD_DIM), np.arange(n_heads * HEAD_DIM) // HEAD_DIM] = 1.0
    return jnp.asarray(ind, BF16)


def _decay_select_matrix(n_heads):
    sel = np.zeros((LANES, n_heads * LANES), np.float32)
    for h in range(n_heads):
        base = h * LANES + (HEAD_DIM if h % 2 == 0 else 0)
        for term in range(3):
            sel[term * n_heads + h, base + term] = 1.0
    return jnp.asarray(sel, BF16)


def _decay(log_f, qk, n_heads, k_fox_block, k_sb_block):
    bsz, s, _ = log_f.shape
    tm = PROJ_ROWS
    d_grp = n_heads * HEAD_DIM
    return pl.pallas_call(
        functools.partial(_decay_kernel, n_heads=n_heads),
        grid=(bsz, s // tm),
        in_specs=[pl.BlockSpec((1, tm, LANES), lambda b, i: (b, i, 0)),
                  pl.BlockSpec((1, tm, d_grp), lambda b, i: (b, i, k_fox_block)),
                  pl.BlockSpec((1, tm, d_grp), lambda b, i: (b, i, k_sb_block)),
                  pl.BlockSpec((LANES, n_heads * LANES), lambda b, i: (0, 0)),
                  pl.BlockSpec((d_grp, LANES), lambda b, i: (0, 0))],
        out_specs=[pl.BlockSpec((1, n_heads, tm, LANES), lambda b, i: (b, 0, i, 0)),
                   pl.BlockSpec((1, n_heads // 2, 1, STATS_ROWS, ATT_COLS), lambda b, i: (b, 0, i, 0, 0))],
        out_shape=[jax.ShapeDtypeStruct((bsz, n_heads, s, LANES), BF16),
                   jax.ShapeDtypeStruct((bsz, n_heads // 2, s // tm, STATS_ROWS, ATT_COLS), F32)],
        scratch_shapes=[pltpu.VMEM((1, LANES), F32), pltpu.VMEM((2, LANES), F32)],
        compiler_params=_params("arbitrary", "arbitrary"),
        name="decay",
    )(log_f, qk, qk, _decay_select_matrix(n_heads), _head_indicator(n_heads))


def _lane_queries(q_ref, extra_even, extra_odd, cw):
    out = []
    for pp in range(q_ref.shape[2] // LANES):
        q = q_ref[0, :, pp * LANES:(pp + 1) * LANES].astype(F32) * (HEAD_DIM ** -0.5 * LOG2E)
        lane = lax.broadcasted_iota(jnp.int32, q.shape, 1)
        heads = (jnp.where(lane < HEAD_DIM, q, extra_even(lane)).T.astype(BF16),
                 jnp.where(lane >= HEAD_DIM, q, extra_odd(lane)).T.astype(BF16))
        out += [heads[hh][:, c * cw:(c + 1) * cw] for hh in range(2) for c in range(q.shape[0] // cw)]
    return out


def _visibility(first_key, first_query, bk, cw, strict):
    last_visible_gap = -1 if strict else 0
    if first_key + bk - 1 - first_query <= last_visible_gap:
        return "all"
    if first_key - (first_query + cw - 1) > last_visible_gap:
        return "none"
    gap = (lax.broadcasted_iota(jnp.int32, (bk, cw), 0) - lax.broadcasted_iota(jnp.int32, (bk, cw), 1)
           + (first_key - first_query))
    return gap <= last_visible_gap


def _diag_visibility(u, c, bk, cw, strict):
    return _visibility((1 - u) * bk, c * cw, bk, cw, strict)


def _hidden(visibility):
    return isinstance(visibility, str) and visibility == "none"


def _query_norm_bounds(queries, n_chunks):
    bounds = []
    for li, q in enumerate(queries):
        hh = (li // n_chunks) % 2
        own = q[hh * HEAD_DIM:(hh + 1) * HEAD_DIM, :].astype(F32)
        bounds.append(jnp.sqrt(jnp.sum(own * own, axis=0, keepdims=True) * NORM_SLACK))
    return bounds


def _finish_heads(lanes, g_ref, o_ref):
    n_pairs = o_ref.shape[2] // LANES
    n_chunks = len(lanes) // (2 * n_pairs)
    for pp in range(n_pairs):
        mine = lanes[2 * pp * n_chunks:2 * (pp + 1) * n_chunks]
        outs = [jnp.concatenate(mine[hh * n_chunks:(hh + 1) * n_chunks], axis=1) for hh in range(2)]
        normed = [o * lax.rsqrt(jnp.mean(o * o, axis=0, keepdims=True) + EPS) for o in outs]
        cols = slice(pp * LANES, (pp + 1) * LANES)
        o_ref[0, :, cols] = (jnp.concatenate(normed, axis=0).T * g_ref[:, cols]).astype(o_ref.dtype)


def _fox_kernel(q_ref, k_ref, vt_ref, g_ref, stats_ref, o_ref, s_buf, cmax_buf, p_buf, acc_buf):
    qi = pl.program_id(2)
    bk = vt_ref.shape[3]
    n_lanes, cw = acc_buf.shape[0], acc_buf.shape[2]
    n_chunks = q_ref.shape[1] // cw
    lane_group = lambda li: (li // (2 * n_chunks), (li // n_chunks) % 2, li % n_chunks)
    n_tiles = 2 * (qi + 1)
    ones3 = lambda lo: (lambda lane: jnp.where((lane >= lo) & (lane < lo + 3), 1.0, 0.0))
    queries = _lane_queries(q_ref, ones3(HEAD_DIM), ones3(0), cw)
    acc_buf[...] = jnp.zeros(acc_buf.shape, F32)
    for li in range(n_lanes):
        if _hidden(_diag_visibility(0, lane_group(li)[2], bk, cw, strict=False)):
            p_buf[0, li] = jnp.zeros((bk, cw), BF16)

    def step(t, slot, carry, score="below", softmax="below", value=True):
        new = []
        for li in range(n_lanes):
            pp, hh, c = lane_group(li)
            see = lambda u: "all" if u == "below" else _diag_visibility(u, c, bk, cw, strict=False)
            if score is not None and not _hidden(see(score)):
                start = pl.multiple_of((n_tiles - 2 - t) * bk, bk)
                s_new = _dot(k_ref[0, 2 * pp + hh, pl.ds(start, bk), :], queries[li])
                if not isinstance(see(score), str):
                    s_new = jnp.where(see(score), s_new, MASKED)
                s_buf[1 - slot, li] = s_new
                cmax_buf[1 - slot, li] = jnp.max(s_new, axis=0, keepdims=True)
            pv = None
            if value:
                vt = vt_ref[0, n_tiles - t, pl.ds(pp * LANES + hh * HEAD_DIM, HEAD_DIM), :]
                pv = _dot(vt, p_buf[1 - slot, li])
            m, l = carry[li]
            if softmax is not None and not _hidden(see(softmax)):
                m_new = jnp.maximum(m, cmax_buf[slot, li])
                alpha = jnp.exp2(m - m_new)
                p = jnp.exp2(s_buf[slot, li] - m_new)
                p_buf[slot, li] = p.astype(BF16)
                m, l = m_new, alpha * l + jnp.sum(p, axis=0, keepdims=True)
                acc_buf[li] = alpha * (acc_buf[li] if pv is None else acc_buf[li] + pv)
            elif pv is not None:
                acc_buf[li] += pv
            new.append((m, l))
        return tuple(new)

    def step_pair(i, carry):
        t = 2 * i + 1
        return step(t + 1, 0, step(t, 1, carry))

    q_norm = _query_norm_bounds(queries, n_chunks)

    def later_tiles_matter(i, carry):
        j_rest = jnp.maximum(n_tiles - 5 - 2 * i, 0)
        worst = None
        for li in range(n_lanes):
            pp, hh, _ = lane_group(li)
            bound = (q_norm[li] * stats_ref[0, pp, j_rest, hh:hh + 1, :]
                     + stats_ref[0, pp, j_rest, 2 + hh:3 + hh, :] - carry[li][0])
            worst = bound if worst is None else jnp.maximum(worst, bound)
        return jnp.max(worst) >= -PRUNE_LOG2

    def pair_and_check(state):
        i, _, carry = state
        carry = step_pair(i, carry)
        return i + 1, later_tiles_matter(i, carry), carry

    carry = tuple((jnp.full((1, cw), M_INIT, F32), jnp.zeros((1, cw), F32)) for _ in range(n_lanes))
    carry = step(-1, 1, carry, score=0, softmax=None, value=False)
    carry = step(0, 0, carry, score=1, softmax=0, value=False)
    n_pairs, _, carry = lax.while_loop(lambda st: (st[0] < qi) & st[1], pair_and_check,
                                       (jnp.int32(0), jnp.bool_(True), carry))
    carry = step(2 * n_pairs + 1, 1, carry, score=None)
    carry = step(2 * n_pairs + 2, 0, carry, score=None, softmax=None)
    _finish_heads([acc_buf[li] / carry[li][1] for li in range(n_lanes)], g_ref, o_ref)


def _sb_kernel(q_ref, k_ref, vt_ref, g_ref, stats_ref, o_ref, z_buf, sp_buf, e_buf, wrow_buf, acc_buf):
    qi = pl.program_id(2)
    bk = vt_ref.shape[3]
    n_wide = z_buf.shape[0]
    n_lanes, cw = acc_buf.shape[0], acc_buf.shape[2]
    n_chunks = q_ref.shape[1] // cw
    lane_group = lambda li: (li // (2 * n_chunks), (li // n_chunks) % 2, li % n_chunks)
    n_tiles = 2 * (qi + 1)
    zero = lambda lane: 0.0
    queries = _lane_queries(q_ref, zero, zero, cw)
    q_norm = _query_norm_bounds(queries, n_chunks)
    suffix = (lax.broadcasted_iota(jnp.int32, (bk, bk), 1)
              >= lax.broadcasted_iota(jnp.int32, (bk, bk), 0)).astype(BF16)
    softplus2 = lambda z: jnp.maximum(z, jnp.log2(1.0 + jnp.exp2(jnp.minimum(z, EXP2_MAX))))
    keys = lambda j, pp: k_ref[0, pl.ds(pl.multiple_of(j * bk, bk), bk), pp * LANES:(pp + 1) * LANES]
    values = lambda j, pp, hh: vt_ref[0, j, pl.ds(pp * LANES + hh * HEAD_DIM, HEAD_DIM), :]

    sees = lambda u, li: _diag_visibility(u, lane_group(li)[2], bk, cw, strict=True) if u < 2 else "all"
    live = [(u, li) for u in range(n_wide) for li in range(n_lanes) if not _hidden(sees(u, li))]
    for u, li in live:
        z = _dot(keys(jnp.maximum(n_tiles - 1 - u, 0), lane_group(li)[0]), queries[li])
        if not isinstance(sees(u, li), str):
            z = jnp.where(sees(u, li), z, MASKED)
        if u >= 2:
            z = jnp.where(u < n_tiles, z, MASKED)
        z_buf[u, li] = z
    for u, li in live:
        sp_buf[u, li] = softplus2(z_buf[u, li]).astype(BF16)
    col_sums = {}
    for u, li in live:
        within = _dot(suffix, sp_buf[u, li])
        col_sums[u, li] = within[0:1, :]
        z_buf[u, li] = z_buf[u, li] - within
    later = []
    for li in range(n_lanes):
        pp, hh, _ = lane_group(li)
        mass, acc = jnp.zeros((1, cw), F32), jnp.zeros((HEAD_DIM, cw), F32)
        for u in range(n_wide):
            if (u, li) in col_sums:
                a = jnp.exp2(z_buf[u, li] - mass)
                acc = acc + _dot(values(jnp.maximum(n_tiles - 1 - u, 0), pp, hh), a.astype(BF16))
                mass = mass + col_sums[u, li]
        acc_buf[li] = acc
        later.append(mass)

    n_rest = n_tiles - n_wide

    def rest_matters(first_unscored, mass):
        j_rest = jnp.maximum(n_rest - 1 - first_unscored, 0)
        worst = None
        for li in range(n_lanes):
            pp, hh, _ = lane_group(li)
            bound = q_norm[li] * stats_ref[0, pp, j_rest, 4 + hh:5 + hh, :] - mass[li]
            worst = bound if worst is None else jnp.maximum(worst, bound)
        return jnp.max(worst) >= -PRUNE_LOG2

    def step(t, slot, later, score=True, softplus=True, cumsum=True, weight=True):
        new_later = []
        for li in range(n_lanes):
            pp, hh, _ = lane_group(li)
            if cumsum:
                within = _dot(suffix, sp_buf[1 - slot, li])
                e_buf[1 - slot, li] = z_buf[1 - slot, li] - within
                wrow_buf[1 - slot, li] = within[0:1, :]
            if score:
                z_buf[1 - slot, li] = _dot(keys(n_rest - 1 - (t + 3), pp), queries[li])
            if weight:
                a = jnp.exp2(e_buf[slot, li] - later[li])
                acc_buf[li] += _dot(values(n_rest - 1 - jnp.maximum(t, 0), pp, hh), a.astype(BF16))
                new_later.append(later[li] + wrow_buf[slot, li])
            else:
                new_later.append(later[li])
            if softplus:
                sp_buf[slot, li] = softplus2(z_buf[slot, li]).astype(BF16)
        return tuple(new_later)

    def pair_and_check(state):
        i, _, later = state
        t = 2 * i - 1
        later = step(t + 1, 0, step(t, 1, later))
        mass = [later[li] + wrow_buf[1, li] for li in range(n_lanes)]
        return i + 1, rest_matters(2 * i + 4, mass), later

    @pl.when((n_rest > 0) & rest_matters(0, later))
    def _():
        e_buf[1] = jnp.full(e_buf.shape[1:], MASKED, F32)
        wrow_buf[1] = jnp.zeros(wrow_buf.shape[1:], F32)
        mass = step(-3, 1, tuple(later), softplus=False, cumsum=False, weight=False)
        mass = step(-2, 0, mass, cumsum=False, weight=False)
        n_pairs, _, mass = lax.while_loop(lambda st: (2 * st[0] + 2 < n_rest) & st[1], pair_and_check,
                                          (jnp.int32(0), jnp.bool_(True), mass))
        mass = step(2 * n_pairs - 1, 1, mass, score=False)
        mass = step(2 * n_pairs, 0, mass, score=False, softplus=False)
        step(2 * n_pairs + 1, 1, mass, score=False, softplus=False, cumsum=False)

    _finish_heads([acc_buf[li] for li in range(n_lanes)], g_ref, o_ref)


def _attention(body, name, pairs, scratch, qk, k_arr, k_spec, vt, g, stats, q_block0, vt_block0, n_heads):
    bsz, s, _ = qk.shape
    bq, bk, width = ATT_Q, ATT_K, pairs * LANES
    d_grp = n_heads * HEAD_DIM
    assert (n_heads // 2) % pairs == 0
    return pl.pallas_call(
        body,
        grid=(bsz, n_heads // 2 // pairs, s // bq),
        in_specs=[pl.BlockSpec((1, bq, width), lambda b, p, i: (b, i, q_block0 + p)),
                  k_spec,
                  pl.BlockSpec((1, s // bk, width, bk), lambda b, p, i: (b, 0, vt_block0 + p, 0)),
                  pl.BlockSpec((1, width), lambda b, p, i: (0, p)),
                  pl.BlockSpec((1, pairs) + stats.shape[2:], lambda b, p, i: (b, p, 0, 0, 0))],
        out_specs=pl.BlockSpec((1, bq, width), lambda b, p, i: (b, i, p)),
        out_shape=jax.ShapeDtypeStruct((bsz, s, d_grp), BF16),
        scratch_shapes=scratch,
        compiler_params=_params("arbitrary", "arbitrary", "arbitrary"),
        name=name,
    )(qk, k_arr, vt, g.reshape(1, d_grp), stats)


def _lane_groups(pairs):
    return pairs * 2 * (ATT_Q // ATT_COLS)


def _fox_scratch(pairs):
    n = _lane_groups(pairs)
    return [pltpu.VMEM((2, n, ATT_K, ATT_COLS), F32), pltpu.VMEM((2, n, 1, ATT_COLS), F32),
            pltpu.VMEM((2, n, ATT_K, ATT_COLS), BF16), pltpu.VMEM((n, HEAD_DIM, ATT_COLS), F32)]


def _sb_scratch(pairs):
    n = _lane_groups(pairs)
    return [pltpu.VMEM((SB_WIDE_TILES, n, ATT_K, ATT_COLS), F32), pltpu.VMEM((SB_WIDE_TILES, n, ATT_K, ATT_COLS), BF16),
            pltpu.VMEM((2, n, ATT_K, ATT_COLS), F32), pltpu.VMEM((2, n, 1, ATT_COLS), F32),
            pltpu.VMEM((n, HEAD_DIM, ATT_COLS), F32)]


def _outproj_kernel(x_ref, mf_ref, ms_ref, w_ref, mod_ref, g_ref, x1_ref, h2_ref):
    mix = jnp.concatenate([mf_ref[0], ms_ref[0]], axis=-1)
    x1 = x_ref[0] + mod_ref[0, 2:3, :] * _dot(mix, w_ref[...])
    x1_ref[0] = x1
    shift = mod_ref[0, 3:4, :]
    scale = mod_ref[0, 4:5, :]
    h2_ref[0] = (_rms_rows(x1) * g_ref[...] * (1.0 + scale) + shift).astype(BF16)


def _outproj(x, mix_f, mix_s, w_out, mod, g):
    bsz, s, d = x.shape
    tm = OUT_ROWS
    row = lambda b, i: (b, i, 0)
    return pl.pallas_call(
        _outproj_kernel,
        grid=(bsz, s // tm),
        in_specs=[pl.BlockSpec((1, tm, d), row),
                  pl.BlockSpec((1, tm, mix_f.shape[2]), row),
                  pl.BlockSpec((1, tm, mix_s.shape[2]), row),
                  pl.BlockSpec(w_out.shape, lambda b, i: (0, 0)),
                  pl.BlockSpec((1, N_MOD, d), lambda b, i: (b, 0, 0)),
                  pl.BlockSpec((1, d), lambda b, i: (0, 0))],
        out_specs=[pl.BlockSpec((1, tm, d), row), pl.BlockSpec((1, tm, d), row)],
        out_shape=[jax.ShapeDtypeStruct((bsz, s, d), F32), jax.ShapeDtypeStruct((bsz, s, d), BF16)],
        compiler_params=_params("arbitrary", "arbitrary"),
        name="outproj",
    )(x, mix_f, mix_s, w_out, mod, g)


def _mlp_kernel(h_ref, halo_ref, x1_ref, mod_ref, wu_ref, cw_ref, cb_ref, wd_ref, gf_ref, o_ref,
                u_buf, acc_ref, *, final_norm):
    i = pl.program_id(1)
    tm = h_ref.shape[1]
    n_chunks = wd_ref.shape[0]
    halo = halo_ref[0]
    halo = jnp.where(i > 0, halo, jnp.zeros_like(halo))
    hx = jnp.concatenate([halo, h_ref[0]], axis=0)
    acc_ref[...] = jnp.zeros_like(acc_ref)

    def project_up(c, slot):
        for br in range(2):
            u_buf[slot, br] = _dot(hx, wu_ref[br, c])

    def mix_down(c, slot):
        branches = []
        for br in range(2):
            out = cb_ref[br, c]
            for tap in range(CONV_WIDTH):
                first = BF16_SUBLANES - (CONV_WIDTH - 1 - tap)
                out = out + cw_ref[br, c, tap:tap + 1, :] * u_buf[slot, br, pl.ds(first, tm), :]
            branches.append(out)
        u_gate, u_val = branches
        acc_ref[...] += _dot((u_gate * jax.nn.sigmoid(u_gate) * u_val).astype(BF16), wd_ref[c])

    def chunk_pair(j, _):
        c = 2 * j
        project_up(c + 1, 1)
        mix_down(c, 0)
        project_up(c + 2, 0)
        mix_down(c + 1, 1)
        return 0

    project_up(0, 0)
    lax.fori_loop(0, (n_chunks - 1) // 2, chunk_pair, 0)
    mix_down(n_chunks - 1, 0)
    x2 = x1_ref[0] + mod_ref[0, 5:6, :] * acc_ref[...]
    o_ref[0] = _rms_rows(x2) * gf_ref[...] if final_norm else x2


def _mlp(h2, x1, mod, w_up, conv_w, conv_b, w_down, g_final, final_norm):
    bsz, s, d = x1.shape
    tm = OUT_ROWS
    n_chunks, tf = w_down.shape[0], w_down.shape[1]
    assert n_chunks % 2 == 1
    halo_blocks = tm // BF16_SUBLANES
    row = lambda b, i: (b, i, 0)
    resident = lambda a: pl.BlockSpec(a.shape, lambda b, i: (0,) * a.ndim, pipeline_mode=pl.Buffered(1))
    return pl.pallas_call(
        functools.partial(_mlp_kernel, final_norm=final_norm),
        grid=(bsz, s // tm),
        in_specs=[pl.BlockSpec((1, tm, d), row),
                  pl.BlockSpec((1, BF16_SUBLANES, d), lambda b, i: (b, jnp.maximum(i * halo_blocks - 1, 0), 0)),
                  pl.BlockSpec((1, tm, d), row),
                  pl.BlockSpec((1, N_MOD, d), lambda b, i: (b, 0, 0)),
                  resident(w_up), resident(conv_w), resident(conv_b), resident(w_down),
                  pl.BlockSpec((1, d), lambda b, i: (0, 0))],
        out_specs=pl.BlockSpec((1, tm, d), row),
        out_shape=jax.ShapeDtypeStruct((bsz, s, d), F32),
        scratch_shapes=[pltpu.VMEM((2, 2, tm + BF16_SUBLANES, tf), F32), pltpu.VMEM((tm, d), F32)],
        compiler_params=_params("arbitrary", "arbitrary"),
        name="mlp",
    )(h2, h2, x1, mod, w_up, conv_w, conv_b, w_down, g_final)


def _chunk_columns(a, d_ff, n_chunks, tf):
    halves = jnp.stack([a[:, :d_ff], a[:, d_ff:]])
    halves = jnp.pad(halves, ((0, 0), (0, 0), (0, n_chunks * tf - d_ff)))
    return halves.reshape(2, a.shape[0], n_chunks, tf).transpose(0, 2, 1, 3)


def _pad_cols(a, n):
    return jnp.pad(a, ((0, 0), (0, n - a.shape[1])))


def kernel(x, c, w_ada, b_ada, g_attn, w_in, b_fgate, g_out_fox, g_out_sb, w_out,
           g_mlp, w_up, conv_w, conv_b, w_down, g_final):
    depth, d, _ = w_ada.shape
    n_fox = b_fgate.shape[1]
    d_fox = n_fox * HEAD_DIM
    d_sb = g_out_sb.shape[1]
    n_sb = d_sb // HEAD_DIM
    d_ff = w_down.shape[1]
    d_ff_pad = -(-d_ff // FF_CHUNK) * FF_CHUNK
    assert n_fox % 2 == 0 and n_sb == n_fox and 3 * n_fox <= LANES
    assert x.shape[1] % OUT_ROWS == 0 and x.shape[1] % ATT_Q == 0 and ATT_Q == 2 * ATT_K
    o_kf, o_vf, o_qs, o_ks, o_vs, o_gate = (d_fox, 2 * d_fox, 3 * d_fox, 3 * d_fox + d_sb,
                                             3 * d_fox + 2 * d_sb, 3 * d_fox + 3 * d_sb)

    for l in range(depth):
        mod = _ada(c, w_ada[l], b_ada[l]).reshape(-1, N_MOD, d)
        w = w_in[l]
        w_nat = jnp.concatenate([w[:, :o_vf], w[:, o_qs:o_vs]], axis=1).astype(BF16)
        w_vt = jnp.concatenate([w[:, o_vf:o_qs], w[:, o_vs:o_gate]], axis=1).T.astype(BF16)
        w_gate = _pad_cols(w[:, o_gate:], LANES).astype(BF16)
        b_gate = _pad_cols(b_fgate[l].reshape(1, n_fox), LANES)
        qk, vt, log_f = _inproj(x, mod, g_attn[l].reshape(1, d), w_nat, w_vt, w_gate, b_gate)

        k_aug, stats = _decay(log_f, qk, n_fox, k_fox_block=1, k_sb_block=3)
        steps_f, steps_s = n_fox // 2 // FOX_PAIRS, n_sb // 2 // SB_PAIRS
        fox_k_spec = pl.BlockSpec((1, 2 * FOX_PAIRS, x.shape[1], LANES), lambda b, p, i: (b, p, 0, 0))
        mix_f = _attention(_fox_kernel, "fox", FOX_PAIRS, _fox_scratch(FOX_PAIRS), qk, k_aug, fox_k_spec, vt,
                           g_out_fox[l], stats, q_block0=0, vt_block0=0, n_heads=n_fox)
        sb_k_spec = pl.BlockSpec((1, x.shape[1], SB_PAIRS * LANES), lambda b, p, i: (b, 0, 3 * steps_s + p))
        mix_s = _attention(_sb_kernel, "sb", SB_PAIRS, _sb_scratch(SB_PAIRS), qk, qk, sb_k_spec, vt,
                           g_out_sb[l], stats, q_block0=2 * steps_s, vt_block0=steps_s, n_heads=n_sb)

        x1, h2 = _outproj(x, mix_f, mix_s, w_out[l].astype(BF16), mod, g_mlp[l].reshape(1, d))

        n_ff = d_ff_pad // FF_CHUNK
        x = _mlp(h2, x1, mod,
                 _chunk_columns(w_up[l], d_ff, n_ff, FF_CHUNK).astype(BF16),
                 _chunk_columns(conv_w[l], d_ff, n_ff, FF_CHUNK),
                 _chunk_columns(conv_b[l].reshape(1, -1), d_ff, n_ff, FF_CHUNK),
                 jnp.pad(w_down[l], ((0, d_ff_pad - d_ff), (0, 0))).astype(BF16).reshape(n_ff, FF_CHUNK, d),
                 g_final.reshape(1, d), final_norm=(l == depth - 1))
    return x
```

```python
import functools

import numpy as np
import jax
import jax.numpy as jnp
from jax import lax
from jax.experimental import pallas as pl
from jax.experimental.pallas import tpu as pltpu

HEAD_DIM = 64
N_MOD = 6
CONV_WIDTH = 3
EPS = 1e-6

LANES = 128
BF16_SUBLANES = 16
VMEM_LIMIT_BYTES = 48 * 1024 * 1024

ATT_Q = 512
ATT_K = 256
ATT_COLS = 256
FOX_PAIRS = 1
SB_PAIRS = 1
SB_WIDE_TILES = 4
PROJ_ROWS = ATT_K
LOG2E = 1.4426950408889634
MASKED = -1e30
M_INIT = -1e29
EXP2_MAX = 126.0
PRUNE_LOG2 = 160.0
NORM_SLACK = 1.02
STATS_ROWS = 8
OUT_ROWS = 512
FF_CHUNK = 256

F32 = jnp.float32
BF16 = jnp.bfloat16
NT_DIMS = (((1,), (1,)), ((), ()))


def _dot(a, b):
    return jnp.dot(a, b, preferred_element_type=F32)


def _dot_nt(a, b):
    return lax.dot_general(a, b, NT_DIMS, preferred_element_type=F32)


def _params(*sem):
    return pltpu.CompilerParams(dimension_semantics=sem, vmem_limit_bytes=VMEM_LIMIT_BYTES)


def _rms_rows(x):
    return x * lax.rsqrt(jnp.mean(x * x, axis=-1, keepdims=True) + EPS)


def _softplus(z):
    return jnp.maximum(z, 0.0) + jnp.log(1.0 + jnp.exp(-jnp.abs(z)))


def _split3(x):
    hi = x.astype(BF16)
    r1 = x - hi.astype(F32)
    mid = r1.astype(BF16)
    lo = (r1 - mid.astype(F32)).astype(BF16)
    return hi, mid, lo


def _ada_kernel(c_ref, w_ref, b_ref, o_ref):
    c = c_ref[...]
    o_ref[...] = _dot(c * jax.nn.sigmoid(c), w_ref[...]) + b_ref[...]


def _ada(c, w, b):
    bsz, d = c.shape
    n = w.shape[1]
    return pl.pallas_call(
        _ada_kernel,
        grid=(n // d,),
        in_specs=[pl.BlockSpec((bsz, d), lambda j: (0, 0)),
                  pl.BlockSpec((d, d), lambda j: (0, j)),
                  pl.BlockSpec((1, d), lambda j: (0, j))],
        out_specs=pl.BlockSpec((bsz, d), lambda j: (0, j)),
        out_shape=jax.ShapeDtypeStruct((bsz, n), F32),
        compiler_params=_params("arbitrary"),
        name="ada",
    )(c, w, b.reshape(1, n))


def _inproj_kernel(x_ref, mod_ref, g_ref, wn_ref, wvt_ref, wg_ref, bg_ref, qk_ref, vt_ref, lf_ref):
    shift = mod_ref[0, 0:1, :]
    scale = mod_ref[0, 1:2, :]
    h = (_rms_rows(x_ref[0]) * g_ref[...] * (1.0 + scale) + shift).astype(BF16)
    qk_ref[0] = _dot(h, wn_ref[...]).astype(BF16)
    vt_ref[0, 0] = _dot_nt(wvt_ref[...], h).astype(BF16)
    logit = _dot(h, wg_ref[...]) + bg_ref[...]
    lf_ref[0] = -_softplus(-logit)


def _inproj(x, mod, g, w_nat, w_vt, w_gate, b_gate):
    bsz, s, d = x.shape
    tm = PROJ_ROWS
    n_nat, n_v = w_nat.shape[1], w_vt.shape[0]
    const = lambda b, i: (0, 0)
    return pl.pallas_call(
        _inproj_kernel,
        grid=(bsz, s // tm),
        in_specs=[pl.BlockSpec((1, tm, d), lambda b, i: (b, i, 0)),
                  pl.BlockSpec((1, N_MOD, d), lambda b, i: (b, 0, 0)),
                  pl.BlockSpec((1, d), const),
                  pl.BlockSpec((d, n_nat), const),
                  pl.BlockSpec((n_v, d), const),
                  pl.BlockSpec((d, LANES), const),
                  pl.BlockSpec((1, LANES), const)],
        out_specs=[pl.BlockSpec((1, tm, n_nat), lambda b, i: (b, i, 0)),
                   pl.BlockSpec((1, 1, n_v, tm), lambda b, i: (b, i, 0, 0)),
                   pl.BlockSpec((1, tm, LANES), lambda b, i: (b, i, 0))],
        out_shape=[jax.ShapeDtypeStruct((bsz, s, n_nat), BF16),
                   jax.ShapeDtypeStruct((bsz, s // tm, n_v, tm), BF16),
                   jax.ShapeDtypeStruct((bsz, s, LANES), F32)],
        compiler_params=_params("arbitrary", "arbitrary"),
        name="inproj",
    )(x, mod, g, w_nat, w_vt, w_gate, b_gate)


def _decay_kernel(lf_ref, k_ref, ks_ref, sel_ref, ind_ref, kaug_ref, stats_ref, carry_ref, kpre_ref, *, n_heads):
    @pl.when(pl.program_id(1) == 0)
    def _():
        carry_ref[...] = jnp.zeros_like(carry_ref)
        kpre_ref[...] = jnp.zeros_like(kpre_ref)

    tm = lf_ref.shape[1]
    lane = lax.broadcasted_iota(jnp.int32, (tm, LANES), 1)
    lf = jnp.where(lane < n_heads, lf_ref[0], 0.0)
    row = lax.broadcasted_iota(jnp.int32, (tm, tm), 0)
    col = lax.broadcasted_iota(jnp.int32, (tm, tm), 1)
    tri = (col <= row).astype(BF16)
    hi, mid, lo = _split3(lf)
    f_run = carry_ref[...] + (_dot(tri, hi) + _dot(tri, mid) + _dot(tri, lo))
    carry_ref[...] = f_run[tm - 1:tm, :]
    ghi, gmid, glo = _split3(-LOG2E * f_run)
    packed = (ghi.astype(F32) + pltpu.roll(gmid.astype(F32), n_heads, 1)
              + pltpu.roll(glo.astype(F32), 2 * n_heads, 1)).astype(BF16)
    placed = _dot(packed, sel_ref[...])
    k_all = k_ref[0]
    for h in range(n_heads):
        k_pair = k_all[:, (h // 2) * LANES:(h // 2 + 1) * LANES]
        own = (lane < HEAD_DIM) if h % 2 == 0 else (lane >= HEAD_DIM)
        kaug_ref[0, h] = jnp.where(own, k_pair, placed[:, h * LANES:(h + 1) * LANES].astype(BF16))

    def head_norm_bound(k):
        k32 = k.astype(F32)
        sq = _dot((k32 * k32).astype(BF16), ind_ref[...])
        return jnp.sqrt(jnp.max(sq, axis=0, keepdims=True) * NORM_SLACK)

    kpre_f = jnp.maximum(kpre_ref[0:1, :], head_norm_bound(k_all))
    kpre_s = jnp.maximum(kpre_ref[1:2, :], head_norm_bound(ks_ref[0]))
    kpre_ref[0:1, :] = kpre_f
    kpre_ref[1:2, :] = kpre_s
    g_end = -LOG2E * f_run[tm - 1:tm, :]
    lane1 = lax.broadcasted_iota(jnp.int32, (1, LANES), 1)

    def spread(v, h):
        return jnp.broadcast_to(jnp.sum(jnp.where(lane1 == h, v, 0.0), axis=1, keepdims=True), (1, ATT_COLS))

    for p in range(n_heads // 2):
        rows = [spread(v, 2 * p + hh) for v in (kpre_f, g_end, kpre_s) for hh in range(2)]
        rows += [jnp.zeros((1, ATT_COLS), F32)] * (stats_ref.shape[3] - len(rows))
        stats_ref[0, p, 0] = jnp.concatenate(rows, axis=0)


def _head_indicator(n_heads):
    ind = np.zeros((n_heads * HEAD_DIM, LANES), np.float32)
    ind[np.arange(n_heads * HEAD_DIM), np.arange(n_heads * HEAD_DIM) // HEAD_DIM] = 1.0
    return jnp.asarray(ind, BF16)


def _decay_select_matrix(n_heads):
    sel = np.zeros((LANES, n_heads * LANES), np.float32)
    for h in range(n_heads):
        base = h * LANES + (HEAD_DIM if h % 2 == 0 else 0)
        for term in range(3):
            sel[term * n_heads + h, base + term] = 1.0
    return jnp.asarray(sel, BF16)


def _decay(log_f, qk, n_heads, k_fox_block, k_sb_block):
    bsz, s, _ = log_f.shape
    tm = PROJ_ROWS
    d_grp = n_heads * HEAD_DIM
    return pl.pallas_call(
        functools.partial(_decay_kernel, n_heads=n_heads),
        grid=(bsz, s // tm),
        in_specs=[pl.BlockSpec((1, tm, LANES), lambda b, i: (b, i, 0)),
                  pl.BlockSpec((1, tm, d_grp), lambda b, i: (b, i, k_fox_block)),
                  pl.BlockSpec((1, tm, d_grp), lambda b, i: (b, i, k_sb_block)),
                  pl.BlockSpec((LANES, n_heads * LANES), lambda b, i: (0, 0)),
                  pl.BlockSpec((d_grp, LANES), lambda b, i: (0, 0))],
        out_specs=[pl.BlockSpec((1, n_heads, tm, LANES), lambda b, i: (b, 0, i, 0)),
                   pl.BlockSpec((1, n_heads // 2, 1, STATS_ROWS, ATT_COLS), lambda b, i: (b, 0, i, 0, 0))],
        out_shape=[jax.ShapeDtypeStruct((bsz, n_heads, s, LANES), BF16),
                   jax.ShapeDtypeStruct((bsz, n_heads // 2, s // tm, STATS_ROWS, ATT_COLS), F32)],
        scratch_shapes=[pltpu.VMEM((1, LANES), F32), pltpu.VMEM((2, LANES), F32)],
        compiler_params=_params("arbitrary", "arbitrary"),
        name="decay",
    )(log_f, qk, qk, _decay_select_matrix(n_heads), _head_indicator(n_heads))


def _lane_queries(q_ref, extra_even, extra_odd, cw):
    out = []
    for pp in range(q_ref.shape[2] // LANES):
        q = q_ref[0, :, pp * LANES:(pp + 1) * LANES].astype(F32) * (HEAD_DIM ** -0.5 * LOG2E)
        lane = lax.broadcasted_iota(jnp.int32, q.shape, 1)
        heads = (jnp.where(lane < HEAD_DIM, q, extra_even(lane)).T.astype(BF16),
                 jnp.where(lane >= HEAD_DIM, q, extra_odd(lane)).T.astype(BF16))
        out += [heads[hh][:, c * cw:(c + 1) * cw] for hh in range(2) for c in range(q.shape[0] // cw)]
    return out


def _visibility(first_key, first_query, bk, cw, strict):
    last_visible_gap = -1 if strict else 0
    if first_key + bk - 1 - first_query <= last_visible_gap:
        return "all"
    if first_key - (first_query + cw - 1) > last_visible_gap:
        return "none"
    gap = (lax.broadcasted_iota(jnp.int32, (bk, cw), 0) - lax.broadcasted_iota(jnp.int32, (bk, cw), 1)
           + (first_key - first_query))
    return gap <= last_visible_gap


def _diag_visibility(u, c, bk, cw, strict):
    return _visibility((1 - u) * bk, c * cw, bk, cw, strict)


def _hidden(visibility):
    return isinstance(visibility, str) and visibility == "none"


def _query_norm_bounds(queries, n_chunks):
    bounds = []
    for li, q in enumerate(queries):
        hh = (li // n_chunks) % 2
        own = q[hh * HEAD_DIM:(hh + 1) * HEAD_DIM, :].astype(F32)
        bounds.append(jnp.sqrt(jnp.sum(own * own, axis=0, keepdims=True) * NORM_SLACK))
    return bounds


def _finish_heads(lanes, g_ref, o_ref):
    n_pairs = o_ref.shape[2] // LANES
    n_chunks = len(lanes) // (2 * n_pairs)
    for pp in range(n_pairs):
        mine = lanes[2 * pp * n_chunks:2 * (pp + 1) * n_chunks]
        outs = [jnp.concatenate(mine[hh * n_chunks:(hh + 1) * n_chunks], axis=1) for hh in range(2)]
        normed = [o * lax.rsqrt(jnp.mean(o * o, axis=0, keepdims=True) + EPS) for o in outs]
        cols = slice(pp * LANES, (pp + 1) * LANES)
        o_ref[0, :, cols] = (jnp.concatenate(normed, axis=0).T * g_ref[:, cols]).astype(o_ref.dtype)


def _fox_kernel(q_ref, k_ref, vt_ref, g_ref, stats_ref, o_ref, s_buf, cmax_buf, p_buf, acc_buf):
    qi = pl.program_id(2)
    bk = vt_ref.shape[3]
    n_lanes, cw = acc_buf.shape[0], acc_buf.shape[2]
    n_chunks = q_ref.shape[1] // cw
    lane_group = lambda li: (li // (2 * n_chunks), (li // n_chunks) % 2, li % n_chunks)
    n_tiles = 2 * (qi + 1)
    ones3 = lambda lo: (lambda lane: jnp.where((lane >= lo) & (lane < lo + 3), 1.0, 0.0))
    queries = _lane_queries(q_ref, ones3(HEAD_DIM), ones3(0), cw)
    acc_buf[...] = jnp.zeros(acc_buf.shape, F32)
    for li in range(n_lanes):
        if _hidden(_diag_visibility(0, lane_group(li)[2], bk, cw, strict=False)):
            p_buf[0, li] = jnp.zeros((bk, cw), BF16)

    def step(t, slot, carry, score="below", softmax="below", value=True):
        new = []
        for li in range(n_lanes):
            pp, hh, c = lane_group(li)
            see = lambda u: "all" if u == "below" else _diag_visibility(u, c, bk, cw, strict=False)
            if score is not None and not _hidden(see(score)):
                start = pl.multiple_of((n_tiles - 2 - t) * bk, bk)
                s_new = _dot(k_ref[0, 2 * pp + hh, pl.ds(start, bk), :], queries[li])
                if not isinstance(see(score), str):
                    s_new = jnp.where(see(score), s_new, MASKED)
                s_buf[1 - slot, li] = s_new
                cmax_buf[1 - slot, li] = jnp.max(s_new, axis=0, keepdims=True)
            pv = None
            if value:
                vt = vt_ref[0, n_tiles - t, pl.ds(pp * LANES + hh * HEAD_DIM, HEAD_DIM), :]
                pv = _dot(vt, p_buf[1 - slot, li])
            m, l = carry[li]
            if softmax is not None and not _hidden(see(softmax)):
                m_new = jnp.maximum(m, cmax_buf[slot, li])
                alpha = jnp.exp2(m - m_new)
                p = jnp.exp2(s_buf[slot, li] - m_new)
                p_buf[slot, li] = p.astype(BF16)
                m, l = m_new, alpha * l + jnp.sum(p, axis=0, keepdims=True)
                acc_buf[li] = alpha * (acc_buf[li] if pv is None else acc_buf[li] + pv)
            elif pv is not None:
                acc_buf[li] += pv
            new.append((m, l))
        return tuple(new)

    def step_pair(i, carry):
        t = 2 * i + 1
        return step(t + 1, 0, step(t, 1, carry))

    q_norm = _query_norm_bounds(queries, n_chunks)

    def later_tiles_matter(i, carry):
        j_rest = jnp.maximum(n_tiles - 5 - 2 * i, 0)
        worst = None
        for li in range(n_lanes):
            pp, hh, _ = lane_group(li)
            bound = (q_norm[li] * stats_ref[0, pp, j_rest, hh:hh + 1, :]
                     + stats_ref[0, pp, j_rest, 2 + hh:3 + hh, :] - carry[li][0])
            worst = bound if worst is None else jnp.maximum(worst, bound)
        return jnp.max(worst) >= -PRUNE_LOG2

    def pair_and_check(state):
        i, _, carry = state
        carry = step_pair(i, carry)
        return i + 1, later_tiles_matter(i, carry), carry

    carry = tuple((jnp.full((1, cw), M_INIT, F32), jnp.zeros((1, cw), F32)) for _ in range(n_lanes))
    carry = step(-1, 1, carry, score=0, softmax=None, value=False)
    carry = step(0, 0, carry, score=1, softmax=0, value=False)
    n_pairs, _, carry = lax.while_loop(lambda st: (st[0] < qi) & st[1], pair_and_check,
                                       (jnp.int32(0), jnp.bool_(True), carry))
    carry = step(2 * n_pairs + 1, 1, carry, score=None)
    carry = step(2 * n_pairs + 2, 0, carry, score=None, softmax=None)
    _finish_heads([acc_buf[li] / carry[li][1] for li in range(n_lanes)], g_ref, o_ref)


def _sb_kernel(q_ref, k_ref, vt_ref, g_ref, stats_ref, o_ref, z_buf, sp_buf, e_buf, wrow_buf, acc_buf):
    qi = pl.program_id(2)
    bk = vt_ref.shape[3]
    n_wide = z_buf.shape[0]
    n_lanes, cw = acc_buf.shape[0], acc_buf.shape[2]
    n_chunks = q_ref.shape[1] // cw
    lane_group = lambda li: (li // (2 * n_chunks), (li // n_chunks) % 2, li % n_chunks)
    n_tiles = 2 * (qi + 1)
    zero = lambda lane: 0.0
    queries = _lane_queries(q_ref, zero, zero, cw)
    q_norm = _query_norm_bounds(queries, n_chunks)
    suffix = (lax.broadcasted_iota(jnp.int32, (bk, bk), 1)
              >= lax.broadcasted_iota(jnp.int32, (bk, bk), 0)).astype(BF16)
    softplus2 = lambda z: jnp.maximum(z, jnp.log2(1.0 + jnp.exp2(jnp.minimum(z, EXP2_MAX))))
    keys = lambda j, pp: k_ref[0, pl.ds(pl.multiple_of(j * bk, bk), bk), pp * LANES:(pp + 1) * LANES]
    values = lambda j, pp, hh: vt_ref[0, j, pl.ds(pp * LANES + hh * HEAD_DIM, HEAD_DIM), :]

    sees = lambda u, li: _diag_visibility(u, lane_group(li)[2], bk, cw, strict=True) if u < 2 else "all"
    live = [(u, li) for u in range(n_wide) for li in range(n_lanes) if not _hidden(sees(u, li))]
    for u, li in live:
        z = _dot(keys(jnp.maximum(n_tiles - 1 - u, 0), lane_group(li)[0]), queries[li])
        if not isinstance(sees(u, li), str):
            z = jnp.where(sees(u, li), z, MASKED)
        if u >= 2:
            z = jnp.where(u < n_tiles, z, MASKED)
        z_buf[u, li] = z
    for u, li in live:
        sp_buf[u, li] = softplus2(z_buf[u, li]).astype(BF16)
    col_sums = {}
    for u, li in live:
        within = _dot(suffix, sp_buf[u, li])
        col_sums[u, li] = within[0:1, :]
        z_buf[u, li] = z_buf[u, li] - within
    later = []
    for li in range(n_lanes):
        pp, hh, _ = lane_group(li)
        mass, acc = jnp.zeros((1, cw), F32), jnp.zeros((HEAD_DIM, cw), F32)
        for u in range(n_wide):
            if (u, li) in col_sums:
                a = jnp.exp2(z_buf[u, li] - mass)
                acc = acc + _dot(values(jnp.maximum(n_tiles - 1 - u, 0), pp, hh), a.astype(BF16))
                mass = mass + col_sums[u, li]
        acc_buf[li] = acc
        later.append(mass)

    n_rest = n_tiles - n_wide

    def rest_matters(first_unscored, mass):
        j_rest = jnp.maximum(n_rest - 1 - first_unscored, 0)
        worst = None
        for li in range(n_lanes):
            pp, hh, _ = lane_group(li)
            bound = q_norm[li] * stats_ref[0, pp, j_rest, 4 + hh:5 + hh, :] - mass[li]
            worst = bound if worst is None else jnp.maximum(worst, bound)
        return jnp.max(worst) >= -PRUNE_LOG2

    def step(t, slot, later, score=True, softplus=True, cumsum=True, weight=True):
        new_later = []
        for li in range(n_lanes):
            pp, hh, _ = lane_group(li)
            if cumsum:
                within = _dot(suffix, sp_buf[1 - slot, li])
                e_buf[1 - slot, li] = z_buf[1 - slot, li] - within
                wrow_buf[1 - slot, li] = within[0:1, :]
            if score:
                z_buf[1 - slot, li] = _dot(keys(n_rest - 1 - (t + 3), pp), queries[li])
            if weight:
                a = jnp.exp2(e_buf[slot, li] - later[li])
                acc_buf[li] += _dot(values(n_rest - 1 - jnp.maximum(t, 0), pp, hh), a.astype(BF16))
                new_later.append(later[li] + wrow_buf[slot, li])
            else:
                new_later.append(later[li])
            if softplus:
                sp_buf[slot, li] = softplus2(z_buf[slot, li]).astype(BF16)
        return tuple(new_later)

    def pair_and_check(state):
        i, _, later = state
        t = 2 * i - 1
        later = step(t + 1, 0, step(t, 1, later))
        mass = [later[li] + wrow_buf[1, li] for li in range(n_lanes)]
        return i + 1, rest_matters(2 * i + 4, mass), later

    @pl.when((n_rest > 0) & rest_matters(0, later))
    def _():
        e_buf[1] = jnp.full(e_buf.shape[1:], MASKED, F32)
        wrow_buf[1] = jnp.zeros(wrow_buf.shape[1:], F32)
        mass = step(-3, 1, tuple(later), softplus=False, cumsum=False, weight=False)
        mass = step(-2, 0, mass, cumsum=False, weight=False)
        n_pairs, _, mass = lax.while_loop(lambda st: (2 * st[0] + 2 < n_rest) & st[1], pair_and_check,
                                          (jnp.int32(0), jnp.bool_(True), mass))
        mass = step(2 * n_pairs - 1, 1, mass, score=False)
        mass = step(2 * n_pairs, 0, mass, score=False, softplus=False)
        step(2 * n_pairs + 1, 1, mass, score=False, softplus=False, cumsum=False)

    _finish_heads([acc_buf[li] for li in range(n_lanes)], g_ref, o_ref)


def _attention(body, name, pairs, scratch, qk, k_arr, k_spec, vt, g, stats, q_block0, vt_block0, n_heads):
    bsz, s, _ = qk.shape
    bq, bk, width = ATT_Q, ATT_K, pairs * LANES
    d_grp = n_heads * HEAD_DIM
    assert (n_heads // 2) % pairs == 0
    return pl.pallas_call(
        body,
        grid=(bsz, n_heads // 2 // pairs, s // bq),
        in_specs=[pl.BlockSpec((1, bq, width), lambda b, p, i: (b, i, q_block0 + p)),
                  k_spec,
                  pl.BlockSpec((1, s // bk, width, bk), lambda b, p, i: (b, 0, vt_block0 + p, 0)),
                  pl.BlockSpec((1, width), lambda b, p, i: (0, p)),
                  pl.BlockSpec((1, pairs) + stats.shape[2:], lambda b, p, i: (b, p, 0, 0, 0))],
        out_specs=pl.BlockSpec((1, bq, width), lambda b, p, i: (b, i, p)),
        out_shape=jax.ShapeDtypeStruct((bsz, s, d_grp), BF16),
        scratch_shapes=scratch,
        compiler_params=_params("arbitrary", "arbitrary", "arbitrary"),
        name=name,
    )(qk, k_arr, vt, g.reshape(1, d_grp), stats)


def _lane_groups(pairs):
    return pairs * 2 * (ATT_Q // ATT_COLS)


def _fox_scratch(pairs):
    n = _lane_groups(pairs)
    return [pltpu.VMEM((2, n, ATT_K, ATT_COLS), F32), pltpu.VMEM((2, n, 1, ATT_COLS), F32),
            pltpu.VMEM((2, n, ATT_K, ATT_COLS), BF16), pltpu.VMEM((n, HEAD_DIM, ATT_COLS), F32)]


def _sb_scratch(pairs):
    n = _lane_groups(pairs)
    return [pltpu.VMEM((SB_WIDE_TILES, n, ATT_K, ATT_COLS), F32), pltpu.VMEM((SB_WIDE_TILES, n, ATT_K, ATT_COLS), BF16),
            pltpu.VMEM((2, n, ATT_K, ATT_COLS), F32), pltpu.VMEM((2, n, 1, ATT_COLS), F32),
            pltpu.VMEM((n, HEAD_DIM, ATT_COLS), F32)]


def _mixer_kernel(x_ref, xh_ref, mf_ref, mfh_ref, ms_ref, msh_ref, mod_ref, wo_ref, gm_ref, wu_ref, cw_ref,
                  cb_ref, wd_ref, gf_ref, o_ref, u_buf, acc_ref, x1_buf, *, final_norm):
    i = pl.program_id(1)
    tm = x_ref.shape[1]
    n_chunks = wd_ref.shape[0]
    mix = jnp.concatenate([jnp.concatenate([mfh_ref[0], msh_ref[0]], axis=-1),
                           jnp.concatenate([mf_ref[0], ms_ref[0]], axis=-1)], axis=0)
    x_ext = jnp.concatenate([xh_ref[0], x_ref[0]], axis=0)
    x1_ext = x_ext + mod_ref[0, 2:3, :] * _dot(mix, wo_ref[...])
    shift = mod_ref[0, 3:4, :]
    scale = mod_ref[0, 4:5, :]
    h_ext = _rms_rows(x1_ext) * gm_ref[...] * (1.0 + scale) + shift
    row = lax.broadcasted_iota(jnp.int32, h_ext.shape, 0)
    hx = jnp.where((row >= BF16_SUBLANES) | (i > 0), h_ext, 0.0).astype(BF16)
    x1_buf[...] = x1_ext[BF16_SUBLANES:, :]
    acc_ref[...] = jnp.zeros_like(acc_ref)

    def project_up(c, slot):
        for br in range(2):
            u_buf[slot, br] = _dot(hx, wu_ref[br, c])

    def mix_down(c, slot):
        branches = []
        for br in range(2):
            out = cb_ref[br, c]
            for tap in range(CONV_WIDTH):
                first = BF16_SUBLANES - (CONV_WIDTH - 1 - tap)
                out = out + cw_ref[br, c, tap:tap + 1, :] * u_buf[slot, br, pl.ds(first, tm), :]
            branches.append(out)
        u_gate, u_val = branches
        acc_ref[...] += _dot((u_gate * jax.nn.sigmoid(u_gate) * u_val).astype(BF16), wd_ref[c])

    def chunk_pair(j, _):
        c = 2 * j
        project_up(c + 1, 1)
        mix_down(c, 0)
        project_up(c + 2, 0)
        mix_down(c + 1, 1)
        return 0

    project_up(0, 0)
    lax.fori_loop(0, (n_chunks - 1) // 2, chunk_pair, 0)
    mix_down(n_chunks - 1, 0)
    x2 = x1_buf[...] + mod_ref[0, 5:6, :] * acc_ref[...]
    o_ref[0] = _rms_rows(x2) * gf_ref[...] if final_norm else x2


def _mixer(x, mix_f, mix_s, mod, w_out, g_mlp, w_up, conv_w, conv_b, w_down, g_final, final_norm):
    bsz, s, d = x.shape
    tm = OUT_ROWS
    n_chunks, tf = w_down.shape[0], w_down.shape[1]
    assert n_chunks % 2 == 1
    halo_blocks = tm // BF16_SUBLANES
    row = lambda b, i: (b, i, 0)
    halo = lambda b, i: (b, jnp.maximum(i * halo_blocks - 1, 0), 0)
    tile_and_halo = lambda a: [pl.BlockSpec((1, tm, a.shape[2]), row),
                               pl.BlockSpec((1, BF16_SUBLANES, a.shape[2]), halo)]
    resident = lambda a: pl.BlockSpec(a.shape, lambda b, i: (0,) * a.ndim, pipeline_mode=pl.Buffered(1))
    return pl.pallas_call(
        functools.partial(_mixer_kernel, final_norm=final_norm),
        grid=(bsz, s // tm),
        in_specs=tile_and_halo(x) + tile_and_halo(mix_f) + tile_and_halo(mix_s)
                 + [pl.BlockSpec((1, N_MOD, d), lambda b, i: (b, 0, 0)), resident(w_out),
                    pl.BlockSpec((1, d), lambda b, i: (0, 0)),
                    resident(w_up), resident(conv_w), resident(conv_b), resident(w_down),
                    pl.BlockSpec((1, d), lambda b, i: (0, 0))],
        out_specs=pl.BlockSpec((1, tm, d), row),
        out_shape=jax.ShapeDtypeStruct((bsz, s, d), F32),
        scratch_shapes=[pltpu.VMEM((2, 2, tm + BF16_SUBLANES, tf), F32), pltpu.VMEM((tm, d), F32),
                        pltpu.VMEM((tm, d), F32)],
        compiler_params=_params("arbitrary", "arbitrary"),
        name="mixer",
    )(x, x, mix_f, mix_f, mix_s, mix_s, mod, w_out, g_mlp, w_up, conv_w, conv_b, w_down, g_final)


def _chunk_columns(a, d_ff, n_chunks, tf):
    halves = jnp.stack([a[:, :d_ff], a[:, d_ff:]])
    halves = jnp.pad(halves, ((0, 0), (0, 0), (0, n_chunks * tf - d_ff)))
    return halves.reshape(2, a.shape[0], n_chunks, tf).transpose(0, 2, 1, 3)


def _pad_cols(a, n):
    return jnp.pad(a, ((0, 0), (0, n - a.shape[1])))


def kernel(x, c, w_ada, b_ada, g_attn, w_in, b_fgate, g_out_fox, g_out_sb, w_out,
           g_mlp, w_up, conv_w, conv_b, w_down, g_final):
    depth, d, _ = w_ada.shape
    n_fox = b_fgate.shape[1]
    d_fox = n_fox * HEAD_DIM
    d_sb = g_out_sb.shape[1]
    n_sb = d_sb // HEAD_DIM
    d_ff = w_down.shape[1]
    d_ff_pad = -(-d_ff // FF_CHUNK) * FF_CHUNK
    assert n_fox % 2 == 0 and n_sb == n_fox and 3 * n_fox <= LANES
    assert x.shape[1] % OUT_ROWS == 0 and x.shape[1] % ATT_Q == 0 and ATT_Q == 2 * ATT_K
    o_kf, o_vf, o_qs, o_ks, o_vs, o_gate = (d_fox, 2 * d_fox, 3 * d_fox, 3 * d_fox + d_sb,
                                             3 * d_fox + 2 * d_sb, 3 * d_fox + 3 * d_sb)

    for l in range(depth):
        mod = _ada(c, w_ada[l], b_ada[l]).reshape(-1, N_MOD, d)
        w = w_in[l]
        w_nat = jnp.concatenate([w[:, :o_vf], w[:, o_qs:o_vs]], axis=1).astype(BF16)
        w_vt = jnp.concatenate([w[:, o_vf:o_qs], w[:, o_vs:o_gate]], axis=1).T.astype(BF16)
        w_gate = _pad_cols(w[:, o_gate:], LANES).astype(BF16)
        b_gate = _pad_cols(b_fgate[l].reshape(1, n_fox), LANES)
        qk, vt, log_f = _inproj(x, mod, g_attn[l].reshape(1, d), w_nat, w_vt, w_gate, b_gate)

        k_aug, stats = _decay(log_f, qk, n_fox, k_fox_block=1, k_sb_block=3)
        steps_f, steps_s = n_fox // 2 // FOX_PAIRS, n_sb // 2 // SB_PAIRS
        fox_k_spec = pl.BlockSpec((1, 2 * FOX_PAIRS, x.shape[1], LANES), lambda b, p, i: (b, p, 0, 0))
        mix_f = _attention(_fox_kernel, "fox", FOX_PAIRS, _fox_scratch(FOX_PAIRS), qk, k_aug, fox_k_spec, vt,
                           g_out_fox[l], stats, q_block0=0, vt_block0=0, n_heads=n_fox)
        sb_k_spec = pl.BlockSpec((1, x.shape[1], SB_PAIRS * LANES), lambda b, p, i: (b, 0, 3 * steps_s + p))
        mix_s = _attention(_sb_kernel, "sb", SB_PAIRS, _sb_scratch(SB_PAIRS), qk, qk, sb_k_spec, vt,
                           g_out_sb[l], stats, q_block0=2 * steps_s, vt_block0=steps_s, n_heads=n_sb)

        n_ff = d_ff_pad // FF_CHUNK
        x = _mixer(x, mix_f, mix_s, mod, w_out[l].astype(BF16), g_mlp[l].reshape(1, d),
                   _chunk_columns(w_up[l], d_ff, n_ff, FF_CHUNK).astype(BF16),
                   _chunk_columns(conv_w[l], d_ff, n_ff, FF_CHUNK),
                   _chunk_columns(conv_b[l].reshape(1, -1), d_ff, n_ff, FF_CHUNK),
                   jnp.pad(w_down[l], ((0, d_ff_pad - d_ff), (0, 0))).astype(BF16).reshape(n_ff, FF_CHUNK, d),
                   g_final.reshape(1, d), final_norm=(l == depth - 1))
    return x
```

```python
import functools

import numpy as np
import jax
import jax.numpy as jnp
from jax import lax
from jax.experimental import pallas as pl
from jax.experimental.pallas import tpu as pltpu

HEAD_DIM = 64
N_MOD = 6
CONV_WIDTH = 3
EPS = 1e-6

LANES = 128
BF16_SUBLANES = 16
VMEM_LIMIT_BYTES = 48 * 1024 * 1024

ATT_Q = 512
ATT_K = 256
ATT_COLS = 256
FOX_PAIRS = 1
SB_PAIRS = 1
SB_WIDE_TILES = 4
PROJ_ROWS = ATT_K
LOG2E = 1.4426950408889634
MASKED = -1e30
M_INIT = -1e29
EXP2_MAX = 126.0
PRUNE_LOG2 = 160.0
NORM_SLACK = 1.02
STATS_ROWS = 8
OUT_ROWS = 512
FF_CHUNK = 256

F32 = jnp.float32
BF16 = jnp.bfloat16
NT_DIMS = (((1,), (1,)), ((), ()))


def _dot(a, b):
    return jnp.dot(a, b, preferred_element_type=F32)


def _dot_nt(a, b):
    return lax.dot_general(a, b, NT_DIMS, preferred_element_type=F32)


def _params(*sem):
    return pltpu.CompilerParams(dimension_semantics=sem, vmem_limit_bytes=VMEM_LIMIT_BYTES)


def _rms_rows(x):
    return x * lax.rsqrt(jnp.mean(x * x, axis=-1, keepdims=True) + EPS)


def _softplus(z):
    return jnp.maximum(z, 0.0) + jnp.log(1.0 + jnp.exp(-jnp.abs(z)))


def _split3(x):
    hi = x.astype(BF16)
    r1 = x - hi.astype(F32)
    mid = r1.astype(BF16)
    lo = (r1 - mid.astype(F32)).astype(BF16)
    return hi, mid, lo


def _ada_kernel(c_ref, w_ref, b_ref, o_ref):
    c = c_ref[...]
    o_ref[...] = _dot(c * jax.nn.sigmoid(c), w_ref[...]) + b_ref[...]


def _ada(c, w, b):
    bsz, d = c.shape
    n = w.shape[1]
    return pl.pallas_call(
        _ada_kernel,
        grid=(n // d,),
        in_specs=[pl.BlockSpec((bsz, d), lambda j: (0, 0)),
                  pl.BlockSpec((d, d), lambda j: (0, j)),
                  pl.BlockSpec((1, d), lambda j: (0, j))],
        out_specs=pl.BlockSpec((bsz, d), lambda j: (0, j)),
        out_shape=jax.ShapeDtypeStruct((bsz, n), F32),
        compiler_params=_params("arbitrary"),
        name="ada",
    )(c, w, b.reshape(1, n))


def _inproj_kernel(x_ref, mod_ref, g_ref, wn_ref, wvt_ref, wg_ref, bg_ref, qk_ref, vt_ref, lf_ref):
    shift = mod_ref[0, 0:1, :]
    scale = mod_ref[0, 1:2, :]
    h = (_rms_rows(x_ref[0]) * g_ref[...] * (1.0 + scale) + shift).astype(BF16)
    qk_ref[0] = _dot(h, wn_ref[...]).astype(BF16)
    vt_ref[0, 0] = _dot_nt(wvt_ref[...], h).astype(BF16)
    logit = _dot(h, wg_ref[...]) + bg_ref[...]
    lf_ref[0] = -_softplus(-logit)


def _inproj(x, mod, g, w_nat, w_vt, w_gate, b_gate):
    bsz, s, d = x.shape
    tm = PROJ_ROWS
    n_nat, n_v = w_nat.shape[1], w_vt.shape[0]
    const = lambda b, i: (0, 0)
    return pl.pallas_call(
        _inproj_kernel,
        grid=(bsz, s // tm),
        in_specs=[pl.BlockSpec((1, tm, d), lambda b, i: (b, i, 0)),
                  pl.BlockSpec((1, N_MOD, d), lambda b, i: (b, 0, 0)),
                  pl.BlockSpec((1, d), const),
                  pl.BlockSpec((d, n_nat), const),
                  pl.BlockSpec((n_v, d), const),
                  pl.BlockSpec((d, LANES), const),
                  pl.BlockSpec((1, LANES), const)],
        out_specs=[pl.BlockSpec((1, tm, n_nat), lambda b, i: (b, i, 0)),
                   pl.BlockSpec((1, 1, n_v, tm), lambda b, i: (b, i, 0, 0)),
                   pl.BlockSpec((1, tm, LANES), lambda b, i: (b, i, 0))],
        out_shape=[jax.ShapeDtypeStruct((bsz, s, n_nat), BF16),
                   jax.ShapeDtypeStruct((bsz, s // tm, n_v, tm), BF16),
                   jax.ShapeDtypeStruct((bsz, s, LANES), F32)],
        compiler_params=_params("arbitrary", "arbitrary"),
        name="inproj",
    )(x, mod, g, w_nat, w_vt, w_gate, b_gate)


def _decay_kernel(lf_ref, k_ref, ks_ref, sel_ref, ind_ref, kaug_ref, stats_ref, carry_ref, kpre_ref, *, n_heads):
    @pl.when(pl.program_id(1) == 0)
    def _():
        carry_ref[...] = jnp.zeros_like(carry_ref)
        kpre_ref[...] = jnp.zeros_like(kpre_ref)

    tm = lf_ref.shape[1]
    lane = lax.broadcasted_iota(jnp.int32, (tm, LANES), 1)
    lf = jnp.where(lane < n_heads, lf_ref[0], 0.0)
    row = lax.broadcasted_iota(jnp.int32, (tm, tm), 0)
    col = lax.broadcasted_iota(jnp.int32, (tm, tm), 1)
    tri = (col <= row).astype(BF16)
    hi, mid, lo = _split3(lf)
    f_run = carry_ref[...] + (_dot(tri, hi) + _dot(tri, mid) + _dot(tri, lo))
    carry_ref[...] = f_run[tm - 1:tm, :]
    ghi, gmid, glo = _split3(-LOG2E * f_run)
    packed = (ghi.astype(F32) + pltpu.roll(gmid.astype(F32), n_heads, 1)
              + pltpu.roll(glo.astype(F32), 2 * n_heads, 1)).astype(BF16)
    placed = _dot(packed, sel_ref[...])
    k_all = k_ref[0]
    for h in range(n_heads):
        k_pair = k_all[:, (h // 2) * LANES:(h // 2 + 1) * LANES]
        own = (lane < HEAD_DIM) if h % 2 == 0 else (lane >= HEAD_DIM)
        kaug_ref[0, h] = jnp.where(own, k_pair, placed[:, h * LANES:(h + 1) * LANES].astype(BF16))

    def head_norm_bound(k):
        k32 = k.astype(F32)
        sq = _dot((k32 * k32).astype(BF16), ind_ref[...])
        return jnp.sqrt(jnp.max(sq, axis=0, keepdims=True) * NORM_SLACK)

    kpre_f = jnp.maximum(kpre_ref[0:1, :], head_norm_bound(k_all))
    kpre_s = jnp.maximum(kpre_ref[1:2, :], head_norm_bound(ks_ref[0]))
    kpre_ref[0:1, :] = kpre_f
    kpre_ref[1:2, :] = kpre_s
    g_end = -LOG2E * f_run[tm - 1:tm, :]
    lane1 = lax.broadcasted_iota(jnp.int32, (1, LANES), 1)

    def spread(v, h):
        return jnp.broadcast_to(jnp.sum(jnp.where(lane1 == h, v, 0.0), axis=1, keepdims=True), (1, ATT_COLS))

    for p in range(n_heads // 2):
        rows = [spread(v, 2 * p + hh) for v in (kpre_f, g_end, kpre_s) for hh in range(2)]
        rows += [jnp.zeros((1, ATT_COLS), F32)] * (stats_ref.shape[3] - len(rows))
        stats_ref[0, p, 0] = jnp.concatenate(rows, axis=0)


def _head_indicator(n_heads):
    ind = np.zeros((n_heads * HEAD_DIM, LANES), np.float32)
    ind[np.arange(n_heads * HEAD_DIM), np.arange(n_heads * HEAD_DIM) // HEAD_DIM] = 1.0
    return jnp.asarray(ind, BF16)


def _decay_select_matrix(n_heads):
    sel = np.zeros((LANES, n_heads * LANES), np.float32)
    for h in range(n_heads):
        base = h * LANES + (HEAD_DIM if h % 2 == 0 else 0)
        for term in range(3):
            sel[term * n_heads + h, base + term] = 1.0
    return jnp.asarray(sel, BF16)


def _decay(log_f, qk, n_heads, k_fox_block, k_sb_block):
    bsz, s, _ = log_f.shape
    tm = PROJ_ROWS
    d_grp = n_heads * HEAD_DIM
    return pl.pallas_call(
        functools.partial(_decay_kernel, n_heads=n_heads),
        grid=(bsz, s // tm),
        in_specs=[pl.BlockSpec((1, tm, LANES), lambda b, i: (b, i, 0)),
                  pl.BlockSpec((1, tm, d_grp), lambda b, i: (b, i, k_fox_block)),
                  pl.BlockSpec((1, tm, d_grp), lambda b, i: (b, i, k_sb_block)),
                  pl.BlockSpec((LANES, n_heads * LANES), lambda b, i: (0, 0)),
                  pl.BlockSpec((d_grp, LANES), lambda b, i: (0, 0))],
        out_specs=[pl.BlockSpec((1, n_heads, tm, LANES), lambda b, i: (b, 0, i, 0)),
                   pl.BlockSpec((1, n_heads // 2, 1, STATS_ROWS, ATT_COLS), lambda b, i: (b, 0, i, 0, 0))],
        out_shape=[jax.ShapeDtypeStruct((bsz, n_heads, s, LANES), BF16),
                   jax.ShapeDtypeStruct((bsz, n_heads // 2, s // tm, STATS_ROWS, ATT_COLS), F32)],
        scratch_shapes=[pltpu.VMEM((1, LANES), F32), pltpu.VMEM((2, LANES), F32)],
        compiler_params=_params("arbitrary", "arbitrary"),
        name="decay",
    )(log_f, qk, qk, _decay_select_matrix(n_heads), _head_indicator(n_heads))


def _lane_queries(q_ref, extra_even, extra_odd, cw):
    out = []
    for pp in range(q_ref.shape[2] // LANES):
        q = q_ref[0, :, pp * LANES:(pp + 1) * LANES].astype(F32) * (HEAD_DIM ** -0.5 * LOG2E)
        lane = lax.broadcasted_iota(jnp.int32, q.shape, 1)
        heads = (jnp.where(lane < HEAD_DIM, q, extra_even(lane)).T.astype(BF16),
                 jnp.where(lane >= HEAD_DIM, q, extra_odd(lane)).T.astype(BF16))
        out += [heads[hh][:, c * cw:(c + 1) * cw] for hh in range(2) for c in range(q.shape[0] // cw)]
    return out


def _visibility(first_key, first_query, bk, cw, strict):
    last_visible_gap = -1 if strict else 0
    if first_key + bk - 1 - first_query <= last_visible_gap:
        return "all"
    if first_key - (first_query + cw - 1) > last_visible_gap:
        return "none"
    gap = (lax.broadcasted_iota(jnp.int32, (bk, cw), 0) - lax.broadcasted_iota(jnp.int32, (bk, cw), 1)
           + (first_key - first_query))
    return gap <= last_visible_gap


def _diag_visibility(u, c, bk, cw, strict):
    return _visibility((1 - u) * bk, c * cw, bk, cw, strict)


def _hidden(visibility):
    return isinstance(visibility, str) and visibility == "none"


def _query_norm_bounds(queries, n_chunks):
    bounds = []
    for li, q in enumerate(queries):
        hh = (li // n_chunks) % 2
        own = q[hh * HEAD_DIM:(hh + 1) * HEAD_DIM, :].astype(F32)
        bounds.append(jnp.sqrt(jnp.sum(own * own, axis=0, keepdims=True) * NORM_SLACK))
    return bounds


def _finish_heads(lanes, g_ref, o_ref):
    n_pairs = o_ref.shape[2] // LANES
    n_chunks = len(lanes) // (2 * n_pairs)
    for pp in range(n_pairs):
        mine = lanes[2 * pp * n_chunks:2 * (pp + 1) * n_chunks]
        outs = [jnp.concatenate(mine[hh * n_chunks:(hh + 1) * n_chunks], axis=1) for hh in range(2)]
        normed = [o * lax.rsqrt(jnp.mean(o * o, axis=0, keepdims=True) + EPS) for o in outs]
        cols = slice(pp * LANES, (pp + 1) * LANES)
        o_ref[0, :, cols] = (jnp.concatenate(normed, axis=0).T * g_ref[:, cols]).astype(o_ref.dtype)


def _fox_kernel(q_ref, k_ref, vt_ref, g_ref, stats_ref, o_ref, s_buf, cmax_buf, p_buf, acc_buf):
    qi = pl.program_id(2)
    bk = vt_ref.shape[3]
    n_lanes, cw = acc_buf.shape[0], acc_buf.shape[2]
    n_chunks = q_ref.shape[1] // cw
    lane_group = lambda li: (li // (2 * n_chunks), (li // n_chunks) % 2, li % n_chunks)
    n_tiles = 2 * (qi + 1)
    ones3 = lambda lo: (lambda lane: jnp.where((lane >= lo) & (lane < lo + 3), 1.0, 0.0))
    queries = _lane_queries(q_ref, ones3(HEAD_DIM), ones3(0), cw)
    acc_buf[...] = jnp.zeros(acc_buf.shape, F32)
    for li in range(n_lanes):
        if _hidden(_diag_visibility(0, lane_group(li)[2], bk, cw, strict=False)):
            p_buf[0, li] = jnp.zeros((bk, cw), BF16)

    def step(t, slot, carry, score="below", softmax="below", value=True):
        new = []
        for li in range(n_lanes):
            pp, hh, c = lane_group(li)
            see = lambda u: "all" if u == "below" else _diag_visibility(u, c, bk, cw, strict=False)
            if score is not None and not _hidden(see(score)):
                start = pl.multiple_of((n_tiles - 2 - t) * bk, bk)
                s_new = _dot(k_ref[0, 2 * pp + hh, pl.ds(start, bk), :], queries[li])
                if not isinstance(see(score), str):
                    s_new = jnp.where(see(score), s_new, MASKED)
                s_buf[1 - slot, li] = s_new
                cmax_buf[1 - slot, li] = jnp.max(s_new, axis=0, keepdims=True)
            pv = None
            if value:
                vt = vt_ref[0, n_tiles - t, pl.ds(pp * LANES + hh * HEAD_DIM, HEAD_DIM), :]
                pv = _dot(vt, p_buf[1 - slot, li])
            m, l = carry[li]
            if softmax is not None and not _hidden(see(softmax)):
                m_new = jnp.maximum(m, cmax_buf[slot, li])
                alpha = jnp.exp2(m - m_new)
                p = jnp.exp2(s_buf[slot, li] - m_new)
                p_buf[slot, li] = p.astype(BF16)
                m, l = m_new, alpha * l + jnp.sum(p, axis=0, keepdims=True)
                acc_buf[li] = alpha * (acc_buf[li] if pv is None else acc_buf[li] + pv)
            elif pv is not None:
                acc_buf[li] += pv
            new.append((m, l))
        return tuple(new)

    def step_pair(i, carry):
        t = 2 * i + 1
        return step(t + 1, 0, step(t, 1, carry))

    q_norm = _query_norm_bounds(queries, n_chunks)

    def later_tiles_matter(i, carry):
        j_rest = jnp.maximum(n_tiles - 5 - 2 * i, 0)
        worst = None
        for li in range(n_lanes):
            pp, hh, _ = lane_group(li)
            bound = (q_norm[li] * stats_ref[0, pp, j_rest, hh:hh + 1, :]
                     + stats_ref[0, pp, j_rest, 2 + hh:3 + hh, :] - carry[li][0])
            worst = bound if worst is None else jnp.maximum(worst, bound)
        return jnp.max(worst) >= -PRUNE_LOG2

    def pair_and_check(state):
        i, _, carry = state
        carry = step_pair(i, carry)
        return i + 1, later_tiles_matter(i, carry), carry

    carry = tuple((jnp.full((1, cw), M_INIT, F32), jnp.zeros((1, cw), F32)) for _ in range(n_lanes))
    carry = step(-1, 1, carry, score=0, softmax=None, value=False)
    carry = step(0, 0, carry, score=1, softmax=0, value=False)
    n_pairs, _, carry = lax.while_loop(lambda st: (st[0] < qi) & st[1], pair_and_check,
                                       (jnp.int32(0), jnp.bool_(True), carry))
    carry = step(2 * n_pairs + 1, 1, carry, score=None)
    carry = step(2 * n_pairs + 2, 0, carry, score=None, softmax=None)
    _finish_heads([acc_buf[li] / carry[li][1] for li in range(n_lanes)], g_ref, o_ref)


def _sb_kernel(q_ref, k_ref, vt_ref, g_ref, stats_ref, o_ref, z_buf, sp_buf, e_buf, wrow_buf, acc_buf):
    qi = pl.program_id(2)
    bk = vt_ref.shape[3]
    n_wide = z_buf.shape[0]
    n_lanes, cw = acc_buf.shape[0], acc_buf.shape[2]
    n_chunks = q_ref.shape[1] // cw
    lane_group = lambda li: (li // (2 * n_chunks), (li // n_chunks) % 2, li % n_chunks)
    n_tiles = 2 * (qi + 1)
    zero = lambda lane: 0.0
    queries = _lane_queries(q_ref, zero, zero, cw)
    q_norm = _query_norm_bounds(queries, n_chunks)
    suffix = (lax.broadcasted_iota(jnp.int32, (bk, bk), 1)
              >= lax.broadcasted_iota(jnp.int32, (bk, bk), 0)).astype(BF16)
    softplus2 = lambda z: jnp.maximum(z, jnp.log2(1.0 + jnp.exp2(jnp.minimum(z, EXP2_MAX))))
    keys = lambda j, pp: k_ref[0, pl.ds(pl.multiple_of(j * bk, bk), bk), pp * LANES:(pp + 1) * LANES]
    values = lambda j, pp, hh: vt_ref[0, j, pl.ds(pp * LANES + hh * HEAD_DIM, HEAD_DIM), :]

    sees = lambda u, li: _diag_visibility(u, lane_group(li)[2], bk, cw, strict=True) if u < 2 else "all"
    live = [(u, li) for u in range(n_wide) for li in range(n_lanes) if not _hidden(sees(u, li))]
    later = [jnp.zeros((1, cw), F32)] * n_lanes
    acc_buf[...] = jnp.zeros(acc_buf.shape, F32)

    def wide_score(u, li):
        z = _dot(keys(jnp.maximum(n_tiles - 1 - u, 0), lane_group(li)[0]), queries[li])
        if not isinstance(sees(u, li), str):
            z = jnp.where(sees(u, li), z, MASKED)
        if u >= 2:
            z = jnp.where(u < n_tiles, z, MASKED)
        z_buf[u, li] = z

    def wide_softplus(u, li):
        sp_buf[u, li] = softplus2(z_buf[u, li]).astype(BF16)

    col_sums = {}

    def wide_cumsum(u, li):
        within = _dot(suffix, sp_buf[u, li])
        col_sums[u, li] = within[0:1, :]
        z_buf[u, li] = z_buf[u, li] - within

    def wide_weight(u, li):
        pp, hh, _ = lane_group(li)
        a = jnp.exp2(z_buf[u, li] - later[li])
        acc_buf[li] += _dot(values(jnp.maximum(n_tiles - 1 - u, 0), pp, hh), a.astype(BF16))
        later[li] = later[li] + col_sums[u, li]

    stages = (wide_score, wide_softplus, wide_cumsum, wide_weight)
    for pos in range(len(live) + len(stages) - 1):
        for lag, stage in enumerate(stages):
            if 0 <= pos - lag < len(live):
                stage(*live[pos - lag])

    n_rest = n_tiles - n_wide

    def rest_matters(first_unscored, mass):
        j_rest = jnp.maximum(n_rest - 1 - first_unscored, 0)
        worst = None
        for li in range(n_lanes):
            pp, hh, _ = lane_group(li)
            bound = q_norm[li] * stats_ref[0, pp, j_rest, 4 + hh:5 + hh, :] - mass[li]
            worst = bound if worst is None else jnp.maximum(worst, bound)
        return jnp.max(worst) >= -PRUNE_LOG2

    def step(t, slot, later, score=True, softplus=True, cumsum=True, weight=True):
        new_later = []
        for li in range(n_lanes):
            pp, hh, _ = lane_group(li)
            if cumsum:
                within = _dot(suffix, sp_buf[1 - slot, li])
                e_buf[1 - slot, li] = z_buf[1 - slot, li] - within
                wrow_buf[1 - slot, li] = within[0:1, :]
            if score:
                z_buf[1 - slot, li] = _dot(keys(n_rest - 1 - (t + 3), pp), queries[li])
            if weight:
                a = jnp.exp2(e_buf[slot, li] - later[li])
                acc_buf[li] += _dot(values(n_rest - 1 - jnp.maximum(t, 0), pp, hh), a.astype(BF16))
                new_later.append(later[li] + wrow_buf[slot, li])
            else:
                new_later.append(later[li])
            if softplus:
                sp_buf[slot, li] = softplus2(z_buf[slot, li]).astype(BF16)
        return tuple(new_later)

    def pair_and_check(state):
        i, _, later = state
        t = 2 * i - 1
        later = step(t + 1, 0, step(t, 1, later))
        mass = [later[li] + wrow_buf[1, li] for li in range(n_lanes)]
        return i + 1, rest_matters(2 * i + 4, mass), later

    @pl.when((n_rest > 0) & rest_matters(0, later))
    def _():
        e_buf[1] = jnp.full(e_buf.shape[1:], MASKED, F32)
        wrow_buf[1] = jnp.zeros(wrow_buf.shape[1:], F32)
        mass = step(-3, 1, tuple(later), softplus=False, cumsum=False, weight=False)
        mass = step(-2, 0, mass, cumsum=False, weight=False)
        n_pairs, _, mass = lax.while_loop(lambda st: (2 * st[0] + 2 < n_rest) & st[1], pair_and_check,
                                          (jnp.int32(0), jnp.bool_(True), mass))
        mass = step(2 * n_pairs - 1, 1, mass, score=False)
        mass = step(2 * n_pairs, 0, mass, score=False, softplus=False)
        step(2 * n_pairs + 1, 1, mass, score=False, softplus=False, cumsum=False)

    _finish_heads([acc_buf[li] for li in range(n_lanes)], g_ref, o_ref)


def _attention(body, name, pairs, scratch, qk, k_arr, k_spec, vt, g, stats, q_block0, vt_block0, n_heads):
    bsz, s, _ = qk.shape
    bq, bk, width = ATT_Q, ATT_K, pairs * LANES
    d_grp = n_heads * HEAD_DIM
    assert (n_heads // 2) % pairs == 0
    return pl.pallas_call(
        body,
        grid=(bsz, n_heads // 2 // pairs, s // bq),
        in_specs=[pl.BlockSpec((1, bq, width), lambda b, p, i: (b, i, q_block0 + p)),
                  k_spec,
                  pl.BlockSpec((1, s // bk, width, bk), lambda b, p, i: (b, 0, vt_block0 + p, 0)),
                  pl.BlockSpec((1, width), lambda b, p, i: (0, p)),
                  pl.BlockSpec((1, pairs) + stats.shape[2:], lambda b, p, i: (b, p, 0, 0, 0))],
        out_specs=pl.BlockSpec((1, bq, width), lambda b, p, i: (b, i, p)),
        out_shape=jax.ShapeDtypeStruct((bsz, s, d_grp), BF16),
        scratch_shapes=scratch,
        compiler_params=_params("arbitrary", "arbitrary", "arbitrary"),
        name=name,
    )(qk, k_arr, vt, g.reshape(1, d_grp), stats)


def _lane_groups(pairs):
    return pairs * 2 * (ATT_Q // ATT_COLS)


def _fox_scratch(pairs):
    n = _lane_groups(pairs)
    return [pltpu.VMEM((2, n, ATT_K, ATT_COLS), F32), pltpu.VMEM((2, n, 1, ATT_COLS), F32),
            pltpu.VMEM((2, n, ATT_K, ATT_COLS), BF16), pltpu.VMEM((n, HEAD_DIM, ATT_COLS), F32)]


def _sb_scratch(pairs):
    n = _lane_groups(pairs)
    return [pltpu.VMEM((SB_WIDE_TILES, n, ATT_K, ATT_COLS), F32), pltpu.VMEM((SB_WIDE_TILES, n, ATT_K, ATT_COLS), BF16),
            pltpu.VMEM((2, n, ATT_K, ATT_COLS), F32), pltpu.VMEM((2, n, 1, ATT_COLS), F32),
            pltpu.VMEM((n, HEAD_DIM, ATT_COLS), F32)]


def _mixer_kernel(x_ref, xh_ref, mf_ref, mfh_ref, ms_ref, msh_ref, mod_ref, wo_ref, gm_ref, wu_ref, cw_ref,
                  cb_ref, wd_ref, gf_ref, o_ref, u_buf, acc_ref, x1_buf, *, final_norm):
    i = pl.program_id(1)
    tm = x_ref.shape[1]
    n_chunks = wd_ref.shape[0]
    mix = jnp.concatenate([jnp.concatenate([mfh_ref[0], msh_ref[0]], axis=-1),
                           jnp.concatenate([mf_ref[0], ms_ref[0]], axis=-1)], axis=0)
    x_ext = jnp.concatenate([xh_ref[0], x_ref[0]], axis=0)
    x1_ext = x_ext + mod_ref[0, 2:3, :] * _dot(mix, wo_ref[...])
    shift = mod_ref[0, 3:4, :]
    scale = mod_ref[0, 4:5, :]
    h_ext = _rms_rows(x1_ext) * gm_ref[...] * (1.0 + scale) + shift
    row = lax.broadcasted_iota(jnp.int32, h_ext.shape, 0)
    hx = jnp.where((row >= BF16_SUBLANES) | (i > 0), h_ext, 0.0).astype(BF16)
    x1_buf[...] = x1_ext[BF16_SUBLANES:, :]
    acc_ref[...] = jnp.zeros_like(acc_ref)

    def project_up(c, slot):
        for br in range(2):
            u_buf[slot, br] = _dot(hx, wu_ref[br, c])

    def mix_down(c, slot):
        branches = []
        for br in range(2):
            out = cb_ref[br, c]
            for tap in range(CONV_WIDTH):
                first = BF16_SUBLANES - (CONV_WIDTH - 1 - tap)
                out = out + cw_ref[br, c, tap:tap + 1, :] * u_buf[slot, br, pl.ds(first, tm), :]
            branches.append(out)
        u_gate, u_val = branches
        acc_ref[...] += _dot((u_gate * jax.nn.sigmoid(u_gate) * u_val).astype(BF16), wd_ref[c])

    def chunk_pair(j, _):
        c = 2 * j
        project_up(c + 1, 1)
        mix_down(c, 0)
        project_up(c + 2, 0)
        mix_down(c + 1, 1)
        return 0

    project_up(0, 0)
    lax.fori_loop(0, (n_chunks - 1) // 2, chunk_pair, 0)
    mix_down(n_chunks - 1, 0)
    x2 = x1_buf[...] + mod_ref[0, 5:6, :] * acc_ref[...]
    o_ref[0] = _rms_rows(x2) * gf_ref[...] if final_norm else x2


def _mixer(x, mix_f, mix_s, mod, w_out, g_mlp, w_up, conv_w, conv_b, w_down, g_final, final_norm):
    bsz, s, d = x.shape
    tm = OUT_ROWS
    n_chunks, tf = w_down.shape[0], w_down.shape[1]
    assert n_chunks % 2 == 1
    halo_blocks = tm // BF16_SUBLANES
    row = lambda b, i: (b, i, 0)
    halo = lambda b, i: (b, jnp.maximum(i * halo_blocks - 1, 0), 0)
    tile_and_halo = lambda a: [pl.BlockSpec((1, tm, a.shape[2]), row),
                               pl.BlockSpec((1, BF16_SUBLANES, a.shape[2]), halo)]
    resident = lambda a: pl.BlockSpec(a.shape, lambda b, i: (0,) * a.ndim, pipeline_mode=pl.Buffered(1))
    return pl.pallas_call(
        functools.partial(_mixer_kernel, final_norm=final_norm),
        grid=(bsz, s // tm),
        in_specs=tile_and_halo(x) + tile_and_halo(mix_f) + tile_and_halo(mix_s)
                 + [pl.BlockSpec((1, N_MOD, d), lambda b, i: (b, 0, 0)), resident(w_out),
                    pl.BlockSpec((1, d), lambda b, i: (0, 0)),
                    resident(w_up), resident(conv_w), resident(conv_b), resident(w_down),
                    pl.BlockSpec((1, d), lambda b, i: (0, 0))],
        out_specs=pl.BlockSpec((1, tm, d), row),
        out_shape=jax.ShapeDtypeStruct((bsz, s, d), F32),
        scratch_shapes=[pltpu.VMEM((2, 2, tm + BF16_SUBLANES, tf), F32), pltpu.VMEM((tm, d), F32),
                        pltpu.VMEM((tm, d), F32)],
        compiler_params=_params("arbitrary", "arbitrary"),
        name="mixer",
    )(x, x, mix_f, mix_f, mix_s, mix_s, mod, w_out, g_mlp, w_up, conv_w, conv_b, w_down, g_final)


def _chunk_columns(a, d_ff, n_chunks, tf):
    halves = jnp.stack([a[:, :d_ff], a[:, d_ff:]])
    halves = jnp.pad(halves, ((0, 0), (0, 0), (0, n_chunks * tf - d_ff)))
    return halves.reshape(2, a.shape[0], n_chunks, tf).transpose(0, 2, 1, 3)


def _pad_cols(a, n):
    return jnp.pad(a, ((0, 0), (0, n - a.shape[1])))


def kernel(x, c, w_ada, b_ada, g_attn, w_in, b_fgate, g_out_fox, g_out_sb, w_out,
           g_mlp, w_up, conv_w, conv_b, w_down, g_final):
    depth, d, _ = w_ada.shape
    n_fox = b_fgate.shape[1]
    d_fox = n_fox * HEAD_DIM
    d_sb = g_out_sb.shape[1]
    n_sb = d_sb // HEAD_DIM
    d_ff = w_down.shape[1]
    d_ff_pad = -(-d_ff // FF_CHUNK) * FF_CHUNK
    assert n_fox % 2 == 0 and n_sb == n_fox and 3 * n_fox <= LANES
    assert x.shape[1] % OUT_ROWS == 0 and x.shape[1] % ATT_Q == 0 and ATT_Q == 2 * ATT_K
    o_kf, o_vf, o_qs, o_ks, o_vs, o_gate = (d_fox, 2 * d_fox, 3 * d_fox, 3 * d_fox + d_sb,
                                             3 * d_fox + 2 * d_sb, 3 * d_fox + 3 * d_sb)

    for l in range(depth):
        mod = _ada(c, w_ada[l], b_ada[l]).reshape(-1, N_MOD, d)
        w = w_in[l]
        w_nat = jnp.concatenate([w[:, :o_vf], w[:, o_qs:o_vs]], axis=1).astype(BF16)
        w_vt = jnp.concatenate([w[:, o_vf:o_qs], w[:, o_vs:o_gate]], axis=1).T.astype(BF16)
        w_gate = _pad_cols(w[:, o_gate:], LANES).astype(BF16)
        b_gate = _pad_cols(b_fgate[l].reshape(1, n_fox), LANES)
        qk, vt, log_f = _inproj(x, mod, g_attn[l].reshape(1, d), w_nat, w_vt, w_gate, b_gate)

        k_aug, stats = _decay(log_f, qk, n_fox, k_fox_block=1, k_sb_block=3)
        steps_f, steps_s = n_fox // 2 // FOX_PAIRS, n_sb // 2 // SB_PAIRS
        fox_k_spec = pl.BlockSpec((1, 2 * FOX_PAIRS, x.shape[1], LANES), lambda b, p, i: (b, p, 0, 0))
        mix_f = _attention(_fox_kernel, "fox", FOX_PAIRS, _fox_scratch(FOX_PAIRS), qk, k_aug, fox_k_spec, vt,
                           g_out_fox[l], stats, q_block0=0, vt_block0=0, n_heads=n_fox)
        sb_k_spec = pl.BlockSpec((1, x.shape[1], SB_PAIRS * LANES), lambda b, p, i: (b, 0, 3 * steps_s + p))
        mix_s = _attention(_sb_kernel, "sb", SB_PAIRS, _sb_scratch(SB_PAIRS), qk, qk, sb_k_spec, vt,
                           g_out_sb[l], stats, q_block0=2 * steps_s, vt_block0=steps_s, n_heads=n_sb)

        n_ff = d_ff_pad // FF_CHUNK
        x = _mixer(x, mix_f, mix_s, mod, w_out[l].astype(BF16), g_mlp[l].reshape(1, d),
                   _chunk_columns(w_up[l], d_ff, n_ff, FF_CHUNK).astype(BF16),
                   _chunk_columns(conv_w[l], d_ff, n_ff, FF_CHUNK),
                   _chunk_columns(conv_b[l].reshape(1, -1), d_ff, n_ff, FF_CHUNK),
                   jnp.pad(w_down[l], ((0, d_ff_pad - d_ff), (0, 0))).astype(BF16).reshape(n_ff, FF_CHUNK, d),
                   g_final.reshape(1, d), final_norm=(l == depth - 1))
    return x
```

```python
import functools

import numpy as np
import jax
import jax.numpy as jnp
from jax import lax
from jax.experimental import pallas as pl
from jax.experimental.pallas import tpu as pltpu

HEAD_DIM = 64
N_MOD = 6
CONV_WIDTH = 3
EPS = 1e-6

LANES = 128
BF16_SUBLANES = 16
VMEM_LIMIT_BYTES = 48 * 1024 * 1024

ATT_Q = 512
ATT_K = 256
ATT_COLS = 256
FOX_PAIRS = 1
SB_PAIRS = 1
SB_WIDE_TILES = 4
PROJ_ROWS = 2 * ATT_K
LOG2E = 1.4426950408889634
MASKED = -1e30
M_INIT = -1e29
EXP2_MAX = 126.0
PRUNE_LOG2 = 160.0
NORM_SLACK = 1.02
STATS_ROWS = 8
OUT_ROWS = 512
FF_CHUNK = 256

F32 = jnp.float32
BF16 = jnp.bfloat16
NT_DIMS = (((1,), (1,)), ((), ()))


def _dot(a, b):
    return jnp.dot(a, b, preferred_element_type=F32)


def _dot_nt(a, b):
    return lax.dot_general(a, b, NT_DIMS, preferred_element_type=F32)


def _params(*sem):
    return pltpu.CompilerParams(dimension_semantics=sem, vmem_limit_bytes=VMEM_LIMIT_BYTES)


def _rms_rows(x):
    return x * lax.rsqrt(jnp.mean(x * x, axis=-1, keepdims=True) + EPS)


def _softplus(z):
    return jnp.maximum(z, 0.0) + jnp.log(1.0 + jnp.exp(-jnp.abs(z)))


def _split3(x):
    hi = x.astype(BF16)
    r1 = x - hi.astype(F32)
    mid = r1.astype(BF16)
    lo = (r1 - mid.astype(F32)).astype(BF16)
    return hi, mid, lo


def _ada_kernel(c_ref, w_ref, b_ref, o_ref):
    c = c_ref[...]
    o_ref[...] = _dot(c * jax.nn.sigmoid(c), w_ref[...]) + b_ref[...]


def _ada(c, w, b):
    bsz, d = c.shape
    n = w.shape[1]
    return pl.pallas_call(
        _ada_kernel,
        grid=(n // d,),
        in_specs=[pl.BlockSpec((bsz, d), lambda j: (0, 0)),
                  pl.BlockSpec((d, d), lambda j: (0, j)),
                  pl.BlockSpec((1, d), lambda j: (0, j))],
        out_specs=pl.BlockSpec((bsz, d), lambda j: (0, j)),
        out_shape=jax.ShapeDtypeStruct((bsz, n), F32),
        compiler_params=_params("arbitrary"),
        name="ada",
    )(c, w, b.reshape(1, n))


def _decay_tile(lf, k_fox, k_sb, sel_ref, ind_ref, carry_ref, kpre_ref, kaug_ref, stats_ref, rows, tile, n_heads):
    tk = lf.shape[0]
    lane = lax.broadcasted_iota(jnp.int32, (tk, LANES), 1)
    lf = jnp.where(lane < n_heads, lf, 0.0)
    row = lax.broadcasted_iota(jnp.int32, (tk, tk), 0)
    col = lax.broadcasted_iota(jnp.int32, (tk, tk), 1)
    tri = (col <= row).astype(BF16)
    hi, mid, lo = _split3(lf)
    f_run = carry_ref[...] + (_dot(tri, hi) + _dot(tri, mid) + _dot(tri, lo))
    carry_ref[...] = f_run[tk - 1:tk, :]
    ghi, gmid, glo = _split3(-LOG2E * f_run)
    packed = (ghi.astype(F32) + pltpu.roll(gmid.astype(F32), n_heads, 1)
              + pltpu.roll(glo.astype(F32), 2 * n_heads, 1)).astype(BF16)
    placed = _dot(packed, sel_ref[...])
    for h in range(n_heads):
        k_pair = k_fox[:, (h // 2) * LANES:(h // 2 + 1) * LANES]
        own = (lane < HEAD_DIM) if h % 2 == 0 else (lane >= HEAD_DIM)
        kaug_ref[0, h, rows, :] = jnp.where(own, k_pair, placed[:, h * LANES:(h + 1) * LANES].astype(BF16))

    def head_norm_bound(k):
        k32 = k.astype(F32)
        sq = _dot((k32 * k32).astype(BF16), ind_ref[...])
        return jnp.sqrt(jnp.max(sq, axis=0, keepdims=True) * NORM_SLACK)

    kpre_f = jnp.maximum(kpre_ref[0:1, :], head_norm_bound(k_fox))
    kpre_s = jnp.maximum(kpre_ref[1:2, :], head_norm_bound(k_sb))
    kpre_ref[0:1, :] = kpre_f
    kpre_ref[1:2, :] = kpre_s
    g_end = -LOG2E * f_run[tk - 1:tk, :]
    lane1 = lax.broadcasted_iota(jnp.int32, (1, LANES), 1)

    def spread(v, h):
        return jnp.broadcast_to(jnp.sum(jnp.where(lane1 == h, v, 0.0), axis=1, keepdims=True), (1, ATT_COLS))

    for p in range(n_heads // 2):
        srows = [spread(v, 2 * p + hh) for v in (kpre_f, g_end, kpre_s) for hh in range(2)]
        srows += [jnp.zeros((1, ATT_COLS), F32)] * (stats_ref.shape[3] - len(srows))
        stats_ref[0, p, tile] = jnp.concatenate(srows, axis=0)


def _inproj_kernel(x_ref, mod_ref, g_ref, wn_ref, wvt_ref, wg_ref, bg_ref, sel_ref, ind_ref,
                   qk_ref, vt_ref, kaug_ref, stats_ref, carry_ref, kpre_ref, *, n_heads):
    @pl.when(pl.program_id(1) == 0)
    def _():
        carry_ref[...] = jnp.zeros_like(carry_ref)
        kpre_ref[...] = jnp.zeros_like(kpre_ref)

    shift = mod_ref[0, 0:1, :]
    scale = mod_ref[0, 1:2, :]
    h = (_rms_rows(x_ref[0]) * g_ref[...] * (1.0 + scale) + shift).astype(BF16)
    qk = _dot(h, wn_ref[...]).astype(BF16)
    qk_ref[0] = qk
    logit = _dot(h, wg_ref[...]) + bg_ref[...]
    log_f = -_softplus(-logit)
    d_grp = n_heads * HEAD_DIM
    tk = vt_ref.shape[3]
    for tile in range(vt_ref.shape[1]):
        rows = slice(tile * tk, (tile + 1) * tk)
        vt_ref[0, tile] = _dot_nt(wvt_ref[...], h[rows, :]).astype(BF16)
        _decay_tile(log_f[rows, :], qk[rows, d_grp:2 * d_grp], qk[rows, 3 * d_grp:4 * d_grp], sel_ref, ind_ref,
                    carry_ref, kpre_ref, kaug_ref, stats_ref, rows, tile, n_heads)


def _head_indicator(n_heads):
    ind = np.zeros((n_heads * HEAD_DIM, LANES), np.float32)
    ind[np.arange(n_heads * HEAD_DIM), np.arange(n_heads * HEAD_DIM) // HEAD_DIM] = 1.0
    return jnp.asarray(ind, BF16)


def _decay_select_matrix(n_heads):
    sel = np.zeros((LANES, n_heads * LANES), np.float32)
    for h in range(n_heads):
        base = h * LANES + (HEAD_DIM if h % 2 == 0 else 0)
        for term in range(3):
            sel[term * n_heads + h, base + term] = 1.0
    return jnp.asarray(sel, BF16)


def _inproj(x, mod, g, w_nat, w_vt, w_gate, b_gate, n_heads):
    bsz, s, d = x.shape
    tm, tk = PROJ_ROWS, ATT_K
    n_nat, n_v = w_nat.shape[1], w_vt.shape[0]
    const = lambda b, i: (0, 0)
    resident = lambda a: pl.BlockSpec(a.shape, const, pipeline_mode=pl.Buffered(1))
    sel, ind = _decay_select_matrix(n_heads), _head_indicator(n_heads)
    return pl.pallas_call(
        functools.partial(_inproj_kernel, n_heads=n_heads),
        grid=(bsz, s // tm),
        in_specs=[pl.BlockSpec((1, tm, d), lambda b, i: (b, i, 0)),
                  pl.BlockSpec((1, N_MOD, d), lambda b, i: (b, 0, 0)),
                  pl.BlockSpec((1, d), const),
                  resident(w_nat), resident(w_vt), resident(w_gate),
                  pl.BlockSpec((1, LANES), const),
                  resident(sel), resident(ind)],
        out_specs=[pl.BlockSpec((1, tm, n_nat), lambda b, i: (b, i, 0)),
                   pl.BlockSpec((1, tm // tk, n_v, tk), lambda b, i: (b, i, 0, 0)),
                   pl.BlockSpec((1, n_heads, tm, LANES), lambda b, i: (b, 0, i, 0)),
                   pl.BlockSpec((1, n_heads // 2, tm // tk, STATS_ROWS, ATT_COLS), lambda b, i: (b, 0, i, 0, 0))],
        out_shape=[jax.ShapeDtypeStruct((bsz, s, n_nat), BF16),
                   jax.ShapeDtypeStruct((bsz, s // tk, n_v, tk), BF16),
                   jax.ShapeDtypeStruct((bsz, n_heads, s, LANES), BF16),
                   jax.ShapeDtypeStruct((bsz, n_heads // 2, s // tk, STATS_ROWS, ATT_COLS), F32)],
        scratch_shapes=[pltpu.VMEM((1, LANES), F32), pltpu.VMEM((2, LANES), F32)],
        compiler_params=_params("arbitrary", "arbitrary"),
        name="inproj",
    )(x, mod, g, w_nat, w_vt, w_gate, b_gate, sel, ind)


def _lane_queries(q_ref, extra_even, extra_odd, cw):
    out = []
    for pp in range(q_ref.shape[2] // LANES):
        q = q_ref[0, :, pp * LANES:(pp + 1) * LANES].astype(F32) * (HEAD_DIM ** -0.5 * LOG2E)
        lane = lax.broadcasted_iota(jnp.int32, q.shape, 1)
        heads = (jnp.where(lane < HEAD_DIM, q, extra_even(lane)).T.astype(BF16),
                 jnp.where(lane >= HEAD_DIM, q, extra_odd(lane)).T.astype(BF16))
        out += [heads[hh][:, c * cw:(c + 1) * cw] for hh in range(2) for c in range(q.shape[0] // cw)]
    return out


def _visibility(first_key, first_query, bk, cw, strict):
    last_visible_gap = -1 if strict else 0
    if first_key + bk - 1 - first_query <= last_visible_gap:
        return "all"
    if first_key - (first_query + cw - 1) > last_visible_gap:
        return "none"
    gap = (lax.broadcasted_iota(jnp.int32, (bk, cw), 0) - lax.broadcasted_iota(jnp.int32, (bk, cw), 1)
           + (first_key - first_query))
    return gap <= last_visible_gap


def _diag_visibility(u, c, bk, cw, strict):
    return _visibility((1 - u) * bk, c * cw, bk, cw, strict)


def _hidden(visibility):
    return isinstance(visibility, str) and visibility == "none"


def _query_norm_bounds(queries, n_chunks):
    bounds = []
    for li, q in enumerate(queries):
        hh = (li // n_chunks) % 2
        own = q[hh * HEAD_DIM:(hh + 1) * HEAD_DIM, :].astype(F32)
        bounds.append(jnp.sqrt(jnp.sum(own * own, axis=0, keepdims=True) * NORM_SLACK))
    return bounds


def _finish_heads(lanes, g_ref, o_ref):
    n_pairs = o_ref.shape[2] // LANES
    n_chunks = len(lanes) // (2 * n_pairs)
    for pp in range(n_pairs):
        mine = lanes[2 * pp * n_chunks:2 * (pp + 1) * n_chunks]
        outs = [jnp.concatenate(mine[hh * n_chunks:(hh + 1) * n_chunks], axis=1) for hh in range(2)]
        normed = [o * lax.rsqrt(jnp.mean(o * o, axis=0, keepdims=True) + EPS) for o in outs]
        cols = slice(pp * LANES, (pp + 1) * LANES)
        o_ref[0, :, cols] = (jnp.concatenate(normed, axis=0).T * g_ref[:, cols]).astype(o_ref.dtype)


def _fox_kernel(q_ref, k_ref, vt_ref, g_ref, stats_ref, o_ref, s_buf, cmax_buf, p_buf, acc_buf):
    qi = pl.program_id(2)
    bk = vt_ref.shape[3]
    n_lanes, cw = acc_buf.shape[0], acc_buf.shape[2]
    n_chunks = q_ref.shape[1] // cw
    lane_group = lambda li: (li // (2 * n_chunks), (li // n_chunks) % 2, li % n_chunks)
    n_tiles = 2 * (qi + 1)
    ones3 = lambda lo: (lambda lane: jnp.where((lane >= lo) & (lane < lo + 3), 1.0, 0.0))
    queries = _lane_queries(q_ref, ones3(HEAD_DIM), ones3(0), cw)
    acc_buf[...] = jnp.zeros(acc_buf.shape, F32)
    for li in range(n_lanes):
        if _hidden(_diag_visibility(0, lane_group(li)[2], bk, cw, strict=False)):
            p_buf[0, li] = jnp.zeros((bk, cw), BF16)

    def step(t, slot, carry, score="below", softmax="below", value=True):
        new = []
        for li in range(n_lanes):
            pp, hh, c = lane_group(li)
            see = lambda u: "all" if u == "below" else _diag_visibility(u, c, bk, cw, strict=False)
            if score is not None and not _hidden(see(score)):
                start = pl.multiple_of((n_tiles - 2 - t) * bk, bk)
                s_new = _dot(k_ref[0, 2 * pp + hh, pl.ds(start, bk), :], queries[li])
                if not isinstance(see(score), str):
                    s_new = jnp.where(see(score), s_new, MASKED)
                s_buf[1 - slot, li] = s_new
                cmax_buf[1 - slot, li] = jnp.max(s_new, axis=0, keepdims=True)
            pv = None
            if value:
                vt = vt_ref[0, n_tiles - t, pl.ds(pp * LANES + hh * HEAD_DIM, HEAD_DIM), :]
                pv = _dot(vt, p_buf[1 - slot, li])
            m, l = carry[li]
            if softmax is not None and not _hidden(see(softmax)):
                m_new = jnp.maximum(m, cmax_buf[slot, li])
                alpha = jnp.exp2(m - m_new)
                p = jnp.exp2(s_buf[slot, li] - m_new)
                p_buf[slot, li] = p.astype(BF16)
                m, l = m_new, alpha * l + jnp.sum(p, axis=0, keepdims=True)
                acc_buf[li] = alpha * (acc_buf[li] if pv is None else acc_buf[li] + pv)
            elif pv is not None:
                acc_buf[li] += pv
            new.append((m, l))
        return tuple(new)

    def step_pair(i, carry):
        t = 2 * i + 1
        return step(t + 1, 0, step(t, 1, carry))

    q_norm = _query_norm_bounds(queries, n_chunks)

    def later_tiles_matter(i, carry):
        j_rest = jnp.maximum(n_tiles - 5 - 2 * i, 0)
        worst = None
        for li in range(n_lanes):
            pp, hh, _ = lane_group(li)
            bound = (q_norm[li] * stats_ref[0, pp, j_rest, hh:hh + 1, :]
                     + stats_ref[0, pp, j_rest, 2 + hh:3 + hh, :] - carry[li][0])
            worst = bound if worst is None else jnp.maximum(worst, bound)
        return jnp.max(worst) >= -PRUNE_LOG2

    def pair_and_check(state):
        i, _, carry = state
        carry = step_pair(i, carry)
        return i + 1, later_tiles_matter(i, carry), carry

    carry = tuple((jnp.full((1, cw), M_INIT, F32), jnp.zeros((1, cw), F32)) for _ in range(n_lanes))
    carry = step(-1, 1, carry, score=0, softmax=None, value=False)
    carry = step(0, 0, carry, score=1, softmax=0, value=False)
    n_pairs, _, carry = lax.while_loop(lambda st: (st[0] < qi) & st[1], pair_and_check,
                                       (jnp.int32(0), jnp.bool_(True), carry))
    carry = step(2 * n_pairs + 1, 1, carry, score=None)
    carry = step(2 * n_pairs + 2, 0, carry, score=None, softmax=None)
    _finish_heads([acc_buf[li] / carry[li][1] for li in range(n_lanes)], g_ref, o_ref)


def _sb_kernel(q_ref, k_ref, vt_ref, g_ref, stats_ref, o_ref, z_buf, sp_buf, e_buf, wrow_buf, acc_buf):
    qi = pl.program_id(2)
    bk = vt_ref.shape[3]
    n_wide = z_buf.shape[0]
    n_lanes, cw = acc_buf.shape[0], acc_buf.shape[2]
    n_chunks = q_ref.shape[1] // cw
    lane_group = lambda li: (li // (2 * n_chunks), (li // n_chunks) % 2, li % n_chunks)
    n_tiles = 2 * (qi + 1)
    zero = lambda lane: 0.0
    queries = _lane_queries(q_ref, zero, zero, cw)
    q_norm = _query_norm_bounds(queries, n_chunks)
    suffix = (lax.broadcasted_iota(jnp.int32, (bk, bk), 1)
              >= lax.broadcasted_iota(jnp.int32, (bk, bk), 0)).astype(BF16)
    softplus2 = lambda z: jnp.maximum(z, jnp.log2(1.0 + jnp.exp2(jnp.minimum(z, EXP2_MAX))))
    keys = lambda j, pp: k_ref[0, pl.ds(pl.multiple_of(j * bk, bk), bk), pp * LANES:(pp + 1) * LANES]
    values = lambda j, pp, hh: vt_ref[0, j, pl.ds(pp * LANES + hh * HEAD_DIM, HEAD_DIM), :]

    sees = lambda u, li: _diag_visibility(u, lane_group(li)[2], bk, cw, strict=True) if u < 2 else "all"
    live = [(u, li) for u in range(n_wide) for li in range(n_lanes) if not _hidden(sees(u, li))]
    later = [jnp.zeros((1, cw), F32)] * n_lanes
    acc_buf[...] = jnp.zeros(acc_buf.shape, F32)

    def wide_score(u, li):
        z = _dot(keys(jnp.maximum(n_tiles - 1 - u, 0), lane_group(li)[0]), queries[li])
        if not isinstance(sees(u, li), str):
            z = jnp.where(sees(u, li), z, MASKED)
        if u >= 2:
            z = jnp.where(u < n_tiles, z, MASKED)
        z_buf[u, li] = z

    def wide_softplus(u, li):
        sp_buf[u, li] = softplus2(z_buf[u, li]).astype(BF16)

    col_sums = {}

    def wide_cumsum(u, li):
        within = _dot(suffix, sp_buf[u, li])
        col_sums[u, li] = within[0:1, :]
        z_buf[u, li] = z_buf[u, li] - within

    def wide_weight(u, li):
        pp, hh, _ = lane_group(li)
        a = jnp.exp2(z_buf[u, li] - later[li])
        acc_buf[li] += _dot(values(jnp.maximum(n_tiles - 1 - u, 0), pp, hh), a.astype(BF16))
        later[li] = later[li] + col_sums[u, li]

    stages = (wide_score, wide_softplus, wide_cumsum, wide_weight)
    for pos in range(len(live) + len(stages) - 1):
        for lag, stage in enumerate(stages):
            if 0 <= pos - lag < len(live):
                stage(*live[pos - lag])

    n_rest = n_tiles - n_wide

    def rest_matters(first_unscored, mass):
        j_rest = jnp.maximum(n_rest - 1 - first_unscored, 0)
        worst = None
        for li in range(n_lanes):
            pp, hh, _ = lane_group(li)
            bound = q_norm[li] * stats_ref[0, pp, j_rest, 4 + hh:5 + hh, :] - mass[li]
            worst = bound if worst is None else jnp.maximum(worst, bound)
        return jnp.max(worst) >= -PRUNE_LOG2

    def step(t, slot, later, score=True, softplus=True, cumsum=True, weight=True):
        new_later = []
        for li in range(n_lanes):
            pp, hh, _ = lane_group(li)
            if cumsum:
                within = _dot(suffix, sp_buf[1 - slot, li])
                e_buf[1 - slot, li] = z_buf[1 - slot, li] - within
                wrow_buf[1 - slot, li] = within[0:1, :]
            if score:
                z_buf[1 - slot, li] = _dot(keys(n_rest - 1 - (t + 3), pp), queries[li])
            if weight:
                a = jnp.exp2(e_buf[slot, li] - later[li])
                acc_buf[li] += _dot(values(n_rest - 1 - jnp.maximum(t, 0), pp, hh), a.astype(BF16))
                new_later.append(later[li] + wrow_buf[slot, li])
            else:
                new_later.append(later[li])
            if softplus:
                sp_buf[slot, li] = softplus2(z_buf[slot, li]).astype(BF16)
        return tuple(new_later)

    def pair_and_check(state):
        i, _, later = state
        t = 2 * i - 1
        later = step(t + 1, 0, step(t, 1, later))
        mass = [later[li] + wrow_buf[1, li] for li in range(n_lanes)]
        return i + 1, rest_matters(2 * i + 4, mass), later

    @pl.when((n_rest > 0) & rest_matters(0, later))
    def _():
        e_buf[1] = jnp.full(e_buf.shape[1:], MASKED, F32)
        wrow_buf[1] = jnp.zeros(wrow_buf.shape[1:], F32)
        mass = step(-3, 1, tuple(later), softplus=False, cumsum=False, weight=False)
        mass = step(-2, 0, mass, cumsum=False, weight=False)
        n_pairs, _, mass = lax.while_loop(lambda st: (2 * st[0] + 2 < n_rest) & st[1], pair_and_check,
                                          (jnp.int32(0), jnp.bool_(True), mass))
        mass = step(2 * n_pairs - 1, 1, mass, score=False)
        mass = step(2 * n_pairs, 0, mass, score=False, softplus=False)
        step(2 * n_pairs + 1, 1, mass, score=False, softplus=False, cumsum=False)

    _finish_heads([acc_buf[li] for li in range(n_lanes)], g_ref, o_ref)


def _attention(body, name, pairs, scratch, qk, k_arr, k_spec, vt, g, stats, q_block0, vt_block0, n_heads):
    bsz, s, _ = qk.shape
    bq, bk, width = ATT_Q, ATT_K, pairs * LANES
    d_grp = n_heads * HEAD_DIM
    assert (n_heads // 2) % pairs == 0
    return pl.pallas_call(
        body,
        grid=(bsz, n_heads // 2 // pairs, s // bq),
        in_specs=[pl.BlockSpec((1, bq, width), lambda b, p, i: (b, i, q_block0 + p)),
                  k_spec,
                  pl.BlockSpec((1, s // bk, width, bk), lambda b, p, i: (b, 0, vt_block0 + p, 0)),
                  pl.BlockSpec((1, width), lambda b, p, i: (0, p)),
                  pl.BlockSpec((1, pairs) + stats.shape[2:], lambda b, p, i: (b, p, 0, 0, 0))],
        out_specs=pl.BlockSpec((1, bq, width), lambda b, p, i: (b, i, p)),
        out_shape=jax.ShapeDtypeStruct((bsz, s, d_grp), BF16),
        scratch_shapes=scratch,
        compiler_params=_params("arbitrary", "arbitrary", "arbitrary"),
        name=name,
    )(qk, k_arr, vt, g.reshape(1, d_grp), stats)


def _lane_groups(pairs):
    return pairs * 2 * (ATT_Q // ATT_COLS)


def _fox_scratch(pairs):
    n = _lane_groups(pairs)
    return [pltpu.VMEM((2, n, ATT_K, ATT_COLS), F32), pltpu.VMEM((2, n, 1, ATT_COLS), F32),
            pltpu.VMEM((2, n, ATT_K, ATT_COLS), BF16), pltpu.VMEM((n, HEAD_DIM, ATT_COLS), F32)]


def _sb_scratch(pairs):
    n = _lane_groups(pairs)
    return [pltpu.VMEM((SB_WIDE_TILES, n, ATT_K, ATT_COLS), F32), pltpu.VMEM((SB_WIDE_TILES, n, ATT_K, ATT_COLS), BF16),
            pltpu.VMEM((2, n, ATT_K, ATT_COLS), F32), pltpu.VMEM((2, n, 1, ATT_COLS), F32),
            pltpu.VMEM((n, HEAD_DIM, ATT_COLS), F32)]


def _mixer_kernel(x_ref, xh_ref, mf_ref, mfh_ref, ms_ref, msh_ref, mod_ref, wo_ref, gm_ref, wu_ref, cw_ref,
                  cb_ref, wd_ref, gf_ref, o_ref, u_buf, acc_ref, x1_buf, *, final_norm):
    i = pl.program_id(1)
    tm = x_ref.shape[1]
    n_chunks = wd_ref.shape[0]
    mix = jnp.concatenate([jnp.concatenate([mfh_ref[0], msh_ref[0]], axis=-1),
                           jnp.concatenate([mf_ref[0], ms_ref[0]], axis=-1)], axis=0)
    x_ext = jnp.concatenate([xh_ref[0], x_ref[0]], axis=0)
    x1_ext = x_ext + mod_ref[0, 2:3, :] * _dot(mix, wo_ref[...])
    shift = mod_ref[0, 3:4, :]
    scale = mod_ref[0, 4:5, :]
    h_ext = _rms_rows(x1_ext) * gm_ref[...] * (1.0 + scale) + shift
    row = lax.broadcasted_iota(jnp.int32, h_ext.shape, 0)
    hx = jnp.where((row >= BF16_SUBLANES) | (i > 0), h_ext, 0.0).astype(BF16)
    x1_buf[...] = x1_ext[BF16_SUBLANES:, :]
    acc_ref[...] = jnp.zeros_like(acc_ref)

    def project_up(c, slot):
        for br in range(2):
            u_buf[slot, br] = _dot(hx, wu_ref[br, c])

    def mix_down(c, slot):
        branches = []
        for br in range(2):
            out = cb_ref[br, c]
            for tap in range(CONV_WIDTH):
                first = BF16_SUBLANES - (CONV_WIDTH - 1 - tap)
                out = out + cw_ref[br, c, tap:tap + 1, :] * u_buf[slot, br, pl.ds(first, tm), :]
            branches.append(out)
        u_gate, u_val = branches
        acc_ref[...] += _dot((u_gate * jax.nn.sigmoid(u_gate) * u_val).astype(BF16), wd_ref[c])

    def chunk_pair(j, _):
        c = 2 * j
        project_up(c + 1, 1)
        mix_down(c, 0)
        project_up(c + 2, 0)
        mix_down(c + 1, 1)
        return 0

    project_up(0, 0)
    lax.fori_loop(0, (n_chunks - 1) // 2, chunk_pair, 0)
    mix_down(n_chunks - 1, 0)
    x2 = x1_buf[...] + mod_ref[0, 5:6, :] * acc_ref[...]
    o_ref[0] = _rms_rows(x2) * gf_ref[...] if final_norm else x2


def _mixer(x, mix_f, mix_s, mod, w_out, g_mlp, w_up, conv_w, conv_b, w_down, g_final, final_norm):
    bsz, s, d = x.shape
    tm = OUT_ROWS
    n_chunks, tf = w_down.shape[0], w_down.shape[1]
    assert n_chunks % 2 == 1
    halo_blocks = tm // BF16_SUBLANES
    row = lambda b, i: (b, i, 0)
    halo = lambda b, i: (b, jnp.maximum(i * halo_blocks - 1, 0), 0)
    tile_and_halo = lambda a: [pl.BlockSpec((1, tm, a.shape[2]), row),
                               pl.BlockSpec((1, BF16_SUBLANES, a.shape[2]), halo)]
    resident = lambda a: pl.BlockSpec(a.shape, lambda b, i: (0,) * a.ndim, pipeline_mode=pl.Buffered(1))
    return pl.pallas_call(
        functools.partial(_mixer_kernel, final_norm=final_norm),
        grid=(bsz, s // tm),
        in_specs=tile_and_halo(x) + tile_and_halo(mix_f) + tile_and_halo(mix_s)
                 + [pl.BlockSpec((1, N_MOD, d), lambda b, i: (b, 0, 0)), resident(w_out),
                    pl.BlockSpec((1, d), lambda b, i: (0, 0)),
                    resident(w_up), resident(conv_w), resident(conv_b), resident(w_down),
                    pl.BlockSpec((1, d), lambda b, i: (0, 0))],
        out_specs=pl.BlockSpec((1, tm, d), row),
        out_shape=jax.ShapeDtypeStruct((bsz, s, d), F32),
        scratch_shapes=[pltpu.VMEM((2, 2, tm + BF16_SUBLANES, tf), F32), pltpu.VMEM((tm, d), F32),
                        pltpu.VMEM((tm, d), F32)],
        compiler_params=_params("arbitrary", "arbitrary"),
        name="mixer",
    )(x, x, mix_f, mix_f, mix_s, mix_s, mod, w_out, g_mlp, w_up, conv_w, conv_b, w_down, g_final)


def _chunk_columns(a, d_ff, n_chunks, tf):
    halves = jnp.stack([a[:, :d_ff], a[:, d_ff:]])
    halves = jnp.pad(halves, ((0, 0), (0, 0), (0, n_chunks * tf - d_ff)))
    return halves.reshape(2, a.shape[0], n_chunks, tf).transpose(0, 2, 1, 3)


def _pad_cols(a, n):
    return jnp.pad(a, ((0, 0), (0, n - a.shape[1])))


def kernel(x, c, w_ada, b_ada, g_attn, w_in, b_fgate, g_out_fox, g_out_sb, w_out,
           g_mlp, w_up, conv_w, conv_b, w_down, g_final):
    depth, d, _ = w_ada.shape
    n_fox = b_fgate.shape[1]
    d_fox = n_fox * HEAD_DIM
    d_sb = g_out_sb.shape[1]
    n_sb = d_sb // HEAD_DIM
    d_ff = w_down.shape[1]
    d_ff_pad = -(-d_ff // FF_CHUNK) * FF_CHUNK
    assert n_fox % 2 == 0 and n_sb == n_fox and 3 * n_fox <= LANES
    assert x.shape[1] % OUT_ROWS == 0 and x.shape[1] % ATT_Q == 0 and ATT_Q == 2 * ATT_K
    o_kf, o_vf, o_qs, o_ks, o_vs, o_gate = (d_fox, 2 * d_fox, 3 * d_fox, 3 * d_fox + d_sb,
                                             3 * d_fox + 2 * d_sb, 3 * d_fox + 3 * d_sb)

    for l in range(depth):
        mod = _ada(c, w_ada[l], b_ada[l]).reshape(-1, N_MOD, d)
        w = w_in[l]
        w_nat = jnp.concatenate([w[:, :o_vf], w[:, o_qs:o_vs]], axis=1).astype(BF16)
        w_vt = jnp.concatenate([w[:, o_vf:o_qs], w[:, o_vs:o_gate]], axis=1).T.astype(BF16)
        w_gate = _pad_cols(w[:, o_gate:], LANES).astype(BF16)
        b_gate = _pad_cols(b_fgate[l].reshape(1, n_fox), LANES)
        qk, vt, k_aug, stats = _inproj(x, mod, g_attn[l].reshape(1, d), w_nat, w_vt, w_gate, b_gate, n_fox)

        steps_f, steps_s = n_fox // 2 // FOX_PAIRS, n_sb // 2 // SB_PAIRS
        fox_k_spec = pl.BlockSpec((1, 2 * FOX_PAIRS, x.shape[1], LANES), lambda b, p, i: (b, p, 0, 0))
        mix_f = _attention(_fox_kernel, "fox", FOX_PAIRS, _fox_scratch(FOX_PAIRS), qk, k_aug, fox_k_spec, vt,
                           g_out_fox[l], stats, q_block0=0, vt_block0=0, n_heads=n_fox)
        sb_k_spec = pl.BlockSpec((1, x.shape[1], SB_PAIRS * LANES), lambda b, p, i: (b, 0, 3 * steps_s + p))
        mix_s = _attention(_sb_kernel, "sb", SB_PAIRS, _sb_scratch(SB_PAIRS), qk, qk, sb_k_spec, vt,
                           g_out_sb[l], stats, q_block0=2 * steps_s, vt_block0=steps_s, n_heads=n_sb)

        n_ff = d_ff_pad // FF_CHUNK
        x = _mixer(x, mix_f, mix_s, mod, w_out[l].astype(BF16), g_mlp[l].reshape(1, d),
                   _chunk_columns(w_up[l], d_ff, n_ff, FF_CHUNK).astype(BF16),
                   _chunk_columns(conv_w[l], d_ff, n_ff, FF_CHUNK),
                   _chunk_columns(conv_b[l].reshape(1, -1), d_ff, n_ff, FF_CHUNK),
                   jnp.pad(w_down[l], ((0, d_ff_pad - d_ff), (0, 0))).astype(BF16).reshape(n_ff, FF_CHUNK, d),
                   g_final.reshape(1, d), final_norm=(l == depth - 1))
    return x
```

```python
import functools

import numpy as np
import jax
import jax.numpy as jnp
from jax import lax
from jax.experimental import pallas as pl
from jax.experimental.pallas import tpu as pltpu

HEAD_DIM = 64
N_MOD = 6
CONV_WIDTH = 3
EPS = 1e-6

LANES = 128
BF16_SUBLANES = 16
VMEM_LIMIT_BYTES = 48 * 1024 * 1024

ATT_Q = 512
ATT_K = 256
ATT_COLS = 256
FOX_PAIRS = 1
SB_PAIRS = 1
SB_WIDE_TILES = 4
SB_MORE_TILES = 2
PROJ_ROWS = 2 * ATT_K
LOG2E = 1.4426950408889634
MASKED = -1e30
M_INIT = -1e29
EXP2_MAX = 126.0
PRUNE_LOG2 = 160.0
NORM_SLACK = 1.02
STATS_ROWS = 8
OUT_ROWS = 512
FF_CHUNK = 256

F32 = jnp.float32
BF16 = jnp.bfloat16
NT_DIMS = (((1,), (1,)), ((), ()))


def _dot(a, b):
    return jnp.dot(a, b, preferred_element_type=F32)


def _dot_nt(a, b):
    return lax.dot_general(a, b, NT_DIMS, preferred_element_type=F32)


def _params(*sem):
    return pltpu.CompilerParams(dimension_semantics=sem, vmem_limit_bytes=VMEM_LIMIT_BYTES)


def _rms_rows(x):
    return x * lax.rsqrt(jnp.mean(x * x, axis=-1, keepdims=True) + EPS)


def _softplus(z):
    return jnp.maximum(z, 0.0) + jnp.log(1.0 + jnp.exp(-jnp.abs(z)))


def _split3(x):
    hi = x.astype(BF16)
    r1 = x - hi.astype(F32)
    mid = r1.astype(BF16)
    lo = (r1 - mid.astype(F32)).astype(BF16)
    return hi, mid, lo


def _ada_kernel(c_ref, w_ref, b_ref, o_ref):
    c = c_ref[...]
    o_ref[...] = _dot(c * jax.nn.sigmoid(c), w_ref[...]) + b_ref[...]


def _ada(c, w, b):
    bsz, d = c.shape
    n = w.shape[1]
    return pl.pallas_call(
        _ada_kernel,
        grid=(n // d,),
        in_specs=[pl.BlockSpec((bsz, d), lambda j: (0, 0)),
                  pl.BlockSpec((d, d), lambda j: (0, j)),
                  pl.BlockSpec((1, d), lambda j: (0, j))],
        out_specs=pl.BlockSpec((bsz, d), lambda j: (0, j)),
        out_shape=jax.ShapeDtypeStruct((bsz, n), F32),
        compiler_params=_params("arbitrary"),
        name="ada",
    )(c, w, b.reshape(1, n))


def _decay_tile(lf, k_fox, k_sb, sel_ref, ind_ref, carry_ref, kpre_ref, kaug_ref, stats_ref, rows, tile, n_heads):
    tk = lf.shape[0]
    lane = lax.broadcasted_iota(jnp.int32, (tk, LANES), 1)
    lf = jnp.where(lane < n_heads, lf, 0.0)
    row = lax.broadcasted_iota(jnp.int32, (tk, tk), 0)
    col = lax.broadcasted_iota(jnp.int32, (tk, tk), 1)
    tri = (col <= row).astype(BF16)
    hi, mid, lo = _split3(lf)
    f_run = carry_ref[...] + (_dot(tri, hi) + _dot(tri, mid) + _dot(tri, lo))
    carry_ref[...] = f_run[tk - 1:tk, :]
    ghi, gmid, glo = _split3(-LOG2E * f_run)
    packed = (ghi.astype(F32) + pltpu.roll(gmid.astype(F32), n_heads, 1)
              + pltpu.roll(glo.astype(F32), 2 * n_heads, 1)).astype(BF16)
    placed = _dot(packed, sel_ref[...])
    for h in range(n_heads):
        k_pair = k_fox[:, (h // 2) * LANES:(h // 2 + 1) * LANES]
        own = (lane < HEAD_DIM) if h % 2 == 0 else (lane >= HEAD_DIM)
        kaug_ref[0, h, rows, :] = jnp.where(own, k_pair, placed[:, h * LANES:(h + 1) * LANES].astype(BF16))

    def head_norm_bound(k):
        k32 = k.astype(F32)
        sq = _dot((k32 * k32).astype(BF16), ind_ref[...])
        return jnp.sqrt(jnp.max(sq, axis=0, keepdims=True) * NORM_SLACK)

    kpre_f = jnp.maximum(kpre_ref[0:1, :], head_norm_bound(k_fox))
    kpre_s = jnp.maximum(kpre_ref[1:2, :], head_norm_bound(k_sb))
    kpre_ref[0:1, :] = kpre_f
    kpre_ref[1:2, :] = kpre_s
    g_end = -LOG2E * f_run[tk - 1:tk, :]
    lane1 = lax.broadcasted_iota(jnp.int32, (1, LANES), 1)

    def spread(v, h):
        return jnp.broadcast_to(jnp.sum(jnp.where(lane1 == h, v, 0.0), axis=1, keepdims=True), (1, ATT_COLS))

    for p in range(n_heads // 2):
        srows = [spread(v, 2 * p + hh) for v in (kpre_f, g_end, kpre_s) for hh in range(2)]
        srows += [jnp.zeros((1, ATT_COLS), F32)] * (stats_ref.shape[3] - len(srows))
        stats_ref[0, p, tile] = jnp.concatenate(srows, axis=0)


def _inproj_kernel(x_ref, mod_ref, g_ref, wn_ref, wvt_ref, wg_ref, bg_ref, sel_ref, ind_ref,
                   qk_ref, vt_ref, kaug_ref, stats_ref, carry_ref, kpre_ref, *, n_heads):
    @pl.when(pl.program_id(1) == 0)
    def _():
        carry_ref[...] = jnp.zeros_like(carry_ref)
        kpre_ref[...] = jnp.zeros_like(kpre_ref)

    shift = mod_ref[0, 0:1, :]
    scale = mod_ref[0, 1:2, :]
    h = (_rms_rows(x_ref[0]) * g_ref[...] * (1.0 + scale) + shift).astype(BF16)
    qk = _dot(h, wn_ref[...]).astype(BF16)
    qk_ref[0] = qk
    logit = _dot(h, wg_ref[...]) + bg_ref[...]
    log_f = -_softplus(-logit)
    d_grp = n_heads * HEAD_DIM
    tk = vt_ref.shape[3]
    for tile in range(vt_ref.shape[1]):
        rows = slice(tile * tk, (tile + 1) * tk)
        vt_ref[0, tile] = _dot_nt(wvt_ref[...], h[rows, :]).astype(BF16)
        _decay_tile(log_f[rows, :], qk[rows, d_grp:2 * d_grp], qk[rows, 3 * d_grp:4 * d_grp], sel_ref, ind_ref,
                    carry_ref, kpre_ref, kaug_ref, stats_ref, rows, tile, n_heads)


def _head_indicator(n_heads):
    ind = np.zeros((n_heads * HEAD_DIM, LANES), np.float32)
    ind[np.arange(n_heads * HEAD_DIM), np.arange(n_heads * HEAD_DIM) // HEAD_DIM] = 1.0
    return jnp.asarray(ind, BF16)


def _decay_select_matrix(n_heads):
    sel = np.zeros((LANES, n_heads * LANES), np.float32)
    for h in range(n_heads):
        base = h * LANES + (HEAD_DIM if h % 2 == 0 else 0)
        for term in range(3):
            sel[term * n_heads + h, base + term] = 1.0
    return jnp.asarray(sel, BF16)


def _inproj(x, mod, g, w_nat, w_vt, w_gate, b_gate, n_heads):
    bsz, s, d = x.shape
    tm, tk = PROJ_ROWS, ATT_K
    n_nat, n_v = w_nat.shape[1], w_vt.shape[0]
    const = lambda b, i: (0, 0)
    resident = lambda a: pl.BlockSpec(a.shape, const, pipeline_mode=pl.Buffered(1))
    sel, ind = _decay_select_matrix(n_heads), _head_indicator(n_heads)
    return pl.pallas_call(
        functools.partial(_inproj_kernel, n_heads=n_heads),
        grid=(bsz, s // tm),
        in_specs=[pl.BlockSpec((1, tm, d), lambda b, i: (b, i, 0)),
                  pl.BlockSpec((1, N_MOD, d), lambda b, i: (b, 0, 0)),
                  pl.BlockSpec((1, d), const),
                  resident(w_nat), resident(w_vt), resident(w_gate),
                  pl.BlockSpec((1, LANES), const),
                  resident(sel), resident(ind)],
        out_specs=[pl.BlockSpec((1, tm, n_nat), lambda b, i: (b, i, 0)),
                   pl.BlockSpec((1, tm // tk, n_v, tk), lambda b, i: (b, i, 0, 0)),
                   pl.BlockSpec((1, n_heads, tm, LANES), lambda b, i: (b, 0, i, 0)),
                   pl.BlockSpec((1, n_heads // 2, tm // tk, STATS_ROWS, ATT_COLS), lambda b, i: (b, 0, i, 0, 0))],
        out_shape=[jax.ShapeDtypeStruct((bsz, s, n_nat), BF16),
                   jax.ShapeDtypeStruct((bsz, s // tk, n_v, tk), BF16),
                   jax.ShapeDtypeStruct((bsz, n_heads, s, LANES), BF16),
                   jax.ShapeDtypeStruct((bsz, n_heads // 2, s // tk, STATS_ROWS, ATT_COLS), F32)],
        scratch_shapes=[pltpu.VMEM((1, LANES), F32), pltpu.VMEM((2, LANES), F32)],
        compiler_params=_params("arbitrary", "arbitrary"),
        name="inproj",
    )(x, mod, g, w_nat, w_vt, w_gate, b_gate, sel, ind)


def _lane_queries(q_ref, extra_even, extra_odd, cw):
    out = []
    for pp in range(q_ref.shape[2] // LANES):
        q = q_ref[0, :, pp * LANES:(pp + 1) * LANES].astype(F32) * (HEAD_DIM ** -0.5 * LOG2E)
        lane = lax.broadcasted_iota(jnp.int32, q.shape, 1)
        heads = (jnp.where(lane < HEAD_DIM, q, extra_even(lane)).T.astype(BF16),
                 jnp.where(lane >= HEAD_DIM, q, extra_odd(lane)).T.astype(BF16))
        out += [heads[hh][:, c * cw:(c + 1) * cw] for hh in range(2) for c in range(q.shape[0] // cw)]
    return out


def _visibility(first_key, first_query, bk, cw, strict):
    last_visible_gap = -1 if strict else 0
    if first_key + bk - 1 - first_query <= last_visible_gap:
        return "all"
    if first_key - (first_query + cw - 1) > last_visible_gap:
        return "none"
    gap = (lax.broadcasted_iota(jnp.int32, (bk, cw), 0) - lax.broadcasted_iota(jnp.int32, (bk, cw), 1)
           + (first_key - first_query))
    return gap <= last_visible_gap


def _diag_visibility(u, c, bk, cw, strict):
    return _visibility((1 - u) * bk, c * cw, bk, cw, strict)


def _hidden(visibility):
    return isinstance(visibility, str) and visibility == "none"


def _query_norm_bounds(queries, n_chunks):
    bounds = []
    for li, q in enumerate(queries):
        hh = (li // n_chunks) % 2
        own = q[hh * HEAD_DIM:(hh + 1) * HEAD_DIM, :].astype(F32)
        bounds.append(jnp.sqrt(jnp.sum(own * own, axis=0, keepdims=True) * NORM_SLACK))
    return bounds


def _finish_heads(lanes, g_ref, o_ref):
    n_pairs = o_ref.shape[2] // LANES
    n_chunks = len(lanes) // (2 * n_pairs)
    for pp in range(n_pairs):
        mine = lanes[2 * pp * n_chunks:2 * (pp + 1) * n_chunks]
        outs = [jnp.concatenate(mine[hh * n_chunks:(hh + 1) * n_chunks], axis=1) for hh in range(2)]
        normed = [o * lax.rsqrt(jnp.mean(o * o, axis=0, keepdims=True) + EPS) for o in outs]
        cols = slice(pp * LANES, (pp + 1) * LANES)
        o_ref[0, :, cols] = (jnp.concatenate(normed, axis=0).T * g_ref[:, cols]).astype(o_ref.dtype)


def _fox_kernel(q_ref, k_ref, vt_ref, g_ref, stats_ref, o_ref, s_buf, cmax_buf, p_buf, acc_buf):
    qi = pl.program_id(2)
    bk = vt_ref.shape[3]
    n_lanes, cw = acc_buf.shape[0], acc_buf.shape[2]
    n_chunks = q_ref.shape[1] // cw
    lane_group = lambda li: (li // (2 * n_chunks), (li // n_chunks) % 2, li % n_chunks)
    n_tiles = 2 * (qi + 1)
    ones3 = lambda lo: (lambda lane: jnp.where((lane >= lo) & (lane < lo + 3), 1.0, 0.0))
    queries = _lane_queries(q_ref, ones3(HEAD_DIM), ones3(0), cw)
    acc_buf[...] = jnp.zeros(acc_buf.shape, F32)
    for li in range(n_lanes):
        if _hidden(_diag_visibility(0, lane_group(li)[2], bk, cw, strict=False)):
            p_buf[0, li] = jnp.zeros((bk, cw), BF16)

    def step(t, slot, carry, score="below", softmax="below", value=True):
        new = []
        for li in range(n_lanes):
            pp, hh, c = lane_group(li)
            see = lambda u: "all" if u == "below" else _diag_visibility(u, c, bk, cw, strict=False)
            if score is not None and not _hidden(see(score)):
                start = pl.multiple_of((n_tiles - 2 - t) * bk, bk)
                s_new = _dot(k_ref[0, 2 * pp + hh, pl.ds(start, bk), :], queries[li])
                if not isinstance(see(score), str):
                    s_new = jnp.where(see(score), s_new, MASKED)
                s_buf[1 - slot, li] = s_new
                cmax_buf[1 - slot, li] = jnp.max(s_new, axis=0, keepdims=True)
            pv = None
            if value:
                vt = vt_ref[0, n_tiles - t, pl.ds(pp * LANES + hh * HEAD_DIM, HEAD_DIM), :]
                pv = _dot(vt, p_buf[1 - slot, li])
            m, l = carry[li]
            if softmax is not None and not _hidden(see(softmax)):
                m_new = jnp.maximum(m, cmax_buf[slot, li])
                alpha = jnp.exp2(m - m_new)
                p = jnp.exp2(s_buf[slot, li] - m_new)
                p_buf[slot, li] = p.astype(BF16)
                m, l = m_new, alpha * l + jnp.sum(p, axis=0, keepdims=True)
                acc_buf[li] = alpha * (acc_buf[li] if pv is None else acc_buf[li] + pv)
            elif pv is not None:
                acc_buf[li] += pv
            new.append((m, l))
        return tuple(new)

    def step_pair(i, carry):
        t = 2 * i + 1
        return step(t + 1, 0, step(t, 1, carry))

    q_norm = _query_norm_bounds(queries, n_chunks)

    def later_tiles_matter(i, carry):
        j_rest = jnp.maximum(n_tiles - 5 - 2 * i, 0)
        worst = None
        for li in range(n_lanes):
            pp, hh, _ = lane_group(li)
            bound = (q_norm[li] * stats_ref[0, pp, j_rest, hh:hh + 1, :]
                     + stats_ref[0, pp, j_rest, 2 + hh:3 + hh, :] - carry[li][0])
            worst = bound if worst is None else jnp.maximum(worst, bound)
        return jnp.max(worst) >= -PRUNE_LOG2

    def pair_and_check(state):
        i, _, carry = state
        carry = step_pair(i, carry)
        return i + 1, later_tiles_matter(i, carry), carry

    carry = tuple((jnp.full((1, cw), M_INIT, F32), jnp.zeros((1, cw), F32)) for _ in range(n_lanes))
    carry = step(-1, 1, carry, score=0, softmax=None, value=False)
    carry = step(0, 0, carry, score=1, softmax=0, value=False)
    n_pairs, _, carry = lax.while_loop(lambda st: (st[0] < qi) & st[1], pair_and_check,
                                       (jnp.int32(0), jnp.bool_(True), carry))
    carry = step(2 * n_pairs + 1, 1, carry, score=None)
    carry = step(2 * n_pairs + 2, 0, carry, score=None, softmax=None)
    _finish_heads([acc_buf[li] / carry[li][1] for li in range(n_lanes)], g_ref, o_ref)


def _sb_kernel(q_ref, k_ref, vt_ref, g_ref, stats_ref, o_ref, z_buf, sp_buf, e_buf, wrow_buf, later_buf, acc_buf):
    qi = pl.program_id(2)
    bk = vt_ref.shape[3]
    n_wide, n_first = z_buf.shape[0], SB_WIDE_TILES
    n_lanes, cw = acc_buf.shape[0], acc_buf.shape[2]
    n_chunks = q_ref.shape[1] // cw
    lane_group = lambda li: (li // (2 * n_chunks), (li // n_chunks) % 2, li % n_chunks)
    n_tiles = 2 * (qi + 1)
    zero = lambda lane: 0.0
    queries = _lane_queries(q_ref, zero, zero, cw)
    q_norm = _query_norm_bounds(queries, n_chunks)
    suffix = (lax.broadcasted_iota(jnp.int32, (bk, bk), 1)
              >= lax.broadcasted_iota(jnp.int32, (bk, bk), 0)).astype(BF16)
    softplus2 = lambda z: jnp.maximum(z, jnp.log2(1.0 + jnp.exp2(jnp.minimum(z, EXP2_MAX))))
    keys = lambda j, pp: k_ref[0, pl.ds(pl.multiple_of(j * bk, bk), bk), pp * LANES:(pp + 1) * LANES]
    values = lambda j, pp, hh: vt_ref[0, j, pl.ds(pp * LANES + hh * HEAD_DIM, HEAD_DIM), :]

    sees = lambda u, li: _diag_visibility(u, lane_group(li)[2], bk, cw, strict=True) if u < 2 else "all"

    def wide_block(tiles, later, check_exists):
        live = [(u, li) for u in tiles for li in range(n_lanes) if not _hidden(sees(u, li))]
        later, col_sums = list(later), {}

        def score(u, li):
            z = _dot(keys(jnp.maximum(n_tiles - 1 - u, 0), lane_group(li)[0]), queries[li])
            if not isinstance(sees(u, li), str):
                z = jnp.where(sees(u, li), z, MASKED)
            if check_exists and u >= 2:
                z = jnp.where(u < n_tiles, z, MASKED)
            z_buf[u, li] = z

        def softplus(u, li):
            sp_buf[u, li] = softplus2(z_buf[u, li]).astype(BF16)

        def cumsum(u, li):
            within = _dot(suffix, sp_buf[u, li])
            col_sums[u, li] = within[0:1, :]
            z_buf[u, li] = z_buf[u, li] - within

        def weight(u, li):
            pp, hh, _ = lane_group(li)
            a = jnp.exp2(z_buf[u, li] - later[li])
            acc_buf[li] += _dot(values(jnp.maximum(n_tiles - 1 - u, 0), pp, hh), a.astype(BF16))
            later[li] = later[li] + col_sums[u, li]

        stages = (score, softplus, cumsum, weight)
        for pos in range(len(live) + len(stages) - 1):
            for lag, stage in enumerate(stages):
                if 0 <= pos - lag < len(live):
                    stage(*live[pos - lag])
        return later

    n_rest = n_tiles - n_wide

    def rest_matters(first_unscored, mass):
        j_rest = jnp.maximum(n_rest - 1 - first_unscored, 0)
        worst = None
        for li in range(n_lanes):
            pp, hh, _ = lane_group(li)
            bound = q_norm[li] * stats_ref[0, pp, j_rest, 4 + hh:5 + hh, :] - mass[li]
            worst = bound if worst is None else jnp.maximum(worst, bound)
        return jnp.max(worst) >= -PRUNE_LOG2

    def step(t, slot, later, score=True, softplus=True, cumsum=True, weight=True):
        new_later = []
        for li in range(n_lanes):
            pp, hh, _ = lane_group(li)
            if cumsum:
                within = _dot(suffix, sp_buf[1 - slot, li])
                e_buf[1 - slot, li] = z_buf[1 - slot, li] - within
                wrow_buf[1 - slot, li] = within[0:1, :]
            if score:
                z_buf[1 - slot, li] = _dot(keys(n_rest - 1 - (t + 3), pp), queries[li])
            if weight:
                a = jnp.exp2(e_buf[slot, li] - later[li])
                acc_buf[li] += _dot(values(n_rest - 1 - jnp.maximum(t, 0), pp, hh), a.astype(BF16))
                new_later.append(later[li] + wrow_buf[slot, li])
            else:
                new_later.append(later[li])
            if softplus:
                sp_buf[slot, li] = softplus2(z_buf[slot, li]).astype(BF16)
        return tuple(new_later)

    def pair_and_check(state):
        i, _, later = state
        t = 2 * i - 1
        later = step(t + 1, 0, step(t, 1, later))
        mass = [later[li] + wrow_buf[1, li] for li in range(n_lanes)]
        return i + 1, rest_matters(2 * i + 4, mass), later

    acc_buf[...] = jnp.zeros(acc_buf.shape, F32)
    first = wide_block(range(n_first), [jnp.zeros((1, cw), F32)] * n_lanes, check_exists=True)
    for li in range(n_lanes):
        later_buf[li] = first[li]
    swept = lambda: [later_buf[li] for li in range(n_lanes)]
    more_matters = (n_rest + (n_wide - n_first) > 0) & rest_matters(n_first - n_wide, first)

    @pl.when(more_matters)
    def _():
        more = wide_block(range(n_first, n_wide), first, check_exists=False)
        for li in range(n_lanes):
            later_buf[li] = more[li]

    @pl.when(more_matters & (n_rest > 0) & rest_matters(0, swept()))
    def _():
        later = swept()
        e_buf[1] = jnp.full(e_buf.shape[1:], MASKED, F32)
        wrow_buf[1] = jnp.zeros(wrow_buf.shape[1:], F32)
        mass = step(-3, 1, tuple(later), softplus=False, cumsum=False, weight=False)
        mass = step(-2, 0, mass, cumsum=False, weight=False)
        n_pairs, _, mass = lax.while_loop(lambda st: (2 * st[0] + 2 < n_rest) & st[1], pair_and_check,
                                          (jnp.int32(0), jnp.bool_(True), mass))
        mass = step(2 * n_pairs - 1, 1, mass, score=False)
        mass = step(2 * n_pairs, 0, mass, score=False, softplus=False)
        step(2 * n_pairs + 1, 1, mass, score=False, softplus=False, cumsum=False)

    _finish_heads([acc_buf[li] for li in range(n_lanes)], g_ref, o_ref)


def _attention(body, name, pairs, scratch, qk, k_arr, k_spec, vt, g, stats, q_block0, vt_block0, n_heads):
    bsz, s, _ = qk.shape
    bq, bk, width = ATT_Q, ATT_K, pairs * LANES
    d_grp = n_heads * HEAD_DIM
    assert (n_heads // 2) % pairs == 0
    return pl.pallas_call(
        body,
        grid=(bsz, n_heads // 2 // pairs, s // bq),
        in_specs=[pl.BlockSpec((1, bq, width), lambda b, p, i: (b, i, q_block0 + p)),
                  k_spec,
                  pl.BlockSpec((1, s // bk, width, bk), lambda b, p, i: (b, 0, vt_block0 + p, 0)),
                  pl.BlockSpec((1, width), lambda b, p, i: (0, p)),
                  pl.BlockSpec((1, pairs) + stats.shape[2:], lambda b, p, i: (b, p, 0, 0, 0))],
        out_specs=pl.BlockSpec((1, bq, width), lambda b, p, i: (b, i, p)),
        out_shape=jax.ShapeDtypeStruct((bsz, s, d_grp), BF16),
        scratch_shapes=scratch,
        compiler_params=_params("arbitrary", "arbitrary", "arbitrary"),
        name=name,
    )(qk, k_arr, vt, g.reshape(1, d_grp), stats)


def _lane_groups(pairs):
    return pairs * 2 * (ATT_Q // ATT_COLS)


def _fox_scratch(pairs):
    n = _lane_groups(pairs)
    return [pltpu.VMEM((2, n, ATT_K, ATT_COLS), F32), pltpu.VMEM((2, n, 1, ATT_COLS), F32),
            pltpu.VMEM((2, n, ATT_K, ATT_COLS), BF16), pltpu.VMEM((n, HEAD_DIM, ATT_COLS), F32)]


def _sb_scratch(pairs):
    n = _lane_groups(pairs)
    wide = SB_WIDE_TILES + SB_MORE_TILES
    return [pltpu.VMEM((wide, n, ATT_K, ATT_COLS), F32), pltpu.VMEM((wide, n, ATT_K, ATT_COLS), BF16),
            pltpu.VMEM((2, n, ATT_K, ATT_COLS), F32), pltpu.VMEM((2, n, 1, ATT_COLS), F32),
            pltpu.VMEM((n, 1, ATT_COLS), F32), pltpu.VMEM((n, HEAD_DIM, ATT_COLS), F32)]


def _mixer_kernel(x_ref, xh_ref, mf_ref, mfh_ref, ms_ref, msh_ref, mod_ref, wo_ref, gm_ref, wu_ref, cw_ref,
                  cb_ref, wd_ref, gf_ref, o_ref, u_buf, acc_ref, x1_buf, *, final_norm):
    i = pl.program_id(1)
    tm = x_ref.shape[1]
    n_chunks = wd_ref.shape[0]
    mix = jnp.concatenate([jnp.concatenate([mfh_ref[0], msh_ref[0]], axis=-1),
                           jnp.concatenate([mf_ref[0], ms_ref[0]], axis=-1)], axis=0)
    x_ext = jnp.concatenate([xh_ref[0], x_ref[0]], axis=0)
    x1_ext = x_ext + mod_ref[0, 2:3, :] * _dot(mix, wo_ref[...])
    shift = mod_ref[0, 3:4, :]
    scale = mod_ref[0, 4:5, :]
    h_ext = _rms_rows(x1_ext) * gm_ref[...] * (1.0 + scale) + shift
    row = lax.broadcasted_iota(jnp.int32, h_ext.shape, 0)
    hx = jnp.where((row >= BF16_SUBLANES) | (i > 0), h_ext, 0.0).astype(BF16)
    x1_buf[...] = x1_ext[BF16_SUBLANES:, :]
    acc_ref[...] = jnp.zeros_like(acc_ref)

    def project_up(c, slot):
        for br in range(2):
            u_buf[slot, br] = _dot(hx, wu_ref[br, c])

    def mix_down(c, slot):
        branches = []
        for br in range(2):
            out = cb_ref[br, c]
            for tap in range(CONV_WIDTH):
                first = BF16_SUBLANES - (CONV_WIDTH - 1 - tap)
                out = out + cw_ref[br, c, tap:tap + 1, :] * u_buf[slot, br, pl.ds(first, tm), :]
            branches.append(out)
        u_gate, u_val = branches
        acc_ref[...] += _dot((u_gate * jax.nn.sigmoid(u_gate) * u_val).astype(BF16), wd_ref[c])

    def chunk_pair(j, _):
        c = 2 * j
        project_up(c + 1, 1)
        mix_down(c, 0)
        project_up(c + 2, 0)
        mix_down(c + 1, 1)
        return 0

    project_up(0, 0)
    lax.fori_loop(0, (n_chunks - 1) // 2, chunk_pair, 0)
    mix_down(n_chunks - 1, 0)
    x2 = x1_buf[...] + mod_ref[0, 5:6, :] * acc_ref[...]
    o_ref[0] = _rms_rows(x2) * gf_ref[...] if final_norm else x2


def _mixer(x, mix_f, mix_s, mod, w_out, g_mlp, w_up, conv_w, conv_b, w_down, g_final, final_norm):
    bsz, s, d = x.shape
    tm = OUT_ROWS
    n_chunks, tf = w_down.shape[0], w_down.shape[1]
    assert n_chunks % 2 == 1
    halo_blocks = tm // BF16_SUBLANES
    row = lambda b, i: (b, i, 0)
    halo = lambda b, i: (b, jnp.maximum(i * halo_blocks - 1, 0), 0)
    tile_and_halo = lambda a: [pl.BlockSpec((1, tm, a.shape[2]), row),
                               pl.BlockSpec((1, BF16_SUBLANES, a.shape[2]), halo)]
    resident = lambda a: pl.BlockSpec(a.shape, lambda b, i: (0,) * a.ndim, pipeline_mode=pl.Buffered(1))
    return pl.pallas_call(
        functools.partial(_mixer_kernel, final_norm=final_norm),
        grid=(bsz, s // tm),
        in_specs=tile_and_halo(x) + tile_and_halo(mix_f) + tile_and_halo(mix_s)
                 + [pl.BlockSpec((1, N_MOD, d), lambda b, i: (b, 0, 0)), resident(w_out),
                    pl.BlockSpec((1, d), lambda b, i: (0, 0)),
                    resident(w_up), resident(conv_w), resident(conv_b), resident(w_down),
                    pl.BlockSpec((1, d), lambda b, i: (0, 0))],
        out_specs=pl.BlockSpec((1, tm, d), row),
        out_shape=jax.ShapeDtypeStruct((bsz, s, d), F32),
        scratch_shapes=[pltpu.VMEM((2, 2, tm + BF16_SUBLANES, tf), F32), pltpu.VMEM((tm, d), F32),
                        pltpu.VMEM((tm, d), F32)],
        compiler_params=_params("arbitrary", "arbitrary"),
        name="mixer",
    )(x, x, mix_f, mix_f, mix_s, mix_s, mod, w_out, g_mlp, w_up, conv_w, conv_b, w_down, g_final)


def _chunk_columns(a, d_ff, n_chunks, tf):
    halves = jnp.stack([a[:, :d_ff], a[:, d_ff:]])
    halves = jnp.pad(halves, ((0, 0), (0, 0), (0, n_chunks * tf - d_ff)))
    return halves.reshape(2, a.shape[0], n_chunks, tf).transpose(0, 2, 1, 3)


def _pad_cols(a, n):
    return jnp.pad(a, ((0, 0), (0, n - a.shape[1])))


def kernel(x, c, w_ada, b_ada, g_attn, w_in, b_fgate, g_out_fox, g_out_sb, w_out,
           g_mlp, w_up, conv_w, conv_b, w_down, g_final):
    depth, d, _ = w_ada.shape
    n_fox = b_fgate.shape[1]
    d_fox = n_fox * HEAD_DIM
    d_sb = g_out_sb.shape[1]
    n_sb = d_sb // HEAD_DIM
    d_ff = w_down.shape[1]
    d_ff_pad = -(-d_ff // FF_CHUNK) * FF_CHUNK
    assert n_fox % 2 == 0 and n_sb == n_fox and 3 * n_fox <= LANES
    assert x.shape[1] % OUT_ROWS == 0 and x.shape[1] % ATT_Q == 0 and ATT_Q == 2 * ATT_K
    o_kf, o_vf, o_qs, o_ks, o_vs, o_gate = (d_fox, 2 * d_fox, 3 * d_fox, 3 * d_fox + d_sb,
                                             3 * d_fox + 2 * d_sb, 3 * d_fox + 3 * d_sb)

    for l in range(depth):
        mod = _ada(c, w_ada[l], b_ada[l]).reshape(-1, N_MOD, d)
        w = w_in[l]
        w_nat = jnp.concatenate([w[:, :o_vf], w[:, o_qs:o_vs]], axis=1).astype(BF16)
        w_vt = jnp.concatenate([w[:, o_vf:o_qs], w[:, o_vs:o_gate]], axis=1).T.astype(BF16)
        w_gate = _pad_cols(w[:, o_gate:], LANES).astype(BF16)
        b_gate = _pad_cols(b_fgate[l].reshape(1, n_fox), LANES)
        qk, vt, k_aug, stats = _inproj(x, mod, g_attn[l].reshape(1, d), w_nat, w_vt, w_gate, b_gate, n_fox)

        steps_f, steps_s = n_fox // 2 // FOX_PAIRS, n_sb // 2 // SB_PAIRS
        fox_k_spec = pl.BlockSpec((1, 2 * FOX_PAIRS, x.shape[1], LANES), lambda b, p, i: (b, p, 0, 0))
        mix_f = _attention(_fox_kernel, "fox", FOX_PAIRS, _fox_scratch(FOX_PAIRS), qk, k_aug, fox_k_spec, vt,
                           g_out_fox[l], stats, q_block0=0, vt_block0=0, n_heads=n_fox)
        sb_k_spec = pl.BlockSpec((1, x.shape[1], SB_PAIRS * LANES), lambda b, p, i: (b, 0, 3 * steps_s + p))
        mix_s = _attention(_sb_kernel, "sb", SB_PAIRS, _sb_scratch(SB_PAIRS), qk, qk, sb_k_spec, vt,
                           g_out_sb[l], stats, q_block0=2 * steps_s, vt_block0=steps_s, n_heads=n_sb)

        n_ff = d_ff_pad // FF_CHUNK
        x = _mixer(x, mix_f, mix_s, mod, w_out[l].astype(BF16), g_mlp[l].reshape(1, d),
                   _chunk_columns(w_up[l], d_ff, n_ff, FF_CHUNK).astype(BF16),
                   _chunk_columns(conv_w[l], d_ff, n_ff, FF_CHUNK),
                   _chunk_columns(conv_b[l].reshape(1, -1), d_ff, n_ff, FF_CHUNK),
                   jnp.pad(w_down[l], ((0, d_ff_pad - d_ff), (0, 0))).astype(BF16).reshape(n_ff, FF_CHUNK, d),
                   g_final.reshape(1, d), final_norm=(l == depth - 1))
    return x
```

```python
import functools

import numpy as np
import jax
import jax.numpy as jnp
from jax import lax
from jax.experimental import pallas as pl
from jax.experimental.pallas import tpu as pltpu

HEAD_DIM = 64
N_MOD = 6
CONV_WIDTH = 3
EPS = 1e-6

LANES = 128
BF16_SUBLANES = 16
VMEM_LIMIT_BYTES = 48 * 1024 * 1024

ATT_Q = 512
ATT_K = 256
ATT_COLS = 256
FOX_PAIRS = 1
SB_PAIRS = 2
SB_WIDE_TILES = 4
SB_MORE_TILES = 2
PROJ_ROWS = 2 * ATT_K
LOG2E = 1.4426950408889634
MASKED = -1e30
M_INIT = -1e29
EXP2_MAX = 126.0
PRUNE_LOG2 = 160.0
NORM_SLACK = 1.02
STATS_ROWS = 8
OUT_ROWS = 512
FF_CHUNK = 256

F32 = jnp.float32
BF16 = jnp.bfloat16
NT_DIMS = (((1,), (1,)), ((), ()))


def _dot(a, b):
    return jnp.dot(a, b, preferred_element_type=F32)


def _dot_nt(a, b):
    return lax.dot_general(a, b, NT_DIMS, preferred_element_type=F32)


def _params(*sem):
    return pltpu.CompilerParams(dimension_semantics=sem, vmem_limit_bytes=VMEM_LIMIT_BYTES)


def _rms_rows(x):
    return x * lax.rsqrt(jnp.mean(x * x, axis=-1, keepdims=True) + EPS)


def _softplus(z):
    return jnp.maximum(z, 0.0) + jnp.log(1.0 + jnp.exp(-jnp.abs(z)))


def _split3(x):
    hi = x.astype(BF16)
    r1 = x - hi.astype(F32)
    mid = r1.astype(BF16)
    lo = (r1 - mid.astype(F32)).astype(BF16)
    return hi, mid, lo


def _ada_kernel(c_ref, w_ref, b_ref, o_ref):
    c = c_ref[...]
    o_ref[...] = _dot(c * jax.nn.sigmoid(c), w_ref[...]) + b_ref[...]


def _ada(c, w, b):
    bsz, d = c.shape
    n = w.shape[1]
    return pl.pallas_call(
        _ada_kernel,
        grid=(n // d,),
        in_specs=[pl.BlockSpec((bsz, d), lambda j: (0, 0)),
                  pl.BlockSpec((d, d), lambda j: (0, j)),
                  pl.BlockSpec((1, d), lambda j: (0, j))],
        out_specs=pl.BlockSpec((bsz, d), lambda j: (0, j)),
        out_shape=jax.ShapeDtypeStruct((bsz, n), F32),
        compiler_params=_params("arbitrary"),
        name="ada",
    )(c, w, b.reshape(1, n))


def _decay_tile(lf, k_fox, k_sb, sel_ref, ind_ref, carry_ref, kpre_ref, kaug_ref, stats_ref, rows, tile, n_heads):
    tk = lf.shape[0]
    lane = lax.broadcasted_iota(jnp.int32, (tk, LANES), 1)
    lf = jnp.where(lane < n_heads, lf, 0.0)
    row = lax.broadcasted_iota(jnp.int32, (tk, tk), 0)
    col = lax.broadcasted_iota(jnp.int32, (tk, tk), 1)
    tri = (col <= row).astype(BF16)
    hi, mid, lo = _split3(lf)
    f_run = carry_ref[...] + (_dot(tri, hi) + _dot(tri, mid) + _dot(tri, lo))
    carry_ref[...] = f_run[tk - 1:tk, :]
    ghi, gmid, glo = _split3(-LOG2E * f_run)
    packed = (ghi.astype(F32) + pltpu.roll(gmid.astype(F32), n_heads, 1)
              + pltpu.roll(glo.astype(F32), 2 * n_heads, 1)).astype(BF16)
    placed = _dot(packed, sel_ref[...])
    for h in range(n_heads):
        k_pair = k_fox[:, (h // 2) * LANES:(h // 2 + 1) * LANES]
        own = (lane < HEAD_DIM) if h % 2 == 0 else (lane >= HEAD_DIM)
        kaug_ref[0, h, rows, :] = jnp.where(own, k_pair, placed[:, h * LANES:(h + 1) * LANES].astype(BF16))

    def head_norm_bound(k):
        k32 = k.astype(F32)
        sq = _dot((k32 * k32).astype(BF16), ind_ref[...])
        return jnp.sqrt(jnp.max(sq, axis=0, keepdims=True) * NORM_SLACK)

    kpre_f = jnp.maximum(kpre_ref[0:1, :], head_norm_bound(k_fox))
    kpre_s = jnp.maximum(kpre_ref[1:2, :], head_norm_bound(k_sb))
    kpre_ref[0:1, :] = kpre_f
    kpre_ref[1:2, :] = kpre_s
    g_end = -LOG2E * f_run[tk - 1:tk, :]
    lane1 = lax.broadcasted_iota(jnp.int32, (1, LANES), 1)

    def spread(v, h):
        return jnp.broadcast_to(jnp.sum(jnp.where(lane1 == h, v, 0.0), axis=1, keepdims=True), (1, ATT_COLS))

    for p in range(n_heads // 2):
        srows = [spread(v, 2 * p + hh) for v in (kpre_f, g_end, kpre_s) for hh in range(2)]
        srows += [jnp.zeros((1, ATT_COLS), F32)] * (stats_ref.shape[3] - len(srows))
        stats_ref[0, p, tile] = jnp.concatenate(srows, axis=0)


def _inproj_kernel(x_ref, mod_ref, g_ref, wn_ref, wvt_ref, wg_ref, bg_ref, sel_ref, ind_ref,
                   qk_ref, vt_ref, kaug_ref, stats_ref, carry_ref, kpre_ref, *, n_heads):
    @pl.when(pl.program_id(1) == 0)
    def _():
        carry_ref[...] = jnp.zeros_like(carry_ref)
        kpre_ref[...] = jnp.zeros_like(kpre_ref)

    shift = mod_ref[0, 0:1, :]
    scale = mod_ref[0, 1:2, :]
    h = (_rms_rows(x_ref[0]) * g_ref[...] * (1.0 + scale) + shift).astype(BF16)
    qk = _dot(h, wn_ref[...]).astype(BF16)
    qk_ref[0] = qk
    logit = _dot(h, wg_ref[...]) + bg_ref[...]
    log_f = -_softplus(-logit)
    d_grp = n_heads * HEAD_DIM
    tk = vt_ref.shape[3]
    for tile in range(vt_ref.shape[1]):
        rows = slice(tile * tk, (tile + 1) * tk)
        vt_ref[0, tile] = _dot_nt(wvt_ref[...], h[rows, :]).astype(BF16)
        _decay_tile(log_f[rows, :], qk[rows, d_grp:2 * d_grp], qk[rows, 3 * d_grp:4 * d_grp], sel_ref, ind_ref,
                    carry_ref, kpre_ref, kaug_ref, stats_ref, rows, tile, n_heads)


def _head_indicator(n_heads):
    ind = np.zeros((n_heads * HEAD_DIM, LANES), np.float32)
    ind[np.arange(n_heads * HEAD_DIM), np.arange(n_heads * HEAD_DIM) // HEAD_DIM] = 1.0
    return jnp.asarray(ind, BF16)


def _decay_select_matrix(n_heads):
    sel = np.zeros((LANES, n_heads * LANES), np.float32)
    for h in range(n_heads):
        base = h * LANES + (HEAD_DIM if h % 2 == 0 else 0)
        for term in range(3):
            sel[term * n_heads + h, base + term] = 1.0
    return jnp.asarray(sel, BF16)


def _inproj(x, mod, g, w_nat, w_vt, w_gate, b_gate, n_heads):
    bsz, s, d = x.shape
    tm, tk = PROJ_ROWS, ATT_K
    n_nat, n_v = w_nat.shape[1], w_vt.shape[0]
    const = lambda b, i: (0, 0)
    resident = lambda a: pl.BlockSpec(a.shape, const, pipeline_mode=pl.Buffered(1))
    sel, ind = _decay_select_matrix(n_heads), _head_indicator(n_heads)
    return pl.pallas_call(
        functools.partial(_inproj_kernel, n_heads=n_heads),
        grid=(bsz, s // tm),
        in_specs=[pl.BlockSpec((1, tm, d), lambda b, i: (b, i, 0)),
                  pl.BlockSpec((1, N_MOD, d), lambda b, i: (b, 0, 0)),
                  pl.BlockSpec((1, d), const),
                  resident(w_nat), resident(w_vt), resident(w_gate),
                  pl.BlockSpec((1, LANES), const),
                  resident(sel), resident(ind)],
        out_specs=[pl.BlockSpec((1, tm, n_nat), lambda b, i: (b, i, 0)),
                   pl.BlockSpec((1, tm // tk, n_v, tk), lambda b, i: (b, i, 0, 0)),
                   pl.BlockSpec((1, n_heads, tm, LANES), lambda b, i: (b, 0, i, 0)),
                   pl.BlockSpec((1, n_heads // 2, tm // tk, STATS_ROWS, ATT_COLS), lambda b, i: (b, 0, i, 0, 0))],
        out_shape=[jax.ShapeDtypeStruct((bsz, s, n_nat), BF16),
                   jax.ShapeDtypeStruct((bsz, s // tk, n_v, tk), BF16),
                   jax.ShapeDtypeStruct((bsz, n_heads, s, LANES), BF16),
                   jax.ShapeDtypeStruct((bsz, n_heads // 2, s // tk, STATS_ROWS, ATT_COLS), F32)],
        scratch_shapes=[pltpu.VMEM((1, LANES), F32), pltpu.VMEM((2, LANES), F32)],
        compiler_params=_params("arbitrary", "arbitrary"),
        name="inproj",
    )(x, mod, g, w_nat, w_vt, w_gate, b_gate, sel, ind)


def _lane_queries(q_ref, extra_even, extra_odd, cw):
    out = []
    for pp in range(q_ref.shape[2] // LANES):
        q = q_ref[0, :, pp * LANES:(pp + 1) * LANES].astype(F32) * (HEAD_DIM ** -0.5 * LOG2E)
        lane = lax.broadcasted_iota(jnp.int32, q.shape, 1)
        heads = (jnp.where(lane < HEAD_DIM, q, extra_even(lane)).T.astype(BF16),
                 jnp.where(lane >= HEAD_DIM, q, extra_odd(lane)).T.astype(BF16))
        out += [heads[hh][:, c * cw:(c + 1) * cw] for hh in range(2) for c in range(q.shape[0] // cw)]
    return out


def _visibility(first_key, first_query, bk, cw, strict):
    last_visible_gap = -1 if strict else 0
    if first_key + bk - 1 - first_query <= last_visible_gap:
        return "all"
    if first_key - (first_query + cw - 1) > last_visible_gap:
        return "none"
    gap = (lax.broadcasted_iota(jnp.int32, (bk, cw), 0) - lax.broadcasted_iota(jnp.int32, (bk, cw), 1)
           + (first_key - first_query))
    return gap <= last_visible_gap


def _diag_visibility(u, c, bk, cw, strict):
    return _visibility((1 - u) * bk, c * cw, bk, cw, strict)


def _hidden(visibility):
    return isinstance(visibility, str) and visibility == "none"


def _query_norm_bounds(queries, n_chunks):
    bounds = []
    for li, q in enumerate(queries):
        hh = (li // n_chunks) % 2
        own = q[hh * HEAD_DIM:(hh + 1) * HEAD_DIM, :].astype(F32)
        bounds.append(jnp.sqrt(jnp.sum(own * own, axis=0, keepdims=True) * NORM_SLACK))
    return bounds


def _finish_heads(lanes, g_ref, o_ref):
    n_pairs = o_ref.shape[2] // LANES
    n_chunks = len(lanes) // (2 * n_pairs)
    for pp in range(n_pairs):
        mine = lanes[2 * pp * n_chunks:2 * (pp + 1) * n_chunks]
        outs = [jnp.concatenate(mine[hh * n_chunks:(hh + 1) * n_chunks], axis=1) for hh in range(2)]
        normed = [o * lax.rsqrt(jnp.mean(o * o, axis=0, keepdims=True) + EPS) for o in outs]
        cols = slice(pp * LANES, (pp + 1) * LANES)
        o_ref[0, :, cols] = (jnp.concatenate(normed, axis=0).T * g_ref[:, cols]).astype(o_ref.dtype)


def _fox_kernel(q_ref, k_ref, vt_ref, g_ref, stats_ref, o_ref, s_buf, cmax_buf, p_buf, acc_buf):
    qi = pl.program_id(2)
    bk = vt_ref.shape[3]
    n_lanes, cw = acc_buf.shape[0], acc_buf.shape[2]
    n_chunks = q_ref.shape[1] // cw
    lane_group = lambda li: (li // (2 * n_chunks), (li // n_chunks) % 2, li % n_chunks)
    n_tiles = 2 * (qi + 1)
    ones3 = lambda lo: (lambda lane: jnp.where((lane >= lo) & (lane < lo + 3), 1.0, 0.0))
    queries = _lane_queries(q_ref, ones3(HEAD_DIM), ones3(0), cw)
    acc_buf[...] = jnp.zeros(acc_buf.shape, F32)
    for li in range(n_lanes):
        if _hidden(_diag_visibility(0, lane_group(li)[2], bk, cw, strict=False)):
            p_buf[0, li] = jnp.zeros((bk, cw), BF16)

    def step(t, slot, carry, score="below", softmax="below", value=True):
        new = []
        for li in range(n_lanes):
            pp, hh, c = lane_group(li)
            see = lambda u: "all" if u == "below" else _diag_visibility(u, c, bk, cw, strict=False)
            if score is not None and not _hidden(see(score)):
                start = pl.multiple_of((n_tiles - 2 - t) * bk, bk)
                s_new = _dot(k_ref[0, 2 * pp + hh, pl.ds(start, bk), :], queries[li])
                if not isinstance(see(score), str):
                    s_new = jnp.where(see(score), s_new, MASKED)
                s_buf[1 - slot, li] = s_new
                cmax_buf[1 - slot, li] = jnp.max(s_new, axis=0, keepdims=True)
            pv = None
            if value:
                vt = vt_ref[0, n_tiles - t, pl.ds(pp * LANES + hh * HEAD_DIM, HEAD_DIM), :]
                pv = _dot(vt, p_buf[1 - slot, li])
            m, l = carry[li]
            if softmax is not None and not _hidden(see(softmax)):
                m_new = jnp.maximum(m, cmax_buf[slot, li])
                alpha = jnp.exp2(m - m_new)
                p = jnp.exp2(s_buf[slot, li] - m_new)
                p_buf[slot, li] = p.astype(BF16)
                m, l = m_new, alpha * l + jnp.sum(p, axis=0, keepdims=True)
                acc_buf[li] = alpha * (acc_buf[li] if pv is None else acc_buf[li] + pv)
            elif pv is not None:
                acc_buf[li] += pv
            new.append((m, l))
        return tuple(new)

    def step_pair(i, carry):
        t = 2 * i + 1
        return step(t + 1, 0, step(t, 1, carry))

    q_norm = _query_norm_bounds(queries, n_chunks)

    def later_tiles_matter(i, carry):
        j_rest = jnp.maximum(n_tiles - 5 - 2 * i, 0)
        worst = None
        for li in range(n_lanes):
            pp, hh, _ = lane_group(li)
            bound = (q_norm[li] * stats_ref[0, pp, j_rest, hh:hh + 1, :]
                     + stats_ref[0, pp, j_rest, 2 + hh:3 + hh, :] - carry[li][0])
            worst = bound if worst is None else jnp.maximum(worst, bound)
        return jnp.max(worst) >= -PRUNE_LOG2

    def pair_and_check(state):
        i, _, carry = state
        carry = step_pair(i, carry)
        return i + 1, later_tiles_matter(i, carry), carry

    carry = tuple((jnp.full((1, cw), M_INIT, F32), jnp.zeros((1, cw), F32)) for _ in range(n_lanes))
    carry = step(-1, 1, carry, score=0, softmax=None, value=False)
    carry = step(0, 0, carry, score=1, softmax=0, value=False)
    n_pairs, _, carry = lax.while_loop(lambda st: (st[0] < qi) & st[1], pair_and_check,
                                       (jnp.int32(0), jnp.bool_(True), carry))
    carry = step(2 * n_pairs + 1, 1, carry, score=None)
    carry = step(2 * n_pairs + 2, 0, carry, score=None, softmax=None)
    _finish_heads([acc_buf[li] / carry[li][1] for li in range(n_lanes)], g_ref, o_ref)


def _sb_kernel(q_ref, k_ref, vt_ref, g_ref, stats_ref, o_ref, z_buf, sp_buf, e_buf, wrow_buf, later_buf, acc_buf):
    qi = pl.program_id(2)
    bk = vt_ref.shape[3]
    n_wide, n_first = z_buf.shape[0], SB_WIDE_TILES
    n_lanes, cw = acc_buf.shape[0], acc_buf.shape[2]
    n_chunks = q_ref.shape[1] // cw
    lane_group = lambda li: (li // (2 * n_chunks), (li // n_chunks) % 2, li % n_chunks)
    n_tiles = 2 * (qi + 1)
    zero = lambda lane: 0.0
    queries = _lane_queries(q_ref, zero, zero, cw)
    q_norm = _query_norm_bounds(queries, n_chunks)
    suffix = (lax.broadcasted_iota(jnp.int32, (bk, bk), 1)
              >= lax.broadcasted_iota(jnp.int32, (bk, bk), 0)).astype(BF16)
    softplus2 = lambda z: jnp.maximum(z, jnp.log2(1.0 + jnp.exp2(jnp.minimum(z, EXP2_MAX))))
    keys = lambda j, pp: k_ref[0, pl.ds(pl.multiple_of(j * bk, bk), bk), pp * LANES:(pp + 1) * LANES]
    values = lambda j, pp, hh: vt_ref[0, j, pl.ds(pp * LANES + hh * HEAD_DIM, HEAD_DIM), :]

    sees = lambda u, li: _diag_visibility(u, lane_group(li)[2], bk, cw, strict=True) if u < 2 else "all"

    def wide_block(tiles, later, check_exists):
        live = [(u, li) for u in tiles for li in range(n_lanes) if not _hidden(sees(u, li))]
        later, col_sums = list(later), {}

        def score(u, li):
            z = _dot(keys(jnp.maximum(n_tiles - 1 - u, 0), lane_group(li)[0]), queries[li])
            if not isinstance(sees(u, li), str):
                z = jnp.where(sees(u, li), z, MASKED)
            if check_exists and u >= 2:
                z = jnp.where(u < n_tiles, z, MASKED)
            z_buf[u, li] = z

        def softplus(u, li):
            sp_buf[u, li] = softplus2(z_buf[u, li]).astype(BF16)

        def cumsum(u, li):
            within = _dot(suffix, sp_buf[u, li])
            col_sums[u, li] = within[0:1, :]
            z_buf[u, li] = z_buf[u, li] - within

        def weight(u, li):
            pp, hh, _ = lane_group(li)
            a = jnp.exp2(z_buf[u, li] - later[li])
            acc_buf[li] += _dot(values(jnp.maximum(n_tiles - 1 - u, 0), pp, hh), a.astype(BF16))
            later[li] = later[li] + col_sums[u, li]

        stages = (score, softplus, cumsum, weight)
        for pos in range(len(live) + len(stages) - 1):
            for lag, stage in enumerate(stages):
                if 0 <= pos - lag < len(live):
                    stage(*live[pos - lag])
        return later

    n_rest = n_tiles - n_wide

    def rest_matters(first_unscored, mass):
        j_rest = jnp.maximum(n_rest - 1 - first_unscored, 0)
        worst = None
        for li in range(n_lanes):
            pp, hh, _ = lane_group(li)
            bound = q_norm[li] * stats_ref[0, pp, j_rest, 4 + hh:5 + hh, :] - mass[li]
            worst = bound if worst is None else jnp.maximum(worst, bound)
        return jnp.max(worst) >= -PRUNE_LOG2

    def step(t, slot, later, score=True, softplus=True, cumsum=True, weight=True):
        new_later = []
        for li in range(n_lanes):
            pp, hh, _ = lane_group(li)
            if cumsum:
                within = _dot(suffix, sp_buf[1 - slot, li])
                e_buf[1 - slot, li] = z_buf[1 - slot, li] - within
                wrow_buf[1 - slot, li] = within[0:1, :]
            if score:
                z_buf[1 - slot, li] = _dot(keys(n_rest - 1 - (t + 3), pp), queries[li])
            if weight:
                a = jnp.exp2(e_buf[slot, li] - later[li])
                acc_buf[li] += _dot(values(n_rest - 1 - jnp.maximum(t, 0), pp, hh), a.astype(BF16))
                new_later.append(later[li] + wrow_buf[slot, li])
            else:
                new_later.append(later[li])
            if softplus:
                sp_buf[slot, li] = softplus2(z_buf[slot, li]).astype(BF16)
        return tuple(new_later)

    def pair_and_check(state):
        i, _, later = state
        t = 2 * i - 1
        later = step(t + 1, 0, step(t, 1, later))
        mass = [later[li] + wrow_buf[1, li] for li in range(n_lanes)]
        return i + 1, rest_matters(2 * i + 4, mass), later

    acc_buf[...] = jnp.zeros(acc_buf.shape, F32)
    first = wide_block(range(n_first), [jnp.zeros((1, cw), F32)] * n_lanes, check_exists=True)
    for li in range(n_lanes):
        later_buf[li] = first[li]
    swept = lambda: [later_buf[li] for li in range(n_lanes)]
    more_matters = (n_rest + (n_wide - n_first) > 0) & rest_matters(n_first - n_wide, first)

    @pl.when(more_matters)
    def _():
        more = wide_block(range(n_first, n_wide), first, check_exists=False)
        for li in range(n_lanes):
            later_buf[li] = more[li]

    @pl.when(more_matters & (n_rest > 0) & rest_matters(0, swept()))
    def _():
        later = swept()
        e_buf[1] = jnp.full(e_buf.shape[1:], MASKED, F32)
        wrow_buf[1] = jnp.zeros(wrow_buf.shape[1:], F32)
        mass = step(-3, 1, tuple(later), softplus=False, cumsum=False, weight=False)
        mass = step(-2, 0, mass, cumsum=False, weight=False)
        n_pairs, _, mass = lax.while_loop(lambda st: (2 * st[0] + 2 < n_rest) & st[1], pair_and_check,
                                          (jnp.int32(0), jnp.bool_(True), mass))
        mass = step(2 * n_pairs - 1, 1, mass, score=False)
        mass = step(2 * n_pairs, 0, mass, score=False, softplus=False)
        step(2 * n_pairs + 1, 1, mass, score=False, softplus=False, cumsum=False)

    _finish_heads([acc_buf[li] for li in range(n_lanes)], g_ref, o_ref)


def _attention(body, name, pairs, scratch, qk, k_arr, k_spec, vt, g, stats, q_block0, vt_block0, n_heads):
    bsz, s, _ = qk.shape
    bq, bk, width = ATT_Q, ATT_K, pairs * LANES
    d_grp = n_heads * HEAD_DIM
    assert (n_heads // 2) % pairs == 0
    return pl.pallas_call(
        body,
        grid=(bsz, n_heads // 2 // pairs, s // bq),
        in_specs=[pl.BlockSpec((1, bq, width), lambda b, p, i: (b, i, q_block0 + p)),
                  k_spec,
                  pl.BlockSpec((1, s // bk, width, bk), lambda b, p, i: (b, 0, vt_block0 + p, 0)),
                  pl.BlockSpec((1, width), lambda b, p, i: (0, p)),
                  pl.BlockSpec((1, pairs) + stats.shape[2:], lambda b, p, i: (b, p, 0, 0, 0))],
        out_specs=pl.BlockSpec((1, bq, width), lambda b, p, i: (b, i, p)),
        out_shape=jax.ShapeDtypeStruct((bsz, s, d_grp), BF16),
        scratch_shapes=scratch,
        compiler_params=_params("arbitrary", "arbitrary", "arbitrary"),
        name=name,
    )(qk, k_arr, vt, g.reshape(1, d_grp), stats)


def _lane_groups(pairs):
    return pairs * 2 * (ATT_Q // ATT_COLS)


def _fox_scratch(pairs):
    n = _lane_groups(pairs)
    return [pltpu.VMEM((2, n, ATT_K, ATT_COLS), F32), pltpu.VMEM((2, n, 1, ATT_COLS), F32),
            pltpu.VMEM((2, n, ATT_K, ATT_COLS), BF16), pltpu.VMEM((n, HEAD_DIM, ATT_COLS), F32)]


def _sb_scratch(pairs):
    n = _lane_groups(pairs)
    wide = SB_WIDE_TILES + SB_MORE_TILES
    return [pltpu.VMEM((wide, n, ATT_K, ATT_COLS), F32), pltpu.VMEM((wide, n, ATT_K, ATT_COLS), BF16),
            pltpu.VMEM((2, n, ATT_K, ATT_COLS), F32), pltpu.VMEM((2, n, 1, ATT_COLS), F32),
            pltpu.VMEM((n, 1, ATT_COLS), F32), pltpu.VMEM((n, HEAD_DIM, ATT_COLS), F32)]


def _mixer_kernel(x_ref, xh_ref, mf_ref, mfh_ref, ms_ref, msh_ref, mod_ref, wo_ref, gm_ref, wu_ref, cw_ref,
                  cb_ref, wd_ref, gf_ref, o_ref, u_buf, acc_ref, x1_buf, *, final_norm):
    i = pl.program_id(1)
    tm = x_ref.shape[1]
    n_chunks = wd_ref.shape[0]
    mix = jnp.concatenate([jnp.concatenate([mfh_ref[0], msh_ref[0]], axis=-1),
                           jnp.concatenate([mf_ref[0], ms_ref[0]], axis=-1)], axis=0)
    x_ext = jnp.concatenate([xh_ref[0], x_ref[0]], axis=0)
    x1_ext = x_ext + mod_ref[0, 2:3, :] * _dot(mix, wo_ref[...])
    shift = mod_ref[0, 3:4, :]
    scale = mod_ref[0, 4:5, :]
    h_ext = _rms_rows(x1_ext) * gm_ref[...] * (1.0 + scale) + shift
    row = lax.broadcasted_iota(jnp.int32, h_ext.shape, 0)
    hx = jnp.where((row >= BF16_SUBLANES) | (i > 0), h_ext, 0.0).astype(BF16)
    x1_buf[...] = x1_ext[BF16_SUBLANES:, :]
    acc_ref[...] = jnp.zeros_like(acc_ref)

    def project_up(c, slot):
        for br in range(2):
            u_buf[slot, br] = _dot(hx, wu_ref[br, c])

    def mix_down(c, slot):
        branches = []
        for br in range(2):
            out = cb_ref[br, c]
            for tap in range(CONV_WIDTH):
                first = BF16_SUBLANES - (CONV_WIDTH - 1 - tap)
                out = out + cw_ref[br, c, tap:tap + 1, :] * u_buf[slot, br, pl.ds(first, tm), :]
            branches.append(out)
        u_gate, u_val = branches
        acc_ref[...] += _dot((u_gate * jax.nn.sigmoid(u_gate) * u_val).astype(BF16), wd_ref[c])

    def chunk_pair(j, _):
        c = 2 * j
        project_up(c + 1, 1)
        mix_down(c, 0)
        project_up(c + 2, 0)
        mix_down(c + 1, 1)
        return 0

    project_up(0, 0)
    lax.fori_loop(0, (n_chunks - 1) // 2, chunk_pair, 0)
    mix_down(n_chunks - 1, 0)
    x2 = x1_buf[...] + mod_ref[0, 5:6, :] * acc_ref[...]
    o_ref[0] = _rms_rows(x2) * gf_ref[...] if final_norm else x2


def _mixer(x, mix_f, mix_s, mod, w_out, g_mlp, w_up, conv_w, conv_b, w_down, g_final, final_norm):
    bsz, s, d = x.shape
    tm = OUT_ROWS
    n_chunks, tf = w_down.shape[0], w_down.shape[1]
    assert n_chunks % 2 == 1
    halo_blocks = tm // BF16_SUBLANES
    row = lambda b, i: (b, i, 0)
    halo = lambda b, i: (b, jnp.maximum(i * halo_blocks - 1, 0), 0)
    tile_and_halo = lambda a: [pl.BlockSpec((1, tm, a.shape[2]), row),
                               pl.BlockSpec((1, BF16_SUBLANES, a.shape[2]), halo)]
    resident = lambda a: pl.BlockSpec(a.shape, lambda b, i: (0,) * a.ndim, pipeline_mode=pl.Buffered(1))
    return pl.pallas_call(
        functools.partial(_mixer_kernel, final_norm=final_norm),
        grid=(bsz, s // tm),
        in_specs=tile_and_halo(x) + tile_and_halo(mix_f) + tile_and_halo(mix_s)
                 + [pl.BlockSpec((1, N_MOD, d), lambda b, i: (b, 0, 0)), resident(w_out),
                    pl.BlockSpec((1, d), lambda b, i: (0, 0)),
                    resident(w_up), resident(conv_w), resident(conv_b), resident(w_down),
                    pl.BlockSpec((1, d), lambda b, i: (0, 0))],
        out_specs=pl.BlockSpec((1, tm, d), row),
        out_shape=jax.ShapeDtypeStruct((bsz, s, d), F32),
        scratch_shapes=[pltpu.VMEM((2, 2, tm + BF16_SUBLANES, tf), F32), pltpu.VMEM((tm, d), F32),
                        pltpu.VMEM((tm, d), F32)],
        compiler_params=_params("arbitrary", "arbitrary"),
        name="mixer",
    )(x, x, mix_f, mix_f, mix_s, mix_s, mod, w_out, g_mlp, w_up, conv_w, conv_b, w_down, g_final)


def _chunk_columns(a, d_ff, n_chunks, tf):
    halves = jnp.stack([a[:, :d_ff], a[:, d_ff:]])
    halves = jnp.pad(halves, ((0, 0), (0, 0), (0, n_chunks * tf - d_ff)))
    return halves.reshape(2, a.shape[0], n_chunks, tf).transpose(0, 2, 1, 3)


def _pad_cols(a, n):
    return jnp.pad(a, ((0, 0), (0, n - a.shape[1])))


def kernel(x, c, w_ada, b_ada, g_attn, w_in, b_fgate, g_out_fox, g_out_sb, w_out,
           g_mlp, w_up, conv_w, conv_b, w_down, g_final):
    depth, d, _ = w_ada.shape
    n_fox = b_fgate.shape[1]
    d_fox = n_fox * HEAD_DIM
    d_sb = g_out_sb.shape[1]
    n_sb = d_sb // HEAD_DIM
    d_ff = w_down.shape[1]
    d_ff_pad = -(-d_ff // FF_CHUNK) * FF_CHUNK
    assert n_fox % 2 == 0 and n_sb == n_fox and 3 * n_fox <= LANES
    assert x.shape[1] % OUT_ROWS == 0 and x.shape[1] % ATT_Q == 0 and ATT_Q == 2 * ATT_K
    o_kf, o_vf, o_qs, o_ks, o_vs, o_gate = (d_fox, 2 * d_fox, 3 * d_fox, 3 * d_fox + d_sb,
                                             3 * d_fox + 2 * d_sb, 3 * d_fox + 3 * d_sb)

    for l in range(depth):
        mod = _ada(c, w_ada[l], b_ada[l]).reshape(-1, N_MOD, d)
        w = w_in[l]
        w_nat = jnp.concatenate([w[:, :o_vf], w[:, o_qs:o_vs]], axis=1).astype(BF16)
        w_vt = jnp.concatenate([w[:, o_vf:o_qs], w[:, o_vs:o_gate]], axis=1).T.astype(BF16)
        w_gate = _pad_cols(w[:, o_gate:], LANES).astype(BF16)
        b_gate = _pad_cols(b_fgate[l].reshape(1, n_fox), LANES)
        qk, vt, k_aug, stats = _inproj(x, mod, g_attn[l].reshape(1, d), w_nat, w_vt, w_gate, b_gate, n_fox)

        steps_f, steps_s = n_fox // 2 // FOX_PAIRS, n_sb // 2 // SB_PAIRS
        fox_k_spec = pl.BlockSpec((1, 2 * FOX_PAIRS, x.shape[1], LANES), lambda b, p, i: (b, p, 0, 0))
        mix_f = _attention(_fox_kernel, "fox", FOX_PAIRS, _fox_scratch(FOX_PAIRS), qk, k_aug, fox_k_spec, vt,
                           g_out_fox[l], stats, q_block0=0, vt_block0=0, n_heads=n_fox)
        sb_k_spec = pl.BlockSpec((1, x.shape[1], SB_PAIRS * LANES), lambda b, p, i: (b, 0, 3 * steps_s + p))
        mix_s = _attention(_sb_kernel, "sb", SB_PAIRS, _sb_scratch(SB_PAIRS), qk, qk, sb_k_spec, vt,
                           g_out_sb[l], stats, q_block0=2 * steps_s, vt_block0=steps_s, n_heads=n_sb)

        n_ff = d_ff_pad // FF_CHUNK
        x = _mixer(x, mix_f, mix_s, mod, w_out[l].astype(BF16), g_mlp[l].reshape(1, d),
                   _chunk_columns(w_up[l], d_ff, n_ff, FF_CHUNK).astype(BF16),
                   _chunk_columns(conv_w[l], d_ff, n_ff, FF_CHUNK),
                   _chunk_columns(conv_b[l].reshape(1, -1), d_ff, n_ff, FF_CHUNK),
                   jnp.pad(w_down[l], ((0, d_ff_pad - d_ff), (0, 0))).astype(BF16).reshape(n_ff, FF_CHUNK, d),
                   g_final.reshape(1, d), final_norm=(l == depth - 1))
    return x
```

```python
import functools

import numpy as np
import jax
import jax.numpy as jnp
from jax import lax
from jax.experimental import pallas as pl
from jax.experimental.pallas import tpu as pltpu

HEAD_DIM = 64
N_MOD = 6
CONV_WIDTH = 3
EPS = 1e-6

LANES = 128
BF16_SUBLANES = 16
VMEM_LIMIT_BYTES = 48 * 1024 * 1024

ATT_Q = 512
FOX_Q = 2 * ATT_Q
ATT_K = 256
ATT_COLS = 256
FOX_PAIRS = 1
SB_PAIRS = 2
SB_WIDE_TILES = 4
SB_MORE_TILES = 2
PROJ_ROWS = 2 * ATT_K
LOG2E = 1.4426950408889634
MASKED = -1e30
M_INIT = -1e29
EXP2_MAX = 126.0
PRUNE_LOG2 = 160.0
NORM_SLACK = 1.02
STATS_ROWS = 8
OUT_ROWS = 512
FF_CHUNK = 256

F32 = jnp.float32
BF16 = jnp.bfloat16
NT_DIMS = (((1,), (1,)), ((), ()))


def _dot(a, b):
    return jnp.dot(a, b, preferred_element_type=F32)


def _dot_nt(a, b):
    return lax.dot_general(a, b, NT_DIMS, preferred_element_type=F32)


def _params(*sem):
    return pltpu.CompilerParams(dimension_semantics=sem, vmem_limit_bytes=VMEM_LIMIT_BYTES)


def _rms_rows(x):
    return x * lax.rsqrt(jnp.mean(x * x, axis=-1, keepdims=True) + EPS)


def _softplus(z):
    return jnp.maximum(z, 0.0) + jnp.log(1.0 + jnp.exp(-jnp.abs(z)))


def _split3(x):
    hi = x.astype(BF16)
    r1 = x - hi.astype(F32)
    mid = r1.astype(BF16)
    lo = (r1 - mid.astype(F32)).astype(BF16)
    return hi, mid, lo


def _ada_kernel(c_ref, w_ref, b_ref, o_ref):
    c = c_ref[...]
    o_ref[...] = _dot(c * jax.nn.sigmoid(c), w_ref[...]) + b_ref[...]


def _ada(c, w, b):
    bsz, d = c.shape
    n = w.shape[1]
    return pl.pallas_call(
        _ada_kernel,
        grid=(n // d,),
        in_specs=[pl.BlockSpec((bsz, d), lambda j: (0, 0)),
                  pl.BlockSpec((d, d), lambda j: (0, j)),
                  pl.BlockSpec((1, d), lambda j: (0, j))],
        out_specs=pl.BlockSpec((bsz, d), lambda j: (0, j)),
        out_shape=jax.ShapeDtypeStruct((bsz, n), F32),
        compiler_params=_params("arbitrary"),
        name="ada",
    )(c, w, b.reshape(1, n))


def _decay_tile(lf, k_fox, k_sb, sel_ref, ind_ref, carry_ref, kpre_ref, kaug_ref, stats_ref, rows, tile, n_heads):
    tk = lf.shape[0]
    lane = lax.broadcasted_iota(jnp.int32, (tk, LANES), 1)
    lf = jnp.where(lane < n_heads, lf, 0.0)
    row = lax.broadcasted_iota(jnp.int32, (tk, tk), 0)
    col = lax.broadcasted_iota(jnp.int32, (tk, tk), 1)
    tri = (col <= row).astype(BF16)
    hi, mid, lo = _split3(lf)
    f_run = carry_ref[...] + (_dot(tri, hi) + _dot(tri, mid) + _dot(tri, lo))
    carry_ref[...] = f_run[tk - 1:tk, :]
    ghi, gmid, glo = _split3(-LOG2E * f_run)
    packed = (ghi.astype(F32) + pltpu.roll(gmid.astype(F32), n_heads, 1)
              + pltpu.roll(glo.astype(F32), 2 * n_heads, 1)).astype(BF16)
    placed = _dot(packed, sel_ref[...])
    for h in range(n_heads):
        k_pair = k_fox[:, (h // 2) * LANES:(h // 2 + 1) * LANES]
        own = (lane < HEAD_DIM) if h % 2 == 0 else (lane >= HEAD_DIM)
        kaug_ref[0, h, rows, :] = jnp.where(own, k_pair, placed[:, h * LANES:(h + 1) * LANES].astype(BF16))

    def head_norm_bound(k):
        k32 = k.astype(F32)
        sq = _dot((k32 * k32).astype(BF16), ind_ref[...])
        return jnp.sqrt(jnp.max(sq, axis=0, keepdims=True) * NORM_SLACK)

    kpre_f = jnp.maximum(kpre_ref[0:1, :], head_norm_bound(k_fox))
    kpre_s = jnp.maximum(kpre_ref[1:2, :], head_norm_bound(k_sb))
    kpre_ref[0:1, :] = kpre_f
    kpre_ref[1:2, :] = kpre_s
    g_end = -LOG2E * f_run[tk - 1:tk, :]
    lane1 = lax.broadcasted_iota(jnp.int32, (1, LANES), 1)

    def spread(v, h):
        return jnp.broadcast_to(jnp.sum(jnp.where(lane1 == h, v, 0.0), axis=1, keepdims=True), (1, ATT_COLS))

    for p in range(n_heads // 2):
        srows = [spread(v, 2 * p + hh) for v in (kpre_f, g_end, kpre_s) for hh in range(2)]
        srows += [jnp.zeros((1, ATT_COLS), F32)] * (stats_ref.shape[3] - len(srows))
        stats_ref[0, p, tile] = jnp.concatenate(srows, axis=0)


def _inproj_kernel(x_ref, mod_ref, g_ref, wn_ref, wvt_ref, wg_ref, bg_ref, sel_ref, ind_ref,
                   qk_ref, vt_ref, kaug_ref, stats_ref, carry_ref, kpre_ref, *, n_heads):
    @pl.when(pl.program_id(1) == 0)
    def _():
        carry_ref[...] = jnp.zeros_like(carry_ref)
        kpre_ref[...] = jnp.zeros_like(kpre_ref)

    shift = mod_ref[0, 0:1, :]
    scale = mod_ref[0, 1:2, :]
    h = (_rms_rows(x_ref[0]) * g_ref[...] * (1.0 + scale) + shift).astype(BF16)
    qk = _dot(h, wn_ref[...]).astype(BF16)
    qk_ref[0] = qk
    logit = _dot(h, wg_ref[...]) + bg_ref[...]
    log_f = -_softplus(-logit)
    d_grp = n_heads * HEAD_DIM
    tk = vt_ref.shape[3]
    for tile in range(vt_ref.shape[1]):
        rows = slice(tile * tk, (tile + 1) * tk)
        vt_ref[0, tile] = _dot_nt(wvt_ref[...], h[rows, :]).astype(BF16)
        _decay_tile(log_f[rows, :], qk[rows, d_grp:2 * d_grp], qk[rows, 3 * d_grp:4 * d_grp], sel_ref, ind_ref,
                    carry_ref, kpre_ref, kaug_ref, stats_ref, rows, tile, n_heads)


def _head_indicator(n_heads):
    ind = np.zeros((n_heads * HEAD_DIM, LANES), np.float32)
    ind[np.arange(n_heads * HEAD_DIM), np.arange(n_heads * HEAD_DIM) // HEAD_DIM] = 1.0
    return jnp.asarray(ind, BF16)


def _decay_select_matrix(n_heads):
    sel = np.zeros((LANES, n_heads * LANES), np.float32)
    for h in range(n_heads):
        base = h * LANES + (HEAD_DIM if h % 2 == 0 else 0)
        for term in range(3):
            sel[term * n_heads + h, base + term] = 1.0
    return jnp.asarray(sel, BF16)


def _inproj(x, mod, g, w_nat, w_vt, w_gate, b_gate, n_heads):
    bsz, s, d = x.shape
    tm, tk = PROJ_ROWS, ATT_K
    n_nat, n_v = w_nat.shape[1], w_vt.shape[0]
    const = lambda b, i: (0, 0)
    resident = lambda a: pl.BlockSpec(a.shape, const, pipeline_mode=pl.Buffered(1))
    sel, ind = _decay_select_matrix(n_heads), _head_indicator(n_heads)
    return pl.pallas_call(
        functools.partial(_inproj_kernel, n_heads=n_heads),
        grid=(bsz, s // tm),
        in_specs=[pl.BlockSpec((1, tm, d), lambda b, i: (b, i, 0)),
                  pl.BlockSpec((1, N_MOD, d), lambda b, i: (b, 0, 0)),
                  pl.BlockSpec((1, d), const),
                  resident(w_nat), resident(w_vt), resident(w_gate),
                  pl.BlockSpec((1, LANES), const),
                  resident(sel), resident(ind)],
        out_specs=[pl.BlockSpec((1, tm, n_nat), lambda b, i: (b, i, 0)),
                   pl.BlockSpec((1, tm // tk, n_v, tk), lambda b, i: (b, i, 0, 0)),
                   pl.BlockSpec((1, n_heads, tm, LANES), lambda b, i: (b, 0, i, 0)),
                   pl.BlockSpec((1, n_heads // 2, tm // tk, STATS_ROWS, ATT_COLS), lambda b, i: (b, 0, i, 0, 0))],
        out_shape=[jax.ShapeDtypeStruct((bsz, s, n_nat), BF16),
                   jax.ShapeDtypeStruct((bsz, s // tk, n_v, tk), BF16),
                   jax.ShapeDtypeStruct((bsz, n_heads, s, LANES), BF16),
                   jax.ShapeDtypeStruct((bsz, n_heads // 2, s // tk, STATS_ROWS, ATT_COLS), F32)],
        scratch_shapes=[pltpu.VMEM((1, LANES), F32), pltpu.VMEM((2, LANES), F32)],
        compiler_params=_params("arbitrary", "arbitrary"),
        name="inproj",
    )(x, mod, g, w_nat, w_vt, w_gate, b_gate, sel, ind)


def _lane_queries(q_ref, extra_even, extra_odd, cw):
    out = []
    for pp in range(q_ref.shape[2] // LANES):
        q = q_ref[0, :, pp * LANES:(pp + 1) * LANES].astype(F32) * (HEAD_DIM ** -0.5 * LOG2E)
        lane = lax.broadcasted_iota(jnp.int32, q.shape, 1)
        heads = (jnp.where(lane < HEAD_DIM, q, extra_even(lane)).T.astype(BF16),
                 jnp.where(lane >= HEAD_DIM, q, extra_odd(lane)).T.astype(BF16))
        out += [heads[hh][:, c * cw:(c + 1) * cw] for hh in range(2) for c in range(q.shape[0] // cw)]
    return out


def _visibility(first_key, first_query, bk, cw, strict):
    last_visible_gap = -1 if strict else 0
    if first_key + bk - 1 - first_query <= last_visible_gap:
        return "all"
    if first_key - (first_query + cw - 1) > last_visible_gap:
        return "none"
    gap = (lax.broadcasted_iota(jnp.int32, (bk, cw), 0) - lax.broadcasted_iota(jnp.int32, (bk, cw), 1)
           + (first_key - first_query))
    return gap <= last_visible_gap


def _diag_visibility(u, c, bk, cw, strict):
    return _visibility((1 - u) * bk, c * cw, bk, cw, strict)


def _hidden(visibility):
    return isinstance(visibility, str) and visibility == "none"


def _query_norm_bounds(queries, n_chunks):
    bounds = []
    for li, q in enumerate(queries):
        hh = (li // n_chunks) % 2
        own = q[hh * HEAD_DIM:(hh + 1) * HEAD_DIM, :].astype(F32)
        bounds.append(jnp.sqrt(jnp.sum(own * own, axis=0, keepdims=True) * NORM_SLACK))
    return bounds


def _finish_heads(lanes, g_ref, o_ref):
    n_pairs = o_ref.shape[2] // LANES
    n_chunks = len(lanes) // (2 * n_pairs)
    for pp in range(n_pairs):
        mine = lanes[2 * pp * n_chunks:2 * (pp + 1) * n_chunks]
        outs = [jnp.concatenate(mine[hh * n_chunks:(hh + 1) * n_chunks], axis=1) for hh in range(2)]
        normed = [o * lax.rsqrt(jnp.mean(o * o, axis=0, keepdims=True) + EPS) for o in outs]
        cols = slice(pp * LANES, (pp + 1) * LANES)
        o_ref[0, :, cols] = (jnp.concatenate(normed, axis=0).T * g_ref[:, cols]).astype(o_ref.dtype)


def _fox_kernel(q_ref, k_ref, vt_ref, g_ref, stats_ref, o_ref, s_buf, cmax_buf, p_buf, acc_buf):
    bk = vt_ref.shape[3]
    n_lanes, cw = acc_buf.shape[0], acc_buf.shape[2]
    n_chunks = q_ref.shape[1] // cw
    chunks_per_sub = ATT_Q // cw
    assert q_ref.shape[1] == 2 * ATT_Q
    lane_group = lambda li: (li // (2 * n_chunks), (li // n_chunks) % 2, li % n_chunks)
    first_q_block = 2 * pl.program_id(2)
    n_tiles = 2 * (first_q_block + 1)
    ones3 = lambda lo: (lambda lane: jnp.where((lane >= lo) & (lane < lo + 3), 1.0, 0.0))
    queries = _lane_queries(q_ref, ones3(HEAD_DIM), ones3(0), cw)
    acc_buf[...] = jnp.zeros(acc_buf.shape, F32)
    for li in range(n_lanes):
        if _hidden(_diag_visibility(0, lane_group(li)[2] % chunks_per_sub, bk, cw, strict=False)):
            p_buf[0, li] = jnp.zeros((bk, cw), BF16)

    full, idle = ("below", "below", True), (None, None, False)

    def step(t, slot, carry, stages=(full, full)):
        new = []
        for li in range(n_lanes):
            pp, hh, chunk = lane_group(li)
            sub, c = divmod(chunk, chunks_per_sub)
            score, softmax, value = stages[sub]
            see = lambda u: "all" if u == "below" else _diag_visibility(u, c, bk, cw, strict=False)
            if score is not None and not _hidden(see(score)):
                start = pl.multiple_of((n_tiles - 2 - t) * bk, bk)
                s_new = _dot(k_ref[0, 2 * pp + hh, pl.ds(start, bk), :], queries[li])
                if not isinstance(see(score), str):
                    s_new = jnp.where(see(score), s_new, MASKED)
                s_buf[1 - slot, li] = s_new
                cmax_buf[1 - slot, li] = jnp.max(s_new, axis=0, keepdims=True)
            pv = None
            if value:
                vt = vt_ref[0, n_tiles - t, pl.ds(pp * LANES + hh * HEAD_DIM, HEAD_DIM), :]
                pv = _dot(vt, p_buf[1 - slot, li])
            m, l = carry[li]
            if softmax is not None and not _hidden(see(softmax)):
                m_new = jnp.maximum(m, cmax_buf[slot, li])
                alpha = jnp.exp2(m - m_new)
                p = jnp.exp2(s_buf[slot, li] - m_new)
                p_buf[slot, li] = p.astype(BF16)
                m, l = m_new, alpha * l + jnp.sum(p, axis=0, keepdims=True)
                acc_buf[li] = alpha * (acc_buf[li] if pv is None else acc_buf[li] + pv)
            elif pv is not None:
                acc_buf[li] += pv
            new.append((m, l))
        return tuple(new)

    def step_pair(i, carry):
        t = 2 * i + 1
        return step(t + 1, 0, step(t, 1, carry))

    q_norm = _query_norm_bounds(queries, n_chunks)

    def later_tiles_matter(i, carry):
        j_rest = jnp.maximum(n_tiles - 5 - 2 * i, 0)
        worst = None
        for li in range(n_lanes):
            pp, hh, _ = lane_group(li)
            bound = (q_norm[li] * stats_ref[0, pp, j_rest, hh:hh + 1, :]
                     + stats_ref[0, pp, j_rest, 2 + hh:3 + hh, :] - carry[li][0])
            worst = bound if worst is None else jnp.maximum(worst, bound)
        return jnp.max(worst) >= -PRUNE_LOG2

    def pair_and_check(state):
        i, _, carry = state
        carry = step_pair(i, carry)
        return i + 1, later_tiles_matter(i, carry), carry

    carry = tuple((jnp.full((1, cw), M_INIT, F32), jnp.zeros((1, cw), F32)) for _ in range(n_lanes))
    carry = step(-3, 1, carry, (idle, (0, None, False)))
    carry = step(-2, 0, carry, (idle, (1, 0, False)))
    carry = step(-1, 1, carry, ((0, None, False), full))
    carry = step(0, 0, carry, ((1, 0, False), full))
    n_pairs, _, carry = lax.while_loop(lambda st: (st[0] < first_q_block) & st[1], pair_and_check,
                                       (jnp.int32(0), jnp.bool_(True), carry))
    drain = (None, "below", True)
    carry = step(2 * n_pairs + 1, 1, carry, (drain, drain))
    drain = (None, None, True)
    carry = step(2 * n_pairs + 2, 0, carry, (drain, drain))
    _finish_heads([acc_buf[li] / carry[li][1] for li in range(n_lanes)], g_ref, o_ref)


def _sb_kernel(q_ref, k_ref, vt_ref, g_ref, stats_ref, o_ref, z_buf, sp_buf, e_buf, wrow_buf, later_buf, acc_buf):
    qi = pl.program_id(2)
    bk = vt_ref.shape[3]
    n_wide, n_first = z_buf.shape[0], SB_WIDE_TILES
    n_lanes, cw = acc_buf.shape[0], acc_buf.shape[2]
    n_chunks = q_ref.shape[1] // cw
    lane_group = lambda li: (li // (2 * n_chunks), (li // n_chunks) % 2, li % n_chunks)
    n_tiles = 2 * (qi + 1)
    zero = lambda lane: 0.0
    queries = _lane_queries(q_ref, zero, zero, cw)
    q_norm = _query_norm_bounds(queries, n_chunks)
    suffix = (lax.broadcasted_iota(jnp.int32, (bk, bk), 1)
              >= lax.broadcasted_iota(jnp.int32, (bk, bk), 0)).astype(BF16)
    softplus2 = lambda z: jnp.maximum(z, jnp.log2(1.0 + jnp.exp2(jnp.minimum(z, EXP2_MAX))))
    keys = lambda j, pp: k_ref[0, pl.ds(pl.multiple_of(j * bk, bk), bk), pp * LANES:(pp + 1) * LANES]
    values = lambda j, pp, hh: vt_ref[0, j, pl.ds(pp * LANES + hh * HEAD_DIM, HEAD_DIM), :]

    sees = lambda u, li: _diag_visibility(u, lane_group(li)[2], bk, cw, strict=True) if u < 2 else "all"

    def wide_block(tiles, later, check_exists):
        live = [(u, li) for u in tiles for li in range(n_lanes) if not _hidden(sees(u, li))]
        later, col_sums = list(later), {}

        def score(u, li):
            z = _dot(keys(jnp.maximum(n_tiles - 1 - u, 0), lane_group(li)[0]), queries[li])
            if not isinstance(sees(u, li), str):
                z = jnp.where(sees(u, li), z, MASKED)
            if check_exists and u >= 2:
                z = jnp.where(u < n_tiles, z, MASKED)
            z_buf[u, li] = z

        def softplus(u, li):
            sp_buf[u, li] = softplus2(z_buf[u, li]).astype(BF16)

        def cumsum(u, li):
            within = _dot(suffix, sp_buf[u, li])
            col_sums[u, li] = within[0:1, :]
            z_buf[u, li] = z_buf[u, li] - within

        def weight(u, li):
            pp, hh, _ = lane_group(li)
            a = jnp.exp2(z_buf[u, li] - later[li])
            acc_buf[li] += _dot(values(jnp.maximum(n_tiles - 1 - u, 0), pp, hh), a.astype(BF16))
            later[li] = later[li] + col_sums[u, li]

        stages = (score, softplus, cumsum, weight)
        for pos in range(len(live) + len(stages) - 1):
            for lag, stage in enumerate(stages):
                if 0 <= pos - lag < len(live):
                    stage(*live[pos - lag])
        return later

    n_rest = n_tiles - n_wide

    def rest_matters(first_unscored, mass):
        j_rest = jnp.maximum(n_rest - 1 - first_unscored, 0)
        worst = None
        for li in range(n_lanes):
            pp, hh, _ = lane_group(li)
            bound = q_norm[li] * stats_ref[0, pp, j_rest, 4 + hh:5 + hh, :] - mass[li]
            worst = bound if worst is None else jnp.maximum(worst, bound)
        return jnp.max(worst) >= -PRUNE_LOG2

    def step(t, slot, later, score=True, softplus=True, cumsum=True, weight=True):
        new_later = []
        for li in range(n_lanes):
            pp, hh, _ = lane_group(li)
            if cumsum:
                within = _dot(suffix, sp_buf[1 - slot, li])
                e_buf[1 - slot, li] = z_buf[1 - slot, li] - within
                wrow_buf[1 - slot, li] = within[0:1, :]
            if score:
                z_buf[1 - slot, li] = _dot(keys(n_rest - 1 - (t + 3), pp), queries[li])
            if weight:
                a = jnp.exp2(e_buf[slot, li] - later[li])
                acc_buf[li] += _dot(values(n_rest - 1 - jnp.maximum(t, 0), pp, hh), a.astype(BF16))
                new_later.append(later[li] + wrow_buf[slot, li])
            else:
                new_later.append(later[li])
            if softplus:
                sp_buf[slot, li] = softplus2(z_buf[slot, li]).astype(BF16)
        return tuple(new_later)

    def pair_and_check(state):
        i, _, later = state
        t = 2 * i - 1
        later = step(t + 1, 0, step(t, 1, later))
        mass = [later[li] + wrow_buf[1, li] for li in range(n_lanes)]
        return i + 1, rest_matters(2 * i + 4, mass), later

    acc_buf[...] = jnp.zeros(acc_buf.shape, F32)
    first = wide_block(range(n_first), [jnp.zeros((1, cw), F32)] * n_lanes, check_exists=True)
    for li in range(n_lanes):
        later_buf[li] = first[li]
    swept = lambda: [later_buf[li] for li in range(n_lanes)]
    more_matters = (n_rest + (n_wide - n_first) > 0) & rest_matters(n_first - n_wide, first)

    @pl.when(more_matters)
    def _():
        more = wide_block(range(n_first, n_wide), first, check_exists=False)
        for li in range(n_lanes):
            later_buf[li] = more[li]

    @pl.when(more_matters & (n_rest > 0) & rest_matters(0, swept()))
    def _():
        later = swept()
        e_buf[1] = jnp.full(e_buf.shape[1:], MASKED, F32)
        wrow_buf[1] = jnp.zeros(wrow_buf.shape[1:], F32)
        mass = step(-3, 1, tuple(later), softplus=False, cumsum=False, weight=False)
        mass = step(-2, 0, mass, cumsum=False, weight=False)
        n_pairs, _, mass = lax.while_loop(lambda st: (2 * st[0] + 2 < n_rest) & st[1], pair_and_check,
                                          (jnp.int32(0), jnp.bool_(True), mass))
        mass = step(2 * n_pairs - 1, 1, mass, score=False)
        mass = step(2 * n_pairs, 0, mass, score=False, softplus=False)
        step(2 * n_pairs + 1, 1, mass, score=False, softplus=False, cumsum=False)

    _finish_heads([acc_buf[li] for li in range(n_lanes)], g_ref, o_ref)


def _attention(body, name, pairs, bq, scratch, qk, k_arr, k_spec, vt, g, stats, q_block0, vt_block0, n_heads):
    bsz, s, _ = qk.shape
    bk, width = ATT_K, pairs * LANES
    d_grp = n_heads * HEAD_DIM
    assert (n_heads // 2) % pairs == 0
    return pl.pallas_call(
        body,
        grid=(bsz, n_heads // 2 // pairs, s // bq),
        in_specs=[pl.BlockSpec((1, bq, width), lambda b, p, i: (b, i, q_block0 + p)),
                  k_spec,
                  pl.BlockSpec((1, s // bk, width, bk), lambda b, p, i: (b, 0, vt_block0 + p, 0)),
                  pl.BlockSpec((1, width), lambda b, p, i: (0, p)),
                  pl.BlockSpec((1, pairs) + stats.shape[2:], lambda b, p, i: (b, p, 0, 0, 0))],
        out_specs=pl.BlockSpec((1, bq, width), lambda b, p, i: (b, i, p)),
        out_shape=jax.ShapeDtypeStruct((bsz, s, d_grp), BF16),
        scratch_shapes=scratch,
        compiler_params=_params("arbitrary", "arbitrary", "arbitrary"),
        name=name,
    )(qk, k_arr, vt, g.reshape(1, d_grp), stats)


def _lane_groups(pairs, bq):
    return pairs * 2 * (bq // ATT_COLS)


def _fox_scratch(pairs):
    n = _lane_groups(pairs, FOX_Q)
    return [pltpu.VMEM((2, n, ATT_K, ATT_COLS), F32), pltpu.VMEM((2, n, 1, ATT_COLS), F32),
            pltpu.VMEM((2, n, ATT_K, ATT_COLS), BF16), pltpu.VMEM((n, HEAD_DIM, ATT_COLS), F32)]


def _sb_scratch(pairs):
    n = _lane_groups(pairs, ATT_Q)
    wide = SB_WIDE_TILES + SB_MORE_TILES
    return [pltpu.VMEM((wide, n, ATT_K, ATT_COLS), F32), pltpu.VMEM((wide, n, ATT_K, ATT_COLS), BF16),
            pltpu.VMEM((2, n, ATT_K, ATT_COLS), F32), pltpu.VMEM((2, n, 1, ATT_COLS), F32),
            pltpu.VMEM((n, 1, ATT_COLS), F32), pltpu.VMEM((n, HEAD_DIM, ATT_COLS), F32)]


def _mixer_kernel(x_ref, xh_ref, mf_ref, mfh_ref, ms_ref, msh_ref, mod_ref, wo_ref, gm_ref, wu_ref, cw_ref,
                  cb_ref, wd_ref, gf_ref, o_ref, u_buf, acc_ref, x1_buf, *, final_norm):
    i = pl.program_id(1)
    tm = x_ref.shape[1]
    n_chunks = wd_ref.shape[0]
    mix = jnp.concatenate([jnp.concatenate([mfh_ref[0], msh_ref[0]], axis=-1),
                           jnp.concatenate([mf_ref[0], ms_ref[0]], axis=-1)], axis=0)
    x_ext = jnp.concatenate([xh_ref[0], x_ref[0]], axis=0)
    x1_ext = x_ext + mod_ref[0, 2:3, :] * _dot(mix, wo_ref[...])
    shift = mod_ref[0, 3:4, :]
    scale = mod_ref[0, 4:5, :]
    h_ext = _rms_rows(x1_ext) * gm_ref[...] * (1.0 + scale) + shift
    row = lax.broadcasted_iota(jnp.int32, h_ext.shape, 0)
    hx = jnp.where((row >= BF16_SUBLANES) | (i > 0), h_ext, 0.0).astype(BF16)
    x1_buf[...] = x1_ext[BF16_SUBLANES:, :]
    acc_ref[...] = jnp.zeros_like(acc_ref)

    def project_up(c, slot):
        for br in range(2):
            u_buf[slot, br] = _dot(hx, wu_ref[br, c])

    def mix_down(c, slot):
        branches = []
        for br in range(2):
            out = cb_ref[br, c]
            for tap in range(CONV_WIDTH):
                first = BF16_SUBLANES - (CONV_WIDTH - 1 - tap)
                out = out + cw_ref[br, c, tap:tap + 1, :] * u_buf[slot, br, pl.ds(first, tm), :]
            branches.append(out)
        u_gate, u_val = branches
        acc_ref[...] += _dot((u_gate * jax.nn.sigmoid(u_gate) * u_val).astype(BF16), wd_ref[c])

    def chunk_pair(j, _):
        c = 2 * j
        project_up(c + 1, 1)
        mix_down(c, 0)
        project_up(c + 2, 0)
        mix_down(c + 1, 1)
        return 0

    project_up(0, 0)
    lax.fori_loop(0, (n_chunks - 1) // 2, chunk_pair, 0)
    mix_down(n_chunks - 1, 0)
    x2 = x1_buf[...] + mod_ref[0, 5:6, :] * acc_ref[...]
    o_ref[0] = _rms_rows(x2) * gf_ref[...] if final_norm else x2


def _mixer(x, mix_f, mix_s, mod, w_out, g_mlp, w_up, conv_w, conv_b, w_down, g_final, final_norm):
    bsz, s, d = x.shape
    tm = OUT_ROWS
    n_chunks, tf = w_down.shape[0], w_down.shape[1]
    assert n_chunks % 2 == 1
    halo_blocks = tm // BF16_SUBLANES
    row = lambda b, i: (b, i, 0)
    halo = lambda b, i: (b, jnp.maximum(i * halo_blocks - 1, 0), 0)
    tile_and_halo = lambda a: [pl.BlockSpec((1, tm, a.shape[2]), row),
                               pl.BlockSpec((1, BF16_SUBLANES, a.shape[2]), halo)]
    resident = lambda a: pl.BlockSpec(a.shape, lambda b, i: (0,) * a.ndim, pipeline_mode=pl.Buffered(1))
    return pl.pallas_call(
        functools.partial(_mixer_kernel, final_norm=final_norm),
        grid=(bsz, s // tm),
        in_specs=tile_and_halo(x) + tile_and_halo(mix_f) + tile_and_halo(mix_s)
                 + [pl.BlockSpec((1, N_MOD, d), lambda b, i: (b, 0, 0)), resident(w_out),
                    pl.BlockSpec((1, d), lambda b, i: (0, 0)),
                    resident(w_up), resident(conv_w), resident(conv_b), resident(w_down),
                    pl.BlockSpec((1, d), lambda b, i: (0, 0))],
        out_specs=pl.BlockSpec((1, tm, d), row),
        out_shape=jax.ShapeDtypeStruct((bsz, s, d), F32),
        scratch_shapes=[pltpu.VMEM((2, 2, tm + BF16_SUBLANES, tf), F32), pltpu.VMEM((tm, d), F32),
                        pltpu.VMEM((tm, d), F32)],
        compiler_params=_params("arbitrary", "arbitrary"),
        name="mixer",
    )(x, x, mix_f, mix_f, mix_s, mix_s, mod, w_out, g_mlp, w_up, conv_w, conv_b, w_down, g_final)


def _chunk_columns(a, d_ff, n_chunks, tf):
    halves = jnp.stack([a[:, :d_ff], a[:, d_ff:]])
    halves = jnp.pad(halves, ((0, 0), (0, 0), (0, n_chunks * tf - d_ff)))
    return halves.reshape(2, a.shape[0], n_chunks, tf).transpose(0, 2, 1, 3)


def _pad_cols(a, n):
    return jnp.pad(a, ((0, 0), (0, n - a.shape[1])))


def kernel(x, c, w_ada, b_ada, g_attn, w_in, b_fgate, g_out_fox, g_out_sb, w_out,
           g_mlp, w_up, conv_w, conv_b, w_down, g_final):
    depth, d, _ = w_ada.shape
    n_fox = b_fgate.shape[1]
    d_fox = n_fox * HEAD_DIM
    d_sb = g_out_sb.shape[1]
    n_sb = d_sb // HEAD_DIM
    d_ff = w_down.shape[1]
    d_ff_pad = -(-d_ff // FF_CHUNK) * FF_CHUNK
    assert n_fox % 2 == 0 and n_sb == n_fox and 3 * n_fox <= LANES
    assert x.shape[1] % OUT_ROWS == 0 and x.shape[1] % FOX_Q == 0 and ATT_Q == 2 * ATT_K
    o_kf, o_vf, o_qs, o_ks, o_vs, o_gate = (d_fox, 2 * d_fox, 3 * d_fox, 3 * d_fox + d_sb,
                                             3 * d_fox + 2 * d_sb, 3 * d_fox + 3 * d_sb)

    for l in range(depth):
        mod = _ada(c, w_ada[l], b_ada[l]).reshape(-1, N_MOD, d)
        w = w_in[l]
        w_nat = jnp.concatenate([w[:, :o_vf], w[:, o_qs:o_vs]], axis=1).astype(BF16)
        w_vt = jnp.concatenate([w[:, o_vf:o_qs], w[:, o_vs:o_gate]], axis=1).T.astype(BF16)
        w_gate = _pad_cols(w[:, o_gate:], LANES).astype(BF16)
        b_gate = _pad_cols(b_fgate[l].reshape(1, n_fox), LANES)
        qk, vt, k_aug, stats = _inproj(x, mod, g_attn[l].reshape(1, d), w_nat, w_vt, w_gate, b_gate, n_fox)

        steps_f, steps_s = n_fox // 2 // FOX_PAIRS, n_sb // 2 // SB_PAIRS
        fox_k_spec = pl.BlockSpec((1, 2 * FOX_PAIRS, x.shape[1], LANES), lambda b, p, i: (b, p, 0, 0))
        mix_f = _attention(_fox_kernel, "fox", FOX_PAIRS, FOX_Q, _fox_scratch(FOX_PAIRS), qk, k_aug, fox_k_spec, vt,
                           g_out_fox[l], stats, q_block0=0, vt_block0=0, n_heads=n_fox)
        sb_k_spec = pl.BlockSpec((1, x.shape[1], SB_PAIRS * LANES), lambda b, p, i: (b, 0, 3 * steps_s + p))
        mix_s = _attention(_sb_kernel, "sb", SB_PAIRS, ATT_Q, _sb_scratch(SB_PAIRS), qk, qk, sb_k_spec, vt,
                           g_out_sb[l], stats, q_block0=2 * steps_s, vt_block0=steps_s, n_heads=n_sb)

        n_ff = d_ff_pad // FF_CHUNK
        x = _mixer(x, mix_f, mix_s, mod, w_out[l].astype(BF16), g_mlp[l].reshape(1, d),
                   _chunk_columns(w_up[l], d_ff, n_ff, FF_CHUNK).astype(BF16),
                   _chunk_columns(conv_w[l], d_ff, n_ff, FF_CHUNK),
                   _chunk_columns(conv_b[l].reshape(1, -1), d_ff, n_ff, FF_CHUNK),
                   jnp.pad(w_down[l], ((0, d_ff_pad - d_ff), (0, 0))).astype(BF16).reshape(n_ff, FF_CHUNK, d),
                   g_final.reshape(1, d), final_norm=(l == depth - 1))
    return x
```

```python
import functools

import numpy as np
import jax
import jax.numpy as jnp
from jax import lax
from jax.experimental import pallas as pl
from jax.experimental.pallas import tpu as pltpu

HEAD_DIM = 64
N_MOD = 6
CONV_WIDTH = 3
EPS = 1e-6

LANES = 128
BF16_SUBLANES = 16
VMEM_LIMIT_BYTES = 48 * 1024 * 1024

ATT_Q = 512
FOX_Q = 2 * ATT_Q
ATT_K = 256
ATT_COLS = 256
FOX_PAIRS = 1
SB_PAIRS = 2
SB_WIDE_TILES = 4
SB_MORE_TILES = 2
PROJ_ROWS = 2 * ATT_K
LOG2E = 1.4426950408889634
MASKED = -1e30
M_INIT = -1e29
EXP2_MAX = 126.0
PRUNE_LOG2 = 160.0
NORM_SLACK = 1.02
STATS_ROWS = 8
OUT_ROWS = 512
FF_CHUNK = 256

F32 = jnp.float32
BF16 = jnp.bfloat16
NT_DIMS = (((1,), (1,)), ((), ()))


def _dot(a, b):
    return jnp.dot(a, b, preferred_element_type=F32)


def _dot_nt(a, b):
    return lax.dot_general(a, b, NT_DIMS, preferred_element_type=F32)


def _params(*sem):
    return pltpu.CompilerParams(dimension_semantics=sem, vmem_limit_bytes=VMEM_LIMIT_BYTES)


def _rms_rows(x):
    return x * lax.rsqrt(jnp.mean(x * x, axis=-1, keepdims=True) + EPS)


def _softplus(z):
    return jnp.maximum(z, 0.0) + jnp.log(1.0 + jnp.exp(-jnp.abs(z)))


def _split3(x):
    hi = x.astype(BF16)
    r1 = x - hi.astype(F32)
    mid = r1.astype(BF16)
    lo = (r1 - mid.astype(F32)).astype(BF16)
    return hi, mid, lo


def _ada_kernel(c_ref, w_ref, b_ref, o_ref):
    c = c_ref[...]
    o_ref[...] = _dot(c * jax.nn.sigmoid(c), w_ref[...]) + b_ref[...]


def _ada(c, w, b):
    bsz, d = c.shape
    n = w.shape[1]
    return pl.pallas_call(
        _ada_kernel,
        grid=(n // d,),
        in_specs=[pl.BlockSpec((bsz, d), lambda j: (0, 0)),
                  pl.BlockSpec((d, d), lambda j: (0, j)),
                  pl.BlockSpec((1, d), lambda j: (0, j))],
        out_specs=pl.BlockSpec((bsz, d), lambda j: (0, j)),
        out_shape=jax.ShapeDtypeStruct((bsz, n), F32),
        compiler_params=_params("arbitrary"),
        name="ada",
    )(c, w, b.reshape(1, n))


def _decay_tile(lf, k_fox, k_sb, sel_ref, ind_ref, carry_ref, kpre_ref, kaug_ref, stats_ref, rows, tile, n_heads):
    tk = lf.shape[0]
    lane = lax.broadcasted_iota(jnp.int32, (tk, LANES), 1)
    lf = jnp.where(lane < n_heads, lf, 0.0)
    row = lax.broadcasted_iota(jnp.int32, (tk, tk), 0)
    col = lax.broadcasted_iota(jnp.int32, (tk, tk), 1)
    tri = (col <= row).astype(BF16)
    hi, mid, lo = _split3(lf)
    f_run = carry_ref[...] + (_dot(tri, hi) + _dot(tri, mid) + _dot(tri, lo))
    carry_ref[...] = f_run[tk - 1:tk, :]
    ghi, gmid, glo = _split3(-LOG2E * f_run)
    packed = (ghi.astype(F32) + pltpu.roll(gmid.astype(F32), n_heads, 1)
              + pltpu.roll(glo.astype(F32), 2 * n_heads, 1)).astype(BF16)
    placed = _dot(packed, sel_ref[...])
    for h in range(n_heads):
        k_pair = k_fox[:, (h // 2) * LANES:(h // 2 + 1) * LANES]
        own = (lane < HEAD_DIM) if h % 2 == 0 else (lane >= HEAD_DIM)
        kaug_ref[0, h, rows, :] = jnp.where(own, k_pair, placed[:, h * LANES:(h + 1) * LANES].astype(BF16))

    def head_norm_bound(k):
        k32 = k.astype(F32)
        sq = _dot((k32 * k32).astype(BF16), ind_ref[...])
        return jnp.sqrt(jnp.max(sq, axis=0, keepdims=True) * NORM_SLACK)

    kpre_f = jnp.maximum(kpre_ref[0:1, :], head_norm_bound(k_fox))
    kpre_s = jnp.maximum(kpre_ref[1:2, :], head_norm_bound(k_sb))
    kpre_ref[0:1, :] = kpre_f
    kpre_ref[1:2, :] = kpre_s
    g_end = -LOG2E * f_run[tk - 1:tk, :]
    lane1 = lax.broadcasted_iota(jnp.int32, (1, LANES), 1)

    def spread(v, h):
        return jnp.broadcast_to(jnp.sum(jnp.where(lane1 == h, v, 0.0), axis=1, keepdims=True), (1, ATT_COLS))

    for p in range(n_heads // 2):
        srows = [spread(v, 2 * p + hh) for v in (kpre_f, g_end, kpre_s) for hh in range(2)]
        srows += [jnp.zeros((1, ATT_COLS), F32)] * (stats_ref.shape[3] - len(srows))
        stats_ref[0, p, tile] = jnp.concatenate(srows, axis=0)


def _inproj_kernel(x_ref, mod_ref, g_ref, wn_ref, wvt_ref, wg_ref, bg_ref, sel_ref, ind_ref,
                   qk_ref, vt_ref, kaug_ref, stats_ref, carry_ref, kpre_ref, *, n_heads):
    @pl.when(pl.program_id(1) == 0)
    def _():
        carry_ref[...] = jnp.zeros_like(carry_ref)
        kpre_ref[...] = jnp.zeros_like(kpre_ref)

    shift = mod_ref[0, 0:1, :]
    scale = mod_ref[0, 1:2, :]
    h = (_rms_rows(x_ref[0]) * g_ref[...] * (1.0 + scale) + shift).astype(BF16)
    qk = _dot(h, wn_ref[...]).astype(BF16)
    qk_ref[0] = qk
    logit = _dot(h, wg_ref[...]) + bg_ref[...]
    log_f = -_softplus(-logit)
    d_grp = n_heads * HEAD_DIM
    tk = vt_ref.shape[3]
    for tile in range(vt_ref.shape[1]):
        rows = slice(tile * tk, (tile + 1) * tk)
        vt_ref[0, tile] = _dot_nt(wvt_ref[...], h[rows, :]).astype(BF16)
        _decay_tile(log_f[rows, :], qk[rows, d_grp:2 * d_grp], qk[rows, 3 * d_grp:4 * d_grp], sel_ref, ind_ref,
                    carry_ref, kpre_ref, kaug_ref, stats_ref, rows, tile, n_heads)


def _head_indicator(n_heads):
    ind = np.zeros((n_heads * HEAD_DIM, LANES), np.float32)
    ind[np.arange(n_heads * HEAD_DIM), np.arange(n_heads * HEAD_DIM) // HEAD_DIM] = 1.0
    return jnp.asarray(ind, BF16)


def _decay_select_matrix(n_heads):
    sel = np.zeros((LANES, n_heads * LANES), np.float32)
    for h in range(n_heads):
        base = h * LANES + (HEAD_DIM if h % 2 == 0 else 0)
        for term in range(3):
            sel[term * n_heads + h, base + term] = 1.0
    return jnp.asarray(sel, BF16)


def _inproj(x, mod, g, w_nat, w_vt, w_gate, b_gate, n_heads):
    bsz, s, d = x.shape
    tm, tk = PROJ_ROWS, ATT_K
    n_nat, n_v = w_nat.shape[1], w_vt.shape[0]
    const = lambda b, i: (0, 0)
    resident = lambda a: pl.BlockSpec(a.shape, const, pipeline_mode=pl.Buffered(1))
    sel, ind = _decay_select_matrix(n_heads), _head_indicator(n_heads)
    return pl.pallas_call(
        functools.partial(_inproj_kernel, n_heads=n_heads),
        grid=(bsz, s // tm),
        in_specs=[pl.BlockSpec((1, tm, d), lambda b, i: (b, i, 0)),
                  pl.BlockSpec((1, N_MOD, d), lambda b, i: (b, 0, 0)),
                  pl.BlockSpec((1, d), const),
                  resident(w_nat), resident(w_vt), resident(w_gate),
                  pl.BlockSpec((1, LANES), const),
                  resident(sel), resident(ind)],
        out_specs=[pl.BlockSpec((1, tm, n_nat), lambda b, i: (b, i, 0)),
                   pl.BlockSpec((1, tm // tk, n_v, tk), lambda b, i: (b, i, 0, 0)),
                   pl.BlockSpec((1, n_heads, tm, LANES), lambda b, i: (b, 0, i, 0)),
                   pl.BlockSpec((1, n_heads // 2, tm // tk, STATS_ROWS, ATT_COLS), lambda b, i: (b, 0, i, 0, 0))],
        out_shape=[jax.ShapeDtypeStruct((bsz, s, n_nat), BF16),
                   jax.ShapeDtypeStruct((bsz, s // tk, n_v, tk), BF16),
                   jax.ShapeDtypeStruct((bsz, n_heads, s, LANES), BF16),
                   jax.ShapeDtypeStruct((bsz, n_heads // 2, s // tk, STATS_ROWS, ATT_COLS), F32)],
        scratch_shapes=[pltpu.VMEM((1, LANES), F32), pltpu.VMEM((2, LANES), F32)],
        compiler_params=_params("arbitrary", "arbitrary"),
        name="inproj",
    )(x, mod, g, w_nat, w_vt, w_gate, b_gate, sel, ind)


def _lane_queries(q_ref, extra_even, extra_odd, cw):
    out = []
    for pp in range(q_ref.shape[2] // LANES):
        q = q_ref[0, :, pp * LANES:(pp + 1) * LANES].astype(F32) * (HEAD_DIM ** -0.5 * LOG2E)
        lane = lax.broadcasted_iota(jnp.int32, q.shape, 1)
        heads = (jnp.where(lane < HEAD_DIM, q, extra_even(lane)).T.astype(BF16),
                 jnp.where(lane >= HEAD_DIM, q, extra_odd(lane)).T.astype(BF16))
        out += [heads[hh][:, c * cw:(c + 1) * cw] for hh in range(2) for c in range(q.shape[0] // cw)]
    return out


def _visibility(first_key, first_query, bk, cw, strict):
    last_visible_gap = -1 if strict else 0
    if first_key + bk - 1 - first_query <= last_visible_gap:
        return "all"
    if first_key - (first_query + cw - 1) > last_visible_gap:
        return "none"
    gap = (lax.broadcasted_iota(jnp.int32, (bk, cw), 0) - lax.broadcasted_iota(jnp.int32, (bk, cw), 1)
           + (first_key - first_query))
    return gap <= last_visible_gap


def _diag_visibility(u, c, bk, cw, strict):
    return _visibility((1 - u) * bk, c * cw, bk, cw, strict)


def _hidden(visibility):
    return isinstance(visibility, str) and visibility == "none"


def _query_norm_bounds(queries, n_chunks):
    bounds = []
    for li, q in enumerate(queries):
        hh = (li // n_chunks) % 2
        own = q[hh * HEAD_DIM:(hh + 1) * HEAD_DIM, :].astype(F32)
        bounds.append(jnp.sqrt(jnp.sum(own * own, axis=0, keepdims=True) * NORM_SLACK))
    return bounds


def _finish_heads(lanes, g_ref, o_ref):
    n_pairs = o_ref.shape[2] // LANES
    n_chunks = len(lanes) // (2 * n_pairs)
    for pp in range(n_pairs):
        mine = lanes[2 * pp * n_chunks:2 * (pp + 1) * n_chunks]
        outs = [jnp.concatenate(mine[hh * n_chunks:(hh + 1) * n_chunks], axis=1) for hh in range(2)]
        normed = [o * lax.rsqrt(jnp.mean(o * o, axis=0, keepdims=True) + EPS) for o in outs]
        cols = slice(pp * LANES, (pp + 1) * LANES)
        o_ref[0, :, cols] = (jnp.concatenate(normed, axis=0).T * g_ref[:, cols]).astype(o_ref.dtype)


def _fox_kernel(q_ref, k_ref, vt_ref, g_ref, stats_ref, o_ref, s_buf, cmax_buf, p_buf, acc_buf):
    bk = vt_ref.shape[3]
    n_lanes, cw = acc_buf.shape[0], acc_buf.shape[2]
    n_chunks = q_ref.shape[1] // cw
    chunks_per_sub = ATT_Q // cw
    assert q_ref.shape[1] == 2 * ATT_Q
    lane_group = lambda li: (li // (2 * n_chunks), (li // n_chunks) % 2, li % n_chunks)
    first_q_block = 2 * pl.program_id(2)
    n_tiles = 2 * (first_q_block + 1)
    ones3 = lambda lo: (lambda lane: jnp.where((lane >= lo) & (lane < lo + 3), 1.0, 0.0))
    queries = _lane_queries(q_ref, ones3(HEAD_DIM), ones3(0), cw)
    acc_buf[...] = jnp.zeros(acc_buf.shape, F32)
    for li in range(n_lanes):
        if _hidden(_diag_visibility(0, lane_group(li)[2] % chunks_per_sub, bk, cw, strict=False)):
            p_buf[0, li] = jnp.zeros((bk, cw), BF16)

    full, idle = ("below", "below", True), (None, None, False)

    def step(t, slot, carry, stages=(full, full)):
        new = []
        for li in range(n_lanes):
            pp, hh, chunk = lane_group(li)
            sub, c = divmod(chunk, chunks_per_sub)
            score, softmax, value = stages[sub]
            see = lambda u: "all" if u == "below" else _diag_visibility(u, c, bk, cw, strict=False)
            if score is not None and not _hidden(see(score)):
                start = pl.multiple_of((n_tiles - 2 - t) * bk, bk)
                s_new = _dot(k_ref[0, 2 * pp + hh, pl.ds(start, bk), :], queries[li])
                if not isinstance(see(score), str):
                    s_new = jnp.where(see(score), s_new, MASKED)
                s_buf[1 - slot, li] = s_new
                cmax_buf[1 - slot, li] = jnp.max(s_new, axis=0, keepdims=True)
            pv = None
            if value:
                vt = vt_ref[0, n_tiles - t, pl.ds(pp * LANES + hh * HEAD_DIM, HEAD_DIM), :]
                pv = _dot(vt, p_buf[1 - slot, li])
            m, l = carry[li]
            if softmax is not None and not _hidden(see(softmax)):
                m_new = jnp.maximum(m, cmax_buf[slot, li])
                alpha = jnp.exp2(m - m_new)
                p = jnp.exp2(s_buf[slot, li] - m_new)
                p_buf[slot, li] = p.astype(BF16)
                m, l = m_new, alpha * l + jnp.sum(p, axis=0, keepdims=True)
                acc_buf[li] = alpha * (acc_buf[li] if pv is None else acc_buf[li] + pv)
            elif pv is not None:
                acc_buf[li] += pv
            new.append((m, l))
        return tuple(new)

    def step_pair(i, carry):
        t = 2 * i + 1
        return step(t + 1, 0, step(t, 1, carry))

    q_norm = _query_norm_bounds(queries, n_chunks)

    def later_tiles_matter(i, carry):
        j_rest = jnp.maximum(n_tiles - 5 - 2 * i, 0)
        worst = None
        for li in range(n_lanes):
            pp, hh, _ = lane_group(li)
            bound = (q_norm[li] * stats_ref[0, pp, j_rest, hh:hh + 1, :]
                     + stats_ref[0, pp, j_rest, 2 + hh:3 + hh, :] - carry[li][0])
            worst = bound if worst is None else jnp.maximum(worst, bound)
        return jnp.max(worst) >= -PRUNE_LOG2

    def pair_and_check(state):
        i, _, carry = state
        carry = step_pair(i, carry)
        return i + 1, later_tiles_matter(i, carry), carry

    carry = tuple((jnp.full((1, cw), M_INIT, F32), jnp.zeros((1, cw), F32)) for _ in range(n_lanes))
    carry = step(-3, 1, carry, (idle, (0, None, False)))
    carry = step(-2, 0, carry, (idle, (1, 0, False)))
    carry = step(-1, 1, carry, ((0, None, False), full))
    carry = step(0, 0, carry, ((1, 0, False), full))
    n_pairs, _, carry = lax.while_loop(lambda st: (st[0] < first_q_block) & st[1], pair_and_check,
                                       (jnp.int32(0), jnp.bool_(True), carry))
    drain = (None, "below", True)
    carry = step(2 * n_pairs + 1, 1, carry, (drain, drain))
    drain = (None, None, True)
    carry = step(2 * n_pairs + 2, 0, carry, (drain, drain))
    _finish_heads([acc_buf[li] / carry[li][1] for li in range(n_lanes)], g_ref, o_ref)


def _sb_kernel(q_ref, k_ref, vt_ref, g_ref, stats_ref, o_ref, z_buf, sp_buf, e_buf, wrow_buf, later_buf, acc_buf):
    qi = pl.program_id(2)
    bk = vt_ref.shape[3]
    n_wide, n_first = z_buf.shape[0], SB_WIDE_TILES
    n_lanes, cw = acc_buf.shape[0], acc_buf.shape[2]
    n_chunks = q_ref.shape[1] // cw
    lane_group = lambda li: (li // (2 * n_chunks), (li // n_chunks) % 2, li % n_chunks)
    n_tiles = 2 * (qi + 1)
    zero = lambda lane: 0.0
    queries = _lane_queries(q_ref, zero, zero, cw)
    q_norm = _query_norm_bounds(queries, n_chunks)
    suffix = (lax.broadcasted_iota(jnp.int32, (bk, bk), 1)
              >= lax.broadcasted_iota(jnp.int32, (bk, bk), 0)).astype(BF16)
    softplus2 = lambda z: jnp.maximum(z, jnp.log2(1.0 + jnp.exp2(jnp.minimum(z, EXP2_MAX))))
    keys = lambda j, pp: k_ref[0, pl.ds(pl.multiple_of(j * bk, bk), bk), pp * LANES:(pp + 1) * LANES]
    values = lambda j, pp, hh: vt_ref[0, j, pl.ds(pp * LANES + hh * HEAD_DIM, HEAD_DIM), :]

    sees = lambda u, li: _diag_visibility(u, lane_group(li)[2], bk, cw, strict=True) if u < 2 else "all"

    def wide_block(tiles, later, check_exists):
        live = [(u, li) for u in tiles for li in range(n_lanes) if not _hidden(sees(u, li))]
        later, col_sums = list(later), {}

        def score(u, li):
            z = _dot(keys(jnp.maximum(n_tiles - 1 - u, 0), lane_group(li)[0]), queries[li])
            if not isinstance(sees(u, li), str):
                z = jnp.where(sees(u, li), z, MASKED)
            if check_exists and u >= 2:
                z = jnp.where(u < n_tiles, z, MASKED)
            z_buf[u, li] = z

        def softplus(u, li):
            sp_buf[u, li] = softplus2(z_buf[u, li]).astype(BF16)

        def cumsum(u, li):
            within = _dot(suffix, sp_buf[u, li])
            col_sums[u, li] = within[0:1, :]
            z_buf[u, li] = z_buf[u, li] - within

        def weight(u, li):
            pp, hh, _ = lane_group(li)
            a = jnp.exp2(z_buf[u, li] - later[li])
            acc_buf[li] += _dot(values(jnp.maximum(n_tiles - 1 - u, 0), pp, hh), a.astype(BF16))
            later[li] = later[li] + col_sums[u, li]

        stages = (score, softplus, cumsum, weight)
        for pos in range(len(live) + len(stages) - 1):
            for lag, stage in enumerate(stages):
                if 0 <= pos - lag < len(live):
                    stage(*live[pos - lag])
        return later

    n_rest = n_tiles - n_wide

    def rest_matters(first_unscored, mass):
        j_rest = jnp.maximum(n_rest - 1 - first_unscored, 0)
        worst = None
        for li in range(n_lanes):
            pp, hh, _ = lane_group(li)
            bound = q_norm[li] * stats_ref[0, pp, j_rest, 4 + hh:5 + hh, :] - mass[li]
            worst = bound if worst is None else jnp.maximum(worst, bound)
        return jnp.max(worst) >= -PRUNE_LOG2

    def step(t, slot, later, score=True, softplus=True, cumsum=True, weight=True):
        new_later = []
        for li in range(n_lanes):
            pp, hh, _ = lane_group(li)
            if cumsum:
                within = _dot(suffix, sp_buf[1 - slot, li])
                e_buf[1 - slot, li] = z_buf[1 - slot, li] - within
                wrow_buf[1 - slot, li] = within[0:1, :]
            if score:
                z_buf[1 - slot, li] = _dot(keys(n_rest - 1 - (t + 3), pp), queries[li])
            if weight:
                a = jnp.exp2(e_buf[slot, li] - later[li])
                acc_buf[li] += _dot(values(n_rest - 1 - jnp.maximum(t, 0), pp, hh), a.astype(BF16))
                new_later.append(later[li] + wrow_buf[slot, li])
            else:
                new_later.append(later[li])
            if softplus:
                sp_buf[slot, li] = softplus2(z_buf[slot, li]).astype(BF16)
        return tuple(new_later)

    def pair_and_check(state):
        i, _, later = state
        t = 2 * i - 1
        later = step(t + 1, 0, step(t, 1, later))
        mass = [later[li] + wrow_buf[1, li] for li in range(n_lanes)]
        return i + 1, rest_matters(2 * i + 4, mass), later

    acc_buf[...] = jnp.zeros(acc_buf.shape, F32)
    first = wide_block(range(n_first), [jnp.zeros((1, cw), F32)] * n_lanes, check_exists=True)
    for li in range(n_lanes):
        later_buf[li] = first[li]
    swept = lambda: [later_buf[li] for li in range(n_lanes)]
    more_matters = (n_rest + (n_wide - n_first) > 0) & rest_matters(n_first - n_wide, first)

    @pl.when(more_matters)
    def _():
        more = wide_block(range(n_first, n_wide), first, check_exists=False)
        for li in range(n_lanes):
            later_buf[li] = more[li]

    @pl.when(more_matters & (n_rest > 0) & rest_matters(0, swept()))
    def _():
        later = swept()
        e_buf[1] = jnp.full(e_buf.shape[1:], MASKED, F32)
        wrow_buf[1] = jnp.zeros(wrow_buf.shape[1:], F32)
        mass = step(-3, 1, tuple(later), softplus=False, cumsum=False, weight=False)
        mass = step(-2, 0, mass, cumsum=False, weight=False)
        n_pairs, _, mass = lax.while_loop(lambda st: (2 * st[0] + 2 < n_rest) & st[1], pair_and_check,
                                          (jnp.int32(0), jnp.bool_(True), mass))
        mass = step(2 * n_pairs - 1, 1, mass, score=False)
        mass = step(2 * n_pairs, 0, mass, score=False, softplus=False)
        step(2 * n_pairs + 1, 1, mass, score=False, softplus=False, cumsum=False)

    _finish_heads([acc_buf[li] for li in range(n_lanes)], g_ref, o_ref)


def _attention(body, name, pairs, bq, scratch, qk, k_arr, k_spec, vt, g, stats, q_block0, vt_block0, n_heads):
    bsz, s, _ = qk.shape
    bk, width = ATT_K, pairs * LANES
    d_grp = n_heads * HEAD_DIM
    assert (n_heads // 2) % pairs == 0
    return pl.pallas_call(
        body,
        grid=(bsz, n_heads // 2 // pairs, s // bq),
        in_specs=[pl.BlockSpec((1, bq, width), lambda b, p, i: (b, i, q_block0 + p)),
                  k_spec,
                  pl.BlockSpec((1, s // bk, width, bk), lambda b, p, i: (b, 0, vt_block0 + p, 0)),
                  pl.BlockSpec((1, width), lambda b, p, i: (0, p)),
                  pl.BlockSpec((1, pairs) + stats.shape[2:], lambda b, p, i: (b, p, 0, 0, 0))],
        out_specs=pl.BlockSpec((1, bq, width), lambda b, p, i: (b, i, p)),
        out_shape=jax.ShapeDtypeStruct((bsz, s, d_grp), BF16),
        scratch_shapes=scratch,
        compiler_params=_params("arbitrary", "arbitrary", "arbitrary"),
        name=name,
    )(qk, k_arr, vt, g.reshape(1, d_grp), stats)


def _lane_groups(pairs, bq):
    return pairs * 2 * (bq // ATT_COLS)


def _fox_scratch(pairs):
    n = _lane_groups(pairs, FOX_Q)
    return [pltpu.VMEM((2, n, ATT_K, ATT_COLS), F32), pltpu.VMEM((2, n, 1, ATT_COLS), F32),
            pltpu.VMEM((2, n, ATT_K, ATT_COLS), BF16), pltpu.VMEM((n, HEAD_DIM, ATT_COLS), F32)]


def _sb_scratch(pairs):
    n = _lane_groups(pairs, ATT_Q)
    wide = SB_WIDE_TILES + SB_MORE_TILES
    return [pltpu.VMEM((wide, n, ATT_K, ATT_COLS), F32), pltpu.VMEM((wide, n, ATT_K, ATT_COLS), BF16),
            pltpu.VMEM((2, n, ATT_K, ATT_COLS), F32), pltpu.VMEM((2, n, 1, ATT_COLS), F32),
            pltpu.VMEM((n, 1, ATT_COLS), F32), pltpu.VMEM((n, HEAD_DIM, ATT_COLS), F32)]


def _mixer_kernel(x_ref, xh_ref, mf_ref, mfh_ref, ms_ref, msh_ref, mod_ref, wo_ref, gm_ref, wu_ref, cw_ref,
                  cb_ref, wd_ref, gf_ref, o_ref, u_buf, acc_ref, x1_buf, *, final_norm):
    i = pl.program_id(1)
    tm = x_ref.shape[1]
    n_chunks = wd_ref.shape[0]
    mix = jnp.concatenate([jnp.concatenate([mfh_ref[0], msh_ref[0]], axis=-1),
                           jnp.concatenate([mf_ref[0], ms_ref[0]], axis=-1)], axis=0)
    x_ext = jnp.concatenate([xh_ref[0], x_ref[0]], axis=0)
    x1_ext = x_ext + mod_ref[0, 2:3, :] * _dot(mix, wo_ref[...])
    shift = mod_ref[0, 3:4, :]
    scale = mod_ref[0, 4:5, :]
    h_ext = _rms_rows(x1_ext) * gm_ref[...] * (1.0 + scale) + shift
    row = lax.broadcasted_iota(jnp.int32, h_ext.shape, 0)
    hx = jnp.where((row >= BF16_SUBLANES) | (i > 0), h_ext, 0.0).astype(BF16)
    x1_buf[...] = x1_ext[BF16_SUBLANES:, :]
    acc_ref[...] = jnp.zeros_like(acc_ref)

    def project_up(c, slot):
        for br in range(2):
            u_buf[slot, br] = _dot(hx, wu_ref[br, c])

    def mix_down(c, slot):
        branches = []
        for br in range(2):
            out = cb_ref[br, c]
            for tap in range(CONV_WIDTH):
                first = BF16_SUBLANES - (CONV_WIDTH - 1 - tap)
                out = out + cw_ref[br, c, tap:tap + 1, :] * u_buf[slot, br, pl.ds(first, tm), :]
            branches.append(out)
        u_gate, u_val = branches
        acc_ref[...] += _dot((u_gate * jax.nn.sigmoid(u_gate) * u_val).astype(BF16), wd_ref[c])

    def chunk_pair(j, _):
        c = 2 * j
        project_up(c + 1, 1)
        mix_down(c, 0)
        project_up(c + 2, 0)
        mix_down(c + 1, 1)
        return 0

    project_up(0, 0)
    lax.fori_loop(0, (n_chunks - 1) // 2, chunk_pair, 0)
    mix_down(n_chunks - 1, 0)
    x2 = x1_buf[...] + mod_ref[0, 5:6, :] * acc_ref[...]
    o_ref[0] = _rms_rows(x2) * gf_ref[...] if final_norm else x2


def _mixer(x, mix_f, mix_s, mod, w_out, g_mlp, w_up, conv_w, conv_b, w_down, g_final, final_norm):
    bsz, s, d = x.shape
    tm = OUT_ROWS
    n_chunks, tf = w_down.shape[0], w_down.shape[1]
    assert n_chunks % 2 == 1
    halo_blocks = tm // BF16_SUBLANES
    row = lambda b, i: (b, i, 0)
    halo = lambda b, i: (b, jnp.maximum(i * halo_blocks - 1, 0), 0)
    tile_and_halo = lambda a: [pl.BlockSpec((1, tm, a.shape[2]), row),
                               pl.BlockSpec((1, BF16_SUBLANES, a.shape[2]), halo)]
    resident = lambda a: pl.BlockSpec(a.shape, lambda b, i: (0,) * a.ndim, pipeline_mode=pl.Buffered(1))
    return pl.pallas_call(
        functools.partial(_mixer_kernel, final_norm=final_norm),
        grid=(bsz, s // tm),
        in_specs=tile_and_halo(x) + tile_and_halo(mix_f) + tile_and_halo(mix_s)
                 + [pl.BlockSpec((1, N_MOD, d), lambda b, i: (b, 0, 0)), resident(w_out),
                    pl.BlockSpec((1, d), lambda b, i: (0, 0)),
                    resident(w_up), resident(conv_w), resident(conv_b), resident(w_down),
                    pl.BlockSpec((1, d), lambda b, i: (0, 0))],
        out_specs=pl.BlockSpec((1, tm, d), row),
        out_shape=jax.ShapeDtypeStruct((bsz, s, d), F32),
        scratch_shapes=[pltpu.VMEM((2, 2, tm + BF16_SUBLANES, tf), F32), pltpu.VMEM((tm, d), F32),
                        pltpu.VMEM((tm, d), F32)],
        compiler_params=_params("arbitrary", "arbitrary"),
        name="mixer",
    )(x, x, mix_f, mix_f, mix_s, mix_s, mod, w_out, g_mlp, w_up, conv_w, conv_b, w_down, g_final)


def _chunk_columns(a, d_ff, n_chunks, tf):
    halves = jnp.stack([a[:, :d_ff], a[:, d_ff:]])
    halves = jnp.pad(halves, ((0, 0), (0, 0), (0, n_chunks * tf - d_ff)))
    return halves.reshape(2, a.shape[0], n_chunks, tf).transpose(0, 2, 1, 3)


def _pad_cols(a, n):
    return jnp.pad(a, ((0, 0), (0, n - a.shape[1])))


def kernel(x, c, w_ada, b_ada, g_attn, w_in, b_fgate, g_out_fox, g_out_sb, w_out,
           g_mlp, w_up, conv_w, conv_b, w_down, g_final):
    depth, d, _ = w_ada.shape
    n_fox = b_fgate.shape[1]
    d_fox = n_fox * HEAD_DIM
    d_sb = g_out_sb.shape[1]
    n_sb = d_sb // HEAD_DIM
    d_ff = w_down.shape[1]
    d_ff_pad = -(-d_ff // FF_CHUNK) * FF_CHUNK
    assert n_fox % 2 == 0 and n_sb == n_fox and 3 * n_fox <= LANES
    assert x.shape[1] % OUT_ROWS == 0 and x.shape[1] % FOX_Q == 0 and ATT_Q == 2 * ATT_K
    o_kf, o_vf, o_qs, o_ks, o_vs, o_gate = (d_fox, 2 * d_fox, 3 * d_fox, 3 * d_fox + d_sb,
                                             3 * d_fox + 2 * d_sb, 3 * d_fox + 3 * d_sb)

    for l in range(depth):
        mod = _ada(c, w_ada[l], b_ada[l]).reshape(-1, N_MOD, d)
        order = jnp.argsort(b_fgate[l])
        fox_cols = (order[:, None] * HEAD_DIM + jnp.arange(HEAD_DIM)[None, :]).reshape(-1)
        sb_cols = jnp.arange(d_sb)
        wt = w_in[l].T
        w_nat = wt[jnp.concatenate([fox_cols, o_kf + fox_cols, o_qs + sb_cols, o_ks + sb_cols])].T.astype(BF16)
        w_vt = wt[jnp.concatenate([o_vf + fox_cols, o_vs + sb_cols])].astype(BF16)
        w_gate = _pad_cols(wt[o_gate + order].T, LANES).astype(BF16)
        b_gate = _pad_cols(b_fgate[l][order].reshape(1, n_fox), LANES)
        qk, vt, k_aug, stats = _inproj(x, mod, g_attn[l].reshape(1, d), w_nat, w_vt, w_gate, b_gate, n_fox)
        w_mix = w_out[l][jnp.concatenate([fox_cols, d_fox + sb_cols])].astype(BF16)

        steps_f, steps_s = n_fox // 2 // FOX_PAIRS, n_sb // 2 // SB_PAIRS
        fox_k_spec = pl.BlockSpec((1, 2 * FOX_PAIRS, x.shape[1], LANES), lambda b, p, i: (b, p, 0, 0))
        mix_f = _attention(_fox_kernel, "fox", FOX_PAIRS, FOX_Q, _fox_scratch(FOX_PAIRS), qk, k_aug, fox_k_spec, vt,
                           g_out_fox[l][fox_cols], stats, q_block0=0, vt_block0=0, n_heads=n_fox)
        sb_k_spec = pl.BlockSpec((1, x.shape[1], SB_PAIRS * LANES), lambda b, p, i: (b, 0, 3 * steps_s + p))
        mix_s = _attention(_sb_kernel, "sb", SB_PAIRS, ATT_Q, _sb_scratch(SB_PAIRS), qk, qk, sb_k_spec, vt,
                           g_out_sb[l], stats, q_block0=2 * steps_s, vt_block0=steps_s, n_heads=n_sb)

        n_ff = d_ff_pad // FF_CHUNK
        x = _mixer(x, mix_f, mix_s, mod, w_mix, g_mlp[l].reshape(1, d),
                   _chunk_columns(w_up[l], d_ff, n_ff, FF_CHUNK).astype(BF16),
                   _chunk_columns(conv_w[l], d_ff, n_ff, FF_CHUNK),
                   _chunk_columns(conv_b[l].reshape(1, -1), d_ff, n_ff, FF_CHUNK),
                   jnp.pad(w_down[l], ((0, d_ff_pad - d_ff), (0, 0))).astype(BF16).reshape(n_ff, FF_CHUNK, d),
                   g_final.reshape(1, d), final_norm=(l == depth - 1))
    return x
```

```python
import functools

import numpy as np
import jax
import jax.numpy as jnp
from jax import lax
from jax.experimental import pallas as pl
from jax.experimental.pallas import tpu as pltpu

HEAD_DIM = 64
N_MOD = 6
CONV_WIDTH = 3
EPS = 1e-6

LANES = 128
BF16_SUBLANES = 16
VMEM_LIMIT_BYTES = 48 * 1024 * 1024

ATT_Q = 512
FOX_Q = 2 * ATT_Q
ATT_K = 256
ATT_COLS = 256
FOX_PAIRS = 1
SB_PAIRS = 2
SB_WIDE_TILES = 4
SB_MORE_TILES = 2
PROJ_ROWS = 2 * ATT_K
LOG2E = 1.4426950408889634
MASKED = -1e30
M_INIT = -1e29
EXP2_MAX = 126.0
PRUNE_LOG2 = 160.0
NORM_SLACK = 1.02
STATS_ROWS = 8
OUT_ROWS = 512
FF_CHUNK = 256

F32 = jnp.float32
BF16 = jnp.bfloat16
NT_DIMS = (((1,), (1,)), ((), ()))


def _dot(a, b):
    return jnp.dot(a, b, preferred_element_type=F32)


def _dot_nt(a, b):
    return lax.dot_general(a, b, NT_DIMS, preferred_element_type=F32)


def _params(*sem):
    return pltpu.CompilerParams(dimension_semantics=sem, vmem_limit_bytes=VMEM_LIMIT_BYTES)


def _rms_rows(x):
    return x * lax.rsqrt(jnp.mean(x * x, axis=-1, keepdims=True) + EPS)


def _softplus(z):
    return jnp.maximum(z, 0.0) + jnp.log(1.0 + jnp.exp(-jnp.abs(z)))


def _split3(x):
    hi = x.astype(BF16)
    r1 = x - hi.astype(F32)
    mid = r1.astype(BF16)
    lo = (r1 - mid.astype(F32)).astype(BF16)
    return hi, mid, lo


def _ada_kernel(c_ref, w_ref, b_ref, o_ref):
    c = c_ref[...]
    o_ref[...] = _dot(c * jax.nn.sigmoid(c), w_ref[...]) + b_ref[...]


def _ada(c, w, b):
    bsz, d = c.shape
    n = w.shape[1]
    return pl.pallas_call(
        _ada_kernel,
        grid=(n // d,),
        in_specs=[pl.BlockSpec((bsz, d), lambda j: (0, 0)),
                  pl.BlockSpec((d, d), lambda j: (0, j)),
                  pl.BlockSpec((1, d), lambda j: (0, j))],
        out_specs=pl.BlockSpec((bsz, d), lambda j: (0, j)),
        out_shape=jax.ShapeDtypeStruct((bsz, n), F32),
        compiler_params=_params("arbitrary"),
        name="ada",
    )(c, w, b.reshape(1, n))


def _decay_tile(lf, k_fox, k_sb, sel_ref, ind_ref, carry_ref, kpre_ref, kaug_ref, stats_ref, rows, tile, n_heads):
    tk = lf.shape[0]
    lane = lax.broadcasted_iota(jnp.int32, (tk, LANES), 1)
    lf = jnp.where(lane < n_heads, lf, 0.0)
    row = lax.broadcasted_iota(jnp.int32, (tk, tk), 0)
    col = lax.broadcasted_iota(jnp.int32, (tk, tk), 1)
    tri = (col <= row).astype(BF16)
    hi, mid, lo = _split3(lf)
    f_run = carry_ref[...] + (_dot(tri, hi) + _dot(tri, mid) + _dot(tri, lo))
    carry_ref[...] = f_run[tk - 1:tk, :]
    ghi, gmid, glo = _split3(-LOG2E * f_run)
    packed = (ghi.astype(F32) + pltpu.roll(gmid.astype(F32), n_heads, 1)
              + pltpu.roll(glo.astype(F32), 2 * n_heads, 1)).astype(BF16)
    placed = _dot(packed, sel_ref[...])
    for h in range(n_heads):
        k_pair = k_fox[:, (h // 2) * LANES:(h // 2 + 1) * LANES]
        own = (lane < HEAD_DIM) if h % 2 == 0 else (lane >= HEAD_DIM)
        kaug_ref[0, h, rows, :] = jnp.where(own, k_pair, placed[:, h * LANES:(h + 1) * LANES].astype(BF16))

    def head_norm_bound(k):
        k32 = k.astype(F32)
        sq = _dot((k32 * k32).astype(BF16), ind_ref[...])
        return jnp.sqrt(jnp.max(sq, axis=0, keepdims=True) * NORM_SLACK)

    kpre_f = jnp.maximum(kpre_ref[0:1, :], head_norm_bound(k_fox))
    kpre_s = jnp.maximum(kpre_ref[1:2, :], head_norm_bound(k_sb))
    kpre_ref[0:1, :] = kpre_f
    kpre_ref[1:2, :] = kpre_s
    g_end = -LOG2E * f_run[tk - 1:tk, :]
    lane1 = lax.broadcasted_iota(jnp.int32, (1, LANES), 1)

    def spread(v, h):
        return jnp.broadcast_to(jnp.sum(jnp.where(lane1 == h, v, 0.0), axis=1, keepdims=True), (1, ATT_COLS))

    for p in range(n_heads // 2):
        srows = [spread(v, 2 * p + hh) for v in (kpre_f, g_end, kpre_s) for hh in range(2)]
        srows += [jnp.zeros((1, ATT_COLS), F32)] * (stats_ref.shape[3] - len(srows))
        stats_ref[0, p, tile] = jnp.concatenate(srows, axis=0)


def _inproj_kernel(x_ref, mod_ref, g_ref, wn_ref, wvt_ref, wg_ref, bg_ref, sel_ref, ind_ref,
                   qk_ref, vt_ref, kaug_ref, stats_ref, carry_ref, kpre_ref, *, n_heads):
    @pl.when(pl.program_id(1) == 0)
    def _():
        carry_ref[...] = jnp.zeros_like(carry_ref)
        kpre_ref[...] = jnp.zeros_like(kpre_ref)

    shift = mod_ref[0, 0:1, :]
    scale = mod_ref[0, 1:2, :]
    h = (_rms_rows(x_ref[0]) * g_ref[...] * (1.0 + scale) + shift).astype(BF16)
    qk = _dot(h, wn_ref[...]).astype(BF16)
    qk_ref[0] = qk
    logit = _dot(h, wg_ref[...]) + bg_ref[...]
    log_f = -_softplus(-logit)
    d_grp = n_heads * HEAD_DIM
    tk = vt_ref.shape[3]
    for tile in range(vt_ref.shape[1]):
        rows = slice(tile * tk, (tile + 1) * tk)
        vt_ref[0, tile] = _dot_nt(wvt_ref[...], h[rows, :]).astype(BF16)
        _decay_tile(log_f[rows, :], qk[rows, d_grp:2 * d_grp], qk[rows, 3 * d_grp:4 * d_grp], sel_ref, ind_ref,
                    carry_ref, kpre_ref, kaug_ref, stats_ref, rows, tile, n_heads)


def _head_indicator(n_heads):
    ind = np.zeros((n_heads * HEAD_DIM, LANES), np.float32)
    ind[np.arange(n_heads * HEAD_DIM), np.arange(n_heads * HEAD_DIM) // HEAD_DIM] = 1.0
    return jnp.asarray(ind, BF16)


def _decay_select_matrix(n_heads):
    sel = np.zeros((LANES, n_heads * LANES), np.float32)
    for h in range(n_heads):
        base = h * LANES + (HEAD_DIM if h % 2 == 0 else 0)
        for term in range(3):
            sel[term * n_heads + h, base + term] = 1.0
    return jnp.asarray(sel, BF16)


def _inproj(x, mod, g, w_nat, w_vt, w_gate, b_gate, n_heads):
    bsz, s, d = x.shape
    tm, tk = PROJ_ROWS, ATT_K
    n_nat, n_v = w_nat.shape[1], w_vt.shape[0]
    const = lambda b, i: (0, 0)
    resident = lambda a: pl.BlockSpec(a.shape, const, pipeline_mode=pl.Buffered(1))
    sel, ind = _decay_select_matrix(n_heads), _head_indicator(n_heads)
    return pl.pallas_call(
        functools.partial(_inproj_kernel, n_heads=n_heads),
        grid=(bsz, s // tm),
        in_specs=[pl.BlockSpec((1, tm, d), lambda b, i: (b, i, 0)),
                  pl.BlockSpec((1, N_MOD, d), lambda b, i: (b, 0, 0)),
                  pl.BlockSpec((1, d), const),
                  resident(w_nat), resident(w_vt), resident(w_gate),
                  pl.BlockSpec((1, LANES), const),
                  resident(sel), resident(ind)],
        out_specs=[pl.BlockSpec((1, tm, n_nat), lambda b, i: (b, i, 0)),
                   pl.BlockSpec((1, tm // tk, n_v, tk), lambda b, i: (b, i, 0, 0)),
                   pl.BlockSpec((1, n_heads, tm, LANES), lambda b, i: (b, 0, i, 0)),
                   pl.BlockSpec((1, n_heads // 2, tm // tk, STATS_ROWS, ATT_COLS), lambda b, i: (b, 0, i, 0, 0))],
        out_shape=[jax.ShapeDtypeStruct((bsz, s, n_nat), BF16),
                   jax.ShapeDtypeStruct((bsz, s // tk, n_v, tk), BF16),
                   jax.ShapeDtypeStruct((bsz, n_heads, s, LANES), BF16),
                   jax.ShapeDtypeStruct((bsz, n_heads // 2, s // tk, STATS_ROWS, ATT_COLS), F32)],
        scratch_shapes=[pltpu.VMEM((1, LANES), F32), pltpu.VMEM((2, LANES), F32)],
        compiler_params=_params("arbitrary", "arbitrary"),
        name="inproj",
    )(x, mod, g, w_nat, w_vt, w_gate, b_gate, sel, ind)


def _lane_queries(q_ref, extra_even, extra_odd, cw):
    out = []
    for pp in range(q_ref.shape[2] // LANES):
        q = q_ref[0, :, pp * LANES:(pp + 1) * LANES].astype(F32) * (HEAD_DIM ** -0.5 * LOG2E)
        lane = lax.broadcasted_iota(jnp.int32, q.shape, 1)
        heads = (jnp.where(lane < HEAD_DIM, q, extra_even(lane)).T.astype(BF16),
                 jnp.where(lane >= HEAD_DIM, q, extra_odd(lane)).T.astype(BF16))
        out += [heads[hh][:, c * cw:(c + 1) * cw] for hh in range(2) for c in range(q.shape[0] // cw)]
    return out


def _visibility(first_key, first_query, bk, cw, strict):
    last_visible_gap = -1 if strict else 0
    if first_key + bk - 1 - first_query <= last_visible_gap:
        return "all"
    if first_key - (first_query + cw - 1) > last_visible_gap:
        return "none"
    gap = (lax.broadcasted_iota(jnp.int32, (bk, cw), 0) - lax.broadcasted_iota(jnp.int32, (bk, cw), 1)
           + (first_key - first_query))
    return gap <= last_visible_gap


def _diag_visibility(u, c, bk, cw, strict):
    return _visibility((1 - u) * bk, c * cw, bk, cw, strict)


def _hidden(visibility):
    return isinstance(visibility, str) and visibility == "none"


def _query_norm_bounds(queries, n_chunks):
    bounds = []
    for li, q in enumerate(queries):
        hh = (li // n_chunks) % 2
        own = q[hh * HEAD_DIM:(hh + 1) * HEAD_DIM, :].astype(F32)
        bounds.append(jnp.sqrt(jnp.sum(own * own, axis=0, keepdims=True) * NORM_SLACK))
    return bounds


def _finish_heads(lanes, g_ref, o_ref):
    n_pairs = o_ref.shape[2] // LANES
    n_chunks = len(lanes) // (2 * n_pairs)
    for pp in range(n_pairs):
        mine = lanes[2 * pp * n_chunks:2 * (pp + 1) * n_chunks]
        outs = [jnp.concatenate(mine[hh * n_chunks:(hh + 1) * n_chunks], axis=1) for hh in range(2)]
        normed = [o * lax.rsqrt(jnp.mean(o * o, axis=0, keepdims=True) + EPS) for o in outs]
        cols = slice(pp * LANES, (pp + 1) * LANES)
        o_ref[0, :, cols] = (jnp.concatenate(normed, axis=0).T * g_ref[:, cols]).astype(o_ref.dtype)


def _fox_kernel(q_ref, k_ref, vt_ref, g_ref, stats_ref, o_ref, s_buf, cmax_buf, p_buf, acc_buf):
    bk = vt_ref.shape[3]
    n_lanes, cw = acc_buf.shape[0], acc_buf.shape[2]
    n_chunks = q_ref.shape[1] // cw
    chunks_per_sub = ATT_Q // cw
    assert q_ref.shape[1] == 2 * ATT_Q
    lane_group = lambda li: (li // (2 * n_chunks), (li // n_chunks) % 2, li % n_chunks)
    first_q_block = 2 * pl.program_id(2)
    n_tiles = 2 * (first_q_block + 1)
    ones3 = lambda lo: (lambda lane: jnp.where((lane >= lo) & (lane < lo + 3), 1.0, 0.0))
    queries = _lane_queries(q_ref, ones3(HEAD_DIM), ones3(0), cw)
    acc_buf[...] = jnp.zeros(acc_buf.shape, F32)
    for li in range(n_lanes):
        if _hidden(_diag_visibility(0, lane_group(li)[2] % chunks_per_sub, bk, cw, strict=False)):
            p_buf[0, li] = jnp.zeros((bk, cw), BF16)

    full, idle = ("below", "below", True), (None, None, False)

    def step(t, slot, carry, stages=(full, full)):
        new = []
        for li in range(n_lanes):
            pp, hh, chunk = lane_group(li)
            sub, c = divmod(chunk, chunks_per_sub)
            score, softmax, value = stages[sub]
            see = lambda u: "all" if u == "below" else _diag_visibility(u, c, bk, cw, strict=False)
            if score is not None and not _hidden(see(score)):
                start = pl.multiple_of((n_tiles - 2 - t) * bk, bk)
                s_new = _dot(k_ref[0, 2 * pp + hh, pl.ds(start, bk), :], queries[li])
                if not isinstance(see(score), str):
                    s_new = jnp.where(see(score), s_new, MASKED)
                s_buf[1 - slot, li] = s_new
                cmax_buf[1 - slot, li] = jnp.max(s_new, axis=0, keepdims=True)
            pv = None
            if value:
                vt = vt_ref[0, n_tiles - t, pl.ds(pp * LANES + hh * HEAD_DIM, HEAD_DIM), :]
                pv = _dot(vt, p_buf[1 - slot, li])
            m, l = carry[li]
            if softmax is not None and not _hidden(see(softmax)):
                m_new = jnp.maximum(m, cmax_buf[slot, li])
                alpha = jnp.exp2(m - m_new)
                p = jnp.exp2(s_buf[slot, li] - m_new)
                p_buf[slot, li] = p.astype(BF16)
                m, l = m_new, alpha * l + jnp.sum(p, axis=0, keepdims=True)
                acc_buf[li] = alpha * (acc_buf[li] if pv is None else acc_buf[li] + pv)
            elif pv is not None:
                acc_buf[li] += pv
            new.append((m, l))
        return tuple(new)

    def step_pair(i, carry):
        t = 2 * i + 1
        return step(t + 1, 0, step(t, 1, carry))

    q_norm = _query_norm_bounds(queries, n_chunks)

    def later_tiles_matter(i, carry):
        j_rest = jnp.maximum(n_tiles - 5 - 2 * i, 0)
        worst = None
        for li in range(n_lanes):
            pp, hh, _ = lane_group(li)
            bound = (q_norm[li] * stats_ref[0, pp, j_rest, hh:hh + 1, :]
                     + stats_ref[0, pp, j_rest, 2 + hh:3 + hh, :] - carry[li][0])
            worst = bound if worst is None else jnp.maximum(worst, bound)
        return jnp.max(worst) >= -PRUNE_LOG2

    def pair_and_check(state):
        i, _, carry = state
        carry = step_pair(i, carry)
        return i + 1, later_tiles_matter(i, carry), carry

    carry = tuple((jnp.full((1, cw), M_INIT, F32), jnp.zeros((1, cw), F32)) for _ in range(n_lanes))
    carry = step(-3, 1, carry, (idle, (0, None, False)))
    carry = step(-2, 0, carry, (idle, (1, 0, False)))
    carry = step(-1, 1, carry, ((0, None, False), full))
    carry = step(0, 0, carry, ((1, 0, False), full))
    n_pairs, _, carry = lax.while_loop(lambda st: (st[0] < first_q_block) & st[1], pair_and_check,
                                       (jnp.int32(0), jnp.bool_(True), carry))
    drain = (None, "below", True)
    carry = step(2 * n_pairs + 1, 1, carry, (drain, drain))
    drain = (None, None, True)
    carry = step(2 * n_pairs + 2, 0, carry, (drain, drain))
    _finish_heads([acc_buf[li] / carry[li][1] for li in range(n_lanes)], g_ref, o_ref)


def _sb_kernel(q_ref, k_ref, vt_ref, g_ref, stats_ref, o_ref, z_buf, sp_buf, e_buf, wrow_buf, later_buf, acc_buf):
    qi = pl.program_id(2)
    bk = vt_ref.shape[3]
    n_wide, n_first = z_buf.shape[0], SB_WIDE_TILES
    n_lanes, cw = acc_buf.shape[0], acc_buf.shape[2]
    n_chunks = q_ref.shape[1] // cw
    lane_group = lambda li: (li // (2 * n_chunks), (li // n_chunks) % 2, li % n_chunks)
    n_tiles = 2 * (qi + 1)
    zero = lambda lane: 0.0
    queries = _lane_queries(q_ref, zero, zero, cw)
    q_norm = _query_norm_bounds(queries, n_chunks)
    suffix = (lax.broadcasted_iota(jnp.int32, (bk, bk), 1)
              >= lax.broadcasted_iota(jnp.int32, (bk, bk), 0)).astype(BF16)
    softplus2 = lambda z: jnp.maximum(z, jnp.log2(1.0 + jnp.exp2(jnp.minimum(z, EXP2_MAX))))
    keys = lambda j, pp: k_ref[0, pl.ds(pl.multiple_of(j * bk, bk), bk), pp * LANES:(pp + 1) * LANES]
    values = lambda j, pp, hh: vt_ref[0, j, pl.ds(pp * LANES + hh * HEAD_DIM, HEAD_DIM), :]

    sees = lambda u, li: _diag_visibility(u, lane_group(li)[2], bk, cw, strict=True) if u < 2 else "all"

    def wide_block(tiles, later, check_exists):
        live = [(u, li) for u in tiles for li in range(n_lanes) if not _hidden(sees(u, li))]
        later, col_sums = list(later), {}

        def score(u, li):
            z = _dot(keys(jnp.maximum(n_tiles - 1 - u, 0), lane_group(li)[0]), queries[li])
            if not isinstance(sees(u, li), str):
                z = jnp.where(sees(u, li), z, MASKED)
            if check_exists and u >= 2:
                z = jnp.where(u < n_tiles, z, MASKED)
            z_buf[u, li] = z

        def softplus(u, li):
            sp_buf[u, li] = softplus2(z_buf[u, li]).astype(BF16)

        def cumsum(u, li):
            within = _dot(suffix, sp_buf[u, li])
            col_sums[u, li] = within[0:1, :]
            z_buf[u, li] = z_buf[u, li] - within

        def weight(u, li):
            pp, hh, _ = lane_group(li)
            a = jnp.exp2(z_buf[u, li] - later[li])
            acc_buf[li] += _dot(values(jnp.maximum(n_tiles - 1 - u, 0), pp, hh), a.astype(BF16))
            later[li] = later[li] + col_sums[u, li]

        stages = (score, softplus, cumsum, weight)
        for pos in range(len(live) + len(stages) - 1):
            for lag, stage in enumerate(stages):
                if 0 <= pos - lag < len(live):
                    stage(*live[pos - lag])
        return later

    n_rest = n_tiles - n_wide

    def rest_matters(first_unscored, mass):
        j_rest = jnp.maximum(n_rest - 1 - first_unscored, 0)
        worst = None
        for li in range(n_lanes):
            pp, hh, _ = lane_group(li)
            bound = q_norm[li] * stats_ref[0, pp, j_rest, 4 + hh:5 + hh, :] - mass[li]
            worst = bound if worst is None else jnp.maximum(worst, bound)
        return jnp.max(worst) >= -PRUNE_LOG2

    def step(t, slot, later, score=True, softplus=True, cumsum=True, weight=True):
        new_later = []
        for li in range(n_lanes):
            pp, hh, _ = lane_group(li)
            if cumsum:
                within = _dot(suffix, sp_buf[1 - slot, li])
                e_buf[1 - slot, li] = z_buf[1 - slot, li] - within
                wrow_buf[1 - slot, li] = within[0:1, :]
            if score:
                z_buf[1 - slot, li] = _dot(keys(n_rest - 1 - (t + 3), pp), queries[li])
            if weight:
                a = jnp.exp2(e_buf[slot, li] - later[li])
                acc_buf[li] += _dot(values(n_rest - 1 - jnp.maximum(t, 0), pp, hh), a.astype(BF16))
                new_later.append(later[li] + wrow_buf[slot, li])
            else:
                new_later.append(later[li])
            if softplus:
                sp_buf[slot, li] = softplus2(z_buf[slot, li]).astype(BF16)
        return tuple(new_later)

    def pair_and_check(state):
        i, _, later = state
        t = 2 * i - 1
        later = step(t + 1, 0, step(t, 1, later))
        mass = [later[li] + wrow_buf[1, li] for li in range(n_lanes)]
        return i + 1, rest_matters(2 * i + 4, mass), later

    acc_buf[...] = jnp.zeros(acc_buf.shape, F32)
    first = wide_block(range(n_first), [jnp.zeros((1, cw), F32)] * n_lanes, check_exists=True)
    for li in range(n_lanes):
        later_buf[li] = first[li]
    swept = lambda: [later_buf[li] for li in range(n_lanes)]
    more_matters = (n_rest + (n_wide - n_first) > 0) & rest_matters(n_first - n_wide, first)

    @pl.when(more_matters)
    def _():
        more = wide_block(range(n_first, n_wide), first, check_exists=False)
        for li in range(n_lanes):
            later_buf[li] = more[li]

    @pl.when(more_matters & (n_rest > 0) & rest_matters(0, swept()))
    def _():
        later = swept()
        e_buf[1] = jnp.full(e_buf.shape[1:], MASKED, F32)
        wrow_buf[1] = jnp.zeros(wrow_buf.shape[1:], F32)
        mass = step(-3, 1, tuple(later), softplus=False, cumsum=False, weight=False)
        mass = step(-2, 0, mass, cumsum=False, weight=False)
        n_pairs, _, mass = lax.while_loop(lambda st: (2 * st[0] + 2 < n_rest) & st[1], pair_and_check,
                                          (jnp.int32(0), jnp.bool_(True), mass))
        mass = step(2 * n_pairs - 1, 1, mass, score=False)
        mass = step(2 * n_pairs, 0, mass, score=False, softplus=False)
        step(2 * n_pairs + 1, 1, mass, score=False, softplus=False, cumsum=False)

    _finish_heads([acc_buf[li] for li in range(n_lanes)], g_ref, o_ref)


def _attention(body, name, pairs, bq, scratch, qk, k_arr, k_spec, vt, g, stats, q_block0, vt_block0, n_heads):
    bsz, s, _ = qk.shape
    bk, width = ATT_K, pairs * LANES
    d_grp = n_heads * HEAD_DIM
    assert (n_heads // 2) % pairs == 0
    return pl.pallas_call(
        body,
        grid=(bsz, n_heads // 2 // pairs, s // bq),
        in_specs=[pl.BlockSpec((1, bq, width), lambda b, p, i: (b, i, q_block0 + p)),
                  k_spec,
                  pl.BlockSpec((1, s // bk, width, bk), lambda b, p, i: (b, 0, vt_block0 + p, 0)),
                  pl.BlockSpec((1, width), lambda b, p, i: (0, p)),
                  pl.BlockSpec((1, pairs) + stats.shape[2:], lambda b, p, i: (b, p, 0, 0, 0))],
        out_specs=pl.BlockSpec((1, bq, width), lambda b, p, i: (b, i, p)),
        out_shape=jax.ShapeDtypeStruct((bsz, s, d_grp), BF16),
        scratch_shapes=scratch,
        compiler_params=_params("arbitrary", "arbitrary", "arbitrary"),
        name=name,
    )(qk, k_arr, vt, g.reshape(1, d_grp), stats)


def _lane_groups(pairs, bq):
    return pairs * 2 * (bq // ATT_COLS)


def _fox_scratch(pairs):
    n = _lane_groups(pairs, FOX_Q)
    return [pltpu.VMEM((2, n, ATT_K, ATT_COLS), F32), pltpu.VMEM((2, n, 1, ATT_COLS), F32),
            pltpu.VMEM((2, n, ATT_K, ATT_COLS), BF16), pltpu.VMEM((n, HEAD_DIM, ATT_COLS), F32)]


def _sb_scratch(pairs):
    n = _lane_groups(pairs, ATT_Q)
    wide = SB_WIDE_TILES + SB_MORE_TILES
    return [pltpu.VMEM((wide, n, ATT_K, ATT_COLS), F32), pltpu.VMEM((wide, n, ATT_K, ATT_COLS), BF16),
            pltpu.VMEM((2, n, ATT_K, ATT_COLS), F32), pltpu.VMEM((2, n, 1, ATT_COLS), F32),
            pltpu.VMEM((n, 1, ATT_COLS), F32), pltpu.VMEM((n, HEAD_DIM, ATT_COLS), F32)]


def _mixer_kernel(x_ref, xh_ref, mf_ref, mfh_ref, ms_ref, msh_ref, mod_ref, wo_ref, gm_ref, wu_ref, cw_ref,
                  cb_ref, wd_ref, gf_ref, o_ref, u_buf, acc_ref, x1_buf, *, final_norm):
    i = pl.program_id(1)
    tm = x_ref.shape[1]
    n_chunks = wd_ref.shape[0]
    mix = jnp.concatenate([jnp.concatenate([mfh_ref[0], msh_ref[0]], axis=-1),
                           jnp.concatenate([mf_ref[0], ms_ref[0]], axis=-1)], axis=0)
    x_ext = jnp.concatenate([xh_ref[0], x_ref[0]], axis=0)
    x1_ext = x_ext + mod_ref[0, 2:3, :] * _dot(mix, wo_ref[...])
    shift = mod_ref[0, 3:4, :]
    scale = mod_ref[0, 4:5, :]
    h_ext = _rms_rows(x1_ext) * gm_ref[...] * (1.0 + scale) + shift
    row = lax.broadcasted_iota(jnp.int32, h_ext.shape, 0)
    hx = jnp.where((row >= BF16_SUBLANES) | (i > 0), h_ext, 0.0).astype(BF16)
    x1_buf[...] = x1_ext[BF16_SUBLANES:, :]
    acc_ref[...] = jnp.zeros_like(acc_ref)

    def project_up(c, slot):
        for br in range(2):
            u_buf[slot, br] = _dot(hx, wu_ref[br, c])

    def mix_down(c, slot):
        branches = []
        for br in range(2):
            out = cb_ref[br, c]
            for tap in range(CONV_WIDTH):
                first = BF16_SUBLANES - (CONV_WIDTH - 1 - tap)
                out = out + cw_ref[br, c, tap:tap + 1, :] * u_buf[slot, br, pl.ds(first, tm), :]
            branches.append(out)
        u_gate, u_val = branches
        acc_ref[...] += _dot((u_gate * jax.nn.sigmoid(u_gate) * u_val).astype(BF16), wd_ref[c])

    def chunk_pair(j, _):
        c = 2 * j
        project_up(c + 1, 1)
        mix_down(c, 0)
        project_up(c + 2, 0)
        mix_down(c + 1, 1)
        return 0

    project_up(0, 0)
    lax.fori_loop(0, (n_chunks - 1) // 2, chunk_pair, 0)
    mix_down(n_chunks - 1, 0)
    x2 = x1_buf[...] + mod_ref[0, 5:6, :] * acc_ref[...]
    o_ref[0] = _rms_rows(x2) * gf_ref[...] if final_norm else x2


def _mixer(x, mix_f, mix_s, mod, w_out, g_mlp, w_up, conv_w, conv_b, w_down, g_final, final_norm):
    bsz, s, d = x.shape
    tm = OUT_ROWS
    n_chunks, tf = w_down.shape[0], w_down.shape[1]
    assert n_chunks % 2 == 1
    halo_blocks = tm // BF16_SUBLANES
    row = lambda b, i: (b, i, 0)
    halo = lambda b, i: (b, jnp.maximum(i * halo_blocks - 1, 0), 0)
    tile_and_halo = lambda a: [pl.BlockSpec((1, tm, a.shape[2]), row),
                               pl.BlockSpec((1, BF16_SUBLANES, a.shape[2]), halo)]
    resident = lambda a: pl.BlockSpec(a.shape, lambda b, i: (0,) * a.ndim, pipeline_mode=pl.Buffered(1))
    return pl.pallas_call(
        functools.partial(_mixer_kernel, final_norm=final_norm),
        grid=(bsz, s // tm),
        in_specs=tile_and_halo(x) + tile_and_halo(mix_f) + tile_and_halo(mix_s)
                 + [pl.BlockSpec((1, N_MOD, d), lambda b, i: (b, 0, 0)), resident(w_out),
                    pl.BlockSpec((1, d), lambda b, i: (0, 0)),
                    resident(w_up), resident(conv_w), resident(conv_b), resident(w_down),
                    pl.BlockSpec((1, d), lambda b, i: (0, 0))],
        out_specs=pl.BlockSpec((1, tm, d), row),
        out_shape=jax.ShapeDtypeStruct((bsz, s, d), F32),
        scratch_shapes=[pltpu.VMEM((2, 2, tm + BF16_SUBLANES, tf), F32), pltpu.VMEM((tm, d), F32),
                        pltpu.VMEM((tm, d), F32)],
        compiler_params=_params("arbitrary", "arbitrary"),
        name="mixer",
    )(x, x, mix_f, mix_f, mix_s, mix_s, mod, w_out, g_mlp, w_up, conv_w, conv_b, w_down, g_final)


def _chunk_columns(a, d_ff, n_chunks, tf):
    halves = jnp.stack([a[:, :d_ff], a[:, d_ff:]])
    halves = jnp.pad(halves, ((0, 0), (0, 0), (0, n_chunks * tf - d_ff)))
    return halves.reshape(2, a.shape[0], n_chunks, tf).transpose(0, 2, 1, 3)


def _pad_cols(a, n):
    return jnp.pad(a, ((0, 0), (0, n - a.shape[1])))


def kernel(x, c, w_ada, b_ada, g_attn, w_in, b_fgate, g_out_fox, g_out_sb, w_out,
           g_mlp, w_up, conv_w, conv_b, w_down, g_final):
    depth, d, _ = w_ada.shape
    n_fox = b_fgate.shape[1]
    d_fox = n_fox * HEAD_DIM
    d_sb = g_out_sb.shape[1]
    n_sb = d_sb // HEAD_DIM
    d_ff = w_down.shape[1]
    d_ff_pad = -(-d_ff // FF_CHUNK) * FF_CHUNK
    assert n_fox % 2 == 0 and n_sb == n_fox and 3 * n_fox <= LANES
    assert x.shape[1] % OUT_ROWS == 0 and x.shape[1] % FOX_Q == 0 and ATT_Q == 2 * ATT_K
    o_kf, o_vf, o_qs, o_ks, o_vs, o_gate = (d_fox, 2 * d_fox, 3 * d_fox, 3 * d_fox + d_sb,
                                             3 * d_fox + 2 * d_sb, 3 * d_fox + 3 * d_sb)

    for l in range(depth):
        mod = _ada(c, w_ada[l], b_ada[l]).reshape(-1, N_MOD, d)
        order = jnp.argsort(b_fgate[l])
        pick_head = (jnp.arange(n_fox)[:, None] == order[None, :]).astype(BF16)
        pick = jnp.kron(pick_head, jnp.eye(HEAD_DIM, dtype=BF16))
        w, wo = w_in[l].astype(BF16), w_out[l].astype(BF16)
        w_nat = jnp.concatenate([jnp.dot(w[:, :o_kf], pick), jnp.dot(w[:, o_kf:o_vf], pick), w[:, o_qs:o_vs]], axis=1)
        w_vt = jnp.concatenate([jnp.dot(pick.T, w[:, o_vf:o_qs].T), w[:, o_vs:o_gate].T], axis=0)
        w_gate = _pad_cols(jnp.dot(w[:, o_gate:], pick_head), LANES)
        b_gate = _pad_cols(b_fgate[l][order].reshape(1, n_fox), LANES)
        g_fox = g_out_fox[l].reshape(n_fox, HEAD_DIM)[order].reshape(-1)
        w_mix = jnp.concatenate([jnp.dot(pick.T, wo[:d_fox]), wo[d_fox:]], axis=0)
        qk, vt, k_aug, stats = _inproj(x, mod, g_attn[l].reshape(1, d), w_nat, w_vt, w_gate, b_gate, n_fox)

        steps_f, steps_s = n_fox // 2 // FOX_PAIRS, n_sb // 2 // SB_PAIRS
        fox_k_spec = pl.BlockSpec((1, 2 * FOX_PAIRS, x.shape[1], LANES), lambda b, p, i: (b, p, 0, 0))
        mix_f = _attention(_fox_kernel, "fox", FOX_PAIRS, FOX_Q, _fox_scratch(FOX_PAIRS), qk, k_aug, fox_k_spec, vt,
                           g_fox, stats, q_block0=0, vt_block0=0, n_heads=n_fox)
        sb_k_spec = pl.BlockSpec((1, x.shape[1], SB_PAIRS * LANES), lambda b, p, i: (b, 0, 3 * steps_s + p))
        mix_s = _attention(_sb_kernel, "sb", SB_PAIRS, ATT_Q, _sb_scratch(SB_PAIRS), qk, qk, sb_k_spec, vt,
                           g_out_sb[l], stats, q_block0=2 * steps_s, vt_block0=steps_s, n_heads=n_sb)

        n_ff = d_ff_pad // FF_CHUNK
        x = _mixer(x, mix_f, mix_s, mod, w_mix, g_mlp[l].reshape(1, d),
                   _chunk_columns(w_up[l], d_ff, n_ff, FF_CHUNK).astype(BF16),
                   _chunk_columns(conv_w[l], d_ff, n_ff, FF_CHUNK),
                   _chunk_columns(conv_b[l].reshape(1, -1), d_ff, n_ff, FF_CHUNK),
                   jnp.pad(w_down[l], ((0, d_ff_pad - d_ff), (0, 0))).astype(BF16).reshape(n_ff, FF_CHUNK, d),
                   g_final.reshape(1, d), final_norm=(l == depth - 1))
    return x
```

```python
import functools

import numpy as np
import jax
import jax.numpy as jnp
from jax import lax
from jax.experimental import pallas as pl
from jax.experimental.pallas import tpu as pltpu

HEAD_DIM = 64
N_MOD = 6
CONV_WIDTH = 3
EPS = 1e-6

LANES = 128
BF16_SUBLANES = 16
VMEM_LIMIT_BYTES = 48 * 1024 * 1024

ATT_Q = 512
FOX_Q = 2 * ATT_Q
ATT_K = 256
ATT_COLS = 256
FOX_PAIRS = 1
SB_PAIRS = 2
SB_WIDE_TILES = 4
SB_MORE_TILES = 2
PROJ_ROWS = 2 * ATT_K
LOG2E = 1.4426950408889634
MASKED = -1e30
M_INIT = -1e29
EXP2_MAX = 126.0
PRUNE_LOG2 = 160.0
NORM_SLACK = 1.02
STATS_ROWS = 8
OUT_ROWS = 512
FF_CHUNK = 256

F32 = jnp.float32
BF16 = jnp.bfloat16
NT_DIMS = (((1,), (1,)), ((), ()))


def _dot(a, b):
    return jnp.dot(a, b, preferred_element_type=F32)


def _dot_nt(a, b):
    return lax.dot_general(a, b, NT_DIMS, preferred_element_type=F32)


def _params(*sem):
    return pltpu.CompilerParams(dimension_semantics=sem, vmem_limit_bytes=VMEM_LIMIT_BYTES)


def _rms_rows(x):
    return x * lax.rsqrt(jnp.mean(x * x, axis=-1, keepdims=True) + EPS)


def _softplus(z):
    return jnp.maximum(z, 0.0) + jnp.log(1.0 + jnp.exp(-jnp.abs(z)))


def _split3(x):
    hi = x.astype(BF16)
    r1 = x - hi.astype(F32)
    mid = r1.astype(BF16)
    lo = (r1 - mid.astype(F32)).astype(BF16)
    return hi, mid, lo


def _ada_kernel(c_ref, w_ref, b_ref, o_ref):
    c = c_ref[...]
    o_ref[...] = _dot(c * jax.nn.sigmoid(c), w_ref[...]) + b_ref[...]


def _ada(c, w, b):
    bsz, d = c.shape
    n = w.shape[1]
    return pl.pallas_call(
        _ada_kernel,
        grid=(n // d,),
        in_specs=[pl.BlockSpec((bsz, d), lambda j: (0, 0)),
                  pl.BlockSpec((d, d), lambda j: (0, j)),
                  pl.BlockSpec((1, d), lambda j: (0, j))],
        out_specs=pl.BlockSpec((bsz, d), lambda j: (0, j)),
        out_shape=jax.ShapeDtypeStruct((bsz, n), F32),
        compiler_params=_params("arbitrary"),
        name="ada",
    )(c, w, b.reshape(1, n))


def _decay_tile(lf, k_fox, sel_ref, ind_ref, carry_ref, kpre_ref, kaug_ref, stats_ref, rows, tile, n_heads):
    tk = lf.shape[0]
    lane = lax.broadcasted_iota(jnp.int32, (tk, LANES), 1)
    lf = jnp.where(lane < n_heads, lf, 0.0)
    row = lax.broadcasted_iota(jnp.int32, (tk, tk), 0)
    col = lax.broadcasted_iota(jnp.int32, (tk, tk), 1)
    tri = (col <= row).astype(BF16)
    hi, mid, lo = _split3(lf)
    f_run = carry_ref[...] + (_dot(tri, hi) + _dot(tri, mid) + _dot(tri, lo))
    carry_ref[...] = f_run[tk - 1:tk, :]
    ghi, gmid, glo = _split3(-LOG2E * f_run)
    packed = (ghi.astype(F32) + pltpu.roll(gmid.astype(F32), n_heads, 1)
              + pltpu.roll(glo.astype(F32), 2 * n_heads, 1)).astype(BF16)
    placed = _dot(packed, sel_ref[...])
    for h in range(n_heads):
        k_pair = k_fox[:, (h // 2) * LANES:(h // 2 + 1) * LANES]
        own = (lane < HEAD_DIM) if h % 2 == 0 else (lane >= HEAD_DIM)
        kaug_ref[0, h, rows, :] = jnp.where(own, k_pair, placed[:, h * LANES:(h + 1) * LANES].astype(BF16))

    k32 = k_fox.astype(F32)
    sq = _dot((k32 * k32).astype(BF16), ind_ref[...])
    kpre_f = jnp.maximum(kpre_ref[...], jnp.sqrt(jnp.max(sq, axis=0, keepdims=True) * NORM_SLACK))
    kpre_ref[...] = kpre_f
    g_end = -LOG2E * f_run[tk - 1:tk, :]
    lane1 = lax.broadcasted_iota(jnp.int32, (1, LANES), 1)

    def spread(v, h):
        return jnp.broadcast_to(jnp.sum(jnp.where(lane1 == h, v, 0.0), axis=1, keepdims=True), (1, ATT_COLS))

    for p in range(n_heads // 2):
        srows = [spread(v, 2 * p + hh) for v in (kpre_f, g_end) for hh in range(2)]
        srows += [jnp.zeros((1, ATT_COLS), F32)] * (stats_ref.shape[3] - len(srows))
        stats_ref[0, p, tile] = jnp.concatenate(srows, axis=0)


def _inproj_kernel(x_ref, mod_ref, g_ref, wn_ref, wvt_ref, wg_ref, bg_ref, sel_ref, ind_ref,
                   qk_ref, vt_ref, kaug_ref, stats_ref, carry_ref, kpre_ref, *, n_heads):
    @pl.when(pl.program_id(1) == 0)
    def _():
        carry_ref[...] = jnp.zeros_like(carry_ref)
        kpre_ref[...] = jnp.zeros_like(kpre_ref)

    shift = mod_ref[0, 0:1, :]
    scale = mod_ref[0, 1:2, :]
    h = (_rms_rows(x_ref[0]) * g_ref[...] * (1.0 + scale) + shift).astype(BF16)
    qk = _dot(h, wn_ref[...]).astype(BF16)
    qk_ref[0] = qk
    logit = _dot(h, wg_ref[...]) + bg_ref[...]
    log_f = -_softplus(-logit)
    d_grp = n_heads * HEAD_DIM
    tk = vt_ref.shape[3]
    for tile in range(vt_ref.shape[1]):
        rows = slice(tile * tk, (tile + 1) * tk)
        vt_ref[0, tile] = _dot_nt(wvt_ref[...], h[rows, :]).astype(BF16)
        _decay_tile(log_f[rows, :], qk[rows, d_grp:2 * d_grp], sel_ref, ind_ref,
                    carry_ref, kpre_ref, kaug_ref, stats_ref, rows, tile, n_heads)


def _head_indicator(n_heads):
    ind = np.zeros((n_heads * HEAD_DIM, LANES), np.float32)
    ind[np.arange(n_heads * HEAD_DIM), np.arange(n_heads * HEAD_DIM) // HEAD_DIM] = 1.0
    return jnp.asarray(ind, BF16)


def _decay_select_matrix(n_heads):
    sel = np.zeros((LANES, n_heads * LANES), np.float32)
    for h in range(n_heads):
        base = h * LANES + (HEAD_DIM if h % 2 == 0 else 0)
        for term in range(3):
            sel[term * n_heads + h, base + term] = 1.0
    return jnp.asarray(sel, BF16)


def _inproj(x, mod, g, w_nat, w_vt, w_gate, b_gate, n_heads):
    bsz, s, d = x.shape
    tm, tk = PROJ_ROWS, ATT_K
    n_nat, n_v = w_nat.shape[1], w_vt.shape[0]
    const = lambda b, i: (0, 0)
    resident = lambda a: pl.BlockSpec(a.shape, const, pipeline_mode=pl.Buffered(1))
    sel, ind = _decay_select_matrix(n_heads), _head_indicator(n_heads)
    return pl.pallas_call(
        functools.partial(_inproj_kernel, n_heads=n_heads),
        grid=(bsz, s // tm),
        in_specs=[pl.BlockSpec((1, tm, d), lambda b, i: (b, i, 0)),
                  pl.BlockSpec((1, N_MOD, d), lambda b, i: (b, 0, 0)),
                  pl.BlockSpec((1, d), const),
                  resident(w_nat), resident(w_vt), resident(w_gate),
                  pl.BlockSpec((1, LANES), const),
                  resident(sel), resident(ind)],
        out_specs=[pl.BlockSpec((1, tm, n_nat), lambda b, i: (b, i, 0)),
                   pl.BlockSpec((1, tm // tk, n_v, tk), lambda b, i: (b, i, 0, 0)),
                   pl.BlockSpec((1, n_heads, tm, LANES), lambda b, i: (b, 0, i, 0)),
                   pl.BlockSpec((1, n_heads // 2, tm // tk, STATS_ROWS, ATT_COLS), lambda b, i: (b, 0, i, 0, 0))],
        out_shape=[jax.ShapeDtypeStruct((bsz, s, n_nat), BF16),
                   jax.ShapeDtypeStruct((bsz, s // tk, n_v, tk), BF16),
                   jax.ShapeDtypeStruct((bsz, n_heads, s, LANES), BF16),
                   jax.ShapeDtypeStruct((bsz, n_heads // 2, s // tk, STATS_ROWS, ATT_COLS), F32)],
        scratch_shapes=[pltpu.VMEM((1, LANES), F32), pltpu.VMEM((1, LANES), F32)],
        compiler_params=_params("arbitrary", "arbitrary"),
        name="inproj",
    )(x, mod, g, w_nat, w_vt, w_gate, b_gate, sel, ind)


def _lane_queries(q_ref, extra_even, extra_odd, cw):
    out = []
    for pp in range(q_ref.shape[2] // LANES):
        q = q_ref[0, :, pp * LANES:(pp + 1) * LANES].astype(F32) * (HEAD_DIM ** -0.5 * LOG2E)
        lane = lax.broadcasted_iota(jnp.int32, q.shape, 1)
        heads = (jnp.where(lane < HEAD_DIM, q, extra_even(lane)).T.astype(BF16),
                 jnp.where(lane >= HEAD_DIM, q, extra_odd(lane)).T.astype(BF16))
        out += [heads[hh][:, c * cw:(c + 1) * cw] for hh in range(2) for c in range(q.shape[0] // cw)]
    return out


def _visibility(first_key, first_query, bk, cw, strict):
    last_visible_gap = -1 if strict else 0
    if first_key + bk - 1 - first_query <= last_visible_gap:
        return "all"
    if first_key - (first_query + cw - 1) > last_visible_gap:
        return "none"
    gap = (lax.broadcasted_iota(jnp.int32, (bk, cw), 0) - lax.broadcasted_iota(jnp.int32, (bk, cw), 1)
           + (first_key - first_query))
    return gap <= last_visible_gap


def _diag_visibility(u, c, bk, cw, strict):
    return _visibility((1 - u) * bk, c * cw, bk, cw, strict)


def _hidden(visibility):
    return isinstance(visibility, str) and visibility == "none"


def _query_norm_bounds(queries, n_chunks):
    bounds = []
    for li, q in enumerate(queries):
        hh = (li // n_chunks) % 2
        own = q[hh * HEAD_DIM:(hh + 1) * HEAD_DIM, :].astype(F32)
        bounds.append(jnp.sqrt(jnp.sum(own * own, axis=0, keepdims=True) * NORM_SLACK))
    return bounds


def _finish_heads(lanes, g_ref, o_ref):
    n_pairs = o_ref.shape[2] // LANES
    n_chunks = len(lanes) // (2 * n_pairs)
    for pp in range(n_pairs):
        mine = lanes[2 * pp * n_chunks:2 * (pp + 1) * n_chunks]
        outs = [jnp.concatenate(mine[hh * n_chunks:(hh + 1) * n_chunks], axis=1) for hh in range(2)]
        normed = [o * lax.rsqrt(jnp.mean(o * o, axis=0, keepdims=True) + EPS) for o in outs]
        cols = slice(pp * LANES, (pp + 1) * LANES)
        o_ref[0, :, cols] = (jnp.concatenate(normed, axis=0).T * g_ref[:, cols]).astype(o_ref.dtype)


def _fox_kernel(q_ref, k_ref, vt_ref, g_ref, stats_ref, o_ref, s_buf, cmax_buf, p_buf, acc_buf):
    bk = vt_ref.shape[3]
    n_lanes, cw = acc_buf.shape[0], acc_buf.shape[2]
    n_chunks = q_ref.shape[1] // cw
    chunks_per_sub = ATT_Q // cw
    assert q_ref.shape[1] == 2 * ATT_Q
    lane_group = lambda li: (li // (2 * n_chunks), (li // n_chunks) % 2, li % n_chunks)
    first_q_block = 2 * pl.program_id(2)
    n_tiles = 2 * (first_q_block + 1)
    ones3 = lambda lo: (lambda lane: jnp.where((lane >= lo) & (lane < lo + 3), 1.0, 0.0))
    queries = _lane_queries(q_ref, ones3(HEAD_DIM), ones3(0), cw)
    acc_buf[...] = jnp.zeros(acc_buf.shape, F32)
    for li in range(n_lanes):
        if _hidden(_diag_visibility(0, lane_group(li)[2] % chunks_per_sub, bk, cw, strict=False)):
            p_buf[0, li] = jnp.zeros((bk, cw), BF16)

    full, idle = ("below", "below", True), (None, None, False)

    def step(t, slot, carry, stages=(full, full)):
        new = []
        for li in range(n_lanes):
            pp, hh, chunk = lane_group(li)
            sub, c = divmod(chunk, chunks_per_sub)
            score, softmax, value = stages[sub]
            see = lambda u: "all" if u == "below" else _diag_visibility(u, c, bk, cw, strict=False)
            if score is not None and not _hidden(see(score)):
                start = pl.multiple_of((n_tiles - 2 - t) * bk, bk)
                s_new = _dot(k_ref[0, 2 * pp + hh, pl.ds(start, bk), :], queries[li])
                if not isinstance(see(score), str):
                    s_new = jnp.where(see(score), s_new, MASKED)
                s_buf[1 - slot, li] = s_new
                cmax_buf[1 - slot, li] = jnp.max(s_new, axis=0, keepdims=True)
            pv = None
            if value:
                vt = vt_ref[0, n_tiles - t, pl.ds(pp * LANES + hh * HEAD_DIM, HEAD_DIM), :]
                pv = _dot(vt, p_buf[1 - slot, li])
            m, l = carry[li]
            if softmax is not None and not _hidden(see(softmax)):
                m_new = jnp.maximum(m, cmax_buf[slot, li])
                alpha = jnp.exp2(m - m_new)
                p = jnp.exp2(s_buf[slot, li] - m_new)
                p_buf[slot, li] = p.astype(BF16)
                m, l = m_new, alpha * l + jnp.sum(p, axis=0, keepdims=True)
                acc_buf[li] = alpha * (acc_buf[li] if pv is None else acc_buf[li] + pv)
            elif pv is not None:
                acc_buf[li] += pv
            new.append((m, l))
        return tuple(new)

    def step_pair(i, carry):
        t = 2 * i + 1
        return step(t + 1, 0, step(t, 1, carry))

    q_norm = _query_norm_bounds(queries, n_chunks)

    def later_tiles_matter(i, carry):
        j_rest = jnp.maximum(n_tiles - 5 - 2 * i, 0)
        worst = None
        for li in range(n_lanes):
            pp, hh, _ = lane_group(li)
            bound = (q_norm[li] * stats_ref[0, pp, j_rest, hh:hh + 1, :]
                     + stats_ref[0, pp, j_rest, 2 + hh:3 + hh, :] - carry[li][0])
            worst = bound if worst is None else jnp.maximum(worst, bound)
        return jnp.max(worst) >= -PRUNE_LOG2

    def pair_and_check(state):
        i, _, carry = state
        carry = step_pair(i, carry)
        return i + 1, later_tiles_matter(i, carry), carry

    carry = tuple((jnp.full((1, cw), M_INIT, F32), jnp.zeros((1, cw), F32)) for _ in range(n_lanes))
    carry = step(-3, 1, carry, (idle, (0, None, False)))
    carry = step(-2, 0, carry, (idle, (1, 0, False)))
    carry = step(-1, 1, carry, ((0, None, False), full))
    carry = step(0, 0, carry, ((1, 0, False), full))
    n_pairs, _, carry = lax.while_loop(lambda st: (st[0] < first_q_block) & st[1], pair_and_check,
                                       (jnp.int32(0), jnp.bool_(True), carry))
    drain = (None, "below", True)
    carry = step(2 * n_pairs + 1, 1, carry, (drain, drain))
    drain = (None, None, True)
    carry = step(2 * n_pairs + 2, 0, carry, (drain, drain))
    _finish_heads([acc_buf[li] / carry[li][1] for li in range(n_lanes)], g_ref, o_ref)


def _sb_kernel(q_ref, k_ref, vt_ref, g_ref, o_ref, z_buf, sp_buf, e_buf, wrow_buf, later_buf, acc_buf):
    qi = pl.program_id(2)
    bk = vt_ref.shape[3]
    n_wide, n_first = z_buf.shape[0], SB_WIDE_TILES
    n_lanes, cw = acc_buf.shape[0], acc_buf.shape[2]
    n_chunks = q_ref.shape[1] // cw
    lane_group = lambda li: (li // (2 * n_chunks), (li // n_chunks) % 2, li % n_chunks)
    n_tiles = 2 * (qi + 1)
    zero = lambda lane: 0.0
    queries = _lane_queries(q_ref, zero, zero, cw)
    suffix = (lax.broadcasted_iota(jnp.int32, (bk, bk), 1)
              >= lax.broadcasted_iota(jnp.int32, (bk, bk), 0)).astype(BF16)
    softplus2 = lambda z: jnp.maximum(z, jnp.log2(1.0 + jnp.exp2(jnp.minimum(z, EXP2_MAX))))
    keys = lambda j, pp: k_ref[0, pl.ds(pl.multiple_of(j * bk, bk), bk), pp * LANES:(pp + 1) * LANES]
    values = lambda j, pp, hh: vt_ref[0, j, pl.ds(pp * LANES + hh * HEAD_DIM, HEAD_DIM), :]

    sees = lambda u, li: _diag_visibility(u, lane_group(li)[2], bk, cw, strict=True) if u < 2 else "all"

    def wide_block(tiles, later, check_exists):
        live = [(u, li) for u in tiles for li in range(n_lanes) if not _hidden(sees(u, li))]
        later, col_sums = list(later), {}

        def score(u, li):
            z = _dot(keys(jnp.maximum(n_tiles - 1 - u, 0), lane_group(li)[0]), queries[li])
            if not isinstance(sees(u, li), str):
                z = jnp.where(sees(u, li), z, MASKED)
            if check_exists and u >= 2:
                z = jnp.where(u < n_tiles, z, MASKED)
            z_buf[u, li] = z

        def softplus(u, li):
            sp_buf[u, li] = softplus2(z_buf[u, li]).astype(BF16)

        def cumsum(u, li):
            within = _dot(suffix, sp_buf[u, li])
            col_sums[u, li] = within[0:1, :]
            z_buf[u, li] = z_buf[u, li] - within

        def weight(u, li):
            pp, hh, _ = lane_group(li)
            a = jnp.exp2(z_buf[u, li] - later[li])
            acc_buf[li] += _dot(values(jnp.maximum(n_tiles - 1 - u, 0), pp, hh), a.astype(BF16))
            later[li] = later[li] + col_sums[u, li]

        stages = (score, softplus, cumsum, weight)
        for pos in range(len(live) + len(stages) - 1):
            for lag, stage in enumerate(stages):
                if 0 <= pos - lag < len(live):
                    stage(*live[pos - lag])
        return later

    n_rest = n_tiles - n_wide

    def rest_matters(mass):
        least = mass[0]
        for li in range(1, n_lanes):
            least = jnp.minimum(least, mass[li])
        return jnp.min(least) <= PRUNE_LOG2

    def step(t, slot, later, score=True, softplus=True, cumsum=True, weight=True):
        new_later = []
        for li in range(n_lanes):
            pp, hh, _ = lane_group(li)
            if cumsum:
                within = _dot(suffix, sp_buf[1 - slot, li])
                e_buf[1 - slot, li] = z_buf[1 - slot, li] - within
                wrow_buf[1 - slot, li] = within[0:1, :]
            if score:
                z_buf[1 - slot, li] = _dot(keys(n_rest - 1 - (t + 3), pp), queries[li])
            if weight:
                a = jnp.exp2(e_buf[slot, li] - later[li])
                acc_buf[li] += _dot(values(n_rest - 1 - jnp.maximum(t, 0), pp, hh), a.astype(BF16))
                new_later.append(later[li] + wrow_buf[slot, li])
            else:
                new_later.append(later[li])
            if softplus:
                sp_buf[slot, li] = softplus2(z_buf[slot, li]).astype(BF16)
        return tuple(new_later)

    def pair_and_check(state):
        i, _, later = state
        t = 2 * i - 1
        later = step(t + 1, 0, step(t, 1, later))
        mass = [later[li] + wrow_buf[1, li] for li in range(n_lanes)]
        return i + 1, rest_matters(mass), later

    acc_buf[...] = jnp.zeros(acc_buf.shape, F32)
    first = wide_block(range(n_first), [jnp.zeros((1, cw), F32)] * n_lanes, check_exists=True)
    for li in range(n_lanes):
        later_buf[li] = first[li]
    swept = lambda: [later_buf[li] for li in range(n_lanes)]
    more_matters = (n_rest + (n_wide - n_first) > 0) & rest_matters(first)

    @pl.when(more_matters)
    def _():
        more = wide_block(range(n_first, n_wide), first, check_exists=False)
        for li in range(n_lanes):
            later_buf[li] = more[li]

    @pl.when(more_matters & (n_rest > 0) & rest_matters(swept()))
    def _():
        later = swept()
        e_buf[1] = jnp.full(e_buf.shape[1:], MASKED, F32)
        wrow_buf[1] = jnp.zeros(wrow_buf.shape[1:], F32)
        mass = step(-3, 1, tuple(later), softplus=False, cumsum=False, weight=False)
        mass = step(-2, 0, mass, cumsum=False, weight=False)
        n_pairs, _, mass = lax.while_loop(lambda st: (2 * st[0] + 2 < n_rest) & st[1], pair_and_check,
                                          (jnp.int32(0), jnp.bool_(True), mass))
        mass = step(2 * n_pairs - 1, 1, mass, score=False)
        mass = step(2 * n_pairs, 0, mass, score=False, softplus=False)
        step(2 * n_pairs + 1, 1, mass, score=False, softplus=False, cumsum=False)

    _finish_heads([acc_buf[li] for li in range(n_lanes)], g_ref, o_ref)


def _attention(body, name, pairs, bq, scratch, qk, k_arr, k_spec, vt, g, q_block0, vt_block0, n_heads, stats=None):
    bsz, s, _ = qk.shape
    bk, width = ATT_K, pairs * LANES
    d_grp = n_heads * HEAD_DIM
    assert (n_heads // 2) % pairs == 0
    in_specs = [pl.BlockSpec((1, bq, width), lambda b, p, i: (b, i, q_block0 + p)),
                k_spec,
                pl.BlockSpec((1, s // bk, width, bk), lambda b, p, i: (b, 0, vt_block0 + p, 0)),
                pl.BlockSpec((1, width), lambda b, p, i: (0, p))]
    operands = [qk, k_arr, vt, g.reshape(1, d_grp)]
    if stats is not None:
        in_specs.append(pl.BlockSpec((1, pairs) + stats.shape[2:], lambda b, p, i: (b, p, 0, 0, 0)))
        operands.append(stats)
    return pl.pallas_call(
        body,
        grid=(bsz, n_heads // 2 // pairs, s // bq),
        in_specs=in_specs,
        out_specs=pl.BlockSpec((1, bq, width), lambda b, p, i: (b, i, p)),
        out_shape=jax.ShapeDtypeStruct((bsz, s, d_grp), BF16),
        scratch_shapes=scratch,
        compiler_params=_params("arbitrary", "arbitrary", "arbitrary"),
        name=name,
    )(*operands)


def _lane_groups(pairs, bq):
    return pairs * 2 * (bq // ATT_COLS)


def _fox_scratch(pairs):
    n = _lane_groups(pairs, FOX_Q)
    return [pltpu.VMEM((2, n, ATT_K, ATT_COLS), F32), pltpu.VMEM((2, n, 1, ATT_COLS), F32),
            pltpu.VMEM((2, n, ATT_K, ATT_COLS), BF16), pltpu.VMEM((n, HEAD_DIM, ATT_COLS), F32)]


def _sb_scratch(pairs):
    n = _lane_groups(pairs, ATT_Q)
    wide = SB_WIDE_TILES + SB_MORE_TILES
    return [pltpu.VMEM((wide, n, ATT_K, ATT_COLS), F32), pltpu.VMEM((wide, n, ATT_K, ATT_COLS), BF16),
            pltpu.VMEM((2, n, ATT_K, ATT_COLS), F32), pltpu.VMEM((2, n, 1, ATT_COLS), F32),
            pltpu.VMEM((n, 1, ATT_COLS), F32), pltpu.VMEM((n, HEAD_DIM, ATT_COLS), F32)]


def _mixer_kernel(x_ref, xh_ref, mf_ref, mfh_ref, ms_ref, msh_ref, mod_ref, wo_ref, gm_ref, wu_ref, cw_ref,
                  cb_ref, wd_ref, gf_ref, o_ref, u_buf, acc_ref, x1_buf, *, final_norm):
    i = pl.program_id(1)
    tm = x_ref.shape[1]
    n_chunks = wd_ref.shape[0]
    mix = jnp.concatenate([jnp.concatenate([mfh_ref[0], msh_ref[0]], axis=-1),
                           jnp.concatenate([mf_ref[0], ms_ref[0]], axis=-1)], axis=0)
    x_ext = jnp.concatenate([xh_ref[0], x_ref[0]], axis=0)
    x1_ext = x_ext + mod_ref[0, 2:3, :] * _dot(mix, wo_ref[...])
    shift = mod_ref[0, 3:4, :]
    scale = mod_ref[0, 4:5, :]
    h_ext = _rms_rows(x1_ext) * gm_ref[...] * (1.0 + scale) + shift
    row = lax.broadcasted_iota(jnp.int32, h_ext.shape, 0)
    hx = jnp.where((row >= BF16_SUBLANES) | (i > 0), h_ext, 0.0).astype(BF16)
    x1_buf[...] = x1_ext[BF16_SUBLANES:, :]
    acc_ref[...] = jnp.zeros_like(acc_ref)

    def project_up(c, slot):
        for br in range(2):
            u_buf[slot, br] = _dot(hx, wu_ref[br, c])

    def mix_down(c, slot):
        branches = []
        for br in range(2):
            out = cb_ref[br, c]
            for tap in range(CONV_WIDTH):
                first = BF16_SUBLANES - (CONV_WIDTH - 1 - tap)
                out = out + cw_ref[br, c, tap:tap + 1, :] * u_buf[slot, br, pl.ds(first, tm), :]
            branches.append(out)
        u_gate, u_val = branches
        acc_ref[...] += _dot((u_gate * jax.nn.sigmoid(u_gate) * u_val).astype(BF16), wd_ref[c])

    def chunk_pair(j, _):
        c = 2 * j
        project_up(c + 1, 1)
        mix_down(c, 0)
        project_up(c + 2, 0)
        mix_down(c + 1, 1)
        return 0

    project_up(0, 0)
    lax.fori_loop(0, (n_chunks - 1) // 2, chunk_pair, 0)
    mix_down(n_chunks - 1, 0)
    x2 = x1_buf[...] + mod_ref[0, 5:6, :] * acc_ref[...]
    o_ref[0] = _rms_rows(x2) * gf_ref[...] if final_norm else x2


def _mixer(x, mix_f, mix_s, mod, w_out, g_mlp, w_up, conv_w, conv_b, w_down, g_final, final_norm):
    bsz, s, d = x.shape
    tm = OUT_ROWS
    n_chunks, tf = w_down.shape[0], w_down.shape[1]
    assert n_chunks % 2 == 1
    halo_blocks = tm // BF16_SUBLANES
    row = lambda b, i: (b, i, 0)
    halo = lambda b, i: (b, jnp.maximum(i * halo_blocks - 1, 0), 0)
    tile_and_halo = lambda a: [pl.BlockSpec((1, tm, a.shape[2]), row),
                               pl.BlockSpec((1, BF16_SUBLANES, a.shape[2]), halo)]
    resident = lambda a: pl.BlockSpec(a.shape, lambda b, i: (0,) * a.ndim, pipeline_mode=pl.Buffered(1))
    return pl.pallas_call(
        functools.partial(_mixer_kernel, final_norm=final_norm),
        grid=(bsz, s // tm),
        in_specs=tile_and_halo(x) + tile_and_halo(mix_f) + tile_and_halo(mix_s)
                 + [pl.BlockSpec((1, N_MOD, d), lambda b, i: (b, 0, 0)), resident(w_out),
                    pl.BlockSpec((1, d), lambda b, i: (0, 0)),
                    resident(w_up), resident(conv_w), resident(conv_b), resident(w_down),
                    pl.BlockSpec((1, d), lambda b, i: (0, 0))],
        out_specs=pl.BlockSpec((1, tm, d), row),
        out_shape=jax.ShapeDtypeStruct((bsz, s, d), F32),
        scratch_shapes=[pltpu.VMEM((2, 2, tm + BF16_SUBLANES, tf), F32), pltpu.VMEM((tm, d), F32),
                        pltpu.VMEM((tm, d), F32)],
        compiler_params=_params("arbitrary", "arbitrary"),
        name="mixer",
    )(x, x, mix_f, mix_f, mix_s, mix_s, mod, w_out, g_mlp, w_up, conv_w, conv_b, w_down, g_final)


def _chunk_columns(a, d_ff, n_chunks, tf):
    halves = jnp.stack([a[:, :d_ff], a[:, d_ff:]])
    halves = jnp.pad(halves, ((0, 0), (0, 0), (0, n_chunks * tf - d_ff)))
    return halves.reshape(2, a.shape[0], n_chunks, tf).transpose(0, 2, 1, 3)


def _pad_cols(a, n):
    return jnp.pad(a, ((0, 0), (0, n - a.shape[1])))


def kernel(x, c, w_ada, b_ada, g_attn, w_in, b_fgate, g_out_fox, g_out_sb, w_out,
           g_mlp, w_up, conv_w, conv_b, w_down, g_final):
    depth, d, _ = w_ada.shape
    n_fox = b_fgate.shape[1]
    d_fox = n_fox * HEAD_DIM
    d_sb = g_out_sb.shape[1]
    n_sb = d_sb // HEAD_DIM
    d_ff = w_down.shape[1]
    d_ff_pad = -(-d_ff // FF_CHUNK) * FF_CHUNK
    assert n_fox % 2 == 0 and n_sb == n_fox and 3 * n_fox <= LANES
    assert x.shape[1] % OUT_ROWS == 0 and x.shape[1] % FOX_Q == 0 and ATT_Q == 2 * ATT_K
    o_kf, o_vf, o_qs, o_ks, o_vs, o_gate = (d_fox, 2 * d_fox, 3 * d_fox, 3 * d_fox + d_sb,
                                             3 * d_fox + 2 * d_sb, 3 * d_fox + 3 * d_sb)

    for l in range(depth):
        mod = _ada(c, w_ada[l], b_ada[l]).reshape(-1, N_MOD, d)
        order = jnp.argsort(b_fgate[l])
        pick_head = (jnp.arange(n_fox)[:, None] == order[None, :]).astype(BF16)
        pick = jnp.kron(pick_head, jnp.eye(HEAD_DIM, dtype=BF16))
        w, wo = w_in[l].astype(BF16), w_out[l].astype(BF16)
        w_nat = jnp.concatenate([jnp.dot(w[:, :o_kf], pick), jnp.dot(w[:, o_kf:o_vf], pick), w[:, o_qs:o_vs]], axis=1)
        w_vt = jnp.concatenate([jnp.dot(pick.T, w[:, o_vf:o_qs].T), w[:, o_vs:o_gate].T], axis=0)
        w_gate = _pad_cols(jnp.dot(w[:, o_gate:], pick_head), LANES)
        b_gate = _pad_cols(b_fgate[l][order].reshape(1, n_fox), LANES)
        g_fox = g_out_fox[l].reshape(n_fox, HEAD_DIM)[order].reshape(-1)
        w_mix = jnp.concatenate([jnp.dot(pick.T, wo[:d_fox]), wo[d_fox:]], axis=0)
        qk, vt, k_aug, stats = _inproj(x, mod, g_attn[l].reshape(1, d), w_nat, w_vt, w_gate, b_gate, n_fox)

        steps_f, steps_s = n_fox // 2 // FOX_PAIRS, n_sb // 2 // SB_PAIRS
        fox_k_spec = pl.BlockSpec((1, 2 * FOX_PAIRS, x.shape[1], LANES), lambda b, p, i: (b, p, 0, 0))
        mix_f = _attention(_fox_kernel, "fox", FOX_PAIRS, FOX_Q, _fox_scratch(FOX_PAIRS), qk, k_aug, fox_k_spec, vt,
                           g_fox, q_block0=0, vt_block0=0, n_heads=n_fox, stats=stats)
        sb_k_spec = pl.BlockSpec((1, x.shape[1], SB_PAIRS * LANES), lambda b, p, i: (b, 0, 3 * steps_s + p))
        mix_s = _attention(_sb_kernel, "sb", SB_PAIRS, ATT_Q, _sb_scratch(SB_PAIRS), qk, qk, sb_k_spec, vt,
                           g_out_sb[l], q_block0=2 * steps_s, vt_block0=steps_s, n_heads=n_sb)

        n_ff = d_ff_pad // FF_CHUNK
        x = _mixer(x, mix_f, mix_s, mod, w_mix, g_mlp[l].reshape(1, d),
                   _chunk_columns(w_up[l], d_ff, n_ff, FF_CHUNK).astype(BF16),
                   _chunk_columns(conv_w[l], d_ff, n_ff, FF_CHUNK),
                   _chunk_columns(conv_b[l].reshape(1, -1), d_ff, n_ff, FF_CHUNK),
                   jnp.pad(w_down[l], ((0, d_ff_pad - d_ff), (0, 0))).astype(BF16).reshape(n_ff, FF_CHUNK, d),
                   g_final.reshape(1, d), final_norm=(l == depth - 1))
    return x
```

```python
import functools

import numpy as np
import jax
import jax.numpy as jnp
from jax import lax
from jax.experimental import pallas as pl
from jax.experimental.pallas import tpu as pltpu

HEAD_DIM = 64
N_MOD = 6
CONV_WIDTH = 3
EPS = 1e-6

LANES = 128
BF16_SUBLANES = 16
VMEM_LIMIT_BYTES = 48 * 1024 * 1024

ATT_Q = 512
FOX_Q = 2 * ATT_Q
ATT_K = 256
ATT_COLS = 256
FOX_PAIRS = 1
SB_PAIRS = 2
SB_WIDE_TILES = 4
SB_MORE_TILES = 2
PROJ_ROWS = 2 * ATT_K
LOG2E = 1.4426950408889634
MASKED = -1e30
M_INIT = -1e29
EXP2_MAX = 126.0
PRUNE_LOG2 = 152.0
NORM_SLACK = 1.02
STATS_ROWS = 8
OUT_ROWS = 512
FF_CHUNK = 256

F32 = jnp.float32
BF16 = jnp.bfloat16
NT_DIMS = (((1,), (1,)), ((), ()))


def _dot(a, b):
    return jnp.dot(a, b, preferred_element_type=F32)


def _dot_nt(a, b):
    return lax.dot_general(a, b, NT_DIMS, preferred_element_type=F32)


def _params(*sem):
    return pltpu.CompilerParams(dimension_semantics=sem, vmem_limit_bytes=VMEM_LIMIT_BYTES)


def _rms_rows(x):
    return x * lax.rsqrt(jnp.mean(x * x, axis=-1, keepdims=True) + EPS)


def _softplus(z):
    return jnp.maximum(z, 0.0) + jnp.log(1.0 + jnp.exp(-jnp.abs(z)))


def _split3(x):
    hi = x.astype(BF16)
    r1 = x - hi.astype(F32)
    mid = r1.astype(BF16)
    lo = (r1 - mid.astype(F32)).astype(BF16)
    return hi, mid, lo


def _ada_kernel(c_ref, w_ref, b_ref, o_ref):
    c = c_ref[...]
    o_ref[...] = _dot(c * jax.nn.sigmoid(c), w_ref[...]) + b_ref[...]


def _ada(c, w, b):
    bsz, d = c.shape
    n = w.shape[1]
    return pl.pallas_call(
        _ada_kernel,
        grid=(n // d,),
        in_specs=[pl.BlockSpec((bsz, d), lambda j: (0, 0)),
                  pl.BlockSpec((d, d), lambda j: (0, j)),
                  pl.BlockSpec((1, d), lambda j: (0, j))],
        out_specs=pl.BlockSpec((bsz, d), lambda j: (0, j)),
        out_shape=jax.ShapeDtypeStruct((bsz, n), F32),
        compiler_params=_params("arbitrary"),
        name="ada",
    )(c, w, b.reshape(1, n))


def _decay_prefix(lf, carry, n_heads):
    tk = lf.shape[0]
    lane = lax.broadcasted_iota(jnp.int32, (tk, LANES), 1)
    lf = jnp.where(lane < n_heads, lf, 0.0)
    row = lax.broadcasted_iota(jnp.int32, (tk, tk), 0)
    col = lax.broadcasted_iota(jnp.int32, (tk, tk), 1)
    tri = (col <= row).astype(BF16)
    hi, mid, lo = _split3(lf)
    terms = (hi.astype(F32) + pltpu.roll(mid.astype(F32), LANES // 4, 1)
             + pltpu.roll(lo.astype(F32), LANES // 2, 1)).astype(BF16)
    part = _dot(tri, terms)
    total = part + pltpu.roll(part, LANES - LANES // 4, 1) + pltpu.roll(part, LANES // 2, 1)
    return jnp.where(lane < n_heads, carry + total, 0.0)


def _decay_keys(f_run, k_fox, sel_ref, kaug_ref, rows, n_heads):
    tk = f_run.shape[0]
    lane = lax.broadcasted_iota(jnp.int32, (tk, LANES), 1)
    ghi, gmid, glo = _split3(-LOG2E * f_run)
    packed = (ghi.astype(F32) + pltpu.roll(gmid.astype(F32), n_heads, 1)
              + pltpu.roll(glo.astype(F32), 2 * n_heads, 1)).astype(BF16)
    placed = _dot(packed, sel_ref[...]).astype(BF16)
    for h in range(n_heads):
        pair = slice((h // 2) * LANES, (h // 2 + 1) * LANES)
        own = (lane < HEAD_DIM) if h % 2 == 0 else (lane >= HEAD_DIM)
        kaug_ref[0, h, rows, :] = jnp.where(own, k_fox[:, pair], placed[:, pair])


def _decay_stats(f_run, k_fox, ind_ref, kpre, stats_ref, tile, n_heads):
    tk = f_run.shape[0]
    k32 = k_fox.astype(F32)
    sq = _dot((k32 * k32).astype(BF16), ind_ref[...])
    kpre = jnp.maximum(kpre, jnp.sqrt(jnp.max(sq, axis=0, keepdims=True) * NORM_SLACK))
    g_end = -LOG2E * f_run[tk - 1:tk, :]
    lane1 = lax.broadcasted_iota(jnp.int32, (1, LANES), 1)

    def spread(v, h):
        return jnp.broadcast_to(jnp.sum(jnp.where(lane1 == h, v, 0.0), axis=1, keepdims=True), (1, ATT_COLS))

    for p in range(n_heads // 2):
        srows = [spread(v, 2 * p + hh) for v in (kpre, g_end) for hh in range(2)]
        srows += [jnp.zeros((1, ATT_COLS), F32)] * (stats_ref.shape[3] - len(srows))
        stats_ref[0, p, tile] = jnp.concatenate(srows, axis=0)
    return kpre


def _inproj_kernel(x_ref, mod_ref, g_ref, wn_ref, wvt_ref, wg_ref, bg_ref, sel_ref, ind_ref,
                   qk_ref, vt_ref, kaug_ref, stats_ref, carry_ref, kpre_ref, *, n_heads):
    @pl.when(pl.program_id(1) == 0)
    def _():
        carry_ref[...] = jnp.zeros_like(carry_ref)
        kpre_ref[...] = jnp.zeros_like(kpre_ref)

    shift = mod_ref[0, 0:1, :]
    scale = mod_ref[0, 1:2, :]
    h = (_rms_rows(x_ref[0]) * g_ref[...] * (1.0 + scale) + shift).astype(BF16)
    logit = _dot(h, wg_ref[...]) + bg_ref[...]
    log_f = -_softplus(-logit)
    qk = _dot(h, wn_ref[...]).astype(BF16)
    qk_ref[0] = qk
    d_grp = n_heads * HEAD_DIM
    tk = vt_ref.shape[3]
    tiles = [slice(t * tk, (t + 1) * tk) for t in range(vt_ref.shape[1])]
    f_run, carry = [], carry_ref[...]
    for rows in tiles:
        f_run.append(_decay_prefix(log_f[rows, :], carry, n_heads))
        carry = f_run[-1][tk - 1:tk, :]
    carry_ref[...] = carry
    v_t = lambda rows: _dot_nt(wvt_ref[...], h[rows, :]).astype(BF16)
    vt_ref[0, 0] = v_t(tiles[0])
    kpre = kpre_ref[...]
    for t, rows in enumerate(tiles):
        _decay_keys(f_run[t], qk[rows, d_grp:2 * d_grp], sel_ref, kaug_ref, rows, n_heads)
    for t, rows in enumerate(tiles):
        kpre = _decay_stats(f_run[t], qk[rows, d_grp:2 * d_grp], ind_ref, kpre, stats_ref, t, n_heads)
    kpre_ref[...] = kpre
    for t in range(1, len(tiles)):
        vt_ref[0, t] = v_t(tiles[t])


def _head_indicator(n_heads):
    ind = np.zeros((n_heads * HEAD_DIM, LANES), np.float32)
    ind[np.arange(n_heads * HEAD_DIM), np.arange(n_heads * HEAD_DIM) // HEAD_DIM] = 1.0
    return jnp.asarray(ind, BF16)


def _decay_select_matrix(n_heads):
    sel = np.zeros((LANES, n_heads // 2 * LANES), np.float32)
    for h in range(n_heads):
        base = h // 2 * LANES + (HEAD_DIM if h % 2 == 0 else 0)
        for term in range(3):
            sel[term * n_heads + h, base + term] = 1.0
    return jnp.asarray(sel, BF16)


def _inproj(x, mod, g, w_nat, w_vt, w_gate, b_gate, n_heads):
    bsz, s, d = x.shape
    tm, tk = PROJ_ROWS, ATT_K
    n_nat, n_v = w_nat.shape[1], w_vt.shape[0]
    const = lambda b, i: (0, 0)
    resident = lambda a: pl.BlockSpec(a.shape, const, pipeline_mode=pl.Buffered(1))
    sel, ind = _decay_select_matrix(n_heads), _head_indicator(n_heads)
    return pl.pallas_call(
        functools.partial(_inproj_kernel, n_heads=n_heads),
        grid=(bsz, s // tm),
        in_specs=[pl.BlockSpec((1, tm, d), lambda b, i: (b, i, 0)),
                  pl.BlockSpec((1, N_MOD, d), lambda b, i: (b, 0, 0)),
                  pl.BlockSpec((1, d), const),
                  resident(w_nat), resident(w_vt), resident(w_gate),
                  pl.BlockSpec((1, LANES), const),
                  resident(sel), resident(ind)],
        out_specs=[pl.BlockSpec((1, tm, n_nat), lambda b, i: (b, i, 0)),
                   pl.BlockSpec((1, tm // tk, n_v, tk), lambda b, i: (b, i, 0, 0)),
                   pl.BlockSpec((1, n_heads, tm, LANES), lambda b, i: (b, 0, i, 0)),
                   pl.BlockSpec((1, n_heads // 2, tm // tk, STATS_ROWS, ATT_COLS), lambda b, i: (b, 0, i, 0, 0))],
        out_shape=[jax.ShapeDtypeStruct((bsz, s, n_nat), BF16),
                   jax.ShapeDtypeStruct((bsz, s // tk, n_v, tk), BF16),
                   jax.ShapeDtypeStruct((bsz, n_heads, s, LANES), BF16),
                   jax.ShapeDtypeStruct((bsz, n_heads // 2, s // tk, STATS_ROWS, ATT_COLS), F32)],
        scratch_shapes=[pltpu.VMEM((1, LANES), F32), pltpu.VMEM((1, LANES), F32)],
        compiler_params=_params("arbitrary", "arbitrary"),
        name="inproj",
    )(x, mod, g, w_nat, w_vt, w_gate, b_gate, sel, ind)


def _lane_queries(q_ref, extra_even, extra_odd, cw):
    out = []
    for pp in range(q_ref.shape[2] // LANES):
        q = q_ref[0, :, pp * LANES:(pp + 1) * LANES].astype(F32) * (HEAD_DIM ** -0.5 * LOG2E)
        lane = lax.broadcasted_iota(jnp.int32, q.shape, 1)
        heads = (jnp.where(lane < HEAD_DIM, q, extra_even(lane)).T.astype(BF16),
                 jnp.where(lane >= HEAD_DIM, q, extra_odd(lane)).T.astype(BF16))
        out += [heads[hh][:, c * cw:(c + 1) * cw] for hh in range(2) for c in range(q.shape[0] // cw)]
    return out


def _visibility(first_key, first_query, bk, cw, strict):
    last_visible_gap = -1 if strict else 0
    if first_key + bk - 1 - first_query <= last_visible_gap:
        return "all"
    if first_key - (first_query + cw - 1) > last_visible_gap:
        return "none"
    gap = (lax.broadcasted_iota(jnp.int32, (bk, cw), 0) - lax.broadcasted_iota(jnp.int32, (bk, cw), 1)
           + (first_key - first_query))
    return gap <= last_visible_gap


def _diag_visibility(u, c, bk, cw, strict):
    return _visibility((1 - u) * bk, c * cw, bk, cw, strict)


def _hidden(visibility):
    return isinstance(visibility, str) and visibility == "none"


def _query_norm_bounds(queries, n_chunks):
    bounds = []
    for li, q in enumerate(queries):
        hh = (li // n_chunks) % 2
        own = q[hh * HEAD_DIM:(hh + 1) * HEAD_DIM, :].astype(F32)
        bounds.append(jnp.sqrt(jnp.sum(own * own, axis=0, keepdims=True) * NORM_SLACK))
    return bounds


def _finish_heads(lanes, g_ref, o_ref):
    n_pairs = o_ref.shape[2] // LANES
    n_chunks = len(lanes) // (2 * n_pairs)
    for pp in range(n_pairs):
        mine = lanes[2 * pp * n_chunks:2 * (pp + 1) * n_chunks]
        outs = [jnp.concatenate(mine[hh * n_chunks:(hh + 1) * n_chunks], axis=1) for hh in range(2)]
        normed = [o * lax.rsqrt(jnp.mean(o * o, axis=0, keepdims=True) + EPS) for o in outs]
        cols = slice(pp * LANES, (pp + 1) * LANES)
        o_ref[0, :, cols] = (jnp.concatenate(normed, axis=0).T * g_ref[:, cols]).astype(o_ref.dtype)


def _fox_kernel(q_ref, k_ref, vt_ref, g_ref, stats_ref, o_ref, s_buf, cmax_buf, p_buf, acc_buf):
    bk = vt_ref.shape[3]
    n_lanes, cw = acc_buf.shape[0], acc_buf.shape[2]
    n_chunks = q_ref.shape[1] // cw
    chunks_per_sub = ATT_Q // cw
    assert q_ref.shape[1] == 2 * ATT_Q
    lane_group = lambda li: (li // (2 * n_chunks), (li // n_chunks) % 2, li % n_chunks)
    first_q_block = 2 * pl.program_id(2)
    n_tiles = 2 * (first_q_block + 1)
    ones3 = lambda lo: (lambda lane: jnp.where((lane >= lo) & (lane < lo + 3), 1.0, 0.0))
    queries = _lane_queries(q_ref, ones3(HEAD_DIM), ones3(0), cw)
    acc_buf[...] = jnp.zeros(acc_buf.shape, F32)
    for li in range(n_lanes):
        if _hidden(_diag_visibility(0, lane_group(li)[2] % chunks_per_sub, bk, cw, strict=False)):
            p_buf[0, li] = jnp.zeros((bk, cw), BF16)

    full, idle = ("below", "below", True), (None, None, False)

    def step(t, slot, carry, stages=(full, full)):
        new = []
        for li in range(n_lanes):
            pp, hh, chunk = lane_group(li)
            sub, c = divmod(chunk, chunks_per_sub)
            score, softmax, value = stages[sub]
            see = lambda u: "all" if u == "below" else _diag_visibility(u, c, bk, cw, strict=False)
            if score is not None and not _hidden(see(score)):
                start = pl.multiple_of((n_tiles - 2 - t) * bk, bk)
                s_new = _dot(k_ref[0, 2 * pp + hh, pl.ds(start, bk), :], queries[li])
                if not isinstance(see(score), str):
                    s_new = jnp.where(see(score), s_new, MASKED)
                s_buf[1 - slot, li] = s_new
                cmax_buf[1 - slot, li] = jnp.max(s_new, axis=0, keepdims=True)
            pv = None
            if value:
                vt = vt_ref[0, n_tiles - t, pl.ds(pp * LANES + hh * HEAD_DIM, HEAD_DIM), :]
                pv = _dot(vt, p_buf[1 - slot, li])
            m, l = carry[li]
            if softmax is not None and not _hidden(see(softmax)):
                m_new = jnp.maximum(m, cmax_buf[slot, li])
                alpha = jnp.exp2(m - m_new)
                p = jnp.exp2(s_buf[slot, li] - m_new)
                p_buf[slot, li] = p.astype(BF16)
                m, l = m_new, alpha * l + jnp.sum(p, axis=0, keepdims=True)
                acc_buf[li] = alpha * (acc_buf[li] if pv is None else acc_buf[li] + pv)
            elif pv is not None:
                acc_buf[li] += pv
            new.append((m, l))
        return tuple(new)

    def step_pair(i, carry):
        t = 2 * i + 1
        return step(t + 1, 0, step(t, 1, carry))

    q_norm = _query_norm_bounds(queries, n_chunks)

    def later_tiles_matter(i, carry):
        j_rest = jnp.maximum(n_tiles - 5 - 2 * i, 0)
        worst = None
        for li in range(n_lanes):
            pp, hh, _ = lane_group(li)
            bound = (q_norm[li] * stats_ref[0, pp, j_rest, hh:hh + 1, :]
                     + stats_ref[0, pp, j_rest, 2 + hh:3 + hh, :] - carry[li][0])
            worst = bound if worst is None else jnp.maximum(worst, bound)
        return jnp.max(worst) >= -PRUNE_LOG2

    def pair_and_check(state):
        i, _, carry = state
        carry = step_pair(i, carry)
        return i + 1, later_tiles_matter(i, carry), carry

    carry = tuple((jnp.full((1, cw), M_INIT, F32), jnp.zeros((1, cw), F32)) for _ in range(n_lanes))
    carry = step(-3, 1, carry, (idle, (0, None, False)))
    carry = step(-2, 0, carry, (idle, (1, 0, False)))
    carry = step(-1, 1, carry, ((0, None, False), full))
    carry = step(0, 0, carry, ((1, 0, False), full))
    n_pairs, _, carry = lax.while_loop(lambda st: (st[0] < first_q_block) & st[1], pair_and_check,
                                       (jnp.int32(0), jnp.bool_(True), carry))
    drain = (None, "below", True)
    carry = step(2 * n_pairs + 1, 1, carry, (drain, drain))
    drain = (None, None, True)
    carry = step(2 * n_pairs + 2, 0, carry, (drain, drain))
    _finish_heads([acc_buf[li] / carry[li][1] for li in range(n_lanes)], g_ref, o_ref)


def _sb_kernel(q_ref, k_ref, vt_ref, g_ref, o_ref, z_buf, sp_buf, e_buf, wrow_buf, later_buf, acc_buf):
    qi = pl.program_id(2)
    bk = vt_ref.shape[3]
    n_wide, n_first = z_buf.shape[0], SB_WIDE_TILES
    n_lanes, cw = acc_buf.shape[0], acc_buf.shape[2]
    n_chunks = q_ref.shape[1] // cw
    lane_group = lambda li: (li // (2 * n_chunks), (li // n_chunks) % 2, li % n_chunks)
    n_tiles = 2 * (qi + 1)
    zero = lambda lane: 0.0
    queries = _lane_queries(q_ref, zero, zero, cw)
    suffix = (lax.broadcasted_iota(jnp.int32, (bk, bk), 1)
              >= lax.broadcasted_iota(jnp.int32, (bk, bk), 0)).astype(BF16)
    softplus2 = lambda z: jnp.maximum(z, jnp.log2(1.0 + jnp.exp2(jnp.minimum(z, EXP2_MAX))))
    keys = lambda j, pp: k_ref[0, pl.ds(pl.multiple_of(j * bk, bk), bk), pp * LANES:(pp + 1) * LANES]
    values = lambda j, pp, hh: vt_ref[0, j, pl.ds(pp * LANES + hh * HEAD_DIM, HEAD_DIM), :]

    sees = lambda u, li: _diag_visibility(u, lane_group(li)[2], bk, cw, strict=True) if u < 2 else "all"

    def wide_block(tiles, later, check_exists):
        live = [(u, li) for u in tiles for li in range(n_lanes) if not _hidden(sees(u, li))]
        later, col_sums = list(later), {}

        def score(u, li):
            z = _dot(keys(jnp.maximum(n_tiles - 1 - u, 0), lane_group(li)[0]), queries[li])
            if not isinstance(sees(u, li), str):
                z = jnp.where(sees(u, li), z, MASKED)
            if check_exists and u >= 2:
                z = jnp.where(u < n_tiles, z, MASKED)
            z_buf[u, li] = z

        def softplus(u, li):
            sp_buf[u, li] = softplus2(z_buf[u, li]).astype(BF16)

        def cumsum(u, li):
            within = _dot(suffix, sp_buf[u, li])
            col_sums[u, li] = within[0:1, :]
            z_buf[u, li] = z_buf[u, li] - within

        def weight(u, li):
            pp, hh, _ = lane_group(li)
            a = jnp.exp2(z_buf[u, li] - later[li])
            acc_buf[li] += _dot(values(jnp.maximum(n_tiles - 1 - u, 0), pp, hh), a.astype(BF16))
            later[li] = later[li] + col_sums[u, li]

        stages = (score, softplus, cumsum, weight)
        for pos in range(len(live) + len(stages) - 1):
            for lag, stage in enumerate(stages):
                if 0 <= pos - lag < len(live):
                    stage(*live[pos - lag])
        return later

    n_rest = n_tiles - n_wide

    def rest_matters(mass):
        least = mass[0]
        for li in range(1, n_lanes):
            least = jnp.minimum(least, mass[li])
        return jnp.min(least) <= PRUNE_LOG2

    def step(t, slot, later, score=True, softplus=True, cumsum=True, weight=True):
        new_later = []
        for li in range(n_lanes):
            pp, hh, _ = lane_group(li)
            if cumsum:
                within = _dot(suffix, sp_buf[1 - slot, li])
                e_buf[1 - slot, li] = z_buf[1 - slot, li] - within
                wrow_buf[1 - slot, li] = within[0:1, :]
            if score:
                z_buf[1 - slot, li] = _dot(keys(n_rest - 1 - (t + 3), pp), queries[li])
            if weight:
                a = jnp.exp2(e_buf[slot, li] - later[li])
                acc_buf[li] += _dot(values(n_rest - 1 - jnp.maximum(t, 0), pp, hh), a.astype(BF16))
                new_later.append(later[li] + wrow_buf[slot, li])
            else:
                new_later.append(later[li])
            if softplus:
                sp_buf[slot, li] = softplus2(z_buf[slot, li]).astype(BF16)
        return tuple(new_later)

    def pair_and_check(state):
        i, _, later = state
        t = 2 * i - 1
        later = step(t + 1, 0, step(t, 1, later))
        mass = [later[li] + wrow_buf[1, li] for li in range(n_lanes)]
        return i + 1, rest_matters(mass), later

    acc_buf[...] = jnp.zeros(acc_buf.shape, F32)
    first = wide_block(range(n_first), [jnp.zeros((1, cw), F32)] * n_lanes, check_exists=True)
    for li in range(n_lanes):
        later_buf[li] = first[li]
    swept = lambda: [later_buf[li] for li in range(n_lanes)]
    more_matters = (n_rest + (n_wide - n_first) > 0) & rest_matters(first)

    @pl.when(more_matters)
    def _():
        more = wide_block(range(n_first, n_wide), first, check_exists=False)
        for li in range(n_lanes):
            later_buf[li] = more[li]

    @pl.when(more_matters & (n_rest > 0) & rest_matters(swept()))
    def _():
        later = swept()
        e_buf[1] = jnp.full(e_buf.shape[1:], MASKED, F32)
        wrow_buf[1] = jnp.zeros(wrow_buf.shape[1:], F32)
        mass = step(-3, 1, tuple(later), softplus=False, cumsum=False, weight=False)
        mass = step(-2, 0, mass, cumsum=False, weight=False)
        n_pairs, _, mass = lax.while_loop(lambda st: (2 * st[0] + 2 < n_rest) & st[1], pair_and_check,
                                          (jnp.int32(0), jnp.bool_(True), mass))
        mass = step(2 * n_pairs - 1, 1, mass, score=False)
        mass = step(2 * n_pairs, 0, mass, score=False, softplus=False)
        step(2 * n_pairs + 1, 1, mass, score=False, softplus=False, cumsum=False)

    _finish_heads([acc_buf[li] for li in range(n_lanes)], g_ref, o_ref)


def _attention(body, name, pairs, bq, scratch, qk, k_arr, k_spec, vt, g, q_block0, vt_block0, n_heads, stats=None):
    bsz, s, _ = qk.shape
    bk, width = ATT_K, pairs * LANES
    d_grp = n_heads * HEAD_DIM
    assert (n_heads // 2) % pairs == 0
    in_specs = [pl.BlockSpec((1, bq, width), lambda b, p, i: (b, i, q_block0 + p)),
                k_spec,
                pl.BlockSpec((1, s // bk, width, bk), lambda b, p, i: (b, 0, vt_block0 + p, 0)),
                pl.BlockSpec((1, width), lambda b, p, i: (0, p))]
    operands = [qk, k_arr, vt, g.reshape(1, d_grp)]
    if stats is not None:
        in_specs.append(pl.BlockSpec((1, pairs) + stats.shape[2:], lambda b, p, i: (b, p, 0, 0, 0)))
        operands.append(stats)
    return pl.pallas_call(
        body,
        grid=(bsz, n_heads // 2 // pairs, s // bq),
        in_specs=in_specs,
        out_specs=pl.BlockSpec((1, bq, width), lambda b, p, i: (b, i, p)),
        out_shape=jax.ShapeDtypeStruct((bsz, s, d_grp), BF16),
        scratch_shapes=scratch,
        compiler_params=_params("arbitrary", "arbitrary", "arbitrary"),
        name=name,
    )(*operands)


def _lane_groups(pairs, bq):
    return pairs * 2 * (bq // ATT_COLS)


def _fox_scratch(pairs):
    n = _lane_groups(pairs, FOX_Q)
    return [pltpu.VMEM((2, n, ATT_K, ATT_COLS), F32), pltpu.VMEM((2, n, 1, ATT_COLS), F32),
            pltpu.VMEM((2, n, ATT_K, ATT_COLS), BF16), pltpu.VMEM((n, HEAD_DIM, ATT_COLS), F32)]


def _sb_scratch(pairs):
    n = _lane_groups(pairs, ATT_Q)
    wide = SB_WIDE_TILES + SB_MORE_TILES
    return [pltpu.VMEM((wide, n, ATT_K, ATT_COLS), F32), pltpu.VMEM((wide, n, ATT_K, ATT_COLS), BF16),
            pltpu.VMEM((2, n, ATT_K, ATT_COLS), F32), pltpu.VMEM((2, n, 1, ATT_COLS), F32),
            pltpu.VMEM((n, 1, ATT_COLS), F32), pltpu.VMEM((n, HEAD_DIM, ATT_COLS), F32)]


def _mixer_kernel(x_ref, xh_ref, mf_ref, mfh_ref, ms_ref, msh_ref, mod_ref, wo_ref, gm_ref, wu_ref, cw_ref,
                  cb_ref, wd_ref, gf_ref, o_ref, u_buf, acc_ref, x1_buf, *, final_norm):
    i = pl.program_id(1)
    tm = x_ref.shape[1]
    n_chunks = wd_ref.shape[0]
    mix = jnp.concatenate([jnp.concatenate([mfh_ref[0], msh_ref[0]], axis=-1),
                           jnp.concatenate([mf_ref[0], ms_ref[0]], axis=-1)], axis=0)
    x_ext = jnp.concatenate([xh_ref[0], x_ref[0]], axis=0)
    x1_ext = x_ext + mod_ref[0, 2:3, :] * _dot(mix, wo_ref[...])
    shift = mod_ref[0, 3:4, :]
    scale = mod_ref[0, 4:5, :]
    h_ext = _rms_rows(x1_ext) * gm_ref[...] * (1.0 + scale) + shift
    row = lax.broadcasted_iota(jnp.int32, h_ext.shape, 0)
    hx = jnp.where((row >= BF16_SUBLANES) | (i > 0), h_ext, 0.0).astype(BF16)
    x1_buf[...] = x1_ext[BF16_SUBLANES:, :]
    acc_ref[...] = jnp.zeros_like(acc_ref)

    def project_up(c, slot):
        for br in range(2):
            u_buf[slot, br] = _dot(hx, wu_ref[br, c])

    def mix_down(c, slot):
        branches = []
        for br in range(2):
            out = cb_ref[br, c]
            for tap in range(CONV_WIDTH):
                first = BF16_SUBLANES - (CONV_WIDTH - 1 - tap)
                out = out + cw_ref[br, c, tap:tap + 1, :] * u_buf[slot, br, pl.ds(first, tm), :]
            branches.append(out)
        u_gate, u_val = branches
        acc_ref[...] += _dot((u_gate * jax.nn.sigmoid(u_gate) * u_val).astype(BF16), wd_ref[c])

    def chunk_pair(j, _):
        c = 2 * j
        project_up(c + 1, 1)
        mix_down(c, 0)
        project_up(c + 2, 0)
        mix_down(c + 1, 1)
        return 0

    project_up(0, 0)
    lax.fori_loop(0, (n_chunks - 1) // 2, chunk_pair, 0)
    mix_down(n_chunks - 1, 0)
    x2 = x1_buf[...] + mod_ref[0, 5:6, :] * acc_ref[...]
    o_ref[0] = _rms_rows(x2) * gf_ref[...] if final_norm else x2


def _mixer(x, mix_f, mix_s, mod, w_out, g_mlp, w_up, conv_w, conv_b, w_down, g_final, final_norm):
    bsz, s, d = x.shape
    tm = OUT_ROWS
    n_chunks, tf = w_down.shape[0], w_down.shape[1]
    assert n_chunks % 2 == 1
    halo_blocks = tm // BF16_SUBLANES
    row = lambda b, i: (b, i, 0)
    halo = lambda b, i: (b, jnp.maximum(i * halo_blocks - 1, 0), 0)
    tile_and_halo = lambda a: [pl.BlockSpec((1, tm, a.shape[2]), row),
                               pl.BlockSpec((1, BF16_SUBLANES, a.shape[2]), halo)]
    resident = lambda a: pl.BlockSpec(a.shape, lambda b, i: (0,) * a.ndim, pipeline_mode=pl.Buffered(1))
    return pl.pallas_call(
        functools.partial(_mixer_kernel, final_norm=final_norm),
        grid=(bsz, s // tm),
        in_specs=tile_and_halo(x) + tile_and_halo(mix_f) + tile_and_halo(mix_s)
                 + [pl.BlockSpec((1, N_MOD, d), lambda b, i: (b, 0, 0)), resident(w_out),
                    pl.BlockSpec((1, d), lambda b, i: (0, 0)),
                    resident(w_up), resident(conv_w), resident(conv_b), resident(w_down),
                    pl.BlockSpec((1, d), lambda b, i: (0, 0))],
        out_specs=pl.BlockSpec((1, tm, d), row),
        out_shape=jax.ShapeDtypeStruct((bsz, s, d), F32),
        scratch_shapes=[pltpu.VMEM((2, 2, tm + BF16_SUBLANES, tf), F32), pltpu.VMEM((tm, d), F32),
                        pltpu.VMEM((tm, d), F32)],
        compiler_params=_params("arbitrary", "arbitrary"),
        name="mixer",
    )(x, x, mix_f, mix_f, mix_s, mix_s, mod, w_out, g_mlp, w_up, conv_w, conv_b, w_down, g_final)


def _chunk_columns(a, d_ff, n_chunks, tf):
    halves = jnp.stack([a[:, :d_ff], a[:, d_ff:]])
    halves = jnp.pad(halves, ((0, 0), (0, 0), (0, n_chunks * tf - d_ff)))
    return halves.reshape(2, a.shape[0], n_chunks, tf).transpose(0, 2, 1, 3)


def _pad_cols(a, n):
    return jnp.pad(a, ((0, 0), (0, n - a.shape[1])))


def kernel(x, c, w_ada, b_ada, g_attn, w_in, b_fgate, g_out_fox, g_out_sb, w_out,
           g_mlp, w_up, conv_w, conv_b, w_down, g_final):
    depth, d, _ = w_ada.shape
    n_fox = b_fgate.shape[1]
    d_fox = n_fox * HEAD_DIM
    d_sb = g_out_sb.shape[1]
    n_sb = d_sb // HEAD_DIM
    d_ff = w_down.shape[1]
    d_ff_pad = -(-d_ff // FF_CHUNK) * FF_CHUNK
    assert n_fox % 2 == 0 and n_sb == n_fox and 3 * n_fox <= LANES
    assert x.shape[1] % OUT_ROWS == 0 and x.shape[1] % FOX_Q == 0 and ATT_Q == 2 * ATT_K
    o_kf, o_vf, o_qs, o_ks, o_vs, o_gate = (d_fox, 2 * d_fox, 3 * d_fox, 3 * d_fox + d_sb,
                                             3 * d_fox + 2 * d_sb, 3 * d_fox + 3 * d_sb)

    for l in range(depth):
        mod = _ada(c, w_ada[l], b_ada[l]).reshape(-1, N_MOD, d)
        order = jnp.argsort(b_fgate[l])
        pick_head = (jnp.arange(n_fox)[:, None] == order[None, :]).astype(BF16)
        pick = jnp.kron(pick_head, jnp.eye(HEAD_DIM, dtype=BF16))
        w, wo = w_in[l].astype(BF16), w_out[l].astype(BF16)
        w_nat = jnp.concatenate([jnp.dot(w[:, :o_kf], pick), jnp.dot(w[:, o_kf:o_vf], pick), w[:, o_qs:o_vs]], axis=1)
        w_vt = jnp.concatenate([jnp.dot(pick.T, w[:, o_vf:o_qs].T), w[:, o_vs:o_gate].T], axis=0)
        w_gate = _pad_cols(jnp.dot(w[:, o_gate:], pick_head), LANES)
        b_gate = _pad_cols(b_fgate[l][order].reshape(1, n_fox), LANES)
        g_fox = g_out_fox[l].reshape(n_fox, HEAD_DIM)[order].reshape(-1)
        w_mix = jnp.concatenate([jnp.dot(pick.T, wo[:d_fox]), wo[d_fox:]], axis=0)
        qk, vt, k_aug, stats = _inproj(x, mod, g_attn[l].reshape(1, d), w_nat, w_vt, w_gate, b_gate, n_fox)

        steps_f, steps_s = n_fox // 2 // FOX_PAIRS, n_sb // 2 // SB_PAIRS
        fox_k_spec = pl.BlockSpec((1, 2 * FOX_PAIRS, x.shape[1], LANES), lambda b, p, i: (b, p, 0, 0))
        mix_f = _attention(_fox_kernel, "fox", FOX_PAIRS, FOX_Q, _fox_scratch(FOX_PAIRS), qk, k_aug, fox_k_spec, vt,
                           g_fox, q_block0=0, vt_block0=0, n_heads=n_fox, stats=stats)
        sb_k_spec = pl.BlockSpec((1, x.shape[1], SB_PAIRS * LANES), lambda b, p, i: (b, 0, 3 * steps_s + p))
        mix_s = _attention(_sb_kernel, "sb", SB_PAIRS, ATT_Q, _sb_scratch(SB_PAIRS), qk, qk, sb_k_spec, vt,
                           g_out_sb[l], q_block0=2 * steps_s, vt_block0=steps_s, n_heads=n_sb)

        n_ff = d_ff_pad // FF_CHUNK
        x = _mixer(x, mix_f, mix_s, mod, w_mix, g_mlp[l].reshape(1, d),
                   _chunk_columns(w_up[l], d_ff, n_ff, FF_CHUNK).astype(BF16),
                   _chunk_columns(conv_w[l], d_ff, n_ff, FF_CHUNK),
                   _chunk_columns(conv_b[l].reshape(1, -1), d_ff, n_ff, FF_CHUNK),
                   jnp.pad(w_down[l], ((0, d_ff_pad - d_ff), (0, 0))).astype(BF16).reshape(n_ff, FF_CHUNK, d),
                   g_final.reshape(1, d), final_norm=(l == depth - 1))
    return x
```

```python
import functools

import numpy as np
import jax
import jax.numpy as jnp
from jax import lax
from jax.experimental import pallas as pl
from jax.experimental.pallas import tpu as pltpu

HEAD_DIM = 64
N_MOD = 6
CONV_WIDTH = 3
EPS = 1e-6

LANES = 128
F32_SUBLANES = 8
BF16_SUBLANES = 16
VMEM_LIMIT_BYTES = 48 * 1024 * 1024

ATT_Q = 512
FOX_Q = 2 * ATT_Q
ATT_K = 256
ATT_COLS = 256
FOX_PAIRS = 1
SB_PAIRS = 2
SB_WIDE_TILES = 4
SB_MORE_TILES = 2
PROJ_ROWS = 2 * ATT_K
LOG2E = 1.4426950408889634
MASKED = -1e30
M_INIT = -1e29
EXP2_MAX = 126.0
PRUNE_LOG2 = 152.0
NORM_SLACK = 1.02
STATS_ROWS = 8
OUT_ROWS = 512
FF_CHUNK = 256

F32 = jnp.float32
BF16 = jnp.bfloat16
NT_DIMS = (((1,), (1,)), ((), ()))


def _dot(a, b):
    return jnp.dot(a, b, preferred_element_type=F32)


def _dot_nt(a, b):
    return lax.dot_general(a, b, NT_DIMS, preferred_element_type=F32)


def _params(*sem):
    return pltpu.CompilerParams(dimension_semantics=sem, vmem_limit_bytes=VMEM_LIMIT_BYTES)


def _rms_rows(x):
    return x * lax.rsqrt(jnp.mean(x * x, axis=-1, keepdims=True) + EPS)


def _softplus(z):
    return jnp.maximum(z, 0.0) + jnp.log(1.0 + jnp.exp(-jnp.abs(z)))


def _split3(x):
    hi = x.astype(BF16)
    r1 = x - hi.astype(F32)
    mid = r1.astype(BF16)
    lo = (r1 - mid.astype(F32)).astype(BF16)
    return hi, mid, lo


def _ada_kernel(c_ref, w_ref, b_ref, o_ref):
    c = c_ref[...]
    o_ref[...] = _dot(c * jax.nn.sigmoid(c), w_ref[...]) + b_ref[...]


def _ada(c, w, b):
    bsz, d = c.shape
    n = w.shape[1]
    return pl.pallas_call(
        _ada_kernel,
        grid=(n // d,),
        in_specs=[pl.BlockSpec((bsz, d), lambda j: (0, 0)),
                  pl.BlockSpec((d, d), lambda j: (0, j)),
                  pl.BlockSpec((1, d), lambda j: (0, j))],
        out_specs=pl.BlockSpec((bsz, d), lambda j: (0, j)),
        out_shape=jax.ShapeDtypeStruct((bsz, n), F32),
        compiler_params=_params("arbitrary"),
        name="ada",
    )(c, w, b.reshape(1, n))


def _decay_prefix(lf, carry, n_heads):
    tk = lf.shape[0]
    lane = lax.broadcasted_iota(jnp.int32, (tk, LANES), 1)
    lf = jnp.where(lane < n_heads, lf, 0.0)
    row = lax.broadcasted_iota(jnp.int32, (tk, tk), 0)
    col = lax.broadcasted_iota(jnp.int32, (tk, tk), 1)
    tri = (col <= row).astype(BF16)
    hi, mid, lo = _split3(lf)
    terms = (hi.astype(F32) + pltpu.roll(mid.astype(F32), LANES // 4, 1)
             + pltpu.roll(lo.astype(F32), LANES // 2, 1)).astype(BF16)
    part = _dot(tri, terms)
    total = part + pltpu.roll(part, LANES - LANES // 4, 1) + pltpu.roll(part, LANES // 2, 1)
    return jnp.where(lane < n_heads, carry + total, 0.0)


def _decay_keys(f_run, k_fox, sel_ref, kaug_ref, rows, n_heads):
    tk = f_run.shape[0]
    lane = lax.broadcasted_iota(jnp.int32, (tk, LANES), 1)
    ghi, gmid, glo = _split3(-LOG2E * f_run)
    packed = (ghi.astype(F32) + pltpu.roll(gmid.astype(F32), n_heads, 1)
              + pltpu.roll(glo.astype(F32), 2 * n_heads, 1)).astype(BF16)
    placed = _dot(packed, sel_ref[...]).astype(BF16)
    for h in range(n_heads):
        pair = slice((h // 2) * LANES, (h // 2 + 1) * LANES)
        own = (lane < HEAD_DIM) if h % 2 == 0 else (lane >= HEAD_DIM)
        kaug_ref[0, h, rows, :] = jnp.where(own, k_fox[:, pair], placed[:, pair])


def _decay_stats(f_run, k_fox, ind_ref, kpre, stats_ref, tile, n_heads):
    tk = f_run.shape[0]
    k32 = k_fox.astype(F32)
    sq = _dot((k32 * k32).astype(BF16), ind_ref[...])
    kpre = jnp.maximum(kpre, jnp.sqrt(jnp.max(sq, axis=0, keepdims=True) * NORM_SLACK))
    g_end = -LOG2E * f_run[tk - 1:tk, :]
    lane1 = lax.broadcasted_iota(jnp.int32, (1, LANES), 1)

    def spread(v, h):
        return jnp.broadcast_to(jnp.sum(jnp.where(lane1 == h, v, 0.0), axis=1, keepdims=True), (1, ATT_COLS))

    for p in range(n_heads // 2):
        srows = [spread(v, 2 * p + hh) for v in (kpre, g_end) for hh in range(2)]
        srows += [jnp.zeros((1, ATT_COLS), F32)] * (stats_ref.shape[3] - len(srows))
        stats_ref[0, p, tile] = jnp.concatenate(srows, axis=0)
    return kpre


def _inproj_kernel(x_ref, mod_ref, g_ref, wn_ref, wvt_ref, wg_ref, bg_ref, sel_ref, ind_ref,
                   qk_ref, vt_ref, kaug_ref, stats_ref, carry_ref, kpre_ref, *, n_heads):
    @pl.when(pl.program_id(1) == 0)
    def _():
        carry_ref[...] = jnp.zeros_like(carry_ref)
        kpre_ref[...] = jnp.zeros_like(kpre_ref)

    shift = mod_ref[0, 0:1, :]
    scale = mod_ref[0, 1:2, :]
    h = (_rms_rows(x_ref[0]) * g_ref[...] * (1.0 + scale) + shift).astype(BF16)
    logit = _dot(h, wg_ref[...]) + bg_ref[...]
    log_f = -_softplus(-logit)
    qk = _dot(h, wn_ref[...]).astype(BF16)
    qk_ref[0] = qk
    d_grp = n_heads * HEAD_DIM
    tk = vt_ref.shape[3]
    tiles = [slice(t * tk, (t + 1) * tk) for t in range(vt_ref.shape[1])]
    f_run, carry = [], carry_ref[...]
    for rows in tiles:
        f_run.append(_decay_prefix(log_f[rows, :], carry, n_heads))
        carry = f_run[-1][tk - 1:tk, :]
    carry_ref[...] = carry
    v_t = lambda rows: _dot_nt(wvt_ref[...], h[rows, :]).astype(BF16)
    vt_ref[0, 0] = v_t(tiles[0])
    kpre = kpre_ref[...]
    for t, rows in enumerate(tiles):
        _decay_keys(f_run[t], qk[rows, d_grp:2 * d_grp], sel_ref, kaug_ref, rows, n_heads)
    for t, rows in enumerate(tiles):
        kpre = _decay_stats(f_run[t], qk[rows, d_grp:2 * d_grp], ind_ref, kpre, stats_ref, t, n_heads)
    kpre_ref[...] = kpre
    for t in range(1, len(tiles)):
        vt_ref[0, t] = v_t(tiles[t])


def _head_indicator(n_heads):
    ind = np.zeros((n_heads * HEAD_DIM, LANES), np.float32)
    ind[np.arange(n_heads * HEAD_DIM), np.arange(n_heads * HEAD_DIM) // HEAD_DIM] = 1.0
    return jnp.asarray(ind, BF16)


def _decay_select_matrix(n_heads):
    sel = np.zeros((LANES, n_heads // 2 * LANES), np.float32)
    for h in range(n_heads):
        base = h // 2 * LANES + (HEAD_DIM if h % 2 == 0 else 0)
        for term in range(3):
            sel[term * n_heads + h, base + term] = 1.0
    return jnp.asarray(sel, BF16)


def _inproj(x, mod, g, w_nat, w_vt, w_gate, b_gate, n_heads):
    bsz, s, d = x.shape
    tm, tk = PROJ_ROWS, ATT_K
    n_nat, n_v = w_nat.shape[1], w_vt.shape[0]
    const = lambda b, i: (0, 0)
    resident = lambda a: pl.BlockSpec(a.shape, const, pipeline_mode=pl.Buffered(1))
    sel, ind = _decay_select_matrix(n_heads), _head_indicator(n_heads)
    return pl.pallas_call(
        functools.partial(_inproj_kernel, n_heads=n_heads),
        grid=(bsz, s // tm),
        in_specs=[pl.BlockSpec((1, tm, d), lambda b, i: (b, i, 0)),
                  pl.BlockSpec((1, N_MOD, d), lambda b, i: (b, 0, 0)),
                  pl.BlockSpec((1, d), const),
                  resident(w_nat), resident(w_vt), resident(w_gate),
                  pl.BlockSpec((1, LANES), const),
                  resident(sel), resident(ind)],
        out_specs=[pl.BlockSpec((1, tm, n_nat), lambda b, i: (b, i, 0)),
                   pl.BlockSpec((1, tm // tk, n_v, tk), lambda b, i: (b, i, 0, 0)),
                   pl.BlockSpec((1, n_heads, tm, LANES), lambda b, i: (b, 0, i, 0)),
                   pl.BlockSpec((1, n_heads // 2, tm // tk, STATS_ROWS, ATT_COLS), lambda b, i: (b, 0, i, 0, 0))],
        out_shape=[jax.ShapeDtypeStruct((bsz, s, n_nat), BF16),
                   jax.ShapeDtypeStruct((bsz, s // tk, n_v, tk), BF16),
                   jax.ShapeDtypeStruct((bsz, n_heads, s, LANES), BF16),
                   jax.ShapeDtypeStruct((bsz, n_heads // 2, s // tk, STATS_ROWS, ATT_COLS), F32)],
        scratch_shapes=[pltpu.VMEM((1, LANES), F32), pltpu.VMEM((1, LANES), F32)],
        compiler_params=_params("arbitrary", "arbitrary"),
        name="inproj",
    )(x, mod, g, w_nat, w_vt, w_gate, b_gate, sel, ind)


def _lane_queries(q_ref, extra_even, extra_odd, cw):
    out = []
    for pp in range(q_ref.shape[2] // LANES):
        q = q_ref[0, :, pp * LANES:(pp + 1) * LANES].astype(F32) * (HEAD_DIM ** -0.5 * LOG2E)
        lane = lax.broadcasted_iota(jnp.int32, q.shape, 1)
        heads = (jnp.where(lane < HEAD_DIM, q, extra_even(lane)).T.astype(BF16),
                 jnp.where(lane >= HEAD_DIM, q, extra_odd(lane)).T.astype(BF16))
        out += [heads[hh][:, c * cw:(c + 1) * cw] for hh in range(2) for c in range(q.shape[0] // cw)]
    return out


def _visibility(first_key, first_query, bk, cw, strict):
    last_visible_gap = -1 if strict else 0
    if first_key + bk - 1 - first_query <= last_visible_gap:
        return "all"
    if first_key - (first_query + cw - 1) > last_visible_gap:
        return "none"
    gap = (lax.broadcasted_iota(jnp.int32, (bk, cw), 0) - lax.broadcasted_iota(jnp.int32, (bk, cw), 1)
           + (first_key - first_query))
    return gap <= last_visible_gap


def _diag_visibility(u, c, bk, cw, strict):
    return _visibility((1 - u) * bk, c * cw, bk, cw, strict)


def _hidden(visibility):
    return isinstance(visibility, str) and visibility == "none"


def _query_norm_bounds(queries, n_chunks):
    bounds = []
    for li, q in enumerate(queries):
        hh = (li // n_chunks) % 2
        own = q[hh * HEAD_DIM:(hh + 1) * HEAD_DIM, :].astype(F32)
        bounds.append(jnp.sqrt(jnp.sum(own * own, axis=0, keepdims=True) * NORM_SLACK))
    return bounds


def _finish_heads(lanes, g_ref, o_ref):
    n_pairs = o_ref.shape[2] // LANES
    n_chunks = len(lanes) // (2 * n_pairs)
    for pp in range(n_pairs):
        mine = lanes[2 * pp * n_chunks:2 * (pp + 1) * n_chunks]
        outs = [jnp.concatenate(mine[hh * n_chunks:(hh + 1) * n_chunks], axis=1) for hh in range(2)]
        normed = [o * lax.rsqrt(jnp.mean(o * o, axis=0, keepdims=True) + EPS) for o in outs]
        cols = slice(pp * LANES, (pp + 1) * LANES)
        o_ref[0, :, cols] = (jnp.concatenate(normed, axis=0).T * g_ref[:, cols]).astype(o_ref.dtype)


def _fox_kernel(q_ref, k_ref, vt_ref, g_ref, stats_ref, o_ref, s_buf, cmax_buf, p_buf, acc_buf):
    bk = vt_ref.shape[3]
    n_lanes, cw = acc_buf.shape[0], acc_buf.shape[2]
    n_chunks = q_ref.shape[1] // cw
    chunks_per_sub = ATT_Q // cw
    assert q_ref.shape[1] == 2 * ATT_Q
    lane_group = lambda li: (li // (2 * n_chunks), (li // n_chunks) % 2, li % n_chunks)
    first_q_block = 2 * pl.program_id(2)
    n_tiles = 2 * (first_q_block + 1)
    ones3 = lambda lo: (lambda lane: jnp.where((lane >= lo) & (lane < lo + 3), 1.0, 0.0))
    queries = _lane_queries(q_ref, ones3(HEAD_DIM), ones3(0), cw)
    acc_buf[...] = jnp.zeros(acc_buf.shape, F32)
    for li in range(n_lanes):
        if _hidden(_diag_visibility(0, lane_group(li)[2] % chunks_per_sub, bk, cw, strict=False)):
            p_buf[0, li] = jnp.zeros((bk, cw), BF16)

    full, idle = ("below", "below", True), (None, None, False)
    ones_rows = jnp.ones((BF16_SUBLANES, bk), BF16)

    def step(t, slot, carry, stages=(full, full)):
        new = []
        for li in range(n_lanes):
            pp, hh, chunk = lane_group(li)
            sub, c = divmod(chunk, chunks_per_sub)
            score, softmax, value = stages[sub]
            see = lambda u: "all" if u == "below" else _diag_visibility(u, c, bk, cw, strict=False)
            if score is not None and not _hidden(see(score)):
                start = pl.multiple_of((n_tiles - 2 - t) * bk, bk)
                s_new = _dot(k_ref[0, 2 * pp + hh, pl.ds(start, bk), :], queries[li])
                if not isinstance(see(score), str):
                    s_new = jnp.where(see(score), s_new, MASKED)
                s_buf[1 - slot, li] = s_new
                cmax_buf[1 - slot, li] = jnp.max(s_new, axis=0, keepdims=True)
            pv = None
            if value:
                vt = vt_ref[0, n_tiles - t, pl.ds(pp * LANES + hh * HEAD_DIM, HEAD_DIM), :]
                pv = _dot(jnp.concatenate([vt, ones_rows], axis=0), p_buf[1 - slot, li])[:acc_buf.shape[1]]
            m = carry[li]
            if softmax is not None and not _hidden(see(softmax)):
                m_new = jnp.maximum(m, cmax_buf[slot, li])
                alpha = jnp.exp2(m - m_new)
                p_buf[slot, li] = jnp.exp2(s_buf[slot, li] - m_new).astype(BF16)
                m = m_new
                acc_buf[li] = alpha * (acc_buf[li] if pv is None else acc_buf[li] + pv)
            elif pv is not None:
                acc_buf[li] += pv
            new.append(m)
        return tuple(new)

    def step_pair(i, carry):
        t = 2 * i + 1
        return step(t + 1, 0, step(t, 1, carry))

    q_norm = _query_norm_bounds(queries, n_chunks)

    def later_tiles_matter(i, carry):
        j_rest = jnp.maximum(n_tiles - 5 - 2 * i, 0)
        worst = None
        for li in range(n_lanes):
            pp, hh, _ = lane_group(li)
            bound = (q_norm[li] * stats_ref[0, pp, j_rest, hh:hh + 1, :]
                     + stats_ref[0, pp, j_rest, 2 + hh:3 + hh, :] - carry[li])
            worst = bound if worst is None else jnp.maximum(worst, bound)
        return jnp.max(worst) >= -PRUNE_LOG2

    def pair_and_check(state):
        i, _, carry = state
        carry = step_pair(i, carry)
        return i + 1, later_tiles_matter(i, carry), carry

    carry = tuple(jnp.full((1, cw), M_INIT, F32) for _ in range(n_lanes))
    carry = step(-3, 1, carry, (idle, (0, None, False)))
    carry = step(-2, 0, carry, (idle, (1, 0, False)))
    carry = step(-1, 1, carry, ((0, None, False), full))
    carry = step(0, 0, carry, ((1, 0, False), full))
    n_pairs, _, carry = lax.while_loop(lambda st: (st[0] < first_q_block) & st[1], pair_and_check,
                                       (jnp.int32(0), jnp.bool_(True), carry))
    drain = (None, "below", True)
    carry = step(2 * n_pairs + 1, 1, carry, (drain, drain))
    drain = (None, None, True)
    carry = step(2 * n_pairs + 2, 0, carry, (drain, drain))
    _finish_heads([acc_buf[li, :HEAD_DIM, :] / acc_buf[li, HEAD_DIM:HEAD_DIM + 1, :] for li in range(n_lanes)], g_ref, o_ref)


def _sb_kernel(q_ref, k_ref, vt_ref, g_ref, o_ref, z_buf, sp_buf, e_buf, wrow_buf, later_buf, acc_buf):
    qi = pl.program_id(2)
    bk = vt_ref.shape[3]
    n_wide, n_first = z_buf.shape[0], SB_WIDE_TILES
    n_lanes, cw = acc_buf.shape[0], acc_buf.shape[2]
    n_chunks = q_ref.shape[1] // cw
    lane_group = lambda li: (li // (2 * n_chunks), (li // n_chunks) % 2, li % n_chunks)
    n_tiles = 2 * (qi + 1)
    zero = lambda lane: 0.0
    queries = _lane_queries(q_ref, zero, zero, cw)
    suffix = (lax.broadcasted_iota(jnp.int32, (bk, bk), 1)
              >= lax.broadcasted_iota(jnp.int32, (bk, bk), 0)).astype(BF16)
    softplus2 = lambda z: jnp.maximum(z, jnp.log2(1.0 + jnp.exp2(jnp.minimum(z, EXP2_MAX))))
    keys = lambda j, pp: k_ref[0, pl.ds(pl.multiple_of(j * bk, bk), bk), pp * LANES:(pp + 1) * LANES]
    values = lambda j, pp, hh: vt_ref[0, j, pl.ds(pp * LANES + hh * HEAD_DIM, HEAD_DIM), :]

    sees = lambda u, li: _diag_visibility(u, lane_group(li)[2], bk, cw, strict=True) if u < 2 else "all"

    def wide_block(tiles, later, check_exists):
        live = [(u, li) for u in tiles for li in range(n_lanes) if not _hidden(sees(u, li))]
        later, col_sums = list(later), {}

        def score(u, li):
            z = _dot(keys(jnp.maximum(n_tiles - 1 - u, 0), lane_group(li)[0]), queries[li])
            if not isinstance(sees(u, li), str):
                z = jnp.where(sees(u, li), z, MASKED)
            if check_exists and u >= 2:
                z = jnp.where(u < n_tiles, z, MASKED)
            z_buf[u, li] = z

        def softplus(u, li):
            sp_buf[u, li] = softplus2(z_buf[u, li]).astype(BF16)

        def cumsum(u, li):
            within = _dot(suffix, sp_buf[u, li])
            col_sums[u, li] = within[0:1, :]
            z_buf[u, li] = z_buf[u, li] - within

        def weight(u, li):
            pp, hh, _ = lane_group(li)
            a = jnp.exp2(z_buf[u, li] - later[li])
            acc_buf[li] += _dot(values(jnp.maximum(n_tiles - 1 - u, 0), pp, hh), a.astype(BF16))
            later[li] = later[li] + col_sums[u, li]

        stages = (score, softplus, cumsum, weight)
        for pos in range(len(live) + len(stages) - 1):
            for lag, stage in enumerate(stages):
                if 0 <= pos - lag < len(live):
                    stage(*live[pos - lag])
        return later

    n_rest = n_tiles - n_wide

    def rest_matters(mass):
        least = mass[0]
        for li in range(1, n_lanes):
            least = jnp.minimum(least, mass[li])
        return jnp.min(least) <= PRUNE_LOG2

    def step(t, slot, later, score=True, softplus=True, cumsum=True, weight=True):
        new_later = []
        for li in range(n_lanes):
            pp, hh, _ = lane_group(li)
            if cumsum:
                within = _dot(suffix, sp_buf[1 - slot, li])
                e_buf[1 - slot, li] = z_buf[1 - slot, li] - within
                wrow_buf[1 - slot, li] = within[0:1, :]
            if score:
                z_buf[1 - slot, li] = _dot(keys(n_rest - 1 - (t + 3), pp), queries[li])
            if weight:
                a = jnp.exp2(e_buf[slot, li] - later[li])
                acc_buf[li] += _dot(values(n_rest - 1 - jnp.maximum(t, 0), pp, hh), a.astype(BF16))
                new_later.append(later[li] + wrow_buf[slot, li])
            else:
                new_later.append(later[li])
            if softplus:
                sp_buf[slot, li] = softplus2(z_buf[slot, li]).astype(BF16)
        return tuple(new_later)

    def pair_and_check(state):
        i, _, later = state
        t = 2 * i - 1
        later = step(t + 1, 0, step(t, 1, later))
        mass = [later[li] + wrow_buf[1, li] for li in range(n_lanes)]
        return i + 1, rest_matters(mass), later

    acc_buf[...] = jnp.zeros(acc_buf.shape, F32)
    first = wide_block(range(n_first), [jnp.zeros((1, cw), F32)] * n_lanes, check_exists=True)
    for li in range(n_lanes):
        later_buf[li] = first[li]
    swept = lambda: [later_buf[li] for li in range(n_lanes)]
    more_matters = (n_rest + (n_wide - n_first) > 0) & rest_matters(first)

    @pl.when(more_matters)
    def _():
        more = wide_block(range(n_first, n_wide), first, check_exists=False)
        for li in range(n_lanes):
            later_buf[li] = more[li]

    @pl.when(more_matters & (n_rest > 0) & rest_matters(swept()))
    def _():
        later = swept()
        e_buf[1] = jnp.full(e_buf.shape[1:], MASKED, F32)
        wrow_buf[1] = jnp.zeros(wrow_buf.shape[1:], F32)
        mass = step(-3, 1, tuple(later), softplus=False, cumsum=False, weight=False)
        mass = step(-2, 0, mass, cumsum=False, weight=False)
        n_pairs, _, mass = lax.while_loop(lambda st: (2 * st[0] + 2 < n_rest) & st[1], pair_and_check,
                                          (jnp.int32(0), jnp.bool_(True), mass))
        mass = step(2 * n_pairs - 1, 1, mass, score=False)
        mass = step(2 * n_pairs, 0, mass, score=False, softplus=False)
        step(2 * n_pairs + 1, 1, mass, score=False, softplus=False, cumsum=False)

    _finish_heads([acc_buf[li] for li in range(n_lanes)], g_ref, o_ref)


def _attention(body, name, pairs, bq, scratch, qk, k_arr, k_spec, vt, g, q_block0, vt_block0, n_heads, stats=None):
    bsz, s, _ = qk.shape
    bk, width = ATT_K, pairs * LANES
    d_grp = n_heads * HEAD_DIM
    assert (n_heads // 2) % pairs == 0
    in_specs = [pl.BlockSpec((1, bq, width), lambda b, p, i: (b, i, q_block0 + p)),
                k_spec,
                pl.BlockSpec((1, s // bk, width, bk), lambda b, p, i: (b, 0, vt_block0 + p, 0)),
                pl.BlockSpec((1, width), lambda b, p, i: (0, p))]
    operands = [qk, k_arr, vt, g.reshape(1, d_grp)]
    if stats is not None:
        in_specs.append(pl.BlockSpec((1, pairs) + stats.shape[2:], lambda b, p, i: (b, p, 0, 0, 0)))
        operands.append(stats)
    return pl.pallas_call(
        body,
        grid=(bsz, n_heads // 2 // pairs, s // bq),
        in_specs=in_specs,
        out_specs=pl.BlockSpec((1, bq, width), lambda b, p, i: (b, i, p)),
        out_shape=jax.ShapeDtypeStruct((bsz, s, d_grp), BF16),
        scratch_shapes=scratch,
        compiler_params=_params("arbitrary", "arbitrary", "arbitrary"),
        name=name,
    )(*operands)


def _lane_groups(pairs, bq):
    return pairs * 2 * (bq // ATT_COLS)


def _fox_scratch(pairs):
    n = _lane_groups(pairs, FOX_Q)
    return [pltpu.VMEM((2, n, ATT_K, ATT_COLS), F32), pltpu.VMEM((2, n, 1, ATT_COLS), F32),
            pltpu.VMEM((2, n, ATT_K, ATT_COLS), BF16), pltpu.VMEM((n, HEAD_DIM + F32_SUBLANES, ATT_COLS), F32)]


def _sb_scratch(pairs):
    n = _lane_groups(pairs, ATT_Q)
    wide = SB_WIDE_TILES + SB_MORE_TILES
    return [pltpu.VMEM((wide, n, ATT_K, ATT_COLS), F32), pltpu.VMEM((wide, n, ATT_K, ATT_COLS), BF16),
            pltpu.VMEM((2, n, ATT_K, ATT_COLS), F32), pltpu.VMEM((2, n, 1, ATT_COLS), F32),
            pltpu.VMEM((n, 1, ATT_COLS), F32), pltpu.VMEM((n, HEAD_DIM, ATT_COLS), F32)]


def _mixer_kernel(x_ref, xh_ref, mf_ref, mfh_ref, ms_ref, msh_ref, mod_ref, wo_ref, gm_ref, wu_ref, cw_ref,
                  cb_ref, wd_ref, gf_ref, o_ref, u_buf, acc_ref, x1_buf, *, final_norm):
    i = pl.program_id(1)
    tm = x_ref.shape[1]
    n_chunks = wd_ref.shape[0]
    mix = jnp.concatenate([jnp.concatenate([mfh_ref[0], msh_ref[0]], axis=-1),
                           jnp.concatenate([mf_ref[0], ms_ref[0]], axis=-1)], axis=0)
    x_ext = jnp.concatenate([xh_ref[0], x_ref[0]], axis=0)
    x1_ext = x_ext + mod_ref[0, 2:3, :] * _dot(mix, wo_ref[...])
    shift = mod_ref[0, 3:4, :]
    scale = mod_ref[0, 4:5, :]
    h_ext = _rms_rows(x1_ext) * gm_ref[...] * (1.0 + scale) + shift
    row = lax.broadcasted_iota(jnp.int32, h_ext.shape, 0)
    hx = jnp.where((row >= BF16_SUBLANES) | (i > 0), h_ext, 0.0).astype(BF16)
    x1_buf[...] = x1_ext[BF16_SUBLANES:, :]
    acc_ref[...] = jnp.zeros_like(acc_ref)

    def project_up(c, slot):
        for br in range(2):
            u_buf[slot, br] = _dot(hx, wu_ref[br, c])

    def mix_down(c, slot):
        branches = []
        for br in range(2):
            out = cb_ref[br, c]
            for tap in range(CONV_WIDTH):
                first = BF16_SUBLANES - (CONV_WIDTH - 1 - tap)
                out = out + cw_ref[br, c, tap:tap + 1, :] * u_buf[slot, br, pl.ds(first, tm), :]
            branches.append(out)
        u_gate, u_val = branches
        acc_ref[...] += _dot((u_gate * jax.nn.sigmoid(u_gate) * u_val).astype(BF16), wd_ref[c])

    def chunk_pair(j, _):
        c = 2 * j
        project_up(c + 1, 1)
        mix_down(c, 0)
        project_up(c + 2, 0)
        mix_down(c + 1, 1)
        return 0

    project_up(0, 0)
    lax.fori_loop(0, (n_chunks - 1) // 2, chunk_pair, 0)
    mix_down(n_chunks - 1, 0)
    x2 = x1_buf[...] + mod_ref[0, 5:6, :] * acc_ref[...]
    o_ref[0] = _rms_rows(x2) * gf_ref[...] if final_norm else x2


def _mixer(x, mix_f, mix_s, mod, w_out, g_mlp, w_up, conv_w, conv_b, w_down, g_final, final_norm):
    bsz, s, d = x.shape
    tm = OUT_ROWS
    n_chunks, tf = w_down.shape[0], w_down.shape[1]
    assert n_chunks % 2 == 1
    halo_blocks = tm // BF16_SUBLANES
    row = lambda b, i: (b, i, 0)
    halo = lambda b, i: (b, jnp.maximum(i * halo_blocks - 1, 0), 0)
    tile_and_halo = lambda a: [pl.BlockSpec((1, tm, a.shape[2]), row),
                               pl.BlockSpec((1, BF16_SUBLANES, a.shape[2]), halo)]
    resident = lambda a: pl.BlockSpec(a.shape, lambda b, i: (0,) * a.ndim, pipeline_mode=pl.Buffered(1))
    return pl.pallas_call(
        functools.partial(_mixer_kernel, final_norm=final_norm),
        grid=(bsz, s // tm),
        in_specs=tile_and_halo(x) + tile_and_halo(mix_f) + tile_and_halo(mix_s)
                 + [pl.BlockSpec((1, N_MOD, d), lambda b, i: (b, 0, 0)), resident(w_out),
                    pl.BlockSpec((1, d), lambda b, i: (0, 0)),
                    resident(w_up), resident(conv_w), resident(conv_b), resident(w_down),
                    pl.BlockSpec((1, d), lambda b, i: (0, 0))],
        out_specs=pl.BlockSpec((1, tm, d), row),
        out_shape=jax.ShapeDtypeStruct((bsz, s, d), F32),
        scratch_shapes=[pltpu.VMEM((2, 2, tm + BF16_SUBLANES, tf), F32), pltpu.VMEM((tm, d), F32),
                        pltpu.VMEM((tm, d), F32)],
        compiler_params=_params("arbitrary", "arbitrary"),
        name="mixer",
    )(x, x, mix_f, mix_f, mix_s, mix_s, mod, w_out, g_mlp, w_up, conv_w, conv_b, w_down, g_final)


def _chunk_columns(a, d_ff, n_chunks, tf):
    halves = jnp.stack([a[:, :d_ff], a[:, d_ff:]])
    halves = jnp.pad(halves, ((0, 0), (0, 0), (0, n_chunks * tf - d_ff)))
    return halves.reshape(2, a.shape[0], n_chunks, tf).transpose(0, 2, 1, 3)


def _pad_cols(a, n):
    return jnp.pad(a, ((0, 0), (0, n - a.shape[1])))


def kernel(x, c, w_ada, b_ada, g_attn, w_in, b_fgate, g_out_fox, g_out_sb, w_out,
           g_mlp, w_up, conv_w, conv_b, w_down, g_final):
    depth, d, _ = w_ada.shape
    n_fox = b_fgate.shape[1]
    d_fox = n_fox * HEAD_DIM
    d_sb = g_out_sb.shape[1]
    n_sb = d_sb // HEAD_DIM
    d_ff = w_down.shape[1]
    d_ff_pad = -(-d_ff // FF_CHUNK) * FF_CHUNK
    assert n_fox % 2 == 0 and n_sb == n_fox and 3 * n_fox <= LANES
    assert x.shape[1] % OUT_ROWS == 0 and x.shape[1] % FOX_Q == 0 and ATT_Q == 2 * ATT_K
    o_kf, o_vf, o_qs, o_ks, o_vs, o_gate = (d_fox, 2 * d_fox, 3 * d_fox, 3 * d_fox + d_sb,
                                             3 * d_fox + 2 * d_sb, 3 * d_fox + 3 * d_sb)

    for l in range(depth):
        mod = _ada(c, w_ada[l], b_ada[l]).reshape(-1, N_MOD, d)
        order = jnp.argsort(b_fgate[l])
        pick_head = (jnp.arange(n_fox)[:, None] == order[None, :]).astype(BF16)
        pick = jnp.kron(pick_head, jnp.eye(HEAD_DIM, dtype=BF16))
        w, wo = w_in[l].astype(BF16), w_out[l].astype(BF16)
        w_nat = jnp.concatenate([jnp.dot(w[:, :o_kf], pick), jnp.dot(w[:, o_kf:o_vf], pick), w[:, o_qs:o_vs]], axis=1)
        w_vt = jnp.concatenate([jnp.dot(pick.T, w[:, o_vf:o_qs].T), w[:, o_vs:o_gate].T], axis=0)
        w_gate = _pad_cols(jnp.dot(w[:, o_gate:], pick_head), LANES)
        b_gate = _pad_cols(b_fgate[l][order].reshape(1, n_fox), LANES)
        g_fox = g_out_fox[l].reshape(n_fox, HEAD_DIM)[order].reshape(-1)
        w_mix = jnp.concatenate([jnp.dot(pick.T, wo[:d_fox]), wo[d_fox:]], axis=0)
        qk, vt, k_aug, stats = _inproj(x, mod, g_attn[l].reshape(1, d), w_nat, w_vt, w_gate, b_gate, n_fox)

        steps_f, steps_s = n_fox // 2 // FOX_PAIRS, n_sb // 2 // SB_PAIRS
        fox_k_spec = pl.BlockSpec((1, 2 * FOX_PAIRS, x.shape[1], LANES), lambda b, p, i: (b, p, 0, 0))
        mix_f = _attention(_fox_kernel, "fox", FOX_PAIRS, FOX_Q, _fox_scratch(FOX_PAIRS), qk, k_aug, fox_k_spec, vt,
                           g_fox, q_block0=0, vt_block0=0, n_heads=n_fox, stats=stats)
        sb_k_spec = pl.BlockSpec((1, x.shape[1], SB_PAIRS * LANES), lambda b, p, i: (b, 0, 3 * steps_s + p))
        mix_s = _attention(_sb_kernel, "sb", SB_PAIRS, ATT_Q, _sb_scratch(SB_PAIRS), qk, qk, sb_k_spec, vt,
                           g_out_sb[l], q_block0=2 * steps_s, vt_block0=steps_s, n_heads=n_sb)

        n_ff = d_ff_pad // FF_CHUNK
        x = _mixer(x, mix_f, mix_s, mod, w_mix, g_mlp[l].reshape(1, d),
                   _chunk_columns(w_up[l], d_ff, n_ff, FF_CHUNK).astype(BF16),
                   _chunk_columns(conv_w[l], d_ff, n_ff, FF_CHUNK),
                   _chunk_columns(conv_b[l].reshape(1, -1), d_ff, n_ff, FF_CHUNK),
                   jnp.pad(w_down[l], ((0, d_ff_pad - d_ff), (0, 0))).astype(BF16).reshape(n_ff, FF_CHUNK, d),
                   g_final.reshape(1, d), final_norm=(l == depth - 1))
    return x
```

```python
import functools

import numpy as np
import jax
import jax.numpy as jnp
from jax import lax
from jax.experimental import pallas as pl
from jax.experimental.pallas import tpu as pltpu

HEAD_DIM = 64
N_MOD = 6
CONV_WIDTH = 3
EPS = 1e-6

LANES = 128
F32_SUBLANES = 8
BF16_SUBLANES = 16
VMEM_LIMIT_BYTES = 48 * 1024 * 1024

ATT_Q = 512
FOX_Q = 2 * ATT_Q
ATT_K = 256
ATT_COLS = 256
FOX_PAIRS = 1
SB_PAIRS = 2
SB_WIDE_TILES = 4
SB_MORE_TILES = 2
PROJ_ROWS = 2 * ATT_K
LOG2E = 1.4426950408889634
MASKED = -1e30
M_INIT = -1e29
EXP2_MAX = 126.0
PRUNE_LOG2 = 152.0
NORM_SLACK = 1.02
STATS_ROWS = 8
OUT_ROWS = 512
FF_CHUNK = 256

F32 = jnp.float32
BF16 = jnp.bfloat16
NT_DIMS = (((1,), (1,)), ((), ()))


def _dot(a, b):
    return jnp.dot(a, b, preferred_element_type=F32)


def _dot_nt(a, b):
    return lax.dot_general(a, b, NT_DIMS, preferred_element_type=F32)


def _params(*sem):
    return pltpu.CompilerParams(dimension_semantics=sem, vmem_limit_bytes=VMEM_LIMIT_BYTES)


def _rms_rows(x):
    return x * lax.rsqrt(jnp.mean(x * x, axis=-1, keepdims=True) + EPS)


def _softplus(z):
    return jnp.maximum(z, 0.0) + jnp.log(1.0 + jnp.exp(-jnp.abs(z)))


def _split3(x):
    hi = x.astype(BF16)
    r1 = x - hi.astype(F32)
    mid = r1.astype(BF16)
    lo = (r1 - mid.astype(F32)).astype(BF16)
    return hi, mid, lo


def _ada_kernel(c_ref, w_ref, b_ref, o_ref):
    c = c_ref[...]
    o_ref[...] = _dot(c * jax.nn.sigmoid(c), w_ref[...]) + b_ref[...]


def _ada(c, w, b):
    bsz, d = c.shape
    n = w.shape[1]
    return pl.pallas_call(
        _ada_kernel,
        grid=(n // d,),
        in_specs=[pl.BlockSpec((bsz, d), lambda j: (0, 0)),
                  pl.BlockSpec((d, d), lambda j: (0, j)),
                  pl.BlockSpec((1, d), lambda j: (0, j))],
        out_specs=pl.BlockSpec((bsz, d), lambda j: (0, j)),
        out_shape=jax.ShapeDtypeStruct((bsz, n), F32),
        compiler_params=_params("arbitrary"),
        name="ada",
    )(c, w, b.reshape(1, n))


def _decay_prefix(lf, carry, n_heads):
    tk = lf.shape[0]
    lane = lax.broadcasted_iota(jnp.int32, (tk, LANES), 1)
    lf = jnp.where(lane < n_heads, lf, 0.0)
    row = lax.broadcasted_iota(jnp.int32, (tk, tk), 0)
    col = lax.broadcasted_iota(jnp.int32, (tk, tk), 1)
    tri = (col <= row).astype(BF16)
    hi, mid, lo = _split3(lf)
    terms = (hi.astype(F32) + pltpu.roll(mid.astype(F32), LANES // 4, 1)
             + pltpu.roll(lo.astype(F32), LANES // 2, 1)).astype(BF16)
    part = _dot(tri, terms)
    total = part + pltpu.roll(part, LANES - LANES // 4, 1) + pltpu.roll(part, LANES // 2, 1)
    return jnp.where(lane < n_heads, carry + total, 0.0)


def _decay_keys(f_run, k_fox, sel_ref, kaug_ref, rows, n_heads):
    tk = f_run.shape[0]
    lane = lax.broadcasted_iota(jnp.int32, (tk, LANES), 1)
    ghi, gmid, glo = _split3(-LOG2E * f_run)
    packed = (ghi.astype(F32) + pltpu.roll(gmid.astype(F32), n_heads, 1)
              + pltpu.roll(glo.astype(F32), 2 * n_heads, 1)).astype(BF16)
    placed = _dot(packed, sel_ref[...]).astype(BF16)
    for h in range(n_heads):
        pair = slice((h // 2) * LANES, (h // 2 + 1) * LANES)
        own = (lane < HEAD_DIM) if h % 2 == 0 else (lane >= HEAD_DIM)
        kaug_ref[0, h, rows, :] = jnp.where(own, k_fox[:, pair], placed[:, pair])


def _decay_stats(f_run, k_fox, ind_ref, kpre, stats_ref, tile, n_heads):
    tk = f_run.shape[0]
    k32 = k_fox.astype(F32)
    sq = _dot((k32 * k32).astype(BF16), ind_ref[...])
    kpre = jnp.maximum(kpre, jnp.sqrt(jnp.max(sq, axis=0, keepdims=True) * NORM_SLACK))
    g_end = -LOG2E * f_run[tk - 1:tk, :]
    lane1 = lax.broadcasted_iota(jnp.int32, (1, LANES), 1)

    def spread(v, h):
        return jnp.broadcast_to(jnp.sum(jnp.where(lane1 == h, v, 0.0), axis=1, keepdims=True), (1, ATT_COLS))

    for p in range(n_heads // 2):
        srows = [spread(v, 2 * p + hh) for v in (kpre, g_end) for hh in range(2)]
        srows += [jnp.zeros((1, ATT_COLS), F32)] * (stats_ref.shape[3] - len(srows))
        stats_ref[0, p, tile] = jnp.concatenate(srows, axis=0)
    return kpre


def _inproj_kernel(x_ref, mod_ref, g_ref, wn_ref, wvt_ref, wg_ref, bg_ref, sel_ref, ind_ref,
                   k_ref, vt_ref, kaug_ref, stats_ref, carry_ref, kpre_ref, *, n_heads):
    @pl.when(pl.program_id(1) == 0)
    def _():
        carry_ref[...] = jnp.zeros_like(carry_ref)
        kpre_ref[...] = jnp.zeros_like(kpre_ref)

    shift = mod_ref[0, 0:1, :]
    scale = mod_ref[0, 1:2, :]
    h = (_rms_rows(x_ref[0]) * g_ref[...] * (1.0 + scale) + shift).astype(BF16)
    logit = _dot(h, wg_ref[...]) + bg_ref[...]
    log_f = -_softplus(-logit)
    k = _dot(h, wn_ref[...]).astype(BF16)
    k_ref[0] = k
    d_grp = n_heads * HEAD_DIM
    tk = vt_ref.shape[3]
    tiles = [slice(t * tk, (t + 1) * tk) for t in range(vt_ref.shape[1])]
    f_run, carry = [], carry_ref[...]
    for rows in tiles:
        f_run.append(_decay_prefix(log_f[rows, :], carry, n_heads))
        carry = f_run[-1][tk - 1:tk, :]
    carry_ref[...] = carry
    v_t = lambda rows: _dot_nt(wvt_ref[...], h[rows, :]).astype(BF16)
    vt_ref[0, 0] = v_t(tiles[0])
    kpre = kpre_ref[...]
    for t, rows in enumerate(tiles):
        _decay_keys(f_run[t], k[rows, :d_grp], sel_ref, kaug_ref, rows, n_heads)
    for t, rows in enumerate(tiles):
        kpre = _decay_stats(f_run[t], k[rows, :d_grp], ind_ref, kpre, stats_ref, t, n_heads)
    kpre_ref[...] = kpre
    for t in range(1, len(tiles)):
        vt_ref[0, t] = v_t(tiles[t])


def _head_indicator(n_heads):
    ind = np.zeros((n_heads * HEAD_DIM, LANES), np.float32)
    ind[np.arange(n_heads * HEAD_DIM), np.arange(n_heads * HEAD_DIM) // HEAD_DIM] = 1.0
    return jnp.asarray(ind, BF16)


def _decay_select_matrix(n_heads):
    sel = np.zeros((LANES, n_heads // 2 * LANES), np.float32)
    for h in range(n_heads):
        base = h // 2 * LANES + (HEAD_DIM if h % 2 == 0 else 0)
        for term in range(3):
            sel[term * n_heads + h, base + term] = 1.0
    return jnp.asarray(sel, BF16)


def _inproj(x, mod, g, w_nat, w_vt, w_gate, b_gate, n_heads):
    bsz, s, d = x.shape
    tm, tk = PROJ_ROWS, ATT_K
    n_nat, n_v = w_nat.shape[1], w_vt.shape[0]
    const = lambda b, i: (0, 0)
    resident = lambda a: pl.BlockSpec(a.shape, const, pipeline_mode=pl.Buffered(1))
    sel, ind = _decay_select_matrix(n_heads), _head_indicator(n_heads)
    return pl.pallas_call(
        functools.partial(_inproj_kernel, n_heads=n_heads),
        grid=(bsz, s // tm),
        in_specs=[pl.BlockSpec((1, tm, d), lambda b, i: (b, i, 0)),
                  pl.BlockSpec((1, N_MOD, d), lambda b, i: (b, 0, 0)),
                  pl.BlockSpec((1, d), const),
                  resident(w_nat), resident(w_vt), resident(w_gate),
                  pl.BlockSpec((1, LANES), const),
                  resident(sel), resident(ind)],
        out_specs=[pl.BlockSpec((1, tm, n_nat), lambda b, i: (b, i, 0)),
                   pl.BlockSpec((1, tm // tk, n_v, tk), lambda b, i: (b, i, 0, 0)),
                   pl.BlockSpec((1, n_heads, tm, LANES), lambda b, i: (b, 0, i, 0)),
                   pl.BlockSpec((1, n_heads // 2, tm // tk, STATS_ROWS, ATT_COLS), lambda b, i: (b, 0, i, 0, 0))],
        out_shape=[jax.ShapeDtypeStruct((bsz, s, n_nat), BF16),
                   jax.ShapeDtypeStruct((bsz, s // tk, n_v, tk), BF16),
                   jax.ShapeDtypeStruct((bsz, n_heads, s, LANES), BF16),
                   jax.ShapeDtypeStruct((bsz, n_heads // 2, s // tk, STATS_ROWS, ATT_COLS), F32)],
        scratch_shapes=[pltpu.VMEM((1, LANES), F32), pltpu.VMEM((1, LANES), F32)],
        compiler_params=_params("arbitrary", "arbitrary"),
        name="inproj",
    )(x, mod, g, w_nat, w_vt, w_gate, b_gate, sel, ind)


def _lane_queries(q_ref, extra_even, extra_odd):
    out = []
    channel = lax.broadcasted_iota(jnp.int32, (LANES, q_ref.shape[3]), 0)
    for pp in range(q_ref.shape[2] // LANES):
        for hh, extra in enumerate((extra_even, extra_odd)):
            own = (channel < HEAD_DIM) if hh == 0 else (channel >= HEAD_DIM)
            for c in range(q_ref.shape[1]):
                q = q_ref[0, c, pp * LANES:(pp + 1) * LANES, :].astype(F32) * (HEAD_DIM ** -0.5 * LOG2E)
                out.append(jnp.where(own, q, extra(channel)).astype(BF16))
    return out


def _visibility(first_key, first_query, bk, cw, strict):
    last_visible_gap = -1 if strict else 0
    if first_key + bk - 1 - first_query <= last_visible_gap:
        return "all"
    if first_key - (first_query + cw - 1) > last_visible_gap:
        return "none"
    gap = (lax.broadcasted_iota(jnp.int32, (bk, cw), 0) - lax.broadcasted_iota(jnp.int32, (bk, cw), 1)
           + (first_key - first_query))
    return gap <= last_visible_gap


def _diag_visibility(u, c, bk, cw, strict):
    return _visibility((1 - u) * bk, c * cw, bk, cw, strict)


def _hidden(visibility):
    return isinstance(visibility, str) and visibility == "none"


def _query_norm_bounds(queries, n_chunks):
    bounds = []
    for li, q in enumerate(queries):
        hh = (li // n_chunks) % 2
        own = q[hh * HEAD_DIM:(hh + 1) * HEAD_DIM, :].astype(F32)
        bounds.append(jnp.sqrt(jnp.sum(own * own, axis=0, keepdims=True) * NORM_SLACK))
    return bounds


def _finish_heads(lanes, g_ref, o_ref):
    n_pairs = o_ref.shape[2] // LANES
    n_chunks = len(lanes) // (2 * n_pairs)
    for pp in range(n_pairs):
        mine = lanes[2 * pp * n_chunks:2 * (pp + 1) * n_chunks]
        outs = [jnp.concatenate(mine[hh * n_chunks:(hh + 1) * n_chunks], axis=1) for hh in range(2)]
        normed = [o * lax.rsqrt(jnp.mean(o * o, axis=0, keepdims=True) + EPS) for o in outs]
        cols = slice(pp * LANES, (pp + 1) * LANES)
        o_ref[0, :, cols] = (jnp.concatenate(normed, axis=0).T * g_ref[:, cols]).astype(o_ref.dtype)


def _fox_kernel(q_ref, k_ref, vt_ref, g_ref, stats_ref, o_ref, s_buf, cmax_buf, p_buf, acc_buf):
    bk = vt_ref.shape[3]
    n_lanes, cw = acc_buf.shape[0], acc_buf.shape[2]
    n_chunks = q_ref.shape[1]
    chunks_per_sub = ATT_Q // cw
    assert n_chunks == 2 * chunks_per_sub and q_ref.shape[3] == cw
    lane_group = lambda li: (li // (2 * n_chunks), (li // n_chunks) % 2, li % n_chunks)
    first_q_block = 2 * pl.program_id(2)
    n_tiles = 2 * (first_q_block + 1)
    ones3 = lambda lo: (lambda lane: jnp.where((lane >= lo) & (lane < lo + 3), 1.0, 0.0))
    queries = _lane_queries(q_ref, ones3(HEAD_DIM), ones3(0))
    acc_buf[...] = jnp.zeros(acc_buf.shape, F32)
    for li in range(n_lanes):
        if _hidden(_diag_visibility(0, lane_group(li)[2] % chunks_per_sub, bk, cw, strict=False)):
            p_buf[0, li] = jnp.zeros((bk, cw), BF16)

    full, idle = ("below", "below", True), (None, None, False)
    ones_rows = jnp.ones((BF16_SUBLANES, bk), BF16)

    def step(t, slot, carry, stages=(full, full)):
        new = []
        for li in range(n_lanes):
            pp, hh, chunk = lane_group(li)
            sub, c = divmod(chunk, chunks_per_sub)
            score, softmax, value = stages[sub]
            see = lambda u: "all" if u == "below" else _diag_visibility(u, c, bk, cw, strict=False)
            if score is not None and not _hidden(see(score)):
                start = pl.multiple_of((n_tiles - 2 - t) * bk, bk)
                s_new = _dot(k_ref[0, 2 * pp + hh, pl.ds(start, bk), :], queries[li])
                if not isinstance(see(score), str):
                    s_new = jnp.where(see(score), s_new, MASKED)
                s_buf[1 - slot, li] = s_new
                cmax_buf[1 - slot, li] = jnp.max(s_new, axis=0, keepdims=True)
            pv = None
            if value:
                vt = vt_ref[0, n_tiles - t, pl.ds(pp * LANES + hh * HEAD_DIM, HEAD_DIM), :]
                pv = _dot(jnp.concatenate([vt, ones_rows], axis=0), p_buf[1 - slot, li])[:acc_buf.shape[1]]
            m = carry[li]
            if softmax is not None and not _hidden(see(softmax)):
                m_new = jnp.maximum(m, cmax_buf[slot, li])
                alpha = jnp.exp2(m - m_new)
                p_buf[slot, li] = jnp.exp2(s_buf[slot, li] - m_new).astype(BF16)
                m = m_new
                acc_buf[li] = alpha * (acc_buf[li] if pv is None else acc_buf[li] + pv)
            elif pv is not None:
                acc_buf[li] += pv
            new.append(m)
        return tuple(new)

    def step_pair(i, carry):
        t = 2 * i + 1
        return step(t + 1, 0, step(t, 1, carry))

    q_norm = _query_norm_bounds(queries, n_chunks)

    def later_tiles_matter(i, carry):
        j_rest = jnp.maximum(n_tiles - 5 - 2 * i, 0)
        worst = None
        for li in range(n_lanes):
            pp, hh, _ = lane_group(li)
            bound = (q_norm[li] * stats_ref[0, pp, j_rest, hh:hh + 1, :]
                     + stats_ref[0, pp, j_rest, 2 + hh:3 + hh, :] - carry[li])
            worst = bound if worst is None else jnp.maximum(worst, bound)
        return jnp.max(worst) >= -PRUNE_LOG2

    def pair_and_check(state):
        i, _, carry = state
        carry = step_pair(i, carry)
        return i + 1, later_tiles_matter(i, carry), carry

    carry = tuple(jnp.full((1, cw), M_INIT, F32) for _ in range(n_lanes))
    carry = step(-3, 1, carry, (idle, (0, None, False)))
    carry = step(-2, 0, carry, (idle, (1, 0, False)))
    carry = step(-1, 1, carry, ((0, None, False), full))
    carry = step(0, 0, carry, ((1, 0, False), full))
    n_pairs, _, carry = lax.while_loop(lambda st: (st[0] < first_q_block) & st[1], pair_and_check,
                                       (jnp.int32(0), jnp.bool_(True), carry))
    drain = (None, "below", True)
    carry = step(2 * n_pairs + 1, 1, carry, (drain, drain))
    drain = (None, None, True)
    carry = step(2 * n_pairs + 2, 0, carry, (drain, drain))
    _finish_heads([acc_buf[li, :HEAD_DIM, :] / acc_buf[li, HEAD_DIM:HEAD_DIM + 1, :] for li in range(n_lanes)], g_ref, o_ref)


def _sb_kernel(q_ref, k_ref, vt_ref, g_ref, o_ref, z_buf, sp_buf, e_buf, wrow_buf, later_buf, acc_buf):
    qi = pl.program_id(2)
    bk = vt_ref.shape[3]
    n_wide, n_first = z_buf.shape[0], SB_WIDE_TILES
    n_lanes, cw = acc_buf.shape[0], acc_buf.shape[2]
    n_chunks = q_ref.shape[1]
    lane_group = lambda li: (li // (2 * n_chunks), (li // n_chunks) % 2, li % n_chunks)
    n_tiles = 2 * (qi + 1)
    zero = lambda channel: 0.0
    queries = _lane_queries(q_ref, zero, zero)
    suffix = (lax.broadcasted_iota(jnp.int32, (bk, bk), 1)
              >= lax.broadcasted_iota(jnp.int32, (bk, bk), 0)).astype(BF16)
    softplus2 = lambda z: jnp.maximum(z, jnp.log2(1.0 + jnp.exp2(jnp.minimum(z, EXP2_MAX))))
    keys = lambda j, pp: k_ref[0, pl.ds(pl.multiple_of(j * bk, bk), bk), pp * LANES:(pp + 1) * LANES]
    values = lambda j, pp, hh: vt_ref[0, j, pl.ds(pp * LANES + hh * HEAD_DIM, HEAD_DIM), :]

    sees = lambda u, li: _diag_visibility(u, lane_group(li)[2], bk, cw, strict=True) if u < 2 else "all"

    def wide_block(tiles, later, check_exists):
        live = [(u, li) for u in tiles for li in range(n_lanes) if not _hidden(sees(u, li))]
        later, col_sums = list(later), {}

        def score(u, li):
            z = _dot(keys(jnp.maximum(n_tiles - 1 - u, 0), lane_group(li)[0]), queries[li])
            if not isinstance(sees(u, li), str):
                z = jnp.where(sees(u, li), z, MASKED)
            if check_exists and u >= 2:
                z = jnp.where(u < n_tiles, z, MASKED)
            z_buf[u, li] = z

        def softplus(u, li):
            sp_buf[u, li] = softplus2(z_buf[u, li]).astype(BF16)

        def cumsum(u, li):
            within = _dot(suffix, sp_buf[u, li])
            col_sums[u, li] = within[0:1, :]
            z_buf[u, li] = z_buf[u, li] - within

        def weight(u, li):
            pp, hh, _ = lane_group(li)
            a = jnp.exp2(z_buf[u, li] - later[li])
            acc_buf[li] += _dot(values(jnp.maximum(n_tiles - 1 - u, 0), pp, hh), a.astype(BF16))
            later[li] = later[li] + col_sums[u, li]

        stages = (score, softplus, cumsum, weight)
        for pos in range(len(live) + len(stages) - 1):
            for lag, stage in enumerate(stages):
                if 0 <= pos - lag < len(live):
                    stage(*live[pos - lag])
        return later

    n_rest = n_tiles - n_wide

    def rest_matters(mass):
        least = mass[0]
        for li in range(1, n_lanes):
            least = jnp.minimum(least, mass[li])
        return jnp.min(least) <= PRUNE_LOG2

    def step(t, slot, later, score=True, softplus=True, cumsum=True, weight=True):
        new_later = []
        for li in range(n_lanes):
            pp, hh, _ = lane_group(li)
            if cumsum:
                within = _dot(suffix, sp_buf[1 - slot, li])
                e_buf[1 - slot, li] = z_buf[1 - slot, li] - within
                wrow_buf[1 - slot, li] = within[0:1, :]
            if score:
                z_buf[1 - slot, li] = _dot(keys(n_rest - 1 - (t + 3), pp), queries[li])
            if weight:
                a = jnp.exp2(e_buf[slot, li] - later[li])
                acc_buf[li] += _dot(values(n_rest - 1 - jnp.maximum(t, 0), pp, hh), a.astype(BF16))
                new_later.append(later[li] + wrow_buf[slot, li])
            else:
                new_later.append(later[li])
            if softplus:
                sp_buf[slot, li] = softplus2(z_buf[slot, li]).astype(BF16)
        return tuple(new_later)

    def pair_and_check(state):
        i, _, later = state
        t = 2 * i - 1
        later = step(t + 1, 0, step(t, 1, later))
        mass = [later[li] + wrow_buf[1, li] for li in range(n_lanes)]
        return i + 1, rest_matters(mass), later

    acc_buf[...] = jnp.zeros(acc_buf.shape, F32)
    first = wide_block(range(n_first), [jnp.zeros((1, cw), F32)] * n_lanes, check_exists=True)
    for li in range(n_lanes):
        later_buf[li] = first[li]
    swept = lambda: [later_buf[li] for li in range(n_lanes)]
    more_matters = (n_rest + (n_wide - n_first) > 0) & rest_matters(first)

    @pl.when(more_matters)
    def _():
        more = wide_block(range(n_first, n_wide), first, check_exists=False)
        for li in range(n_lanes):
            later_buf[li] = more[li]

    @pl.when(more_matters & (n_rest > 0) & rest_matters(swept()))
    def _():
        later = swept()
        e_buf[1] = jnp.full(e_buf.shape[1:], MASKED, F32)
        wrow_buf[1] = jnp.zeros(wrow_buf.shape[1:], F32)
        mass = step(-3, 1, tuple(later), softplus=False, cumsum=False, weight=False)
        mass = step(-2, 0, mass, cumsum=False, weight=False)
        n_pairs, _, mass = lax.while_loop(lambda st: (2 * st[0] + 2 < n_rest) & st[1], pair_and_check,
                                          (jnp.int32(0), jnp.bool_(True), mass))
        mass = step(2 * n_pairs - 1, 1, mass, score=False)
        mass = step(2 * n_pairs, 0, mass, score=False, softplus=False)
        step(2 * n_pairs + 1, 1, mass, score=False, softplus=False, cumsum=False)

    _finish_heads([acc_buf[li] for li in range(n_lanes)], g_ref, o_ref)


def _attention(body, name, pairs, bq, scratch, k_arr, k_spec, vqt, g, q_block0, vt_block0, n_heads, stats=None):
    bsz, s = vqt.shape[0], vqt.shape[1] * vqt.shape[3]
    bk, width = ATT_K, pairs * LANES
    d_grp = n_heads * HEAD_DIM
    assert (n_heads // 2) % pairs == 0
    in_specs = [pl.BlockSpec((1, bq // ATT_COLS, width, ATT_COLS), lambda b, p, i: (b, i, q_block0 + p, 0)),
                k_spec,
                pl.BlockSpec((1, s // bk, width, bk), lambda b, p, i: (b, 0, vt_block0 + p, 0)),
                pl.BlockSpec((1, width), lambda b, p, i: (0, p))]
    operands = [vqt, k_arr, vqt, g.reshape(1, d_grp)]
    if stats is not None:
        in_specs.append(pl.BlockSpec((1, pairs) + stats.shape[2:], lambda b, p, i: (b, p, 0, 0, 0)))
        operands.append(stats)
    return pl.pallas_call(
        body,
        grid=(bsz, n_heads // 2 // pairs, s // bq),
        in_specs=in_specs,
        out_specs=pl.BlockSpec((1, bq, width), lambda b, p, i: (b, i, p)),
        out_shape=jax.ShapeDtypeStruct((bsz, s, d_grp), BF16),
        scratch_shapes=scratch,
        compiler_params=_params("arbitrary", "arbitrary", "arbitrary"),
        name=name,
    )(*operands)


def _lane_groups(pairs, bq):
    return pairs * 2 * (bq // ATT_COLS)


def _fox_scratch(pairs):
    n = _lane_groups(pairs, FOX_Q)
    return [pltpu.VMEM((2, n, ATT_K, ATT_COLS), F32), pltpu.VMEM((2, n, 1, ATT_COLS), F32),
            pltpu.VMEM((2, n, ATT_K, ATT_COLS), BF16), pltpu.VMEM((n, HEAD_DIM + F32_SUBLANES, ATT_COLS), F32)]


def _sb_scratch(pairs):
    n = _lane_groups(pairs, ATT_Q)
    wide = SB_WIDE_TILES + SB_MORE_TILES
    return [pltpu.VMEM((wide, n, ATT_K, ATT_COLS), F32), pltpu.VMEM((wide, n, ATT_K, ATT_COLS), BF16),
            pltpu.VMEM((2, n, ATT_K, ATT_COLS), F32), pltpu.VMEM((2, n, 1, ATT_COLS), F32),
            pltpu.VMEM((n, 1, ATT_COLS), F32), pltpu.VMEM((n, HEAD_DIM, ATT_COLS), F32)]


def _mixer_kernel(x_ref, xh_ref, mf_ref, mfh_ref, ms_ref, msh_ref, mod_ref, wo_ref, gm_ref, wu_ref, cw_ref,
                  cb_ref, wd_ref, gf_ref, o_ref, u_buf, acc_ref, x1_buf, *, final_norm):
    i = pl.program_id(1)
    tm = x_ref.shape[1]
    n_chunks = wd_ref.shape[0]
    mix = jnp.concatenate([jnp.concatenate([mfh_ref[0], msh_ref[0]], axis=-1),
                           jnp.concatenate([mf_ref[0], ms_ref[0]], axis=-1)], axis=0)
    x_ext = jnp.concatenate([xh_ref[0], x_ref[0]], axis=0)
    x1_ext = x_ext + mod_ref[0, 2:3, :] * _dot(mix, wo_ref[...])
    shift = mod_ref[0, 3:4, :]
    scale = mod_ref[0, 4:5, :]
    h_ext = _rms_rows(x1_ext) * gm_ref[...] * (1.0 + scale) + shift
    row = lax.broadcasted_iota(jnp.int32, h_ext.shape, 0)
    hx = jnp.where((row >= BF16_SUBLANES) | (i > 0), h_ext, 0.0).astype(BF16)
    x1_buf[...] = x1_ext[BF16_SUBLANES:, :]
    acc_ref[...] = jnp.zeros_like(acc_ref)

    def project_up(c, slot):
        for br in range(2):
            u_buf[slot, br] = _dot(hx, wu_ref[br, c])

    def mix_down(c, slot):
        branches = []
        for br in range(2):
            out = cb_ref[br, c]
            for tap in range(CONV_WIDTH):
                first = BF16_SUBLANES - (CONV_WIDTH - 1 - tap)
                out = out + cw_ref[br, c, tap:tap + 1, :] * u_buf[slot, br, pl.ds(first, tm), :]
            branches.append(out)
        u_gate, u_val = branches
        acc_ref[...] += _dot((u_gate * jax.nn.sigmoid(u_gate) * u_val).astype(BF16), wd_ref[c])

    def chunk_pair(j, _):
        c = 2 * j
        project_up(c + 1, 1)
        mix_down(c, 0)
        project_up(c + 2, 0)
        mix_down(c + 1, 1)
        return 0

    project_up(0, 0)
    lax.fori_loop(0, (n_chunks - 1) // 2, chunk_pair, 0)
    mix_down(n_chunks - 1, 0)
    x2 = x1_buf[...] + mod_ref[0, 5:6, :] * acc_ref[...]
    o_ref[0] = _rms_rows(x2) * gf_ref[...] if final_norm else x2


def _mixer(x, mix_f, mix_s, mod, w_out, g_mlp, w_up, conv_w, conv_b, w_down, g_final, final_norm):
    bsz, s, d = x.shape
    tm = OUT_ROWS
    n_chunks, tf = w_down.shape[0], w_down.shape[1]
    assert n_chunks % 2 == 1
    halo_blocks = tm // BF16_SUBLANES
    row = lambda b, i: (b, i, 0)
    halo = lambda b, i: (b, jnp.maximum(i * halo_blocks - 1, 0), 0)
    tile_and_halo = lambda a: [pl.BlockSpec((1, tm, a.shape[2]), row),
                               pl.BlockSpec((1, BF16_SUBLANES, a.shape[2]), halo)]
    resident = lambda a: pl.BlockSpec(a.shape, lambda b, i: (0,) * a.ndim, pipeline_mode=pl.Buffered(1))
    return pl.pallas_call(
        functools.partial(_mixer_kernel, final_norm=final_norm),
        grid=(bsz, s // tm),
        in_specs=tile_and_halo(x) + tile_and_halo(mix_f) + tile_and_halo(mix_s)
                 + [pl.BlockSpec((1, N_MOD, d), lambda b, i: (b, 0, 0)), resident(w_out),
                    pl.BlockSpec((1, d), lambda b, i: (0, 0)),
                    resident(w_up), resident(conv_w), resident(conv_b), resident(w_down),
                    pl.BlockSpec((1, d), lambda b, i: (0, 0))],
        out_specs=pl.BlockSpec((1, tm, d), row),
        out_shape=jax.ShapeDtypeStruct((bsz, s, d), F32),
        scratch_shapes=[pltpu.VMEM((2, 2, tm + BF16_SUBLANES, tf), F32), pltpu.VMEM((tm, d), F32),
                        pltpu.VMEM((tm, d), F32)],
        compiler_params=_params("arbitrary", "arbitrary"),
        name="mixer",
    )(x, x, mix_f, mix_f, mix_s, mix_s, mod, w_out, g_mlp, w_up, conv_w, conv_b, w_down, g_final)


def _chunk_columns(a, d_ff, n_chunks, tf):
    halves = jnp.stack([a[:, :d_ff], a[:, d_ff:]])
    halves = jnp.pad(halves, ((0, 0), (0, 0), (0, n_chunks * tf - d_ff)))
    return halves.reshape(2, a.shape[0], n_chunks, tf).transpose(0, 2, 1, 3)


def _pad_cols(a, n):
    return jnp.pad(a, ((0, 0), (0, n - a.shape[1])))


def kernel(x, c, w_ada, b_ada, g_attn, w_in, b_fgate, g_out_fox, g_out_sb, w_out,
           g_mlp, w_up, conv_w, conv_b, w_down, g_final):
    depth, d, _ = w_ada.shape
    n_fox = b_fgate.shape[1]
    d_fox = n_fox * HEAD_DIM
    d_sb = g_out_sb.shape[1]
    n_sb = d_sb // HEAD_DIM
    d_ff = w_down.shape[1]
    d_ff_pad = -(-d_ff // FF_CHUNK) * FF_CHUNK
    assert n_fox % 2 == 0 and n_sb == n_fox and 3 * n_fox <= LANES
    assert x.shape[1] % OUT_ROWS == 0 and x.shape[1] % FOX_Q == 0 and ATT_Q == 2 * ATT_K and ATT_COLS == ATT_K
    o_kf, o_vf, o_qs, o_ks, o_vs, o_gate = (d_fox, 2 * d_fox, 3 * d_fox, 3 * d_fox + d_sb,
                                             3 * d_fox + 2 * d_sb, 3 * d_fox + 3 * d_sb)

    for l in range(depth):
        mod = _ada(c, w_ada[l], b_ada[l]).reshape(-1, N_MOD, d)
        order = jnp.argsort(b_fgate[l])
        pick_head = (jnp.arange(n_fox)[:, None] == order[None, :]).astype(BF16)
        pick = jnp.kron(pick_head, jnp.eye(HEAD_DIM, dtype=BF16))
        w, wo = w_in[l].astype(BF16), w_out[l].astype(BF16)
        w_nat = jnp.concatenate([jnp.dot(w[:, o_kf:o_vf], pick), w[:, o_ks:o_vs]], axis=1)
        w_vt = jnp.concatenate([jnp.dot(pick.T, w[:, o_vf:o_qs].T), w[:, o_vs:o_gate].T,
                                jnp.dot(pick.T, w[:, :o_kf].T), w[:, o_qs:o_ks].T], axis=0)
        w_gate = _pad_cols(jnp.dot(w[:, o_gate:], pick_head), LANES)
        b_gate = _pad_cols(b_fgate[l][order].reshape(1, n_fox), LANES)
        g_fox = g_out_fox[l].reshape(n_fox, HEAD_DIM)[order].reshape(-1)
        w_mix = jnp.concatenate([jnp.dot(pick.T, wo[:d_fox]), wo[d_fox:]], axis=0)
        k_nat, vqt, k_aug, stats = _inproj(x, mod, g_attn[l].reshape(1, d), w_nat, w_vt, w_gate, b_gate, n_fox)

        wf, ws = FOX_PAIRS * LANES, SB_PAIRS * LANES
        fox_k_spec = pl.BlockSpec((1, 2 * FOX_PAIRS, x.shape[1], LANES), lambda b, p, i: (b, p, 0, 0))
        mix_f = _attention(_fox_kernel, "fox", FOX_PAIRS, FOX_Q, _fox_scratch(FOX_PAIRS), k_aug, fox_k_spec, vqt,
                           g_fox, q_block0=(d_fox + d_sb) // wf, vt_block0=0, n_heads=n_fox, stats=stats)
        sb_k_spec = pl.BlockSpec((1, x.shape[1], ws), lambda b, p, i: (b, 0, d_fox // ws + p))
        mix_s = _attention(_sb_kernel, "sb", SB_PAIRS, ATT_Q, _sb_scratch(SB_PAIRS), k_nat, sb_k_spec, vqt,
                           g_out_sb[l], q_block0=(2 * d_fox + d_sb) // ws, vt_block0=d_fox // ws, n_heads=n_sb)

        n_ff = d_ff_pad // FF_CHUNK
        x = _mixer(x, mix_f, mix_s, mod, w_mix, g_mlp[l].reshape(1, d),
                   _chunk_columns(w_up[l], d_ff, n_ff, FF_CHUNK).astype(BF16),
                   _chunk_columns(conv_w[l], d_ff, n_ff, FF_CHUNK),
                   _chunk_columns(conv_b[l].reshape(1, -1), d_ff, n_ff, FF_CHUNK),
                   jnp.pad(w_down[l], ((0, d_ff_pad - d_ff), (0, 0))).astype(BF16).reshape(n_ff, FF_CHUNK, d),
                   g_final.reshape(1, d), final_norm=(l == depth - 1))
    return x
```

```python
import functools

import numpy as np
import jax
import jax.numpy as jnp
from jax import lax
from jax.experimental import pallas as pl
from jax.experimental.pallas import tpu as pltpu

HEAD_DIM = 64
N_MOD = 6
CONV_WIDTH = 3
EPS = 1e-6

LANES = 128
F32_SUBLANES = 8
BF16_SUBLANES = 16
VMEM_LIMIT_BYTES = 56 * 1024 * 1024

ATT_Q = 512
FOX_Q = 2 * ATT_Q
ATT_K = 256
ATT_COLS = 256
FOX_PAIRS = 1
SB_PAIRS = 2
SB_WIDE_TILES = 4
SB_MORE_TILES = 2
PROJ_ROWS = 2 * ATT_K
LOG2E = 1.4426950408889634
MASKED = -1e30
M_INIT = -1e29
EXP2_MAX = 126.0
PRUNE_LOG2 = 152.0
NORM_SLACK = 1.02
STATS_ROWS = 8
OUT_ROWS = 1024
FF_CHUNK = 256

F32 = jnp.float32
BF16 = jnp.bfloat16
NT_DIMS = (((1,), (1,)), ((), ()))


def _dot(a, b):
    return jnp.dot(a, b, preferred_element_type=F32)


def _dot_nt(a, b):
    return lax.dot_general(a, b, NT_DIMS, preferred_element_type=F32)


def _params(*sem):
    return pltpu.CompilerParams(dimension_semantics=sem, vmem_limit_bytes=VMEM_LIMIT_BYTES)


def _rms_rows(x):
    return x * lax.rsqrt(jnp.mean(x * x, axis=-1, keepdims=True) + EPS)


def _softplus(z):
    return jnp.maximum(z, 0.0) + jnp.log(1.0 + jnp.exp(-jnp.abs(z)))


def _split3(x):
    hi = x.astype(BF16)
    r1 = x - hi.astype(F32)
    mid = r1.astype(BF16)
    lo = (r1 - mid.astype(F32)).astype(BF16)
    return hi, mid, lo


def _ada_kernel(c_ref, w_ref, b_ref, o_ref):
    c = c_ref[...]
    o_ref[...] = _dot(c * jax.nn.sigmoid(c), w_ref[...]) + b_ref[...]


def _ada(c, w, b):
    bsz, d = c.shape
    n = w.shape[1]
    return pl.pallas_call(
        _ada_kernel,
        grid=(n // d,),
        in_specs=[pl.BlockSpec((bsz, d), lambda j: (0, 0)),
                  pl.BlockSpec((d, d), lambda j: (0, j)),
                  pl.BlockSpec((1, d), lambda j: (0, j))],
        out_specs=pl.BlockSpec((bsz, d), lambda j: (0, j)),
        out_shape=jax.ShapeDtypeStruct((bsz, n), F32),
        compiler_params=_params("arbitrary"),
        name="ada",
    )(c, w, b.reshape(1, n))


def _decay_prefix(lf, carry, n_heads):
    tk = lf.shape[0]
    lane = lax.broadcasted_iota(jnp.int32, (tk, LANES), 1)
    lf = jnp.where(lane < n_heads, lf, 0.0)
    row = lax.broadcasted_iota(jnp.int32, (tk, tk), 0)
    col = lax.broadcasted_iota(jnp.int32, (tk, tk), 1)
    tri = (col <= row).astype(BF16)
    hi, mid, lo = _split3(lf)
    terms = (hi.astype(F32) + pltpu.roll(mid.astype(F32), LANES // 4, 1)
             + pltpu.roll(lo.astype(F32), LANES // 2, 1)).astype(BF16)
    part = _dot(tri, terms)
    total = part + pltpu.roll(part, LANES - LANES // 4, 1) + pltpu.roll(part, LANES // 2, 1)
    return jnp.where(lane < n_heads, carry + total, 0.0)


def _decay_keys(f_run, k_fox, sel_ref, kaug_ref, rows, n_heads):
    tk = f_run.shape[0]
    lane = lax.broadcasted_iota(jnp.int32, (tk, LANES), 1)
    ghi, gmid, glo = _split3(-LOG2E * f_run)
    packed = (ghi.astype(F32) + pltpu.roll(gmid.astype(F32), n_heads, 1)
              + pltpu.roll(glo.astype(F32), 2 * n_heads, 1)).astype(BF16)
    placed = _dot(packed, sel_ref[...]).astype(BF16)
    for h in range(n_heads):
        pair = slice((h // 2) * LANES, (h // 2 + 1) * LANES)
        own = (lane < HEAD_DIM) if h % 2 == 0 else (lane >= HEAD_DIM)
        kaug_ref[0, h, rows, :] = jnp.where(own, k_fox[:, pair], placed[:, pair])


def _decay_stats(f_run, k_fox, ind_ref, kpre, stats_ref, tile, n_heads):
    tk = f_run.shape[0]
    k32 = k_fox.astype(F32)
    sq = _dot((k32 * k32).astype(BF16), ind_ref[...])
    kpre = jnp.maximum(kpre, jnp.sqrt(jnp.max(sq, axis=0, keepdims=True) * NORM_SLACK))
    g_end = -LOG2E * f_run[tk - 1:tk, :]
    lane1 = lax.broadcasted_iota(jnp.int32, (1, LANES), 1)

    def spread(v, h):
        return jnp.broadcast_to(jnp.sum(jnp.where(lane1 == h, v, 0.0), axis=1, keepdims=True), (1, ATT_COLS))

    for p in range(n_heads // 2):
        srows = [spread(v, 2 * p + hh) for v in (kpre, g_end) for hh in range(2)]
        srows += [jnp.zeros((1, ATT_COLS), F32)] * (stats_ref.shape[3] - len(srows))
        stats_ref[0, p, tile] = jnp.concatenate(srows, axis=0)
    return kpre


def _inproj_kernel(x_ref, mod_ref, g_ref, wn_ref, wvt_ref, wg_ref, bg_ref, sel_ref, ind_ref,
                   k_ref, vt_ref, kaug_ref, stats_ref, carry_ref, kpre_ref, *, n_heads):
    @pl.when(pl.program_id(1) == 0)
    def _():
        carry_ref[...] = jnp.zeros_like(carry_ref)
        kpre_ref[...] = jnp.zeros_like(kpre_ref)

    shift = mod_ref[0, 0:1, :]
    scale = mod_ref[0, 1:2, :]
    h = (_rms_rows(x_ref[0]) * g_ref[...] * (1.0 + scale) + shift).astype(BF16)
    logit = _dot(h, wg_ref[...]) + bg_ref[...]
    log_f = -_softplus(-logit)
    k = _dot(h, wn_ref[...]).astype(BF16)
    k_ref[0] = k
    d_grp = n_heads * HEAD_DIM
    tk = vt_ref.shape[3]
    tiles = [slice(t * tk, (t + 1) * tk) for t in range(vt_ref.shape[1])]
    f_run, carry = [], carry_ref[...]
    for rows in tiles:
        f_run.append(_decay_prefix(log_f[rows, :], carry, n_heads))
        carry = f_run[-1][tk - 1:tk, :]
    carry_ref[...] = carry
    v_t = lambda rows: _dot_nt(wvt_ref[...], h[rows, :]).astype(BF16)
    vt_ref[0, 0] = v_t(tiles[0])
    kpre = kpre_ref[...]
    for t, rows in enumerate(tiles):
        _decay_keys(f_run[t], k[rows, :d_grp], sel_ref, kaug_ref, rows, n_heads)
    for t, rows in enumerate(tiles):
        kpre = _decay_stats(f_run[t], k[rows, :d_grp], ind_ref, kpre, stats_ref, t, n_heads)
    kpre_ref[...] = kpre
    for t in range(1, len(tiles)):
        vt_ref[0, t] = v_t(tiles[t])


def _head_indicator(n_heads):
    ind = np.zeros((n_heads * HEAD_DIM, LANES), np.float32)
    ind[np.arange(n_heads * HEAD_DIM), np.arange(n_heads * HEAD_DIM) // HEAD_DIM] = 1.0
    return jnp.asarray(ind, BF16)


def _decay_select_matrix(n_heads):
    sel = np.zeros((LANES, n_heads // 2 * LANES), np.float32)
    for h in range(n_heads):
        base = h // 2 * LANES + (HEAD_DIM if h % 2 == 0 else 0)
        for term in range(3):
            sel[term * n_heads + h, base + term] = 1.0
    return jnp.asarray(sel, BF16)


def _inproj(x, mod, g, w_nat, w_vt, w_gate, b_gate, n_heads):
    bsz, s, d = x.shape
    tm, tk = PROJ_ROWS, ATT_K
    n_nat, n_v = w_nat.shape[1], w_vt.shape[0]
    const = lambda b, i: (0, 0)
    resident = lambda a: pl.BlockSpec(a.shape, const, pipeline_mode=pl.Buffered(1))
    sel, ind = _decay_select_matrix(n_heads), _head_indicator(n_heads)
    return pl.pallas_call(
        functools.partial(_inproj_kernel, n_heads=n_heads),
        grid=(bsz, s // tm),
        in_specs=[pl.BlockSpec((1, tm, d), lambda b, i: (b, i, 0)),
                  pl.BlockSpec((1, N_MOD, d), lambda b, i: (b, 0, 0)),
                  pl.BlockSpec((1, d), const),
                  resident(w_nat), resident(w_vt), resident(w_gate),
                  pl.BlockSpec((1, LANES), const),
                  resident(sel), resident(ind)],
        out_specs=[pl.BlockSpec((1, tm, n_nat), lambda b, i: (b, i, 0)),
                   pl.BlockSpec((1, tm // tk, n_v, tk), lambda b, i: (b, i, 0, 0)),
                   pl.BlockSpec((1, n_heads, tm, LANES), lambda b, i: (b, 0, i, 0)),
                   pl.BlockSpec((1, n_heads // 2, tm // tk, STATS_ROWS, ATT_COLS), lambda b, i: (b, 0, i, 0, 0))],
        out_shape=[jax.ShapeDtypeStruct((bsz, s, n_nat), BF16),
                   jax.ShapeDtypeStruct((bsz, s // tk, n_v, tk), BF16),
                   jax.ShapeDtypeStruct((bsz, n_heads, s, LANES), BF16),
                   jax.ShapeDtypeStruct((bsz, n_heads // 2, s // tk, STATS_ROWS, ATT_COLS), F32)],
        scratch_shapes=[pltpu.VMEM((1, LANES), F32), pltpu.VMEM((1, LANES), F32)],
        compiler_params=_params("arbitrary", "arbitrary"),
        name="inproj",
    )(x, mod, g, w_nat, w_vt, w_gate, b_gate, sel, ind)


def _lane_queries(q_ref, extra_even, extra_odd):
    out = []
    channel = lax.broadcasted_iota(jnp.int32, (LANES, q_ref.shape[3]), 0)
    for pp in range(q_ref.shape[2] // LANES):
        for hh, extra in enumerate((extra_even, extra_odd)):
            own = (channel < HEAD_DIM) if hh == 0 else (channel >= HEAD_DIM)
            for c in range(q_ref.shape[1]):
                q = q_ref[0, c, pp * LANES:(pp + 1) * LANES, :].astype(F32) * (HEAD_DIM ** -0.5 * LOG2E)
                out.append(jnp.where(own, q, extra(channel)).astype(BF16))
    return out


def _visibility(first_key, first_query, bk, cw, strict):
    last_visible_gap = -1 if strict else 0
    if first_key + bk - 1 - first_query <= last_visible_gap:
        return "all"
    if first_key - (first_query + cw - 1) > last_visible_gap:
        return "none"
    gap = (lax.broadcasted_iota(jnp.int32, (bk, cw), 0) - lax.broadcasted_iota(jnp.int32, (bk, cw), 1)
           + (first_key - first_query))
    return gap <= last_visible_gap


def _diag_visibility(u, c, bk, cw, strict):
    return _visibility((1 - u) * bk, c * cw, bk, cw, strict)


def _hidden(visibility):
    return isinstance(visibility, str) and visibility == "none"


def _query_norm_bounds(queries, n_chunks):
    bounds = []
    for li, q in enumerate(queries):
        hh = (li // n_chunks) % 2
        own = q[hh * HEAD_DIM:(hh + 1) * HEAD_DIM, :].astype(F32)
        bounds.append(jnp.sqrt(jnp.sum(own * own, axis=0, keepdims=True) * NORM_SLACK))
    return bounds


def _finish_heads(lanes, g_ref, o_ref):
    n_pairs = o_ref.shape[2] // LANES
    n_chunks = len(lanes) // (2 * n_pairs)
    for pp in range(n_pairs):
        mine = lanes[2 * pp * n_chunks:2 * (pp + 1) * n_chunks]
        outs = [jnp.concatenate(mine[hh * n_chunks:(hh + 1) * n_chunks], axis=1) for hh in range(2)]
        normed = [o * lax.rsqrt(jnp.mean(o * o, axis=0, keepdims=True) + EPS) for o in outs]
        cols = slice(pp * LANES, (pp + 1) * LANES)
        o_ref[0, :, cols] = (jnp.concatenate(normed, axis=0).T * g_ref[:, cols]).astype(o_ref.dtype)


def _fox_kernel(q_ref, k_ref, vt_ref, g_ref, stats_ref, o_ref, s_buf, cmax_buf, p_buf, acc_buf):
    bk = vt_ref.shape[3]
    n_lanes, cw = acc_buf.shape[0], acc_buf.shape[2]
    n_chunks = q_ref.shape[1]
    chunks_per_sub = ATT_Q // cw
    assert n_chunks == 2 * chunks_per_sub and q_ref.shape[3] == cw
    lane_group = lambda li: (li // (2 * n_chunks), (li // n_chunks) % 2, li % n_chunks)
    first_q_block = 2 * pl.program_id(2)
    n_tiles = 2 * (first_q_block + 1)
    ones3 = lambda lo: (lambda lane: jnp.where((lane >= lo) & (lane < lo + 3), 1.0, 0.0))
    queries = _lane_queries(q_ref, ones3(HEAD_DIM), ones3(0))
    acc_buf[...] = jnp.zeros(acc_buf.shape, F32)
    for li in range(n_lanes):
        if _hidden(_diag_visibility(0, lane_group(li)[2] % chunks_per_sub, bk, cw, strict=False)):
            p_buf[0, li] = jnp.zeros((bk, cw), BF16)

    full, idle = ("below", "below", True), (None, None, False)
    ones_rows = jnp.ones((BF16_SUBLANES, bk), BF16)

    def step(t, slot, carry, stages=(full, full)):
        new = []
        for li in range(n_lanes):
            pp, hh, chunk = lane_group(li)
            sub, c = divmod(chunk, chunks_per_sub)
            score, softmax, value = stages[sub]
            see = lambda u: "all" if u == "below" else _diag_visibility(u, c, bk, cw, strict=False)
            if score is not None and not _hidden(see(score)):
                start = pl.multiple_of((n_tiles - 2 - t) * bk, bk)
                s_new = _dot(k_ref[0, 2 * pp + hh, pl.ds(start, bk), :], queries[li])
                if not isinstance(see(score), str):
                    s_new = jnp.where(see(score), s_new, MASKED)
                s_buf[1 - slot, li] = s_new
                cmax_buf[1 - slot, li] = jnp.max(s_new, axis=0, keepdims=True)
            pv = None
            if value:
                vt = vt_ref[0, n_tiles - t, pl.ds(pp * LANES + hh * HEAD_DIM, HEAD_DIM), :]
                pv = _dot(jnp.concatenate([vt, ones_rows], axis=0), p_buf[1 - slot, li])[:acc_buf.shape[1]]
            m = carry[li]
            if softmax is not None and not _hidden(see(softmax)):
                m_new = jnp.maximum(m, cmax_buf[slot, li])
                alpha = jnp.exp2(m - m_new)
                p_buf[slot, li] = jnp.exp2(s_buf[slot, li] - m_new).astype(BF16)
                m = m_new
                acc_buf[li] = alpha * (acc_buf[li] if pv is None else acc_buf[li] + pv)
            elif pv is not None:
                acc_buf[li] += pv
            new.append(m)
        return tuple(new)

    def step_pair(i, carry):
        t = 2 * i + 1
        return step(t + 1, 0, step(t, 1, carry))

    q_norm = _query_norm_bounds(queries, n_chunks)

    def later_tiles_matter(i, carry):
        j_rest = jnp.maximum(n_tiles - 5 - 2 * i, 0)
        worst = None
        for li in range(n_lanes):
            pp, hh, _ = lane_group(li)
            bound = (q_norm[li] * stats_ref[0, pp, j_rest, hh:hh + 1, :]
                     + stats_ref[0, pp, j_rest, 2 + hh:3 + hh, :] - carry[li])
            worst = bound if worst is None else jnp.maximum(worst, bound)
        return jnp.max(worst) >= -PRUNE_LOG2

    def pair_and_check(state):
        i, _, carry = state
        carry = step_pair(i, carry)
        return i + 1, later_tiles_matter(i, carry), carry

    carry = tuple(jnp.full((1, cw), M_INIT, F32) for _ in range(n_lanes))
    carry = step(-3, 1, carry, (idle, (0, None, False)))
    carry = step(-2, 0, carry, (idle, (1, 0, False)))
    carry = step(-1, 1, carry, ((0, None, False), full))
    carry = step(0, 0, carry, ((1, 0, False), full))
    n_pairs, _, carry = lax.while_loop(lambda st: (st[0] < first_q_block) & st[1], pair_and_check,
                                       (jnp.int32(0), jnp.bool_(True), carry))
    drain = (None, "below", True)
    carry = step(2 * n_pairs + 1, 1, carry, (drain, drain))
    drain = (None, None, True)
    carry = step(2 * n_pairs + 2, 0, carry, (drain, drain))
    _finish_heads([acc_buf[li, :HEAD_DIM, :] / acc_buf[li, HEAD_DIM:HEAD_DIM + 1, :] for li in range(n_lanes)], g_ref, o_ref)


def _sb_kernel(q_ref, k_ref, vt_ref, g_ref, o_ref, z_buf, sp_buf, e_buf, wrow_buf, later_buf, acc_buf):
    qi = pl.program_id(2)
    bk = vt_ref.shape[3]
    n_wide, n_first = z_buf.shape[0], SB_WIDE_TILES
    n_lanes, cw = acc_buf.shape[0], acc_buf.shape[2]
    n_chunks = q_ref.shape[1]
    lane_group = lambda li: (li // (2 * n_chunks), (li // n_chunks) % 2, li % n_chunks)
    n_tiles = 2 * (qi + 1)
    zero = lambda channel: 0.0
    queries = _lane_queries(q_ref, zero, zero)
    suffix = (lax.broadcasted_iota(jnp.int32, (bk, bk), 1)
              >= lax.broadcasted_iota(jnp.int32, (bk, bk), 0)).astype(BF16)
    softplus2 = lambda z: jnp.maximum(z, jnp.log2(1.0 + jnp.exp2(jnp.minimum(z, EXP2_MAX))))
    keys = lambda j, pp: k_ref[0, pl.ds(pl.multiple_of(j * bk, bk), bk), pp * LANES:(pp + 1) * LANES]
    values = lambda j, pp, hh: vt_ref[0, j, pl.ds(pp * LANES + hh * HEAD_DIM, HEAD_DIM), :]

    sees = lambda u, li: _diag_visibility(u, lane_group(li)[2], bk, cw, strict=True) if u < 2 else "all"

    def wide_block(tiles, later, check_exists):
        live = [(u, li) for u in tiles for li in range(n_lanes) if not _hidden(sees(u, li))]
        later, col_sums = list(later), {}

        def score(u, li):
            z = _dot(keys(jnp.maximum(n_tiles - 1 - u, 0), lane_group(li)[0]), queries[li])
            if not isinstance(sees(u, li), str):
                z = jnp.where(sees(u, li), z, MASKED)
            if check_exists and u >= 2:
                z = jnp.where(u < n_tiles, z, MASKED)
            z_buf[u, li] = z

        def softplus(u, li):
            sp_buf[u, li] = softplus2(z_buf[u, li]).astype(BF16)

        def cumsum(u, li):
            within = _dot(suffix, sp_buf[u, li])
            col_sums[u, li] = within[0:1, :]
            z_buf[u, li] = z_buf[u, li] - within

        def weight(u, li):
            pp, hh, _ = lane_group(li)
            a = jnp.exp2(z_buf[u, li] - later[li])
            acc_buf[li] += _dot(values(jnp.maximum(n_tiles - 1 - u, 0), pp, hh), a.astype(BF16))
            later[li] = later[li] + col_sums[u, li]

        stages = (score, softplus, cumsum, weight)
        for pos in range(len(live) + len(stages) - 1):
            for lag, stage in enumerate(stages):
                if 0 <= pos - lag < len(live):
                    stage(*live[pos - lag])
        return later

    n_rest = n_tiles - n_wide

    def rest_matters(mass):
        least = mass[0]
        for li in range(1, n_lanes):
            least = jnp.minimum(least, mass[li])
        return jnp.min(least) <= PRUNE_LOG2

    def step(t, slot, later, score=True, softplus=True, cumsum=True, weight=True):
        new_later = []
        for li in range(n_lanes):
            pp, hh, _ = lane_group(li)
            if cumsum:
                within = _dot(suffix, sp_buf[1 - slot, li])
                e_buf[1 - slot, li] = z_buf[1 - slot, li] - within
                wrow_buf[1 - slot, li] = within[0:1, :]
            if score:
                z_buf[1 - slot, li] = _dot(keys(n_rest - 1 - (t + 3), pp), queries[li])
            if weight:
                a = jnp.exp2(e_buf[slot, li] - later[li])
                acc_buf[li] += _dot(values(n_rest - 1 - jnp.maximum(t, 0), pp, hh), a.astype(BF16))
                new_later.append(later[li] + wrow_buf[slot, li])
            else:
                new_later.append(later[li])
            if softplus:
                sp_buf[slot, li] = softplus2(z_buf[slot, li]).astype(BF16)
        return tuple(new_later)

    def pair_and_check(state):
        i, _, later = state
        t = 2 * i - 1
        later = step(t + 1, 0, step(t, 1, later))
        mass = [later[li] + wrow_buf[1, li] for li in range(n_lanes)]
        return i + 1, rest_matters(mass), later

    acc_buf[...] = jnp.zeros(acc_buf.shape, F32)
    first = wide_block(range(n_first), [jnp.zeros((1, cw), F32)] * n_lanes, check_exists=True)
    for li in range(n_lanes):
        later_buf[li] = first[li]
    swept = lambda: [later_buf[li] for li in range(n_lanes)]
    more_matters = (n_rest + (n_wide - n_first) > 0) & rest_matters(first)

    @pl.when(more_matters)
    def _():
        more = wide_block(range(n_first, n_wide), first, check_exists=False)
        for li in range(n_lanes):
            later_buf[li] = more[li]

    @pl.when(more_matters & (n_rest > 0) & rest_matters(swept()))
    def _():
        later = swept()
        e_buf[1] = jnp.full(e_buf.shape[1:], MASKED, F32)
        wrow_buf[1] = jnp.zeros(wrow_buf.shape[1:], F32)
        mass = step(-3, 1, tuple(later), softplus=False, cumsum=False, weight=False)
        mass = step(-2, 0, mass, cumsum=False, weight=False)
        n_pairs, _, mass = lax.while_loop(lambda st: (2 * st[0] + 2 < n_rest) & st[1], pair_and_check,
                                          (jnp.int32(0), jnp.bool_(True), mass))
        mass = step(2 * n_pairs - 1, 1, mass, score=False)
        mass = step(2 * n_pairs, 0, mass, score=False, softplus=False)
        step(2 * n_pairs + 1, 1, mass, score=False, softplus=False, cumsum=False)

    _finish_heads([acc_buf[li] for li in range(n_lanes)], g_ref, o_ref)


def _attention(body, name, pairs, bq, scratch, k_arr, k_spec, vqt, g, q_block0, vt_block0, n_heads, stats=None):
    bsz, s = vqt.shape[0], vqt.shape[1] * vqt.shape[3]
    bk, width = ATT_K, pairs * LANES
    d_grp = n_heads * HEAD_DIM
    assert (n_heads // 2) % pairs == 0
    in_specs = [pl.BlockSpec((1, bq // ATT_COLS, width, ATT_COLS), lambda b, p, i: (b, i, q_block0 + p, 0)),
                k_spec,
                pl.BlockSpec((1, s // bk, width, bk), lambda b, p, i: (b, 0, vt_block0 + p, 0)),
                pl.BlockSpec((1, width), lambda b, p, i: (0, p))]
    operands = [vqt, k_arr, vqt, g.reshape(1, d_grp)]
    if stats is not None:
        in_specs.append(pl.BlockSpec((1, pairs) + stats.shape[2:], lambda b, p, i: (b, p, 0, 0, 0)))
        operands.append(stats)
    return pl.pallas_call(
        body,
        grid=(bsz, n_heads // 2 // pairs, s // bq),
        in_specs=in_specs,
        out_specs=pl.BlockSpec((1, bq, width), lambda b, p, i: (b, i, p)),
        out_shape=jax.ShapeDtypeStruct((bsz, s, d_grp), BF16),
        scratch_shapes=scratch,
        compiler_params=_params("arbitrary", "arbitrary", "arbitrary"),
        name=name,
    )(*operands)


def _lane_groups(pairs, bq):
    return pairs * 2 * (bq // ATT_COLS)


def _fox_scratch(pairs):
    n = _lane_groups(pairs, FOX_Q)
    return [pltpu.VMEM((2, n, ATT_K, ATT_COLS), F32), pltpu.VMEM((2, n, 1, ATT_COLS), F32),
            pltpu.VMEM((2, n, ATT_K, ATT_COLS), BF16), pltpu.VMEM((n, HEAD_DIM + F32_SUBLANES, ATT_COLS), F32)]


def _sb_scratch(pairs):
    n = _lane_groups(pairs, ATT_Q)
    wide = SB_WIDE_TILES + SB_MORE_TILES
    return [pltpu.VMEM((wide, n, ATT_K, ATT_COLS), F32), pltpu.VMEM((wide, n, ATT_K, ATT_COLS), BF16),
            pltpu.VMEM((2, n, ATT_K, ATT_COLS), F32), pltpu.VMEM((2, n, 1, ATT_COLS), F32),
            pltpu.VMEM((n, 1, ATT_COLS), F32), pltpu.VMEM((n, HEAD_DIM, ATT_COLS), F32)]


def _mixer_kernel(x_ref, xh_ref, mf_ref, mfh_ref, ms_ref, msh_ref, mod_ref, wo_ref, gm_ref, wu_ref, cw_ref,
                  cb_ref, wd_ref, gf_ref, o_ref, u_buf, acc_ref, x1_buf, *, final_norm):
    i = pl.program_id(1)
    tm = x_ref.shape[1]
    n_chunks = wd_ref.shape[0]
    mix = jnp.concatenate([jnp.concatenate([mfh_ref[0], msh_ref[0]], axis=-1),
                           jnp.concatenate([mf_ref[0], ms_ref[0]], axis=-1)], axis=0)
    x_ext = jnp.concatenate([xh_ref[0], x_ref[0]], axis=0)
    x1_ext = x_ext + mod_ref[0, 2:3, :] * _dot(mix, wo_ref[...])
    shift = mod_ref[0, 3:4, :]
    scale = mod_ref[0, 4:5, :]
    h_ext = _rms_rows(x1_ext) * gm_ref[...] * (1.0 + scale) + shift
    row = lax.broadcasted_iota(jnp.int32, h_ext.shape, 0)
    hx = jnp.where((row >= BF16_SUBLANES) | (i > 0), h_ext, 0.0).astype(BF16)
    x1_buf[...] = x1_ext[BF16_SUBLANES:, :]
    acc_ref[...] = jnp.zeros_like(acc_ref)

    def project_up(c, slot):
        for br in range(2):
            u_buf[slot, br] = _dot(hx, wu_ref[br, c])

    def mix_down(c, slot):
        branches = []
        for br in range(2):
            out = cb_ref[br, c]
            for tap in range(CONV_WIDTH):
                first = BF16_SUBLANES - (CONV_WIDTH - 1 - tap)
                out = out + cw_ref[br, c, tap:tap + 1, :] * u_buf[slot, br, pl.ds(first, tm), :]
            branches.append(out)
        u_gate, u_val = branches
        acc_ref[...] += _dot((u_gate * jax.nn.sigmoid(u_gate) * u_val).astype(BF16), wd_ref[c])

    def chunk_pair(j, _):
        c = 2 * j
        project_up(c + 1, 1)
        mix_down(c, 0)
        project_up(c + 2, 0)
        mix_down(c + 1, 1)
        return 0

    project_up(0, 0)
    lax.fori_loop(0, (n_chunks - 1) // 2, chunk_pair, 0)
    mix_down(n_chunks - 1, 0)
    x2 = x1_buf[...] + mod_ref[0, 5:6, :] * acc_ref[...]
    o_ref[0] = _rms_rows(x2) * gf_ref[...] if final_norm else x2


def _mixer(x, mix_f, mix_s, mod, w_out, g_mlp, w_up, conv_w, conv_b, w_down, g_final, final_norm):
    bsz, s, d = x.shape
    tm = OUT_ROWS
    n_chunks, tf = w_down.shape[0], w_down.shape[1]
    assert n_chunks % 2 == 1
    halo_blocks = tm // BF16_SUBLANES
    row = lambda b, i: (b, i, 0)
    halo = lambda b, i: (b, jnp.maximum(i * halo_blocks - 1, 0), 0)
    tile_and_halo = lambda a: [pl.BlockSpec((1, tm, a.shape[2]), row),
                               pl.BlockSpec((1, BF16_SUBLANES, a.shape[2]), halo)]
    resident = lambda a: pl.BlockSpec(a.shape, lambda b, i: (0,) * a.ndim, pipeline_mode=pl.Buffered(1))
    return pl.pallas_call(
        functools.partial(_mixer_kernel, final_norm=final_norm),
        grid=(bsz, s // tm),
        in_specs=tile_and_halo(x) + tile_and_halo(mix_f) + tile_and_halo(mix_s)
                 + [pl.BlockSpec((1, N_MOD, d), lambda b, i: (b, 0, 0)), resident(w_out),
                    pl.BlockSpec((1, d), lambda b, i: (0, 0)),
                    resident(w_up), resident(conv_w), resident(conv_b), resident(w_down),
                    pl.BlockSpec((1, d), lambda b, i: (0, 0))],
        out_specs=pl.BlockSpec((1, tm, d), row),
        out_shape=jax.ShapeDtypeStruct((bsz, s, d), F32),
        scratch_shapes=[pltpu.VMEM((2, 2, tm + BF16_SUBLANES, tf), F32), pltpu.VMEM((tm, d), F32),
                        pltpu.VMEM((tm, d), F32)],
        compiler_params=_params("arbitrary", "arbitrary"),
        name="mixer",
    )(x, x, mix_f, mix_f, mix_s, mix_s, mod, w_out, g_mlp, w_up, conv_w, conv_b, w_down, g_final)


def _chunk_columns(a, d_ff, n_chunks, tf):
    halves = jnp.stack([a[:, :d_ff], a[:, d_ff:]])
    halves = jnp.pad(halves, ((0, 0), (0, 0), (0, n_chunks * tf - d_ff)))
    return halves.reshape(2, a.shape[0], n_chunks, tf).transpose(0, 2, 1, 3)


def _pad_cols(a, n):
    return jnp.pad(a, ((0, 0), (0, n - a.shape[1])))


def kernel(x, c, w_ada, b_ada, g_attn, w_in, b_fgate, g_out_fox, g_out_sb, w_out,
           g_mlp, w_up, conv_w, conv_b, w_down, g_final):
    depth, d, _ = w_ada.shape
    n_fox = b_fgate.shape[1]
    d_fox = n_fox * HEAD_DIM
    d_sb = g_out_sb.shape[1]
    n_sb = d_sb // HEAD_DIM
    d_ff = w_down.shape[1]
    d_ff_pad = -(-d_ff // FF_CHUNK) * FF_CHUNK
    assert n_fox % 2 == 0 and n_sb == n_fox and 3 * n_fox <= LANES
    assert x.shape[1] % OUT_ROWS == 0 and x.shape[1] % FOX_Q == 0 and ATT_Q == 2 * ATT_K and ATT_COLS == ATT_K
    o_kf, o_vf, o_qs, o_ks, o_vs, o_gate = (d_fox, 2 * d_fox, 3 * d_fox, 3 * d_fox + d_sb,
                                             3 * d_fox + 2 * d_sb, 3 * d_fox + 3 * d_sb)

    for l in range(depth):
        mod = _ada(c, w_ada[l], b_ada[l]).reshape(-1, N_MOD, d)
        order = jnp.argsort(b_fgate[l])
        pick_head = (jnp.arange(n_fox)[:, None] == order[None, :]).astype(BF16)
        pick = jnp.kron(pick_head, jnp.eye(HEAD_DIM, dtype=BF16))
        w, wo = w_in[l].astype(BF16), w_out[l].astype(BF16)
        w_nat = jnp.concatenate([jnp.dot(w[:, o_kf:o_vf], pick), w[:, o_ks:o_vs]], axis=1)
        w_vt = jnp.concatenate([jnp.dot(pick.T, w[:, o_vf:o_qs].T), w[:, o_vs:o_gate].T,
                                jnp.dot(pick.T, w[:, :o_kf].T), w[:, o_qs:o_ks].T], axis=0)
        w_gate = _pad_cols(jnp.dot(w[:, o_gate:], pick_head), LANES)
        b_gate = _pad_cols(b_fgate[l][order].reshape(1, n_fox), LANES)
        g_fox = g_out_fox[l].reshape(n_fox, HEAD_DIM)[order].reshape(-1)
        w_mix = jnp.concatenate([jnp.dot(pick.T, wo[:d_fox]), wo[d_fox:]], axis=0)
        k_nat, vqt, k_aug, stats = _inproj(x, mod, g_attn[l].reshape(1, d), w_nat, w_vt, w_gate, b_gate, n_fox)

        wf, ws = FOX_PAIRS * LANES, SB_PAIRS * LANES
        fox_k_spec = pl.BlockSpec((1, 2 * FOX_PAIRS, x.shape[1], LANES), lambda b, p, i: (b, p, 0, 0))
        mix_f = _attention(_fox_kernel, "fox", FOX_PAIRS, FOX_Q, _fox_scratch(FOX_PAIRS), k_aug, fox_k_spec, vqt,
                           g_fox, q_block0=(d_fox + d_sb) // wf, vt_block0=0, n_heads=n_fox, stats=stats)
        sb_k_spec = pl.BlockSpec((1, x.shape[1], ws), lambda b, p, i: (b, 0, d_fox // ws + p))
        mix_s = _attention(_sb_kernel, "sb", SB_PAIRS, ATT_Q, _sb_scratch(SB_PAIRS), k_nat, sb_k_spec, vqt,
                           g_out_sb[l], q_block0=(2 * d_fox + d_sb) // ws, vt_block0=d_fox // ws, n_heads=n_sb)

        n_ff = d_ff_pad // FF_CHUNK
        x = _mixer(x, mix_f, mix_s, mod, w_mix, g_mlp[l].reshape(1, d),
                   _chunk_columns(w_up[l], d_ff, n_ff, FF_CHUNK).astype(BF16),
                   _chunk_columns(conv_w[l], d_ff, n_ff, FF_CHUNK),
                   _chunk_columns(conv_b[l].reshape(1, -1), d_ff, n_ff, FF_CHUNK),
                   jnp.pad(w_down[l], ((0, d_ff_pad - d_ff), (0, 0))).astype(BF16).reshape(n_ff, FF_CHUNK, d),
                   g_final.reshape(1, d), final_norm=(l == depth - 1))
    return x
```

```python
import functools

import numpy as np
import jax
import jax.numpy as jnp
from jax import lax
from jax.experimental import pallas as pl
from jax.experimental.pallas import tpu as pltpu

HEAD_DIM = 64
N_MOD = 6
CONV_WIDTH = 3
EPS = 1e-6

LANES = 128
F32_SUBLANES = 8
BF16_SUBLANES = 16
VMEM_LIMIT_BYTES = 56 * 1024 * 1024

ATT_Q = 512
FOX_Q = 2 * ATT_Q
ATT_K = 256
ATT_COLS = 256
FOX_PAIRS = 1
SB_PAIRS = 2
SB_WIDE_TILES = 4
SB_MORE_TILES = 2
PROJ_ROWS = 4 * ATT_K
LOG2E = 1.4426950408889634
MASKED = -1e30
M_INIT = -1e29
EXP2_MAX = 126.0
PRUNE_LOG2 = 152.0
NORM_SLACK = 1.02
STATS_ROWS = 8
OUT_ROWS = 1024
FF_CHUNK = 256

F32 = jnp.float32
BF16 = jnp.bfloat16
NT_DIMS = (((1,), (1,)), ((), ()))


def _dot(a, b):
    return jnp.dot(a, b, preferred_element_type=F32)


def _dot_nt(a, b):
    return lax.dot_general(a, b, NT_DIMS, preferred_element_type=F32)


def _params(*sem):
    return pltpu.CompilerParams(dimension_semantics=sem, vmem_limit_bytes=VMEM_LIMIT_BYTES)


def _rms_rows(x):
    return x * lax.rsqrt(jnp.mean(x * x, axis=-1, keepdims=True) + EPS)


def _softplus(z):
    return jnp.maximum(z, 0.0) + jnp.log(1.0 + jnp.exp(-jnp.abs(z)))


def _split3(x):
    hi = x.astype(BF16)
    r1 = x - hi.astype(F32)
    mid = r1.astype(BF16)
    lo = (r1 - mid.astype(F32)).astype(BF16)
    return hi, mid, lo


def _ada_kernel(c_ref, w_ref, b_ref, o_ref):
    c = c_ref[...]
    o_ref[...] = _dot(c * jax.nn.sigmoid(c), w_ref[...]) + b_ref[...]


def _ada(c, w, b):
    bsz, d = c.shape
    n = w.shape[1]
    return pl.pallas_call(
        _ada_kernel,
        grid=(n // d,),
        in_specs=[pl.BlockSpec((bsz, d), lambda j: (0, 0)),
                  pl.BlockSpec((d, d), lambda j: (0, j)),
                  pl.BlockSpec((1, d), lambda j: (0, j))],
        out_specs=pl.BlockSpec((bsz, d), lambda j: (0, j)),
        out_shape=jax.ShapeDtypeStruct((bsz, n), F32),
        compiler_params=_params("arbitrary"),
        name="ada",
    )(c, w, b.reshape(1, n))


def _decay_prefix(lf, carry, n_heads):
    tk = lf.shape[0]
    lane = lax.broadcasted_iota(jnp.int32, (tk, LANES), 1)
    lf = jnp.where(lane < n_heads, lf, 0.0)
    row = lax.broadcasted_iota(jnp.int32, (tk, tk), 0)
    col = lax.broadcasted_iota(jnp.int32, (tk, tk), 1)
    tri = (col <= row).astype(BF16)
    hi, mid, lo = _split3(lf)
    terms = (hi.astype(F32) + pltpu.roll(mid.astype(F32), LANES // 4, 1)
             + pltpu.roll(lo.astype(F32), LANES // 2, 1)).astype(BF16)
    part = _dot(tri, terms)
    total = part + pltpu.roll(part, LANES - LANES // 4, 1) + pltpu.roll(part, LANES // 2, 1)
    return jnp.where(lane < n_heads, carry + total, 0.0)


def _decay_keys(f_run, k_fox, sel_ref, kaug_ref, rows, n_heads):
    tk = f_run.shape[0]
    lane = lax.broadcasted_iota(jnp.int32, (tk, LANES), 1)
    ghi, gmid, glo = _split3(-LOG2E * f_run)
    packed = (ghi.astype(F32) + pltpu.roll(gmid.astype(F32), n_heads, 1)
              + pltpu.roll(glo.astype(F32), 2 * n_heads, 1)).astype(BF16)
    placed = _dot(packed, sel_ref[...]).astype(BF16)
    for h in range(n_heads):
        pair = slice((h // 2) * LANES, (h // 2 + 1) * LANES)
        own = (lane < HEAD_DIM) if h % 2 == 0 else (lane >= HEAD_DIM)
        kaug_ref[0, h, rows, :] = jnp.where(own, k_fox[:, pair], placed[:, pair])


def _decay_stats(f_run, k_fox, ind_ref, kpre, stats_ref, tile, n_heads):
    tk = f_run.shape[0]
    k32 = k_fox.astype(F32)
    sq = _dot((k32 * k32).astype(BF16), ind_ref[...])
    kpre = jnp.maximum(kpre, jnp.sqrt(jnp.max(sq, axis=0, keepdims=True) * NORM_SLACK))
    g_end = -LOG2E * f_run[tk - 1:tk, :]
    lane1 = lax.broadcasted_iota(jnp.int32, (1, LANES), 1)

    def spread(v, h):
        return jnp.broadcast_to(jnp.sum(jnp.where(lane1 == h, v, 0.0), axis=1, keepdims=True), (1, ATT_COLS))

    for p in range(n_heads // 2):
        srows = [spread(v, 2 * p + hh) for v in (kpre, g_end) for hh in range(2)]
        srows += [jnp.zeros((1, ATT_COLS), F32)] * (stats_ref.shape[3] - len(srows))
        stats_ref[0, p, tile] = jnp.concatenate(srows, axis=0)
    return kpre


def _inproj_kernel(x_ref, mod_ref, g_ref, wn_ref, wvt_ref, wg_ref, bg_ref, sel_ref, ind_ref,
                   k_ref, vt_ref, kaug_ref, stats_ref, carry_ref, kpre_ref, *, n_heads):
    @pl.when(pl.program_id(1) == 0)
    def _():
        carry_ref[...] = jnp.zeros_like(carry_ref)
        kpre_ref[...] = jnp.zeros_like(kpre_ref)

    shift = mod_ref[0, 0:1, :]
    scale = mod_ref[0, 1:2, :]
    h = (_rms_rows(x_ref[0]) * g_ref[...] * (1.0 + scale) + shift).astype(BF16)
    logit = _dot(h, wg_ref[...]) + bg_ref[...]
    log_f = -_softplus(-logit)
    k = _dot(h, wn_ref[...]).astype(BF16)
    k_ref[0] = k
    d_grp = n_heads * HEAD_DIM
    tk = vt_ref.shape[3]
    tiles = [slice(t * tk, (t + 1) * tk) for t in range(vt_ref.shape[1])]
    f_run, carry = [], carry_ref[...]
    for rows in tiles:
        f_run.append(_decay_prefix(log_f[rows, :], carry, n_heads))
        carry = f_run[-1][tk - 1:tk, :]
    carry_ref[...] = carry
    v_t = lambda rows: _dot_nt(wvt_ref[...], h[rows, :]).astype(BF16)
    vt_ref[0, 0] = v_t(tiles[0])
    kpre = kpre_ref[...]
    for t, rows in enumerate(tiles):
        _decay_keys(f_run[t], k[rows, :d_grp], sel_ref, kaug_ref, rows, n_heads)
    for t, rows in enumerate(tiles):
        kpre = _decay_stats(f_run[t], k[rows, :d_grp], ind_ref, kpre, stats_ref, t, n_heads)
    kpre_ref[...] = kpre
    for t in range(1, len(tiles)):
        vt_ref[0, t] = v_t(tiles[t])


def _head_indicator(n_heads):
    ind = np.zeros((n_heads * HEAD_DIM, LANES), np.float32)
    ind[np.arange(n_heads * HEAD_DIM), np.arange(n_heads * HEAD_DIM) // HEAD_DIM] = 1.0
    return jnp.asarray(ind, BF16)


def _decay_select_matrix(n_heads):
    sel = np.zeros((LANES, n_heads // 2 * LANES), np.float32)
    for h in range(n_heads):
        base = h // 2 * LANES + (HEAD_DIM if h % 2 == 0 else 0)
        for term in range(3):
            sel[term * n_heads + h, base + term] = 1.0
    return jnp.asarray(sel, BF16)


def _inproj(x, mod, g, w_nat, w_vt, w_gate, b_gate, n_heads):
    bsz, s, d = x.shape
    tm, tk = PROJ_ROWS, ATT_K
    n_nat, n_v = w_nat.shape[1], w_vt.shape[0]
    const = lambda b, i: (0, 0)
    resident = lambda a: pl.BlockSpec(a.shape, const, pipeline_mode=pl.Buffered(1))
    sel, ind = _decay_select_matrix(n_heads), _head_indicator(n_heads)
    return pl.pallas_call(
        functools.partial(_inproj_kernel, n_heads=n_heads),
        grid=(bsz, s // tm),
        in_specs=[pl.BlockSpec((1, tm, d), lambda b, i: (b, i, 0)),
                  pl.BlockSpec((1, N_MOD, d), lambda b, i: (b, 0, 0)),
                  pl.BlockSpec((1, d), const),
                  resident(w_nat), resident(w_vt), resident(w_gate),
                  pl.BlockSpec((1, LANES), const),
                  resident(sel), resident(ind)],
        out_specs=[pl.BlockSpec((1, tm, n_nat), lambda b, i: (b, i, 0)),
                   pl.BlockSpec((1, tm // tk, n_v, tk), lambda b, i: (b, i, 0, 0)),
                   pl.BlockSpec((1, n_heads, tm, LANES), lambda b, i: (b, 0, i, 0)),
                   pl.BlockSpec((1, n_heads // 2, tm // tk, STATS_ROWS, ATT_COLS), lambda b, i: (b, 0, i, 0, 0))],
        out_shape=[jax.ShapeDtypeStruct((bsz, s, n_nat), BF16),
                   jax.ShapeDtypeStruct((bsz, s // tk, n_v, tk), BF16),
                   jax.ShapeDtypeStruct((bsz, n_heads, s, LANES), BF16),
                   jax.ShapeDtypeStruct((bsz, n_heads // 2, s // tk, STATS_ROWS, ATT_COLS), F32)],
        scratch_shapes=[pltpu.VMEM((1, LANES), F32), pltpu.VMEM((1, LANES), F32)],
        compiler_params=_params("arbitrary", "arbitrary"),
        name="inproj",
    )(x, mod, g, w_nat, w_vt, w_gate, b_gate, sel, ind)


def _lane_queries(q_ref, extra_even, extra_odd):
    out = []
    channel = lax.broadcasted_iota(jnp.int32, (LANES, q_ref.shape[3]), 0)
    for pp in range(q_ref.shape[2] // LANES):
        for hh, extra in enumerate((extra_even, extra_odd)):
            own = (channel < HEAD_DIM) if hh == 0 else (channel >= HEAD_DIM)
            for c in range(q_ref.shape[1]):
                q = q_ref[0, c, pp * LANES:(pp + 1) * LANES, :].astype(F32) * (HEAD_DIM ** -0.5 * LOG2E)
                out.append(jnp.where(own, q, extra(channel)).astype(BF16))
    return out


def _visibility(first_key, first_query, bk, cw, strict):
    last_visible_gap = -1 if strict else 0
    if first_key + bk - 1 - first_query <= last_visible_gap:
        return "all"
    if first_key - (first_query + cw - 1) > last_visible_gap:
        return "none"
    gap = (lax.broadcasted_iota(jnp.int32, (bk, cw), 0) - lax.broadcasted_iota(jnp.int32, (bk, cw), 1)
           + (first_key - first_query))
    return gap <= last_visible_gap


def _diag_visibility(u, c, bk, cw, strict):
    return _visibility((1 - u) * bk, c * cw, bk, cw, strict)


def _hidden(visibility):
    return isinstance(visibility, str) and visibility == "none"


def _query_norm_bounds(queries, n_chunks):
    bounds = []
    for li, q in enumerate(queries):
        hh = (li // n_chunks) % 2
        own = q[hh * HEAD_DIM:(hh + 1) * HEAD_DIM, :].astype(F32)
        bounds.append(jnp.sqrt(jnp.sum(own * own, axis=0, keepdims=True) * NORM_SLACK))
    return bounds


def _finish_heads(lanes, g_ref, o_ref):
    n_pairs = o_ref.shape[2] // LANES
    n_chunks = len(lanes) // (2 * n_pairs)
    for pp in range(n_pairs):
        mine = lanes[2 * pp * n_chunks:2 * (pp + 1) * n_chunks]
        outs = [jnp.concatenate(mine[hh * n_chunks:(hh + 1) * n_chunks], axis=1) for hh in range(2)]
        normed = [o * lax.rsqrt(jnp.mean(o * o, axis=0, keepdims=True) + EPS) for o in outs]
        cols = slice(pp * LANES, (pp + 1) * LANES)
        o_ref[0, :, cols] = (jnp.concatenate(normed, axis=0).T * g_ref[:, cols]).astype(o_ref.dtype)


def _fox_kernel(q_ref, k_ref, vt_ref, g_ref, stats_ref, o_ref, s_buf, cmax_buf, p_buf, acc_buf):
    bk = vt_ref.shape[3]
    n_lanes, cw = acc_buf.shape[0], acc_buf.shape[2]
    n_chunks = q_ref.shape[1]
    chunks_per_sub = ATT_Q // cw
    assert n_chunks == 2 * chunks_per_sub and q_ref.shape[3] == cw
    lane_group = lambda li: (li // (2 * n_chunks), (li // n_chunks) % 2, li % n_chunks)
    first_q_block = 2 * pl.program_id(2)
    n_tiles = 2 * (first_q_block + 1)
    ones3 = lambda lo: (lambda lane: jnp.where((lane >= lo) & (lane < lo + 3), 1.0, 0.0))
    queries = _lane_queries(q_ref, ones3(HEAD_DIM), ones3(0))
    acc_buf[...] = jnp.zeros(acc_buf.shape, F32)
    for li in range(n_lanes):
        if _hidden(_diag_visibility(0, lane_group(li)[2] % chunks_per_sub, bk, cw, strict=False)):
            p_buf[0, li] = jnp.zeros((bk, cw), BF16)

    full, idle = ("below", "below", True), (None, None, False)
    ones_rows = jnp.ones((BF16_SUBLANES, bk), BF16)

    def step(t, slot, carry, stages=(full, full)):
        new = []
        for li in range(n_lanes):
            pp, hh, chunk = lane_group(li)
            sub, c = divmod(chunk, chunks_per_sub)
            score, softmax, value = stages[sub]
            see = lambda u: "all" if u == "below" else _diag_visibility(u, c, bk, cw, strict=False)
            if score is not None and not _hidden(see(score)):
                start = pl.multiple_of((n_tiles - 2 - t) * bk, bk)
                s_new = _dot(k_ref[0, 2 * pp + hh, pl.ds(start, bk), :], queries[li])
                if not isinstance(see(score), str):
                    s_new = jnp.where(see(score), s_new, MASKED)
                s_buf[1 - slot, li] = s_new
                cmax_buf[1 - slot, li] = jnp.max(s_new, axis=0, keepdims=True)
            pv = None
            if value:
                vt = vt_ref[0, n_tiles - t, pl.ds(pp * LANES + hh * HEAD_DIM, HEAD_DIM), :]
                pv = _dot(jnp.concatenate([vt, ones_rows], axis=0), p_buf[1 - slot, li])[:acc_buf.shape[1]]
            m = carry[li]
            if softmax is not None and not _hidden(see(softmax)):
                m_new = jnp.maximum(m, cmax_buf[slot, li])
                alpha = jnp.exp2(m - m_new)
                p_buf[slot, li] = jnp.exp2(s_buf[slot, li] - m_new).astype(BF16)
                m = m_new
                acc_buf[li] = alpha * (acc_buf[li] if pv is None else acc_buf[li] + pv)
            elif pv is not None:
                acc_buf[li] += pv
            new.append(m)
        return tuple(new)

    def step_pair(i, carry):
        t = 2 * i + 1
        return step(t + 1, 0, step(t, 1, carry))

    q_norm = _query_norm_bounds(queries, n_chunks)

    def later_tiles_matter(i, carry):
        j_rest = jnp.maximum(n_tiles - 5 - 2 * i, 0)
        worst = None
        for li in range(n_lanes):
            pp, hh, _ = lane_group(li)
            bound = (q_norm[li] * stats_ref[0, pp, j_rest, hh:hh + 1, :]
                     + stats_ref[0, pp, j_rest, 2 + hh:3 + hh, :] - carry[li])
            worst = bound if worst is None else jnp.maximum(worst, bound)
        return jnp.max(worst) >= -PRUNE_LOG2

    def pair_and_check(state):
        i, _, carry = state
        carry = step_pair(i, carry)
        return i + 1, later_tiles_matter(i, carry), carry

    carry = tuple(jnp.full((1, cw), M_INIT, F32) for _ in range(n_lanes))
    carry = step(-3, 1, carry, (idle, (0, None, False)))
    carry = step(-2, 0, carry, (idle, (1, 0, False)))
    carry = step(-1, 1, carry, ((0, None, False), full))
    carry = step(0, 0, carry, ((1, 0, False), full))
    n_pairs, _, carry = lax.while_loop(lambda st: (st[0] < first_q_block) & st[1], pair_and_check,
                                       (jnp.int32(0), jnp.bool_(True), carry))
    drain = (None, "below", True)
    carry = step(2 * n_pairs + 1, 1, carry, (drain, drain))
    drain = (None, None, True)
    carry = step(2 * n_pairs + 2, 0, carry, (drain, drain))
    _finish_heads([acc_buf[li, :HEAD_DIM, :] / acc_buf[li, HEAD_DIM:HEAD_DIM + 1, :] for li in range(n_lanes)], g_ref, o_ref)


def _sb_kernel(q_ref, k_ref, vt_ref, g_ref, o_ref, z_buf, sp_buf, e_buf, wrow_buf, later_buf, acc_buf):
    qi = pl.program_id(2)
    bk = vt_ref.shape[3]
    n_wide, n_first = z_buf.shape[0], SB_WIDE_TILES
    n_lanes, cw = acc_buf.shape[0], acc_buf.shape[2]
    n_chunks = q_ref.shape[1]
    lane_group = lambda li: (li // (2 * n_chunks), (li // n_chunks) % 2, li % n_chunks)
    n_tiles = 2 * (qi + 1)
    zero = lambda channel: 0.0
    queries = _lane_queries(q_ref, zero, zero)
    suffix = (lax.broadcasted_iota(jnp.int32, (bk, bk), 1)
              >= lax.broadcasted_iota(jnp.int32, (bk, bk), 0)).astype(BF16)
    softplus2 = lambda z: jnp.maximum(z, jnp.log2(1.0 + jnp.exp2(jnp.minimum(z, EXP2_MAX))))
    keys = lambda j, pp: k_ref[0, pl.ds(pl.multiple_of(j * bk, bk), bk), pp * LANES:(pp + 1) * LANES]
    values = lambda j, pp, hh: vt_ref[0, j, pl.ds(pp * LANES + hh * HEAD_DIM, HEAD_DIM), :]

    sees = lambda u, li: _diag_visibility(u, lane_group(li)[2], bk, cw, strict=True) if u < 2 else "all"

    def wide_block(tiles, later, check_exists):
        live = [(u, li) for u in tiles for li in range(n_lanes) if not _hidden(sees(u, li))]
        later, col_sums = list(later), {}

        def score(u, li):
            z = _dot(keys(jnp.maximum(n_tiles - 1 - u, 0), lane_group(li)[0]), queries[li])
            if not isinstance(sees(u, li), str):
                z = jnp.where(sees(u, li), z, MASKED)
            if check_exists and u >= 2:
                z = jnp.where(u < n_tiles, z, MASKED)
            z_buf[u, li] = z

        def softplus(u, li):
            sp_buf[u, li] = softplus2(z_buf[u, li]).astype(BF16)

        def cumsum(u, li):
            within = _dot(suffix, sp_buf[u, li])
            col_sums[u, li] = within[0:1, :]
            z_buf[u, li] = z_buf[u, li] - within

        def weight(u, li):
            pp, hh, _ = lane_group(li)
            a = jnp.exp2(z_buf[u, li] - later[li])
            acc_buf[li] += _dot(values(jnp.maximum(n_tiles - 1 - u, 0), pp, hh), a.astype(BF16))
            later[li] = later[li] + col_sums[u, li]

        stages = (score, softplus, cumsum, weight)
        for pos in range(len(live) + len(stages) - 1):
            for lag, stage in enumerate(stages):
                if 0 <= pos - lag < len(live):
                    stage(*live[pos - lag])
        return later

    n_rest = n_tiles - n_wide

    def rest_matters(mass):
        least = mass[0]
        for li in range(1, n_lanes):
            least = jnp.minimum(least, mass[li])
        return jnp.min(least) <= PRUNE_LOG2

    def step(t, slot, later, score=True, softplus=True, cumsum=True, weight=True):
        new_later = []
        for li in range(n_lanes):
            pp, hh, _ = lane_group(li)
            if cumsum:
                within = _dot(suffix, sp_buf[1 - slot, li])
                e_buf[1 - slot, li] = z_buf[1 - slot, li] - within
                wrow_buf[1 - slot, li] = within[0:1, :]
            if score:
                z_buf[1 - slot, li] = _dot(keys(n_rest - 1 - (t + 3), pp), queries[li])
            if weight:
                a = jnp.exp2(e_buf[slot, li] - later[li])
                acc_buf[li] += _dot(values(n_rest - 1 - jnp.maximum(t, 0), pp, hh), a.astype(BF16))
                new_later.append(later[li] + wrow_buf[slot, li])
            else:
                new_later.append(later[li])
            if softplus:
                sp_buf[slot, li] = softplus2(z_buf[slot, li]).astype(BF16)
        return tuple(new_later)

    def pair_and_check(state):
        i, _, later = state
        t = 2 * i - 1
        later = step(t + 1, 0, step(t, 1, later))
        mass = [later[li] + wrow_buf[1, li] for li in range(n_lanes)]
        return i + 1, rest_matters(mass), later

    acc_buf[...] = jnp.zeros(acc_buf.shape, F32)
    first = wide_block(range(n_first), [jnp.zeros((1, cw), F32)] * n_lanes, check_exists=True)
    for li in range(n_lanes):
        later_buf[li] = first[li]
    swept = lambda: [later_buf[li] for li in range(n_lanes)]
    more_matters = (n_rest + (n_wide - n_first) > 0) & rest_matters(first)

    @pl.when(more_matters)
    def _():
        more = wide_block(range(n_first, n_wide), first, check_exists=False)
        for li in range(n_lanes):
            later_buf[li] = more[li]

    @pl.when(more_matters & (n_rest > 0) & rest_matters(swept()))
    def _():
        later = swept()
        e_buf[1] = jnp.full(e_buf.shape[1:], MASKED, F32)
        wrow_buf[1] = jnp.zeros(wrow_buf.shape[1:], F32)
        mass = step(-3, 1, tuple(later), softplus=False, cumsum=False, weight=False)
        mass = step(-2, 0, mass, cumsum=False, weight=False)
        n_pairs, _, mass = lax.while_loop(lambda st: (2 * st[0] + 2 < n_rest) & st[1], pair_and_check,
                                          (jnp.int32(0), jnp.bool_(True), mass))
        mass = step(2 * n_pairs - 1, 1, mass, score=False)
        mass = step(2 * n_pairs, 0, mass, score=False, softplus=False)
        step(2 * n_pairs + 1, 1, mass, score=False, softplus=False, cumsum=False)

    _finish_heads([acc_buf[li] for li in range(n_lanes)], g_ref, o_ref)


def _attention(body, name, pairs, bq, scratch, k_arr, k_spec, vqt, g, q_block0, vt_block0, n_heads, stats=None):
    bsz, s = vqt.shape[0], vqt.shape[1] * vqt.shape[3]
    bk, width = ATT_K, pairs * LANES
    d_grp = n_heads * HEAD_DIM
    assert (n_heads // 2) % pairs == 0
    in_specs = [pl.BlockSpec((1, bq // ATT_COLS, width, ATT_COLS), lambda b, p, i: (b, i, q_block0 + p, 0)),
                k_spec,
                pl.BlockSpec((1, s // bk, width, bk), lambda b, p, i: (b, 0, vt_block0 + p, 0)),
                pl.BlockSpec((1, width), lambda b, p, i: (0, p))]
    operands = [vqt, k_arr, vqt, g.reshape(1, d_grp)]
    if stats is not None:
        in_specs.append(pl.BlockSpec((1, pairs) + stats.shape[2:], lambda b, p, i: (b, p, 0, 0, 0)))
        operands.append(stats)
    return pl.pallas_call(
        body,
        grid=(bsz, n_heads // 2 // pairs, s // bq),
        in_specs=in_specs,
        out_specs=pl.BlockSpec((1, bq, width), lambda b, p, i: (b, i, p)),
        out_shape=jax.ShapeDtypeStruct((bsz, s, d_grp), BF16),
        scratch_shapes=scratch,
        compiler_params=_params("arbitrary", "arbitrary", "arbitrary"),
        name=name,
    )(*operands)


def _lane_groups(pairs, bq):
    return pairs * 2 * (bq // ATT_COLS)


def _fox_scratch(pairs):
    n = _lane_groups(pairs, FOX_Q)
    return [pltpu.VMEM((2, n, ATT_K, ATT_COLS), F32), pltpu.VMEM((2, n, 1, ATT_COLS), F32),
            pltpu.VMEM((2, n, ATT_K, ATT_COLS), BF16), pltpu.VMEM((n, HEAD_DIM + F32_SUBLANES, ATT_COLS), F32)]


def _sb_scratch(pairs):
    n = _lane_groups(pairs, ATT_Q)
    wide = SB_WIDE_TILES + SB_MORE_TILES
    return [pltpu.VMEM((wide, n, ATT_K, ATT_COLS), F32), pltpu.VMEM((wide, n, ATT_K, ATT_COLS), BF16),
            pltpu.VMEM((2, n, ATT_K, ATT_COLS), F32), pltpu.VMEM((2, n, 1, ATT_COLS), F32),
            pltpu.VMEM((n, 1, ATT_COLS), F32), pltpu.VMEM((n, HEAD_DIM, ATT_COLS), F32)]


def _mixer_kernel(x_ref, xh_ref, mf_ref, mfh_ref, ms_ref, msh_ref, mod_ref, wo_ref, gm_ref, wu_ref, cw_ref,
                  cb_ref, wd_ref, gf_ref, o_ref, u_buf, acc_ref, x1_buf, *, final_norm):
    i = pl.program_id(1)
    tm = x_ref.shape[1]
    n_chunks = wd_ref.shape[0]
    mix = jnp.concatenate([jnp.concatenate([mfh_ref[0], msh_ref[0]], axis=-1),
                           jnp.concatenate([mf_ref[0], ms_ref[0]], axis=-1)], axis=0)
    x_ext = jnp.concatenate([xh_ref[0], x_ref[0]], axis=0)
    x1_ext = x_ext + mod_ref[0, 2:3, :] * _dot(mix, wo_ref[...])
    shift = mod_ref[0, 3:4, :]
    scale = mod_ref[0, 4:5, :]
    h_ext = _rms_rows(x1_ext) * gm_ref[...] * (1.0 + scale) + shift
    row = lax.broadcasted_iota(jnp.int32, h_ext.shape, 0)
    hx = jnp.where((row >= BF16_SUBLANES) | (i > 0), h_ext, 0.0).astype(BF16)
    x1_buf[...] = x1_ext[BF16_SUBLANES:, :]
    acc_ref[...] = jnp.zeros_like(acc_ref)

    def project_up(c, slot):
        for br in range(2):
            u_buf[slot, br] = _dot(hx, wu_ref[br, c])

    def mix_down(c, slot):
        branches = []
        for br in range(2):
            out = cb_ref[br, c]
            for tap in range(CONV_WIDTH):
                first = BF16_SUBLANES - (CONV_WIDTH - 1 - tap)
                out = out + cw_ref[br, c, tap:tap + 1, :] * u_buf[slot, br, pl.ds(first, tm), :]
            branches.append(out)
        u_gate, u_val = branches
        acc_ref[...] += _dot((u_gate * jax.nn.sigmoid(u_gate) * u_val).astype(BF16), wd_ref[c])

    def chunk_pair(j, _):
        c = 2 * j
        project_up(c + 1, 1)
        mix_down(c, 0)
        project_up(c + 2, 0)
        mix_down(c + 1, 1)
        return 0

    project_up(0, 0)
    lax.fori_loop(0, (n_chunks - 1) // 2, chunk_pair, 0)
    mix_down(n_chunks - 1, 0)
    x2 = x1_buf[...] + mod_ref[0, 5:6, :] * acc_ref[...]
    o_ref[0] = _rms_rows(x2) * gf_ref[...] if final_norm else x2


def _mixer(x, mix_f, mix_s, mod, w_out, g_mlp, w_up, conv_w, conv_b, w_down, g_final, final_norm):
    bsz, s, d = x.shape
    tm = OUT_ROWS
    n_chunks, tf = w_down.shape[0], w_down.shape[1]
    assert n_chunks % 2 == 1
    halo_blocks = tm // BF16_SUBLANES
    row = lambda b, i: (b, i, 0)
    halo = lambda b, i: (b, jnp.maximum(i * halo_blocks - 1, 0), 0)
    tile_and_halo = lambda a: [pl.BlockSpec((1, tm, a.shape[2]), row),
                               pl.BlockSpec((1, BF16_SUBLANES, a.shape[2]), halo)]
    resident = lambda a: pl.BlockSpec(a.shape, lambda b, i: (0,) * a.ndim, pipeline_mode=pl.Buffered(1))
    return pl.pallas_call(
        functools.partial(_mixer_kernel, final_norm=final_norm),
        grid=(bsz, s // tm),
        in_specs=tile_and_halo(x) + tile_and_halo(mix_f) + tile_and_halo(mix_s)
                 + [pl.BlockSpec((1, N_MOD, d), lambda b, i: (b, 0, 0)), resident(w_out),
                    pl.BlockSpec((1, d), lambda b, i: (0, 0)),
                    resident(w_up), resident(conv_w), resident(conv_b), resident(w_down),
                    pl.BlockSpec((1, d), lambda b, i: (0, 0))],
        out_specs=pl.BlockSpec((1, tm, d), row),
        out_shape=jax.ShapeDtypeStruct((bsz, s, d), F32),
        scratch_shapes=[pltpu.VMEM((2, 2, tm + BF16_SUBLANES, tf), F32), pltpu.VMEM((tm, d), F32),
                        pltpu.VMEM((tm, d), F32)],
        compiler_params=_params("arbitrary", "arbitrary"),
        name="mixer",
    )(x, x, mix_f, mix_f, mix_s, mix_s, mod, w_out, g_mlp, w_up, conv_w, conv_b, w_down, g_final)


def _chunk_columns(a, d_ff, n_chunks, tf):
    halves = jnp.stack([a[:, :d_ff], a[:, d_ff:]])
    halves = jnp.pad(halves, ((0, 0), (0, 0), (0, n_chunks * tf - d_ff)))
    return halves.reshape(2, a.shape[0], n_chunks, tf).transpose(0, 2, 1, 3)


def _pad_cols(a, n):
    return jnp.pad(a, ((0, 0), (0, n - a.shape[1])))


def kernel(x, c, w_ada, b_ada, g_attn, w_in, b_fgate, g_out_fox, g_out_sb, w_out,
           g_mlp, w_up, conv_w, conv_b, w_down, g_final):
    depth, d, _ = w_ada.shape
    n_fox = b_fgate.shape[1]
    d_fox = n_fox * HEAD_DIM
    d_sb = g_out_sb.shape[1]
    n_sb = d_sb // HEAD_DIM
    d_ff = w_down.shape[1]
    d_ff_pad = -(-d_ff // FF_CHUNK) * FF_CHUNK
    assert n_fox % 2 == 0 and n_sb == n_fox and 3 * n_fox <= LANES
    assert x.shape[1] % OUT_ROWS == 0 and x.shape[1] % FOX_Q == 0 and ATT_Q == 2 * ATT_K and ATT_COLS == ATT_K
    o_kf, o_vf, o_qs, o_ks, o_vs, o_gate = (d_fox, 2 * d_fox, 3 * d_fox, 3 * d_fox + d_sb,
                                             3 * d_fox + 2 * d_sb, 3 * d_fox + 3 * d_sb)

    for l in range(depth):
        mod = _ada(c, w_ada[l], b_ada[l]).reshape(-1, N_MOD, d)
        order = jnp.argsort(b_fgate[l])
        pick_head = (jnp.arange(n_fox)[:, None] == order[None, :]).astype(BF16)
        pick = jnp.kron(pick_head, jnp.eye(HEAD_DIM, dtype=BF16))
        w, wo = w_in[l].astype(BF16), w_out[l].astype(BF16)
        w_nat = jnp.concatenate([jnp.dot(w[:, o_kf:o_vf], pick), w[:, o_ks:o_vs]], axis=1)
        w_vt = jnp.concatenate([jnp.dot(pick.T, w[:, o_vf:o_qs].T), w[:, o_vs:o_gate].T,
                                jnp.dot(pick.T, w[:, :o_kf].T), w[:, o_qs:o_ks].T], axis=0)
        w_gate = _pad_cols(jnp.dot(w[:, o_gate:], pick_head), LANES)
        b_gate = _pad_cols(b_fgate[l][order].reshape(1, n_fox), LANES)
        g_fox = g_out_fox[l].reshape(n_fox, HEAD_DIM)[order].reshape(-1)
        w_mix = jnp.concatenate([jnp.dot(pick.T, wo[:d_fox]), wo[d_fox:]], axis=0)
        k_nat, vqt, k_aug, stats = _inproj(x, mod, g_attn[l].reshape(1, d), w_nat, w_vt, w_gate, b_gate, n_fox)

        wf, ws = FOX_PAIRS * LANES, SB_PAIRS * LANES
        fox_k_spec = pl.BlockSpec((1, 2 * FOX_PAIRS, x.shape[1], LANES), lambda b, p, i: (b, p, 0, 0))
        mix_f = _attention(_fox_kernel, "fox", FOX_PAIRS, FOX_Q, _fox_scratch(FOX_PAIRS), k_aug, fox_k_spec, vqt,
                           g_fox, q_block0=(d_fox + d_sb) // wf, vt_block0=0, n_heads=n_fox, stats=stats)
        sb_k_spec = pl.BlockSpec((1, x.shape[1], ws), lambda b, p, i: (b, 0, d_fox // ws + p))
        mix_s = _attention(_sb_kernel, "sb", SB_PAIRS, ATT_Q, _sb_scratch(SB_PAIRS), k_nat, sb_k_spec, vqt,
                           g_out_sb[l], q_block0=(2 * d_fox + d_sb) // ws, vt_block0=d_fox // ws, n_heads=n_sb)

        n_ff = d_ff_pad // FF_CHUNK
        x = _mixer(x, mix_f, mix_s, mod, w_mix, g_mlp[l].reshape(1, d),
                   _chunk_columns(w_up[l], d_ff, n_ff, FF_CHUNK).astype(BF16),
                   _chunk_columns(conv_w[l], d_ff, n_ff, FF_CHUNK),
                   _chunk_columns(conv_b[l].reshape(1, -1), d_ff, n_ff, FF_CHUNK),
                   jnp.pad(w_down[l], ((0, d_ff_pad - d_ff), (0, 0))).astype(BF16).reshape(n_ff, FF_CHUNK, d),
                   g_final.reshape(1, d), final_norm=(l == depth - 1))
    return x
```

```python
import functools

import numpy as np
import jax
import jax.numpy as jnp
from jax import lax
from jax.experimental import pallas as pl
from jax.experimental.pallas import tpu as pltpu

HEAD_DIM = 64
N_MOD = 6
CONV_WIDTH = 3
EPS = 1e-6

LANES = 128
F32_SUBLANES = 8
BF16_SUBLANES = 16

ATT_Q = 512
FOX_Q = 2 * ATT_Q
ATT_K = 256
ATT_COLS = 256
FOX_PAIRS = 1
SB_PAIRS = 2
SB_WIDE_TILES = 4
SB_MORE_TILES = 2
PROJ_ROWS = 4 * ATT_K
LOG2E = 1.4426950408889634
MASKED = -1e30
M_INIT = -1e29
EXP2_MAX = 126.0
PRUNE_LOG2 = 152.0
NORM_SLACK = 1.02
STATS_ROWS = 8
OUT_ROWS = 1024
FF_CHUNK = 256

F32 = jnp.float32
BF16 = jnp.bfloat16
NT_DIMS = (((1,), (1,)), ((), ()))


def _dot(a, b):
    return jnp.dot(a, b, preferred_element_type=F32)


def _dot_nt(a, b):
    return lax.dot_general(a, b, NT_DIMS, preferred_element_type=F32)


def _params(*sem, vmem_mib):
    return pltpu.CompilerParams(dimension_semantics=sem, vmem_limit_bytes=vmem_mib * 1024 * 1024)


def _rms_rows(x):
    return x * lax.rsqrt(jnp.mean(x * x, axis=-1, keepdims=True) + EPS)


def _softplus(z):
    return jnp.maximum(z, 0.0) + jnp.log(1.0 + jnp.exp(-jnp.abs(z)))


def _split3(x):
    hi = x.astype(BF16)
    r1 = x - hi.astype(F32)
    mid = r1.astype(BF16)
    lo = (r1 - mid.astype(F32)).astype(BF16)
    return hi, mid, lo


def _ada_kernel(c_ref, w_ref, b_ref, o_ref):
    c = c_ref[...]
    o_ref[...] = _dot(c * jax.nn.sigmoid(c), w_ref[...]) + b_ref[...]


def _ada(c, w, b):
    bsz, d = c.shape
    n = w.shape[1]
    return pl.pallas_call(
        _ada_kernel,
        grid=(n // d,),
        in_specs=[pl.BlockSpec((bsz, d), lambda j: (0, 0)),
                  pl.BlockSpec((d, d), lambda j: (0, j)),
                  pl.BlockSpec((1, d), lambda j: (0, j))],
        out_specs=pl.BlockSpec((bsz, d), lambda j: (0, j)),
        out_shape=jax.ShapeDtypeStruct((bsz, n), F32),
        compiler_params=_params("arbitrary", vmem_mib=24),
        name="ada",
    )(c, w, b.reshape(1, n))


def _decay_prefix(lf, carry, n_heads):
    tk = lf.shape[0]
    lane = lax.broadcasted_iota(jnp.int32, (tk, LANES), 1)
    lf = jnp.where(lane < n_heads, lf, 0.0)
    row = lax.broadcasted_iota(jnp.int32, (tk, tk), 0)
    col = lax.broadcasted_iota(jnp.int32, (tk, tk), 1)
    tri = (col <= row).astype(BF16)
    hi, mid, lo = _split3(lf)
    terms = (hi.astype(F32) + pltpu.roll(mid.astype(F32), LANES // 4, 1)
             + pltpu.roll(lo.astype(F32), LANES // 2, 1)).astype(BF16)
    part = _dot(tri, terms)
    total = part + pltpu.roll(part, LANES - LANES // 4, 1) + pltpu.roll(part, LANES // 2, 1)
    return jnp.where(lane < n_heads, carry + total, 0.0)


def _decay_keys(f_run, k_fox, sel_ref, kaug_ref, rows, n_heads):
    tk = f_run.shape[0]
    lane = lax.broadcasted_iota(jnp.int32, (tk, LANES), 1)
    ghi, gmid, glo = _split3(-LOG2E * f_run)
    packed = (ghi.astype(F32) + pltpu.roll(gmid.astype(F32), n_heads, 1)
              + pltpu.roll(glo.astype(F32), 2 * n_heads, 1)).astype(BF16)
    placed = _dot(packed, sel_ref[...]).astype(BF16)
    for h in range(n_heads):
        pair = slice((h // 2) * LANES, (h // 2 + 1) * LANES)
        own = (lane < HEAD_DIM) if h % 2 == 0 else (lane >= HEAD_DIM)
        kaug_ref[0, h, rows, :] = jnp.where(own, k_fox[:, pair], placed[:, pair])


def _decay_stats(f_run, k_fox, ind_ref, kpre, stats_ref, tile, n_heads):
    tk = f_run.shape[0]
    k32 = k_fox.astype(F32)
    sq = _dot((k32 * k32).astype(BF16), ind_ref[...])
    kpre = jnp.maximum(kpre, jnp.sqrt(jnp.max(sq, axis=0, keepdims=True) * NORM_SLACK))
    g_end = -LOG2E * f_run[tk - 1:tk, :]
    lane1 = lax.broadcasted_iota(jnp.int32, (1, LANES), 1)

    def spread(v, h):
        return jnp.broadcast_to(jnp.sum(jnp.where(lane1 == h, v, 0.0), axis=1, keepdims=True), (1, ATT_COLS))

    for p in range(n_heads // 2):
        srows = [spread(v, 2 * p + hh) for v in (kpre, g_end) for hh in range(2)]
        srows += [jnp.zeros((1, ATT_COLS), F32)] * (stats_ref.shape[3] - len(srows))
        stats_ref[0, p, tile] = jnp.concatenate(srows, axis=0)
    return kpre


def _inproj_kernel(x_ref, mod_ref, g_ref, wn_ref, wvt_ref, wg_ref, bg_ref, sel_ref, ind_ref,
                   k_ref, vt_ref, kaug_ref, stats_ref, carry_ref, kpre_ref, *, n_heads):
    @pl.when(pl.program_id(1) == 0)
    def _():
        carry_ref[...] = jnp.zeros_like(carry_ref)
        kpre_ref[...] = jnp.zeros_like(kpre_ref)

    shift = mod_ref[0, 0:1, :]
    scale = mod_ref[0, 1:2, :]
    h = (_rms_rows(x_ref[0]) * g_ref[...] * (1.0 + scale) + shift).astype(BF16)
    logit = _dot(h, wg_ref[...]) + bg_ref[...]
    log_f = -_softplus(-logit)
    k = _dot(h, wn_ref[...]).astype(BF16)
    k_ref[0] = k
    d_grp = n_heads * HEAD_DIM
    tk = vt_ref.shape[3]
    tiles = [slice(t * tk, (t + 1) * tk) for t in range(vt_ref.shape[1])]
    f_run, carry = [], carry_ref[...]
    for rows in tiles:
        f_run.append(_decay_prefix(log_f[rows, :], carry, n_heads))
        carry = f_run[-1][tk - 1:tk, :]
    carry_ref[...] = carry
    v_t = lambda rows: _dot_nt(wvt_ref[...], h[rows, :]).astype(BF16)
    vt_ref[0, 0] = v_t(tiles[0])
    kpre = kpre_ref[...]
    for t, rows in enumerate(tiles):
        _decay_keys(f_run[t], k[rows, :d_grp], sel_ref, kaug_ref, rows, n_heads)
    for t, rows in enumerate(tiles):
        kpre = _decay_stats(f_run[t], k[rows, :d_grp], ind_ref, kpre, stats_ref, t, n_heads)
    kpre_ref[...] = kpre
    for t in range(1, len(tiles)):
        vt_ref[0, t] = v_t(tiles[t])


def _head_indicator(n_heads):
    ind = np.zeros((n_heads * HEAD_DIM, LANES), np.float32)
    ind[np.arange(n_heads * HEAD_DIM), np.arange(n_heads * HEAD_DIM) // HEAD_DIM] = 1.0
    return jnp.asarray(ind, BF16)


def _decay_select_matrix(n_heads):
    sel = np.zeros((LANES, n_heads // 2 * LANES), np.float32)
    for h in range(n_heads):
        base = h // 2 * LANES + (HEAD_DIM if h % 2 == 0 else 0)
        for term in range(3):
            sel[term * n_heads + h, base + term] = 1.0
    return jnp.asarray(sel, BF16)


def _inproj(x, mod, g, w_nat, w_vt, w_gate, b_gate, n_heads):
    bsz, s, d = x.shape
    tm, tk = PROJ_ROWS, ATT_K
    n_nat, n_v = w_nat.shape[1], w_vt.shape[0]
    const = lambda b, i: (0, 0)
    resident = lambda a: pl.BlockSpec(a.shape, const, pipeline_mode=pl.Buffered(1))
    sel, ind = _decay_select_matrix(n_heads), _head_indicator(n_heads)
    return pl.pallas_call(
        functools.partial(_inproj_kernel, n_heads=n_heads),
        grid=(bsz, s // tm),
        in_specs=[pl.BlockSpec((1, tm, d), lambda b, i: (b, i, 0)),
                  pl.BlockSpec((1, N_MOD, d), lambda b, i: (b, 0, 0)),
                  pl.BlockSpec((1, d), const),
                  resident(w_nat), resident(w_vt), resident(w_gate),
                  pl.BlockSpec((1, LANES), const),
                  resident(sel), resident(ind)],
        out_specs=[pl.BlockSpec((1, tm, n_nat), lambda b, i: (b, i, 0)),
                   pl.BlockSpec((1, tm // tk, n_v, tk), lambda b, i: (b, i, 0, 0)),
                   pl.BlockSpec((1, n_heads, tm, LANES), lambda b, i: (b, 0, i, 0)),
                   pl.BlockSpec((1, n_heads // 2, tm // tk, STATS_ROWS, ATT_COLS), lambda b, i: (b, 0, i, 0, 0))],
        out_shape=[jax.ShapeDtypeStruct((bsz, s, n_nat), BF16),
                   jax.ShapeDtypeStruct((bsz, s // tk, n_v, tk), BF16),
                   jax.ShapeDtypeStruct((bsz, n_heads, s, LANES), BF16),
                   jax.ShapeDtypeStruct((bsz, n_heads // 2, s // tk, STATS_ROWS, ATT_COLS), F32)],
        scratch_shapes=[pltpu.VMEM((1, LANES), F32), pltpu.VMEM((1, LANES), F32)],
        compiler_params=_params("arbitrary", "arbitrary", vmem_mib=56),
        name="inproj",
    )(x, mod, g, w_nat, w_vt, w_gate, b_gate, sel, ind)


def _lane_queries(q_ref, extra_even, extra_odd):
    out = []
    channel = lax.broadcasted_iota(jnp.int32, (LANES, q_ref.shape[3]), 0)
    for pp in range(q_ref.shape[2] // LANES):
        for hh, extra in enumerate((extra_even, extra_odd)):
            own = (channel < HEAD_DIM) if hh == 0 else (channel >= HEAD_DIM)
            for c in range(q_ref.shape[1]):
                q = q_ref[0, c, pp * LANES:(pp + 1) * LANES, :].astype(F32) * (HEAD_DIM ** -0.5 * LOG2E)
                out.append(jnp.where(own, q, extra(channel)).astype(BF16))
    return out


def _visibility(first_key, first_query, bk, cw, strict):
    last_visible_gap = -1 if strict else 0
    if first_key + bk - 1 - first_query <= last_visible_gap:
        return "all"
    if first_key - (first_query + cw - 1) > last_visible_gap:
        return "none"
    gap = (lax.broadcasted_iota(jnp.int32, (bk, cw), 0) - lax.broadcasted_iota(jnp.int32, (bk, cw), 1)
           + (first_key - first_query))
    return gap <= last_visible_gap


def _diag_visibility(u, c, bk, cw, strict):
    return _visibility((1 - u) * bk, c * cw, bk, cw, strict)


def _hidden(visibility):
    return isinstance(visibility, str) and visibility == "none"


def _query_norm_bounds(queries, n_chunks):
    bounds = []
    for li, q in enumerate(queries):
        hh = (li // n_chunks) % 2
        own = q[hh * HEAD_DIM:(hh + 1) * HEAD_DIM, :].astype(F32)
        bounds.append(jnp.sqrt(jnp.sum(own * own, axis=0, keepdims=True) * NORM_SLACK))
    return bounds


def _finish_heads(lanes, g_ref, o_ref):
    n_pairs = o_ref.shape[2] // LANES
    n_chunks = len(lanes) // (2 * n_pairs)
    for pp in range(n_pairs):
        mine = lanes[2 * pp * n_chunks:2 * (pp + 1) * n_chunks]
        outs = [jnp.concatenate(mine[hh * n_chunks:(hh + 1) * n_chunks], axis=1) for hh in range(2)]
        normed = [o * lax.rsqrt(jnp.mean(o * o, axis=0, keepdims=True) + EPS) for o in outs]
        cols = slice(pp * LANES, (pp + 1) * LANES)
        o_ref[0, :, cols] = (jnp.concatenate(normed, axis=0).T * g_ref[:, cols]).astype(o_ref.dtype)


def _fox_kernel(q_ref, k_ref, vt_ref, g_ref, stats_ref, o_ref, s_buf, cmax_buf, p_buf, acc_buf):
    bk = vt_ref.shape[3]
    n_lanes, cw = acc_buf.shape[0], acc_buf.shape[2]
    n_chunks = q_ref.shape[1]
    chunks_per_sub = ATT_Q // cw
    assert n_chunks == 2 * chunks_per_sub and q_ref.shape[3] == cw
    lane_group = lambda li: (li // (2 * n_chunks), (li // n_chunks) % 2, li % n_chunks)
    first_q_block = 2 * pl.program_id(2)
    n_tiles = 2 * (first_q_block + 1)
    ones3 = lambda lo: (lambda lane: jnp.where((lane >= lo) & (lane < lo + 3), 1.0, 0.0))
    queries = _lane_queries(q_ref, ones3(HEAD_DIM), ones3(0))
    acc_buf[...] = jnp.zeros(acc_buf.shape, F32)
    for li in range(n_lanes):
        if _hidden(_diag_visibility(0, lane_group(li)[2] % chunks_per_sub, bk, cw, strict=False)):
            p_buf[0, li] = jnp.zeros((bk, cw), BF16)

    full, idle = ("below", "below", True), (None, None, False)
    ones_rows = jnp.ones((BF16_SUBLANES, bk), BF16)

    def step(t, slot, carry, stages=(full, full)):
        new = []
        for li in range(n_lanes):
            pp, hh, chunk = lane_group(li)
            sub, c = divmod(chunk, chunks_per_sub)
            score, softmax, value = stages[sub]
            see = lambda u: "all" if u == "below" else _diag_visibility(u, c, bk, cw, strict=False)
            if score is not None and not _hidden(see(score)):
                start = pl.multiple_of((n_tiles - 2 - t) * bk, bk)
                s_new = _dot(k_ref[0, 2 * pp + hh, pl.ds(start, bk), :], queries[li])
                if not isinstance(see(score), str):
                    s_new = jnp.where(see(score), s_new, MASKED)
                s_buf[1 - slot, li] = s_new
                cmax_buf[1 - slot, li] = jnp.max(s_new, axis=0, keepdims=True)
            pv = None
            if value:
                vt = vt_ref[0, n_tiles - t, pl.ds(pp * LANES + hh * HEAD_DIM, HEAD_DIM), :]
                pv = _dot(jnp.concatenate([vt, ones_rows], axis=0), p_buf[1 - slot, li])[:acc_buf.shape[1]]
            m = carry[li]
            if softmax is not None and not _hidden(see(softmax)):
                m_new = jnp.maximum(m, cmax_buf[slot, li])
                alpha = jnp.exp2(m - m_new)
                p_buf[slot, li] = jnp.exp2(s_buf[slot, li] - m_new).astype(BF16)
                m = m_new
                acc_buf[li] = alpha * (acc_buf[li] if pv is None else acc_buf[li] + pv)
            elif pv is not None:
                acc_buf[li] += pv
            new.append(m)
        return tuple(new)

    def step_pair(i, carry):
        t = 2 * i + 1
        return step(t + 1, 0, step(t, 1, carry))

    q_norm = _query_norm_bounds(queries, n_chunks)

    def later_tiles_matter(i, carry):
        j_rest = jnp.maximum(n_tiles - 5 - 2 * i, 0)
        worst = None
        for li in range(n_lanes):
            pp, hh, _ = lane_group(li)
            bound = (q_norm[li] * stats_ref[0, pp, j_rest, hh:hh + 1, :]
                     + stats_ref[0, pp, j_rest, 2 + hh:3 + hh, :] - carry[li])
            worst = bound if worst is None else jnp.maximum(worst, bound)
        return jnp.max(worst) >= -PRUNE_LOG2

    def pair_and_check(state):
        i, _, carry = state
        carry = step_pair(i, carry)
        return i + 1, later_tiles_matter(i, carry), carry

    carry = tuple(jnp.full((1, cw), M_INIT, F32) for _ in range(n_lanes))
    carry = step(-3, 1, carry, (idle, (0, None, False)))
    carry = step(-2, 0, carry, (idle, (1, 0, False)))
    carry = step(-1, 1, carry, ((0, None, False), full))
    carry = step(0, 0, carry, ((1, 0, False), full))
    n_pairs, _, carry = lax.while_loop(lambda st: (st[0] < first_q_block) & st[1], pair_and_check,
                                       (jnp.int32(0), jnp.bool_(True), carry))
    drain = (None, "below", True)
    carry = step(2 * n_pairs + 1, 1, carry, (drain, drain))
    drain = (None, None, True)
    carry = step(2 * n_pairs + 2, 0, carry, (drain, drain))
    _finish_heads([acc_buf[li, :HEAD_DIM, :] / acc_buf[li, HEAD_DIM:HEAD_DIM + 1, :] for li in range(n_lanes)], g_ref, o_ref)


def _sb_kernel(q_ref, k_ref, vt_ref, g_ref, o_ref, z_buf, sp_buf, e_buf, wrow_buf, later_buf, acc_buf):
    qi = pl.program_id(2)
    bk = vt_ref.shape[3]
    n_wide, n_first = z_buf.shape[0], SB_WIDE_TILES
    n_lanes, cw = acc_buf.shape[0], acc_buf.shape[2]
    n_chunks = q_ref.shape[1]
    lane_group = lambda li: (li // (2 * n_chunks), (li // n_chunks) % 2, li % n_chunks)
    n_tiles = 2 * (qi + 1)
    zero = lambda channel: 0.0
    queries = _lane_queries(q_ref, zero, zero)
    suffix = (lax.broadcasted_iota(jnp.int32, (bk, bk), 1)
              >= lax.broadcasted_iota(jnp.int32, (bk, bk), 0)).astype(BF16)
    softplus2 = lambda z: jnp.maximum(z, jnp.log2(1.0 + jnp.exp2(jnp.minimum(z, EXP2_MAX))))
    keys = lambda j, pp: k_ref[0, pl.ds(pl.multiple_of(j * bk, bk), bk), pp * LANES:(pp + 1) * LANES]
    values = lambda j, pp, hh: vt_ref[0, j, pl.ds(pp * LANES + hh * HEAD_DIM, HEAD_DIM), :]

    sees = lambda u, li: _diag_visibility(u, lane_group(li)[2], bk, cw, strict=True) if u < 2 else "all"

    def wide_block(tiles, later, check_exists):
        live = [(u, li) for u in tiles for li in range(n_lanes) if not _hidden(sees(u, li))]
        later, col_sums = list(later), {}

        def score(u, li):
            z = _dot(keys(jnp.maximum(n_tiles - 1 - u, 0), lane_group(li)[0]), queries[li])
            if not isinstance(sees(u, li), str):
                z = jnp.where(sees(u, li), z, MASKED)
            if check_exists and u >= 2:
                z = jnp.where(u < n_tiles, z, MASKED)
            z_buf[u, li] = z

        def softplus(u, li):
            sp_buf[u, li] = softplus2(z_buf[u, li]).astype(BF16)

        def cumsum(u, li):
            within = _dot(suffix, sp_buf[u, li])
            col_sums[u, li] = within[0:1, :]
            z_buf[u, li] = z_buf[u, li] - within

        def weight(u, li):
            pp, hh, _ = lane_group(li)
            a = jnp.exp2(z_buf[u, li] - later[li])
            acc_buf[li] += _dot(values(jnp.maximum(n_tiles - 1 - u, 0), pp, hh), a.astype(BF16))
            later[li] = later[li] + col_sums[u, li]

        stages = (score, softplus, cumsum, weight)
        for pos in range(len(live) + len(stages) - 1):
            for lag, stage in enumerate(stages):
                if 0 <= pos - lag < len(live):
                    stage(*live[pos - lag])
        return later

    n_rest = n_tiles - n_wide

    def rest_matters(mass):
        least = mass[0]
        for li in range(1, n_lanes):
            least = jnp.minimum(least, mass[li])
        return jnp.min(least) <= PRUNE_LOG2

    def step(t, slot, later, score=True, softplus=True, cumsum=True, weight=True):
        new_later = []
        for li in range(n_lanes):
            pp, hh, _ = lane_group(li)
            if cumsum:
                within = _dot(suffix, sp_buf[1 - slot, li])
                e_buf[1 - slot, li] = z_buf[1 - slot, li] - within
                wrow_buf[1 - slot, li] = within[0:1, :]
            if score:
                z_buf[1 - slot, li] = _dot(keys(n_rest - 1 - (t + 3), pp), queries[li])
            if weight:
                a = jnp.exp2(e_buf[slot, li] - later[li])
                acc_buf[li] += _dot(values(n_rest - 1 - jnp.maximum(t, 0), pp, hh), a.astype(BF16))
                new_later.append(later[li] + wrow_buf[slot, li])
            else:
                new_later.append(later[li])
            if softplus:
                sp_buf[slot, li] = softplus2(z_buf[slot, li]).astype(BF16)
        return tuple(new_later)

    def pair_and_check(state):
        i, _, later = state
        t = 2 * i - 1
        later = step(t + 1, 0, step(t, 1, later))
        mass = [later[li] + wrow_buf[1, li] for li in range(n_lanes)]
        return i + 1, rest_matters(mass), later

    acc_buf[...] = jnp.zeros(acc_buf.shape, F32)
    first = wide_block(range(n_first), [jnp.zeros((1, cw), F32)] * n_lanes, check_exists=True)
    for li in range(n_lanes):
        later_buf[li] = first[li]
    swept = lambda: [later_buf[li] for li in range(n_lanes)]
    more_matters = (n_rest + (n_wide - n_first) > 0) & rest_matters(first)

    @pl.when(more_matters)
    def _():
        more = wide_block(range(n_first, n_wide), first, check_exists=False)
        for li in range(n_lanes):
            later_buf[li] = more[li]

    @pl.when(more_matters & (n_rest > 0) & rest_matters(swept()))
    def _():
        later = swept()
        e_buf[1] = jnp.full(e_buf.shape[1:], MASKED, F32)
        wrow_buf[1] = jnp.zeros(wrow_buf.shape[1:], F32)
        mass = step(-3, 1, tuple(later), softplus=False, cumsum=False, weight=False)
        mass = step(-2, 0, mass, cumsum=False, weight=False)
        n_pairs, _, mass = lax.while_loop(lambda st: (2 * st[0] + 2 < n_rest) & st[1], pair_and_check,
                                          (jnp.int32(0), jnp.bool_(True), mass))
        mass = step(2 * n_pairs - 1, 1, mass, score=False)
        mass = step(2 * n_pairs, 0, mass, score=False, softplus=False)
        step(2 * n_pairs + 1, 1, mass, score=False, softplus=False, cumsum=False)

    _finish_heads([acc_buf[li] for li in range(n_lanes)], g_ref, o_ref)


def _attention(body, name, pairs, bq, scratch, k_arr, k_spec, vqt, g, q_block0, vt_block0, n_heads, vmem_mib, stats=None):
    bsz, s = vqt.shape[0], vqt.shape[1] * vqt.shape[3]
    bk, width = ATT_K, pairs * LANES
    d_grp = n_heads * HEAD_DIM
    assert (n_heads // 2) % pairs == 0
    in_specs = [pl.BlockSpec((1, bq // ATT_COLS, width, ATT_COLS), lambda b, p, i: (b, i, q_block0 + p, 0)),
                k_spec,
                pl.BlockSpec((1, s // bk, width, bk), lambda b, p, i: (b, 0, vt_block0 + p, 0)),
                pl.BlockSpec((1, width), lambda b, p, i: (0, p))]
    operands = [vqt, k_arr, vqt, g.reshape(1, d_grp)]
    if stats is not None:
        in_specs.append(pl.BlockSpec((1, pairs) + stats.shape[2:], lambda b, p, i: (b, p, 0, 0, 0)))
        operands.append(stats)
    return pl.pallas_call(
        body,
        grid=(bsz, n_heads // 2 // pairs, s // bq),
        in_specs=in_specs,
        out_specs=pl.BlockSpec((1, bq, width), lambda b, p, i: (b, i, p)),
        out_shape=jax.ShapeDtypeStruct((bsz, s, d_grp), BF16),
        scratch_shapes=scratch,
        compiler_params=_params("arbitrary", "arbitrary", "arbitrary", vmem_mib=vmem_mib),
        name=name,
    )(*operands)


def _lane_groups(pairs, bq):
    return pairs * 2 * (bq // ATT_COLS)


def _fox_scratch(pairs):
    n = _lane_groups(pairs, FOX_Q)
    return [pltpu.VMEM((2, n, ATT_K, ATT_COLS), F32), pltpu.VMEM((2, n, 1, ATT_COLS), F32),
            pltpu.VMEM((2, n, ATT_K, ATT_COLS), BF16), pltpu.VMEM((n, HEAD_DIM + F32_SUBLANES, ATT_COLS), F32)]


def _sb_scratch(pairs):
    n = _lane_groups(pairs, ATT_Q)
    wide = SB_WIDE_TILES + SB_MORE_TILES
    return [pltpu.VMEM((wide, n, ATT_K, ATT_COLS), F32), pltpu.VMEM((wide, n, ATT_K, ATT_COLS), BF16),
            pltpu.VMEM((2, n, ATT_K, ATT_COLS), F32), pltpu.VMEM((2, n, 1, ATT_COLS), F32),
            pltpu.VMEM((n, 1, ATT_COLS), F32), pltpu.VMEM((n, HEAD_DIM, ATT_COLS), F32)]


def _mixer_kernel(x_ref, xh_ref, mf_ref, mfh_ref, ms_ref, msh_ref, mod_ref, wo_ref, gm_ref, wu_ref, cw_ref,
                  cb_ref, wd_ref, gf_ref, o_ref, u_buf, acc_ref, x1_buf, *, final_norm):
    i = pl.program_id(1)
    tm = x_ref.shape[1]
    n_chunks = wd_ref.shape[0]
    mix = jnp.concatenate([jnp.concatenate([mfh_ref[0], msh_ref[0]], axis=-1),
                           jnp.concatenate([mf_ref[0], ms_ref[0]], axis=-1)], axis=0)
    x_ext = jnp.concatenate([xh_ref[0], x_ref[0]], axis=0)
    x1_ext = x_ext + mod_ref[0, 2:3, :] * _dot(mix, wo_ref[...])
    shift = mod_ref[0, 3:4, :]
    scale = mod_ref[0, 4:5, :]
    h_ext = _rms_rows(x1_ext) * gm_ref[...] * (1.0 + scale) + shift
    row = lax.broadcasted_iota(jnp.int32, h_ext.shape, 0)
    hx = jnp.where((row >= BF16_SUBLANES) | (i > 0), h_ext, 0.0).astype(BF16)
    x1_buf[...] = x1_ext[BF16_SUBLANES:, :]
    acc_ref[...] = jnp.zeros_like(acc_ref)

    def project_up(c, slot):
        for br in range(2):
            u_buf[slot, br] = _dot(hx, wu_ref[br, c])

    def mix_down(c, slot):
        branches = []
        for br in range(2):
            out = cb_ref[br, c]
            for tap in range(CONV_WIDTH):
                first = BF16_SUBLANES - (CONV_WIDTH - 1 - tap)
                out = out + cw_ref[br, c, tap:tap + 1, :] * u_buf[slot, br, pl.ds(first, tm), :]
            branches.append(out)
        u_gate, u_val = branches
        acc_ref[...] += _dot((u_gate * jax.nn.sigmoid(u_gate) * u_val).astype(BF16), wd_ref[c])

    def chunk_pair(j, _):
        c = 2 * j
        project_up(c + 1, 1)
        mix_down(c, 0)
        project_up(c + 2, 0)
        mix_down(c + 1, 1)
        return 0

    project_up(0, 0)
    lax.fori_loop(0, (n_chunks - 1) // 2, chunk_pair, 0)
    mix_down(n_chunks - 1, 0)
    x2 = x1_buf[...] + mod_ref[0, 5:6, :] * acc_ref[...]
    o_ref[0] = _rms_rows(x2) * gf_ref[...] if final_norm else x2


def _mixer(x, mix_f, mix_s, mod, w_out, g_mlp, w_up, conv_w, conv_b, w_down, g_final, final_norm):
    bsz, s, d = x.shape
    tm = OUT_ROWS
    n_chunks, tf = w_down.shape[0], w_down.shape[1]
    assert n_chunks % 2 == 1
    halo_blocks = tm // BF16_SUBLANES
    row = lambda b, i: (b, i, 0)
    halo = lambda b, i: (b, jnp.maximum(i * halo_blocks - 1, 0), 0)
    tile_and_halo = lambda a: [pl.BlockSpec((1, tm, a.shape[2]), row),
                               pl.BlockSpec((1, BF16_SUBLANES, a.shape[2]), halo)]
    resident = lambda a: pl.BlockSpec(a.shape, lambda b, i: (0,) * a.ndim, pipeline_mode=pl.Buffered(1))
    return pl.pallas_call(
        functools.partial(_mixer_kernel, final_norm=final_norm),
        grid=(bsz, s // tm),
        in_specs=tile_and_halo(x) + tile_and_halo(mix_f) + tile_and_halo(mix_s)
                 + [pl.BlockSpec((1, N_MOD, d), lambda b, i: (b, 0, 0)), resident(w_out),
                    pl.BlockSpec((1, d), lambda b, i: (0, 0)),
                    resident(w_up), resident(conv_w), resident(conv_b), resident(w_down),
                    pl.BlockSpec((1, d), lambda b, i: (0, 0))],
        out_specs=pl.BlockSpec((1, tm, d), row),
        out_shape=jax.ShapeDtypeStruct((bsz, s, d), F32),
        scratch_shapes=[pltpu.VMEM((2, 2, tm + BF16_SUBLANES, tf), F32), pltpu.VMEM((tm, d), F32),
                        pltpu.VMEM((tm, d), F32)],
        compiler_params=_params("arbitrary", "arbitrary", vmem_mib=56),
        name="mixer",
    )(x, x, mix_f, mix_f, mix_s, mix_s, mod, w_out, g_mlp, w_up, conv_w, conv_b, w_down, g_final)


def _chunk_columns(a, d_ff, n_chunks, tf):
    halves = jnp.stack([a[:, :d_ff], a[:, d_ff:]])
    halves = jnp.pad(halves, ((0, 0), (0, 0), (0, n_chunks * tf - d_ff)))
    return halves.reshape(2, a.shape[0], n_chunks, tf).transpose(0, 2, 1, 3)


def _pad_cols(a, n):
    return jnp.pad(a, ((0, 0), (0, n - a.shape[1])))


def kernel(x, c, w_ada, b_ada, g_attn, w_in, b_fgate, g_out_fox, g_out_sb, w_out,
           g_mlp, w_up, conv_w, conv_b, w_down, g_final):
    depth, d, _ = w_ada.shape
    n_fox = b_fgate.shape[1]
    d_fox = n_fox * HEAD_DIM
    d_sb = g_out_sb.shape[1]
    n_sb = d_sb // HEAD_DIM
    d_ff = w_down.shape[1]
    d_ff_pad = -(-d_ff // FF_CHUNK) * FF_CHUNK
    assert n_fox % 2 == 0 and n_sb == n_fox and 3 * n_fox <= LANES
    assert x.shape[1] % OUT_ROWS == 0 and x.shape[1] % FOX_Q == 0 and ATT_Q == 2 * ATT_K and ATT_COLS == ATT_K
    o_kf, o_vf, o_qs, o_ks, o_vs, o_gate = (d_fox, 2 * d_fox, 3 * d_fox, 3 * d_fox + d_sb,
                                             3 * d_fox + 2 * d_sb, 3 * d_fox + 3 * d_sb)

    for l in range(depth):
        mod = _ada(c, w_ada[l], b_ada[l]).reshape(-1, N_MOD, d)
        order = jnp.argsort(b_fgate[l])
        pick_head = (jnp.arange(n_fox)[:, None] == order[None, :]).astype(BF16)
        pick = jnp.kron(pick_head, jnp.eye(HEAD_DIM, dtype=BF16))
        w, wo = w_in[l].astype(BF16), w_out[l].astype(BF16)
        w_nat = jnp.concatenate([jnp.dot(w[:, o_kf:o_vf], pick), w[:, o_ks:o_vs]], axis=1)
        w_vt = jnp.concatenate([jnp.dot(pick.T, w[:, o_vf:o_qs].T), w[:, o_vs:o_gate].T,
                                jnp.dot(pick.T, w[:, :o_kf].T), w[:, o_qs:o_ks].T], axis=0)
        w_gate = _pad_cols(jnp.dot(w[:, o_gate:], pick_head), LANES)
        b_gate = _pad_cols(b_fgate[l][order].reshape(1, n_fox), LANES)
        g_fox = g_out_fox[l].reshape(n_fox, HEAD_DIM)[order].reshape(-1)
        w_mix = jnp.concatenate([jnp.dot(pick.T, wo[:d_fox]), wo[d_fox:]], axis=0)
        k_nat, vqt, k_aug, stats = _inproj(x, mod, g_attn[l].reshape(1, d), w_nat, w_vt, w_gate, b_gate, n_fox)

        wf, ws = FOX_PAIRS * LANES, SB_PAIRS * LANES
        fox_k_spec = pl.BlockSpec((1, 2 * FOX_PAIRS, x.shape[1], LANES), lambda b, p, i: (b, p, 0, 0))
        mix_f = _attention(_fox_kernel, "fox", FOX_PAIRS, FOX_Q, _fox_scratch(FOX_PAIRS), k_aug, fox_k_spec, vqt,
                           g_fox, q_block0=(d_fox + d_sb) // wf, vt_block0=0, n_heads=n_fox, vmem_mib=40, stats=stats)
        sb_k_spec = pl.BlockSpec((1, x.shape[1], ws), lambda b, p, i: (b, 0, d_fox // ws + p))
        mix_s = _attention(_sb_kernel, "sb", SB_PAIRS, ATT_Q, _sb_scratch(SB_PAIRS), k_nat, sb_k_spec, vqt,
                           g_out_sb[l], q_block0=(2 * d_fox + d_sb) // ws, vt_block0=d_fox // ws, n_heads=n_sb, vmem_mib=48)

        n_ff = d_ff_pad // FF_CHUNK
        x = _mixer(x, mix_f, mix_s, mod, w_mix, g_mlp[l].reshape(1, d),
                   _chunk_columns(w_up[l], d_ff, n_ff, FF_CHUNK).astype(BF16),
                   _chunk_columns(conv_w[l], d_ff, n_ff, FF_CHUNK),
                   _chunk_columns(conv_b[l].reshape(1, -1), d_ff, n_ff, FF_CHUNK),
                   jnp.pad(w_down[l], ((0, d_ff_pad - d_ff), (0, 0))).astype(BF16).reshape(n_ff, FF_CHUNK, d),
                   g_final.reshape(1, d), final_norm=(l == depth - 1))
    return x
```
